```python
import math
import jax, jax.numpy as jnp
from jax import lax
import numpy as np


D_MODEL = 1024
BATCH = 2
SEQ = 8192
DEPTH = 1

NSA_HEADS = 8
NSA_KV_GROUPS = 2
NSA_HPG = NSA_HEADS // NSA_KV_GROUPS
NSA_HEAD_DIM = 64
CMP_BLOCK = 32
CMP_STRIDE = 16
CMP_HIDDEN = 256
SLC_BLOCK = 64
SLC_TOPK = 16
WINDOW = 512
Q_BLOCK = 128
GDN_HEADS = 4
GDN_HEAD_DIM = 128
GDN_CONV = 4
GDN_CHUNK = 64
N_EXPERTS = 32
TOP_K = 4
D_EXPERT = 1024
SWIGLU_LIMIT = 7.0
SWIGLU_ALPHA = 1.702
MOE_ROW_BLOCK = 256

EPS = 1e-6
NEG = -1e30
NSA_WIDTH = NSA_HEADS * NSA_HEAD_DIM
NSA_KV_WIDTH = NSA_KV_GROUPS * NSA_HEAD_DIM
GDN_WIDTH = GDN_HEADS * GDN_HEAD_DIM
MIX_WIDTH = NSA_WIDTH + GDN_WIDTH
IN_SPLITS = (NSA_WIDTH, NSA_KV_WIDTH, NSA_KV_WIDTH, NSA_KV_WIDTH, NSA_KV_WIDTH, NSA_KV_WIDTH,
             NSA_KV_WIDTH, 3 * NSA_HEADS, 3 * GDN_WIDTH, GDN_WIDTH, GDN_HEADS, GDN_HEADS)
IN_WIDTH = sum(IN_SPLITS)

kernel_name = 'hybrid_nsa_gdn_moe_block'


def rms_norm(x, g):
    xf = x.astype(jnp.float32)
    y = xf * lax.rsqrt(jnp.mean(xf * xf, axis=-1, keepdims=True) + EPS)
    return (y * g.astype(jnp.float32)).astype(x.dtype)


def l2_norm(x):
    return x * lax.rsqrt(jnp.sum(x * x, axis=-1, keepdims=True) + EPS)


def masked_probs(s, mask):
    s = jnp.where(mask, s.astype(jnp.float32), NEG)
    m = jnp.max(s, axis=-1, keepdims=True)
    p = jnp.where(mask, jnp.exp(s - m), 0.0)
    return p / jnp.maximum(jnp.sum(p, axis=-1, keepdims=True), 1e-30)


def nsa_compress(u, pos, w1, b1, w2, b2):
    B, T = u.shape[:2]
    n_cmp = (T - CMP_BLOCK) // CMP_STRIDE + 1
    idx = jnp.arange(n_cmp)[:, None] * CMP_STRIDE + jnp.arange(CMP_BLOCK)[None, :]
    kv = u.reshape(B, T, NSA_KV_GROUPS, NSA_HEAD_DIM)
    blocks = kv[:, idx] + pos[:, None, :]
    blocks = blocks.transpose(0, 3, 1, 2, 4).reshape(B, NSA_KV_GROUPS, n_cmp, CMP_BLOCK * NSA_HEAD_DIM)
    hid = jax.nn.gelu(blocks @ w1 + b1)
    return hid @ w2 + b2


def nsa_mixer(u_q, u_kc, u_vc, u_ks, u_vs, u_kw, u_vw, u_gate, q_norm_g, kc_norm_g, ks_norm_g, kw_norm_g,
              ck_pos, ck_w1, ck_b1, ck_w2, ck_b2, cv_pos, cv_w1, cv_b1, cv_w2, cv_b2):
    B, T = u_q.shape[:2]
    G, HPG, DH = NSA_KV_GROUPS, NSA_HPG, NSA_HEAD_DIM
    n_slc = T // SLC_BLOCK
    n_top = min(SLC_TOPK, n_slc)
    q = rms_norm(u_q.reshape(B, T, G, HPG, DH), q_norm_g) * (DH ** -0.5)
    q = q.transpose(0, 2, 3, 1, 4)
    gate = jax.nn.sigmoid(u_gate.reshape(B, T, G, HPG, 3)).transpose(0, 2, 3, 1, 4)
    kc = rms_norm(nsa_compress(u_kc, ck_pos, ck_w1, ck_b1, ck_w2, ck_b2), kc_norm_g)
    vc = nsa_compress(u_vc, cv_pos, cv_w1, cv_b1, cv_w2, cv_b2)
    n_cmp = kc.shape[2]
    ks = rms_norm(u_ks.reshape(B, T, G, DH), ks_norm_g).transpose(0, 2, 1, 3).reshape(B, G, n_slc, SLC_BLOCK, DH)
    vs = u_vs.reshape(B, T, G, DH).transpose(0, 2, 1, 3).reshape(B, G, n_slc, SLC_BLOCK, DH)
    pad = ((0, 0), (0, 0), (WINDOW, 0), (0, 0))
    kw = jnp.pad(rms_norm(u_kw.reshape(B, T, G, DH), kw_norm_g).transpose(0, 2, 1, 3), pad)
    vw = jnp.pad(u_vw.reshape(B, T, G, DH).transpose(0, 2, 1, 3), pad)

    cmp_start = jnp.arange(n_cmp) * CMP_STRIDE
    cmp_end = cmp_start + CMP_BLOCK - 1
    slc_start = jnp.arange(n_slc) * SLC_BLOCK
    overlap = ((cmp_start[:, None] < slc_start[None, :] + SLC_BLOCK) &
               (cmp_start[:, None] + CMP_BLOCK > slc_start[None, :])).astype(jnp.float32)
    blk = jnp.arange(n_slc)
    bi = jnp.arange(B)[:, None, None, None]
    gi = jnp.arange(G)[None, :, None, None]

    def q_block(i):
        s0 = i * Q_BLOCK
        qb = lax.dynamic_slice_in_dim(q, s0, Q_BLOCK, axis=3)
        gb = lax.dynamic_slice_in_dim(gate, s0, Q_BLOCK, axis=3)
        t = s0 + jnp.arange(Q_BLOCK)
        sc = jnp.einsum('bghqd,bgcd->bghqc', qb, kc)
        pc = masked_probs(sc, cmp_end[None, :] <= t[:, None])
        oc = jnp.einsum('bghqc,bgcd->bghqd', pc.astype(vc.dtype), vc)
        imp = jnp.einsum('bghqc,cn->bgqn', pc, overlap)
        cur = t // SLC_BLOCK
        imp = jnp.where(slc_start[None, :] > t[:, None], NEG, imp)
        forced = (blk[None, :] == 0) | (blk[None, :] == cur[:, None]) | (blk[None, :] == cur[:, None] - 1)
        imp = jnp.where(forced, -NEG, imp)
        _, sel = lax.top_k(imp, n_top)
        ksel = ks[bi, gi, sel]
        vsel = vs[bi, gi, sel].reshape(B, G, Q_BLOCK, n_top * SLC_BLOCK, DH)
        pos = sel[..., None] * SLC_BLOCK + jnp.arange(SLC_BLOCK)
        smask = (pos <= t[None, None, :, None, None]).reshape(B, G, 1, Q_BLOCK, n_top * SLC_BLOCK)
        ss = jnp.einsum('bghqd,bgqnsd->bghqns', qb, ksel).reshape(B, G, HPG, Q_BLOCK, n_top * SLC_BLOCK)
        ps = masked_probs(ss, smask)
        osl = jnp.einsum('bghqk,bgqkd->bghqd', ps.astype(vsel.dtype), vsel)
        kwb = lax.dynamic_slice_in_dim(kw, s0, WINDOW + Q_BLOCK, axis=2)
        vwb = lax.dynamic_slice_in_dim(vw, s0, WINDOW + Q_BLOCK, axis=2)
        kpos = s0 - WINDOW + jnp.arange(WINDOW + Q_BLOCK)
        wmask = (kpos[None, :] <= t[:, None]) & (kpos[None, :] > t[:, None] - WINDOW) & (kpos[None, :] >= 0)
        sw = jnp.einsum('bghqd,bgkd->bghqk', qb, kwb)
        pw = masked_probs(sw, wmask)
        ow = jnp.einsum('bghqk,bgkd->bghqd', pw.astype(vwb.dtype), vwb)
        return gb[..., 0:1] * oc + gb[..., 1:2] * osl + gb[..., 2:3] * ow

    outs = lax.map(q_block, jnp.arange(T // Q_BLOCK))
    return outs.transpose(1, 0, 4, 2, 3, 5).reshape(B, T, NSA_WIDTH)


def causal_dwconv(x, w):
    k = w.shape[0]
    return lax.conv_general_dilated(x, w[:, None, :], window_strides=(1,), padding=[(k - 1, 0)],
                                    dimension_numbers=('NWC', 'WIO', 'NWC'), feature_group_count=x.shape[-1])


def chunk_gated_delta(q, k, v, g, beta):
    B, H, T, DK = q.shape
    DV = v.shape[-1]
    C = GDN_CHUNK
    N = T // C
    q = q.reshape(B, H, N, C, DK)
    k = k.reshape(B, H, N, C, DK)
    v = v.reshape(B, H, N, C, DV)
    g = jnp.cumsum(g.reshape(B, H, N, C), axis=-1)
    beta = beta.reshape(B, H, N, C)
    kb = k * beta[..., None]
    vb = v * beta[..., None]
    incl = jnp.tril(jnp.ones((C, C), bool))
    strict = jnp.tril(jnp.ones((C, C), bool), -1)
    decay = jnp.exp(jnp.where(incl, g[..., :, None] - g[..., None, :], -jnp.inf))
    lmat = jnp.where(strict, jnp.einsum('bhncd,bhnsd->bhncs', kb, k) * decay, 0.0)
    eye = jnp.broadcast_to(jnp.eye(C, dtype=q.dtype), lmat.shape)
    tinv = lax.linalg.triangular_solve(eye + lmat, eye, left_side=True, lower=True, unit_diagonal=True)
    u = jnp.einsum('bhncs,bhnse->bhnce', tinv, vb)
    w = jnp.einsum('bhncs,bhnsd->bhncd', tinv, kb * jnp.exp(g)[..., None])
    qk = jnp.where(incl, jnp.einsum('bhncd,bhnsd->bhncs', q, k) * decay, 0.0)
    qg = q * jnp.exp(g)[..., None]
    kd = k * jnp.exp(g[..., -1:] - g)[..., None]
    gl = jnp.exp(g[..., -1])

    def step(S, xs):
        qg_i, kd_i, u_i, w_i, qk_i, gl_i = xs
        v_new = u_i - jnp.einsum('bhcd,bhde->bhce', w_i, S)
        o_i = jnp.einsum('bhcd,bhde->bhce', qg_i, S) + jnp.einsum('bhcs,bhse->bhce', qk_i, v_new)
        S = S * gl_i[..., None, None] + jnp.einsum('bhcd,bhce->bhde', kd_i, v_new)
        return S, o_i

    xs = (jnp.moveaxis(qg, 2, 0), jnp.moveaxis(kd, 2, 0), jnp.moveaxis(u, 2, 0),
          jnp.moveaxis(w, 2, 0), jnp.moveaxis(qk, 2, 0), jnp.moveaxis(gl, 2, 0))
    s0 = jnp.zeros((B, H, DK, DV), jnp.float32)
    _, o = lax.scan(step, s0, xs)
    return jnp.moveaxis(o, 0, 2).reshape(B, H, T, DV)


def gdn_mixer(u_qkv, u_z, u_a, u_b, conv_w, a_log, dt_bias, out_norm_g):
    B, T = u_qkv.shape[:2]
    H, D = GDN_HEADS, GDN_HEAD_DIM
    qkv = jax.nn.silu(causal_dwconv(u_qkv, conv_w)).astype(jnp.float32)
    q, k, v = jnp.split(qkv, 3, axis=-1)
    q, k, v = (a.reshape(B, T, H, D).transpose(0, 2, 1, 3) for a in (q, k, v))
    q = l2_norm(q) * (D ** -0.5)
    k = l2_norm(k)
    beta = jax.nn.sigmoid(u_b.astype(jnp.float32)).transpose(0, 2, 1)
    g = (-jnp.exp(a_log.astype(jnp.float32)) *
         jax.nn.softplus(u_a.astype(jnp.float32) + dt_bias.astype(jnp.float32))).transpose(0, 2, 1)
    o = chunk_gated_delta(q, k, v, g, beta).transpose(0, 2, 1, 3)
    o = rms_norm(o, out_norm_g) * jax.nn.silu(u_z.astype(jnp.float32).reshape(B, T, H, D))
    return o.reshape(B, T, GDN_WIDTH).astype(u_qkv.dtype)


def expert_ffn(xb, wg, bg, wu, bu, wd, bd):
    gate = jnp.minimum(xb @ wg + bg, SWIGLU_LIMIT)
    up = jnp.clip(xb @ wu + bu, -SWIGLU_LIMIT, SWIGLU_LIMIT)
    glu = gate * jax.nn.sigmoid(gate * SWIGLU_ALPHA)
    return ((up + 1.0) * glu) @ wd + bd


def moe_ffn(h, w_router, b_router, w_gate, b_gate, w_up, b_up, w_down, b_down):
    N, D = h.shape
    R = MOE_ROW_BLOCK
    logits = (h @ w_router + b_router).astype(jnp.float32)
    top_val, top_idx = lax.top_k(logits, TOP_K)
    gates = jax.nn.softmax(top_val, axis=-1)
    nk = N * TOP_K
    flat_e = top_idx.reshape(-1)
    order = jnp.argsort(flat_e)
    e_sorted = flat_e[order]
    tok_sorted = order // TOP_K
    gate_sorted = gates.reshape(-1)[order]
    counts = jnp.bincount(flat_e, length=N_EXPERTS)
    starts = jnp.cumsum(counts) - counts
    pcounts = (counts + R - 1) // R * R
    pends = jnp.cumsum(pcounts)
    pstarts = pends - pcounts
    dest = pstarts[e_sorted] + jnp.arange(nk) - starts[e_sorted]
    n_rows = (nk + R - 1) // R * R + N_EXPERTS * R
    n_blocks = n_rows // R
    row_tok = jnp.zeros((n_rows,), jnp.int32).at[dest].set(tok_sorted.astype(jnp.int32))
    row_gate = jnp.zeros((n_rows,), jnp.float32).at[dest].set(gate_sorted)
    blk_e = jnp.minimum(jnp.searchsorted(pends, jnp.arange(n_blocks) * R, side='right'), N_EXPERTS - 1)
    xs = h[row_tok].reshape(n_blocks, R, D)

    def run(args):
        xb, e = args
        return expert_ffn(xb, w_gate[e], b_gate[e], w_up[e], b_up[e], w_down[e], b_down[e])

    ys = lax.map(run, (xs, blk_e)).reshape(n_rows, D)
    ys = ys * row_gate[:, None].astype(ys.dtype)
    return jnp.zeros((N, D), h.dtype).at[row_tok].add(ys)


def setup_inputs(seed: int = 0) -> dict:
    key = jax.random.key(seed)
    keys = iter(list(jax.random.split(key, 64)))

    def normal(shape, scale):
        return jax.random.normal(next(keys), shape, jnp.float32) * scale

    def gain(shape):
        return 1.0 + normal(shape, 0.02)

    L, D, DH = DEPTH, D_MODEL, NSA_HEAD_DIM
    flat = CMP_BLOCK * DH
    dt = jnp.exp(jax.random.uniform(next(keys), (L, GDN_HEADS), jnp.float32, math.log(1e-3), math.log(1e-1)))
    a_log = jnp.log(jax.random.uniform(next(keys), (L, GDN_HEADS), jnp.float32, 1.0, 16.0))
    return {
        'x': normal((BATCH, SEQ, D), 1.0),
        'attn_norm_g': gain((L, D)),
        'w_in': normal((L, D, IN_WIDTH), D ** -0.5),
        'nsa_q_norm_g': gain((L, DH)),
        'nsa_kc_norm_g': gain((L, DH)),
        'nsa_ks_norm_g': gain((L, DH)),
        'nsa_kw_norm_g': gain((L, DH)),
        'cmp_k_pos': normal((L, CMP_BLOCK, DH), 0.1),
        'cmp_k_w1': normal((L, flat, CMP_HIDDEN), flat ** -0.5),
        'cmp_k_b1': normal((L, CMP_HIDDEN), 0.01),
        'cmp_k_w2': normal((L, CMP_HIDDEN, DH), CMP_HIDDEN ** -0.5),
        'cmp_k_b2': normal((L, DH), 0.01),
        'cmp_v_pos': normal((L, CMP_BLOCK, DH), 0.1),
        'cmp_v_w1': normal((L, flat, CMP_HIDDEN), flat ** -0.5),
        'cmp_v_b1': normal((L, CMP_HIDDEN), 0.01),
        'cmp_v_w2': normal((L, CMP_HIDDEN, DH), CMP_HIDDEN ** -0.5),
        'cmp_v_b2': normal((L, DH), 0.01),
        'nsa_out_norm_g': gain((L, NSA_WIDTH)),
        'gdn_conv_w': normal((L, GDN_CONV, 3 * GDN_WIDTH), GDN_CONV ** -0.5),
        'gdn_a_log': a_log,
        'gdn_dt_bias': dt + jnp.log(-jnp.expm1(-dt)),
        'gdn_out_norm_g': gain((L, GDN_HEAD_DIM)),
        'w_out': normal((L, MIX_WIDTH, D), MIX_WIDTH ** -0.5),
        'ffn_norm_g': gain((L, D)),
        'router_w': normal((L, D, N_EXPERTS), D ** -0.5),
        'router_b': normal((L, N_EXPERTS), 0.01),
        'exp_w_gate': normal((L, N_EXPERTS, D, D_EXPERT), D ** -0.5),
        'exp_b_gate': normal((L, N_EXPERTS, D_EXPERT), 0.01),
        'exp_w_up': normal((L, N_EXPERTS, D, D_EXPERT), D ** -0.5),
        'exp_b_up': normal((L, N_EXPERTS, D_EXPERT), 0.01),
        'exp_w_down': normal((L, N_EXPERTS, D_EXPERT, D), D_EXPERT ** -0.5),
        'exp_b_down': normal((L, N_EXPERTS, D), 0.01),
    }


def reference(x, attn_norm_g, w_in, nsa_q_norm_g, nsa_kc_norm_g, nsa_ks_norm_g, nsa_kw_norm_g,
              cmp_k_pos, cmp_k_w1, cmp_k_b1, cmp_k_w2, cmp_k_b2,
              cmp_v_pos, cmp_v_w1, cmp_v_b1, cmp_v_w2, cmp_v_b2,
              nsa_out_norm_g, gdn_conv_w, gdn_a_log, gdn_dt_bias, gdn_out_norm_g, w_out,
              ffn_norm_g, router_w, router_b, exp_w_gate, exp_b_gate, exp_w_up, exp_b_up,
              exp_w_down, exp_b_down):
    B, T, D = x.shape
    splits = [int(s) for s in np.cumsum(IN_SPLITS)[:-1]]
    for l in range(DEPTH):
        h = rms_norm(x, attn_norm_g[l])
        u = h @ w_in[l]
        (u_q, u_kc, u_vc, u_ks, u_vs, u_kw, u_vw, u_gate,
         u_qkv, u_z, u_a, u_b) = jnp.split(u, splits, axis=-1)
        o_nsa = nsa_mixer(u_q, u_kc, u_vc, u_ks, u_vs, u_kw, u_vw, u_gate,
                          nsa_q_norm_g[l], nsa_kc_norm_g[l], nsa_ks_norm_g[l], nsa_kw_norm_g[l],
                          cmp_k_pos[l], cmp_k_w1[l], cmp_k_b1[l], cmp_k_w2[l], cmp_k_b2[l],
                          cmp_v_pos[l], cmp_v_w1[l], cmp_v_b1[l], cmp_v_w2[l], cmp_v_b2[l])
        o_nsa = rms_norm(o_nsa, nsa_out_norm_g[l])
        o_gdn = gdn_mixer(u_qkv, u_z, u_a, u_b, gdn_conv_w[l], gdn_a_log[l], gdn_dt_bias[l], gdn_out_norm_g[l])
        x = x + jnp.concatenate([o_nsa, o_gdn], axis=-1) @ w_out[l]
        h = rms_norm(x, ffn_norm_g[l])
        y = moe_ffn(h.reshape(B * T, D), router_w[l], router_b[l], exp_w_gate[l], exp_b_gate[l],
                    exp_w_up[l], exp_b_up[l], exp_w_down[l], exp_b_down[l])
        x = x + y.reshape(B, T, D)
    return x
```

```python
import functools

import jax
import jax.numpy as jnp
import numpy as np
from jax import lax
from jax.experimental import pallas as pl
from jax.experimental.pallas import tpu as pltpu

F32 = jnp.float32
BF16 = jnp.bfloat16

EPS = 1e-6
NEG = -1e30

NSA_HEADS = 8
NSA_GROUPS = 2
NSA_HPG = 4
NSA_DH = 64
CMP_BLOCK = 32
CMP_STRIDE = 16
CMP_HIDDEN = 256
SLC_BLOCK = 64
SLC_TOPK = 16
WINDOW = 512
Q_BLOCK = 128
GDN_HEADS = 4
GDN_DH = 128
GDN_CONV = 4
GDN_CHUNK = 64
N_EXPERTS = 32
TOP_K = 4
SWIGLU_LIMIT = 7.0
SWIGLU_ALPHA = 1.702
MOE_ROW_BLOCK = 256

LANES = 128
NSA_W = NSA_HEADS * NSA_DH
GDN_W = GDN_HEADS * GDN_DH

_NT = (((1,), (1,)), ((), ()))
_TN = (((0,), (0,)), ((), ()))


def _cparams(sem, vmem_mb):
    return pltpu.CompilerParams(dimension_semantics=sem, vmem_limit_bytes=vmem_mb * 1024 * 1024)


def _dot(a, b):
    return jnp.dot(a, b, preferred_element_type=F32)


def _dot_nt(a, b):
    return lax.dot_general(a, b, _NT, preferred_element_type=F32)


def _dot_hi(a, b):
    return jnp.dot(a, b, preferred_element_type=F32, precision=lax.Precision.HIGHEST)


def _inproj_body(x_ref, g_ref, wq_ref, wkv_ref, wqkv_ref, wz_ref, ws_ref, qg_ref, kg_ref,
                 oq_ref, okv_ref, oqkv_ref, oz_ref, os_ref):
    x = x_ref[...]
    h = (x * lax.rsqrt(jnp.mean(x * x, axis=-1, keepdims=True) + EPS) * g_ref[...]).astype(BF16)
    tm = x.shape[0]

    yq = _dot(h, wq_ref[...])
    for s in range(NSA_HEADS):
        sl = slice(s * LANES, (s + 1) * LANES)
        ys = yq[:, sl]
        ms = jnp.sum(ys * ys, axis=-1, keepdims=True) * (1.0 / NSA_DH)
        oq_ref[:, sl] = (ys * lax.rsqrt(ms + EPS) * qg_ref[:, sl]).astype(BF16)

    ykv = _dot(h, wkv_ref[...])
    lane = lax.broadcasted_iota(jnp.int32, (tm, LANES), 1)
    low = lane < NSA_DH
    for s in range(6):
        sl = slice(s * LANES, (s + 1) * LANES)
        ys = ykv[:, sl]
        if s in (2, 4):
            y2 = ys * ys
            s0 = jnp.sum(jnp.where(low, y2, 0.0), axis=-1, keepdims=True)
            s1 = jnp.sum(jnp.where(low, 0.0, y2), axis=-1, keepdims=True)
            ms = jnp.where(low, s0, s1) * (1.0 / NSA_DH)
            ys = ys * lax.rsqrt(ms + EPS) * kg_ref[:, sl]
        okv_ref[:, sl] = ys.astype(BF16)

    oqkv_ref[...] = _dot(h, wqkv_ref[...]).astype(BF16)
    oz_ref[...] = _dot(h, wz_ref[...]).astype(BF16)
    os_ref[...] = _dot(h, ws_ref[...])


def _in_proj(x2, g, wq, wkv, wqkv, wz, ws, qg, kg, tm=512):
    n, d = x2.shape
    full = lambda a: pl.BlockSpec(a.shape, lambda i: (0,) * a.ndim)
    row = lambda w: pl.BlockSpec((tm, w), lambda i: (i, 0))
    outs = [(wq.shape[1], BF16), (wkv.shape[1], BF16), (wqkv.shape[1], BF16), (wz.shape[1], BF16),
            (ws.shape[1], F32)]
    return pl.pallas_call(
        _inproj_body,
        grid=(n // tm,),
        in_specs=[row(d), full(g), full(wq), full(wkv), full(wqkv), full(wz), full(ws), full(qg), full(kg)],
        out_specs=[row(w) for w, _ in outs],
        out_shape=[jax.ShapeDtypeStruct((n, w), dt) for w, dt in outs],
        compiler_params=_cparams(("parallel",), 56),
        name="in_proj",
    )(x2, g, wq, wkv, wqkv, wz, ws, qg, kg)


def _compress_body(x_ref, pos_ref, w1_ref, b1_ref, w2_ref, b2_ref, g_ref, o_ref, *, n_cmp):
    is_key = pl.program_id(1) == 0
    nch = x_ref.shape[3]
    row = lax.broadcasted_iota(jnp.int32, (nch, NSA_DH), 0)
    outs = []
    for grp in range(NSA_GROUPS):
        x = x_ref[0, 0, grp].astype(F32)
        xa = (x + pos_ref[0, 0]).astype(BF16)
        xb = (x + pos_ref[0, 1]).astype(BF16)
        a = _dot(xa, w1_ref[0, 0])
        b = _dot(xb, w1_ref[0, 1])
        b_next = pltpu.roll(b, nch - 1, 0)
        hid = jax.nn.gelu(a + b_next + b1_ref[0])
        out = _dot(hid.astype(BF16), w2_ref[0]) + b2_ref[0]
        normed = out * lax.rsqrt(jnp.mean(out * out, axis=-1, keepdims=True) + EPS) * g_ref[...]
        out = jnp.where(is_key, normed, out)
        outs.append(jnp.where(row < n_cmp, out, 0.0))
    o_ref[0, 0] = jnp.concatenate(outs, axis=-1).astype(BF16)


def _compress(xflat, pos, w1, b1, w2, b2, kc_g, n_cmp):
    b, _, _, nch, flat = xflat.shape
    return pl.pallas_call(
        functools.partial(_compress_body, n_cmp=n_cmp),
        grid=(b, 2),
        in_specs=[
            pl.BlockSpec((1, 1, NSA_GROUPS, nch, flat), lambda i, j: (i, j, 0, 0, 0)),
            pl.BlockSpec((1, 2, 1, flat), lambda i, j: (j, 0, 0, 0)),
            pl.BlockSpec((1, 2, flat, CMP_HIDDEN), lambda i, j: (j, 0, 0, 0)),
            pl.BlockSpec((1, 1, CMP_HIDDEN), lambda i, j: (j, 0, 0)),
            pl.BlockSpec((1, CMP_HIDDEN, NSA_DH), lambda i, j: (j, 0, 0)),
            pl.BlockSpec((1, 1, NSA_DH), lambda i, j: (j, 0, 0)),
            pl.BlockSpec((1, NSA_DH), lambda i, j: (0, 0)),
        ],
        out_specs=pl.BlockSpec((1, 1, nch, LANES), lambda i, j: (i, j, 0, 0)),
        out_shape=jax.ShapeDtypeStruct((b, 2, nch, LANES), BF16),
        compiler_params=_cparams(("parallel", "parallel"), 32),
        name="nsa_compress",
    )(xflat, pos, w1, b1, w2, b2, kc_g)


def _masked_softmax3(s3, valid):
    s3 = jnp.where(valid[None], s3, NEG)
    m = jnp.max(s3, axis=-1, keepdims=True)
    p = jnp.where(valid[None], jnp.exp(s3 - m), 0.0)
    return p / jnp.maximum(jnp.sum(p, axis=-1, keepdims=True), 1e-30)


def _nsa_body(q_ref, gt_ref, kc_ref, vc_ref, ks_ref, vs_ref, kw_ref, vw_ref, e_ref, ov_ref, o_ref,
              m_sc, l_sc, acc_sc, *, n_cmp, n_top, kt):
    grp = pl.program_id(1)
    s0 = pl.program_id(2) * Q_BLOCK
    hq = NSA_HPG * Q_BLOCK
    nch = kc_ref.shape[2]
    nblk = ov_ref.shape[0]

    q = jnp.concatenate([q_ref[0, :, h * LANES:(h + 1) * LANES] for h in range(NSA_HPG)], axis=0)
    t_col = s0 + lax.broadcasted_iota(jnp.int32, (Q_BLOCK, 1), 0)
    t_row = s0 + lax.broadcasted_iota(jnp.int32, (1, Q_BLOCK), 1)

    sc = _dot_nt(q, kc_ref[0, 0]).reshape(NSA_HPG, Q_BLOCK, nch)
    cidx = lax.broadcasted_iota(jnp.int32, (1, nch), 1)
    cvalid = (cidx * CMP_STRIDE + (CMP_BLOCK - 1) <= t_col) & (cidx < n_cmp)
    pc = _masked_softmax3(sc, cvalid)
    oc = _dot(pc.reshape(hq, nch).astype(BF16), vc_ref[0, 0])

    pcs = pc[0] + pc[1] + pc[2] + pc[3]
    pcs_hi = pcs.astype(BF16)
    pcs_lo = (pcs - pcs_hi.astype(F32)).astype(BF16)
    imp = _dot_nt(ov_ref[...], pcs_hi) + _dot_nt(ov_ref[...], pcs_lo)
    blk = lax.broadcasted_iota(jnp.int32, (nblk, Q_BLOCK), 0)
    cur = t_row // SLC_BLOCK
    imp = jnp.where(blk * SLC_BLOCK > t_row, NEG, imp)
    imp = jnp.where((blk == 0) | (blk == cur) | (blk == cur - 1), -NEG, imp)

    def pick(_, carry):
        v, sel = carry
        mx = jnp.max(v, axis=0, keepdims=True)
        first = jnp.min(jnp.where(v == mx, blk, nblk), axis=0, keepdims=True)
        hit = blk == first
        return jnp.where(hit, -jnp.inf, v), jnp.where(hit, 1.0, sel)

    _, sel_t = lax.fori_loop(0, n_top, pick, (imp, jnp.zeros((nblk, Q_BLOCK), F32)), unroll=True)
    sel = sel_t.T.astype(BF16)

    m_sc[...] = jnp.full(m_sc.shape, NEG, F32)
    l_sc[...] = jnp.zeros(l_sc.shape, F32)
    acc_sc[...] = jnp.zeros(acc_sc.shape, F32)

    def tile(j, carry):
        k0 = pl.multiple_of(j * kt, kt)
        kpos = k0 + lax.broadcasted_iota(jnp.int32, (1, kt), 1)
        picked = _dot(sel, e_ref[:, pl.ds(k0, kt)])
        bias = jnp.where((picked > 0.5) & (kpos <= t_col), 0.0, NEG)
        s3 = _dot_nt(q, ks_ref[0, pl.ds(k0, kt), :]).reshape(NSA_HPG, Q_BLOCK, kt) + bias[None]
        m_old = m_sc[...]
        m_new = jnp.maximum(m_old, jnp.max(s3, axis=-1, keepdims=True))
        alpha = jnp.exp(m_old - m_new)
        p = jnp.exp(s3 - m_new)
        l_sc[...] = alpha * l_sc[...] + jnp.sum(p, axis=-1, keepdims=True)
        pv = _dot(p.reshape(hq, kt).astype(BF16), vs_ref[0, pl.ds(k0, kt), :])
        acc_sc[...] = alpha * acc_sc[...] + pv.reshape(NSA_HPG, Q_BLOCK, LANES)
        m_sc[...] = m_new
        return carry

    lax.fori_loop(0, (s0 + Q_BLOCK + kt - 1) // kt, tile, 0)
    osl = acc_sc[...] / jnp.maximum(l_sc[...], 1e-30)

    wlen = WINDOW + Q_BLOCK
    w0 = pl.multiple_of(jnp.maximum(s0 - WINDOW, 0), Q_BLOCK)
    kpos = w0 + lax.broadcasted_iota(jnp.int32, (1, wlen), 1)
    wvalid = (kpos <= t_col) & (kpos > t_col - WINDOW)
    sw = _dot_nt(q, kw_ref[0, pl.ds(w0, wlen), :]).reshape(NSA_HPG, Q_BLOCK, wlen)
    pw = _masked_softmax3(sw, wvalid)
    ow = _dot(pw.reshape(hq, wlen).astype(BF16), vw_ref[0, pl.ds(w0, wlen), :])

    gts = jax.nn.sigmoid(gt_ref[0])
    halves = []
    for h in range(NSA_HPG):
        rows = slice(h * Q_BLOCK, (h + 1) * Q_BLOCK)
        mix = (gts[:, 3 * h:3 * h + 1] * oc[rows] + gts[:, 3 * h + 1:3 * h + 2] * osl[h]
               + gts[:, 3 * h + 2:3 * h + 3] * ow[rows])
        halves.append(jnp.where(grp == 0, mix[:, :NSA_DH], mix[:, NSA_DH:]))
    o_ref[0] = jnp.concatenate(halves, axis=-1)


def _nsa_attention(q, small, kcp, okv, expand, overlap_t, n_cmp, n_top, kt):
    b, t, _ = q.shape
    nch = kcp.shape[2]
    seq = lambda s: pl.BlockSpec((1, t, LANES), lambda bi, g, i: (bi, 0, s))
    return pl.pallas_call(
        functools.partial(_nsa_body, n_cmp=n_cmp, n_top=n_top, kt=kt),
        grid=(b, NSA_GROUPS, t // Q_BLOCK),
        in_specs=[
            pl.BlockSpec((1, Q_BLOCK, NSA_HPG * LANES), lambda bi, g, i: (bi, i, g)),
            pl.BlockSpec((1, Q_BLOCK, LANES), lambda bi, g, i: (bi, i, g)),
            pl.BlockSpec((1, 1, nch, LANES), lambda bi, g, i: (bi, 0, 0, 0)),
            pl.BlockSpec((1, 1, nch, LANES), lambda bi, g, i: (bi, 1, 0, 0)),
            seq(2), seq(3), seq(4), seq(5),
            pl.BlockSpec(expand.shape, lambda bi, g, i: (0, 0)),
            pl.BlockSpec(overlap_t.shape, lambda bi, g, i: (0, 0)),
        ],
        out_specs=pl.BlockSpec((1, Q_BLOCK, NSA_HPG * NSA_DH), lambda bi, g, i: (bi, i, g)),
        out_shape=jax.ShapeDtypeStruct((b, t, NSA_W), F32),
        scratch_shapes=[pltpu.VMEM((NSA_HPG, Q_BLOCK, 1), F32), pltpu.VMEM((NSA_HPG, Q_BLOCK, 1), F32),
                        pltpu.VMEM((NSA_HPG, Q_BLOCK, LANES), F32)],
        compiler_params=_cparams(("parallel", "parallel", "arbitrary"), 56),
        name="nsa_attention",
    )(q, small, kcp, kcp, okv, okv, okv, okv, expand, overlap_t)


def _unit_lower_inverse(lmat):
    c = lmat.shape[0]
    r = lax.broadcasted_iota(jnp.int32, (c, c), 0)
    col = lax.broadcasted_iota(jnp.int32, (c, c), 1)
    x = jnp.where(r == col, 1.0, 0.0) - lmat
    p = _dot_hi(lmat, lmat)
    steps = int(np.log2(c)) - 1
    for s in range(steps):
        x = x + _dot_hi(x, p)
        if s + 1 < steps:
            p = _dot_hi(p, p)
    return x


def _gdn_body(x_ref, z_ref, ab_ref, cw_ref, alog_ref, dtb_ref, og_ref, o_ref, xbuf, s_sc, *, ct):
    c = pl.program_id(1)
    halo = 8

    @pl.when(c == 0)
    def _():
        xbuf[0:halo, :] = jnp.zeros((halo, xbuf.shape[1]), F32)
        s_sc[...] = jnp.zeros(s_sc.shape, F32)

    @pl.when(c > 0)
    def _():
        xbuf[0:halo, :] = xbuf[ct:ct + halo, :]

    xbuf[halo:halo + ct, :] = x_ref[0].astype(F32)
    y = cw_ref[0:1, :] * xbuf[pl.ds(halo - 3, ct), :]
    for k in range(1, GDN_CONV):
        y = y + cw_ref[k:k + 1, :] * xbuf[pl.ds(halo - 3 + k, ct), :]
    y = y * jax.nn.sigmoid(y)

    ab = ab_ref[0]
    g_all = -jnp.exp(alog_ref[...]) * jax.nn.softplus(ab + dtb_ref[...])
    beta_all = jax.nn.sigmoid(ab)

    ch = GDN_CHUNK
    r = lax.broadcasted_iota(jnp.int32, (ch, ch), 0)
    col = lax.broadcasted_iota(jnp.int32, (ch, ch), 1)
    incl = r >= col
    strict = r > col
    tril = jnp.where(incl, 1.0, 0.0)

    for ci in range(ct // ch):
        rows = slice(ci * ch, (ci + 1) * ch)
        gc_all = _dot_hi(tril, g_all[rows])
        gc_t = gc_all.T
        for h in range(GDN_HEADS):
            hs = slice(h * GDN_DH, (h + 1) * GDN_DH)
            qh = y[rows, hs]
            kh = y[rows, GDN_W + h * GDN_DH:GDN_W + (h + 1) * GDN_DH]
            vh = y[rows, 2 * GDN_W + h * GDN_DH:2 * GDN_W + (h + 1) * GDN_DH]
            qh = qh * lax.rsqrt(jnp.sum(qh * qh, axis=-1, keepdims=True) + EPS) * (GDN_DH ** -0.5)
            kh = kh * lax.rsqrt(jnp.sum(kh * kh, axis=-1, keepdims=True) + EPS)
            gc = gc_all[:, h:h + 1]
            gr = gc_t[h:h + 1, :]
            g_last = gc_all[ch - 1:ch, h:h + 1]
            beta = beta_all[rows, GDN_HEADS + h:GDN_HEADS + h + 1]
            eg = jnp.exp(gc)
            decay = jnp.where(incl, jnp.exp(jnp.minimum(gc - gr, 0.0)), 0.0)
            kb = kh * beta
            vb = vh * beta
            k16 = kh.astype(BF16)
            lmat = jnp.where(strict, _dot_nt(kb.astype(BF16), k16) * decay, 0.0)
            tinv = _unit_lower_inverse(lmat).astype(BF16)
            u = _dot(tinv, vb.astype(BF16))
            w = _dot(tinv, (kb * eg).astype(BF16))
            qk = jnp.where(incl, _dot_nt(qh.astype(BF16), k16) * decay, 0.0)
            s_old = s_sc[h]
            s16 = s_old.astype(BF16)
            v_new = u - _dot(w.astype(BF16), s16)
            o = _dot((qh * eg).astype(BF16), s16) + _dot(qk.astype(BF16), v_new.astype(BF16))
            kd = kh * jnp.exp(g_last - gc)
            s_sc[h] = s_old * jnp.exp(g_last) + lax.dot_general(
                kd.astype(BF16), v_new.astype(BF16), _TN, preferred_element_type=F32)
            on = o * lax.rsqrt(jnp.mean(o * o, axis=-1, keepdims=True) + EPS) * og_ref[...]
            zh = z_ref[0, rows, hs].astype(F32)
            o_ref[0, rows, hs] = (on * (zh * jax.nn.sigmoid(zh))).astype(BF16)


def _gdn(oqkv, oz, small, conv_w, alog, dtb, og, ct=128):
    b, t, w3 = oqkv.shape
    full = lambda a: pl.BlockSpec(a.shape, lambda bi, c: (0,) * a.ndim)
    return pl.pallas_call(
        functools.partial(_gdn_body, ct=ct),
        grid=(b, t // ct),
        in_specs=[
            pl.BlockSpec((1, ct, w3), lambda bi, c: (bi, c, 0)),
            pl.BlockSpec((1, ct, GDN_W), lambda bi, c: (bi, c, 0)),
            pl.BlockSpec((1, ct, LANES), lambda bi, c: (bi, c, 2)),
            full(conv_w), full(alog), full(dtb), full(og),
        ],
        out_specs=pl.BlockSpec((1, ct, GDN_W), lambda bi, c: (bi, c, 0)),
        out_shape=jax.ShapeDtypeStruct((b, t, GDN_W), BF16),
        scratch_shapes=[pltpu.VMEM((ct + 8, w3), F32), pltpu.VMEM((GDN_HEADS, GDN_DH, GDN_DH), F32)],
        compiler_params=_cparams(("parallel", "arbitrary"), 32),
        name="gdn",
    )(oqkv, oz, small, conv_w, alog, dtb, og)


def _outproj_body(on_ref, og_ref, x_ref, ng_ref, wo_ref, fg_ref, wr_ref, br_ref,
                  x1_ref, h2_ref, idx_ref, gate_ref, rank_ref, cnt_ref, cnt_sc):
    i = pl.program_id(0)
    tm = x_ref.shape[0]

    @pl.when(i == 0)
    def _():
        cnt_sc[...] = jnp.zeros(cnt_sc.shape, F32)

    a = on_ref[...]
    a = (a * lax.rsqrt(jnp.mean(a * a, axis=-1, keepdims=True) + EPS) * ng_ref[...]).astype(BF16)
    mix = jnp.concatenate([a, og_ref[...]], axis=-1)
    x1 = x_ref[...] + _dot(mix, wo_ref[...])
    x1_ref[...] = x1
    h2 = (x1 * lax.rsqrt(jnp.mean(x1 * x1, axis=-1, keepdims=True) + EPS) * fg_ref[...]).astype(BF16)
    h2_ref[...] = h2

    lane = lax.broadcasted_iota(jnp.int32, (tm, LANES), 1)
    logits = jnp.where(lane < N_EXPERTS, _dot(h2, wr_ref[...]) + br_ref[...], -jnp.inf)
    onehot = jnp.zeros((tm, LANES), F32)
    idx = jnp.zeros((tm, LANES), jnp.int32)
    vals = jnp.zeros((tm, LANES), F32)
    firsts = []
    v = logits
    top0 = None
    for k in range(TOP_K):
        mx = jnp.max(v, axis=-1, keepdims=True)
        first = jnp.min(jnp.where(v == mx, lane, LANES), axis=-1, keepdims=True)
        hit = lane == first
        v = jnp.where(hit, -jnp.inf, v)
        onehot = jnp.where(hit, 1.0, onehot)
        idx = jnp.where(lane == k, first, idx)
        top0 = mx if k == 0 else top0
        vals = jnp.where(lane == k, jnp.exp(mx - top0), vals)
        firsts.append(first)
    idx_ref[...] = idx
    gate_ref[...] = vals / jnp.sum(vals, axis=-1, keepdims=True)

    r = lax.broadcasted_iota(jnp.int32, (tm, tm), 0)
    c = lax.broadcasted_iota(jnp.int32, (tm, tm), 1)
    before = jnp.where(r > c, 1.0, 0.0).astype(BF16)
    excl = cnt_sc[...] + _dot(before, onehot.astype(BF16))
    rank = jnp.zeros((tm, LANES), F32)
    for k in range(TOP_K):
        rk = jnp.sum(jnp.where(lane == firsts[k], excl, 0.0), axis=-1, keepdims=True)
        rank = jnp.where(lane == k, rk, rank)
    rank_ref[...] = rank.astype(jnp.int32)
    cnt_sc[...] = cnt_sc[...] + jnp.sum(onehot, axis=0, keepdims=True)
    cnt_ref[...] = cnt_sc[...].astype(jnp.int32)


def _out_proj(o_nsa, o_gdn, x2, ng, wo, fg, wr, br, tm=512):
    n, d = x2.shape
    full = lambda a: pl.BlockSpec(a.shape, lambda i: (0,) * a.ndim)
    row = lambda w: pl.BlockSpec((tm, w), lambda i: (i, 0))
    return pl.pallas_call(
        _outproj_body,
        grid=(n // tm,),
        in_specs=[row(NSA_W), row(GDN_W), row(d), full(ng), full(wo), full(fg), full(wr), full(br)],
        out_specs=[row(d), row(d), row(LANES), row(LANES), row(LANES),
                   pl.BlockSpec((1, LANES), lambda i: (0, 0))],
        out_shape=[jax.ShapeDtypeStruct((n, d), F32), jax.ShapeDtypeStruct((n, d), BF16),
                   jax.ShapeDtypeStruct((n, LANES), jnp.int32), jax.ShapeDtypeStruct((n, LANES), F32),
                   jax.ShapeDtypeStruct((n, LANES), jnp.int32), jax.ShapeDtypeStruct((1, LANES), jnp.int32)],
        scratch_shapes=[pltpu.VMEM((1, LANES), F32)],
        compiler_params=_cparams(("arbitrary",), 48),
        name="out_proj_router",
    )(o_nsa, o_gdn, x2, ng, wo, fg, wr, br)


def _expert_body(be_ref, xs_ref, wg_ref, bg_ref, wu_ref, bu_ref, wd_ref, bd_ref, y_ref, wg16, wu16, wd16):
    i = pl.program_id(0)
    prev = be_ref[jnp.maximum(i - 1, 0)]
    fresh = (i == 0) | (be_ref[i] != prev)

    @pl.when(fresh)
    def _():
        wg16[...] = wg_ref[0].astype(BF16)
        wu16[...] = wu_ref[0].astype(BF16)
        wd16[...] = wd_ref[0].astype(BF16)

    x = xs_ref[...]
    gate = jnp.minimum(_dot(x, wg16[...]) + bg_ref[0], SWIGLU_LIMIT)
    up = jnp.clip(_dot(x, wu16[...]) + bu_ref[0], -SWIGLU_LIMIT, SWIGLU_LIMIT)
    glu = gate * jax.nn.sigmoid(gate * SWIGLU_ALPHA)
    y_ref[...] = _dot(((up + 1.0) * glu).astype(BF16), wd16[...]) + bd_ref[0]


def _experts(blk_e, xs, wg, bg, wu, bu, wd, bd):
    n_rows, d = xs.shape
    de = wg.shape[2]
    r = MOE_ROW_BLOCK
    wspec = lambda a, b: pl.BlockSpec((1, a, b), lambda i, be: (be[i], 0, 0))
    grid_spec = pltpu.PrefetchScalarGridSpec(
        num_scalar_prefetch=1,
        grid=(n_rows // r,),
        in_specs=[pl.BlockSpec((r, d), lambda i, be: (i, 0)),
                  wspec(d, de), wspec(1, de), wspec(d, de), wspec(1, de), wspec(de, d), wspec(1, d)],
        out_specs=pl.BlockSpec((r, d), lambda i, be: (i, 0)),
        scratch_shapes=[pltpu.VMEM((d, de), BF16), pltpu.VMEM((d, de), BF16), pltpu.VMEM((de, d), BF16)],
    )
    return pl.pallas_call(
        _expert_body,
        grid_spec=grid_spec,
        out_shape=jax.ShapeDtypeStruct((n_rows, d), F32),
        compiler_params=_cparams(("arbitrary",), 56),
        name="moe_experts",
    )(blk_e, xs, wg, bg, wu, bu, wd, bd)


def _pad_lanes(a, width=LANES):
    return jnp.pad(a, ((0, 0), (0, width - a.shape[1])))


def _layer(x, attn_norm_g, w_in, q_g, kc_g, ks_g, kw_g, ck_pos, ck_w1, ck_b1, ck_w2, ck_b2,
           cv_pos, cv_w1, cv_b1, cv_w2, cv_b2, nsa_out_g, conv_w, a_log, dt_bias, gdn_out_g, w_out,
           ffn_g, router_w, router_b, e_wg, e_bg, e_wu, e_bu, e_wd, e_bd):
    b, t, d = x.shape
    n = b * t
    x2 = x.reshape(n, d)

    o = np.cumsum([0, NSA_W] + [NSA_GROUPS * NSA_DH] * 6 + [3 * NSA_HEADS, 3 * GDN_W, GDN_W, GDN_HEADS, GDN_HEADS])
    wq = w_in[:, o[0]:o[1]].reshape(d, NSA_GROUPS, NSA_HPG, NSA_DH)
    wq_pad = jnp.zeros((d, NSA_GROUPS, NSA_HPG, NSA_GROUPS, NSA_DH), F32)
    for g in range(NSA_GROUPS):
        wq_pad = wq_pad.at[:, g, :, g, :].set(wq[:, g])
    wq_pad = wq_pad.reshape(d, NSA_HEADS * LANES).astype(BF16)
    qg_pad = jnp.zeros((NSA_GROUPS, NSA_HPG, NSA_GROUPS, NSA_DH), F32)
    for g in range(NSA_GROUPS):
        qg_pad = qg_pad.at[g, :, g, :].set(jnp.broadcast_to(q_g * (NSA_DH ** -0.5), (NSA_HPG, NSA_DH)))
    qg_pad = qg_pad.reshape(1, NSA_HEADS * LANES)
    wkv = w_in[:, o[1]:o[7]].astype(BF16)
    kg = jnp.concatenate([jnp.ones((2 * LANES,), F32), ks_g, ks_g, jnp.ones((LANES,), F32), kw_g, kw_g,
                          jnp.ones((LANES,), F32)]).reshape(1, 6 * LANES)
    w_gate = w_in[:, o[7]:o[8]].reshape(d, NSA_GROUPS, NSA_HPG * 3)
    ws = jnp.concatenate([_pad_lanes(w_gate[:, 0]), _pad_lanes(w_gate[:, 1]), _pad_lanes(w_in[:, o[10]:o[12]])],
                         axis=1).astype(BF16)
    wqkv = w_in[:, o[8]:o[9]].astype(BF16)
    wz = w_in[:, o[9]:o[10]].astype(BF16)

    oq, okv, oqkv, oz, osm = _in_proj(x2, attn_norm_g.reshape(1, d), wq_pad, wkv, wqkv, wz, ws, qg_pad, kg)

    nch = t // CMP_STRIDE
    n_cmp = (t - CMP_BLOCK) // CMP_STRIDE + 1
    half = CMP_STRIDE * NSA_DH
    xflat = okv[:, :2 * LANES].reshape(b, nch, CMP_STRIDE, 2, NSA_GROUPS, NSA_DH)
    xflat = xflat.transpose(0, 3, 4, 1, 2, 5).reshape(b, 2, NSA_GROUPS, nch, half)
    pos = jnp.stack([ck_pos, cv_pos]).reshape(2, 2, 1, half)
    w1 = jnp.stack([ck_w1, cv_w1]).reshape(2, 2, half, CMP_HIDDEN).astype(BF16)
    b1 = jnp.stack([ck_b1, cv_b1]).reshape(2, 1, CMP_HIDDEN)
    w2 = jnp.stack([ck_w2, cv_w2]).astype(BF16)
    b2 = jnp.stack([ck_b2, cv_b2]).reshape(2, 1, NSA_DH)
    kcp = _compress(xflat, pos, w1, b1, w2, b2, kc_g.reshape(1, NSA_DH), n_cmp)

    n_slc = t // SLC_BLOCK
    n_top = min(SLC_TOPK, n_slc)
    nblk = max(n_slc, LANES)
    kt = min(512, t)
    ci = np.arange(nch)[None, :] * CMP_STRIDE
    sj = np.arange(nblk)[:, None] * SLC_BLOCK
    overlap_t = ((ci < sj + SLC_BLOCK) & (ci + CMP_BLOCK > sj) & (np.arange(nch)[None, :] < n_cmp)
                 & (np.arange(nblk)[:, None] < n_slc))
    expand = (np.arange(t)[None, :] // SLC_BLOCK) == np.arange(nblk)[:, None]
    o_nsa = _nsa_attention(oq.reshape(b, t, -1), osm.reshape(b, t, -1), kcp, okv.reshape(b, t, -1),
                           jnp.asarray(expand, BF16), jnp.asarray(overlap_t, BF16), n_cmp, n_top, kt)

    alog_row = _pad_lanes(a_log.reshape(1, GDN_HEADS))
    dtb_row = _pad_lanes(dt_bias.reshape(1, GDN_HEADS))
    o_gdn = _gdn(oqkv.reshape(b, t, -1), oz.reshape(b, t, -1), osm.reshape(b, t, -1), conv_w,
                 alog_row, dtb_row, gdn_out_g.reshape(1, GDN_DH))

    wr = _pad_lanes(router_w).astype(BF16)
    br = _pad_lanes(router_b.reshape(1, N_EXPERTS))
    x1, h2, idx, gates, rank, counts = _out_proj(
        o_nsa.reshape(n, NSA_W), o_gdn.reshape(n, GDN_W), x2, nsa_out_g.reshape(1, NSA_W),
        w_out.astype(BF16), ffn_g.reshape(1, d), wr, br)

    r = MOE_ROW_BLOCK
    nk = n * TOP_K
    counts = counts[0, :N_EXPERTS]
    pcounts = (counts + r - 1) // r * r
    pends = jnp.cumsum(pcounts)
    pstarts = pends - pcounts
    top_idx = idx[:, :TOP_K]
    dest = pstarts[top_idx] + rank[:, :TOP_K]
    n_rows = (nk + r - 1) // r * r + N_EXPERTS * r
    n_blocks = n_rows // r
    tok = jnp.broadcast_to(jnp.arange(n, dtype=jnp.int32)[:, None], (n, TOP_K))
    row_tok = jnp.zeros((n_rows,), jnp.int32).at[dest.reshape(-1)].set(tok.reshape(-1))
    blk_e = jnp.minimum(jnp.searchsorted(pends, jnp.arange(n_blocks) * r, side='right'),
                        N_EXPERTS - 1).astype(jnp.int32)
    xs = h2[row_tok]
    ys = _experts(blk_e, xs, e_wg, e_bg.reshape(N_EXPERTS, 1, -1), e_wu, e_bu.reshape(N_EXPERTS, 1, -1),
                  e_wd, e_bd.reshape(N_EXPERTS, 1, -1))
    y = jnp.sum(ys[dest] * gates[:, :TOP_K, None], axis=1)
    return (x1 + y).reshape(b, t, d)


def kernel(x, attn_norm_g, w_in, nsa_q_norm_g, nsa_kc_norm_g, nsa_ks_norm_g, nsa_kw_norm_g, cmp_k_pos, cmp_k_w1, cmp_k_b1, cmp_k_w2, cmp_k_b2, cmp_v_pos, cmp_v_w1, cmp_v_b1, cmp_v_w2, cmp_v_b2, nsa_out_norm_g, gdn_conv_w, gdn_a_log, gdn_dt_bias, gdn_out_norm_g, w_out, ffn_norm_g, router_w, router_b, exp_w_gate, exp_b_gate, exp_w_up, exp_b_up, exp_w_down, exp_b_down):
    params = (attn_norm_g, w_in, nsa_q_norm_g, nsa_kc_norm_g, nsa_ks_norm_g, nsa_kw_norm_g,
              cmp_k_pos, cmp_k_w1, cmp_k_b1, cmp_k_w2, cmp_k_b2, cmp_v_pos, cmp_v_w1, cmp_v_b1, cmp_v_w2, cmp_v_b2,
              nsa_out_norm_g, gdn_conv_w, gdn_a_log, gdn_dt_bias, gdn_out_norm_g, w_out, ffn_norm_g,
              router_w, router_b, exp_w_gate, exp_b_gate, exp_w_up, exp_b_up, exp_w_down, exp_b_down)
    for l in range(attn_norm_g.shape[0]):
        x = _layer(x, *(p[l] for p in params))
    return x
```

```python
import functools

import jax
import jax.numpy as jnp
import numpy as np
from jax import lax
from jax.experimental import pallas as pl
from jax.experimental.pallas import tpu as pltpu

F32 = jnp.float32
BF16 = jnp.bfloat16

EPS = 1e-6
NEG = -1e30

NSA_HEADS = 8
NSA_GROUPS = 2
NSA_HPG = 4
NSA_DH = 64
CMP_BLOCK = 32
CMP_STRIDE = 16
CMP_HIDDEN = 256
SLC_BLOCK = 64
SLC_TOPK = 16
WINDOW = 512
Q_BLOCK = 128
GDN_HEADS = 4
GDN_DH = 128
GDN_CONV = 4
GDN_CHUNK = 64
N_EXPERTS = 32
TOP_K = 4
SWIGLU_LIMIT = 7.0
SWIGLU_ALPHA = 1.702
MOE_ROW_BLOCK = 256

LANES = 128
GATE_ROWS = 16
NSA_W = NSA_HEADS * NSA_DH
GDN_W = GDN_HEADS * GDN_DH

_NT = (((1,), (1,)), ((), ()))
_TN = (((0,), (0,)), ((), ()))


def _cparams(sem, vmem_mb):
    return pltpu.CompilerParams(dimension_semantics=sem, vmem_limit_bytes=vmem_mb * 1024 * 1024)


def _dot(a, b):
    return jnp.dot(a, b, preferred_element_type=F32)


def _dot_nt(a, b):
    return lax.dot_general(a, b, _NT, preferred_element_type=F32)


def _dot_tn(a, b):
    return lax.dot_general(a, b, _TN, preferred_element_type=F32)


def _dot_hi(a, b):
    return jnp.dot(a, b, preferred_element_type=F32, precision=lax.Precision.HIGHEST)


def _inproj_body(x_ref, g_ref, wqt_ref, wkv_ref, wvt_ref, wgt_ref, wqkv_ref, wz_ref, wab_ref, qg_ref, kg_ref,
                 oqt_ref, okv_ref, ovt_ref, ogt_ref, oqkv_ref, oz_ref, oab_ref):
    x = x_ref[...]
    h = (x * lax.rsqrt(jnp.mean(x * x, axis=-1, keepdims=True) + EPS) * g_ref[...]).astype(BF16)
    tm = x.shape[0]

    yq = _dot_nt(wqt_ref[...], h)
    for s in range(NSA_HEADS):
        sl = slice(s * LANES, (s + 1) * LANES)
        ys = yq[sl, :]
        ms = jnp.sum(ys * ys, axis=0, keepdims=True) * (1.0 / NSA_DH)
        oqt_ref[sl, :] = (ys * lax.rsqrt(ms + EPS) * qg_ref[sl, :]).astype(BF16)

    ykv = _dot(h, wkv_ref[...])
    lane = lax.broadcasted_iota(jnp.int32, (tm, LANES), 1)
    low = lane < NSA_DH
    for s in range(6):
        sl = slice(s * LANES, (s + 1) * LANES)
        ys = ykv[:, sl]
        if s in (2, 4):
            y2 = ys * ys
            s0 = jnp.sum(jnp.where(low, y2, 0.0), axis=-1, keepdims=True)
            s1 = jnp.sum(jnp.where(low, 0.0, y2), axis=-1, keepdims=True)
            ms = jnp.where(low, s0, s1) * (1.0 / NSA_DH)
            ys = ys * lax.rsqrt(ms + EPS) * kg_ref[:, sl]
        okv_ref[:, sl] = ys.astype(BF16)

    ovt_ref[...] = _dot_nt(wvt_ref[...], h).astype(BF16)
    ogt_ref[...] = _dot_nt(wgt_ref[...], h)
    oqkv_ref[...] = _dot(h, wqkv_ref[...]).astype(BF16)
    oz_ref[...] = _dot(h, wz_ref[...]).astype(BF16)
    oab_ref[...] = _dot(h, wab_ref[...])


def _in_proj(x2, g, wqt, wkv, wvt, wgt, wqkv, wz, wab, qg, kg, tm=512):
    n, d = x2.shape
    full = lambda a: pl.BlockSpec(a.shape, lambda i: (0,) * a.ndim)
    row = lambda w: pl.BlockSpec((tm, w), lambda i: (i, 0))
    colb = lambda r: pl.BlockSpec((r, tm), lambda i: (0, i))
    return pl.pallas_call(
        _inproj_body,
        grid=(n // tm,),
        in_specs=[row(d)] + [full(a) for a in (g, wqt, wkv, wvt, wgt, wqkv, wz, wab, qg, kg)],
        out_specs=[colb(wqt.shape[0]), row(wkv.shape[1]), colb(wvt.shape[0]), colb(wgt.shape[0]),
                   row(wqkv.shape[1]), row(wz.shape[1]), row(wab.shape[1])],
        out_shape=[jax.ShapeDtypeStruct((wqt.shape[0], n), BF16), jax.ShapeDtypeStruct((n, wkv.shape[1]), BF16),
                   jax.ShapeDtypeStruct((wvt.shape[0], n), BF16), jax.ShapeDtypeStruct((wgt.shape[0], n), F32),
                   jax.ShapeDtypeStruct((n, wqkv.shape[1]), BF16), jax.ShapeDtypeStruct((n, wz.shape[1]), BF16),
                   jax.ShapeDtypeStruct((n, wab.shape[1]), F32)],
        compiler_params=_cparams(("parallel",), 56),
        name="in_proj",
    )(x2, g, wqt, wkv, wvt, wgt, wqkv, wz, wab, qg, kg)


def _compress_body(x_ref, pos_ref, w1_ref, b1_ref, w2_ref, b2_ref, w2t_ref, b2t_ref, g_ref, ok_ref, ovt_ref,
                   *, n_cmp):
    is_key = pl.program_id(1) == 0
    nch = x_ref.shape[3]
    hids = []
    for grp in range(NSA_GROUPS):
        x = x_ref[0, 0, grp].astype(F32)
        xa = (x + pos_ref[0, 0]).astype(BF16)
        xb = (x + pos_ref[0, 1]).astype(BF16)
        a = _dot(xa, w1_ref[0, 0])
        b = _dot(xb, w1_ref[0, 1])
        b_next = pltpu.roll(b, nch - 1, 0)
        hids.append(jax.nn.gelu(a + b_next + b1_ref[0]).astype(BF16))

    @pl.when(is_key)
    def _():
        row = lax.broadcasted_iota(jnp.int32, (nch, NSA_DH), 0)
        outs = []
        for grp in range(NSA_GROUPS):
            out = _dot(hids[grp], w2_ref[0]) + b2_ref[0]
            out = out * lax.rsqrt(jnp.mean(out * out, axis=-1, keepdims=True) + EPS) * g_ref[...]
            outs.append(jnp.where(row < n_cmp, out, 0.0))
        ok_ref[0] = jnp.concatenate(outs, axis=-1).astype(BF16)

    @pl.when(jnp.logical_not(is_key))
    def _():
        col = lax.broadcasted_iota(jnp.int32, (NSA_DH, nch), 1)
        outs = []
        for grp in range(NSA_GROUPS):
            out = _dot_nt(w2t_ref[0], hids[grp]) + b2t_ref[0]
            outs.append(jnp.where(col < n_cmp, out, 0.0))
        ovt_ref[0] = jnp.concatenate(outs, axis=0).astype(BF16)


def _compress(xflat, pos, w1, b1, w2, b2, w2t, b2t, kc_g, n_cmp):
    b, _, _, nch, flat = xflat.shape
    return pl.pallas_call(
        functools.partial(_compress_body, n_cmp=n_cmp),
        grid=(b, 2),
        in_specs=[
            pl.BlockSpec((1, 1, NSA_GROUPS, nch, flat), lambda i, j: (i, j, 0, 0, 0)),
            pl.BlockSpec((1, 2, 1, flat), lambda i, j: (j, 0, 0, 0)),
            pl.BlockSpec((1, 2, flat, CMP_HIDDEN), lambda i, j: (j, 0, 0, 0)),
            pl.BlockSpec((1, 1, CMP_HIDDEN), lambda i, j: (j, 0, 0)),
            pl.BlockSpec((1, CMP_HIDDEN, NSA_DH), lambda i, j: (j, 0, 0)),
            pl.BlockSpec((1, 1, NSA_DH), lambda i, j: (j, 0, 0)),
            pl.BlockSpec((1, NSA_DH, CMP_HIDDEN), lambda i, j: (j, 0, 0)),
            pl.BlockSpec((1, NSA_DH, 1), lambda i, j: (j, 0, 0)),
            pl.BlockSpec((1, NSA_DH), lambda i, j: (0, 0)),
        ],
        out_specs=[pl.BlockSpec((1, nch, LANES), lambda i, j: (i, 0, 0)),
                   pl.BlockSpec((1, LANES, nch), lambda i, j: (i, 0, 0))],
        out_shape=[jax.ShapeDtypeStruct((b, nch, LANES), BF16), jax.ShapeDtypeStruct((b, LANES, nch), BF16)],
        compiler_params=_cparams(("parallel", "arbitrary"), 32),
        name="nsa_compress",
    )(xflat, pos, w1, b1, w2, b2, w2t, b2t, kc_g)


def _tile_heads(a):
    return jnp.concatenate([a] * NSA_HPG, axis=1)


def _masked_softmax_t(s, valid, any_valid=None):
    s = s + _tile_heads(jnp.where(valid, 0.0, NEG))
    p = jnp.exp(s - jnp.max(s, axis=0, keepdims=True))
    inv = 1.0 / jnp.maximum(jnp.sum(p, axis=0, keepdims=True), 1e-30)
    if any_valid is not None:
        inv = jnp.where(_tile_heads(any_valid), inv, 0.0)
    return p * inv


def _nsa_body(qt_ref, gt_ref, kc_ref, vct_ref, ks_ref, kw_ref, vt_ref, et_ref, ov_ref, o_ref, acc_sc,
              *, n_cmp, n_top, kt):
    grp = pl.program_id(1)
    s0 = pl.program_id(2) * Q_BLOCK
    nch = kc_ref.shape[1]
    nblk = ov_ref.shape[0]

    qt = jnp.concatenate([qt_ref[h * LANES:(h + 1) * LANES, :] for h in range(NSA_HPG)], axis=1)
    t_row = s0 + lax.broadcasted_iota(jnp.int32, (1, Q_BLOCK), 1)

    cidx = lax.broadcasted_iota(jnp.int32, (nch, 1), 0)
    cvalid = (cidx * CMP_STRIDE + (CMP_BLOCK - 1) <= t_row) & (cidx < n_cmp)
    pc = _masked_softmax_t(_dot(kc_ref[0], qt), cvalid, t_row >= CMP_BLOCK - 1)
    oc = _dot(vct_ref[0], pc.astype(BF16))

    pcs = (pc[:, 0:Q_BLOCK] + pc[:, Q_BLOCK:2 * Q_BLOCK] + pc[:, 2 * Q_BLOCK:3 * Q_BLOCK]
           + pc[:, 3 * Q_BLOCK:4 * Q_BLOCK])
    pcs_hi = pcs.astype(BF16)
    pcs_lo = (pcs - pcs_hi.astype(F32)).astype(BF16)
    imp = _dot(ov_ref[...], pcs_hi) + _dot(ov_ref[...], pcs_lo)
    blk = lax.broadcasted_iota(jnp.int32, (nblk, Q_BLOCK), 0)
    cur = t_row // SLC_BLOCK
    imp = jnp.where(blk * SLC_BLOCK > t_row, NEG, imp)
    imp = jnp.where((blk == 0) | (blk == cur) | (blk == cur - 1), -NEG, imp)

    def pick(_, carry):
        v, sel = carry
        mx = jnp.max(v, axis=0, keepdims=True)
        first = jnp.min(jnp.where(v == mx, blk, nblk), axis=0, keepdims=True)
        hit = blk == first
        return jnp.where(hit, -jnp.inf, v), jnp.where(hit, 1.0, sel)

    _, sel = lax.fori_loop(0, n_top, pick, (imp, jnp.zeros((nblk, Q_BLOCK), F32)), unroll=True)
    sel = sel.astype(BF16)

    acc_sc[...] = jnp.zeros(acc_sc.shape, F32)

    def tile(j, carry):
        m_old, l_old = carry
        k0 = pl.multiple_of(j * kt, kt)
        kpos = k0 + lax.broadcasted_iota(jnp.int32, (kt, 1), 0)
        picked = _dot(et_ref[pl.ds(k0, kt), :], sel)
        bias = jnp.where((picked > 0.5) & (kpos <= t_row), 0.0, NEG)
        s = _dot(ks_ref[0, pl.ds(k0, kt), :], qt) + _tile_heads(bias)
        m_new = jnp.maximum(m_old, jnp.max(s, axis=0, keepdims=True))
        alpha = jnp.exp(m_old - m_new)
        p = jnp.exp(s - m_new)
        l_new = alpha * l_old + jnp.sum(p, axis=0, keepdims=True)
        acc_sc[...] = alpha * acc_sc[...] + _dot(vt_ref[0:LANES, pl.ds(k0, kt)], p.astype(BF16))
        return m_new, l_new

    hq = NSA_HPG * Q_BLOCK
    _, l_fin = lax.fori_loop(0, (s0 + Q_BLOCK + kt - 1) // kt, tile,
                             (jnp.full((1, hq), NEG, F32), jnp.zeros((1, hq), F32)))
    osl = acc_sc[...] / jnp.maximum(l_fin, 1e-30)

    wlen = WINDOW + Q_BLOCK
    w0 = pl.multiple_of(jnp.maximum(s0 - WINDOW, 0), Q_BLOCK)
    kpos = w0 + lax.broadcasted_iota(jnp.int32, (wlen, 1), 0)
    wvalid = (kpos <= t_row) & (kpos > t_row - WINDOW)
    pw = _masked_softmax_t(_dot(kw_ref[0, pl.ds(w0, wlen), :], qt), wvalid)
    ow = _dot(vt_ref[LANES:2 * LANES, pl.ds(w0, wlen)], pw.astype(BF16))

    gts = jax.nn.sigmoid(gt_ref[...])
    for h in range(NSA_HPG):
        cols = slice(h * Q_BLOCK, (h + 1) * Q_BLOCK)
        mix = (gts[3 * h:3 * h + 1, :] * oc[:, cols] + gts[3 * h + 1:3 * h + 2, :] * osl[:, cols]
               + gts[3 * h + 2:3 * h + 3, :] * ow[:, cols])
        o_ref[h * NSA_DH:(h + 1) * NSA_DH, :] = jnp.where(grp == 0, mix[:NSA_DH, :], mix[NSA_DH:, :])


def _nsa_attention(qt, gt, kc, vct, okv, vt, expand_t, overlap, b, t, n_cmp, n_top, kt):
    nch = kc.shape[1]
    nq = t // Q_BLOCK
    n = b * t
    return pl.pallas_call(
        functools.partial(_nsa_body, n_cmp=n_cmp, n_top=n_top, kt=kt),
        grid=(b, NSA_GROUPS, nq),
        in_specs=[
            pl.BlockSpec((NSA_HPG * LANES, Q_BLOCK), lambda bi, g, i: (g, bi * nq + i)),
            pl.BlockSpec((GATE_ROWS, Q_BLOCK), lambda bi, g, i: (g, bi * nq + i)),
            pl.BlockSpec((1, nch, LANES), lambda bi, g, i: (bi, 0, 0)),
            pl.BlockSpec((1, LANES, nch), lambda bi, g, i: (bi, 0, 0)),
            pl.BlockSpec((1, t, LANES), lambda bi, g, i: (bi, 0, 2)),
            pl.BlockSpec((1, t, LANES), lambda bi, g, i: (bi, 0, 4)),
            pl.BlockSpec((2 * LANES, t), lambda bi, g, i: (0, bi)),
            pl.BlockSpec(expand_t.shape, lambda bi, g, i: (0, 0)),
            pl.BlockSpec(overlap.shape, lambda bi, g, i: (0, 0)),
        ],
        out_specs=pl.BlockSpec((NSA_HPG * NSA_DH, Q_BLOCK), lambda bi, g, i: (g, bi * nq + i)),
        out_shape=jax.ShapeDtypeStruct((NSA_W, n), F32),
        scratch_shapes=[pltpu.VMEM((LANES, NSA_HPG * Q_BLOCK), F32)],
        compiler_params=_cparams(("parallel", "parallel", "arbitrary"), 56),
        name="nsa_attention",
    )(qt, gt, kc, vct, okv, okv, vt, expand_t, overlap)


def _split_bf16(a):
    hi = a.astype(BF16)
    return hi, (a - hi.astype(F32)).astype(BF16)


def _split_lhs(a):
    hi, lo = _split_bf16(a)
    return jnp.concatenate([hi, hi, lo], axis=1)


def _split_rhs(b):
    hi, lo = _split_bf16(b)
    return jnp.concatenate([hi, lo, hi], axis=0)


def _unit_lower_inverses(lmats):
    c = lmats[0].shape[0]
    r = lax.broadcasted_iota(jnp.int32, (c, c), 0)
    col = lax.broadcasted_iota(jnp.int32, (c, c), 1)
    eye = jnp.where(r == col, 1.0, 0.0)
    xs = [eye - l for l in lmats]
    ps = [_dot(_split_lhs(l), _split_rhs(l)) for l in lmats]
    steps = int(np.log2(c)) - 1
    for s in range(steps):
        last = s + 1 == steps
        for i in range(len(lmats)):
            rhs = _split_rhs(ps[i])
            if last:
                xs[i] = xs[i] + _dot(_split_lhs(xs[i]), rhs)
            else:
                both = _dot(_split_lhs(jnp.concatenate([xs[i], ps[i]], axis=0)), rhs)
                xs[i] = xs[i] + both[:c]
                ps[i] = both[c:]
    return xs


def _gdn_body(x_ref, z_ref, ab_ref, cw_ref, alog_ref, dtb_ref, og_ref, o_ref, xbuf, s_sc, *, ct):
    c = pl.program_id(1)
    halo = 8

    @pl.when(c == 0)
    def _():
        xbuf[0:halo, :] = jnp.zeros((halo, xbuf.shape[1]), F32)
        s_sc[...] = jnp.zeros(s_sc.shape, F32)

    @pl.when(c > 0)
    def _():
        xbuf[0:halo, :] = xbuf[ct:ct + halo, :]

    xbuf[halo:halo + ct, :] = x_ref[0].astype(F32)
    y = cw_ref[0:1, :] * xbuf[pl.ds(halo - 3, ct), :]
    for k in range(1, GDN_CONV):
        y = y + cw_ref[k:k + 1, :] * xbuf[pl.ds(halo - 3 + k, ct), :]
    y = y * jax.nn.sigmoid(y)

    ab = ab_ref[0]
    g_all = -jnp.exp(alog_ref[...]) * jax.nn.softplus(ab + dtb_ref[...])
    beta_all = jax.nn.sigmoid(ab)

    ch = GDN_CHUNK
    r = lax.broadcasted_iota(jnp.int32, (ch, ch), 0)
    col = lax.broadcasted_iota(jnp.int32, (ch, ch), 1)
    incl = r >= col
    strict = r > col
    tril = jnp.where(incl, 1.0, 0.0)

    tril16 = jnp.concatenate([tril.astype(BF16)] * 3, axis=1)
    units = []
    for ci in range(ct // ch):
        rows = slice(ci * ch, (ci + 1) * ch)
        g_hi, g_lo = _split_bf16(g_all[rows])
        g_lo2 = (g_all[rows] - g_hi.astype(F32) - g_lo.astype(F32)).astype(BF16)
        gc_all = _dot(tril16, jnp.concatenate([g_hi, g_lo, g_lo2], axis=0))
        gc_t = gc_all.T
        for h in range(GDN_HEADS):
            hs = slice(h * GDN_DH, (h + 1) * GDN_DH)
            qh = y[rows, hs]
            kh = y[rows, GDN_W + h * GDN_DH:GDN_W + (h + 1) * GDN_DH]
            vh = y[rows, 2 * GDN_W + h * GDN_DH:2 * GDN_W + (h + 1) * GDN_DH]
            qh = qh * lax.rsqrt(jnp.sum(qh * qh, axis=-1, keepdims=True) + EPS) * (GDN_DH ** -0.5)
            kh = kh * lax.rsqrt(jnp.sum(kh * kh, axis=-1, keepdims=True) + EPS)
            gc = gc_all[:, h:h + 1]
            gr = gc_t[h:h + 1, :]
            g_last = gc_all[ch - 1:ch, h:h + 1]
            beta = beta_all[rows, GDN_HEADS + h:GDN_HEADS + h + 1]
            eg = jnp.exp(gc)
            decay = jnp.where(incl, jnp.exp(jnp.minimum(gc - gr, 0.0)), 0.0)
            kb = kh * beta
            k16 = kh.astype(BF16)
            units.append(dict(
                rows=rows, h=h,
                lmat=jnp.where(strict, _dot_nt(kb.astype(BF16), k16) * decay, 0.0),
                vb=(vh * beta).astype(BF16), kbg=(kb * eg).astype(BF16),
                qk=jnp.where(incl, _dot_nt(qh.astype(BF16), k16) * decay, 0.0).astype(BF16),
                qg=(qh * eg).astype(BF16), kd=(kh * jnp.exp(g_last - gc)).astype(BF16),
                gl=jnp.exp(g_last)))
    tinvs = _unit_lower_inverses([u["lmat"] for u in units])
    for u, tinv in zip(units, tinvs):
        tinv = tinv.astype(BF16)
        u["u"] = _dot(tinv, u["vb"])
        u["w"] = _dot(tinv, u["kbg"]).astype(BF16)

    for u in units:
        h, rows = u["h"], u["rows"]
        hs = slice(h * GDN_DH, (h + 1) * GDN_DH)
        s_old = s_sc[h]
        s16 = s_old.astype(BF16)
        v_new = (u["u"] - _dot(u["w"], s16)).astype(BF16)
        o = _dot(u["qg"], s16) + _dot(u["qk"], v_new)
        s_sc[h] = s_old * u["gl"] + _dot_tn(u["kd"], v_new)
        on = o * lax.rsqrt(jnp.mean(o * o, axis=-1, keepdims=True) + EPS) * og_ref[...]
        zh = z_ref[0, rows, hs].astype(F32)
        o_ref[0, rows, hs] = (on * (zh * jax.nn.sigmoid(zh))).astype(BF16)


def _gdn(oqkv, oz, oab, conv_w, alog, dtb, og, ct=128):
    b, t, w3 = oqkv.shape
    full = lambda a: pl.BlockSpec(a.shape, lambda bi, c: (0,) * a.ndim)
    return pl.pallas_call(
        functools.partial(_gdn_body, ct=ct),
        grid=(b, t // ct),
        in_specs=[
            pl.BlockSpec((1, ct, w3), lambda bi, c: (bi, c, 0)),
            pl.BlockSpec((1, ct, GDN_W), lambda bi, c: (bi, c, 0)),
            pl.BlockSpec((1, ct, LANES), lambda bi, c: (bi, c, 0)),
            full(conv_w), full(alog), full(dtb), full(og),
        ],
        out_specs=pl.BlockSpec((1, ct, GDN_W), lambda bi, c: (bi, c, 0)),
        out_shape=jax.ShapeDtypeStruct((b, t, GDN_W), BF16),
        scratch_shapes=[pltpu.VMEM((ct + 8, w3), F32), pltpu.VMEM((GDN_HEADS, GDN_DH, GDN_DH), F32)],
        compiler_params=_cparams(("parallel", "arbitrary"), 32),
        name="gdn",
    )(oqkv, oz, oab, conv_w, alog, dtb, og)


def _outproj_body(ont_ref, og_ref, x_ref, ng_ref, wo_ref, fg_ref, wr_ref, br_ref,
                  x1_ref, h2_ref, idx_ref, gate_ref, rank_ref, cnt_ref, cnt_sc):
    i = pl.program_id(0)
    tm = x_ref.shape[0]

    @pl.when(i == 0)
    def _():
        cnt_sc[...] = jnp.zeros(cnt_sc.shape, F32)

    a = ont_ref[...]
    a = (a * lax.rsqrt(jnp.mean(a * a, axis=0, keepdims=True) + EPS) * ng_ref[...]).astype(BF16)
    x1 = x_ref[...] + _dot_tn(a, wo_ref[0:NSA_W, :]) + _dot(og_ref[...], wo_ref[NSA_W:, :])
    x1_ref[...] = x1
    h2 = (x1 * lax.rsqrt(jnp.mean(x1 * x1, axis=-1, keepdims=True) + EPS) * fg_ref[...]).astype(BF16)
    h2_ref[...] = h2

    lane = lax.broadcasted_iota(jnp.int32, (tm, LANES), 1)
    logits = jnp.where(lane < N_EXPERTS, _dot(h2, wr_ref[...]) + br_ref[...], -jnp.inf)
    onehot = jnp.zeros((tm, LANES), F32)
    idx = jnp.zeros((tm, LANES), jnp.int32)
    vals = jnp.zeros((tm, LANES), F32)
    firsts = []
    v = logits
    top0 = None
    for k in range(TOP_K):
        mx = jnp.max(v, axis=-1, keepdims=True)
        first = jnp.min(jnp.where(v == mx, lane, LANES), axis=-1, keepdims=True)
        hit = lane == first
        v = jnp.where(hit, -jnp.inf, v)
        onehot = jnp.where(hit, 1.0, onehot)
        idx = jnp.where(lane == k, first, idx)
        top0 = mx if k == 0 else top0
        vals = jnp.where(lane == k, jnp.exp(mx - top0), vals)
        firsts.append(first)
    idx_ref[...] = idx
    gate_ref[...] = vals / jnp.sum(vals, axis=-1, keepdims=True)

    r = lax.broadcasted_iota(jnp.int32, (tm, tm), 0)
    c = lax.broadcasted_iota(jnp.int32, (tm, tm), 1)
    before = jnp.where(r > c, 1.0, 0.0).astype(BF16)
    excl = cnt_sc[...] + _dot(before, onehot.astype(BF16))
    rank = jnp.zeros((tm, LANES), F32)
    for k in range(TOP_K):
        rk = jnp.sum(jnp.where(lane == firsts[k], excl, 0.0), axis=-1, keepdims=True)
        rank = jnp.where(lane == k, rk, rank)
    rank_ref[...] = rank.astype(jnp.int32)
    cnt_sc[...] = cnt_sc[...] + jnp.sum(onehot, axis=0, keepdims=True)
    cnt_ref[...] = cnt_sc[...].astype(jnp.int32)


def _out_proj(o_nsa_t, o_gdn, x2, ng, wo, fg, wr, br, tm=512):
    n, d = x2.shape
    full = lambda a: pl.BlockSpec(a.shape, lambda i: (0,) * a.ndim)
    row = lambda w: pl.BlockSpec((tm, w), lambda i: (i, 0))
    return pl.pallas_call(
        _outproj_body,
        grid=(n // tm,),
        in_specs=[pl.BlockSpec((NSA_W, tm), lambda i: (0, i)), row(GDN_W), row(d), full(ng), full(wo), full(fg),
                  full(wr), full(br)],
        out_specs=[row(d), row(d), row(LANES), row(LANES), row(LANES),
                   pl.BlockSpec((1, LANES), lambda i: (0, 0))],
        out_shape=[jax.ShapeDtypeStruct((n, d), F32), jax.ShapeDtypeStruct((n, d), BF16),
                   jax.ShapeDtypeStruct((n, LANES), jnp.int32), jax.ShapeDtypeStruct((n, LANES), F32),
                   jax.ShapeDtypeStruct((n, LANES), jnp.int32), jax.ShapeDtypeStruct((1, LANES), jnp.int32)],
        scratch_shapes=[pltpu.VMEM((1, LANES), F32)],
        compiler_params=_cparams(("arbitrary",), 48),
        name="out_proj_router",
    )(o_nsa_t, o_gdn, x2, ng, wo, fg, wr, br)


def _expert_body(be_ref, xs_ref, wg_ref, bg_ref, wu_ref, bu_ref, wd_ref, bd_ref, y_ref, wg16, wu16, wd16):
    i = pl.program_id(0)
    prev = be_ref[jnp.maximum(i - 1, 0)]
    fresh = (i == 0) | (be_ref[i] != prev)

    @pl.when(fresh)
    def _():
        wg16[...] = wg_ref[0].astype(BF16)
        wu16[...] = wu_ref[0].astype(BF16)
        wd16[...] = wd_ref[0].astype(BF16)

    x = xs_ref[...]
    gate = jnp.minimum(_dot(x, wg16[...]) + bg_ref[0], SWIGLU_LIMIT)
    up = jnp.clip(_dot(x, wu16[...]) + bu_ref[0], -SWIGLU_LIMIT, SWIGLU_LIMIT)
    glu = gate * jax.nn.sigmoid(gate * SWIGLU_ALPHA)
    y_ref[...] = _dot(((up + 1.0) * glu).astype(BF16), wd16[...]) + bd_ref[0]


def _experts(blk_e, xs, wg, bg, wu, bu, wd, bd):
    n_rows, d = xs.shape
    de = wg.shape[2]
    r = MOE_ROW_BLOCK
    wspec = lambda a, b: pl.BlockSpec((1, a, b), lambda i, be: (be[i], 0, 0))
    grid_spec = pltpu.PrefetchScalarGridSpec(
        num_scalar_prefetch=1,
        grid=(n_rows // r,),
        in_specs=[pl.BlockSpec((r, d), lambda i, be: (i, 0)),
                  wspec(d, de), wspec(1, de), wspec(d, de), wspec(1, de), wspec(de, d), wspec(1, d)],
        out_specs=pl.BlockSpec((r, d), lambda i, be: (i, 0)),
        scratch_shapes=[pltpu.VMEM((d, de), BF16), pltpu.VMEM((d, de), BF16), pltpu.VMEM((de, d), BF16)],
    )
    return pl.pallas_call(
        _expert_body,
        grid_spec=grid_spec,
        out_shape=jax.ShapeDtypeStruct((n_rows, d), F32),
        compiler_params=_cparams(("arbitrary",), 56),
        name="moe_experts",
    )(blk_e, xs, wg, bg, wu, bu, wd, bd)


def _pad_lanes(a, width=LANES):
    return jnp.pad(a, ((0, 0), (0, width - a.shape[1])))


def _layer(x, attn_norm_g, w_in, q_g, kc_g, ks_g, kw_g, ck_pos, ck_w1, ck_b1, ck_w2, ck_b2,
           cv_pos, cv_w1, cv_b1, cv_w2, cv_b2, nsa_out_g, conv_w, a_log, dt_bias, gdn_out_g, w_out,
           ffn_g, router_w, router_b, e_wg, e_bg, e_wu, e_bu, e_wd, e_bd):
    b, t, d = x.shape
    n = b * t
    x2 = x.reshape(n, d)

    o = np.cumsum([0, NSA_W] + [NSA_GROUPS * NSA_DH] * 6 + [3 * NSA_HEADS, 3 * GDN_W, GDN_W, GDN_HEADS, GDN_HEADS])
    wq_t = w_in[:, o[0]:o[1]].T.reshape(NSA_GROUPS, NSA_HPG, NSA_DH, d)
    zq = jnp.zeros((NSA_HPG, NSA_DH, d), F32)
    wq_t = jnp.stack([jnp.concatenate([wq_t[0], zq], axis=1), jnp.concatenate([zq, wq_t[1]], axis=1)])
    wq_t = wq_t.reshape(NSA_HEADS * LANES, d).astype(BF16)
    qg1 = q_g * (NSA_DH ** -0.5)
    zg = jnp.zeros((NSA_DH,), F32)
    qg_col = jnp.concatenate([jnp.tile(jnp.concatenate([qg1, zg]), NSA_HPG),
                              jnp.tile(jnp.concatenate([zg, qg1]), NSA_HPG)]).reshape(NSA_HEADS * LANES, 1)
    wkv = w_in[:, o[1]:o[7]].astype(BF16)
    ones = jnp.ones((LANES,), F32)
    kg = jnp.concatenate([ones, ones, ks_g, ks_g, ones, kw_g, kw_g, ones]).reshape(1, 6 * LANES)
    wv_t = jnp.concatenate([w_in[:, o[4]:o[5]], w_in[:, o[6]:o[7]]], axis=1).T.astype(BF16)
    wg_t = w_in[:, o[7]:o[8]].T.reshape(NSA_GROUPS, NSA_HPG * 3, d)
    wg_t = jnp.pad(wg_t, ((0, 0), (0, GATE_ROWS - NSA_HPG * 3), (0, 0))).reshape(NSA_GROUPS * GATE_ROWS, d)
    wg_t = wg_t.astype(BF16)
    wab = _pad_lanes(w_in[:, o[10]:o[12]]).astype(BF16)
    wqkv = w_in[:, o[8]:o[9]].astype(BF16)
    wz = w_in[:, o[9]:o[10]].astype(BF16)

    oqt, okv, ovt, ogt, oqkv, oz, oab = _in_proj(x2, attn_norm_g.reshape(1, d), wq_t, wkv, wv_t, wg_t, wqkv, wz,
                                                 wab, qg_col, kg)

    nch = t // CMP_STRIDE
    n_cmp = (t - CMP_BLOCK) // CMP_STRIDE + 1
    half = CMP_STRIDE * NSA_DH
    xflat = okv[:, :2 * LANES].reshape(b, nch, CMP_STRIDE, 2, NSA_GROUPS, NSA_DH)
    xflat = xflat.transpose(0, 3, 4, 1, 2, 5).reshape(b, 2, NSA_GROUPS, nch, half)
    pos = jnp.stack([ck_pos, cv_pos]).reshape(2, 2, 1, half)
    w1 = jnp.stack([ck_w1, cv_w1]).reshape(2, 2, half, CMP_HIDDEN).astype(BF16)
    b1 = jnp.stack([ck_b1, cv_b1]).reshape(2, 1, CMP_HIDDEN)
    w2 = jnp.stack([ck_w2, cv_w2]).astype(BF16)
    b2 = jnp.stack([ck_b2, cv_b2]).reshape(2, 1, NSA_DH)
    w2t = jnp.stack([ck_w2.T, cv_w2.T]).astype(BF16)
    b2t = jnp.stack([ck_b2, cv_b2]).reshape(2, NSA_DH, 1)
    kc, vct = _compress(xflat, pos, w1, b1, w2, b2, w2t, b2t, kc_g.reshape(1, NSA_DH), n_cmp)

    n_slc = t // SLC_BLOCK
    n_top = min(SLC_TOPK, n_slc)
    nblk = max(n_slc, LANES)
    kt = min(512, t)
    ci = np.arange(nch)[None, :] * CMP_STRIDE
    sj = np.arange(nblk)[:, None] * SLC_BLOCK
    overlap = ((ci < sj + SLC_BLOCK) & (ci + CMP_BLOCK > sj) & (np.arange(nch)[None, :] < n_cmp)
               & (np.arange(nblk)[:, None] < n_slc))
    expand_t = (np.arange(t)[:, None] // SLC_BLOCK) == np.arange(nblk)[None, :]
    o_nsa_t = _nsa_attention(oqt, ogt, kc, vct, okv.reshape(b, t, -1), ovt, jnp.asarray(expand_t, BF16),
                             jnp.asarray(overlap, BF16), b, t, n_cmp, n_top, kt)

    alog_row = _pad_lanes(a_log.reshape(1, GDN_HEADS))
    dtb_row = _pad_lanes(dt_bias.reshape(1, GDN_HEADS))
    o_gdn = _gdn(oqkv.reshape(b, t, -1), oz.reshape(b, t, -1), oab.reshape(b, t, -1), conv_w,
                 alog_row, dtb_row, gdn_out_g.reshape(1, GDN_DH))

    wr = _pad_lanes(router_w).astype(BF16)
    br = _pad_lanes(router_b.reshape(1, N_EXPERTS))
    x1, h2, idx, gates, rank, counts = _out_proj(
        o_nsa_t, o_gdn.reshape(n, GDN_W), x2, nsa_out_g.reshape(NSA_W, 1),
        w_out.astype(BF16), ffn_g.reshape(1, d), wr, br)

    r = MOE_ROW_BLOCK
    nk = n * TOP_K
    counts = counts[0, :N_EXPERTS]
    pcounts = (counts + r - 1) // r * r
    pends = jnp.cumsum(pcounts)
    pstarts = pends - pcounts
    top_idx = idx[:, :TOP_K]
    dest = pstarts[top_idx] + rank[:, :TOP_K]
    n_rows = (nk + r - 1) // r * r + N_EXPERTS * r
    n_blocks = n_rows // r
    tok = jnp.broadcast_to(jnp.arange(n, dtype=jnp.int32)[:, None], (n, TOP_K))
    row_tok = jnp.zeros((n_rows,), jnp.int32).at[dest.reshape(-1)].set(tok.reshape(-1))
    blk_e = jnp.minimum(jnp.searchsorted(pends, jnp.arange(n_blocks) * r, side='right'),
                        N_EXPERTS - 1).astype(jnp.int32)
    xs = h2[row_tok]
    ys = _experts(blk_e, xs, e_wg, e_bg.reshape(N_EXPERTS, 1, -1), e_wu, e_bu.reshape(N_EXPERTS, 1, -1),
                  e_wd, e_bd.reshape(N_EXPERTS, 1, -1))
    y = jnp.sum(ys[dest] * gates[:, :TOP_K, None], axis=1)
    return (x1 + y).reshape(b, t, d)


def kernel(x, attn_norm_g, w_in, nsa_q_norm_g, nsa_kc_norm_g, nsa_ks_norm_g, nsa_kw_norm_g, cmp_k_pos, cmp_k_w1, cmp_k_b1, cmp_k_w2, cmp_k_b2, cmp_v_pos, cmp_v_w1, cmp_v_b1, cmp_v_w2, cmp_v_b2, nsa_out_norm_g, gdn_conv_w, gdn_a_log, gdn_dt_bias, gdn_out_norm_g, w_out, ffn_norm_g, router_w, router_b, exp_w_gate, exp_b_gate, exp_w_up, exp_b_up, exp_w_down, exp_b_down):
    params = (attn_norm_g, w_in, nsa_q_norm_g, nsa_kc_norm_g, nsa_ks_norm_g, nsa_kw_norm_g,
              cmp_k_pos, cmp_k_w1, cmp_k_b1, cmp_k_w2, cmp_k_b2, cmp_v_pos, cmp_v_w1, cmp_v_b1, cmp_v_w2, cmp_v_b2,
              nsa_out_norm_g, gdn_conv_w, gdn_a_log, gdn_dt_bias, gdn_out_norm_g, w_out, ffn_norm_g,
              router_w, router_b, exp_w_gate, exp_b_gate, exp_w_up, exp_b_up, exp_w_down, exp_b_down)
    for l in range(attn_norm_g.shape[0]):
        x = _layer(x, *(p[l] for p in params))
    return x
```

```python
import functools

import jax
import jax.numpy as jnp
import numpy as np
from jax import lax
from jax.experimental import pallas as pl
from jax.experimental.pallas import tpu as pltpu
from jax.experimental.pallas import tpu_sc as plsc

F32 = jnp.float32
BF16 = jnp.bfloat16

EPS = 1e-6
NEG = -1e30

NSA_HEADS = 8
NSA_GROUPS = 2
NSA_HPG = 4
NSA_DH = 64
CMP_BLOCK = 32
CMP_STRIDE = 16
CMP_HIDDEN = 256
SLC_BLOCK = 64
SLC_TOPK = 16
WINDOW = 512
Q_BLOCK = 128
GDN_HEADS = 4
GDN_DH = 128
GDN_CONV = 4
GDN_CHUNK = 64
N_EXPERTS = 32
TOP_K = 4
SWIGLU_LIMIT = 7.0
SWIGLU_ALPHA = 1.702
MOE_ROW_BLOCK = 256

LANES = 128
GATE_ROWS = 16
NSA_W = NSA_HEADS * NSA_DH
GDN_W = GDN_HEADS * GDN_DH

_NT = (((1,), (1,)), ((), ()))
_TN = (((0,), (0,)), ((), ()))


def _cparams(sem, vmem_mb):
    return pltpu.CompilerParams(dimension_semantics=sem, vmem_limit_bytes=vmem_mb * 1024 * 1024)


def _dot(a, b):
    return jnp.dot(a, b, preferred_element_type=F32)


def _dot_nt(a, b):
    return lax.dot_general(a, b, _NT, preferred_element_type=F32)


def _dot_tn(a, b):
    return lax.dot_general(a, b, _TN, preferred_element_type=F32)


def _dot_hi(a, b):
    return jnp.dot(a, b, preferred_element_type=F32, precision=lax.Precision.HIGHEST)


def _inproj_body(x_ref, g_ref, wqt_ref, wkv_ref, wvt_ref, wgt_ref, wqkv_ref, wz_ref, wab_ref, qg_ref, kg_ref,
                 oqt_ref, okv_ref, ovt_ref, ogt_ref, oqkv_ref, oz_ref, oab_ref):
    x = x_ref[...]
    h = (x * lax.rsqrt(jnp.mean(x * x, axis=-1, keepdims=True) + EPS) * g_ref[...]).astype(BF16)
    tm = x.shape[0]

    yq = _dot_nt(wqt_ref[...], h)
    for s in range(NSA_HEADS):
        sl = slice(s * LANES, (s + 1) * LANES)
        ys = yq[sl, :]
        ms = jnp.sum(ys * ys, axis=0, keepdims=True) * (1.0 / NSA_DH)
        oqt_ref[sl, :] = (ys * lax.rsqrt(ms + EPS) * qg_ref[sl, :]).astype(BF16)

    ykv = _dot(h, wkv_ref[...])
    lane = lax.broadcasted_iota(jnp.int32, (tm, LANES), 1)
    low = lane < NSA_DH
    for s in range(6):
        sl = slice(s * LANES, (s + 1) * LANES)
        ys = ykv[:, sl]
        if s in (2, 4):
            y2 = ys * ys
            s0 = jnp.sum(jnp.where(low, y2, 0.0), axis=-1, keepdims=True)
            s1 = jnp.sum(jnp.where(low, 0.0, y2), axis=-1, keepdims=True)
            ms = jnp.where(low, s0, s1) * (1.0 / NSA_DH)
            ys = ys * lax.rsqrt(ms + EPS) * kg_ref[:, sl]
        okv_ref[:, sl] = ys.astype(BF16)

    ovt_ref[...] = _dot_nt(wvt_ref[...], h).astype(BF16)
    ogt_ref[...] = _dot_nt(wgt_ref[...], h)
    oqkv_ref[...] = _dot(h, wqkv_ref[...]).astype(BF16)
    oz_ref[...] = _dot(h, wz_ref[...]).astype(BF16)
    oab_ref[...] = _dot(h, wab_ref[...])


def _in_proj(x2, g, wqt, wkv, wvt, wgt, wqkv, wz, wab, qg, kg, tm=512):
    n, d = x2.shape
    full = lambda a: pl.BlockSpec(a.shape, lambda i: (0,) * a.ndim)
    row = lambda w: pl.BlockSpec((tm, w), lambda i: (i, 0))
    colb = lambda r: pl.BlockSpec((r, tm), lambda i: (0, i))
    return pl.pallas_call(
        _inproj_body,
        grid=(n // tm,),
        in_specs=[row(d)] + [full(a) for a in (g, wqt, wkv, wvt, wgt, wqkv, wz, wab, qg, kg)],
        out_specs=[colb(wqt.shape[0]), row(wkv.shape[1]), colb(wvt.shape[0]), colb(wgt.shape[0]),
                   row(wqkv.shape[1]), row(wz.shape[1]), row(wab.shape[1])],
        out_shape=[jax.ShapeDtypeStruct((wqt.shape[0], n), BF16), jax.ShapeDtypeStruct((n, wkv.shape[1]), BF16),
                   jax.ShapeDtypeStruct((wvt.shape[0], n), BF16), jax.ShapeDtypeStruct((wgt.shape[0], n), F32),
                   jax.ShapeDtypeStruct((n, wqkv.shape[1]), BF16), jax.ShapeDtypeStruct((n, wz.shape[1]), BF16),
                   jax.ShapeDtypeStruct((n, wab.shape[1]), F32)],
        compiler_params=_cparams(("parallel",), 56),
        name="in_proj",
    )(x2, g, wqt, wkv, wvt, wgt, wqkv, wz, wab, qg, kg)


def _compress_body(x_ref, pos_ref, w1_ref, b1_ref, w2_ref, b2_ref, w2t_ref, b2t_ref, g_ref, ok_ref, ovt_ref,
                   *, n_cmp):
    is_key = pl.program_id(1) == 0
    nch = x_ref.shape[3]
    hids = []
    for grp in range(NSA_GROUPS):
        x = x_ref[0, 0, grp].astype(F32)
        xa = (x + pos_ref[0, 0]).astype(BF16)
        xb = (x + pos_ref[0, 1]).astype(BF16)
        a = _dot(xa, w1_ref[0, 0])
        b = _dot(xb, w1_ref[0, 1])
        b_next = pltpu.roll(b, nch - 1, 0)
        hids.append(jax.nn.gelu(a + b_next + b1_ref[0]).astype(BF16))

    @pl.when(is_key)
    def _():
        row = lax.broadcasted_iota(jnp.int32, (nch, NSA_DH), 0)
        outs = []
        for grp in range(NSA_GROUPS):
            out = _dot(hids[grp], w2_ref[0]) + b2_ref[0]
            out = out * lax.rsqrt(jnp.mean(out * out, axis=-1, keepdims=True) + EPS) * g_ref[...]
            outs.append(jnp.where(row < n_cmp, out, 0.0))
        ok_ref[0] = jnp.concatenate(outs, axis=-1).astype(BF16)

    @pl.when(jnp.logical_not(is_key))
    def _():
        col = lax.broadcasted_iota(jnp.int32, (NSA_DH, nch), 1)
        outs = []
        for grp in range(NSA_GROUPS):
            out = _dot_nt(w2t_ref[0], hids[grp]) + b2t_ref[0]
            outs.append(jnp.where(col < n_cmp, out, 0.0))
        ovt_ref[0] = jnp.concatenate(outs, axis=0).astype(BF16)


def _compress(xflat, pos, w1, b1, w2, b2, w2t, b2t, kc_g, n_cmp):
    b, _, _, nch, flat = xflat.shape
    return pl.pallas_call(
        functools.partial(_compress_body, n_cmp=n_cmp),
        grid=(b, 2),
        in_specs=[
            pl.BlockSpec((1, 1, NSA_GROUPS, nch, flat), lambda i, j: (i, j, 0, 0, 0)),
            pl.BlockSpec((1, 2, 1, flat), lambda i, j: (j, 0, 0, 0)),
            pl.BlockSpec((1, 2, flat, CMP_HIDDEN), lambda i, j: (j, 0, 0, 0)),
            pl.BlockSpec((1, 1, CMP_HIDDEN), lambda i, j: (j, 0, 0)),
            pl.BlockSpec((1, CMP_HIDDEN, NSA_DH), lambda i, j: (j, 0, 0)),
            pl.BlockSpec((1, 1, NSA_DH), lambda i, j: (j, 0, 0)),
            pl.BlockSpec((1, NSA_DH, CMP_HIDDEN), lambda i, j: (j, 0, 0)),
            pl.BlockSpec((1, NSA_DH, 1), lambda i, j: (j, 0, 0)),
            pl.BlockSpec((1, NSA_DH), lambda i, j: (0, 0)),
        ],
        out_specs=[pl.BlockSpec((1, nch, LANES), lambda i, j: (i, 0, 0)),
                   pl.BlockSpec((1, LANES, nch), lambda i, j: (i, 0, 0))],
        out_shape=[jax.ShapeDtypeStruct((b, nch, LANES), BF16), jax.ShapeDtypeStruct((b, LANES, nch), BF16)],
        compiler_params=_cparams(("parallel", "arbitrary"), 32),
        name="nsa_compress",
    )(xflat, pos, w1, b1, w2, b2, w2t, b2t, kc_g)


def _tile_heads(a):
    return jnp.concatenate([a] * NSA_HPG, axis=1)


def _masked_softmax_t(s, valid, any_valid=None):
    s = s + _tile_heads(jnp.where(valid, 0.0, NEG))
    p = jnp.exp(s - jnp.max(s, axis=0, keepdims=True))
    inv = 1.0 / jnp.maximum(jnp.sum(p, axis=0, keepdims=True), 1e-30)
    if any_valid is not None:
        inv = jnp.where(_tile_heads(any_valid), inv, 0.0)
    return p * inv


def _nsa_body(qt_ref, gt_ref, kc_ref, vct_ref, ks_ref, kw_ref, vt_ref, et_ref, ov_ref, o_ref, acc_sc,
              *, n_cmp, n_top, kt):
    grp = pl.program_id(1)
    s0 = pl.program_id(2) * Q_BLOCK
    nch = kc_ref.shape[1]
    nblk = ov_ref.shape[0]

    qt = jnp.concatenate([qt_ref[h * LANES:(h + 1) * LANES, :] for h in range(NSA_HPG)], axis=1)
    t_row = s0 + lax.broadcasted_iota(jnp.int32, (1, Q_BLOCK), 1)

    cidx = lax.broadcasted_iota(jnp.int32, (nch, 1), 0)
    cvalid = (cidx * CMP_STRIDE + (CMP_BLOCK - 1) <= t_row) & (cidx < n_cmp)
    pc = _masked_softmax_t(_dot(kc_ref[0], qt), cvalid, t_row >= CMP_BLOCK - 1)
    oc = _dot(vct_ref[0], pc.astype(BF16))

    pcs = (pc[:, 0:Q_BLOCK] + pc[:, Q_BLOCK:2 * Q_BLOCK] + pc[:, 2 * Q_BLOCK:3 * Q_BLOCK]
           + pc[:, 3 * Q_BLOCK:4 * Q_BLOCK])
    pcs_hi = pcs.astype(BF16)
    pcs_lo = (pcs - pcs_hi.astype(F32)).astype(BF16)
    imp = _dot(ov_ref[...], pcs_hi) + _dot(ov_ref[...], pcs_lo)
    blk = lax.broadcasted_iota(jnp.int32, (nblk, Q_BLOCK), 0)
    cur = t_row // SLC_BLOCK
    imp = jnp.where(blk * SLC_BLOCK > t_row, NEG, imp)
    imp = jnp.where((blk == 0) | (blk == cur) | (blk == cur - 1), -NEG, imp)

    def pick(_, carry):
        v, sel = carry
        mx = jnp.max(v, axis=0, keepdims=True)
        first = jnp.min(jnp.where(v == mx, blk, nblk), axis=0, keepdims=True)
        hit = blk == first
        return jnp.where(hit, -jnp.inf, v), jnp.where(hit, 1.0, sel)

    _, sel = lax.fori_loop(0, n_top, pick, (imp, jnp.zeros((nblk, Q_BLOCK), F32)), unroll=True)
    sel = sel.astype(BF16)

    acc_sc[...] = jnp.zeros(acc_sc.shape, F32)

    def tile(j, carry):
        m_old, l_old = carry
        k0 = pl.multiple_of(j * kt, kt)
        kpos = k0 + lax.broadcasted_iota(jnp.int32, (kt, 1), 0)
        picked = _dot(et_ref[pl.ds(k0, kt), :], sel)
        bias = jnp.where((picked > 0.5) & (kpos <= t_row), 0.0, NEG)
        s = _dot(ks_ref[0, pl.ds(k0, kt), :], qt) + _tile_heads(bias)
        m_new = jnp.maximum(m_old, jnp.max(s, axis=0, keepdims=True))
        alpha = jnp.exp(m_old - m_new)
        p = jnp.exp(s - m_new)
        l_new = alpha * l_old + jnp.sum(p, axis=0, keepdims=True)
        acc_sc[...] = alpha * acc_sc[...] + _dot(vt_ref[0:LANES, pl.ds(k0, kt)], p.astype(BF16))
        return m_new, l_new

    hq = NSA_HPG * Q_BLOCK
    _, l_fin = lax.fori_loop(0, (s0 + Q_BLOCK + kt - 1) // kt, tile,
                             (jnp.full((1, hq), NEG, F32), jnp.zeros((1, hq), F32)))
    osl = acc_sc[...] / jnp.maximum(l_fin, 1e-30)

    wlen = WINDOW + Q_BLOCK
    w0 = pl.multiple_of(jnp.maximum(s0 - WINDOW, 0), Q_BLOCK)
    kpos = w0 + lax.broadcasted_iota(jnp.int32, (wlen, 1), 0)
    wvalid = (kpos <= t_row) & (kpos > t_row - WINDOW)
    pw = _masked_softmax_t(_dot(kw_ref[0, pl.ds(w0, wlen), :], qt), wvalid)
    ow = _dot(vt_ref[LANES:2 * LANES, pl.ds(w0, wlen)], pw.astype(BF16))

    gts = jax.nn.sigmoid(gt_ref[...])
    for h in range(NSA_HPG):
        cols = slice(h * Q_BLOCK, (h + 1) * Q_BLOCK)
        mix = (gts[3 * h:3 * h + 1, :] * oc[:, cols] + gts[3 * h + 1:3 * h + 2, :] * osl[:, cols]
               + gts[3 * h + 2:3 * h + 3, :] * ow[:, cols])
        o_ref[h * NSA_DH:(h + 1) * NSA_DH, :] = jnp.where(grp == 0, mix[:NSA_DH, :], mix[NSA_DH:, :])


def _nsa_attention(qt, gt, kc, vct, okv, vt, expand_t, overlap, b, t, n_cmp, n_top, kt):
    nch = kc.shape[1]
    nq = t // Q_BLOCK
    n = b * t
    return pl.pallas_call(
        functools.partial(_nsa_body, n_cmp=n_cmp, n_top=n_top, kt=kt),
        grid=(b, NSA_GROUPS, nq),
        in_specs=[
            pl.BlockSpec((NSA_HPG * LANES, Q_BLOCK), lambda bi, g, i: (g, bi * nq + i)),
            pl.BlockSpec((GATE_ROWS, Q_BLOCK), lambda bi, g, i: (g, bi * nq + i)),
            pl.BlockSpec((1, nch, LANES), lambda bi, g, i: (bi, 0, 0)),
            pl.BlockSpec((1, LANES, nch), lambda bi, g, i: (bi, 0, 0)),
            pl.BlockSpec((1, t, LANES), lambda bi, g, i: (bi, 0, 2)),
            pl.BlockSpec((1, t, LANES), lambda bi, g, i: (bi, 0, 4)),
            pl.BlockSpec((2 * LANES, t), lambda bi, g, i: (0, bi)),
            pl.BlockSpec(expand_t.shape, lambda bi, g, i: (0, 0)),
            pl.BlockSpec(overlap.shape, lambda bi, g, i: (0, 0)),
        ],
        out_specs=pl.BlockSpec((NSA_HPG * NSA_DH, Q_BLOCK), lambda bi, g, i: (g, bi * nq + i)),
        out_shape=jax.ShapeDtypeStruct((NSA_W, n), F32),
        scratch_shapes=[pltpu.VMEM((LANES, NSA_HPG * Q_BLOCK), F32)],
        compiler_params=_cparams(("parallel", "parallel", "arbitrary"), 56),
        name="nsa_attention",
    )(qt, gt, kc, vct, okv, okv, vt, expand_t, overlap)


def _split_bf16(a):
    hi = a.astype(BF16)
    return hi, (a - hi.astype(F32)).astype(BF16)


def _split_lhs(a):
    hi, lo = _split_bf16(a)
    return jnp.concatenate([hi, hi, lo], axis=1)


def _split_rhs(b):
    hi, lo = _split_bf16(b)
    return jnp.concatenate([hi, lo, hi], axis=0)


def _unit_lower_inverses(lmats):
    c = lmats[0].shape[0]
    r = lax.broadcasted_iota(jnp.int32, (c, c), 0)
    col = lax.broadcasted_iota(jnp.int32, (c, c), 1)
    eye = jnp.where(r == col, 1.0, 0.0)
    xs = [eye - l for l in lmats]
    ps = [_dot(_split_lhs(l), _split_rhs(l)) for l in lmats]
    steps = int(np.log2(c)) - 1
    for s in range(steps):
        last = s + 1 == steps
        for i in range(len(lmats)):
            rhs = _split_rhs(ps[i])
            if last:
                xs[i] = xs[i] + _dot(_split_lhs(xs[i]), rhs)
            else:
                both = _dot(_split_lhs(jnp.concatenate([xs[i], ps[i]], axis=0)), rhs)
                xs[i] = xs[i] + both[:c]
                ps[i] = both[c:]
    return xs


def _gdn_body(x_ref, z_ref, ab_ref, cw_ref, alog_ref, dtb_ref, og_ref, o_ref, xbuf, s_sc, *, ct):
    c = pl.program_id(1)
    halo = 8

    @pl.when(c == 0)
    def _():
        xbuf[0:halo, :] = jnp.zeros((halo, xbuf.shape[1]), F32)
        s_sc[...] = jnp.zeros(s_sc.shape, F32)

    @pl.when(c > 0)
    def _():
        xbuf[0:halo, :] = xbuf[ct:ct + halo, :]

    xbuf[halo:halo + ct, :] = x_ref[0].astype(F32)
    y = cw_ref[0:1, :] * xbuf[pl.ds(halo - 3, ct), :]
    for k in range(1, GDN_CONV):
        y = y + cw_ref[k:k + 1, :] * xbuf[pl.ds(halo - 3 + k, ct), :]
    y = y * jax.nn.sigmoid(y)

    ab = ab_ref[0]
    g_all = -jnp.exp(alog_ref[...]) * jax.nn.softplus(ab + dtb_ref[...])
    beta_all = jax.nn.sigmoid(ab)

    ch = GDN_CHUNK
    r = lax.broadcasted_iota(jnp.int32, (ch, ch), 0)
    col = lax.broadcasted_iota(jnp.int32, (ch, ch), 1)
    incl = r >= col
    strict = r > col
    tril = jnp.where(incl, 1.0, 0.0)

    tril16 = jnp.concatenate([tril.astype(BF16)] * 3, axis=1)
    units = []
    for ci in range(ct // ch):
        rows = slice(ci * ch, (ci + 1) * ch)
        g_hi, g_lo = _split_bf16(g_all[rows])
        g_lo2 = (g_all[rows] - g_hi.astype(F32) - g_lo.astype(F32)).astype(BF16)
        gc_all = _dot(tril16, jnp.concatenate([g_hi, g_lo, g_lo2], axis=0))
        gc_t = gc_all.T
        for h in range(GDN_HEADS):
            hs = slice(h * GDN_DH, (h + 1) * GDN_DH)
            qh = y[rows, hs]
            kh = y[rows, GDN_W + h * GDN_DH:GDN_W + (h + 1) * GDN_DH]
            vh = y[rows, 2 * GDN_W + h * GDN_DH:2 * GDN_W + (h + 1) * GDN_DH]
            qh = qh * lax.rsqrt(jnp.sum(qh * qh, axis=-1, keepdims=True) + EPS) * (GDN_DH ** -0.5)
            kh = kh * lax.rsqrt(jnp.sum(kh * kh, axis=-1, keepdims=True) + EPS)
            gc = gc_all[:, h:h + 1]
            gr = gc_t[h:h + 1, :]
            g_last = gc_all[ch - 1:ch, h:h + 1]
            beta = beta_all[rows, GDN_HEADS + h:GDN_HEADS + h + 1]
            eg = jnp.exp(gc)
            decay = jnp.where(incl, jnp.exp(jnp.minimum(gc - gr, 0.0)), 0.0)
            kb = kh * beta
            k16 = kh.astype(BF16)
            units.append(dict(
                rows=rows, h=h,
                lmat=jnp.where(strict, _dot_nt(kb.astype(BF16), k16) * decay, 0.0),
                vb=(vh * beta).astype(BF16), kbg=(kb * eg).astype(BF16),
                qk=jnp.where(incl, _dot_nt(qh.astype(BF16), k16) * decay, 0.0).astype(BF16),
                qg=(qh * eg).astype(BF16), kd=(kh * jnp.exp(g_last - gc)).astype(BF16),
                gl=jnp.exp(g_last)))
    tinvs = _unit_lower_inverses([u["lmat"] for u in units])
    for u, tinv in zip(units, tinvs):
        tinv = tinv.astype(BF16)
        u["u"] = _dot(tinv, u["vb"])
        u["w"] = _dot(tinv, u["kbg"]).astype(BF16)

    for u in units:
        h, rows = u["h"], u["rows"]
        hs = slice(h * GDN_DH, (h + 1) * GDN_DH)
        s_old = s_sc[h]
        s16 = s_old.astype(BF16)
        v_new = (u["u"] - _dot(u["w"], s16)).astype(BF16)
        o = _dot(u["qg"], s16) + _dot(u["qk"], v_new)
        s_sc[h] = s_old * u["gl"] + _dot_tn(u["kd"], v_new)
        on = o * lax.rsqrt(jnp.mean(o * o, axis=-1, keepdims=True) + EPS) * og_ref[...]
        zh = z_ref[0, rows, hs].astype(F32)
        o_ref[0, rows, hs] = (on * (zh * jax.nn.sigmoid(zh))).astype(BF16)


def _gdn(oqkv, oz, oab, conv_w, alog, dtb, og, ct=128):
    b, t, w3 = oqkv.shape
    full = lambda a: pl.BlockSpec(a.shape, lambda bi, c: (0,) * a.ndim)
    return pl.pallas_call(
        functools.partial(_gdn_body, ct=ct),
        grid=(b, t // ct),
        in_specs=[
            pl.BlockSpec((1, ct, w3), lambda bi, c: (bi, c, 0)),
            pl.BlockSpec((1, ct, GDN_W), lambda bi, c: (bi, c, 0)),
            pl.BlockSpec((1, ct, LANES), lambda bi, c: (bi, c, 0)),
            full(conv_w), full(alog), full(dtb), full(og),
        ],
        out_specs=pl.BlockSpec((1, ct, GDN_W), lambda bi, c: (bi, c, 0)),
        out_shape=jax.ShapeDtypeStruct((b, t, GDN_W), BF16),
        scratch_shapes=[pltpu.VMEM((ct + 8, w3), F32), pltpu.VMEM((GDN_HEADS, GDN_DH, GDN_DH), F32)],
        compiler_params=_cparams(("parallel", "arbitrary"), 32),
        name="gdn",
    )(oqkv, oz, oab, conv_w, alog, dtb, og)


def _outproj_body(ont_ref, og_ref, x_ref, ng_ref, wo_ref, fg_ref, wr_ref, br_ref,
                  x1_ref, h2_ref, idx_ref, gate_ref, rank_ref, cnt_ref, cnt_sc):
    i = pl.program_id(0)
    tm = x_ref.shape[0]

    @pl.when(i == 0)
    def _():
        cnt_sc[...] = jnp.zeros(cnt_sc.shape, F32)

    a = ont_ref[...]
    a = (a * lax.rsqrt(jnp.mean(a * a, axis=0, keepdims=True) + EPS) * ng_ref[...]).astype(BF16)
    x1 = x_ref[...] + _dot_tn(a, wo_ref[0:NSA_W, :]) + _dot(og_ref[...], wo_ref[NSA_W:, :])
    x1_ref[...] = x1
    h2f = x1 * lax.rsqrt(jnp.mean(x1 * x1, axis=-1, keepdims=True) + EPS) * fg_ref[...]
    h2_ref[...] = h2f
    h2 = h2f.astype(BF16)

    lane = lax.broadcasted_iota(jnp.int32, (tm, LANES), 1)
    logits = jnp.where(lane < N_EXPERTS, _dot(h2, wr_ref[...]) + br_ref[...], -jnp.inf)
    onehot = jnp.zeros((tm, LANES), F32)
    idx = jnp.zeros((tm, LANES), jnp.int32)
    vals = jnp.zeros((tm, LANES), F32)
    firsts = []
    v = logits
    top0 = None
    for k in range(TOP_K):
        mx = jnp.max(v, axis=-1, keepdims=True)
        first = jnp.min(jnp.where(v == mx, lane, LANES), axis=-1, keepdims=True)
        hit = lane == first
        v = jnp.where(hit, -jnp.inf, v)
        onehot = jnp.where(hit, 1.0, onehot)
        idx = jnp.where(lane == k, first, idx)
        top0 = mx if k == 0 else top0
        vals = jnp.where(lane == k, jnp.exp(mx - top0), vals)
        firsts.append(first)
    idx_ref[...] = idx
    gate_ref[...] = vals / jnp.sum(vals, axis=-1, keepdims=True)

    r = lax.broadcasted_iota(jnp.int32, (tm, tm), 0)
    c = lax.broadcasted_iota(jnp.int32, (tm, tm), 1)
    before = jnp.where(r > c, 1.0, 0.0).astype(BF16)
    excl = cnt_sc[...] + _dot(before, onehot.astype(BF16))
    rank = jnp.zeros((tm, LANES), F32)
    for k in range(TOP_K):
        rk = jnp.sum(jnp.where(lane == firsts[k], excl, 0.0), axis=-1, keepdims=True)
        rank = jnp.where(lane == k, rk, rank)
    rank_ref[...] = rank.astype(jnp.int32)
    cnt_sc[...] = cnt_sc[...] + jnp.sum(onehot, axis=0, keepdims=True)
    cnt_ref[...] = cnt_sc[...].astype(jnp.int32)


def _out_proj(o_nsa_t, o_gdn, x2, ng, wo, fg, wr, br, tm=512):
    n, d = x2.shape
    full = lambda a: pl.BlockSpec(a.shape, lambda i: (0,) * a.ndim)
    row = lambda w: pl.BlockSpec((tm, w), lambda i: (i, 0))
    return pl.pallas_call(
        _outproj_body,
        grid=(n // tm,),
        in_specs=[pl.BlockSpec((NSA_W, tm), lambda i: (0, i)), row(GDN_W), row(d), full(ng), full(wo), full(fg),
                  full(wr), full(br)],
        out_specs=[row(d), row(d), row(LANES), row(LANES), row(LANES),
                   pl.BlockSpec((1, LANES), lambda i: (0, 0))],
        out_shape=[jax.ShapeDtypeStruct((n, d), F32), jax.ShapeDtypeStruct((n, d), F32),
                   jax.ShapeDtypeStruct((n, LANES), jnp.int32), jax.ShapeDtypeStruct((n, LANES), F32),
                   jax.ShapeDtypeStruct((n, LANES), jnp.int32), jax.ShapeDtypeStruct((1, LANES), jnp.int32)],
        scratch_shapes=[pltpu.VMEM((1, LANES), F32)],
        compiler_params=_cparams(("arbitrary",), 48),
        name="out_proj_router",
    )(o_nsa_t, o_gdn, x2, ng, wo, fg, wr, br)


def _expert_body(be_ref, xs_ref, wg_ref, bg_ref, wu_ref, bu_ref, wd_ref, bd_ref, y_ref, wg16, wu16, wd16):
    i = pl.program_id(0)
    prev = be_ref[jnp.maximum(i - 1, 0)]
    fresh = (i == 0) | (be_ref[i] != prev)

    @pl.when(fresh)
    def _():
        wg16[...] = wg_ref[0].astype(BF16)
        wu16[...] = wu_ref[0].astype(BF16)
        wd16[...] = wd_ref[0].astype(BF16)

    x = xs_ref[...].astype(BF16)
    gate = jnp.minimum(_dot(x, wg16[...]) + bg_ref[0], SWIGLU_LIMIT)
    up = jnp.clip(_dot(x, wu16[...]) + bu_ref[0], -SWIGLU_LIMIT, SWIGLU_LIMIT)
    glu = gate * jax.nn.sigmoid(gate * SWIGLU_ALPHA)
    y_ref[...] = _dot(((up + 1.0) * glu).astype(BF16), wd16[...]) + bd_ref[0]


def _experts(blk_e, xs, wg, bg, wu, bu, wd, bd):
    n_rows, d = xs.shape
    de = wg.shape[2]
    r = MOE_ROW_BLOCK
    wspec = lambda a, b: pl.BlockSpec((1, a, b), lambda i, be: (be[i], 0, 0))
    grid_spec = pltpu.PrefetchScalarGridSpec(
        num_scalar_prefetch=1,
        grid=(n_rows // r,),
        in_specs=[pl.BlockSpec((r, d), lambda i, be: (i, 0)),
                  wspec(d, de), wspec(1, de), wspec(d, de), wspec(1, de), wspec(de, d), wspec(1, d)],
        out_specs=pl.BlockSpec((r, d), lambda i, be: (i, 0)),
        scratch_shapes=[pltpu.VMEM((d, de), BF16), pltpu.VMEM((d, de), BF16), pltpu.VMEM((de, d), BF16)],
    )
    return pl.pallas_call(
        _expert_body,
        grid_spec=grid_spec,
        out_shape=jax.ShapeDtypeStruct((n_rows, d), F32),
        compiler_params=_cparams(("arbitrary",), 56),
        name="moe_experts",
    )(blk_e, xs, wg, bg, wu, bu, wd, bd)


SC_WINDOW = 128
SC_SUBROW = 256


def _sc_mesh():
    return plsc.VectorSubcoreMesh(core_axis_name="c", subcore_axis_name="s")


def _sc_dispatch(h2, dest_rows, n_rows):
    n, d = h2.shape

    @functools.partial(pl.kernel, out_type=jax.ShapeDtypeStruct((n_rows, d), h2.dtype), mesh=_sc_mesh())
    def dispatch(x_hbm, *refs):
        idx_hbm, o_hbm = refs[:TOP_K], refs[TOP_K]

        def body(x_vmem, *idx_vmem):
            for iv in idx_vmem:
                pltpu.sync_copy(x_vmem, o_hbm.at[iv.at[0]])

        pltpu.emit_pipeline(
            body,
            grid=(n // SC_WINDOW,),
            in_specs=[pl.BlockSpec((SC_WINDOW, d), lambda i: (i, 0))]
                     + [pl.BlockSpec((1, SC_WINDOW), lambda i: (0, i))] * TOP_K,
            out_specs=[],
            core_axis_name=("c", "s"),
            dimension_semantics=(pltpu.PARALLEL,),
        )(x_hbm, *idx_hbm)

    return dispatch(h2, *dest_rows)


def _sc_gather(table, idx):
    _, d = table.shape
    m = idx.shape[1]

    @functools.partial(pl.kernel, out_type=jax.ShapeDtypeStruct((m, d), table.dtype), mesh=_sc_mesh())
    def gather(t_hbm, i_hbm, o_hbm):
        def body(i_vmem, o_vmem):
            pltpu.sync_copy(t_hbm.at[i_vmem.at[0]], o_vmem)

        pltpu.emit_pipeline(
            body,
            grid=(m // SC_WINDOW,),
            in_specs=[pl.BlockSpec((1, SC_WINDOW), lambda i: (0, i))],
            out_specs=[pl.BlockSpec((SC_WINDOW, d), lambda i: (i, 0))],
            core_axis_name=("c", "s"),
            dimension_semantics=(pltpu.PARALLEL,),
        )(i_hbm, o_hbm)

    return gather(table, idx)


def _combine_body(x1_ref, y_ref, gate_ref, o_ref):
    acc = x1_ref[...]
    for k in range(TOP_K):
        acc = acc + gate_ref[:, k:k + 1] * y_ref[k]
    o_ref[...] = acc


def _combine(x1, y4, gates, tm=512):
    n, d = x1.shape
    row = lambda w: pl.BlockSpec((tm, w), lambda i: (i, 0))
    return pl.pallas_call(
        _combine_body,
        grid=(n // tm,),
        in_specs=[row(d), pl.BlockSpec((TOP_K, tm, d), lambda i: (0, i, 0)), row(LANES)],
        out_specs=row(d),
        out_shape=jax.ShapeDtypeStruct((n, d), F32),
        compiler_params=_cparams(("parallel",), 48),
        name="moe_combine",
    )(x1, y4, gates)


def _pad_lanes(a, width=LANES):
    return jnp.pad(a, ((0, 0), (0, width - a.shape[1])))


def _layer(x, attn_norm_g, w_in, q_g, kc_g, ks_g, kw_g, ck_pos, ck_w1, ck_b1, ck_w2, ck_b2,
           cv_pos, cv_w1, cv_b1, cv_w2, cv_b2, nsa_out_g, conv_w, a_log, dt_bias, gdn_out_g, w_out,
           ffn_g, router_w, router_b, e_wg, e_bg, e_wu, e_bu, e_wd, e_bd):
    b, t, d = x.shape
    n = b * t
    x2 = x.reshape(n, d)

    o = np.cumsum([0, NSA_W] + [NSA_GROUPS * NSA_DH] * 6 + [3 * NSA_HEADS, 3 * GDN_W, GDN_W, GDN_HEADS, GDN_HEADS])
    wq_t = w_in[:, o[0]:o[1]].T.reshape(NSA_GROUPS, NSA_HPG, NSA_DH, d)
    zq = jnp.zeros((NSA_HPG, NSA_DH, d), F32)
    wq_t = jnp.stack([jnp.concatenate([wq_t[0], zq], axis=1), jnp.concatenate([zq, wq_t[1]], axis=1)])
    wq_t = wq_t.reshape(NSA_HEADS * LANES, d).astype(BF16)
    qg1 = q_g * (NSA_DH ** -0.5)
    zg = jnp.zeros((NSA_DH,), F32)
    qg_col = jnp.concatenate([jnp.tile(jnp.concatenate([qg1, zg]), NSA_HPG),
                              jnp.tile(jnp.concatenate([zg, qg1]), NSA_HPG)]).reshape(NSA_HEADS * LANES, 1)
    wkv = w_in[:, o[1]:o[7]].astype(BF16)
    ones = jnp.ones((LANES,), F32)
    kg = jnp.concatenate([ones, ones, ks_g, ks_g, ones, kw_g, kw_g, ones]).reshape(1, 6 * LANES)
    wv_t = jnp.concatenate([w_in[:, o[4]:o[5]], w_in[:, o[6]:o[7]]], axis=1).T.astype(BF16)
    wg_t = w_in[:, o[7]:o[8]].T.reshape(NSA_GROUPS, NSA_HPG * 3, d)
    wg_t = jnp.pad(wg_t, ((0, 0), (0, GATE_ROWS - NSA_HPG * 3), (0, 0))).reshape(NSA_GROUPS * GATE_ROWS, d)
    wg_t = wg_t.astype(BF16)
    wab = _pad_lanes(w_in[:, o[10]:o[12]]).astype(BF16)
    wqkv = w_in[:, o[8]:o[9]].astype(BF16)
    wz = w_in[:, o[9]:o[10]].astype(BF16)

    oqt, okv, ovt, ogt, oqkv, oz, oab = _in_proj(x2, attn_norm_g.reshape(1, d), wq_t, wkv, wv_t, wg_t, wqkv, wz,
                                                 wab, qg_col, kg)

    nch = t // CMP_STRIDE
    n_cmp = (t - CMP_BLOCK) // CMP_STRIDE + 1
    half = CMP_STRIDE * NSA_DH
    xflat = okv[:, :2 * LANES].reshape(b, nch, CMP_STRIDE, 2, NSA_GROUPS, NSA_DH)
    xflat = xflat.transpose(0, 3, 4, 1, 2, 5).reshape(b, 2, NSA_GROUPS, nch, half)
    pos = jnp.stack([ck_pos, cv_pos]).reshape(2, 2, 1, half)
    w1 = jnp.stack([ck_w1, cv_w1]).reshape(2, 2, half, CMP_HIDDEN).astype(BF16)
    b1 = jnp.stack([ck_b1, cv_b1]).reshape(2, 1, CMP_HIDDEN)
    w2 = jnp.stack([ck_w2, cv_w2]).astype(BF16)
    b2 = jnp.stack([ck_b2, cv_b2]).reshape(2, 1, NSA_DH)
    w2t = jnp.stack([ck_w2.T, cv_w2.T]).astype(BF16)
    b2t = jnp.stack([ck_b2, cv_b2]).reshape(2, NSA_DH, 1)
    kc, vct = _compress(xflat, pos, w1, b1, w2, b2, w2t, b2t, kc_g.reshape(1, NSA_DH), n_cmp)

    n_slc = t // SLC_BLOCK
    n_top = min(SLC_TOPK, n_slc)
    nblk = max(n_slc, LANES)
    kt = min(512, t)
    ci = np.arange(nch)[None, :] * CMP_STRIDE
    sj = np.arange(nblk)[:, None] * SLC_BLOCK
    overlap = ((ci < sj + SLC_BLOCK) & (ci + CMP_BLOCK > sj) & (np.arange(nch)[None, :] < n_cmp)
               & (np.arange(nblk)[:, None] < n_slc))
    expand_t = (np.arange(t)[:, None] // SLC_BLOCK) == np.arange(nblk)[None, :]
    o_nsa_t = _nsa_attention(oqt, ogt, kc, vct, okv.reshape(b, t, -1), ovt, jnp.asarray(expand_t, BF16),
                             jnp.asarray(overlap, BF16), b, t, n_cmp, n_top, kt)

    alog_row = _pad_lanes(a_log.reshape(1, GDN_HEADS))
    dtb_row = _pad_lanes(dt_bias.reshape(1, GDN_HEADS))
    o_gdn = _gdn(oqkv.reshape(b, t, -1), oz.reshape(b, t, -1), oab.reshape(b, t, -1), conv_w,
                 alog_row, dtb_row, gdn_out_g.reshape(1, GDN_DH))

    wr = _pad_lanes(router_w).astype(BF16)
    br = _pad_lanes(router_b.reshape(1, N_EXPERTS))
    x1, h2, idx, gates, rank, counts = _out_proj(
        o_nsa_t, o_gdn.reshape(n, GDN_W), x2, nsa_out_g.reshape(NSA_W, 1),
        w_out.astype(BF16), ffn_g.reshape(1, d), wr, br)

    r = MOE_ROW_BLOCK
    nk = n * TOP_K
    counts = counts[0, :N_EXPERTS]
    pcounts = (counts + r - 1) // r * r
    pends = jnp.cumsum(pcounts)
    pstarts = pends - pcounts
    top_idx = idx[:, :TOP_K]
    dest = pstarts[top_idx] + rank[:, :TOP_K]
    n_rows = (nk + r - 1) // r * r + N_EXPERTS * r
    n_blocks = n_rows // r
    blk_start = jnp.arange(n_blocks, dtype=jnp.int32)[:, None] * r
    blk_e = jnp.minimum(jnp.sum(pends[None, :] <= blk_start, axis=1), N_EXPERTS - 1).astype(jnp.int32)
    pieces = d // SC_SUBROW
    dest_p = (dest.T.astype(jnp.int32)[:, :, None] * pieces
              + jnp.arange(pieces, dtype=jnp.int32)).reshape(TOP_K, 1, n * pieces)
    xs = _sc_dispatch(h2.reshape(n * pieces, SC_SUBROW), [dest_p[k] for k in range(TOP_K)], n_rows * pieces)
    ys = _experts(blk_e, xs.reshape(n_rows, d), e_wg, e_bg.reshape(N_EXPERTS, 1, -1), e_wu,
                  e_bu.reshape(N_EXPERTS, 1, -1), e_wd, e_bd.reshape(N_EXPERTS, 1, -1))
    y4 = _sc_gather(ys.reshape(n_rows * pieces, SC_SUBROW), dest_p.reshape(1, nk * pieces))
    return _combine(x1, y4.reshape(TOP_K, n, d), gates).reshape(b, t, d)


def kernel(x, attn_norm_g, w_in, nsa_q_norm_g, nsa_kc_norm_g, nsa_ks_norm_g, nsa_kw_norm_g, cmp_k_pos, cmp_k_w1, cmp_k_b1, cmp_k_w2, cmp_k_b2, cmp_v_pos, cmp_v_w1, cmp_v_b1, cmp_v_w2, cmp_v_b2, nsa_out_norm_g, gdn_conv_w, gdn_a_log, gdn_dt_bias, gdn_out_norm_g, w_out, ffn_norm_g, router_w, router_b, exp_w_gate, exp_b_gate, exp_w_up, exp_b_up, exp_w_down, exp_b_down):
    params = (attn_norm_g, w_in, nsa_q_norm_g, nsa_kc_norm_g, nsa_ks_norm_g, nsa_kw_norm_g,
              cmp_k_pos, cmp_k_w1, cmp_k_b1, cmp_k_w2, cmp_k_b2, cmp_v_pos, cmp_v_w1, cmp_v_b1, cmp_v_w2, cmp_v_b2,
              nsa_out_norm_g, gdn_conv_w, gdn_a_log, gdn_dt_bias, gdn_out_norm_g, w_out, ffn_norm_g,
              router_w, router_b, exp_w_gate, exp_b_gate, exp_w_up, exp_b_up, exp_w_down, exp_b_down)
    for l in range(attn_norm_g.shape[0]):
        x = _layer(x, *(p[l] for p in params))
    return x
```

```python
import functools

import jax
import jax.numpy as jnp
import numpy as np
from jax import lax
from jax.experimental import pallas as pl
from jax.experimental.pallas import tpu as pltpu
from jax.experimental.pallas import tpu_sc as plsc

F32 = jnp.float32
BF16 = jnp.bfloat16

EPS = 1e-6
NEG = -1e30

NSA_HEADS = 8
NSA_GROUPS = 2
NSA_HPG = 4
NSA_DH = 64
CMP_BLOCK = 32
CMP_STRIDE = 16
CMP_HIDDEN = 256
SLC_BLOCK = 64
SLC_TOPK = 16
WINDOW = 512
Q_BLOCK = 128
GDN_HEADS = 4
GDN_DH = 128
GDN_CONV = 4
GDN_CHUNK = 64
N_EXPERTS = 32
TOP_K = 4
SWIGLU_LIMIT = 7.0
SWIGLU_ALPHA = 1.702
MOE_ROW_BLOCK = 256

LANES = 128
GATE_ROWS = 16
NSA_W = NSA_HEADS * NSA_DH
GDN_W = GDN_HEADS * GDN_DH

_NT = (((1,), (1,)), ((), ()))
_TN = (((0,), (0,)), ((), ()))


def _cparams(sem, vmem_mb):
    return pltpu.CompilerParams(dimension_semantics=sem, vmem_limit_bytes=vmem_mb * 1024 * 1024)


def _dot(a, b):
    return jnp.dot(a, b, preferred_element_type=F32)


def _dot_nt(a, b):
    return lax.dot_general(a, b, _NT, preferred_element_type=F32)


def _dot_tn(a, b):
    return lax.dot_general(a, b, _TN, preferred_element_type=F32)


def _join_pieces(ref):
    return jnp.concatenate([ref[j] for j in range(ref.shape[0])], axis=1)


def _store_pieces(ref, val):
    sub = ref.shape[2]
    for j in range(ref.shape[0]):
        ref[j] = val[:, j * sub:(j + 1) * sub]


def _inproj_body(x_ref, g_ref, wqt_ref, wkv_ref, wvt_ref, wgt_ref, wqkv_ref, wz_ref, wab_ref, qg_ref, kg_ref,
                 oqt_ref, okv_ref, ovt_ref, ogt_ref, oqkv_ref, oz_ref, oab_ref):
    x = x_ref[...]
    h = (x * lax.rsqrt(jnp.mean(x * x, axis=-1, keepdims=True) + EPS) * g_ref[...]).astype(BF16)
    tm = x.shape[0]

    yq = _dot_nt(wqt_ref[...], h)
    for s in range(NSA_HEADS):
        sl = slice(s * LANES, (s + 1) * LANES)
        ys = yq[sl, :]
        ms = jnp.sum(ys * ys, axis=0, keepdims=True) * (1.0 / NSA_DH)
        oqt_ref[sl, :] = (ys * lax.rsqrt(ms + EPS) * qg_ref[sl, :]).astype(BF16)

    ykv = _dot(h, wkv_ref[...])
    lane = lax.broadcasted_iota(jnp.int32, (tm, LANES), 1)
    low = lane < NSA_DH
    for s in range(6):
        sl = slice(s * LANES, (s + 1) * LANES)
        ys = ykv[:, sl]
        if s in (2, 4):
            y2 = ys * ys
            s0 = jnp.sum(jnp.where(low, y2, 0.0), axis=-1, keepdims=True)
            s1 = jnp.sum(jnp.where(low, 0.0, y2), axis=-1, keepdims=True)
            ms = jnp.where(low, s0, s1) * (1.0 / NSA_DH)
            ys = ys * lax.rsqrt(ms + EPS) * kg_ref[:, sl]
        okv_ref[:, sl] = ys.astype(BF16)

    ovt_ref[...] = _dot_nt(wvt_ref[...], h).astype(BF16)
    ogt_ref[...] = _dot_nt(wgt_ref[...], h)
    oqkv_ref[...] = _dot(h, wqkv_ref[...]).astype(BF16)
    oz_ref[...] = _dot(h, wz_ref[...]).astype(BF16)
    oab_ref[...] = _dot(h, wab_ref[...])


def _in_proj(x2, g, wqt, wkv, wvt, wgt, wqkv, wz, wab, qg, kg, tm=512):
    n, d = x2.shape
    full = lambda a: pl.BlockSpec(a.shape, lambda i: (0,) * a.ndim)
    row = lambda w: pl.BlockSpec((tm, w), lambda i: (i, 0))
    colb = lambda r: pl.BlockSpec((r, tm), lambda i: (0, i))
    return pl.pallas_call(
        _inproj_body,
        grid=(n // tm,),
        in_specs=[row(d)] + [full(a) for a in (g, wqt, wkv, wvt, wgt, wqkv, wz, wab, qg, kg)],
        out_specs=[colb(wqt.shape[0]), row(wkv.shape[1]), colb(wvt.shape[0]), colb(wgt.shape[0]),
                   row(wqkv.shape[1]), row(wz.shape[1]), row(wab.shape[1])],
        out_shape=[jax.ShapeDtypeStruct((wqt.shape[0], n), BF16), jax.ShapeDtypeStruct((n, wkv.shape[1]), BF16),
                   jax.ShapeDtypeStruct((wvt.shape[0], n), BF16), jax.ShapeDtypeStruct((wgt.shape[0], n), F32),
                   jax.ShapeDtypeStruct((n, wqkv.shape[1]), BF16), jax.ShapeDtypeStruct((n, wz.shape[1]), BF16),
                   jax.ShapeDtypeStruct((n, wab.shape[1]), F32)],
        compiler_params=_cparams(("parallel",), 56),
        name="in_proj",
    )(x2, g, wqt, wkv, wvt, wgt, wqkv, wz, wab, qg, kg)


def _compress_body(x_ref, pos_ref, w1_ref, b1_ref, w2_ref, b2_ref, w2t_ref, b2t_ref, g_ref, ok_ref, ovt_ref,
                   *, n_cmp):
    is_key = pl.program_id(1) == 0
    nch = x_ref.shape[3]
    hids = []
    for grp in range(NSA_GROUPS):
        x = x_ref[0, 0, grp].astype(F32)
        xa = (x + pos_ref[0, 0]).astype(BF16)
        xb = (x + pos_ref[0, 1]).astype(BF16)
        a = _dot(xa, w1_ref[0, 0])
        b = _dot(xb, w1_ref[0, 1])
        b_next = pltpu.roll(b, nch - 1, 0)
        hids.append(jax.nn.gelu(a + b_next + b1_ref[0]).astype(BF16))

    @pl.when(is_key)
    def _():
        row = lax.broadcasted_iota(jnp.int32, (nch, NSA_DH), 0)
        outs = []
        for grp in range(NSA_GROUPS):
            out = _dot(hids[grp], w2_ref[0]) + b2_ref[0]
            out = out * lax.rsqrt(jnp.mean(out * out, axis=-1, keepdims=True) + EPS) * g_ref[...]
            outs.append(jnp.where(row < n_cmp, out, 0.0))
        ok_ref[0] = jnp.concatenate(outs, axis=-1).astype(BF16)

    @pl.when(jnp.logical_not(is_key))
    def _():
        col = lax.broadcasted_iota(jnp.int32, (NSA_DH, nch), 1)
        outs = []
        for grp in range(NSA_GROUPS):
            out = _dot_nt(w2t_ref[0], hids[grp]) + b2t_ref[0]
            outs.append(jnp.where(col < n_cmp, out, 0.0))
        ovt_ref[0] = jnp.concatenate(outs, axis=0).astype(BF16)


def _compress(xflat, pos, w1, b1, w2, b2, w2t, b2t, kc_g, n_cmp):
    b, _, _, nch, flat = xflat.shape
    return pl.pallas_call(
        functools.partial(_compress_body, n_cmp=n_cmp),
        grid=(b, 2),
        in_specs=[
            pl.BlockSpec((1, 1, NSA_GROUPS, nch, flat), lambda i, j: (i, j, 0, 0, 0)),
            pl.BlockSpec((1, 2, 1, flat), lambda i, j: (j, 0, 0, 0)),
            pl.BlockSpec((1, 2, flat, CMP_HIDDEN), lambda i, j: (j, 0, 0, 0)),
            pl.BlockSpec((1, 1, CMP_HIDDEN), lambda i, j: (j, 0, 0)),
            pl.BlockSpec((1, CMP_HIDDEN, NSA_DH), lambda i, j: (j, 0, 0)),
            pl.BlockSpec((1, 1, NSA_DH), lambda i, j: (j, 0, 0)),
            pl.BlockSpec((1, NSA_DH, CMP_HIDDEN), lambda i, j: (j, 0, 0)),
            pl.BlockSpec((1, NSA_DH, 1), lambda i, j: (j, 0, 0)),
            pl.BlockSpec((1, NSA_DH), lambda i, j: (0, 0)),
        ],
        out_specs=[pl.BlockSpec((1, nch, LANES), lambda i, j: (i, 0, 0)),
                   pl.BlockSpec((1, LANES, nch), lambda i, j: (i, 0, 0))],
        out_shape=[jax.ShapeDtypeStruct((b, nch, LANES), BF16), jax.ShapeDtypeStruct((b, LANES, nch), BF16)],
        compiler_params=_cparams(("parallel", "arbitrary"), 32),
        name="nsa_compress",
    )(xflat, pos, w1, b1, w2, b2, w2t, b2t, kc_g)


def _tile_heads(a):
    return jnp.concatenate([a] * NSA_HPG, axis=1)


def _masked_softmax_t(s, valid, any_valid=None):
    s = s + _tile_heads(jnp.where(valid, 0.0, NEG))
    p = jnp.exp2(s - jnp.max(s, axis=0, keepdims=True))
    inv = 1.0 / jnp.maximum(jnp.sum(p, axis=0, keepdims=True), 1e-30)
    if any_valid is not None:
        inv = jnp.where(_tile_heads(any_valid), inv, 0.0)
    return p * inv


def _nsa_body(qt_ref, gt_ref, kc_ref, vct_ref, ks_ref, kw_ref, vt_ref, et_ref, ov_ref, o_ref, acc_sc,
              *, n_cmp, n_top, kt):
    grp = pl.program_id(1)
    s0 = pl.program_id(2) * Q_BLOCK
    nch = kc_ref.shape[1]
    nblk = ov_ref.shape[0]

    qt = jnp.concatenate([qt_ref[h * LANES:(h + 1) * LANES, :] for h in range(NSA_HPG)], axis=1)
    t_row = s0 + lax.broadcasted_iota(jnp.int32, (1, Q_BLOCK), 1)

    cidx = lax.broadcasted_iota(jnp.int32, (nch, 1), 0)
    cvalid = (cidx * CMP_STRIDE + (CMP_BLOCK - 1) <= t_row) & (cidx < n_cmp)
    pc = _masked_softmax_t(_dot(kc_ref[0], qt), cvalid, t_row >= CMP_BLOCK - 1)
    oc = _dot(vct_ref[0], pc.astype(BF16))

    pcs = (pc[:, 0:Q_BLOCK] + pc[:, Q_BLOCK:2 * Q_BLOCK] + pc[:, 2 * Q_BLOCK:3 * Q_BLOCK]
           + pc[:, 3 * Q_BLOCK:4 * Q_BLOCK])
    pcs_hi = pcs.astype(BF16)
    pcs_lo = (pcs - pcs_hi.astype(F32)).astype(BF16)
    imp = _dot(ov_ref[...], pcs_hi) + _dot(ov_ref[...], pcs_lo)
    blk = lax.broadcasted_iota(jnp.int32, (nblk, Q_BLOCK), 0)
    cur = t_row // SLC_BLOCK
    imp = jnp.where(blk * SLC_BLOCK > t_row, NEG, imp)
    imp = jnp.where((blk == 0) | (blk == cur) | (blk == cur - 1), -NEG, imp)

    def pick(_, carry):
        v, sel = carry
        mx = jnp.max(v, axis=0, keepdims=True)
        first = jnp.min(jnp.where(v == mx, blk, nblk), axis=0, keepdims=True)
        hit = blk == first
        return jnp.where(hit, -jnp.inf, v), jnp.where(hit, 1.0, sel)

    _, sel = lax.fori_loop(0, n_top, pick, (imp, jnp.zeros((nblk, Q_BLOCK), F32)), unroll=True)
    sel = sel.astype(BF16)

    acc_sc[...] = jnp.zeros(acc_sc.shape, F32)

    def tile(idx, m_old, l_old, acc_ref):
        k0 = pl.multiple_of(idx * kt, kt)
        kpos = k0 + lax.broadcasted_iota(jnp.int32, (kt, 1), 0)
        picked = _dot(et_ref[pl.ds(k0, kt), :], sel)
        bias = jnp.where((picked > 0.5) & (kpos <= t_row), 0.0, NEG)
        s = _dot(ks_ref[0, pl.ds(k0, kt), :], qt) + _tile_heads(bias)
        m_new = jnp.maximum(m_old, jnp.max(s, axis=0, keepdims=True))
        alpha = jnp.exp2(m_old - m_new)
        p = jnp.exp2(s - m_new)
        l_new = alpha * l_old + jnp.sum(p, axis=0, keepdims=True)
        acc_ref[...] = alpha * acc_ref[...] + _dot(vt_ref[0:LANES, pl.ds(k0, kt)], p.astype(BF16))
        return m_new, l_new

    def tile_pair(j, carry):
        m0, l0, m1, l1 = carry
        m0, l0 = tile(2 * j, m0, l0, acc_sc.at[0])
        m1, l1 = tile(2 * j + 1, m1, l1, acc_sc.at[1])
        return m0, l0, m1, l1

    hq = NSA_HPG * Q_BLOCK
    n_tiles = (s0 + Q_BLOCK + kt - 1) // kt
    m_init, l_init = jnp.full((1, hq), NEG, F32), jnp.zeros((1, hq), F32)
    m0, l0, m1, l1 = lax.fori_loop(0, (n_tiles + 1) // 2, tile_pair, (m_init, l_init, m_init, l_init))
    m_fin = jnp.maximum(m0, m1)
    w0, w1 = jnp.exp2(m0 - m_fin), jnp.exp2(m1 - m_fin)
    l_fin = l0 * w0 + l1 * w1
    osl = (acc_sc[0] * w0 + acc_sc[1] * w1) / jnp.maximum(l_fin, 1e-30)

    wlen = WINDOW + Q_BLOCK
    w0 = pl.multiple_of(jnp.maximum(s0 - WINDOW, 0), Q_BLOCK)
    kpos = w0 + lax.broadcasted_iota(jnp.int32, (wlen, 1), 0)
    wvalid = (kpos <= t_row) & (kpos > t_row - WINDOW)
    pw = _masked_softmax_t(_dot(kw_ref[0, pl.ds(w0, wlen), :], qt), wvalid)
    ow = _dot(vt_ref[LANES:2 * LANES, pl.ds(w0, wlen)], pw.astype(BF16))

    gts = jax.nn.sigmoid(gt_ref[...])
    for h in range(NSA_HPG):
        cols = slice(h * Q_BLOCK, (h + 1) * Q_BLOCK)
        mix = (gts[3 * h:3 * h + 1, :] * oc[:, cols] + gts[3 * h + 1:3 * h + 2, :] * osl[:, cols]
               + gts[3 * h + 2:3 * h + 3, :] * ow[:, cols])
        o_ref[h * NSA_DH:(h + 1) * NSA_DH, :] = jnp.where(grp == 0, mix[:NSA_DH, :], mix[NSA_DH:, :])


def _nsa_attention(qt, gt, kc, vct, okv, vt, expand_t, overlap, b, t, n_cmp, n_top, kt):
    nch = kc.shape[1]
    nq = t // Q_BLOCK
    n = b * t
    return pl.pallas_call(
        functools.partial(_nsa_body, n_cmp=n_cmp, n_top=n_top, kt=kt),
        grid=(b, NSA_GROUPS, nq),
        in_specs=[
            pl.BlockSpec((NSA_HPG * LANES, Q_BLOCK), lambda bi, g, i: (g, bi * nq + i)),
            pl.BlockSpec((GATE_ROWS, Q_BLOCK), lambda bi, g, i: (g, bi * nq + i)),
            pl.BlockSpec((1, nch, LANES), lambda bi, g, i: (bi, 0, 0)),
            pl.BlockSpec((1, LANES, nch), lambda bi, g, i: (bi, 0, 0)),
            pl.BlockSpec((1, t, LANES), lambda bi, g, i: (bi, 0, 2)),
            pl.BlockSpec((1, t, LANES), lambda bi, g, i: (bi, 0, 4)),
            pl.BlockSpec((2 * LANES, t), lambda bi, g, i: (0, bi)),
            pl.BlockSpec(expand_t.shape, lambda bi, g, i: (0, 0)),
            pl.BlockSpec(overlap.shape, lambda bi, g, i: (0, 0)),
        ],
        out_specs=pl.BlockSpec((NSA_HPG * NSA_DH, Q_BLOCK), lambda bi, g, i: (g, bi * nq + i)),
        out_shape=jax.ShapeDtypeStruct((NSA_W, n), F32),
        scratch_shapes=[pltpu.VMEM((2, LANES, NSA_HPG * Q_BLOCK), F32)],
        compiler_params=_cparams(("parallel", "parallel", "arbitrary"), 56),
        name="nsa_attention",
    )(qt, gt, kc, vct, okv, okv, vt, expand_t, overlap)


def _split_bf16(a):
    hi = a.astype(BF16)
    return hi, (a - hi.astype(F32)).astype(BF16)


def _split_lhs(a):
    hi, lo = _split_bf16(a)
    return jnp.concatenate([hi, hi, lo], axis=1)


def _split_rhs(b):
    hi, lo = _split_bf16(b)
    return jnp.concatenate([hi, lo, hi], axis=0)


def _unit_lower_inverses(lmats):
    c = lmats[0].shape[0]
    r = lax.broadcasted_iota(jnp.int32, (c, c), 0)
    col = lax.broadcasted_iota(jnp.int32, (c, c), 1)
    eye = jnp.where(r == col, 1.0, 0.0)
    xs = [eye - l for l in lmats]
    ps = [_dot(_split_lhs(l), _split_rhs(l)) for l in lmats]
    steps = int(np.log2(c)) - 1
    for s in range(steps):
        last = s + 1 == steps
        for i in range(len(lmats)):
            rhs = _split_rhs(ps[i])
            if last:
                xs[i] = xs[i] + _dot(_split_lhs(xs[i]), rhs)
            else:
                both = _dot(_split_lhs(jnp.concatenate([xs[i], ps[i]], axis=0)), rhs)
                xs[i] = xs[i] + both[:c]
                ps[i] = both[c:]
    return xs


def _gdn_body(x_ref, z_ref, ab_ref, cw_ref, alog_ref, dtb_ref, og_ref, o_ref, xbuf, s_sc, *, ct):
    c = pl.program_id(1)
    halo = 8

    @pl.when(c == 0)
    def _():
        xbuf[0:halo, :] = jnp.zeros((halo, xbuf.shape[1]), F32)
        s_sc[...] = jnp.zeros(s_sc.shape, F32)

    @pl.when(c > 0)
    def _():
        xbuf[0:halo, :] = xbuf[ct:ct + halo, :]

    xbuf[halo:halo + ct, :] = x_ref[0].astype(F32)
    y = cw_ref[0:1, :] * xbuf[pl.ds(halo - 3, ct), :]
    for k in range(1, GDN_CONV):
        y = y + cw_ref[k:k + 1, :] * xbuf[pl.ds(halo - 3 + k, ct), :]
    y = y * jax.nn.sigmoid(y)

    ab = ab_ref[0]
    g_all = -jnp.exp(alog_ref[...]) * jax.nn.softplus(ab + dtb_ref[...])
    beta_all = jax.nn.sigmoid(ab)

    ch = GDN_CHUNK
    r = lax.broadcasted_iota(jnp.int32, (ch, ch), 0)
    col = lax.broadcasted_iota(jnp.int32, (ch, ch), 1)
    incl = r >= col
    strict = r > col
    tril = jnp.where(incl, 1.0, 0.0)

    tril16 = jnp.concatenate([tril.astype(BF16)] * 3, axis=1)
    units = []
    for ci in range(ct // ch):
        rows = slice(ci * ch, (ci + 1) * ch)
        g_hi, g_lo = _split_bf16(g_all[rows])
        g_lo2 = (g_all[rows] - g_hi.astype(F32) - g_lo.astype(F32)).astype(BF16)
        gc_all = _dot(tril16, jnp.concatenate([g_hi, g_lo, g_lo2], axis=0))
        gc_t = gc_all.T
        for h in range(GDN_HEADS):
            hs = slice(h * GDN_DH, (h + 1) * GDN_DH)
            qh = y[rows, hs]
            kh = y[rows, GDN_W + h * GDN_DH:GDN_W + (h + 1) * GDN_DH]
            vh = y[rows, 2 * GDN_W + h * GDN_DH:2 * GDN_W + (h + 1) * GDN_DH]
            qh = qh * lax.rsqrt(jnp.sum(qh * qh, axis=-1, keepdims=True) + EPS) * (GDN_DH ** -0.5)
            kh = kh * lax.rsqrt(jnp.sum(kh * kh, axis=-1, keepdims=True) + EPS)
            gc = gc_all[:, h:h + 1]
            gr = gc_t[h:h + 1, :]
            g_last = gc_all[ch - 1:ch, h:h + 1]
            beta = beta_all[rows, GDN_HEADS + h:GDN_HEADS + h + 1]
            eg = jnp.exp(gc)
            decay = jnp.where(incl, jnp.exp(jnp.minimum(gc - gr, 0.0)), 0.0)
            kb = kh * beta
            k16 = kh.astype(BF16)
            units.append(dict(
                rows=rows, h=h,
                lmat=jnp.where(strict, _dot_nt(kb.astype(BF16), k16) * decay, 0.0),
                vb=(vh * beta).astype(BF16), kbg=(kb * eg).astype(BF16),
                qk=jnp.where(incl, _dot_nt(qh.astype(BF16), k16) * decay, 0.0).astype(BF16),
                qg=(qh * eg).astype(BF16), kd=(kh * jnp.exp(g_last - gc)).astype(BF16),
                gl=jnp.exp(g_last)))
    tinvs = _unit_lower_inverses([u["lmat"] for u in units])
    for u, tinv in zip(units, tinvs):
        tinv = tinv.astype(BF16)
        u["u"] = _dot(tinv, u["vb"])
        u["w"] = _dot(tinv, u["kbg"]).astype(BF16)

    for u in units:
        h, rows = u["h"], u["rows"]
        hs = slice(h * GDN_DH, (h + 1) * GDN_DH)
        s_old = s_sc[h]
        s16 = s_old.astype(BF16)
        v_new = (u["u"] - _dot(u["w"], s16)).astype(BF16)
        o = _dot(u["qg"], s16) + _dot(u["qk"], v_new)
        s_sc[h] = s_old * u["gl"] + _dot_tn(u["kd"], v_new)
        on = o * lax.rsqrt(jnp.mean(o * o, axis=-1, keepdims=True) + EPS) * og_ref[...]
        zh = z_ref[0, rows, hs].astype(F32)
        o_ref[0, rows, hs] = (on * (zh * jax.nn.sigmoid(zh))).astype(BF16)


def _gdn(oqkv, oz, oab, conv_w, alog, dtb, og, ct=128):
    b, t, w3 = oqkv.shape
    full = lambda a: pl.BlockSpec(a.shape, lambda bi, c: (0,) * a.ndim)
    return pl.pallas_call(
        functools.partial(_gdn_body, ct=ct),
        grid=(b, t // ct),
        in_specs=[
            pl.BlockSpec((1, ct, w3), lambda bi, c: (bi, c, 0)),
            pl.BlockSpec((1, ct, GDN_W), lambda bi, c: (bi, c, 0)),
            pl.BlockSpec((1, ct, LANES), lambda bi, c: (bi, c, 0)),
            full(conv_w), full(alog), full(dtb), full(og),
        ],
        out_specs=pl.BlockSpec((1, ct, GDN_W), lambda bi, c: (bi, c, 0)),
        out_shape=jax.ShapeDtypeStruct((b, t, GDN_W), BF16),
        scratch_shapes=[pltpu.VMEM((ct + 8, w3), F32), pltpu.VMEM((GDN_HEADS, GDN_DH, GDN_DH), F32)],
        compiler_params=_cparams(("parallel", "arbitrary"), 32),
        name="gdn",
    )(oqkv, oz, oab, conv_w, alog, dtb, og)


def _outproj_body(ont_ref, og_ref, x_ref, ng_ref, wo_ref, fg_ref, wr_ref, br_ref,
                  x1_ref, h2_ref, idx_ref, gate_ref, rank_ref, cnt_ref, cnt_sc):
    i = pl.program_id(0)
    tm = x_ref.shape[0]

    @pl.when(i == 0)
    def _():
        cnt_sc[...] = jnp.zeros(cnt_sc.shape, F32)

    a = ont_ref[...]
    a = (a * lax.rsqrt(jnp.mean(a * a, axis=0, keepdims=True) + EPS) * ng_ref[...]).astype(BF16)
    x1 = x_ref[...] + _dot_tn(a, wo_ref[0:NSA_W, :]) + _dot(og_ref[...], wo_ref[NSA_W:, :])
    x1_ref[...] = x1
    h2f = x1 * lax.rsqrt(jnp.mean(x1 * x1, axis=-1, keepdims=True) + EPS) * fg_ref[...]
    _store_pieces(h2_ref, h2f)
    h2 = h2f.astype(BF16)

    lane = lax.broadcasted_iota(jnp.int32, (tm, LANES), 1)
    logits = jnp.where(lane < N_EXPERTS, _dot(h2, wr_ref[...]) + br_ref[...], -jnp.inf)
    onehot = jnp.zeros((tm, LANES), F32)
    idx = jnp.zeros((tm, LANES), jnp.int32)
    vals = jnp.zeros((tm, LANES), F32)
    firsts = []
    v = logits
    top0 = None
    for k in range(TOP_K):
        mx = jnp.max(v, axis=-1, keepdims=True)
        first = jnp.min(jnp.where(v == mx, lane, LANES), axis=-1, keepdims=True)
        hit = lane == first
        v = jnp.where(hit, -jnp.inf, v)
        onehot = jnp.where(hit, 1.0, onehot)
        idx = jnp.where(lane == k, first, idx)
        top0 = mx if k == 0 else top0
        vals = jnp.where(lane == k, jnp.exp(mx - top0), vals)
        firsts.append(first)
    idx_ref[...] = idx
    gate_ref[...] = vals / jnp.sum(vals, axis=-1, keepdims=True)

    r = lax.broadcasted_iota(jnp.int32, (tm, tm), 0)
    c = lax.broadcasted_iota(jnp.int32, (tm, tm), 1)
    before = jnp.where(r > c, 1.0, 0.0).astype(BF16)
    excl = cnt_sc[...] + _dot(before, onehot.astype(BF16))
    rank = jnp.zeros((tm, LANES), F32)
    for k in range(TOP_K):
        rk = jnp.sum(jnp.where(lane == firsts[k], excl, 0.0), axis=-1, keepdims=True)
        rank = jnp.where(lane == k, rk, rank)
    rank_ref[...] = rank.astype(jnp.int32)
    cnt_sc[...] = cnt_sc[...] + jnp.sum(onehot, axis=0, keepdims=True)
    cnt_ref[...] = cnt_sc[...].astype(jnp.int32)


def _out_proj(o_nsa_t, o_gdn, x2, ng, wo, fg, wr, br, tm=512):
    n, d = x2.shape
    full = lambda a: pl.BlockSpec(a.shape, lambda i: (0,) * a.ndim)
    row = lambda w: pl.BlockSpec((tm, w), lambda i: (i, 0))
    return pl.pallas_call(
        _outproj_body,
        grid=(n // tm,),
        in_specs=[pl.BlockSpec((NSA_W, tm), lambda i: (0, i)), row(GDN_W), row(d), full(ng), full(wo), full(fg),
                  full(wr), full(br)],
        out_specs=[row(d), pl.BlockSpec((d // SC_SUBROW, tm, SC_SUBROW), lambda i: (0, i, 0)),
                   row(LANES), row(LANES), row(LANES), pl.BlockSpec((1, LANES), lambda i: (0, 0))],
        out_shape=[jax.ShapeDtypeStruct((n, d), F32), jax.ShapeDtypeStruct((d // SC_SUBROW, n, SC_SUBROW), F32),
                   jax.ShapeDtypeStruct((n, LANES), jnp.int32), jax.ShapeDtypeStruct((n, LANES), F32),
                   jax.ShapeDtypeStruct((n, LANES), jnp.int32), jax.ShapeDtypeStruct((1, LANES), jnp.int32)],
        scratch_shapes=[pltpu.VMEM((1, LANES), F32)],
        compiler_params=_cparams(("arbitrary",), 48),
        name="out_proj_router",
    )(o_nsa_t, o_gdn, x2, ng, wo, fg, wr, br)


def _expert_body(be_ref, xs_ref, wg_ref, bg_ref, wu_ref, bu_ref, wd_ref, bd_ref, y_ref, wg16, wu16, wd16):
    i = pl.program_id(0)
    prev = be_ref[jnp.maximum(i - 1, 0)]
    fresh = (i == 0) | (be_ref[i] != prev)

    @pl.when(fresh)
    def _():
        wg16[...] = wg_ref[0].astype(BF16)
        wu16[...] = wu_ref[0].astype(BF16)
        wd16[...] = wd_ref[0].astype(BF16)

    x = _join_pieces(xs_ref).astype(BF16)
    gate = jnp.minimum(_dot(x, wg16[...]) + bg_ref[0], SWIGLU_LIMIT)
    up = jnp.clip(_dot(x, wu16[...]) + bu_ref[0], -SWIGLU_LIMIT, SWIGLU_LIMIT)
    glu = gate * jax.nn.sigmoid(gate * SWIGLU_ALPHA)
    _store_pieces(y_ref, _dot(((up + 1.0) * glu).astype(BF16), wd16[...]) + bd_ref[0])


def _experts(blk_e, xs, wg, bg, wu, bu, wd, bd):
    pieces, n_rows, sub = xs.shape
    d, de = wg.shape[1], wg.shape[2]
    r = MOE_ROW_BLOCK
    wspec = lambda a, b: pl.BlockSpec((1, a, b), lambda i, be: (be[i], 0, 0))
    grid_spec = pltpu.PrefetchScalarGridSpec(
        num_scalar_prefetch=1,
        grid=(n_rows // r,),
        in_specs=[pl.BlockSpec((pieces, r, sub), lambda i, be: (0, i, 0)),
                  wspec(d, de), wspec(1, de), wspec(d, de), wspec(1, de), wspec(de, d), wspec(1, d)],
        out_specs=pl.BlockSpec((pieces, r, sub), lambda i, be: (0, i, 0)),
        scratch_shapes=[pltpu.VMEM((d, de), BF16), pltpu.VMEM((d, de), BF16), pltpu.VMEM((de, d), BF16)],
    )
    return pl.pallas_call(
        _expert_body,
        grid_spec=grid_spec,
        out_shape=jax.ShapeDtypeStruct((pieces, n_rows, sub), F32),
        compiler_params=_cparams(("arbitrary",), 56),
        name="moe_experts",
    )(blk_e, xs, wg, bg, wu, bu, wd, bd)


SC_WINDOW = 128
SC_SUBROW = 256


def _sc_mesh():
    return plsc.VectorSubcoreMesh(core_axis_name="c", subcore_axis_name="s")


def _sc_dispatch(h2, dest_rows, n_rows):
    n, d = h2.shape

    @functools.partial(pl.kernel, out_type=jax.ShapeDtypeStruct((n_rows, d), h2.dtype), mesh=_sc_mesh())
    def dispatch(x_hbm, *refs):
        idx_hbm, o_hbm = refs[:TOP_K], refs[TOP_K]

        def body(x_vmem, *idx_vmem):
            for iv in idx_vmem:
                pltpu.sync_copy(x_vmem, o_hbm.at[iv.at[0]])

        pltpu.emit_pipeline(
            body,
            grid=(n // SC_WINDOW,),
            in_specs=[pl.BlockSpec((SC_WINDOW, d), lambda i: (i, 0))]
                     + [pl.BlockSpec((1, SC_WINDOW), lambda i: (0, i))] * TOP_K,
            out_specs=[],
            core_axis_name=("c", "s"),
            dimension_semantics=(pltpu.PARALLEL,),
        )(x_hbm, *idx_hbm)

    return dispatch(h2, *dest_rows)


def _sc_gather(table, idx):
    _, d = table.shape
    m = idx.shape[1]

    @functools.partial(pl.kernel, out_type=jax.ShapeDtypeStruct((m, d), table.dtype), mesh=_sc_mesh())
    def gather(t_hbm, i_hbm, o_hbm):
        def body(i_vmem, o_vmem):
            pltpu.sync_copy(t_hbm.at[i_vmem.at[0]], o_vmem)

        pltpu.emit_pipeline(
            body,
            grid=(m // SC_WINDOW,),
            in_specs=[pl.BlockSpec((1, SC_WINDOW), lambda i: (0, i))],
            out_specs=[pl.BlockSpec((SC_WINDOW, d), lambda i: (i, 0))],
            core_axis_name=("c", "s"),
            dimension_semantics=(pltpu.PARALLEL,),
        )(i_hbm, o_hbm)

    return gather(table, idx)


def _combine_body(x1_ref, y_ref, gate_ref, o_ref):
    acc = x1_ref[...]
    for k in range(TOP_K):
        acc = acc + gate_ref[:, k:k + 1] * _join_pieces(y_ref.at[k])
    o_ref[...] = acc


def _combine(x1, y4, gates, tm=512):
    n, d = x1.shape
    pieces, sub = y4.shape[1], y4.shape[3]
    row = lambda w: pl.BlockSpec((tm, w), lambda i: (i, 0))
    return pl.pallas_call(
        _combine_body,
        grid=(n // tm,),
        in_specs=[row(d), pl.BlockSpec((TOP_K, pieces, tm, sub), lambda i: (0, 0, i, 0)), row(LANES)],
        out_specs=row(d),
        out_shape=jax.ShapeDtypeStruct((n, d), F32),
        compiler_params=_cparams(("parallel",), 48),
        name="moe_combine",
    )(x1, y4, gates)


def _pad_lanes(a, width=LANES):
    return jnp.pad(a, ((0, 0), (0, width - a.shape[1])))


def _layer(x, attn_norm_g, w_in, q_g, kc_g, ks_g, kw_g, ck_pos, ck_w1, ck_b1, ck_w2, ck_b2,
           cv_pos, cv_w1, cv_b1, cv_w2, cv_b2, nsa_out_g, conv_w, a_log, dt_bias, gdn_out_g, w_out,
           ffn_g, router_w, router_b, e_wg, e_bg, e_wu, e_bu, e_wd, e_bd):
    b, t, d = x.shape
    n = b * t
    x2 = x.reshape(n, d)

    o = np.cumsum([0, NSA_W] + [NSA_GROUPS * NSA_DH] * 6 + [3 * NSA_HEADS, 3 * GDN_W, GDN_W, GDN_HEADS, GDN_HEADS])
    wq_t = w_in[:, o[0]:o[1]].T.reshape(NSA_GROUPS, NSA_HPG, NSA_DH, d)
    zq = jnp.zeros((NSA_HPG, NSA_DH, d), F32)
    wq_t = jnp.stack([jnp.concatenate([wq_t[0], zq], axis=1), jnp.concatenate([zq, wq_t[1]], axis=1)])
    wq_t = wq_t.reshape(NSA_HEADS * LANES, d).astype(BF16)
    qg1 = q_g * (NSA_DH ** -0.5 * np.log2(np.e))
    zg = jnp.zeros((NSA_DH,), F32)
    qg_col = jnp.concatenate([jnp.tile(jnp.concatenate([qg1, zg]), NSA_HPG),
                              jnp.tile(jnp.concatenate([zg, qg1]), NSA_HPG)]).reshape(NSA_HEADS * LANES, 1)
    wkv = w_in[:, o[1]:o[7]].astype(BF16)
    ones = jnp.ones((LANES,), F32)
    kg = jnp.concatenate([ones, ones, ks_g, ks_g, ones, kw_g, kw_g, ones]).reshape(1, 6 * LANES)
    wv_t = jnp.concatenate([w_in[:, o[4]:o[5]], w_in[:, o[6]:o[7]]], axis=1).T.astype(BF16)
    wg_t = w_in[:, o[7]:o[8]].T.reshape(NSA_GROUPS, NSA_HPG * 3, d)
    wg_t = jnp.pad(wg_t, ((0, 0), (0, GATE_ROWS - NSA_HPG * 3), (0, 0))).reshape(NSA_GROUPS * GATE_ROWS, d)
    wg_t = wg_t.astype(BF16)
    wab = _pad_lanes(w_in[:, o[10]:o[12]]).astype(BF16)
    wqkv = w_in[:, o[8]:o[9]].astype(BF16)
    wz = w_in[:, o[9]:o[10]].astype(BF16)

    oqt, okv, ovt, ogt, oqkv, oz, oab = _in_proj(x2, attn_norm_g.reshape(1, d), wq_t, wkv, wv_t, wg_t, wqkv, wz,
                                                 wab, qg_col, kg)

    nch = t // CMP_STRIDE
    n_cmp = (t - CMP_BLOCK) // CMP_STRIDE + 1
    half = CMP_STRIDE * NSA_DH
    xflat = okv[:, :2 * LANES].reshape(b, nch, CMP_STRIDE, 2, NSA_GROUPS, NSA_DH)
    xflat = xflat.transpose(0, 3, 4, 1, 2, 5).reshape(b, 2, NSA_GROUPS, nch, half)
    pos = jnp.stack([ck_pos, cv_pos]).reshape(2, 2, 1, half)
    w1 = jnp.stack([ck_w1, cv_w1]).reshape(2, 2, half, CMP_HIDDEN).astype(BF16)
    b1 = jnp.stack([ck_b1, cv_b1]).reshape(2, 1, CMP_HIDDEN)
    w2 = jnp.stack([ck_w2, cv_w2]).astype(BF16)
    b2 = jnp.stack([ck_b2, cv_b2]).reshape(2, 1, NSA_DH)
    w2t = jnp.stack([ck_w2.T, cv_w2.T]).astype(BF16)
    b2t = jnp.stack([ck_b2, cv_b2]).reshape(2, NSA_DH, 1)
    kc, vct = _compress(xflat, pos, w1, b1, w2, b2, w2t, b2t, kc_g.reshape(1, NSA_DH), n_cmp)

    n_slc = t // SLC_BLOCK
    n_top = min(SLC_TOPK, n_slc)
    nblk = max(n_slc, LANES)
    kt = min(512, t // 2)
    assert (t // kt) % 2 == 0
    ci = np.arange(nch)[None, :] * CMP_STRIDE
    sj = np.arange(nblk)[:, None] * SLC_BLOCK
    overlap = ((ci < sj + SLC_BLOCK) & (ci + CMP_BLOCK > sj) & (np.arange(nch)[None, :] < n_cmp)
               & (np.arange(nblk)[:, None] < n_slc))
    expand_t = (np.arange(t)[:, None] // SLC_BLOCK) == np.arange(nblk)[None, :]
    o_nsa_t = _nsa_attention(oqt, ogt, kc, vct, okv.reshape(b, t, -1), ovt, jnp.asarray(expand_t, BF16),
                             jnp.asarray(overlap, BF16), b, t, n_cmp, n_top, kt)

    alog_row = _pad_lanes(a_log.reshape(1, GDN_HEADS))
    dtb_row = _pad_lanes(dt_bias.reshape(1, GDN_HEADS))
    o_gdn = _gdn(oqkv.reshape(b, t, -1), oz.reshape(b, t, -1), oab.reshape(b, t, -1), conv_w,
                 alog_row, dtb_row, gdn_out_g.reshape(1, GDN_DH))

    wr = _pad_lanes(router_w).astype(BF16)
    br = _pad_lanes(router_b.reshape(1, N_EXPERTS))
    x1, h2, idx, gates, rank, counts = _out_proj(
        o_nsa_t, o_gdn.reshape(n, GDN_W), x2, nsa_out_g.reshape(NSA_W, 1),
        w_out.astype(BF16), ffn_g.reshape(1, d), wr, br)

    r = MOE_ROW_BLOCK
    nk = n * TOP_K
    counts = counts[0, :N_EXPERTS]
    pcounts = (counts + r - 1) // r * r
    pends = jnp.cumsum(pcounts)
    pstarts = pends - pcounts
    top_idx = idx[:, :TOP_K]
    dest = pstarts[top_idx] + rank[:, :TOP_K]
    n_rows = (nk + r - 1) // r * r + N_EXPERTS * r
    n_blocks = n_rows // r
    blk_start = jnp.arange(n_blocks, dtype=jnp.int32)[:, None] * r
    blk_e = jnp.minimum(jnp.sum(pends[None, :] <= blk_start, axis=1), N_EXPERTS - 1).astype(jnp.int32)
    pieces = d // SC_SUBROW
    dest_p = (dest.T.astype(jnp.int32)[:, None, :]
              + (jnp.arange(pieces, dtype=jnp.int32) * n_rows)[None, :, None])
    xs = _sc_dispatch(h2.reshape(pieces * n, SC_SUBROW), [dest_p[k].reshape(1, pieces * n) for k in range(TOP_K)],
                      pieces * n_rows)
    ys = _experts(blk_e, xs.reshape(pieces, n_rows, SC_SUBROW), e_wg, e_bg.reshape(N_EXPERTS, 1, -1), e_wu,
                  e_bu.reshape(N_EXPERTS, 1, -1), e_wd, e_bd.reshape(N_EXPERTS, 1, -1))
    y4 = _sc_gather(ys.reshape(pieces * n_rows, SC_SUBROW), dest_p.reshape(1, nk * pieces))
    return _combine(x1, y4.reshape(TOP_K, pieces, n, SC_SUBROW), gates).reshape(b, t, d)


def kernel(x, attn_norm_g, w_in, nsa_q_norm_g, nsa_kc_norm_g, nsa_ks_norm_g, nsa_kw_norm_g, cmp_k_pos, cmp_k_w1, cmp_k_b1, cmp_k_w2, cmp_k_b2, cmp_v_pos, cmp_v_w1, cmp_v_b1, cmp_v_w2, cmp_v_b2, nsa_out_norm_g, gdn_conv_w, gdn_a_log, gdn_dt_bias, gdn_out_norm_g, w_out, ffn_norm_g, router_w, router_b, exp_w_gate, exp_b_gate, exp_w_up, exp_b_up, exp_w_down, exp_b_down):
    params = (attn_norm_g, w_in, nsa_q_norm_g, nsa_kc_norm_g, nsa_ks_norm_g, nsa_kw_norm_g,
              cmp_k_pos, cmp_k_w1, cmp_k_b1, cmp_k_w2, cmp_k_b2, cmp_v_pos, cmp_v_w1, cmp_v_b1, cmp_v_w2, cmp_v_b2,
              nsa_out_norm_g, gdn_conv_w, gdn_a_log, gdn_dt_bias, gdn_out_norm_g, w_out, ffn_norm_g,
              router_w, router_b, exp_w_gate, exp_b_gate, exp_w_up, exp_b_up, exp_w_down, exp_b_down)
    for l in range(attn_norm_g.shape[0]):
        x = _layer(x, *(p[l] for p in params))
    return x
```

```python
import functools

import jax
import jax.numpy as jnp
import numpy as np
from jax import lax
from jax.experimental import pallas as pl
from jax.experimental.pallas import tpu as pltpu
from jax.experimental.pallas import tpu_sc as plsc

F32 = jnp.float32
BF16 = jnp.bfloat16

EPS = 1e-6
NEG = -1e30
MASKED = -2.0 ** 100

NSA_HEADS = 8
NSA_GROUPS = 2
NSA_HPG = 4
NSA_DH = 64
CMP_BLOCK = 32
CMP_STRIDE = 16
CMP_HIDDEN = 256
SLC_BLOCK = 64
SLC_TOPK = 16
WINDOW = 512
Q_BLOCK = 128
GDN_HEADS = 4
GDN_DH = 128
GDN_CONV = 4
GDN_CHUNK = 64
N_EXPERTS = 32
TOP_K = 4
SWIGLU_LIMIT = 7.0
SWIGLU_ALPHA = 1.702
MOE_ROW_BLOCK = 256

LANES = 128
GATE_ROWS = 16
NSA_W = NSA_HEADS * NSA_DH
GDN_W = GDN_HEADS * GDN_DH

_NT = (((1,), (1,)), ((), ()))
_TN = (((0,), (0,)), ((), ()))


def _cparams(sem, vmem_mb):
    return pltpu.CompilerParams(dimension_semantics=sem, vmem_limit_bytes=vmem_mb * 1024 * 1024)


def _dot(a, b):
    return jnp.dot(a, b, preferred_element_type=F32)


def _dot_nt(a, b):
    return lax.dot_general(a, b, _NT, preferred_element_type=F32)


def _dot_tn(a, b):
    return lax.dot_general(a, b, _TN, preferred_element_type=F32)


def _join_pieces(ref):
    return jnp.concatenate([ref[j] for j in range(ref.shape[0])], axis=1)


def _store_pieces(ref, val):
    sub = ref.shape[2]
    for j in range(ref.shape[0]):
        ref[j] = val[:, j * sub:(j + 1) * sub]


def _inproj_body(x_ref, g_ref, wqt_ref, wkv_ref, wvt_ref, wgt_ref, wqkv_ref, wz_ref, wab_ref, qg_ref, kg_ref,
                 vone_ref, oqt_ref, okv_ref, ovt_ref, ogt_ref, oqkv_ref, oz_ref, oab_ref):
    x = x_ref[...]
    h = (x * lax.rsqrt(jnp.mean(x * x, axis=-1, keepdims=True) + EPS) * g_ref[...]).astype(BF16)
    tm = x.shape[0]

    yq = _dot_nt(wqt_ref[...], h)
    for s in range(NSA_HEADS):
        sl = slice(s * LANES, (s + 1) * LANES)
        ys = yq[sl, :]
        ms = jnp.sum(ys * ys, axis=0, keepdims=True) * (1.0 / NSA_DH)
        oqt_ref[sl, :] = (ys * lax.rsqrt(ms + EPS) * qg_ref[sl, :]).astype(BF16)

    ykv = _dot(h, wkv_ref[...])
    lane = lax.broadcasted_iota(jnp.int32, (tm, LANES), 1)
    low = lane < NSA_DH
    for s in range(6):
        sl = slice(s * LANES, (s + 1) * LANES)
        ys = ykv[:, sl]
        if s in (2, 4):
            y2 = ys * ys
            s0 = jnp.sum(jnp.where(low, y2, 0.0), axis=-1, keepdims=True)
            s1 = jnp.sum(jnp.where(low, 0.0, y2), axis=-1, keepdims=True)
            ms = jnp.where(low, s0, s1) * (1.0 / NSA_DH)
            ys = ys * lax.rsqrt(ms + EPS) * kg_ref[:, sl]
        okv_ref[:, sl] = ys.astype(BF16)

    ovt_ref[...] = (_dot_nt(wvt_ref[...], h) + vone_ref[...]).astype(BF16)
    ogt_ref[...] = _dot_nt(wgt_ref[...], h)
    oqkv_ref[...] = _dot(h, wqkv_ref[...]).astype(BF16)
    oz_ref[...] = _dot(h, wz_ref[...]).astype(BF16)
    oab_ref[...] = _dot(h, wab_ref[...])


def _in_proj(x2, g, wqt, wkv, wvt, wgt, wqkv, wz, wab, qg, kg, vone, tm=512):
    n, d = x2.shape
    full = lambda a: pl.BlockSpec(a.shape, lambda i: (0,) * a.ndim)
    row = lambda w: pl.BlockSpec((tm, w), lambda i: (i, 0))
    colb = lambda r: pl.BlockSpec((r, tm), lambda i: (0, i))
    return pl.pallas_call(
        _inproj_body,
        grid=(n // tm,),
        in_specs=[row(d)] + [full(a) for a in (g, wqt, wkv, wvt, wgt, wqkv, wz, wab, qg, kg, vone)],
        out_specs=[colb(wqt.shape[0]), row(wkv.shape[1]), colb(wvt.shape[0]), colb(wgt.shape[0]),
                   row(wqkv.shape[1]), row(wz.shape[1]), row(wab.shape[1])],
        out_shape=[jax.ShapeDtypeStruct((wqt.shape[0], n), BF16), jax.ShapeDtypeStruct((n, wkv.shape[1]), BF16),
                   jax.ShapeDtypeStruct((wvt.shape[0], n), BF16), jax.ShapeDtypeStruct((wgt.shape[0], n), F32),
                   jax.ShapeDtypeStruct((n, wqkv.shape[1]), BF16), jax.ShapeDtypeStruct((n, wz.shape[1]), BF16),
                   jax.ShapeDtypeStruct((n, wab.shape[1]), F32)],
        compiler_params=_cparams(("parallel",), 56),
        name="in_proj",
    )(x2, g, wqt, wkv, wvt, wgt, wqkv, wz, wab, qg, kg, vone)


def _compress_body(x_ref, pos_ref, w1_ref, b1_ref, w2_ref, b2_ref, w2t_ref, b2t_ref, g_ref, ok_ref, ovt_ref,
                   *, n_cmp):
    is_key = pl.program_id(1) == 0
    nch = x_ref.shape[3]
    hids = []
    for grp in range(NSA_GROUPS):
        x = x_ref[0, 0, grp].astype(F32)
        xa = (x + pos_ref[0, 0]).astype(BF16)
        xb = (x + pos_ref[0, 1]).astype(BF16)
        a = _dot(xa, w1_ref[0, 0])
        b = _dot(xb, w1_ref[0, 1])
        b_next = pltpu.roll(b, nch - 1, 0)
        hids.append(jax.nn.gelu(a + b_next + b1_ref[0]).astype(BF16))

    @pl.when(is_key)
    def _():
        row = lax.broadcasted_iota(jnp.int32, (nch, NSA_DH), 0)
        outs = []
        for grp in range(NSA_GROUPS):
            out = _dot(hids[grp], w2_ref[0]) + b2_ref[0]
            out = out * lax.rsqrt(jnp.mean(out * out, axis=-1, keepdims=True) + EPS) * g_ref[...]
            outs.append(jnp.where(row < n_cmp, out, 0.0))
        ok_ref[0] = jnp.concatenate(outs, axis=-1).astype(BF16)

    @pl.when(jnp.logical_not(is_key))
    def _():
        col = lax.broadcasted_iota(jnp.int32, (NSA_DH, nch), 1)
        outs = []
        for grp in range(NSA_GROUPS):
            out = _dot_nt(w2t_ref[0], hids[grp]) + b2t_ref[0]
            outs.append(jnp.where(col < n_cmp, out, 0.0))
        ovt_ref[0] = jnp.concatenate(outs, axis=0).astype(BF16)


def _compress(xflat, pos, w1, b1, w2, b2, w2t, b2t, kc_g, n_cmp):
    b, _, _, nch, flat = xflat.shape
    return pl.pallas_call(
        functools.partial(_compress_body, n_cmp=n_cmp),
        grid=(b, 2),
        in_specs=[
            pl.BlockSpec((1, 1, NSA_GROUPS, nch, flat), lambda i, j: (i, j, 0, 0, 0)),
            pl.BlockSpec((1, 2, 1, flat), lambda i, j: (j, 0, 0, 0)),
            pl.BlockSpec((1, 2, flat, CMP_HIDDEN), lambda i, j: (j, 0, 0, 0)),
            pl.BlockSpec((1, 1, CMP_HIDDEN), lambda i, j: (j, 0, 0)),
            pl.BlockSpec((1, CMP_HIDDEN, NSA_DH), lambda i, j: (j, 0, 0)),
            pl.BlockSpec((1, 1, NSA_DH), lambda i, j: (j, 0, 0)),
            pl.BlockSpec((1, NSA_DH, CMP_HIDDEN), lambda i, j: (j, 0, 0)),
            pl.BlockSpec((1, NSA_DH, 1), lambda i, j: (j, 0, 0)),
            pl.BlockSpec((1, NSA_DH), lambda i, j: (0, 0)),
        ],
        out_specs=[pl.BlockSpec((1, nch, LANES), lambda i, j: (i, 0, 0)),
                   pl.BlockSpec((1, LANES, nch), lambda i, j: (i, 0, 0))],
        out_shape=[jax.ShapeDtypeStruct((b, nch, LANES), BF16), jax.ShapeDtypeStruct((b, LANES, nch), BF16)],
        compiler_params=_cparams(("parallel", "arbitrary"), 32),
        name="nsa_compress",
    )(xflat, pos, w1, b1, w2, b2, w2t, b2t, kc_g)


def _tile_heads(a):
    return jnp.concatenate([a] * NSA_HPG, axis=1)


def _masked_softmax_t(s, valid, any_valid=None):
    s = s + _tile_heads(jnp.where(valid, 0.0, NEG))
    p = jnp.exp2(s - jnp.max(s, axis=0, keepdims=True))
    inv = 1.0 / jnp.maximum(jnp.sum(p, axis=0, keepdims=True), 1e-30)
    if any_valid is not None:
        inv = jnp.where(_tile_heads(any_valid), inv, 0.0)
    return p * inv


def _nsa_body(qt_ref, gt_ref, kc_ref, vct_ref, ks_ref, kw_ref, vst_ref, vwt_ref, et_ref, ov_ref, o_ref, acc_sc, s_sc,
              *, n_cmp, n_top, kt):
    grp = pl.program_id(1)
    s0 = pl.program_id(2) * Q_BLOCK
    nch = kc_ref.shape[1]
    nblk = ov_ref.shape[0]

    qt = jnp.concatenate([qt_ref[h * LANES:(h + 1) * LANES, :] for h in range(NSA_HPG)], axis=1)
    t_row = s0 + lax.broadcasted_iota(jnp.int32, (1, Q_BLOCK), 1)

    cidx = lax.broadcasted_iota(jnp.int32, (nch, 1), 0)
    cvalid = (cidx * CMP_STRIDE + (CMP_BLOCK - 1) <= t_row) & (cidx < n_cmp)
    pc = _masked_softmax_t(_dot(kc_ref[0], qt), cvalid, t_row >= CMP_BLOCK - 1)
    oc = _dot(vct_ref[0], pc.astype(BF16))

    pcs = (pc[:, 0:Q_BLOCK] + pc[:, Q_BLOCK:2 * Q_BLOCK] + pc[:, 2 * Q_BLOCK:3 * Q_BLOCK]
           + pc[:, 3 * Q_BLOCK:4 * Q_BLOCK])
    pcs_hi = pcs.astype(BF16)
    pcs_lo = (pcs - pcs_hi.astype(F32)).astype(BF16)
    imp = _dot(ov_ref[...], pcs_hi) + _dot(ov_ref[...], pcs_lo)
    blk = lax.broadcasted_iota(jnp.int32, (nblk, Q_BLOCK), 0)
    cur = t_row // SLC_BLOCK
    imp = jnp.where(blk * SLC_BLOCK > t_row, NEG, imp)
    imp = jnp.where((blk == 0) | (blk == cur) | (blk == cur - 1), -NEG, imp)

    def pick_rounds(v, rounds):
        for _ in range(rounds):
            mx = jnp.max(v, axis=0, keepdims=True)
            first = jnp.min(jnp.where(v == mx, blk, nblk), axis=0, keepdims=True)
            v = jnp.where(blk == first, -jnp.inf, v)
        return v

    quarter = n_top // 4
    picked = pick_rounds(imp, quarter)

    wlen = WINDOW + Q_BLOCK
    w0 = pl.multiple_of(jnp.maximum(s0 - WINDOW, 0), Q_BLOCK)
    kpos = w0 + lax.broadcasted_iota(jnp.int32, (wlen, 1), 0)
    wbias = jnp.where((kpos <= t_row) & (kpos > t_row - WINDOW), 0.0, NEG)
    sw = _dot(kw_ref[0, pl.ds(w0, wlen), :], qt) + _tile_heads(wbias)
    picked = pick_rounds(picked, quarter)
    pw = jnp.exp2(sw - jnp.max(sw, axis=0, keepdims=True)).astype(BF16)
    picked = pick_rounds(picked, quarter)
    ow = _dot(vwt_ref[:, pl.ds(w0, wlen)], pw)
    ow = ow[:NSA_DH] / ow[NSA_DH:NSA_DH + 1]

    d0 = pl.multiple_of(s0, Q_BLOCK)
    dpos = s0 + lax.broadcasted_iota(jnp.int32, (Q_BLOCK, 1), 0)
    sd = _dot(ks_ref[0, pl.ds(d0, Q_BLOCK), :], qt) + _tile_heads(jnp.where(dpos <= t_row, 0.0, NEG))
    m_diag = jnp.max(sd, axis=0, keepdims=True)
    acc_sc[0] = _dot(vst_ref[:, pl.ds(d0, Q_BLOCK)], jnp.exp2(sd - m_diag).astype(BF16))
    acc_sc[1] = jnp.zeros(acc_sc.shape[1:], F32)
    picked = pick_rounds(picked, n_top - 3 * quarter)

    before = blk * SLC_BLOCK < s0
    selb = jnp.where((picked == -jnp.inf) & before, 0.0, MASKED).astype(BF16)
    rhs = jnp.concatenate([qt, _tile_heads(selb)], axis=0)

    last_tile = ks_ref.shape[1] // kt - 1

    def scores(idx, slot):
        k0 = pl.multiple_of(jnp.minimum(idx, last_tile) * kt, kt)
        lhs = jnp.concatenate([ks_ref[0, pl.ds(k0, kt), :], et_ref[pl.ds(k0, kt), :]], axis=1)
        s_sc[slot] = _dot(lhs, rhs)

    def update(idx, slot, m_old, acc_ref):
        k0 = pl.multiple_of(idx * kt, kt)
        m_new = jnp.maximum(m_old, jnp.max(s_sc[slot], axis=0, keepdims=True))
        p = jnp.exp2(s_sc[slot] - m_new).astype(BF16)
        acc_ref[...] = jnp.exp2(m_old - m_new) * acc_ref[...] + _dot(vst_ref[:, pl.ds(k0, kt)], p)
        return m_new

    def four_tiles(j, carry):
        m0, m1 = carry
        i = 4 * j
        scores(i + 2, 2)
        m0 = update(i, 0, m0, acc_sc.at[0])
        scores(i + 3, 3)
        m1 = update(i + 1, 1, m1, acc_sc.at[1])
        scores(i + 4, 0)
        m0 = update(i + 2, 2, m0, acc_sc.at[0])
        scores(i + 5, 1)
        m1 = update(i + 3, 3, m1, acc_sc.at[1])
        return m0, m1

    n_tiles = (s0 + kt - 1) // kt
    scores(0, 0)
    scores(1, 1)
    m0, m1 = lax.fori_loop(0, (n_tiles + 3) // 4, four_tiles,
                           (m_diag, jnp.full((1, NSA_HPG * Q_BLOCK), NEG, F32)))
    m_fin = jnp.maximum(m0, m1)
    acc = acc_sc[0] * jnp.exp2(m0 - m_fin) + acc_sc[1] * jnp.exp2(m1 - m_fin)
    osl = acc[:NSA_DH] / acc[NSA_DH:NSA_DH + 1]

    oc = jnp.where(grp == 0, oc[:NSA_DH], oc[NSA_DH:])
    gts = jax.nn.sigmoid(gt_ref[...])
    for h in range(NSA_HPG):
        cols = slice(h * Q_BLOCK, (h + 1) * Q_BLOCK)
        o_ref[h * NSA_DH:(h + 1) * NSA_DH, :] = (
            gts[3 * h:3 * h + 1, :] * oc[:, cols] + gts[3 * h + 1:3 * h + 2, :] * osl[:, cols]
            + gts[3 * h + 2:3 * h + 3, :] * ow[:, cols])


def _nsa_attention(qt, gt, kc, vct, okv, vt, expand_t, overlap, b, t, n_cmp, n_top, kt):
    nch = kc.shape[1]
    nq = t // Q_BLOCK
    n = b * t
    return pl.pallas_call(
        functools.partial(_nsa_body, n_cmp=n_cmp, n_top=n_top, kt=kt),
        grid=(b, NSA_GROUPS, nq),
        in_specs=[
            pl.BlockSpec((NSA_HPG * LANES, Q_BLOCK), lambda bi, g, i: (g, bi * nq + i)),
            pl.BlockSpec((GATE_ROWS, Q_BLOCK), lambda bi, g, i: (g, bi * nq + i)),
            pl.BlockSpec((1, nch, LANES), lambda bi, g, i: (bi, 0, 0)),
            pl.BlockSpec((1, LANES, nch), lambda bi, g, i: (bi, 0, 0)),
            pl.BlockSpec((1, t, LANES), lambda bi, g, i: (bi, 0, 2)),
            pl.BlockSpec((1, t, LANES), lambda bi, g, i: (bi, 0, 4)),
            pl.BlockSpec((LANES, t), lambda bi, g, i: (g, bi)),
            pl.BlockSpec((LANES, t), lambda bi, g, i: (NSA_GROUPS + g, bi)),
            pl.BlockSpec(expand_t.shape, lambda bi, g, i: (0, 0)),
            pl.BlockSpec(overlap.shape, lambda bi, g, i: (0, 0)),
        ],
        out_specs=pl.BlockSpec((NSA_HPG * NSA_DH, Q_BLOCK), lambda bi, g, i: (g, bi * nq + i)),
        out_shape=jax.ShapeDtypeStruct((NSA_W, n), F32),
        scratch_shapes=[pltpu.VMEM((2, LANES, NSA_HPG * Q_BLOCK), F32),
                        pltpu.VMEM((4, kt, NSA_HPG * Q_BLOCK), F32)],
        compiler_params=_cparams(("parallel", "parallel", "arbitrary"), 56),
        name="nsa_attention",
    )(qt, gt, kc, vct, okv, okv, vt, vt, expand_t, overlap)


def _split_bf16(a):
    hi = a.astype(BF16)
    return hi, (a - hi.astype(F32)).astype(BF16)


def _split_lhs(a):
    hi, lo = _split_bf16(a)
    return jnp.concatenate([hi, hi, lo], axis=1)


def _split_rhs(b):
    hi, lo = _split_bf16(b)
    return jnp.concatenate([hi, lo, hi], axis=0)


def _unit_lower_inverses(lmats):
    c = lmats[0].shape[0]
    r = lax.broadcasted_iota(jnp.int32, (c, c), 0)
    col = lax.broadcasted_iota(jnp.int32, (c, c), 1)
    eye = jnp.where(r == col, 1.0, 0.0)
    xs = [eye - l for l in lmats]
    ps = [_dot(_split_lhs(l), _split_rhs(l)) for l in lmats]
    steps = int(np.log2(c)) - 1
    for s in range(steps):
        last = s + 1 == steps
        for i in range(len(lmats)):
            rhs = _split_rhs(ps[i])
            if last:
                xs[i] = xs[i] + _dot(_split_lhs(xs[i]), rhs)
            else:
                both = _dot(_split_lhs(jnp.concatenate([xs[i], ps[i]], axis=0)), rhs)
                xs[i] = xs[i] + both[:c]
                ps[i] = both[c:]
    return xs


def _gdn_body(x_ref, z_ref, ab_ref, cw_ref, alog_ref, dtb_ref, og_ref, o_ref, xbuf, s_sc, *, ct):
    c = pl.program_id(1)
    halo = 8

    @pl.when(c == 0)
    def _():
        xbuf[0:halo, :] = jnp.zeros((halo, xbuf.shape[1]), F32)
        s_sc[...] = jnp.zeros(s_sc.shape, F32)

    @pl.when(c > 0)
    def _():
        xbuf[0:halo, :] = xbuf[ct:ct + halo, :]

    xbuf[halo:halo + ct, :] = x_ref[0].astype(F32)
    y = cw_ref[0:1, :] * xbuf[pl.ds(halo - 3, ct), :]
    for k in range(1, GDN_CONV):
        y = y + cw_ref[k:k + 1, :] * xbuf[pl.ds(halo - 3 + k, ct), :]
    y = y * jax.nn.sigmoid(y)

    ab = ab_ref[0]
    g_all = -jnp.exp(alog_ref[...]) * jax.nn.softplus(ab + dtb_ref[...])
    beta_all = jax.nn.sigmoid(ab)

    ch = GDN_CHUNK
    r = lax.broadcasted_iota(jnp.int32, (ch, ch), 0)
    col = lax.broadcasted_iota(jnp.int32, (ch, ch), 1)
    incl = r >= col
    strict = r > col
    tril = jnp.where(incl, 1.0, 0.0)

    tril16 = jnp.concatenate([tril.astype(BF16)] * 3, axis=1)
    units = []
    for ci in range(ct // ch):
        rows = slice(ci * ch, (ci + 1) * ch)
        g_hi, g_lo = _split_bf16(g_all[rows])
        g_lo2 = (g_all[rows] - g_hi.astype(F32) - g_lo.astype(F32)).astype(BF16)
        gc_all = _dot(tril16, jnp.concatenate([g_hi, g_lo, g_lo2], axis=0))
        gc_t = gc_all.T
        for h in range(GDN_HEADS):
            hs = slice(h * GDN_DH, (h + 1) * GDN_DH)
            qh = y[rows, hs]
            kh = y[rows, GDN_W + h * GDN_DH:GDN_W + (h + 1) * GDN_DH]
            vh = y[rows, 2 * GDN_W + h * GDN_DH:2 * GDN_W + (h + 1) * GDN_DH]
            qh = qh * lax.rsqrt(jnp.sum(qh * qh, axis=-1, keepdims=True) + EPS) * (GDN_DH ** -0.5)
            kh = kh * lax.rsqrt(jnp.sum(kh * kh, axis=-1, keepdims=True) + EPS)
            gc = gc_all[:, h:h + 1]
            gr = gc_t[h:h + 1, :]
            g_last = gc_all[ch - 1:ch, h:h + 1]
            beta = beta_all[rows, GDN_HEADS + h:GDN_HEADS + h + 1]
            eg = jnp.exp(gc)
            decay = jnp.where(incl, jnp.exp(jnp.minimum(gc - gr, 0.0)), 0.0)
            kb = kh * beta
            k16 = kh.astype(BF16)
            units.append(dict(
                rows=rows, h=h,
                lmat=jnp.where(strict, _dot_nt(kb.astype(BF16), k16) * decay, 0.0),
                vb=(vh * beta).astype(BF16), kbg=(kb * eg).astype(BF16),
                qk=jnp.where(incl, _dot_nt(qh.astype(BF16), k16) * decay, 0.0).astype(BF16),
                qg=(qh * eg).astype(BF16), kd=(kh * jnp.exp(g_last - gc)).astype(BF16),
                gl=jnp.exp(g_last)))
    tinvs = _unit_lower_inverses([u["lmat"] for u in units])
    for u, tinv in zip(units, tinvs):
        tinv = tinv.astype(BF16)
        u["u"] = _dot(tinv, u["vb"])
        u["w"] = _dot(tinv, u["kbg"]).astype(BF16)

    for u in units:
        h, rows = u["h"], u["rows"]
        hs = slice(h * GDN_DH, (h + 1) * GDN_DH)
        s_old = s_sc[h]
        s16 = s_old.astype(BF16)
        v_new = (u["u"] - _dot(u["w"], s16)).astype(BF16)
        o = _dot(u["qg"], s16) + _dot(u["qk"], v_new)
        s_sc[h] = s_old * u["gl"] + _dot_tn(u["kd"], v_new)
        on = o * lax.rsqrt(jnp.mean(o * o, axis=-1, keepdims=True) + EPS) * og_ref[...]
        zh = z_ref[0, rows, hs].astype(F32)
        o_ref[0, rows, hs] = (on * (zh * jax.nn.sigmoid(zh))).astype(BF16)


def _gdn(oqkv, oz, oab, conv_w, alog, dtb, og, ct=128):
    b, t, w3 = oqkv.shape
    full = lambda a: pl.BlockSpec(a.shape, lambda bi, c: (0,) * a.ndim)
    return pl.pallas_call(
        functools.partial(_gdn_body, ct=ct),
        grid=(b, t // ct),
        in_specs=[
            pl.BlockSpec((1, ct, w3), lambda bi, c: (bi, c, 0)),
            pl.BlockSpec((1, ct, GDN_W), lambda bi, c: (bi, c, 0)),
            pl.BlockSpec((1, ct, LANES), lambda bi, c: (bi, c, 0)),
            full(conv_w), full(alog), full(dtb), full(og),
        ],
        out_specs=pl.BlockSpec((1, ct, GDN_W), lambda bi, c: (bi, c, 0)),
        out_shape=jax.ShapeDtypeStruct((b, t, GDN_W), BF16),
        scratch_shapes=[pltpu.VMEM((ct + 8, w3), F32), pltpu.VMEM((GDN_HEADS, GDN_DH, GDN_DH), F32)],
        compiler_params=_cparams(("parallel", "arbitrary"), 32),
        name="gdn",
    )(oqkv, oz, oab, conv_w, alog, dtb, og)


def _outproj_body(ont_ref, og_ref, x_ref, ng_ref, wo_ref, fg_ref, wr_ref, br_ref,
                  x1_ref, h2_ref, idx_ref, gate_ref, rank_ref, cnt_ref, cnt_sc):
    i = pl.program_id(0)
    tm = x_ref.shape[0]

    @pl.when(i == 0)
    def _():
        cnt_sc[...] = jnp.zeros(cnt_sc.shape, F32)

    a = ont_ref[...]
    a = (a * lax.rsqrt(jnp.mean(a * a, axis=0, keepdims=True) + EPS) * ng_ref[...]).astype(BF16)
    x1 = x_ref[...] + _dot_tn(a, wo_ref[0:NSA_W, :]) + _dot(og_ref[...], wo_ref[NSA_W:, :])
    x1_ref[...] = x1
    h2f = x1 * lax.rsqrt(jnp.mean(x1 * x1, axis=-1, keepdims=True) + EPS) * fg_ref[...]
    _store_pieces(h2_ref, h2f)
    h2 = h2f.astype(BF16)

    lane = lax.broadcasted_iota(jnp.int32, (tm, LANES), 1)
    logits = jnp.where(lane < N_EXPERTS, _dot(h2, wr_ref[...]) + br_ref[...], -jnp.inf)
    onehot = jnp.zeros((tm, LANES), F32)
    idx = jnp.zeros((tm, LANES), jnp.int32)
    vals = jnp.zeros((tm, LANES), F32)
    firsts = []
    v = logits
    top0 = None
    for k in range(TOP_K):
        mx = jnp.max(v, axis=-1, keepdims=True)
        first = jnp.min(jnp.where(v == mx, lane, LANES), axis=-1, keepdims=True)
        hit = lane == first
        v = jnp.where(hit, -jnp.inf, v)
        onehot = jnp.where(hit, 1.0, onehot)
        idx = jnp.where(lane == k, first, idx)
        top0 = mx if k == 0 else top0
        vals = jnp.where(lane == k, jnp.exp(mx - top0), vals)
        firsts.append(first)
    idx_ref[...] = idx
    gate_ref[...] = vals / jnp.sum(vals, axis=-1, keepdims=True)

    r = lax.broadcasted_iota(jnp.int32, (tm, tm), 0)
    c = lax.broadcasted_iota(jnp.int32, (tm, tm), 1)
    before = jnp.where(r > c, 1.0, 0.0).astype(BF16)
    excl = cnt_sc[...] + _dot(before, onehot.astype(BF16))
    rank = jnp.zeros((tm, LANES), F32)
    for k in range(TOP_K):
        rk = jnp.sum(jnp.where(lane == firsts[k], excl, 0.0), axis=-1, keepdims=True)
        rank = jnp.where(lane == k, rk, rank)
    rank_ref[...] = rank.astype(jnp.int32)
    cnt_sc[...] = cnt_sc[...] + jnp.sum(onehot, axis=0, keepdims=True)
    cnt_ref[...] = cnt_sc[...].astype(jnp.int32)


def _out_proj(o_nsa_t, o_gdn, x2, ng, wo, fg, wr, br, tm=512):
    n, d = x2.shape
    full = lambda a: pl.BlockSpec(a.shape, lambda i: (0,) * a.ndim)
    row = lambda w: pl.BlockSpec((tm, w), lambda i: (i, 0))
    return pl.pallas_call(
        _outproj_body,
        grid=(n // tm,),
        in_specs=[pl.BlockSpec((NSA_W, tm), lambda i: (0, i)), row(GDN_W), row(d), full(ng), full(wo), full(fg),
                  full(wr), full(br)],
        out_specs=[row(d), pl.BlockSpec((d // SC_SUBROW, tm, SC_SUBROW), lambda i: (0, i, 0)),
                   row(LANES), row(LANES), row(LANES), pl.BlockSpec((1, LANES), lambda i: (0, 0))],
        out_shape=[jax.ShapeDtypeStruct((n, d), F32), jax.ShapeDtypeStruct((d // SC_SUBROW, n, SC_SUBROW), F32),
                   jax.ShapeDtypeStruct((n, LANES), jnp.int32), jax.ShapeDtypeStruct((n, LANES), F32),
                   jax.ShapeDtypeStruct((n, LANES), jnp.int32), jax.ShapeDtypeStruct((1, LANES), jnp.int32)],
        scratch_shapes=[pltpu.VMEM((1, LANES), F32)],
        compiler_params=_cparams(("arbitrary",), 48),
        name="out_proj_router",
    )(o_nsa_t, o_gdn, x2, ng, wo, fg, wr, br)


def _expert_body(be_ref, xs_ref, wg_ref, bg_ref, wu_ref, bu_ref, wd_ref, bd_ref, y_ref, wg16, wu16, wd16):
    i = pl.program_id(0)
    prev = be_ref[jnp.maximum(i - 1, 0)]
    fresh = (i == 0) | (be_ref[i] != prev)

    @pl.when(fresh)
    def _():
        wg16[...] = wg_ref[0].astype(BF16)
        wu16[...] = wu_ref[0].astype(BF16)
        wd16[...] = wd_ref[0].astype(BF16)

    x = _join_pieces(xs_ref).astype(BF16)
    gate = jnp.minimum(_dot(x, wg16[...]) + bg_ref[0], SWIGLU_LIMIT)
    up = jnp.clip(_dot(x, wu16[...]) + bu_ref[0], -SWIGLU_LIMIT, SWIGLU_LIMIT)
    glu = gate * jax.nn.sigmoid(gate * SWIGLU_ALPHA)
    _store_pieces(y_ref, _dot(((up + 1.0) * glu).astype(BF16), wd16[...]) + bd_ref[0])


def _experts(blk_e, xs, wg, bg, wu, bu, wd, bd):
    pieces, n_rows, sub = xs.shape
    d, de = wg.shape[1], wg.shape[2]
    r = MOE_ROW_BLOCK
    wspec = lambda a, b: pl.BlockSpec((1, a, b), lambda i, be: (be[i], 0, 0))
    grid_spec = pltpu.PrefetchScalarGridSpec(
        num_scalar_prefetch=1,
        grid=(n_rows // r,),
        in_specs=[pl.BlockSpec((pieces, r, sub), lambda i, be: (0, i, 0)),
                  wspec(d, de), wspec(1, de), wspec(d, de), wspec(1, de), wspec(de, d), wspec(1, d)],
        out_specs=pl.BlockSpec((pieces, r, sub), lambda i, be: (0, i, 0)),
        scratch_shapes=[pltpu.VMEM((d, de), BF16), pltpu.VMEM((d, de), BF16), pltpu.VMEM((de, d), BF16)],
    )
    return pl.pallas_call(
        _expert_body,
        grid_spec=grid_spec,
        out_shape=jax.ShapeDtypeStruct((pieces, n_rows, sub), F32),
        compiler_params=_cparams(("arbitrary",), 56),
        name="moe_experts",
    )(blk_e, xs, wg, bg, wu, bu, wd, bd)


SC_WINDOW = 128
SC_SUBROW = 256


def _sc_mesh():
    return plsc.VectorSubcoreMesh(core_axis_name="c", subcore_axis_name="s")


def _sc_dispatch(h2, dest_rows, n_rows):
    n, d = h2.shape

    @functools.partial(pl.kernel, out_type=jax.ShapeDtypeStruct((n_rows, d), h2.dtype), mesh=_sc_mesh())
    def dispatch(x_hbm, *refs):
        idx_hbm, o_hbm = refs[:TOP_K], refs[TOP_K]

        def body(x_vmem, *idx_vmem):
            for iv in idx_vmem:
                pltpu.sync_copy(x_vmem, o_hbm.at[iv.at[0]])

        pltpu.emit_pipeline(
            body,
            grid=(n // SC_WINDOW,),
            in_specs=[pl.BlockSpec((SC_WINDOW, d), lambda i: (i, 0))]
                     + [pl.BlockSpec((1, SC_WINDOW), lambda i: (0, i))] * TOP_K,
            out_specs=[],
            core_axis_name=("c", "s"),
            dimension_semantics=(pltpu.PARALLEL,),
        )(x_hbm, *idx_hbm)

    return dispatch(h2, *dest_rows)


def _sc_gather(table, idx):
    _, d = table.shape
    m = idx.shape[1]

    @functools.partial(pl.kernel, out_type=jax.ShapeDtypeStruct((m, d), table.dtype), mesh=_sc_mesh())
    def gather(t_hbm, i_hbm, o_hbm):
        def body(i_vmem, o_vmem):
            pltpu.sync_copy(t_hbm.at[i_vmem.at[0]], o_vmem)

        pltpu.emit_pipeline(
            body,
            grid=(m // SC_WINDOW,),
            in_specs=[pl.BlockSpec((1, SC_WINDOW), lambda i: (0, i))],
            out_specs=[pl.BlockSpec((SC_WINDOW, d), lambda i: (i, 0))],
            core_axis_name=("c", "s"),
            dimension_semantics=(pltpu.PARALLEL,),
        )(i_hbm, o_hbm)

    return gather(table, idx)


def _combine_body(x1_ref, y_ref, gate_ref, o_ref):
    acc = x1_ref[...]
    for k in range(TOP_K):
        acc = acc + gate_ref[:, k:k + 1] * _join_pieces(y_ref.at[k])
    o_ref[...] = acc


def _combine(x1, y4, gates, tm=512):
    n, d = x1.shape
    pieces, sub = y4.shape[1], y4.shape[3]
    row = lambda w: pl.BlockSpec((tm, w), lambda i: (i, 0))
    return pl.pallas_call(
        _combine_body,
        grid=(n // tm,),
        in_specs=[row(d), pl.BlockSpec((TOP_K, pieces, tm, sub), lambda i: (0, 0, i, 0)), row(LANES)],
        out_specs=row(d),
        out_shape=jax.ShapeDtypeStruct((n, d), F32),
        compiler_params=_cparams(("parallel",), 48),
        name="moe_combine",
    )(x1, y4, gates)


def _pad_lanes(a, width=LANES):
    return jnp.pad(a, ((0, 0), (0, width - a.shape[1])))


def _layer(x, attn_norm_g, w_in, q_g, kc_g, ks_g, kw_g, ck_pos, ck_w1, ck_b1, ck_w2, ck_b2,
           cv_pos, cv_w1, cv_b1, cv_w2, cv_b2, nsa_out_g, conv_w, a_log, dt_bias, gdn_out_g, w_out,
           ffn_g, router_w, router_b, e_wg, e_bg, e_wu, e_bu, e_wd, e_bd):
    b, t, d = x.shape
    n = b * t
    x2 = x.reshape(n, d)

    o = np.cumsum([0, NSA_W] + [NSA_GROUPS * NSA_DH] * 6 + [3 * NSA_HEADS, 3 * GDN_W, GDN_W, GDN_HEADS, GDN_HEADS])
    wq_t = w_in[:, o[0]:o[1]].T.reshape(NSA_GROUPS, NSA_HPG, NSA_DH, d)
    zq = jnp.zeros((NSA_HPG, NSA_DH, d), F32)
    wq_t = jnp.stack([jnp.concatenate([wq_t[0], zq], axis=1), jnp.concatenate([zq, wq_t[1]], axis=1)])
    wq_t = wq_t.reshape(NSA_HEADS * LANES, d).astype(BF16)
    qg1 = q_g * (NSA_DH ** -0.5 * np.log2(np.e))
    zg = jnp.zeros((NSA_DH,), F32)
    qg_col = jnp.concatenate([jnp.tile(jnp.concatenate([qg1, zg]), NSA_HPG),
                              jnp.tile(jnp.concatenate([zg, qg1]), NSA_HPG)]).reshape(NSA_HEADS * LANES, 1)
    wkv = w_in[:, o[1]:o[7]].astype(BF16)
    ones = jnp.ones((LANES,), F32)
    kg = jnp.concatenate([ones, ones, ks_g, ks_g, ones, kw_g, kw_g, ones]).reshape(1, 6 * LANES)
    wv_t = jnp.concatenate([w_in[:, o[4]:o[5]], w_in[:, o[6]:o[7]]], axis=1).T.reshape(2 * NSA_GROUPS, NSA_DH, d)
    wv_t = jnp.pad(wv_t, ((0, 0), (0, LANES - NSA_DH), (0, 0))).reshape(2 * NSA_GROUPS * LANES, d).astype(BF16)
    vone = jnp.asarray((np.arange(2 * NSA_GROUPS * LANES) % LANES == NSA_DH).astype(np.float32)[:, None])
    wg_t = w_in[:, o[7]:o[8]].T.reshape(NSA_GROUPS, NSA_HPG * 3, d)
    wg_t = jnp.pad(wg_t, ((0, 0), (0, GATE_ROWS - NSA_HPG * 3), (0, 0))).reshape(NSA_GROUPS * GATE_ROWS, d)
    wg_t = wg_t.astype(BF16)
    wab = _pad_lanes(w_in[:, o[10]:o[12]]).astype(BF16)
    wqkv = w_in[:, o[8]:o[9]].astype(BF16)
    wz = w_in[:, o[9]:o[10]].astype(BF16)

    oqt, okv, ovt, ogt, oqkv, oz, oab = _in_proj(x2, attn_norm_g.reshape(1, d), wq_t, wkv, wv_t, wg_t, wqkv, wz,
                                                 wab, qg_col, kg, vone)

    nch = t // CMP_STRIDE
    n_cmp = (t - CMP_BLOCK) // CMP_STRIDE + 1
    half = CMP_STRIDE * NSA_DH
    xflat = okv[:, :2 * LANES].reshape(b, nch, CMP_STRIDE, 2, NSA_GROUPS, NSA_DH)
    xflat = xflat.transpose(0, 3, 4, 1, 2, 5).reshape(b, 2, NSA_GROUPS, nch, half)
    pos = jnp.stack([ck_pos, cv_pos]).reshape(2, 2, 1, half)
    w1 = jnp.stack([ck_w1, cv_w1]).reshape(2, 2, half, CMP_HIDDEN).astype(BF16)
    b1 = jnp.stack([ck_b1, cv_b1]).reshape(2, 1, CMP_HIDDEN)
    w2 = jnp.stack([ck_w2, cv_w2]).astype(BF16)
    b2 = jnp.stack([ck_b2, cv_b2]).reshape(2, 1, NSA_DH)
    w2t = jnp.stack([ck_w2.T, cv_w2.T]).astype(BF16)
    b2t = jnp.stack([ck_b2, cv_b2]).reshape(2, NSA_DH, 1)
    kc, vct = _compress(xflat, pos, w1, b1, w2, b2, w2t, b2t, kc_g.reshape(1, NSA_DH), n_cmp)

    n_slc = t // SLC_BLOCK
    n_top = min(SLC_TOPK, n_slc)
    nblk = max(n_slc, LANES)
    kt = min(256, t // 4)
    assert (t // kt) % 4 == 0
    ci = np.arange(nch)[None, :] * CMP_STRIDE
    sj = np.arange(nblk)[:, None] * SLC_BLOCK
    overlap = ((ci < sj + SLC_BLOCK) & (ci + CMP_BLOCK > sj) & (np.arange(nch)[None, :] < n_cmp)
               & (np.arange(nblk)[:, None] < n_slc))
    expand_t = (np.arange(t)[:, None] // SLC_BLOCK) == np.arange(nblk)[None, :]
    o_nsa_t = _nsa_attention(oqt, ogt, kc, vct, okv.reshape(b, t, -1), ovt, jnp.asarray(expand_t, BF16),
                             jnp.asarray(overlap, BF16), b, t, n_cmp, n_top, kt)

    alog_row = _pad_lanes(a_log.reshape(1, GDN_HEADS))
    dtb_row = _pad_lanes(dt_bias.reshape(1, GDN_HEADS))
    o_gdn = _gdn(oqkv.reshape(b, t, -1), oz.reshape(b, t, -1), oab.reshape(b, t, -1), conv_w,
                 alog_row, dtb_row, gdn_out_g.reshape(1, GDN_DH))

    wr = _pad_lanes(router_w).astype(BF16)
    br = _pad_lanes(router_b.reshape(1, N_EXPERTS))
    x1, h2, idx, gates, rank, counts = _out_proj(
        o_nsa_t, o_gdn.reshape(n, GDN_W), x2, nsa_out_g.reshape(NSA_W, 1),
        w_out.astype(BF16), ffn_g.reshape(1, d), wr, br)

    r = MOE_ROW_BLOCK
    nk = n * TOP_K
    counts = counts[0, :N_EXPERTS]
    pcounts = (counts + r - 1) // r * r
    pends = jnp.cumsum(pcounts)
    pstarts = pends - pcounts
    top_idx = idx[:, :TOP_K]
    dest = pstarts[top_idx] + rank[:, :TOP_K]
    n_rows = (nk + r - 1) // r * r + N_EXPERTS * r
    n_blocks = n_rows // r
    blk_start = jnp.arange(n_blocks, dtype=jnp.int32)[:, None] * r
    blk_e = jnp.minimum(jnp.sum(pends[None, :] <= blk_start, axis=1), N_EXPERTS - 1).astype(jnp.int32)
    pieces = d // SC_SUBROW
    dest_p = (dest.T.astype(jnp.int32)[:, None, :]
              + (jnp.arange(pieces, dtype=jnp.int32) * n_rows)[None, :, None])
    xs = _sc_dispatch(h2.reshape(pieces * n, SC_SUBROW), [dest_p[k].reshape(1, pieces * n) for k in range(TOP_K)],
                      pieces * n_rows)
    ys = _experts(blk_e, xs.reshape(pieces, n_rows, SC_SUBROW), e_wg, e_bg.reshape(N_EXPERTS, 1, -1), e_wu,
                  e_bu.reshape(N_EXPERTS, 1, -1), e_wd, e_bd.reshape(N_EXPERTS, 1, -1))
    y4 = _sc_gather(ys.reshape(pieces * n_rows, SC_SUBROW), dest_p.reshape(1, nk * pieces))
    return _combine(x1, y4.reshape(TOP_K, pieces, n, SC_SUBROW), gates).reshape(b, t, d)


def kernel(x, attn_norm_g, w_in, nsa_q_norm_g, nsa_kc_norm_g, nsa_ks_norm_g, nsa_kw_norm_g, cmp_k_pos, cmp_k_w1, cmp_k_b1, cmp_k_w2, cmp_k_b2, cmp_v_pos, cmp_v_w1, cmp_v_b1, cmp_v_w2, cmp_v_b2, nsa_out_norm_g, gdn_conv_w, gdn_a_log, gdn_dt_bias, gdn_out_norm_g, w_out, ffn_norm_g, router_w, router_b, exp_w_gate, exp_b_gate, exp_w_up, exp_b_up, exp_w_down, exp_b_down):
    params = (attn_norm_g, w_in, nsa_q_norm_g, nsa_kc_norm_g, nsa_ks_norm_g, nsa_kw_norm_g,
              cmp_k_pos, cmp_k_w1, cmp_k_b1, cmp_k_w2, cmp_k_b2, cmp_v_pos, cmp_v_w1, cmp_v_b1, cmp_v_w2, cmp_v_b2,
              nsa_out_norm_g, gdn_conv_w, gdn_a_log, gdn_dt_bias, gdn_out_norm_g, w_out, ffn_norm_g,
              router_w, router_b, exp_w_gate, exp_b_gate, exp_w_up, exp_b_up, exp_w_down, exp_b_down)
    for l in range(attn_norm_g.shape[0]):
        x = _layer(x, *(p[l] for p in params))
    return x
```

```python
import functools

import jax
import jax.numpy as jnp
import numpy as np
from jax import lax
from jax.experimental import pallas as pl
from jax.experimental.pallas import tpu as pltpu
from jax.experimental.pallas import tpu_sc as plsc

F32 = jnp.float32
BF16 = jnp.bfloat16

EPS = 1e-6
NEG = -1e30
MASKED = -2.0 ** 100

NSA_HEADS = 8
NSA_GROUPS = 2
NSA_HPG = 4
NSA_DH = 64
CMP_BLOCK = 32
CMP_STRIDE = 16
CMP_HIDDEN = 256
SLC_BLOCK = 64
SLC_TOPK = 16
WINDOW = 512
Q_BLOCK = 128
GDN_HEADS = 4
GDN_DH = 128
GDN_CONV = 4
GDN_CHUNK = 64
N_EXPERTS = 32
TOP_K = 4
SWIGLU_LIMIT = 7.0
SWIGLU_ALPHA = 1.702
MOE_ROW_BLOCK = 256

LANES = 128
GATE_ROWS = 16
NSA_W = NSA_HEADS * NSA_DH
GDN_W = GDN_HEADS * GDN_DH

_NT = (((1,), (1,)), ((), ()))
_TN = (((0,), (0,)), ((), ()))


def _cparams(sem, vmem_mb):
    return pltpu.CompilerParams(dimension_semantics=sem, vmem_limit_bytes=vmem_mb * 1024 * 1024)


def _dot(a, b):
    return jnp.dot(a, b, preferred_element_type=F32)


def _dot_nt(a, b):
    return lax.dot_general(a, b, _NT, preferred_element_type=F32)


def _dot_tn(a, b):
    return lax.dot_general(a, b, _TN, preferred_element_type=F32)


def _join_pieces(ref):
    return jnp.concatenate([ref[j] for j in range(ref.shape[0])], axis=1)


def _store_pieces(ref, val):
    sub = ref.shape[2]
    for j in range(ref.shape[0]):
        ref[j] = val[:, j * sub:(j + 1) * sub]


def _inproj_body(x_ref, g_ref, wqt_ref, wkv_ref, wvt_ref, wgt_ref, wqkv_ref, wz_ref, wab_ref, qg_ref, kg_ref,
                 vone_ref, cw_ref, oqt_ref, okv_ref, ovt_ref, ogt_ref, oqkv_ref, oz_ref, oab_ref, ybuf,
                 *, tiles_per_seq):
    x = x_ref[...]
    h = (x * lax.rsqrt(jnp.mean(x * x, axis=-1, keepdims=True) + EPS) * g_ref[...]).astype(BF16)
    tm = x.shape[0]

    yq = _dot_nt(wqt_ref[...], h)
    for s in range(NSA_HEADS):
        sl = slice(s * LANES, (s + 1) * LANES)
        ys = yq[sl, :]
        ms = jnp.sum(ys * ys, axis=0, keepdims=True) * (1.0 / NSA_DH)
        oqt_ref[sl, :] = (ys * lax.rsqrt(ms + EPS) * qg_ref[sl, :]).astype(BF16)

    ykv = _dot(h, wkv_ref[...])
    lane = lax.broadcasted_iota(jnp.int32, (tm, LANES), 1)
    low = lane < NSA_DH
    for s in range(6):
        sl = slice(s * LANES, (s + 1) * LANES)
        ys = ykv[:, sl]
        if s in (2, 4):
            y2 = ys * ys
            s0 = jnp.sum(jnp.where(low, y2, 0.0), axis=-1, keepdims=True)
            s1 = jnp.sum(jnp.where(low, 0.0, y2), axis=-1, keepdims=True)
            ms = jnp.where(low, s0, s1) * (1.0 / NSA_DH)
            ys = ys * lax.rsqrt(ms + EPS) * kg_ref[:, sl]
        okv_ref[:, sl] = ys.astype(BF16)

    ovt_ref[...] = (_dot_nt(wvt_ref[...], h) + vone_ref[...]).astype(BF16)
    ogt_ref[...] = _dot_nt(wgt_ref[...], h)
    oz_ref[...] = _dot(h, wz_ref[...]).astype(BF16)
    oab_ref[...] = _dot(h, wab_ref[...])

    halo = ybuf.shape[0] - tm
    first = pl.program_id(0) % tiles_per_seq == 0

    @pl.when(first)
    def _():
        ybuf[0:halo, :] = jnp.zeros((halo, ybuf.shape[1]), F32)

    @pl.when(jnp.logical_not(first))
    def _():
        ybuf[0:halo, :] = ybuf[tm:tm + halo, :]

    ybuf[halo:halo + tm, :] = _dot(h, wqkv_ref[...])
    taps = cw_ref.shape[0]
    y = cw_ref[0:1, :] * ybuf[pl.ds(halo - taps + 1, tm), :]
    for k in range(1, taps):
        y = y + cw_ref[k:k + 1, :] * ybuf[pl.ds(halo - taps + 1 + k, tm), :]
    hy = 0.5 * y
    y = hy + hy * jnp.tanh(hy)
    for s in range(3 * GDN_HEADS):
        sl = slice(s * GDN_DH, (s + 1) * GDN_DH)
        ys = y[:, sl]
        if s < 2 * GDN_HEADS:
            scale = GDN_DH ** -0.5 if s < GDN_HEADS else 1.0
            ys = ys * (lax.rsqrt(jnp.sum(ys * ys, axis=-1, keepdims=True) + EPS) * scale)
        oqkv_ref[:, sl] = ys.astype(BF16)


def _in_proj(x2, g, wqt, wkv, wvt, wgt, wqkv, wz, wab, qg, kg, vone, conv_w, tiles_per_seq, tm):
    n, d = x2.shape
    full = lambda a: pl.BlockSpec(a.shape, lambda i: (0,) * a.ndim)
    row = lambda w: pl.BlockSpec((tm, w), lambda i: (i, 0))
    colb = lambda r: pl.BlockSpec((r, tm), lambda i: (0, i))
    return pl.pallas_call(
        functools.partial(_inproj_body, tiles_per_seq=tiles_per_seq),
        grid=(n // tm,),
        in_specs=[row(d)] + [full(a) for a in (g, wqt, wkv, wvt, wgt, wqkv, wz, wab, qg, kg, vone, conv_w)],
        out_specs=[colb(wqt.shape[0]), row(wkv.shape[1]), colb(wvt.shape[0]), colb(wgt.shape[0]),
                   row(wqkv.shape[1]), row(wz.shape[1]), row(wab.shape[1])],
        out_shape=[jax.ShapeDtypeStruct((wqt.shape[0], n), BF16), jax.ShapeDtypeStruct((n, wkv.shape[1]), BF16),
                   jax.ShapeDtypeStruct((wvt.shape[0], n), BF16), jax.ShapeDtypeStruct((wgt.shape[0], n), F32),
                   jax.ShapeDtypeStruct((n, wqkv.shape[1]), BF16), jax.ShapeDtypeStruct((n, wz.shape[1]), BF16),
                   jax.ShapeDtypeStruct((n, wab.shape[1]), F32)],
        scratch_shapes=[pltpu.VMEM((tm + 8, wqkv.shape[1]), F32)],
        compiler_params=_cparams(("arbitrary",), 56),
        name="in_proj",
    )(x2, g, wqt, wkv, wvt, wgt, wqkv, wz, wab, qg, kg, vone, conv_w)


def _compress_body(x_ref, pos_ref, w1_ref, b1_ref, w2_ref, b2_ref, w2t_ref, b2t_ref, g_ref, ok_ref, ovt_ref,
                   *, n_cmp):
    is_key = pl.program_id(1) == 0
    nch = x_ref.shape[3]
    hids = []
    for grp in range(NSA_GROUPS):
        x = x_ref[0, 0, grp].astype(F32)
        xa = (x + pos_ref[0, 0]).astype(BF16)
        xb = (x + pos_ref[0, 1]).astype(BF16)
        a = _dot(xa, w1_ref[0, 0])
        b = _dot(xb, w1_ref[0, 1])
        b_next = pltpu.roll(b, nch - 1, 0)
        hids.append(jax.nn.gelu(a + b_next + b1_ref[0]).astype(BF16))

    @pl.when(is_key)
    def _():
        row = lax.broadcasted_iota(jnp.int32, (nch, NSA_DH), 0)
        outs = []
        for grp in range(NSA_GROUPS):
            out = _dot(hids[grp], w2_ref[0]) + b2_ref[0]
            out = out * lax.rsqrt(jnp.mean(out * out, axis=-1, keepdims=True) + EPS) * g_ref[...]
            outs.append(jnp.where(row < n_cmp, out, 0.0))
        ok_ref[0] = jnp.concatenate(outs, axis=-1).astype(BF16)

    @pl.when(jnp.logical_not(is_key))
    def _():
        col = lax.broadcasted_iota(jnp.int32, (NSA_DH, nch), 1)
        outs = []
        for grp in range(NSA_GROUPS):
            out = _dot_nt(w2t_ref[0], hids[grp]) + b2t_ref[0]
            outs.append(jnp.where(col < n_cmp, out, 0.0))
        ovt_ref[0] = jnp.concatenate(outs, axis=0).astype(BF16)


def _compress(xflat, pos, w1, b1, w2, b2, w2t, b2t, kc_g, n_cmp):
    b, _, _, nch, flat = xflat.shape
    return pl.pallas_call(
        functools.partial(_compress_body, n_cmp=n_cmp),
        grid=(b, 2),
        in_specs=[
            pl.BlockSpec((1, 1, NSA_GROUPS, nch, flat), lambda i, j: (i, j, 0, 0, 0)),
            pl.BlockSpec((1, 2, 1, flat), lambda i, j: (j, 0, 0, 0)),
            pl.BlockSpec((1, 2, flat, CMP_HIDDEN), lambda i, j: (j, 0, 0, 0)),
            pl.BlockSpec((1, 1, CMP_HIDDEN), lambda i, j: (j, 0, 0)),
            pl.BlockSpec((1, CMP_HIDDEN, NSA_DH), lambda i, j: (j, 0, 0)),
            pl.BlockSpec((1, 1, NSA_DH), lambda i, j: (j, 0, 0)),
            pl.BlockSpec((1, NSA_DH, CMP_HIDDEN), lambda i, j: (j, 0, 0)),
            pl.BlockSpec((1, NSA_DH, 1), lambda i, j: (j, 0, 0)),
            pl.BlockSpec((1, NSA_DH), lambda i, j: (0, 0)),
        ],
        out_specs=[pl.BlockSpec((1, nch, LANES), lambda i, j: (i, 0, 0)),
                   pl.BlockSpec((1, LANES, nch), lambda i, j: (i, 0, 0))],
        out_shape=[jax.ShapeDtypeStruct((b, nch, LANES), BF16), jax.ShapeDtypeStruct((b, LANES, nch), BF16)],
        compiler_params=_cparams(("parallel", "arbitrary"), 32),
        name="nsa_compress",
    )(xflat, pos, w1, b1, w2, b2, w2t, b2t, kc_g)


def _tile_heads(a):
    return jnp.concatenate([a] * NSA_HPG, axis=1)


def _masked_softmax_t(s, valid, any_valid=None):
    s = s + _tile_heads(jnp.where(valid, 0.0, NEG))
    p = jnp.exp2(s - jnp.max(s, axis=0, keepdims=True))
    inv = 1.0 / jnp.maximum(jnp.sum(p, axis=0, keepdims=True), 1e-30)
    if any_valid is not None:
        inv = jnp.where(_tile_heads(any_valid), inv, 0.0)
    return p * inv


def _nsa_body(qt_ref, gt_ref, kc_ref, vct_ref, ks_ref, kw_ref, vst_ref, vwt_ref, et_ref, ov_ref, o_ref, acc_sc, s_sc,
              *, n_cmp, n_top, kt):
    grp = pl.program_id(1)
    s0 = pl.program_id(2) * Q_BLOCK
    nch = kc_ref.shape[1]
    nblk = ov_ref.shape[0]

    qt = jnp.concatenate([qt_ref[h * LANES:(h + 1) * LANES, :] for h in range(NSA_HPG)], axis=1)
    t_row = s0 + lax.broadcasted_iota(jnp.int32, (1, Q_BLOCK), 1)

    cidx = lax.broadcasted_iota(jnp.int32, (nch, 1), 0)
    cvalid = (cidx * CMP_STRIDE + (CMP_BLOCK - 1) <= t_row) & (cidx < n_cmp)
    pc = _masked_softmax_t(_dot(kc_ref[0], qt), cvalid, t_row >= CMP_BLOCK - 1)
    oc = _dot(vct_ref[0], pc.astype(BF16))

    pcs = (pc[:, 0:Q_BLOCK] + pc[:, Q_BLOCK:2 * Q_BLOCK] + pc[:, 2 * Q_BLOCK:3 * Q_BLOCK]
           + pc[:, 3 * Q_BLOCK:4 * Q_BLOCK])
    pcs_hi = pcs.astype(BF16)
    pcs_lo = (pcs - pcs_hi.astype(F32)).astype(BF16)
    imp = _dot(ov_ref[...], pcs_hi) + _dot(ov_ref[...], pcs_lo)
    blk = lax.broadcasted_iota(jnp.int32, (nblk, Q_BLOCK), 0)
    cur = t_row // SLC_BLOCK
    imp = jnp.where(blk * SLC_BLOCK > t_row, NEG, imp)
    imp = jnp.where((blk == 0) | (blk == cur) | (blk == cur - 1), -NEG, imp)

    def pick_rounds(v, rounds):
        for _ in range(rounds):
            mx = jnp.max(v, axis=0, keepdims=True)
            first = jnp.min(jnp.where(v == mx, blk, nblk), axis=0, keepdims=True)
            v = jnp.where(blk == first, -jnp.inf, v)
        return v

    quarter = n_top // 4
    picked = pick_rounds(imp, quarter)

    wlen = WINDOW + Q_BLOCK
    w0 = pl.multiple_of(jnp.maximum(s0 - WINDOW, 0), Q_BLOCK)
    kpos = w0 + lax.broadcasted_iota(jnp.int32, (wlen, 1), 0)
    wbias = jnp.where((kpos <= t_row) & (kpos > t_row - WINDOW), 0.0, NEG)
    sw = _dot(kw_ref[0, pl.ds(w0, wlen), :], qt) + _tile_heads(wbias)
    picked = pick_rounds(picked, quarter)
    pw = jnp.exp2(sw - jnp.max(sw, axis=0, keepdims=True)).astype(BF16)
    picked = pick_rounds(picked, quarter)
    ow = _dot(vwt_ref[:, pl.ds(w0, wlen)], pw)
    ow = ow[:NSA_DH] / ow[NSA_DH:NSA_DH + 1]

    d0 = pl.multiple_of(s0, Q_BLOCK)
    dpos = s0 + lax.broadcasted_iota(jnp.int32, (Q_BLOCK, 1), 0)
    sd = _dot(ks_ref[0, pl.ds(d0, Q_BLOCK), :], qt) + _tile_heads(jnp.where(dpos <= t_row, 0.0, NEG))
    m_diag = jnp.max(sd, axis=0, keepdims=True)
    acc_sc[0] = _dot(vst_ref[:, pl.ds(d0, Q_BLOCK)], jnp.exp2(sd - m_diag).astype(BF16))
    acc_sc[1] = jnp.zeros(acc_sc.shape[1:], F32)
    picked = pick_rounds(picked, n_top - 3 * quarter)

    before = blk * SLC_BLOCK < s0
    selb = jnp.where((picked == -jnp.inf) & before, 0.0, MASKED).astype(BF16)
    rhs = jnp.concatenate([qt, _tile_heads(selb)], axis=0)

    last_tile = ks_ref.shape[1] // kt - 1

    def scores(idx, slot):
        k0 = pl.multiple_of(jnp.minimum(idx, last_tile) * kt, kt)
        lhs = jnp.concatenate([ks_ref[0, pl.ds(k0, kt), :], et_ref[pl.ds(k0, kt), :]], axis=1)
        s_sc[slot] = _dot(lhs, rhs)

    def update(idx, slot, m_old, acc_ref):
        k0 = pl.multiple_of(idx * kt, kt)
        m_new = jnp.maximum(m_old, jnp.max(s_sc[slot], axis=0, keepdims=True))
        p = jnp.exp2(s_sc[slot] - m_new).astype(BF16)
        acc_ref[...] = jnp.exp2(m_old - m_new) * acc_ref[...] + _dot(vst_ref[:, pl.ds(k0, kt)], p)
        return m_new

    def four_tiles(j, carry):
        m0, m1 = carry
        i = 4 * j
        scores(i + 2, 2)
        m0 = update(i, 0, m0, acc_sc.at[0])
        scores(i + 3, 3)
        m1 = update(i + 1, 1, m1, acc_sc.at[1])
        scores(i + 4, 0)
        m0 = update(i + 2, 2, m0, acc_sc.at[0])
        scores(i + 5, 1)
        m1 = update(i + 3, 3, m1, acc_sc.at[1])
        return m0, m1

    n_tiles = (s0 + kt - 1) // kt
    scores(0, 0)
    scores(1, 1)
    m0, m1 = lax.fori_loop(0, (n_tiles + 3) // 4, four_tiles,
                           (m_diag, jnp.full((1, NSA_HPG * Q_BLOCK), NEG, F32)))
    m_fin = jnp.maximum(m0, m1)
    acc = acc_sc[0] * jnp.exp2(m0 - m_fin) + acc_sc[1] * jnp.exp2(m1 - m_fin)
    osl = acc[:NSA_DH] / acc[NSA_DH:NSA_DH + 1]

    oc = jnp.where(grp == 0, oc[:NSA_DH], oc[NSA_DH:])
    gts = jax.nn.sigmoid(gt_ref[...])
    for h in range(NSA_HPG):
        cols = slice(h * Q_BLOCK, (h + 1) * Q_BLOCK)
        o_ref[h * NSA_DH:(h + 1) * NSA_DH, :] = (
            gts[3 * h:3 * h + 1, :] * oc[:, cols] + gts[3 * h + 1:3 * h + 2, :] * osl[:, cols]
            + gts[3 * h + 2:3 * h + 3, :] * ow[:, cols])


def _nsa_attention(qt, gt, kc, vct, okv, vt, expand_t, overlap, b, t, n_cmp, n_top, kt):
    nch = kc.shape[1]
    nq = t // Q_BLOCK
    n = b * t
    return pl.pallas_call(
        functools.partial(_nsa_body, n_cmp=n_cmp, n_top=n_top, kt=kt),
        grid=(b, NSA_GROUPS, nq),
        in_specs=[
            pl.BlockSpec((NSA_HPG * LANES, Q_BLOCK), lambda bi, g, i: (g, bi * nq + i)),
            pl.BlockSpec((GATE_ROWS, Q_BLOCK), lambda bi, g, i: (g, bi * nq + i)),
            pl.BlockSpec((1, nch, LANES), lambda bi, g, i: (bi, 0, 0)),
            pl.BlockSpec((1, LANES, nch), lambda bi, g, i: (bi, 0, 0)),
            pl.BlockSpec((1, t, LANES), lambda bi, g, i: (bi, 0, 2)),
            pl.BlockSpec((1, t, LANES), lambda bi, g, i: (bi, 0, 4)),
            pl.BlockSpec((LANES, t), lambda bi, g, i: (g, bi)),
            pl.BlockSpec((LANES, t), lambda bi, g, i: (NSA_GROUPS + g, bi)),
            pl.BlockSpec(expand_t.shape, lambda bi, g, i: (0, 0)),
            pl.BlockSpec(overlap.shape, lambda bi, g, i: (0, 0)),
        ],
        out_specs=pl.BlockSpec((NSA_HPG * NSA_DH, Q_BLOCK), lambda bi, g, i: (g, bi * nq + i)),
        out_shape=jax.ShapeDtypeStruct((NSA_W, n), F32),
        scratch_shapes=[pltpu.VMEM((2, LANES, NSA_HPG * Q_BLOCK), F32),
                        pltpu.VMEM((4, kt, NSA_HPG * Q_BLOCK), F32)],
        compiler_params=_cparams(("parallel", "parallel", "arbitrary"), 56),
        name="nsa_attention",
    )(qt, gt, kc, vct, okv, okv, vt, vt, expand_t, overlap)


def _split_bf16(a):
    hi = a.astype(BF16)
    return hi, (a - hi.astype(F32)).astype(BF16)


def _unit_lower_inverses(lmats):
    c = lmats[0].shape[0]
    r = lax.broadcasted_iota(jnp.int32, (c, c), 0)
    col = lax.broadcasted_iota(jnp.int32, (c, c), 1)
    eye = jnp.where(r == col, 1.0, 0.0)
    xs = [eye - l for l in lmats]
    ps = []
    for l in lmats:
        l16 = l.astype(BF16)
        ps.append(_dot(l16, l16))
    steps = int(np.log2(c)) - 1
    for s in range(steps):
        last = s + 1 == steps
        for i in range(len(lmats)):
            rhs = ps[i].astype(BF16)
            if last:
                xs[i] = xs[i] + _dot(xs[i].astype(BF16), rhs)
            else:
                both = _dot(jnp.concatenate([xs[i], ps[i]], axis=0).astype(BF16), rhs)
                xs[i] = xs[i] + both[:c]
                ps[i] = both[c:]
    return xs


def _gdn_body(x_ref, z_ref, ab_ref, alog_ref, dtb_ref, og_ref, o_ref, s_sc, *, ct):
    nb = x_ref.shape[0]

    @pl.when(pl.program_id(0) == 0)
    def _():
        s_sc[...] = jnp.zeros(s_sc.shape, F32)

    ch = GDN_CHUNK
    r = lax.broadcasted_iota(jnp.int32, (ch, ch), 0)
    col = lax.broadcasted_iota(jnp.int32, (ch, ch), 1)
    incl = r >= col
    strict = r > col
    tril16 = jnp.concatenate([jnp.where(incl, 1.0, 0.0).astype(BF16)] * 3, axis=1)

    units = []
    for ci in range(ct // ch):
        rows = slice(ci * ch, (ci + 1) * ch)
        for bi in range(nb):
            ab = ab_ref[bi, rows, :]
            g_all = -jnp.exp(alog_ref[...]) * jax.nn.softplus(ab + dtb_ref[...])
            beta_all = jax.nn.sigmoid(ab)
            g_hi, g_lo = _split_bf16(g_all)
            g_lo2 = (g_all - g_hi.astype(F32) - g_lo.astype(F32)).astype(BF16)
            gc_all = _dot(tril16, jnp.concatenate([g_hi, g_lo, g_lo2], axis=0))
            gc_t = gc_all.T
            for h in range(GDN_HEADS):
                hs = slice(h * GDN_DH, (h + 1) * GDN_DH)
                q16 = x_ref[bi, rows, hs]
                k16 = x_ref[bi, rows, GDN_W + h * GDN_DH:GDN_W + (h + 1) * GDN_DH]
                qh, kh = q16.astype(F32), k16.astype(F32)
                vh = x_ref[bi, rows, 2 * GDN_W + h * GDN_DH:2 * GDN_W + (h + 1) * GDN_DH].astype(F32)
                gc = gc_all[:, h:h + 1]
                gr = gc_t[h:h + 1, :]
                g_last = gc_all[ch - 1:ch, h:h + 1]
                beta = beta_all[:, GDN_HEADS + h:GDN_HEADS + h + 1]
                eg = jnp.exp(gc)
                decay = jnp.where(incl, jnp.exp(jnp.minimum(gc - gr, 0.0)), 0.0)
                kb = kh * beta
                units.append(dict(
                    rows=rows, bi=bi, h=h,
                    lmat=jnp.where(strict, _dot_nt(kb.astype(BF16), k16) * decay, 0.0),
                    vb=(vh * beta).astype(BF16), kbg=(kb * eg).astype(BF16),
                    qk=jnp.where(incl, _dot_nt(q16, k16) * decay, 0.0).astype(BF16),
                    qg=(qh * eg).astype(BF16), kd_t=(kh * jnp.exp(g_last - gc)).T.astype(BF16),
                    gl=jnp.exp(g_last)))
    tinvs = _unit_lower_inverses([u["lmat"] for u in units])
    for u, tinv in zip(units, tinvs):
        tinv = tinv.astype(BF16)
        u["u"] = _dot(tinv, u["vb"])
        u["w"] = _dot(tinv, u["kbg"]).astype(BF16)

    for u in units:
        bi, h, rows = u["bi"], u["h"], u["rows"]
        hs = slice(h * GDN_DH, (h + 1) * GDN_DH)
        s_old = s_sc[bi * GDN_HEADS + h]
        s16 = s_old.astype(BF16)
        v_new = (u["u"] - _dot(u["w"], s16)).astype(BF16)
        s_sc[bi * GDN_HEADS + h] = s_old * u["gl"] + _dot(u["kd_t"], v_new)
        o = _dot(u["qg"], s16) + _dot(u["qk"], v_new)
        on = o * lax.rsqrt(jnp.mean(o * o, axis=-1, keepdims=True) + EPS) * og_ref[...]
        zh = z_ref[bi, rows, hs].astype(F32)
        o_ref[bi, rows, hs] = (on * (zh * jax.nn.sigmoid(zh))).astype(BF16)


def _gdn(oqkv, oz, oab, alog, dtb, og, ct=128):
    b, t, w3 = oqkv.shape
    full = lambda a: pl.BlockSpec(a.shape, lambda c: (0,) * a.ndim)
    return pl.pallas_call(
        functools.partial(_gdn_body, ct=ct),
        grid=(t // ct,),
        in_specs=[
            pl.BlockSpec((b, ct, w3), lambda c: (0, c, 0)),
            pl.BlockSpec((b, ct, GDN_W), lambda c: (0, c, 0)),
            pl.BlockSpec((b, ct, LANES), lambda c: (0, c, 0)),
            full(alog), full(dtb), full(og),
        ],
        out_specs=pl.BlockSpec((b, ct, GDN_W), lambda c: (0, c, 0)),
        out_shape=jax.ShapeDtypeStruct((b, t, GDN_W), BF16),
        scratch_shapes=[pltpu.VMEM((b * GDN_HEADS, GDN_DH, GDN_DH), F32)],
        compiler_params=_cparams(("arbitrary",), 32),
        name="gdn",
    )(oqkv, oz, oab, alog, dtb, og)


def _outproj_body(ont_ref, og_ref, x_ref, ng_ref, wo_ref, fg_ref, wr_ref, br_ref,
                  x1_ref, h2_ref, idx_ref, gate_ref, rank_ref, cnt_ref, cnt_sc):
    i = pl.program_id(0)
    tm = x_ref.shape[0]

    @pl.when(i == 0)
    def _():
        cnt_sc[...] = jnp.zeros(cnt_sc.shape, F32)

    a = ont_ref[...]
    a = (a * lax.rsqrt(jnp.mean(a * a, axis=0, keepdims=True) + EPS) * ng_ref[...]).astype(BF16)
    x1 = x_ref[...] + _dot_tn(a, wo_ref[0:NSA_W, :]) + _dot(og_ref[...], wo_ref[NSA_W:, :])
    x1_ref[...] = x1
    h2f = x1 * lax.rsqrt(jnp.mean(x1 * x1, axis=-1, keepdims=True) + EPS) * fg_ref[...]
    _store_pieces(h2_ref, h2f)
    h2 = h2f.astype(BF16)

    lane = lax.broadcasted_iota(jnp.int32, (tm, LANES), 1)
    logits = jnp.where(lane < N_EXPERTS, _dot(h2, wr_ref[...]) + br_ref[...], -jnp.inf)
    onehot = jnp.zeros((tm, LANES), F32)
    idx = jnp.zeros((tm, LANES), jnp.int32)
    vals = jnp.zeros((tm, LANES), F32)
    firsts = []
    v = logits
    top0 = None
    for k in range(TOP_K):
        mx = jnp.max(v, axis=-1, keepdims=True)
        first = jnp.min(jnp.where(v == mx, lane, LANES), axis=-1, keepdims=True)
        hit = lane == first
        v = jnp.where(hit, -jnp.inf, v)
        onehot = jnp.where(hit, 1.0, onehot)
        idx = jnp.where(lane == k, first, idx)
        top0 = mx if k == 0 else top0
        vals = jnp.where(lane == k, jnp.exp(mx - top0), vals)
        firsts.append(first)
    idx_ref[...] = idx
    gate_ref[...] = vals / jnp.sum(vals, axis=-1, keepdims=True)

    r = lax.broadcasted_iota(jnp.int32, (tm, tm), 0)
    c = lax.broadcasted_iota(jnp.int32, (tm, tm), 1)
    before = jnp.where(r > c, 1.0, 0.0).astype(BF16)
    excl = cnt_sc[...] + _dot(before, onehot.astype(BF16))
    rank = jnp.zeros((tm, LANES), F32)
    for k in range(TOP_K):
        rk = jnp.sum(jnp.where(lane == firsts[k], excl, 0.0), axis=-1, keepdims=True)
        rank = jnp.where(lane == k, rk, rank)
    rank_ref[...] = rank.astype(jnp.int32)
    cnt_sc[...] = cnt_sc[...] + jnp.sum(onehot, axis=0, keepdims=True)
    cnt_ref[...] = cnt_sc[...].astype(jnp.int32)


def _out_proj(o_nsa_t, o_gdn, x2, ng, wo, fg, wr, br, tm=512):
    n, d = x2.shape
    full = lambda a: pl.BlockSpec(a.shape, lambda i: (0,) * a.ndim)
    row = lambda w: pl.BlockSpec((tm, w), lambda i: (i, 0))
    return pl.pallas_call(
        _outproj_body,
        grid=(n // tm,),
        in_specs=[pl.BlockSpec((NSA_W, tm), lambda i: (0, i)), row(GDN_W), row(d), full(ng), full(wo), full(fg),
                  full(wr), full(br)],
        out_specs=[row(d), pl.BlockSpec((d // SC_SUBROW, tm, SC_SUBROW), lambda i: (0, i, 0)),
                   row(LANES), row(LANES), row(LANES), pl.BlockSpec((1, LANES), lambda i: (0, 0))],
        out_shape=[jax.ShapeDtypeStruct((n, d), F32), jax.ShapeDtypeStruct((d // SC_SUBROW, n, SC_SUBROW), F32),
                   jax.ShapeDtypeStruct((n, LANES), jnp.int32), jax.ShapeDtypeStruct((n, LANES), F32),
                   jax.ShapeDtypeStruct((n, LANES), jnp.int32), jax.ShapeDtypeStruct((1, LANES), jnp.int32)],
        scratch_shapes=[pltpu.VMEM((1, LANES), F32)],
        compiler_params=_cparams(("arbitrary",), 48),
        name="out_proj_router",
    )(o_nsa_t, o_gdn, x2, ng, wo, fg, wr, br)


def _expert_body(be_ref, xs_ref, wg_ref, bg_ref, wu_ref, bu_ref, wd_ref, bd_ref, y_ref, wg16, wu16, wd16):
    i = pl.program_id(0)
    prev = be_ref[jnp.maximum(i - 1, 0)]
    fresh = (i == 0) | (be_ref[i] != prev)

    @pl.when(fresh)
    def _():
        wg16[...] = wg_ref[0].astype(BF16)
        wu16[...] = wu_ref[0].astype(BF16)
        wd16[...] = wd_ref[0].astype(BF16)

    x = _join_pieces(xs_ref).astype(BF16)
    gate = jnp.minimum(_dot(x, wg16[...]) + bg_ref[0], SWIGLU_LIMIT)
    up = jnp.clip(_dot(x, wu16[...]) + bu_ref[0], -SWIGLU_LIMIT, SWIGLU_LIMIT)
    glu = gate * jax.nn.sigmoid(gate * SWIGLU_ALPHA)
    _store_pieces(y_ref, _dot(((up + 1.0) * glu).astype(BF16), wd16[...]) + bd_ref[0])


def _experts(blk_e, xs, wg, bg, wu, bu, wd, bd):
    pieces, n_rows, sub = xs.shape
    d, de = wg.shape[1], wg.shape[2]
    r = MOE_ROW_BLOCK
    wspec = lambda a, b: pl.BlockSpec((1, a, b), lambda i, be: (be[i], 0, 0))
    grid_spec = pltpu.PrefetchScalarGridSpec(
        num_scalar_prefetch=1,
        grid=(n_rows // r,),
        in_specs=[pl.BlockSpec((pieces, r, sub), lambda i, be: (0, i, 0)),
                  wspec(d, de), wspec(1, de), wspec(d, de), wspec(1, de), wspec(de, d), wspec(1, d)],
        out_specs=pl.BlockSpec((pieces, r, sub), lambda i, be: (0, i, 0)),
        scratch_shapes=[pltpu.VMEM((d, de), BF16), pltpu.VMEM((d, de), BF16), pltpu.VMEM((de, d), BF16)],
    )
    return pl.pallas_call(
        _expert_body,
        grid_spec=grid_spec,
        out_shape=jax.ShapeDtypeStruct((pieces, n_rows, sub), F32),
        compiler_params=_cparams(("arbitrary",), 56),
        name="moe_experts",
    )(blk_e, xs, wg, bg, wu, bu, wd, bd)


SC_WINDOW = 128
SC_SUBROW = 256


def _sc_mesh():
    return plsc.VectorSubcoreMesh(core_axis_name="c", subcore_axis_name="s")


def _sc_dispatch(h2, dest_rows, n_rows):
    n, d = h2.shape

    @functools.partial(pl.kernel, out_type=jax.ShapeDtypeStruct((n_rows, d), h2.dtype), mesh=_sc_mesh())
    def dispatch(x_hbm, *refs):
        idx_hbm, o_hbm = refs[:TOP_K], refs[TOP_K]

        def body(x_vmem, *idx_vmem):
            for iv in idx_vmem:
                pltpu.sync_copy(x_vmem, o_hbm.at[iv.at[0]])

        pltpu.emit_pipeline(
            body,
            grid=(n // SC_WINDOW,),
            in_specs=[pl.BlockSpec((SC_WINDOW, d), lambda i: (i, 0))]
                     + [pl.BlockSpec((1, SC_WINDOW), lambda i: (0, i))] * TOP_K,
            out_specs=[],
            core_axis_name=("c", "s"),
            dimension_semantics=(pltpu.PARALLEL,),
        )(x_hbm, *idx_hbm)

    return dispatch(h2, *dest_rows)


def _sc_gather(table, idx):
    _, d = table.shape
    m = idx.shape[1]

    @functools.partial(pl.kernel, out_type=jax.ShapeDtypeStruct((m, d), table.dtype), mesh=_sc_mesh())
    def gather(t_hbm, i_hbm, o_hbm):
        def body(i_vmem, o_vmem):
            pltpu.sync_copy(t_hbm.at[i_vmem.at[0]], o_vmem)

        pltpu.emit_pipeline(
            body,
            grid=(m // SC_WINDOW,),
            in_specs=[pl.BlockSpec((1, SC_WINDOW), lambda i: (0, i))],
            out_specs=[pl.BlockSpec((SC_WINDOW, d), lambda i: (i, 0))],
            core_axis_name=("c", "s"),
            dimension_semantics=(pltpu.PARALLEL,),
        )(i_hbm, o_hbm)

    return gather(table, idx)


def _combine_body(x1_ref, y_ref, gate_ref, o_ref):
    acc = x1_ref[...]
    for k in range(TOP_K):
        acc = acc + gate_ref[:, k:k + 1] * _join_pieces(y_ref.at[k])
    o_ref[...] = acc


def _combine(x1, y4, gates, tm=512):
    n, d = x1.shape
    pieces, sub = y4.shape[1], y4.shape[3]
    row = lambda w: pl.BlockSpec((tm, w), lambda i: (i, 0))
    return pl.pallas_call(
        _combine_body,
        grid=(n // tm,),
        in_specs=[row(d), pl.BlockSpec((TOP_K, pieces, tm, sub), lambda i: (0, 0, i, 0)), row(LANES)],
        out_specs=row(d),
        out_shape=jax.ShapeDtypeStruct((n, d), F32),
        compiler_params=_cparams(("parallel",), 48),
        name="moe_combine",
    )(x1, y4, gates)


def _pad_lanes(a, width=LANES):
    return jnp.pad(a, ((0, 0), (0, width - a.shape[1])))


def _layer(x, attn_norm_g, w_in, q_g, kc_g, ks_g, kw_g, ck_pos, ck_w1, ck_b1, ck_w2, ck_b2,
           cv_pos, cv_w1, cv_b1, cv_w2, cv_b2, nsa_out_g, conv_w, a_log, dt_bias, gdn_out_g, w_out,
           ffn_g, router_w, router_b, e_wg, e_bg, e_wu, e_bu, e_wd, e_bd):
    b, t, d = x.shape
    n = b * t
    x2 = x.reshape(n, d)

    o = np.cumsum([0, NSA_W] + [NSA_GROUPS * NSA_DH] * 6 + [3 * NSA_HEADS, 3 * GDN_W, GDN_W, GDN_HEADS, GDN_HEADS])
    wq_t = w_in[:, o[0]:o[1]].T.reshape(NSA_GROUPS, NSA_HPG, NSA_DH, d)
    zq = jnp.zeros((NSA_HPG, NSA_DH, d), F32)
    wq_t = jnp.stack([jnp.concatenate([wq_t[0], zq], axis=1), jnp.concatenate([zq, wq_t[1]], axis=1)])
    wq_t = wq_t.reshape(NSA_HEADS * LANES, d).astype(BF16)
    qg1 = q_g * (NSA_DH ** -0.5 * np.log2(np.e))
    zg = jnp.zeros((NSA_DH,), F32)
    qg_col = jnp.concatenate([jnp.tile(jnp.concatenate([qg1, zg]), NSA_HPG),
                              jnp.tile(jnp.concatenate([zg, qg1]), NSA_HPG)]).reshape(NSA_HEADS * LANES, 1)
    wkv = w_in[:, o[1]:o[7]].astype(BF16)
    ones = jnp.ones((LANES,), F32)
    kg = jnp.concatenate([ones, ones, ks_g, ks_g, ones, kw_g, kw_g, ones]).reshape(1, 6 * LANES)
    wv_t = jnp.concatenate([w_in[:, o[4]:o[5]], w_in[:, o[6]:o[7]]], axis=1).T.reshape(2 * NSA_GROUPS, NSA_DH, d)
    wv_t = jnp.pad(wv_t, ((0, 0), (0, LANES - NSA_DH), (0, 0))).reshape(2 * NSA_GROUPS * LANES, d).astype(BF16)
    vone = jnp.asarray((np.arange(2 * NSA_GROUPS * LANES) % LANES == NSA_DH).astype(np.float32)[:, None])
    wg_t = w_in[:, o[7]:o[8]].T.reshape(NSA_GROUPS, NSA_HPG * 3, d)
    wg_t = jnp.pad(wg_t, ((0, 0), (0, GATE_ROWS - NSA_HPG * 3), (0, 0))).reshape(NSA_GROUPS * GATE_ROWS, d)
    wg_t = wg_t.astype(BF16)
    wab = _pad_lanes(w_in[:, o[10]:o[12]]).astype(BF16)
    wqkv = w_in[:, o[8]:o[9]].astype(BF16)
    wz = w_in[:, o[9]:o[10]].astype(BF16)

    tm = min(512, t)
    oqt, okv, ovt, ogt, oqkv, oz, oab = _in_proj(x2, attn_norm_g.reshape(1, d), wq_t, wkv, wv_t, wg_t, wqkv, wz,
                                                 wab, qg_col, kg, vone, conv_w, t // tm, tm)

    nch = t // CMP_STRIDE
    n_cmp = (t - CMP_BLOCK) // CMP_STRIDE + 1
    half = CMP_STRIDE * NSA_DH
    xflat = okv[:, :2 * LANES].reshape(b, nch, CMP_STRIDE, 2, NSA_GROUPS, NSA_DH)
    xflat = xflat.transpose(0, 3, 4, 1, 2, 5).reshape(b, 2, NSA_GROUPS, nch, half)
    pos = jnp.stack([ck_pos, cv_pos]).reshape(2, 2, 1, half)
    w1 = jnp.stack([ck_w1, cv_w1]).reshape(2, 2, half, CMP_HIDDEN).astype(BF16)
    b1 = jnp.stack([ck_b1, cv_b1]).reshape(2, 1, CMP_HIDDEN)
    w2 = jnp.stack([ck_w2, cv_w2]).astype(BF16)
    b2 = jnp.stack([ck_b2, cv_b2]).reshape(2, 1, NSA_DH)
    w2t = jnp.stack([ck_w2.T, cv_w2.T]).astype(BF16)
    b2t = jnp.stack([ck_b2, cv_b2]).reshape(2, NSA_DH, 1)
    kc, vct = _compress(xflat, pos, w1, b1, w2, b2, w2t, b2t, kc_g.reshape(1, NSA_DH), n_cmp)

    n_slc = t // SLC_BLOCK
    n_top = min(SLC_TOPK, n_slc)
    nblk = max(n_slc, LANES)
    kt = min(256, t // 4)
    assert (t // kt) % 4 == 0
    ci = np.arange(nch)[None, :] * CMP_STRIDE
    sj = np.arange(nblk)[:, None] * SLC_BLOCK
    overlap = ((ci < sj + SLC_BLOCK) & (ci + CMP_BLOCK > sj) & (np.arange(nch)[None, :] < n_cmp)
               & (np.arange(nblk)[:, None] < n_slc))
    expand_t = (np.arange(t)[:, None] // SLC_BLOCK) == np.arange(nblk)[None, :]
    o_nsa_t = _nsa_attention(oqt, ogt, kc, vct, okv.reshape(b, t, -1), ovt, jnp.asarray(expand_t, BF16),
                             jnp.asarray(overlap, BF16), b, t, n_cmp, n_top, kt)

    alog_row = _pad_lanes(a_log.reshape(1, GDN_HEADS))
    dtb_row = _pad_lanes(dt_bias.reshape(1, GDN_HEADS))
    o_gdn = _gdn(oqkv.reshape(b, t, -1), oz.reshape(b, t, -1), oab.reshape(b, t, -1),
                 alog_row, dtb_row, gdn_out_g.reshape(1, GDN_DH))

    wr = _pad_lanes(router_w).astype(BF16)
    br = _pad_lanes(router_b.reshape(1, N_EXPERTS))
    x1, h2, idx, gates, rank, counts = _out_proj(
        o_nsa_t, o_gdn.reshape(n, GDN_W), x2, nsa_out_g.reshape(NSA_W, 1),
        w_out.astype(BF16), ffn_g.reshape(1, d), wr, br)

    r = MOE_ROW_BLOCK
    nk = n * TOP_K
    counts = counts[0, :N_EXPERTS]
    pcounts = (counts + r - 1) // r * r
    pends = jnp.cumsum(pcounts)
    pstarts = pends - pcounts
    top_idx = idx[:, :TOP_K]
    dest = pstarts[top_idx] + rank[:, :TOP_K]
    n_rows = (nk + r - 1) // r * r + N_EXPERTS * r
    n_blocks = n_rows // r
    blk_start = jnp.arange(n_blocks, dtype=jnp.int32)[:, None] * r
    blk_e = jnp.minimum(jnp.sum(pends[None, :] <= blk_start, axis=1), N_EXPERTS - 1).astype(jnp.int32)
    pieces = d // SC_SUBROW
    dest_p = (dest.T.astype(jnp.int32)[:, None, :]
              + (jnp.arange(pieces, dtype=jnp.int32) * n_rows)[None, :, None])
    xs = _sc_dispatch(h2.reshape(pieces * n, SC_SUBROW), [dest_p[k].reshape(1, pieces * n) for k in range(TOP_K)],
                      pieces * n_rows)
    ys = _experts(blk_e, xs.reshape(pieces, n_rows, SC_SUBROW), e_wg, e_bg.reshape(N_EXPERTS, 1, -1), e_wu,
                  e_bu.reshape(N_EXPERTS, 1, -1), e_wd, e_bd.reshape(N_EXPERTS, 1, -1))
    y4 = _sc_gather(ys.reshape(pieces * n_rows, SC_SUBROW), dest_p.reshape(1, nk * pieces))
    return _combine(x1, y4.reshape(TOP_K, pieces, n, SC_SUBROW), gates).reshape(b, t, d)


def kernel(x, attn_norm_g, w_in, nsa_q_norm_g, nsa_kc_norm_g, nsa_ks_norm_g, nsa_kw_norm_g, cmp_k_pos, cmp_k_w1, cmp_k_b1, cmp_k_w2, cmp_k_b2, cmp_v_pos, cmp_v_w1, cmp_v_b1, cmp_v_w2, cmp_v_b2, nsa_out_norm_g, gdn_conv_w, gdn_a_log, gdn_dt_bias, gdn_out_norm_g, w_out, ffn_norm_g, router_w, router_b, exp_w_gate, exp_b_gate, exp_w_up, exp_b_up, exp_w_down, exp_b_down):
    params = (attn_norm_g, w_in, nsa_q_norm_g, nsa_kc_norm_g, nsa_ks_norm_g, nsa_kw_norm_g,
              cmp_k_pos, cmp_k_w1, cmp_k_b1, cmp_k_w2, cmp_k_b2, cmp_v_pos, cmp_v_w1, cmp_v_b1, cmp_v_w2, cmp_v_b2,
              nsa_out_norm_g, gdn_conv_w, gdn_a_log, gdn_dt_bias, gdn_out_norm_g, w_out, ffn_norm_g,
              router_w, router_b, exp_w_gate, exp_b_gate, exp_w_up, exp_b_up, exp_w_down, exp_b_down)
    for l in range(attn_norm_g.shape[0]):
        x = _layer(x, *(p[l] for p in params))
    return x
```

```python
import functools

import jax
import jax.numpy as jnp
import numpy as np
from jax import lax
from jax.experimental import pallas as pl
from jax.experimental.pallas import tpu as pltpu
from jax.experimental.pallas import tpu_sc as plsc

F32 = jnp.float32
BF16 = jnp.bfloat16

EPS = 1e-6
NEG = -1e30
MASKED = -2.0 ** 100

NSA_HEADS = 8
NSA_GROUPS = 2
NSA_HPG = 4
NSA_DH = 64
CMP_BLOCK = 32
CMP_STRIDE = 16
CMP_HIDDEN = 256
SLC_BLOCK = 64
SLC_TOPK = 16
WINDOW = 512
Q_BLOCK = 128
GDN_HEADS = 4
GDN_DH = 128
GDN_CONV = 4
GDN_CHUNK = 64
N_EXPERTS = 32
TOP_K = 4
SWIGLU_LIMIT = 7.0
SWIGLU_ALPHA = 1.702
MOE_ROW_BLOCK = 256

LANES = 128
GATE_ROWS = 16
NSA_W = NSA_HEADS * NSA_DH
GDN_W = GDN_HEADS * GDN_DH

_NT = (((1,), (1,)), ((), ()))
_TN = (((0,), (0,)), ((), ()))


def _cparams(sem, vmem_mb):
    return pltpu.CompilerParams(dimension_semantics=sem, vmem_limit_bytes=vmem_mb * 1024 * 1024)


def _dot(a, b):
    return jnp.dot(a, b, preferred_element_type=F32)


def _dot_nt(a, b):
    return lax.dot_general(a, b, _NT, preferred_element_type=F32)


def _dot_tn(a, b):
    return lax.dot_general(a, b, _TN, preferred_element_type=F32)


def _store_pieces(ref, val):
    half = val.shape[1] // 2
    hi = lax.bitcast_convert_type(val[:, :half].astype(BF16).astype(F32), jnp.uint32)
    lo = lax.bitcast_convert_type(val[:, half:].astype(BF16).astype(F32), jnp.uint32)
    words = lax.bitcast_convert_type(hi | (lo >> 16), jnp.int32)
    sub = ref.shape[2]
    for j in range(ref.shape[0]):
        ref[j] = words[:, j * sub:(j + 1) * sub]


def _join_pieces(ref):
    words = jnp.concatenate([ref[j] for j in range(ref.shape[0])], axis=1)
    words = lax.bitcast_convert_type(words, jnp.uint32)
    hi = lax.bitcast_convert_type(words & jnp.uint32(0xFFFF0000), F32)
    lo = lax.bitcast_convert_type(words << 16, F32)
    return jnp.concatenate([hi, lo], axis=1)


def _inproj_body(x_ref, g_ref, wqt_ref, wkv_ref, wvt_ref, wgt_ref, wqkv_ref, wz_ref, wab_ref, qg_ref, kg_ref,
                 vone_ref, cw_ref, oqt_ref, okv_ref, ovt_ref, ogt_ref, oqkv_ref, oz_ref, oab_ref, ybuf,
                 *, tiles_per_seq):
    x = x_ref[...]
    h = (x * lax.rsqrt(jnp.mean(x * x, axis=-1, keepdims=True) + EPS) * g_ref[...]).astype(BF16)
    tm = x.shape[0]

    yq = _dot_nt(wqt_ref[...], h)
    for s in range(NSA_HEADS):
        sl = slice(s * LANES, (s + 1) * LANES)
        ys = yq[sl, :]
        ms = jnp.sum(ys * ys, axis=0, keepdims=True) * (1.0 / NSA_DH)
        oqt_ref[sl, :] = (ys * lax.rsqrt(ms + EPS) * qg_ref[sl, :]).astype(BF16)

    ykv = _dot(h, wkv_ref[...])
    lane = lax.broadcasted_iota(jnp.int32, (tm, LANES), 1)
    low = lane < NSA_DH
    for s in range(6):
        sl = slice(s * LANES, (s + 1) * LANES)
        ys = ykv[:, sl]
        if s in (2, 4):
            y2 = ys * ys
            s0 = jnp.sum(jnp.where(low, y2, 0.0), axis=-1, keepdims=True)
            s1 = jnp.sum(jnp.where(low, 0.0, y2), axis=-1, keepdims=True)
            ms = jnp.where(low, s0, s1) * (1.0 / NSA_DH)
            ys = ys * lax.rsqrt(ms + EPS) * kg_ref[:, sl]
        okv_ref[:, sl] = ys.astype(BF16)

    ovt_ref[...] = (_dot_nt(wvt_ref[...], h) + vone_ref[...]).astype(BF16)
    ogt_ref[...] = _dot_nt(wgt_ref[...], h)
    oz_ref[...] = _dot(h, wz_ref[...]).astype(BF16)
    oab_ref[...] = _dot(h, wab_ref[...])

    halo = ybuf.shape[0] - tm
    first = pl.program_id(0) % tiles_per_seq == 0

    @pl.when(first)
    def _():
        ybuf[0:halo, :] = jnp.zeros((halo, ybuf.shape[1]), F32)

    @pl.when(jnp.logical_not(first))
    def _():
        ybuf[0:halo, :] = ybuf[tm:tm + halo, :]

    ybuf[halo:halo + tm, :] = _dot(h, wqkv_ref[...])
    taps = cw_ref.shape[0]
    y = cw_ref[0:1, :] * ybuf[pl.ds(halo - taps + 1, tm), :]
    for k in range(1, taps):
        y = y + cw_ref[k:k + 1, :] * ybuf[pl.ds(halo - taps + 1 + k, tm), :]
    hy = 0.5 * y
    y = hy + hy * jnp.tanh(hy)
    for s in range(3 * GDN_HEADS):
        sl = slice(s * GDN_DH, (s + 1) * GDN_DH)
        ys = y[:, sl]
        if s < 2 * GDN_HEADS:
            scale = GDN_DH ** -0.5 if s < GDN_HEADS else 1.0
            ys = ys * (lax.rsqrt(jnp.sum(ys * ys, axis=-1, keepdims=True) + EPS) * scale)
        oqkv_ref[:, sl] = ys.astype(BF16)


def _in_proj(x2, g, wqt, wkv, wvt, wgt, wqkv, wz, wab, qg, kg, vone, conv_w, tiles_per_seq, tm):
    n, d = x2.shape
    full = lambda a: pl.BlockSpec(a.shape, lambda i: (0,) * a.ndim)
    row = lambda w: pl.BlockSpec((tm, w), lambda i: (i, 0))
    colb = lambda r: pl.BlockSpec((r, tm), lambda i: (0, i))
    return pl.pallas_call(
        functools.partial(_inproj_body, tiles_per_seq=tiles_per_seq),
        grid=(n // tm,),
        in_specs=[row(d)] + [full(a) for a in (g, wqt, wkv, wvt, wgt, wqkv, wz, wab, qg, kg, vone, conv_w)],
        out_specs=[colb(wqt.shape[0]), row(wkv.shape[1]), colb(wvt.shape[0]), colb(wgt.shape[0]),
                   row(wqkv.shape[1]), row(wz.shape[1]), row(wab.shape[1])],
        out_shape=[jax.ShapeDtypeStruct((wqt.shape[0], n), BF16), jax.ShapeDtypeStruct((n, wkv.shape[1]), BF16),
                   jax.ShapeDtypeStruct((wvt.shape[0], n), BF16), jax.ShapeDtypeStruct((wgt.shape[0], n), F32),
                   jax.ShapeDtypeStruct((n, wqkv.shape[1]), BF16), jax.ShapeDtypeStruct((n, wz.shape[1]), BF16),
                   jax.ShapeDtypeStruct((n, wab.shape[1]), F32)],
        scratch_shapes=[pltpu.VMEM((tm + 8, wqkv.shape[1]), F32)],
        compiler_params=_cparams(("arbitrary",), 56),
        name="in_proj",
    )(x2, g, wqt, wkv, wvt, wgt, wqkv, wz, wab, qg, kg, vone, conv_w)


def _compress_body(x_ref, pos_ref, w1_ref, b1_ref, w2_ref, b2_ref, w2t_ref, b2t_ref, g_ref, ok_ref, ovt_ref,
                   *, n_cmp):
    is_key = pl.program_id(1) == 0
    nch = x_ref.shape[3]
    hids = []
    for grp in range(NSA_GROUPS):
        x = x_ref[0, 0, grp].astype(F32)
        xa = (x + pos_ref[0, 0]).astype(BF16)
        xb = (x + pos_ref[0, 1]).astype(BF16)
        a = _dot(xa, w1_ref[0, 0])
        b = _dot(xb, w1_ref[0, 1])
        b_next = pltpu.roll(b, nch - 1, 0)
        hids.append(jax.nn.gelu(a + b_next + b1_ref[0]).astype(BF16))

    @pl.when(is_key)
    def _():
        row = lax.broadcasted_iota(jnp.int32, (nch, NSA_DH), 0)
        outs = []
        for grp in range(NSA_GROUPS):
            out = _dot(hids[grp], w2_ref[0]) + b2_ref[0]
            out = out * lax.rsqrt(jnp.mean(out * out, axis=-1, keepdims=True) + EPS) * g_ref[...]
            outs.append(jnp.where(row < n_cmp, out, 0.0))
        ok_ref[0] = jnp.concatenate(outs, axis=-1).astype(BF16)

    @pl.when(jnp.logical_not(is_key))
    def _():
        col = lax.broadcasted_iota(jnp.int32, (NSA_DH, nch), 1)
        outs = []
        for grp in range(NSA_GROUPS):
            out = _dot_nt(w2t_ref[0], hids[grp]) + b2t_ref[0]
            outs.append(jnp.where(col < n_cmp, out, 0.0))
        ovt_ref[0] = jnp.concatenate(outs, axis=0).astype(BF16)


def _compress(xflat, pos, w1, b1, w2, b2, w2t, b2t, kc_g, n_cmp):
    b, _, _, nch, flat = xflat.shape
    return pl.pallas_call(
        functools.partial(_compress_body, n_cmp=n_cmp),
        grid=(b, 2),
        in_specs=[
            pl.BlockSpec((1, 1, NSA_GROUPS, nch, flat), lambda i, j: (i, j, 0, 0, 0)),
            pl.BlockSpec((1, 2, 1, flat), lambda i, j: (j, 0, 0, 0)),
            pl.BlockSpec((1, 2, flat, CMP_HIDDEN), lambda i, j: (j, 0, 0, 0)),
            pl.BlockSpec((1, 1, CMP_HIDDEN), lambda i, j: (j, 0, 0)),
            pl.BlockSpec((1, CMP_HIDDEN, NSA_DH), lambda i, j: (j, 0, 0)),
            pl.BlockSpec((1, 1, NSA_DH), lambda i, j: (j, 0, 0)),
            pl.BlockSpec((1, NSA_DH, CMP_HIDDEN), lambda i, j: (j, 0, 0)),
            pl.BlockSpec((1, NSA_DH, 1), lambda i, j: (j, 0, 0)),
            pl.BlockSpec((1, NSA_DH), lambda i, j: (0, 0)),
        ],
        out_specs=[pl.BlockSpec((1, nch, LANES), lambda i, j: (i, 0, 0)),
                   pl.BlockSpec((1, LANES, nch), lambda i, j: (i, 0, 0))],
        out_shape=[jax.ShapeDtypeStruct((b, nch, LANES), BF16), jax.ShapeDtypeStruct((b, LANES, nch), BF16)],
        compiler_params=_cparams(("parallel", "arbitrary"), 32),
        name="nsa_compress",
    )(xflat, pos, w1, b1, w2, b2, w2t, b2t, kc_g)


def _tile_heads(a):
    return jnp.concatenate([a] * NSA_HPG, axis=1)


def _masked_softmax_t(s, valid, any_valid=None):
    s = s + _tile_heads(jnp.where(valid, 0.0, NEG))
    p = jnp.exp2(s - jnp.max(s, axis=0, keepdims=True))
    inv = 1.0 / jnp.maximum(jnp.sum(p, axis=0, keepdims=True), 1e-30)
    if any_valid is not None:
        inv = jnp.where(_tile_heads(any_valid), inv, 0.0)
    return p * inv


def _nsa_body(qt_ref, gt_ref, kc_ref, vct_ref, ks_ref, kw_ref, vst_ref, vwt_ref, et_ref, ov_ref, o_ref, acc_sc, s_sc,
              *, n_cmp, n_top, kt):
    grp = pl.program_id(1)
    s0 = pl.program_id(2) * Q_BLOCK
    nch = kc_ref.shape[1]
    nblk = ov_ref.shape[0]

    qt = jnp.concatenate([qt_ref[h * LANES:(h + 1) * LANES, :] for h in range(NSA_HPG)], axis=1)
    t_row = s0 + lax.broadcasted_iota(jnp.int32, (1, Q_BLOCK), 1)

    cidx = lax.broadcasted_iota(jnp.int32, (nch, 1), 0)
    cvalid = (cidx * CMP_STRIDE + (CMP_BLOCK - 1) <= t_row) & (cidx < n_cmp)
    pc = _masked_softmax_t(_dot(kc_ref[0], qt), cvalid, t_row >= CMP_BLOCK - 1)
    oc = _dot(vct_ref[0], pc.astype(BF16))

    pcs = (pc[:, 0:Q_BLOCK] + pc[:, Q_BLOCK:2 * Q_BLOCK] + pc[:, 2 * Q_BLOCK:3 * Q_BLOCK]
           + pc[:, 3 * Q_BLOCK:4 * Q_BLOCK])
    pcs_hi = pcs.astype(BF16)
    pcs_lo = (pcs - pcs_hi.astype(F32)).astype(BF16)
    imp = _dot(ov_ref[...], pcs_hi) + _dot(ov_ref[...], pcs_lo)
    blk = lax.broadcasted_iota(jnp.int32, (nblk, Q_BLOCK), 0)
    cur = t_row // SLC_BLOCK
    imp = jnp.where(blk * SLC_BLOCK > t_row, NEG, imp)
    imp = jnp.where((blk == 0) | (blk == cur) | (blk == cur - 1), -NEG, imp)

    def pick_rounds(v, rounds):
        for _ in range(rounds):
            mx = jnp.max(v, axis=0, keepdims=True)
            first = jnp.min(jnp.where(v == mx, blk, nblk), axis=0, keepdims=True)
            v = jnp.where(blk == first, -jnp.inf, v)
        return v

    quarter = n_top // 4
    picked = pick_rounds(imp, quarter)

    wlen = WINDOW + Q_BLOCK
    w0 = pl.multiple_of(jnp.maximum(s0 - WINDOW, 0), Q_BLOCK)
    kpos = w0 + lax.broadcasted_iota(jnp.int32, (wlen, 1), 0)
    wbias = jnp.where((kpos <= t_row) & (kpos > t_row - WINDOW), 0.0, NEG)
    sw = _dot(kw_ref[0, pl.ds(w0, wlen), :], qt) + _tile_heads(wbias)
    picked = pick_rounds(picked, quarter)
    pw = jnp.exp2(sw - jnp.max(sw, axis=0, keepdims=True)).astype(BF16)
    picked = pick_rounds(picked, quarter)
    ow = _dot(vwt_ref[:, pl.ds(w0, wlen)], pw)
    ow = ow[:NSA_DH] / ow[NSA_DH:NSA_DH + 1]

    d0 = pl.multiple_of(s0, Q_BLOCK)
    dpos = s0 + lax.broadcasted_iota(jnp.int32, (Q_BLOCK, 1), 0)
    sd = _dot(ks_ref[0, pl.ds(d0, Q_BLOCK), :], qt) + _tile_heads(jnp.where(dpos <= t_row, 0.0, NEG))
    m_diag = jnp.max(sd, axis=0, keepdims=True)
    acc_sc[0] = _dot(vst_ref[:, pl.ds(d0, Q_BLOCK)], jnp.exp2(sd - m_diag).astype(BF16))
    acc_sc[1] = jnp.zeros(acc_sc.shape[1:], F32)
    picked = pick_rounds(picked, n_top - 3 * quarter)

    before = blk * SLC_BLOCK < s0
    selb = jnp.where((picked == -jnp.inf) & before, 0.0, MASKED).astype(BF16)
    rhs = jnp.concatenate([qt, _tile_heads(selb)], axis=0)

    last_tile = ks_ref.shape[1] // kt - 1

    def scores(idx, slot):
        k0 = pl.multiple_of(jnp.minimum(idx, last_tile) * kt, kt)
        lhs = jnp.concatenate([ks_ref[0, pl.ds(k0, kt), :], et_ref[pl.ds(k0, kt), :]], axis=1)
        s_sc[slot] = _dot(lhs, rhs)

    def update(idx, slot, m_old, acc_ref):
        k0 = pl.multiple_of(idx * kt, kt)
        m_new = jnp.maximum(m_old, jnp.max(s_sc[slot], axis=0, keepdims=True))
        p = jnp.exp2(s_sc[slot] - m_new).astype(BF16)
        acc_ref[...] = jnp.exp2(m_old - m_new) * acc_ref[...] + _dot(vst_ref[:, pl.ds(k0, kt)], p)
        return m_new

    def four_tiles(j, carry):
        m0, m1 = carry
        i = 4 * j
        scores(i + 2, 2)
        m0 = update(i, 0, m0, acc_sc.at[0])
        scores(i + 3, 3)
        m1 = update(i + 1, 1, m1, acc_sc.at[1])
        scores(i + 4, 0)
        m0 = update(i + 2, 2, m0, acc_sc.at[0])
        scores(i + 5, 1)
        m1 = update(i + 3, 3, m1, acc_sc.at[1])
        return m0, m1

    n_tiles = (s0 + kt - 1) // kt
    scores(0, 0)
    scores(1, 1)
    m0, m1 = lax.fori_loop(0, (n_tiles + 3) // 4, four_tiles,
                           (m_diag, jnp.full((1, NSA_HPG * Q_BLOCK), NEG, F32)))
    m_fin = jnp.maximum(m0, m1)
    acc = acc_sc[0] * jnp.exp2(m0 - m_fin) + acc_sc[1] * jnp.exp2(m1 - m_fin)
    osl = acc[:NSA_DH] / acc[NSA_DH:NSA_DH + 1]

    oc = jnp.where(grp == 0, oc[:NSA_DH], oc[NSA_DH:])
    gts = jax.nn.sigmoid(gt_ref[...])
    for h in range(NSA_HPG):
        cols = slice(h * Q_BLOCK, (h + 1) * Q_BLOCK)
        o_ref[h * NSA_DH:(h + 1) * NSA_DH, :] = (
            gts[3 * h:3 * h + 1, :] * oc[:, cols] + gts[3 * h + 1:3 * h + 2, :] * osl[:, cols]
            + gts[3 * h + 2:3 * h + 3, :] * ow[:, cols])


def _nsa_attention(qt, gt, kc, vct, okv, vt, expand_t, overlap, b, t, n_cmp, n_top, kt):
    nch = kc.shape[1]
    nq = t // Q_BLOCK
    n = b * t
    return pl.pallas_call(
        functools.partial(_nsa_body, n_cmp=n_cmp, n_top=n_top, kt=kt),
        grid=(b, NSA_GROUPS, nq),
        in_specs=[
            pl.BlockSpec((NSA_HPG * LANES, Q_BLOCK), lambda bi, g, i: (g, bi * nq + i)),
            pl.BlockSpec((GATE_ROWS, Q_BLOCK), lambda bi, g, i: (g, bi * nq + i)),
            pl.BlockSpec((1, nch, LANES), lambda bi, g, i: (bi, 0, 0)),
            pl.BlockSpec((1, LANES, nch), lambda bi, g, i: (bi, 0, 0)),
            pl.BlockSpec((1, t, LANES), lambda bi, g, i: (bi, 0, 2)),
            pl.BlockSpec((1, t, LANES), lambda bi, g, i: (bi, 0, 4)),
            pl.BlockSpec((LANES, t), lambda bi, g, i: (g, bi)),
            pl.BlockSpec((LANES, t), lambda bi, g, i: (NSA_GROUPS + g, bi)),
            pl.BlockSpec(expand_t.shape, lambda bi, g, i: (0, 0)),
            pl.BlockSpec(overlap.shape, lambda bi, g, i: (0, 0)),
        ],
        out_specs=pl.BlockSpec((NSA_HPG * NSA_DH, Q_BLOCK), lambda bi, g, i: (g, bi * nq + i)),
        out_shape=jax.ShapeDtypeStruct((NSA_W, n), F32),
        scratch_shapes=[pltpu.VMEM((2, LANES, NSA_HPG * Q_BLOCK), F32),
                        pltpu.VMEM((4, kt, NSA_HPG * Q_BLOCK), F32)],
        compiler_params=_cparams(("parallel", "parallel", "arbitrary"), 56),
        name="nsa_attention",
    )(qt, gt, kc, vct, okv, okv, vt, vt, expand_t, overlap)


def _split_bf16(a):
    hi = a.astype(BF16)
    return hi, (a - hi.astype(F32)).astype(BF16)


def _unit_lower_inverses(lmats):
    c = lmats[0].shape[0]
    r = lax.broadcasted_iota(jnp.int32, (c, c), 0)
    col = lax.broadcasted_iota(jnp.int32, (c, c), 1)
    eye = jnp.where(r == col, 1.0, 0.0)
    xs = [eye - l for l in lmats]
    ps = []
    for l in lmats:
        l16 = l.astype(BF16)
        ps.append(_dot(l16, l16))
    steps = int(np.log2(c)) - 1
    for s in range(steps):
        last = s + 1 == steps
        for i in range(len(lmats)):
            rhs = ps[i].astype(BF16)
            if last:
                xs[i] = xs[i] + _dot(xs[i].astype(BF16), rhs)
            else:
                both = _dot(jnp.concatenate([xs[i], ps[i]], axis=0).astype(BF16), rhs)
                xs[i] = xs[i] + both[:c]
                ps[i] = both[c:]
    return xs


def _gdn_body(x_ref, z_ref, ab_ref, alog_ref, dtb_ref, og_ref, o_ref, s_sc, *, ct):
    nb = x_ref.shape[0]

    @pl.when(pl.program_id(0) == 0)
    def _():
        s_sc[...] = jnp.zeros(s_sc.shape, F32)

    ch = GDN_CHUNK
    r = lax.broadcasted_iota(jnp.int32, (ch, ch), 0)
    col = lax.broadcasted_iota(jnp.int32, (ch, ch), 1)
    incl = r >= col
    strict = r > col
    tril16 = jnp.concatenate([jnp.where(incl, 1.0, 0.0).astype(BF16)] * 3, axis=1)

    units = []
    for ci in range(ct // ch):
        rows = slice(ci * ch, (ci + 1) * ch)
        for bi in range(nb):
            ab = ab_ref[bi, rows, :]
            g_all = -jnp.exp(alog_ref[...]) * jax.nn.softplus(ab + dtb_ref[...])
            beta_all = jax.nn.sigmoid(ab)
            g_hi, g_lo = _split_bf16(g_all)
            g_lo2 = (g_all - g_hi.astype(F32) - g_lo.astype(F32)).astype(BF16)
            gc_all = _dot(tril16, jnp.concatenate([g_hi, g_lo, g_lo2], axis=0))
            gc_t = gc_all.T
            for h in range(GDN_HEADS):
                hs = slice(h * GDN_DH, (h + 1) * GDN_DH)
                q16 = x_ref[bi, rows, hs]
                k16 = x_ref[bi, rows, GDN_W + h * GDN_DH:GDN_W + (h + 1) * GDN_DH]
                qh, kh = q16.astype(F32), k16.astype(F32)
                vh = x_ref[bi, rows, 2 * GDN_W + h * GDN_DH:2 * GDN_W + (h + 1) * GDN_DH].astype(F32)
                gc = gc_all[:, h:h + 1]
                gr = gc_t[h:h + 1, :]
                g_last = gc_all[ch - 1:ch, h:h + 1]
                beta = beta_all[:, GDN_HEADS + h:GDN_HEADS + h + 1]
                eg = jnp.exp(gc)
                decay = jnp.where(incl, jnp.exp(jnp.minimum(gc - gr, 0.0)), 0.0)
                kb = kh * beta
                units.append(dict(
                    rows=rows, bi=bi, h=h,
                    lmat=jnp.where(strict, _dot_nt(kb.astype(BF16), k16) * decay, 0.0),
                    vb=(vh * beta).astype(BF16), kbg=(kb * eg).astype(BF16),
                    qk=jnp.where(incl, _dot_nt(q16, k16) * decay, 0.0).astype(BF16),
                    qg=(qh * eg).astype(BF16), kd_t=(kh * jnp.exp(g_last - gc)).T.astype(BF16),
                    gl=jnp.exp(g_last)))
    tinvs = _unit_lower_inverses([u["lmat"] for u in units])
    for u, tinv in zip(units, tinvs):
        tinv = tinv.astype(BF16)
        u["u"] = _dot(tinv, u["vb"])
        u["w"] = _dot(tinv, u["kbg"]).astype(BF16)

    for u in units:
        bi, h, rows = u["bi"], u["h"], u["rows"]
        hs = slice(h * GDN_DH, (h + 1) * GDN_DH)
        s_old = s_sc[bi * GDN_HEADS + h]
        s16 = s_old.astype(BF16)
        v_new = (u["u"] - _dot(u["w"], s16)).astype(BF16)
        s_sc[bi * GDN_HEADS + h] = s_old * u["gl"] + _dot(u["kd_t"], v_new)
        o = _dot(u["qg"], s16) + _dot(u["qk"], v_new)
        on = o * lax.rsqrt(jnp.mean(o * o, axis=-1, keepdims=True) + EPS) * og_ref[...]
        zh = z_ref[bi, rows, hs].astype(F32)
        o_ref[bi, rows, hs] = (on * (zh * jax.nn.sigmoid(zh))).astype(BF16)


def _gdn(oqkv, oz, oab, alog, dtb, og, ct=128):
    b, t, w3 = oqkv.shape
    full = lambda a: pl.BlockSpec(a.shape, lambda c: (0,) * a.ndim)
    return pl.pallas_call(
        functools.partial(_gdn_body, ct=ct),
        grid=(t // ct,),
        in_specs=[
            pl.BlockSpec((b, ct, w3), lambda c: (0, c, 0)),
            pl.BlockSpec((b, ct, GDN_W), lambda c: (0, c, 0)),
            pl.BlockSpec((b, ct, LANES), lambda c: (0, c, 0)),
            full(alog), full(dtb), full(og),
        ],
        out_specs=pl.BlockSpec((b, ct, GDN_W), lambda c: (0, c, 0)),
        out_shape=jax.ShapeDtypeStruct((b, t, GDN_W), BF16),
        scratch_shapes=[pltpu.VMEM((b * GDN_HEADS, GDN_DH, GDN_DH), F32)],
        compiler_params=_cparams(("arbitrary",), 32),
        name="gdn",
    )(oqkv, oz, oab, alog, dtb, og)


def _outproj_body(ont_ref, og_ref, x_ref, ng_ref, wo_ref, fg_ref, wr_ref, br_ref,
                  x1_ref, h2_ref, idx_ref, gate_ref, rank_ref, cnt_ref, cnt_sc):
    i = pl.program_id(0)
    tm = x_ref.shape[0]

    @pl.when(i == 0)
    def _():
        cnt_sc[...] = jnp.zeros(cnt_sc.shape, F32)

    a = ont_ref[...]
    a = (a * lax.rsqrt(jnp.mean(a * a, axis=0, keepdims=True) + EPS) * ng_ref[...]).astype(BF16)
    x1 = x_ref[...] + _dot_tn(a, wo_ref[0:NSA_W, :]) + _dot(og_ref[...], wo_ref[NSA_W:, :])
    x1_ref[...] = x1
    h2f = x1 * lax.rsqrt(jnp.mean(x1 * x1, axis=-1, keepdims=True) + EPS) * fg_ref[...]
    _store_pieces(h2_ref, h2f)
    h2 = h2f.astype(BF16)

    lane = lax.broadcasted_iota(jnp.int32, (tm, LANES), 1)
    logits = jnp.where(lane < N_EXPERTS, _dot(h2, wr_ref[...]) + br_ref[...], -jnp.inf)
    onehot = jnp.zeros((tm, LANES), F32)
    idx = jnp.zeros((tm, LANES), jnp.int32)
    vals = jnp.zeros((tm, LANES), F32)
    firsts = []
    v = logits
    top0 = None
    for k in range(TOP_K):
        mx = jnp.max(v, axis=-1, keepdims=True)
        first = jnp.min(jnp.where(v == mx, lane, LANES), axis=-1, keepdims=True)
        hit = lane == first
        v = jnp.where(hit, -jnp.inf, v)
        onehot = jnp.where(hit, 1.0, onehot)
        idx = jnp.where(lane == k, first, idx)
        top0 = mx if k == 0 else top0
        vals = jnp.where(lane == k, jnp.exp(mx - top0), vals)
        firsts.append(first)
    idx_ref[...] = idx
    gate_ref[...] = vals / jnp.sum(vals, axis=-1, keepdims=True)

    r = lax.broadcasted_iota(jnp.int32, (tm, tm), 0)
    c = lax.broadcasted_iota(jnp.int32, (tm, tm), 1)
    before = jnp.where(r > c, 1.0, 0.0).astype(BF16)
    excl = cnt_sc[...] + _dot(before, onehot.astype(BF16))
    rank = jnp.zeros((tm, LANES), F32)
    for k in range(TOP_K):
        rk = jnp.sum(jnp.where(lane == firsts[k], excl, 0.0), axis=-1, keepdims=True)
        rank = jnp.where(lane == k, rk, rank)
    rank_ref[...] = rank.astype(jnp.int32)
    cnt_sc[...] = cnt_sc[...] + jnp.sum(onehot, axis=0, keepdims=True)
    cnt_ref[...] = cnt_sc[...].astype(jnp.int32)


def _out_proj(o_nsa_t, o_gdn, x2, ng, wo, fg, wr, br, tm=512):
    n, d = x2.shape
    full = lambda a: pl.BlockSpec(a.shape, lambda i: (0,) * a.ndim)
    row = lambda w: pl.BlockSpec((tm, w), lambda i: (i, 0))
    return pl.pallas_call(
        _outproj_body,
        grid=(n // tm,),
        in_specs=[pl.BlockSpec((NSA_W, tm), lambda i: (0, i)), row(GDN_W), row(d), full(ng), full(wo), full(fg),
                  full(wr), full(br)],
        out_specs=[row(d), pl.BlockSpec((d // 2 // SC_SUBROW, tm, SC_SUBROW), lambda i: (0, i, 0)),
                   row(LANES), row(LANES), row(LANES), pl.BlockSpec((1, LANES), lambda i: (0, 0))],
        out_shape=[jax.ShapeDtypeStruct((n, d), F32),
                   jax.ShapeDtypeStruct((d // 2 // SC_SUBROW, n, SC_SUBROW), jnp.int32),
                   jax.ShapeDtypeStruct((n, LANES), jnp.int32), jax.ShapeDtypeStruct((n, LANES), F32),
                   jax.ShapeDtypeStruct((n, LANES), jnp.int32), jax.ShapeDtypeStruct((1, LANES), jnp.int32)],
        scratch_shapes=[pltpu.VMEM((1, LANES), F32)],
        compiler_params=_cparams(("arbitrary",), 48),
        name="out_proj_router",
    )(o_nsa_t, o_gdn, x2, ng, wo, fg, wr, br)


def _expert_body(be_ref, xs_ref, wg_ref, bg_ref, wu_ref, bu_ref, wd_ref, bd_ref, y_ref, wg16, wu16, wd16):
    i = pl.program_id(0)
    prev = be_ref[jnp.maximum(i - 1, 0)]
    fresh = (i == 0) | (be_ref[i] != prev)

    @pl.when(fresh)
    def _():
        wg16[...] = wg_ref[0].astype(BF16)
        wu16[...] = wu_ref[0].astype(BF16)
        wd16[...] = wd_ref[0].astype(BF16)

    @pl.when(i < be_ref[pl.num_programs(0)])
    def _():
        x = _join_pieces(xs_ref).astype(BF16)
        gate = jnp.minimum(_dot(x, wg16[...]) + bg_ref[0], SWIGLU_LIMIT)
        up = jnp.clip(_dot(x, wu16[...]) + bu_ref[0], -SWIGLU_LIMIT, SWIGLU_LIMIT)
        glu = gate * jax.nn.sigmoid(gate * SWIGLU_ALPHA)
        _store_pieces(y_ref, _dot(((up + 1.0) * glu).astype(BF16), wd16[...]) + bd_ref[0])


def _experts(blk_e, xs, wg, bg, wu, bu, wd, bd):
    pieces, n_rows, sub = xs.shape
    d, de = wg.shape[1], wg.shape[2]
    r = MOE_ROW_BLOCK
    wspec = lambda a, b: pl.BlockSpec((1, a, b), lambda i, be: (be[i], 0, 0))
    grid_spec = pltpu.PrefetchScalarGridSpec(
        num_scalar_prefetch=1,
        grid=(n_rows // r,),
        in_specs=[pl.BlockSpec((pieces, r, sub), lambda i, be: (0, i, 0)),
                  wspec(d, de), wspec(1, de), wspec(d, de), wspec(1, de), wspec(de, d), wspec(1, d)],
        out_specs=pl.BlockSpec((pieces, r, sub), lambda i, be: (0, i, 0)),
        scratch_shapes=[pltpu.VMEM((d, de), BF16), pltpu.VMEM((d, de), BF16), pltpu.VMEM((de, d), BF16)],
    )
    return pl.pallas_call(
        _expert_body,
        grid_spec=grid_spec,
        out_shape=jax.ShapeDtypeStruct((pieces, n_rows, sub), jnp.int32),
        compiler_params=_cparams(("arbitrary",), 56),
        name="moe_experts",
    )(blk_e, xs, wg, bg, wu, bu, wd, bd)


SC_WINDOW = 128
SC_SUBROW = 256


def _sc_mesh():
    return plsc.VectorSubcoreMesh(core_axis_name="c", subcore_axis_name="s")


def _sc_dispatch(h2, dest_rows, n_rows):
    n, d = h2.shape

    @functools.partial(pl.kernel, out_type=jax.ShapeDtypeStruct((n_rows, d), h2.dtype), mesh=_sc_mesh())
    def dispatch(x_hbm, *refs):
        idx_hbm, o_hbm = refs[:TOP_K], refs[TOP_K]

        def body(x_vmem, *idx_vmem):
            for iv in idx_vmem:
                pltpu.sync_copy(x_vmem, o_hbm.at[iv.at[0]])

        pltpu.emit_pipeline(
            body,
            grid=(n // SC_WINDOW,),
            in_specs=[pl.BlockSpec((SC_WINDOW, d), lambda i: (i, 0))]
                     + [pl.BlockSpec((1, SC_WINDOW), lambda i: (0, i))] * TOP_K,
            out_specs=[],
            core_axis_name=("c", "s"),
            dimension_semantics=(pltpu.PARALLEL,),
        )(x_hbm, *idx_hbm)

    return dispatch(h2, *dest_rows)


def _sc_gather(table, idx):
    _, d = table.shape
    m = idx.shape[1]

    @functools.partial(pl.kernel, out_type=jax.ShapeDtypeStruct((m, d), table.dtype), mesh=_sc_mesh())
    def gather(t_hbm, i_hbm, o_hbm):
        def body(i_vmem, o_vmem):
            pltpu.sync_copy(t_hbm.at[i_vmem.at[0]], o_vmem)

        pltpu.emit_pipeline(
            body,
            grid=(m // SC_WINDOW,),
            in_specs=[pl.BlockSpec((1, SC_WINDOW), lambda i: (0, i))],
            out_specs=[pl.BlockSpec((SC_WINDOW, d), lambda i: (i, 0))],
            core_axis_name=("c", "s"),
            dimension_semantics=(pltpu.PARALLEL,),
        )(i_hbm, o_hbm)

    return gather(table, idx)


def _combine_body(x1_ref, y_ref, gate_ref, o_ref):
    acc = x1_ref[...]
    for k in range(TOP_K):
        acc = acc + gate_ref[:, k:k + 1] * _join_pieces(y_ref.at[k])
    o_ref[...] = acc


def _combine(x1, y4, gates, tm=512):
    n, d = x1.shape
    pieces, sub = y4.shape[1], y4.shape[3]
    row = lambda w: pl.BlockSpec((tm, w), lambda i: (i, 0))
    return pl.pallas_call(
        _combine_body,
        grid=(n // tm,),
        in_specs=[row(d), pl.BlockSpec((TOP_K, pieces, tm, sub), lambda i: (0, 0, i, 0)), row(LANES)],
        out_specs=row(d),
        out_shape=jax.ShapeDtypeStruct((n, d), F32),
        compiler_params=_cparams(("parallel",), 48),
        name="moe_combine",
    )(x1, y4, gates)


def _pad_lanes(a, width=LANES):
    return jnp.pad(a, ((0, 0), (0, width - a.shape[1])))


def _layer(x, attn_norm_g, w_in, q_g, kc_g, ks_g, kw_g, ck_pos, ck_w1, ck_b1, ck_w2, ck_b2,
           cv_pos, cv_w1, cv_b1, cv_w2, cv_b2, nsa_out_g, conv_w, a_log, dt_bias, gdn_out_g, w_out,
           ffn_g, router_w, router_b, e_wg, e_bg, e_wu, e_bu, e_wd, e_bd):
    b, t, d = x.shape
    n = b * t
    x2 = x.reshape(n, d)

    o = np.cumsum([0, NSA_W] + [NSA_GROUPS * NSA_DH] * 6 + [3 * NSA_HEADS, 3 * GDN_W, GDN_W, GDN_HEADS, GDN_HEADS])
    wq_t = w_in[:, o[0]:o[1]].T.reshape(NSA_GROUPS, NSA_HPG, NSA_DH, d)
    zq = jnp.zeros((NSA_HPG, NSA_DH, d), F32)
    wq_t = jnp.stack([jnp.concatenate([wq_t[0], zq], axis=1), jnp.concatenate([zq, wq_t[1]], axis=1)])
    wq_t = wq_t.reshape(NSA_HEADS * LANES, d).astype(BF16)
    qg1 = q_g * (NSA_DH ** -0.5 * np.log2(np.e))
    zg = jnp.zeros((NSA_DH,), F32)
    qg_col = jnp.concatenate([jnp.tile(jnp.concatenate([qg1, zg]), NSA_HPG),
                              jnp.tile(jnp.concatenate([zg, qg1]), NSA_HPG)]).reshape(NSA_HEADS * LANES, 1)
    wkv = w_in[:, o[1]:o[7]].astype(BF16)
    ones = jnp.ones((LANES,), F32)
    kg = jnp.concatenate([ones, ones, ks_g, ks_g, ones, kw_g, kw_g, ones]).reshape(1, 6 * LANES)
    wv_t = jnp.concatenate([w_in[:, o[4]:o[5]], w_in[:, o[6]:o[7]]], axis=1).T.reshape(2 * NSA_GROUPS, NSA_DH, d)
    wv_t = jnp.pad(wv_t, ((0, 0), (0, LANES - NSA_DH), (0, 0))).reshape(2 * NSA_GROUPS * LANES, d).astype(BF16)
    vone = jnp.asarray((np.arange(2 * NSA_GROUPS * LANES) % LANES == NSA_DH).astype(np.float32)[:, None])
    wg_t = w_in[:, o[7]:o[8]].T.reshape(NSA_GROUPS, NSA_HPG * 3, d)
    wg_t = jnp.pad(wg_t, ((0, 0), (0, GATE_ROWS - NSA_HPG * 3), (0, 0))).reshape(NSA_GROUPS * GATE_ROWS, d)
    wg_t = wg_t.astype(BF16)
    wab = _pad_lanes(w_in[:, o[10]:o[12]]).astype(BF16)
    wqkv = w_in[:, o[8]:o[9]].astype(BF16)
    wz = w_in[:, o[9]:o[10]].astype(BF16)

    tm = min(512, t)
    oqt, okv, ovt, ogt, oqkv, oz, oab = _in_proj(x2, attn_norm_g.reshape(1, d), wq_t, wkv, wv_t, wg_t, wqkv, wz,
                                                 wab, qg_col, kg, vone, conv_w, t // tm, tm)

    nch = t // CMP_STRIDE
    n_cmp = (t - CMP_BLOCK) // CMP_STRIDE + 1
    half = CMP_STRIDE * NSA_DH
    xflat = okv[:, :2 * LANES].reshape(b, nch, CMP_STRIDE, 2, NSA_GROUPS, NSA_DH)
    xflat = xflat.transpose(0, 3, 4, 1, 2, 5).reshape(b, 2, NSA_GROUPS, nch, half)
    pos = jnp.stack([ck_pos, cv_pos]).reshape(2, 2, 1, half)
    w1 = jnp.stack([ck_w1, cv_w1]).reshape(2, 2, half, CMP_HIDDEN).astype(BF16)
    b1 = jnp.stack([ck_b1, cv_b1]).reshape(2, 1, CMP_HIDDEN)
    w2 = jnp.stack([ck_w2, cv_w2]).astype(BF16)
    b2 = jnp.stack([ck_b2, cv_b2]).reshape(2, 1, NSA_DH)
    w2t = jnp.stack([ck_w2.T, cv_w2.T]).astype(BF16)
    b2t = jnp.stack([ck_b2, cv_b2]).reshape(2, NSA_DH, 1)
    kc, vct = _compress(xflat, pos, w1, b1, w2, b2, w2t, b2t, kc_g.reshape(1, NSA_DH), n_cmp)

    n_slc = t // SLC_BLOCK
    n_top = min(SLC_TOPK, n_slc)
    nblk = max(n_slc, LANES)
    kt = min(256, t // 4)
    assert (t // kt) % 4 == 0
    ci = np.arange(nch)[None, :] * CMP_STRIDE
    sj = np.arange(nblk)[:, None] * SLC_BLOCK
    overlap = ((ci < sj + SLC_BLOCK) & (ci + CMP_BLOCK > sj) & (np.arange(nch)[None, :] < n_cmp)
               & (np.arange(nblk)[:, None] < n_slc))
    expand_t = (np.arange(t)[:, None] // SLC_BLOCK) == np.arange(nblk)[None, :]
    o_nsa_t = _nsa_attention(oqt, ogt, kc, vct, okv.reshape(b, t, -1), ovt, jnp.asarray(expand_t, BF16),
                             jnp.asarray(overlap, BF16), b, t, n_cmp, n_top, kt)

    alog_row = _pad_lanes(a_log.reshape(1, GDN_HEADS))
    dtb_row = _pad_lanes(dt_bias.reshape(1, GDN_HEADS))
    o_gdn = _gdn(oqkv.reshape(b, t, -1), oz.reshape(b, t, -1), oab.reshape(b, t, -1),
                 alog_row, dtb_row, gdn_out_g.reshape(1, GDN_DH))

    wr = _pad_lanes(router_w).astype(BF16)
    br = _pad_lanes(router_b.reshape(1, N_EXPERTS))
    x1, h2, idx, gates, rank, counts = _out_proj(
        o_nsa_t, o_gdn.reshape(n, GDN_W), x2, nsa_out_g.reshape(NSA_W, 1),
        w_out.astype(BF16), ffn_g.reshape(1, d), wr, br)

    r = MOE_ROW_BLOCK
    nk = n * TOP_K
    counts = counts[0, :N_EXPERTS]
    pcounts = (counts + r - 1) // r * r
    pends = jnp.cumsum(pcounts)
    pstarts = pends - pcounts
    top_idx = idx[:, :TOP_K]
    dest = pstarts[top_idx] + rank[:, :TOP_K]
    n_rows = (nk + r - 1) // r * r + N_EXPERTS * r
    n_blocks = n_rows // r
    blk_start = jnp.arange(n_blocks, dtype=jnp.int32)[:, None] * r
    blk_e = jnp.minimum(jnp.sum(pends[None, :] <= blk_start, axis=1), N_EXPERTS - 1).astype(jnp.int32)
    blk_e = jnp.concatenate([blk_e, (pends[-1:] // r).astype(jnp.int32)])
    pieces = d // 2 // SC_SUBROW
    dest_p = (dest.T.astype(jnp.int32)[:, None, :]
              + (jnp.arange(pieces, dtype=jnp.int32) * n_rows)[None, :, None])
    xs = _sc_dispatch(h2.reshape(pieces * n, SC_SUBROW), [dest_p[k].reshape(1, pieces * n) for k in range(TOP_K)],
                      pieces * n_rows)
    ys = _experts(blk_e, xs.reshape(pieces, n_rows, SC_SUBROW), e_wg, e_bg.reshape(N_EXPERTS, 1, -1), e_wu,
                  e_bu.reshape(N_EXPERTS, 1, -1), e_wd, e_bd.reshape(N_EXPERTS, 1, -1))
    y4 = _sc_gather(ys.reshape(pieces * n_rows, SC_SUBROW), dest_p.reshape(1, nk * pieces))
    return _combine(x1, y4.reshape(TOP_K, pieces, n, SC_SUBROW), gates).reshape(b, t, d)


def kernel(x, attn_norm_g, w_in, nsa_q_norm_g, nsa_kc_norm_g, nsa_ks_norm_g, nsa_kw_norm_g, cmp_k_pos, cmp_k_w1, cmp_k_b1, cmp_k_w2, cmp_k_b2, cmp_v_pos, cmp_v_w1, cmp_v_b1, cmp_v_w2, cmp_v_b2, nsa_out_norm_g, gdn_conv_w, gdn_a_log, gdn_dt_bias, gdn_out_norm_g, w_out, ffn_norm_g, router_w, router_b, exp_w_gate, exp_b_gate, exp_w_up, exp_b_up, exp_w_down, exp_b_down):
    params = (attn_norm_g, w_in, nsa_q_norm_g, nsa_kc_norm_g, nsa_ks_norm_g, nsa_kw_norm_g,
              cmp_k_pos, cmp_k_w1, cmp_k_b1, cmp_k_w2, cmp_k_b2, cmp_v_pos, cmp_v_w1, cmp_v_b1, cmp_v_w2, cmp_v_b2,
              nsa_out_norm_g, gdn_conv_w, gdn_a_log, gdn_dt_bias, gdn_out_norm_g, w_out, ffn_norm_g,
              router_w, router_b, exp_w_gate, exp_b_gate, exp_w_up, exp_b_up, exp_w_down, exp_b_down)
    for l in range(attn_norm_g.shape[0]):
        x = _layer(x, *(p[l] for p in params))
    return x
```

```python
import functools

import jax
import jax.numpy as jnp
import numpy as np
from jax import lax
from jax.experimental import pallas as pl
from jax.experimental.pallas import tpu as pltpu
from jax.experimental.pallas import tpu_sc as plsc

F32 = jnp.float32
BF16 = jnp.bfloat16

EPS = 1e-6
NEG = -1e30
MASKED = -2.0 ** 100

NSA_HEADS = 8
NSA_GROUPS = 2
NSA_HPG = 4
NSA_DH = 64
CMP_BLOCK = 32
CMP_STRIDE = 16
CMP_HIDDEN = 256
SLC_BLOCK = 64
SLC_TOPK = 16
WINDOW = 512
Q_BLOCK = 128
GDN_HEADS = 4
GDN_DH = 128
GDN_CONV = 4
GDN_CHUNK = 64
N_EXPERTS = 32
TOP_K = 4
SWIGLU_LIMIT = 7.0
SWIGLU_ALPHA = 1.702
MOE_ROW_BLOCK = 256

LANES = 128
GATE_ROWS = 16
NSA_W = NSA_HEADS * NSA_DH
GDN_W = GDN_HEADS * GDN_DH

_NT = (((1,), (1,)), ((), ()))
_TN = (((0,), (0,)), ((), ()))


def _cparams(sem, vmem_mb):
    return pltpu.CompilerParams(dimension_semantics=sem, vmem_limit_bytes=vmem_mb * 1024 * 1024)


def _dot(a, b):
    return jnp.dot(a, b, preferred_element_type=F32)


def _dot_nt(a, b):
    return lax.dot_general(a, b, _NT, preferred_element_type=F32)


def _dot_tn(a, b):
    return lax.dot_general(a, b, _TN, preferred_element_type=F32)


def _store_pieces(ref, val):
    half = val.shape[1] // 2
    hi = lax.bitcast_convert_type(val[:, :half].astype(BF16).astype(F32), jnp.uint32)
    lo = lax.bitcast_convert_type(val[:, half:].astype(BF16).astype(F32), jnp.uint32)
    words = lax.bitcast_convert_type(hi | (lo >> 16), jnp.int32)
    sub = ref.shape[2]
    for j in range(ref.shape[0]):
        ref[j] = words[:, j * sub:(j + 1) * sub]


def _join_pieces(ref):
    words = jnp.concatenate([ref[j] for j in range(ref.shape[0])], axis=1)
    words = lax.bitcast_convert_type(words, jnp.uint32)
    hi = lax.bitcast_convert_type(words & jnp.uint32(0xFFFF0000), F32)
    lo = lax.bitcast_convert_type(words << 16, F32)
    return jnp.concatenate([hi, lo], axis=1)


def _inproj_body(x_ref, g_ref, wqt_ref, wkv_ref, wvt_ref, wgt_ref, wqkv_ref, wz_ref, wab_ref, qg_ref, kg_ref,
                 vone_ref, cw_ref, oqt_ref, okv_ref, ovt_ref, ogt_ref, oqkv_ref, oz_ref, oab_ref, ybuf,
                 *, tiles_per_seq):
    x = x_ref[...]
    h = (x * lax.rsqrt(jnp.mean(x * x, axis=-1, keepdims=True) + EPS) * g_ref[...]).astype(BF16)
    tm = x.shape[0]

    yq = _dot_nt(wqt_ref[...], h)
    for s in range(NSA_HEADS):
        sl = slice(s * LANES, (s + 1) * LANES)
        ys = yq[sl, :]
        ms = jnp.sum(ys * ys, axis=0, keepdims=True) * (1.0 / NSA_DH)
        oqt_ref[sl, :] = (ys * lax.rsqrt(ms + EPS) * qg_ref[sl, :]).astype(BF16)

    ykv = _dot(h, wkv_ref[...])
    lane = lax.broadcasted_iota(jnp.int32, (tm, LANES), 1)
    low = lane < NSA_DH
    for s in range(6):
        sl = slice(s * LANES, (s + 1) * LANES)
        ys = ykv[:, sl]
        if s in (2, 4):
            y2 = ys * ys
            s0 = jnp.sum(jnp.where(low, y2, 0.0), axis=-1, keepdims=True)
            s1 = jnp.sum(jnp.where(low, 0.0, y2), axis=-1, keepdims=True)
            ms = jnp.where(low, s0, s1) * (1.0 / NSA_DH)
            ys = ys * lax.rsqrt(ms + EPS) * kg_ref[:, sl]
        okv_ref[:, sl] = ys.astype(BF16)

    ovt_ref[...] = (_dot_nt(wvt_ref[...], h) + vone_ref[...]).astype(BF16)
    ogt_ref[...] = _dot_nt(wgt_ref[...], h)
    oz_ref[...] = _dot(h, wz_ref[...]).astype(BF16)
    oab_ref[...] = _dot(h, wab_ref[...])

    halo = ybuf.shape[0] - tm
    first = pl.program_id(0) % tiles_per_seq == 0

    @pl.when(first)
    def _():
        ybuf[0:halo, :] = jnp.zeros((halo, ybuf.shape[1]), F32)

    @pl.when(jnp.logical_not(first))
    def _():
        ybuf[0:halo, :] = ybuf[tm:tm + halo, :]

    ybuf[halo:halo + tm, :] = _dot(h, wqkv_ref[...])
    taps = cw_ref.shape[0]
    y = cw_ref[0:1, :] * ybuf[pl.ds(halo - taps + 1, tm), :]
    for k in range(1, taps):
        y = y + cw_ref[k:k + 1, :] * ybuf[pl.ds(halo - taps + 1 + k, tm), :]
    hy = 0.5 * y
    y = hy + hy * jnp.tanh(hy)
    for s in range(3 * GDN_HEADS):
        sl = slice(s * GDN_DH, (s + 1) * GDN_DH)
        ys = y[:, sl]
        if s < 2 * GDN_HEADS:
            scale = GDN_DH ** -0.5 if s < GDN_HEADS else 1.0
            ys = ys * (lax.rsqrt(jnp.sum(ys * ys, axis=-1, keepdims=True) + EPS) * scale)
        oqkv_ref[:, sl] = ys.astype(BF16)


def _in_proj(x2, g, wqt, wkv, wvt, wgt, wqkv, wz, wab, qg, kg, vone, conv_w, tiles_per_seq, tm):
    n, d = x2.shape
    full = lambda a: pl.BlockSpec(a.shape, lambda i: (0,) * a.ndim)
    row = lambda w: pl.BlockSpec((tm, w), lambda i: (i, 0))
    colb = lambda r: pl.BlockSpec((r, tm), lambda i: (0, i))
    return pl.pallas_call(
        functools.partial(_inproj_body, tiles_per_seq=tiles_per_seq),
        grid=(n // tm,),
        in_specs=[row(d)] + [full(a) for a in (g, wqt, wkv, wvt, wgt, wqkv, wz, wab, qg, kg, vone, conv_w)],
        out_specs=[colb(wqt.shape[0]), row(wkv.shape[1]), colb(wvt.shape[0]), colb(wgt.shape[0]),
                   row(wqkv.shape[1]), row(wz.shape[1]), row(wab.shape[1])],
        out_shape=[jax.ShapeDtypeStruct((wqt.shape[0], n), BF16), jax.ShapeDtypeStruct((n, wkv.shape[1]), BF16),
                   jax.ShapeDtypeStruct((wvt.shape[0], n), BF16), jax.ShapeDtypeStruct((wgt.shape[0], n), F32),
                   jax.ShapeDtypeStruct((n, wqkv.shape[1]), BF16), jax.ShapeDtypeStruct((n, wz.shape[1]), BF16),
                   jax.ShapeDtypeStruct((n, wab.shape[1]), F32)],
        scratch_shapes=[pltpu.VMEM((tm + 8, wqkv.shape[1]), F32)],
        compiler_params=_cparams(("arbitrary",), 56),
        name="in_proj",
    )(x2, g, wqt, wkv, wvt, wgt, wqkv, wz, wab, qg, kg, vone, conv_w)


def _compress_body(x_ref, pos_ref, w1_ref, b1_ref, w2_ref, b2_ref, w2t_ref, b2t_ref, g_ref, ok_ref, ovt_ref,
                   *, n_cmp):
    is_key = pl.program_id(1) == 0
    nch = x_ref.shape[3]
    hids = []
    for grp in range(NSA_GROUPS):
        x = x_ref[0, 0, grp].astype(F32)
        xa = (x + pos_ref[0, 0]).astype(BF16)
        xb = (x + pos_ref[0, 1]).astype(BF16)
        a = _dot(xa, w1_ref[0, 0])
        b = _dot(xb, w1_ref[0, 1])
        b_next = pltpu.roll(b, nch - 1, 0)
        hids.append(jax.nn.gelu(a + b_next + b1_ref[0]).astype(BF16))

    @pl.when(is_key)
    def _():
        row = lax.broadcasted_iota(jnp.int32, (nch, NSA_DH), 0)
        outs = []
        for grp in range(NSA_GROUPS):
            out = _dot(hids[grp], w2_ref[0]) + b2_ref[0]
            out = out * lax.rsqrt(jnp.mean(out * out, axis=-1, keepdims=True) + EPS) * g_ref[...]
            outs.append(jnp.where(row < n_cmp, out, 0.0))
        ok_ref[0] = jnp.concatenate(outs, axis=-1).astype(BF16)

    @pl.when(jnp.logical_not(is_key))
    def _():
        col = lax.broadcasted_iota(jnp.int32, (NSA_DH, nch), 1)
        outs = []
        for grp in range(NSA_GROUPS):
            out = _dot_nt(w2t_ref[0], hids[grp]) + b2t_ref[0]
            outs.append(jnp.where(col < n_cmp, out, 0.0))
        ovt_ref[0] = jnp.concatenate(outs, axis=0).astype(BF16)


def _compress(xflat, pos, w1, b1, w2, b2, w2t, b2t, kc_g, n_cmp):
    b, _, _, nch, flat = xflat.shape
    return pl.pallas_call(
        functools.partial(_compress_body, n_cmp=n_cmp),
        grid=(b, 2),
        in_specs=[
            pl.BlockSpec((1, 1, NSA_GROUPS, nch, flat), lambda i, j: (i, j, 0, 0, 0)),
            pl.BlockSpec((1, 2, 1, flat), lambda i, j: (j, 0, 0, 0)),
            pl.BlockSpec((1, 2, flat, CMP_HIDDEN), lambda i, j: (j, 0, 0, 0)),
            pl.BlockSpec((1, 1, CMP_HIDDEN), lambda i, j: (j, 0, 0)),
            pl.BlockSpec((1, CMP_HIDDEN, NSA_DH), lambda i, j: (j, 0, 0)),
            pl.BlockSpec((1, 1, NSA_DH), lambda i, j: (j, 0, 0)),
            pl.BlockSpec((1, NSA_DH, CMP_HIDDEN), lambda i, j: (j, 0, 0)),
            pl.BlockSpec((1, NSA_DH, 1), lambda i, j: (j, 0, 0)),
            pl.BlockSpec((1, NSA_DH), lambda i, j: (0, 0)),
        ],
        out_specs=[pl.BlockSpec((1, nch, LANES), lambda i, j: (i, 0, 0)),
                   pl.BlockSpec((1, LANES, nch), lambda i, j: (i, 0, 0))],
        out_shape=[jax.ShapeDtypeStruct((b, nch, LANES), BF16), jax.ShapeDtypeStruct((b, LANES, nch), BF16)],
        compiler_params=_cparams(("parallel", "arbitrary"), 32),
        name="nsa_compress",
    )(xflat, pos, w1, b1, w2, b2, w2t, b2t, kc_g)


def _tile_heads(a):
    return jnp.concatenate([a] * NSA_HPG, axis=1)


def _masked_softmax_t(s, valid, any_valid=None):
    s = s + _tile_heads(jnp.where(valid, 0.0, NEG))
    p = jnp.exp2(s - jnp.max(s, axis=0, keepdims=True))
    inv = 1.0 / jnp.maximum(jnp.sum(p, axis=0, keepdims=True), 1e-30)
    if any_valid is not None:
        inv = jnp.where(_tile_heads(any_valid), inv, 0.0)
    return p * inv


def _nsa_body(qt_ref, gt_ref, kc_ref, vct_ref, ks_ref, kw_ref, vst_ref, vwt_ref, et_ref, ov_ref, o_ref, acc_sc, s_sc,
              *, n_cmp, n_top, kt):
    grp = pl.program_id(1)
    s0 = pl.program_id(2) * Q_BLOCK
    nch = kc_ref.shape[1]
    nblk = ov_ref.shape[0]

    qt = jnp.concatenate([qt_ref[h * LANES:(h + 1) * LANES, :] for h in range(NSA_HPG)], axis=1)
    t_row = s0 + lax.broadcasted_iota(jnp.int32, (1, Q_BLOCK), 1)

    cidx = lax.broadcasted_iota(jnp.int32, (nch, 1), 0)
    cvalid = (cidx * CMP_STRIDE + (CMP_BLOCK - 1) <= t_row) & (cidx < n_cmp)
    pc = _masked_softmax_t(_dot(kc_ref[0], qt), cvalid, t_row >= CMP_BLOCK - 1)
    oc = _dot(vct_ref[0], pc.astype(BF16))

    pcs = (pc[:, 0:Q_BLOCK] + pc[:, Q_BLOCK:2 * Q_BLOCK] + pc[:, 2 * Q_BLOCK:3 * Q_BLOCK]
           + pc[:, 3 * Q_BLOCK:4 * Q_BLOCK])
    pcs_hi = pcs.astype(BF16)
    pcs_lo = (pcs - pcs_hi.astype(F32)).astype(BF16)
    imp = _dot(ov_ref[...], pcs_hi) + _dot(ov_ref[...], pcs_lo)
    blk = lax.broadcasted_iota(jnp.int32, (nblk, Q_BLOCK), 0)
    cur = t_row // SLC_BLOCK
    imp = jnp.where(blk * SLC_BLOCK > t_row, NEG, imp)
    imp = jnp.where((blk == 0) | (blk == cur) | (blk == cur - 1), -NEG, imp)

    def pick_rounds(v, rounds):
        for _ in range(rounds):
            mx = jnp.max(v, axis=0, keepdims=True)
            first = jnp.min(jnp.where(v == mx, blk, nblk), axis=0, keepdims=True)
            v = jnp.where(blk == first, -jnp.inf, v)
        return v

    quarter = n_top // 4
    picked = pick_rounds(imp, quarter)

    wlen = WINDOW + Q_BLOCK
    w0 = pl.multiple_of(jnp.maximum(s0 - WINDOW, 0), Q_BLOCK)
    kpos = w0 + lax.broadcasted_iota(jnp.int32, (wlen, 1), 0)
    wbias = jnp.where((kpos <= t_row) & (kpos > t_row - WINDOW), 0.0, NEG)
    sw = _dot(kw_ref[0, pl.ds(w0, wlen), :], qt) + _tile_heads(wbias)
    picked = pick_rounds(picked, quarter)
    pw = jnp.exp2(sw - jnp.max(sw, axis=0, keepdims=True)).astype(BF16)
    picked = pick_rounds(picked, quarter)
    ow = _dot(vwt_ref[:, pl.ds(w0, wlen)], pw)
    ow = ow[:NSA_DH] / ow[NSA_DH:NSA_DH + 1]

    d0 = pl.multiple_of(s0, Q_BLOCK)
    dpos = s0 + lax.broadcasted_iota(jnp.int32, (Q_BLOCK, 1), 0)
    sd = _dot(ks_ref[0, pl.ds(d0, Q_BLOCK), :], qt) + _tile_heads(jnp.where(dpos <= t_row, 0.0, NEG))
    m_diag = jnp.max(sd, axis=0, keepdims=True)
    acc_sc[0] = _dot(vst_ref[:, pl.ds(d0, Q_BLOCK)], jnp.exp2(sd - m_diag).astype(BF16))
    acc_sc[1] = jnp.zeros(acc_sc.shape[1:], F32)
    picked = pick_rounds(picked, n_top - 3 * quarter)

    before = blk * SLC_BLOCK < s0
    selb = jnp.where((picked == -jnp.inf) & before, 0.0, MASKED).astype(BF16)
    rhs = jnp.concatenate([qt, _tile_heads(selb)], axis=0)

    last_tile = ks_ref.shape[1] // kt - 1

    def scores(idx, slot):
        k0 = pl.multiple_of(jnp.minimum(idx, last_tile) * kt, kt)
        lhs = jnp.concatenate([ks_ref[0, pl.ds(k0, kt), :], et_ref[pl.ds(k0, kt), :]], axis=1)
        s_sc[slot] = _dot(lhs, rhs)

    def update(idx, slot, m_old, acc_ref):
        k0 = pl.multiple_of(idx * kt, kt)
        m_new = jnp.maximum(m_old, jnp.max(s_sc[slot], axis=0, keepdims=True))
        p = jnp.exp2(s_sc[slot] - m_new).astype(BF16)
        acc_ref[...] = jnp.exp2(m_old - m_new) * acc_ref[...] + _dot(vst_ref[:, pl.ds(k0, kt)], p)
        return m_new

    def four_tiles(j, carry):
        m0, m1 = carry
        i = 4 * j
        scores(i + 2, 2)
        m0 = update(i, 0, m0, acc_sc.at[0])
        scores(i + 3, 3)
        m1 = update(i + 1, 1, m1, acc_sc.at[1])
        scores(i + 4, 0)
        m0 = update(i + 2, 2, m0, acc_sc.at[0])
        scores(i + 5, 1)
        m1 = update(i + 3, 3, m1, acc_sc.at[1])
        return m0, m1

    n_tiles = (s0 + kt - 1) // kt
    scores(0, 0)
    scores(1, 1)
    m0, m1 = lax.fori_loop(0, (n_tiles + 3) // 4, four_tiles,
                           (m_diag, jnp.full((1, NSA_HPG * Q_BLOCK), NEG, F32)))
    m_fin = jnp.maximum(m0, m1)
    acc = acc_sc[0] * jnp.exp2(m0 - m_fin) + acc_sc[1] * jnp.exp2(m1 - m_fin)
    osl = acc[:NSA_DH] / acc[NSA_DH:NSA_DH + 1]

    oc = jnp.where(grp == 0, oc[:NSA_DH], oc[NSA_DH:])
    gts = jax.nn.sigmoid(gt_ref[...])
    for h in range(NSA_HPG):
        cols = slice(h * Q_BLOCK, (h + 1) * Q_BLOCK)
        o_ref[h * NSA_DH:(h + 1) * NSA_DH, :] = (
            gts[3 * h:3 * h + 1, :] * oc[:, cols] + gts[3 * h + 1:3 * h + 2, :] * osl[:, cols]
            + gts[3 * h + 2:3 * h + 3, :] * ow[:, cols])


def _nsa_attention(qt, gt, kc, vct, okv, vt, expand_t, overlap, b, t, n_cmp, n_top, kt):
    nch = kc.shape[1]
    nq = t // Q_BLOCK
    n = b * t
    return pl.pallas_call(
        functools.partial(_nsa_body, n_cmp=n_cmp, n_top=n_top, kt=kt),
        grid=(b, NSA_GROUPS, nq),
        in_specs=[
            pl.BlockSpec((NSA_HPG * LANES, Q_BLOCK), lambda bi, g, i: (g, bi * nq + i)),
            pl.BlockSpec((GATE_ROWS, Q_BLOCK), lambda bi, g, i: (g, bi * nq + i)),
            pl.BlockSpec((1, nch, LANES), lambda bi, g, i: (bi, 0, 0)),
            pl.BlockSpec((1, LANES, nch), lambda bi, g, i: (bi, 0, 0)),
            pl.BlockSpec((1, t, LANES), lambda bi, g, i: (bi, 0, 2)),
            pl.BlockSpec((1, t, LANES), lambda bi, g, i: (bi, 0, 4)),
            pl.BlockSpec((LANES, t), lambda bi, g, i: (g, bi)),
            pl.BlockSpec((LANES, t), lambda bi, g, i: (NSA_GROUPS + g, bi)),
            pl.BlockSpec(expand_t.shape, lambda bi, g, i: (0, 0)),
            pl.BlockSpec(overlap.shape, lambda bi, g, i: (0, 0)),
        ],
        out_specs=pl.BlockSpec((NSA_HPG * NSA_DH, Q_BLOCK), lambda bi, g, i: (g, bi * nq + i)),
        out_shape=jax.ShapeDtypeStruct((NSA_W, n), F32),
        scratch_shapes=[pltpu.VMEM((2, LANES, NSA_HPG * Q_BLOCK), F32),
                        pltpu.VMEM((4, kt, NSA_HPG * Q_BLOCK), F32)],
        compiler_params=_cparams(("parallel", "parallel", "arbitrary"), 56),
        name="nsa_attention",
    )(qt, gt, kc, vct, okv, okv, vt, vt, expand_t, overlap)


def _split_bf16(a):
    hi = a.astype(BF16)
    return hi, (a - hi.astype(F32)).astype(BF16)


def _unit_lower_inverses(lmats):
    c = lmats[0].shape[0]
    r = lax.broadcasted_iota(jnp.int32, (c, c), 0)
    col = lax.broadcasted_iota(jnp.int32, (c, c), 1)
    eye = jnp.where(r == col, 1.0, 0.0)
    xs = [eye - l for l in lmats]
    ps = []
    for l in lmats:
        l16 = l.astype(BF16)
        ps.append(_dot(l16, l16))
    steps = int(np.log2(c)) - 1
    for s in range(steps):
        last = s + 1 == steps
        for i in range(len(lmats)):
            rhs = ps[i].astype(BF16)
            if last:
                xs[i] = xs[i] + _dot(xs[i].astype(BF16), rhs)
            else:
                both = _dot(jnp.concatenate([xs[i], ps[i]], axis=0).astype(BF16), rhs)
                xs[i] = xs[i] + both[:c]
                ps[i] = both[c:]
    return xs


def _gdn_body(x_ref, z_ref, ab_ref, alog_ref, dtb_ref, og_ref, o_ref, s_sc, *, ct):
    nb = x_ref.shape[0]

    @pl.when(pl.program_id(0) == 0)
    def _():
        s_sc[...] = jnp.zeros(s_sc.shape, F32)

    ch = GDN_CHUNK
    r = lax.broadcasted_iota(jnp.int32, (ch, ch), 0)
    col = lax.broadcasted_iota(jnp.int32, (ch, ch), 1)
    incl = r >= col
    strict = r > col
    tril16 = jnp.concatenate([jnp.where(incl, 1.0, 0.0).astype(BF16)] * 3, axis=1)

    units = []
    for ci in range(ct // ch):
        rows = slice(ci * ch, (ci + 1) * ch)
        for bi in range(nb):
            ab = ab_ref[bi, rows, :]
            g_all = -jnp.exp(alog_ref[...]) * jax.nn.softplus(ab + dtb_ref[...])
            beta_all = jax.nn.sigmoid(ab)
            g_hi, g_lo = _split_bf16(g_all)
            g_lo2 = (g_all - g_hi.astype(F32) - g_lo.astype(F32)).astype(BF16)
            gc_all = _dot(tril16, jnp.concatenate([g_hi, g_lo, g_lo2], axis=0))
            gc_t = gc_all.T
            for h in range(GDN_HEADS):
                hs = slice(h * GDN_DH, (h + 1) * GDN_DH)
                q16 = x_ref[bi, rows, hs]
                k16 = x_ref[bi, rows, GDN_W + h * GDN_DH:GDN_W + (h + 1) * GDN_DH]
                qh, kh = q16.astype(F32), k16.astype(F32)
                vh = x_ref[bi, rows, 2 * GDN_W + h * GDN_DH:2 * GDN_W + (h + 1) * GDN_DH].astype(F32)
                gc = gc_all[:, h:h + 1]
                gr = gc_t[h:h + 1, :]
                g_last = gc_all[ch - 1:ch, h:h + 1]
                beta = beta_all[:, GDN_HEADS + h:GDN_HEADS + h + 1]
                eg = jnp.exp(gc)
                decay = jnp.where(incl, jnp.exp(jnp.minimum(gc - gr, 0.0)), 0.0)
                kb = kh * beta
                units.append(dict(
                    rows=rows, bi=bi, h=h,
                    lmat=jnp.where(strict, _dot_nt(kb.astype(BF16), k16) * decay, 0.0),
                    vb=(vh * beta).astype(BF16), kbg=(kb * eg).astype(BF16),
                    qk=jnp.where(incl, _dot_nt(q16, k16) * decay, 0.0).astype(BF16),
                    qg=(qh * eg).astype(BF16), kd_t=(kh * jnp.exp(g_last - gc)).T.astype(BF16),
                    gl=jnp.exp(g_last)))
    tinvs = _unit_lower_inverses([u["lmat"] for u in units])
    for u, tinv in zip(units, tinvs):
        tinv = tinv.astype(BF16)
        u["u"] = _dot(tinv, u["vb"])
        u["w"] = _dot(tinv, u["kbg"]).astype(BF16)

    for u in units:
        bi, h, rows = u["bi"], u["h"], u["rows"]
        hs = slice(h * GDN_DH, (h + 1) * GDN_DH)
        s_old = s_sc[bi * GDN_HEADS + h]
        s16 = s_old.astype(BF16)
        v_new = (u["u"] - _dot(u["w"], s16)).astype(BF16)
        s_sc[bi * GDN_HEADS + h] = s_old * u["gl"] + _dot(u["kd_t"], v_new)
        o = _dot(u["qg"], s16) + _dot(u["qk"], v_new)
        on = o * lax.rsqrt(jnp.mean(o * o, axis=-1, keepdims=True) + EPS) * og_ref[...]
        zh = z_ref[bi, rows, hs].astype(F32)
        o_ref[bi, rows, hs] = (on * (zh * jax.nn.sigmoid(zh))).astype(BF16)


def _gdn(oqkv, oz, oab, alog, dtb, og, ct=128):
    b, t, w3 = oqkv.shape
    full = lambda a: pl.BlockSpec(a.shape, lambda c: (0,) * a.ndim)
    return pl.pallas_call(
        functools.partial(_gdn_body, ct=ct),
        grid=(t // ct,),
        in_specs=[
            pl.BlockSpec((b, ct, w3), lambda c: (0, c, 0)),
            pl.BlockSpec((b, ct, GDN_W), lambda c: (0, c, 0)),
            pl.BlockSpec((b, ct, LANES), lambda c: (0, c, 0)),
            full(alog), full(dtb), full(og),
        ],
        out_specs=pl.BlockSpec((b, ct, GDN_W), lambda c: (0, c, 0)),
        out_shape=jax.ShapeDtypeStruct((b, t, GDN_W), BF16),
        scratch_shapes=[pltpu.VMEM((b * GDN_HEADS, GDN_DH, GDN_DH), F32)],
        compiler_params=_cparams(("arbitrary",), 32),
        name="gdn",
    )(oqkv, oz, oab, alog, dtb, og)


def _outproj_body(ont_ref, og_ref, x_ref, ng_ref, wo_ref, fg_ref, wr_ref, br_ref,
                  x1_ref, h2_ref, idx_ref, gate_ref, rank_ref, cnt_ref, cnt_sc):
    i = pl.program_id(0)
    tm = x_ref.shape[0]

    @pl.when(i == 0)
    def _():
        cnt_sc[...] = jnp.zeros(cnt_sc.shape, F32)

    a = ont_ref[...]
    a = (a * lax.rsqrt(jnp.mean(a * a, axis=0, keepdims=True) + EPS) * ng_ref[...]).astype(BF16)
    x1 = x_ref[...] + _dot_tn(a, wo_ref[0:NSA_W, :]) + _dot(og_ref[...], wo_ref[NSA_W:, :])
    x1_ref[...] = x1
    h2f = x1 * lax.rsqrt(jnp.mean(x1 * x1, axis=-1, keepdims=True) + EPS) * fg_ref[...]
    _store_pieces(h2_ref, h2f)
    h2 = h2f.astype(BF16)

    lane = lax.broadcasted_iota(jnp.int32, (tm, LANES), 1)
    logits = jnp.where(lane < N_EXPERTS, _dot(h2, wr_ref[...]) + br_ref[...], -jnp.inf)
    onehot = jnp.zeros((tm, LANES), F32)
    idx = jnp.zeros((tm, LANES), jnp.int32)
    vals = jnp.zeros((tm, LANES), F32)
    firsts = []
    v = logits
    top0 = None
    for k in range(TOP_K):
        mx = jnp.max(v, axis=-1, keepdims=True)
        first = jnp.min(jnp.where(v == mx, lane, LANES), axis=-1, keepdims=True)
        hit = lane == first
        v = jnp.where(hit, -jnp.inf, v)
        onehot = jnp.where(hit, 1.0, onehot)
        idx = jnp.where(lane == k, first, idx)
        top0 = mx if k == 0 else top0
        vals = jnp.where(lane == k, jnp.exp(mx - top0), vals)
        firsts.append(first)
    idx_ref[...] = idx
    gate_ref[...] = vals / jnp.sum(vals, axis=-1, keepdims=True)

    r = lax.broadcasted_iota(jnp.int32, (tm, tm), 0)
    c = lax.broadcasted_iota(jnp.int32, (tm, tm), 1)
    before = jnp.where(r > c, 1.0, 0.0).astype(BF16)
    excl = cnt_sc[...] + _dot(before, onehot.astype(BF16))
    rank = jnp.zeros((tm, LANES), F32)
    for k in range(TOP_K):
        rk = jnp.sum(jnp.where(lane == firsts[k], excl, 0.0), axis=-1, keepdims=True)
        rank = jnp.where(lane == k, rk, rank)
    rank_ref[...] = rank.astype(jnp.int32)
    cnt_sc[...] = cnt_sc[...] + jnp.sum(onehot, axis=0, keepdims=True)
    cnt_ref[...] = cnt_sc[...].astype(jnp.int32)


def _out_proj(o_nsa_t, o_gdn, x2, ng, wo, fg, wr, br, tm=512):
    n, d = x2.shape
    full = lambda a: pl.BlockSpec(a.shape, lambda i: (0,) * a.ndim)
    row = lambda w: pl.BlockSpec((tm, w), lambda i: (i, 0))
    return pl.pallas_call(
        _outproj_body,
        grid=(n // tm,),
        in_specs=[pl.BlockSpec((NSA_W, tm), lambda i: (0, i)), row(GDN_W), row(d), full(ng), full(wo), full(fg),
                  full(wr), full(br)],
        out_specs=[row(d), pl.BlockSpec((d // 2 // SC_SUBROW, tm, SC_SUBROW), lambda i: (0, i, 0)),
                   row(LANES), row(LANES), row(LANES), pl.BlockSpec((1, LANES), lambda i: (0, 0))],
        out_shape=[jax.ShapeDtypeStruct((n, d), F32),
                   jax.ShapeDtypeStruct((d // 2 // SC_SUBROW, n, SC_SUBROW), jnp.int32),
                   jax.ShapeDtypeStruct((n, LANES), jnp.int32), jax.ShapeDtypeStruct((n, LANES), F32),
                   jax.ShapeDtypeStruct((n, LANES), jnp.int32), jax.ShapeDtypeStruct((1, LANES), jnp.int32)],
        scratch_shapes=[pltpu.VMEM((1, LANES), F32)],
        compiler_params=_cparams(("arbitrary",), 48),
        name="out_proj_router",
    )(o_nsa_t, o_gdn, x2, ng, wo, fg, wr, br)


def _expert_body(be_ref, fresh_ref, slot_ref, next_ref, xs_ref, wg_hbm, bg_ref, wu_hbm, bu_ref, wd_hbm, bd_ref,
                 y_ref, wf32, w16, sems):
    i = pl.program_id(0)
    used = i < be_ref[pl.num_programs(0)]
    hbm = (wg_hbm, wu_hbm, wd_hbm)

    def weight_copy(expert, slot, j):
        return pltpu.make_async_copy(hbm[j].at[expert], wf32.at[slot, j], sems.at[slot, j])

    @pl.when((i == 0) & used)
    def _():
        for j in range(3):
            weight_copy(be_ref[0], 0, j).start()

    @pl.when(used & (fresh_ref[i] == 1))
    def _():
        slot = slot_ref[i]
        for j in range(3):
            weight_copy(be_ref[i], slot, j).wait()
            w16[j] = wf32[slot, j].astype(BF16)

        @pl.when(next_ref[i] >= 0)
        def _():
            for j in range(3):
                weight_copy(next_ref[i], 1 - slot, j).start()

    @pl.when(jnp.logical_not(used))
    def _():
        y_ref[...] = jnp.zeros(y_ref.shape, y_ref.dtype)

    @pl.when(used)
    def _():
        x = _join_pieces(xs_ref).astype(BF16)
        gate = jnp.minimum(_dot(x, w16[0]) + bg_ref[0], SWIGLU_LIMIT)
        up = jnp.clip(_dot(x, w16[1]) + bu_ref[0], -SWIGLU_LIMIT, SWIGLU_LIMIT)
        glu = gate * jax.nn.sigmoid(gate * SWIGLU_ALPHA)
        _store_pieces(y_ref, _dot(((up + 1.0) * glu).astype(BF16), w16[2]) + bd_ref[0])


def _experts(blk_e, fresh, slot, nxt, xs, wg, bg, wu, bu, wd, bd):
    pieces, n_rows, sub = xs.shape
    d, de = wg.shape[1], wg.shape[2]
    assert d == de
    r = MOE_ROW_BLOCK
    bspec = lambda w: pl.BlockSpec((1, 1, w), lambda i, be, *_: (be[i], 0, 0))
    hbm = pl.BlockSpec(memory_space=pl.ANY)
    grid_spec = pltpu.PrefetchScalarGridSpec(
        num_scalar_prefetch=4,
        grid=(n_rows // r,),
        in_specs=[pl.BlockSpec((pieces, r, sub), lambda i, *_: (0, i, 0)),
                  hbm, bspec(de), hbm, bspec(de), hbm, bspec(d)],
        out_specs=pl.BlockSpec((pieces, r, sub), lambda i, *_: (0, i, 0)),
        scratch_shapes=[pltpu.VMEM((2, 3, d, de), F32), pltpu.VMEM((3, d, de), BF16),
                        pltpu.SemaphoreType.DMA((2, 3))],
    )
    return pl.pallas_call(
        _expert_body,
        grid_spec=grid_spec,
        out_shape=jax.ShapeDtypeStruct((pieces, n_rows, sub), jnp.int32),
        compiler_params=_cparams(("arbitrary",), 56),
        name="moe_experts",
    )(blk_e, fresh, slot, nxt, xs, wg, bg, wu, bu, wd, bd)


SC_WINDOW = 128
SC_SUBROW = 256


def _sc_mesh():
    return plsc.VectorSubcoreMesh(core_axis_name="c", subcore_axis_name="s")


def _sc_dispatch(h2, dest_rows, n_rows):
    n, d = h2.shape

    @functools.partial(pl.kernel, out_type=jax.ShapeDtypeStruct((n_rows, d), h2.dtype), mesh=_sc_mesh())
    def dispatch(x_hbm, *refs):
        idx_hbm, o_hbm = refs[:TOP_K], refs[TOP_K]

        def body(x_vmem, *idx_vmem):
            for iv in idx_vmem:
                pltpu.sync_copy(x_vmem, o_hbm.at[iv.at[0]])

        pltpu.emit_pipeline(
            body,
            grid=(n // SC_WINDOW,),
            in_specs=[pl.BlockSpec((SC_WINDOW, d), lambda i: (i, 0))]
                     + [pl.BlockSpec((1, SC_WINDOW), lambda i: (0, i))] * TOP_K,
            out_specs=[],
            core_axis_name=("c", "s"),
            dimension_semantics=(pltpu.PARALLEL,),
        )(x_hbm, *idx_hbm)

    return dispatch(h2, *dest_rows)


def _sc_gather(table, idx):
    _, d = table.shape
    m = idx.shape[1]

    @functools.partial(pl.kernel, out_type=jax.ShapeDtypeStruct((m, d), table.dtype), mesh=_sc_mesh())
    def gather(t_hbm, i_hbm, o_hbm):
        def body(i_vmem, o_vmem):
            pltpu.sync_copy(t_hbm.at[i_vmem.at[0]], o_vmem)

        pltpu.emit_pipeline(
            body,
            grid=(m // SC_WINDOW,),
            in_specs=[pl.BlockSpec((1, SC_WINDOW), lambda i: (0, i))],
            out_specs=[pl.BlockSpec((SC_WINDOW, d), lambda i: (i, 0))],
            core_axis_name=("c", "s"),
            dimension_semantics=(pltpu.PARALLEL,),
        )(i_hbm, o_hbm)

    return gather(table, idx)


def _combine_body(x1_ref, y_ref, gate_ref, o_ref):
    acc = x1_ref[...]
    for k in range(TOP_K):
        acc = acc + gate_ref[:, k:k + 1] * _join_pieces(y_ref.at[k])
    o_ref[...] = acc


def _combine(x1, y4, gates, tm=512):
    n, d = x1.shape
    pieces, sub = y4.shape[1], y4.shape[3]
    row = lambda w: pl.BlockSpec((tm, w), lambda i: (i, 0))
    return pl.pallas_call(
        _combine_body,
        grid=(n // tm,),
        in_specs=[row(d), pl.BlockSpec((TOP_K, pieces, tm, sub), lambda i: (0, 0, i, 0)), row(LANES)],
        out_specs=row(d),
        out_shape=jax.ShapeDtypeStruct((n, d), F32),
        compiler_params=_cparams(("parallel",), 48),
        name="moe_combine",
    )(x1, y4, gates)


def _pad_lanes(a, width=LANES):
    return jnp.pad(a, ((0, 0), (0, width - a.shape[1])))


def _layer(x, attn_norm_g, w_in, q_g, kc_g, ks_g, kw_g, ck_pos, ck_w1, ck_b1, ck_w2, ck_b2,
           cv_pos, cv_w1, cv_b1, cv_w2, cv_b2, nsa_out_g, conv_w, a_log, dt_bias, gdn_out_g, w_out,
           ffn_g, router_w, router_b, e_wg, e_bg, e_wu, e_bu, e_wd, e_bd):
    b, t, d = x.shape
    n = b * t
    x2 = x.reshape(n, d)

    o = np.cumsum([0, NSA_W] + [NSA_GROUPS * NSA_DH] * 6 + [3 * NSA_HEADS, 3 * GDN_W, GDN_W, GDN_HEADS, GDN_HEADS])
    wq_t = w_in[:, o[0]:o[1]].T.reshape(NSA_GROUPS, NSA_HPG, NSA_DH, d)
    zq = jnp.zeros((NSA_HPG, NSA_DH, d), F32)
    wq_t = jnp.stack([jnp.concatenate([wq_t[0], zq], axis=1), jnp.concatenate([zq, wq_t[1]], axis=1)])
    wq_t = wq_t.reshape(NSA_HEADS * LANES, d).astype(BF16)
    qg1 = q_g * (NSA_DH ** -0.5 * np.log2(np.e))
    zg = jnp.zeros((NSA_DH,), F32)
    qg_col = jnp.concatenate([jnp.tile(jnp.concatenate([qg1, zg]), NSA_HPG),
                              jnp.tile(jnp.concatenate([zg, qg1]), NSA_HPG)]).reshape(NSA_HEADS * LANES, 1)
    wkv = w_in[:, o[1]:o[7]].astype(BF16)
    ones = jnp.ones((LANES,), F32)
    kg = jnp.concatenate([ones, ones, ks_g, ks_g, ones, kw_g, kw_g, ones]).reshape(1, 6 * LANES)
    wv_t = jnp.concatenate([w_in[:, o[4]:o[5]], w_in[:, o[6]:o[7]]], axis=1).T.reshape(2 * NSA_GROUPS, NSA_DH, d)
    wv_t = jnp.pad(wv_t, ((0, 0), (0, LANES - NSA_DH), (0, 0))).reshape(2 * NSA_GROUPS * LANES, d).astype(BF16)
    vone = jnp.asarray((np.arange(2 * NSA_GROUPS * LANES) % LANES == NSA_DH).astype(np.float32)[:, None])
    wg_t = w_in[:, o[7]:o[8]].T.reshape(NSA_GROUPS, NSA_HPG * 3, d)
    wg_t = jnp.pad(wg_t, ((0, 0), (0, GATE_ROWS - NSA_HPG * 3), (0, 0))).reshape(NSA_GROUPS * GATE_ROWS, d)
    wg_t = wg_t.astype(BF16)
    wab = _pad_lanes(w_in[:, o[10]:o[12]]).astype(BF16)
    wqkv = w_in[:, o[8]:o[9]].astype(BF16)
    wz = w_in[:, o[9]:o[10]].astype(BF16)

    tm = min(512, t)
    oqt, okv, ovt, ogt, oqkv, oz, oab = _in_proj(x2, attn_norm_g.reshape(1, d), wq_t, wkv, wv_t, wg_t, wqkv, wz,
                                                 wab, qg_col, kg, vone, conv_w, t // tm, tm)

    nch = t // CMP_STRIDE
    n_cmp = (t - CMP_BLOCK) // CMP_STRIDE + 1
    half = CMP_STRIDE * NSA_DH
    xflat = okv[:, :2 * LANES].reshape(b, nch, CMP_STRIDE, 2, NSA_GROUPS, NSA_DH)
    xflat = xflat.transpose(0, 3, 4, 1, 2, 5).reshape(b, 2, NSA_GROUPS, nch, half)
    pos = jnp.stack([ck_pos, cv_pos]).reshape(2, 2, 1, half)
    w1 = jnp.stack([ck_w1, cv_w1]).reshape(2, 2, half, CMP_HIDDEN).astype(BF16)
    b1 = jnp.stack([ck_b1, cv_b1]).reshape(2, 1, CMP_HIDDEN)
    w2 = jnp.stack([ck_w2, cv_w2]).astype(BF16)
    b2 = jnp.stack([ck_b2, cv_b2]).reshape(2, 1, NSA_DH)
    w2t = jnp.stack([ck_w2.T, cv_w2.T]).astype(BF16)
    b2t = jnp.stack([ck_b2, cv_b2]).reshape(2, NSA_DH, 1)
    kc, vct = _compress(xflat, pos, w1, b1, w2, b2, w2t, b2t, kc_g.reshape(1, NSA_DH), n_cmp)

    n_slc = t // SLC_BLOCK
    n_top = min(SLC_TOPK, n_slc)
    nblk = max(n_slc, LANES)
    kt = min(256, t // 4)
    assert (t // kt) % 4 == 0
    ci = np.arange(nch)[None, :] * CMP_STRIDE
    sj = np.arange(nblk)[:, None] * SLC_BLOCK
    overlap = ((ci < sj + SLC_BLOCK) & (ci + CMP_BLOCK > sj) & (np.arange(nch)[None, :] < n_cmp)
               & (np.arange(nblk)[:, None] < n_slc))
    expand_t = (np.arange(t)[:, None] // SLC_BLOCK) == np.arange(nblk)[None, :]
    o_nsa_t = _nsa_attention(oqt, ogt, kc, vct, okv.reshape(b, t, -1), ovt, jnp.asarray(expand_t, BF16),
                             jnp.asarray(overlap, BF16), b, t, n_cmp, n_top, kt)

    alog_row = _pad_lanes(a_log.reshape(1, GDN_HEADS))
    dtb_row = _pad_lanes(dt_bias.reshape(1, GDN_HEADS))
    o_gdn = _gdn(oqkv.reshape(b, t, -1), oz.reshape(b, t, -1), oab.reshape(b, t, -1),
                 alog_row, dtb_row, gdn_out_g.reshape(1, GDN_DH))

    wr = _pad_lanes(router_w).astype(BF16)
    br = _pad_lanes(router_b.reshape(1, N_EXPERTS))
    x1, h2, idx, gates, rank, counts = _out_proj(
        o_nsa_t, o_gdn.reshape(n, GDN_W), x2, nsa_out_g.reshape(NSA_W, 1),
        w_out.astype(BF16), ffn_g.reshape(1, d), wr, br)

    r = MOE_ROW_BLOCK
    nk = n * TOP_K
    counts = counts[0, :N_EXPERTS]
    pcounts = (counts + r - 1) // r * r
    pends = jnp.cumsum(pcounts)
    pstarts = pends - pcounts
    top_idx = idx[:, :TOP_K]
    dest = pstarts[top_idx] + rank[:, :TOP_K]
    n_rows = (nk + r - 1) // r * r + N_EXPERTS * r
    n_blocks = n_rows // r
    blk_start = jnp.arange(n_blocks, dtype=jnp.int32)[:, None] * r
    blk_e = jnp.minimum(jnp.sum(pends[None, :] <= blk_start, axis=1), N_EXPERTS - 1).astype(jnp.int32)
    n_used = (pends[-1] // r).astype(jnp.int32)
    fresh = (jnp.arange(n_blocks) < n_used) & (blk_e != jnp.concatenate([jnp.full((1,), -1, jnp.int32), blk_e[:-1]]))
    slot = ((jnp.cumsum(fresh) - 1) % 2).astype(jnp.int32)
    eid = jnp.arange(N_EXPERTS, dtype=jnp.int32)
    later = jnp.where((eid[None, :] > eid[:, None]) & (pcounts[None, :] > 0), eid[None, :], N_EXPERTS)
    next_expert = jnp.min(later, axis=1)
    next_expert = jnp.where(next_expert < N_EXPERTS, next_expert, -1).astype(jnp.int32)
    nxt = next_expert[blk_e]
    blk_e = jnp.concatenate([blk_e, n_used[None]])
    pieces = d // 2 // SC_SUBROW
    dest_p = (dest.T.astype(jnp.int32)[:, None, :]
              + (jnp.arange(pieces, dtype=jnp.int32) * n_rows)[None, :, None])
    xs = _sc_dispatch(h2.reshape(pieces * n, SC_SUBROW), [dest_p[k].reshape(1, pieces * n) for k in range(TOP_K)],
                      pieces * n_rows)
    ys = _experts(blk_e, fresh.astype(jnp.int32), slot, nxt, xs.reshape(pieces, n_rows, SC_SUBROW), e_wg, e_bg.reshape(N_EXPERTS, 1, -1), e_wu,
                  e_bu.reshape(N_EXPERTS, 1, -1), e_wd, e_bd.reshape(N_EXPERTS, 1, -1))
    y4 = _sc_gather(ys.reshape(pieces * n_rows, SC_SUBROW), dest_p.reshape(1, nk * pieces))
    return _combine(x1, y4.reshape(TOP_K, pieces, n, SC_SUBROW), gates).reshape(b, t, d)


def kernel(x, attn_norm_g, w_in, nsa_q_norm_g, nsa_kc_norm_g, nsa_ks_norm_g, nsa_kw_norm_g, cmp_k_pos, cmp_k_w1, cmp_k_b1, cmp_k_w2, cmp_k_b2, cmp_v_pos, cmp_v_w1, cmp_v_b1, cmp_v_w2, cmp_v_b2, nsa_out_norm_g, gdn_conv_w, gdn_a_log, gdn_dt_bias, gdn_out_norm_g, w_out, ffn_norm_g, router_w, router_b, exp_w_gate, exp_b_gate, exp_w_up, exp_b_up, exp_w_down, exp_b_down):
    params = (attn_norm_g, w_in, nsa_q_norm_g, nsa_kc_norm_g, nsa_ks_norm_g, nsa_kw_norm_g,
              cmp_k_pos, cmp_k_w1, cmp_k_b1, cmp_k_w2, cmp_k_b2, cmp_v_pos, cmp_v_w1, cmp_v_b1, cmp_v_w2, cmp_v_b2,
              nsa_out_norm_g, gdn_conv_w, gdn_a_log, gdn_dt_bias, gdn_out_norm_g, w_out, ffn_norm_g,
              router_w, router_b, exp_w_gate, exp_b_gate, exp_w_up, exp_b_up, exp_w_down, exp_b_down)
    for l in range(attn_norm_g.shape[0]):
        x = _layer(x, *(p[l] for p in params))
    return x
```

```python
import functools

import jax
import jax.numpy as jnp
import numpy as np
from jax import lax
from jax.experimental import pallas as pl
from jax.experimental.pallas import tpu as pltpu
from jax.experimental.pallas import tpu_sc as plsc

F32 = jnp.float32
BF16 = jnp.bfloat16

EPS = 1e-6
NEG = -1e30
MASKED = -2.0 ** 100

NSA_HEADS = 8
NSA_GROUPS = 2
NSA_HPG = 4
NSA_DH = 64
CMP_BLOCK = 32
CMP_STRIDE = 16
CMP_HIDDEN = 256
SLC_BLOCK = 64
SLC_TOPK = 16
WINDOW = 512
Q_BLOCK = 128
GDN_HEADS = 4
GDN_DH = 128
GDN_CONV = 4
GDN_CHUNK = 64
N_EXPERTS = 32
TOP_K = 4
SWIGLU_LIMIT = 7.0
SWIGLU_ALPHA = 1.702
MOE_ROW_BLOCK = 256

LANES = 128
GATE_ROWS = 16
NSA_W = NSA_HEADS * NSA_DH
GDN_W = GDN_HEADS * GDN_DH

_NT = (((1,), (1,)), ((), ()))
_TN = (((0,), (0,)), ((), ()))


def _cparams(sem, vmem_mb):
    return pltpu.CompilerParams(dimension_semantics=sem, vmem_limit_bytes=vmem_mb * 1024 * 1024)


def _dot(a, b):
    return jnp.dot(a, b, preferred_element_type=F32)


def _dot_nt(a, b):
    return lax.dot_general(a, b, _NT, preferred_element_type=F32)


def _dot_tn(a, b):
    return lax.dot_general(a, b, _TN, preferred_element_type=F32)


def _store_pieces(ref, val):
    half = val.shape[1] // 2
    hi = lax.bitcast_convert_type(val[:, :half].astype(BF16).astype(F32), jnp.uint32)
    lo = lax.bitcast_convert_type(val[:, half:].astype(BF16).astype(F32), jnp.uint32)
    words = lax.bitcast_convert_type(hi | (lo >> 16), jnp.int32)
    sub = ref.shape[2]
    for j in range(ref.shape[0]):
        ref[j] = words[:, j * sub:(j + 1) * sub]


def _join_pieces(ref):
    words = jnp.concatenate([ref[j] for j in range(ref.shape[0])], axis=1)
    words = lax.bitcast_convert_type(words, jnp.uint32)
    hi = lax.bitcast_convert_type(words & jnp.uint32(0xFFFF0000), F32)
    lo = lax.bitcast_convert_type(words << 16, F32)
    return jnp.concatenate([hi, lo], axis=1)


def _inproj_body(x_ref, g_ref, wqt_ref, wkv_ref, wvt_ref, wgt_ref, wqkv_ref, wz_ref, wab_ref, qg_ref, kg_ref,
                 vone_ref, cw_ref, oqt_ref, okv_ref, ovt_ref, ogt_ref, oqkv_ref, oz_ref, oab_ref, ybuf,
                 *, tiles_per_seq):
    x = x_ref[...]
    h = (x * lax.rsqrt(jnp.mean(x * x, axis=-1, keepdims=True) + EPS) * g_ref[...]).astype(BF16)
    tm = x.shape[0]

    yq = _dot_nt(wqt_ref[...], h)
    for s in range(NSA_HEADS):
        sl = slice(s * LANES, (s + 1) * LANES)
        ys = yq[sl, :]
        ms = jnp.sum(ys * ys, axis=0, keepdims=True) * (1.0 / NSA_DH)
        oqt_ref[sl, :] = (ys * lax.rsqrt(ms + EPS) * qg_ref[sl, :]).astype(BF16)

    ykv = _dot(h, wkv_ref[...])
    lane = lax.broadcasted_iota(jnp.int32, (tm, LANES), 1)
    low = lane < NSA_DH
    for s in range(6):
        sl = slice(s * LANES, (s + 1) * LANES)
        ys = ykv[:, sl]
        if s in (2, 4):
            y2 = ys * ys
            s0 = jnp.sum(jnp.where(low, y2, 0.0), axis=-1, keepdims=True)
            s1 = jnp.sum(jnp.where(low, 0.0, y2), axis=-1, keepdims=True)
            ms = jnp.where(low, s0, s1) * (1.0 / NSA_DH)
            ys = ys * lax.rsqrt(ms + EPS) * kg_ref[:, sl]
        okv_ref[:, sl] = ys.astype(BF16)

    ovt_ref[...] = (_dot_nt(wvt_ref[...], h) + vone_ref[...]).astype(BF16)
    ogt_ref[...] = _dot_nt(wgt_ref[...], h)
    oz_ref[...] = _dot(h, wz_ref[...]).astype(BF16)
    oab_ref[...] = _dot(h, wab_ref[...])

    halo = ybuf.shape[0] - tm
    first = pl.program_id(0) % tiles_per_seq == 0

    @pl.when(first)
    def _():
        ybuf[0:halo, :] = jnp.zeros((halo, ybuf.shape[1]), F32)

    @pl.when(jnp.logical_not(first))
    def _():
        ybuf[0:halo, :] = ybuf[tm:tm + halo, :]

    ybuf[halo:halo + tm, :] = _dot(h, wqkv_ref[...])
    taps = cw_ref.shape[0]
    y = cw_ref[0:1, :] * ybuf[pl.ds(halo - taps + 1, tm), :]
    for k in range(1, taps):
        y = y + cw_ref[k:k + 1, :] * ybuf[pl.ds(halo - taps + 1 + k, tm), :]
    hy = 0.5 * y
    y = hy + hy * jnp.tanh(hy)
    for s in range(3 * GDN_HEADS):
        sl = slice(s * GDN_DH, (s + 1) * GDN_DH)
        ys = y[:, sl]
        if s < 2 * GDN_HEADS:
            scale = GDN_DH ** -0.5 if s < GDN_HEADS else 1.0
            ys = ys * (lax.rsqrt(jnp.sum(ys * ys, axis=-1, keepdims=True) + EPS) * scale)
        oqkv_ref[:, sl] = ys.astype(BF16)


def _in_proj(x2, g, wqt, wkv, wvt, wgt, wqkv, wz, wab, qg, kg, vone, conv_w, tiles_per_seq, tm):
    n, d = x2.shape
    full = lambda a: pl.BlockSpec(a.shape, lambda i: (0,) * a.ndim)
    row = lambda w: pl.BlockSpec((tm, w), lambda i: (i, 0))
    colb = lambda r: pl.BlockSpec((r, tm), lambda i: (0, i))
    return pl.pallas_call(
        functools.partial(_inproj_body, tiles_per_seq=tiles_per_seq),
        grid=(n // tm,),
        in_specs=[row(d)] + [full(a) for a in (g, wqt, wkv, wvt, wgt, wqkv, wz, wab, qg, kg, vone, conv_w)],
        out_specs=[colb(wqt.shape[0]), row(wkv.shape[1]), colb(wvt.shape[0]), colb(wgt.shape[0]),
                   row(wqkv.shape[1]), row(wz.shape[1]), row(wab.shape[1])],
        out_shape=[jax.ShapeDtypeStruct((wqt.shape[0], n), BF16), jax.ShapeDtypeStruct((n, wkv.shape[1]), BF16),
                   jax.ShapeDtypeStruct((wvt.shape[0], n), BF16), jax.ShapeDtypeStruct((wgt.shape[0], n), F32),
                   jax.ShapeDtypeStruct((n, wqkv.shape[1]), BF16), jax.ShapeDtypeStruct((n, wz.shape[1]), BF16),
                   jax.ShapeDtypeStruct((n, wab.shape[1]), F32)],
        scratch_shapes=[pltpu.VMEM((tm + 8, wqkv.shape[1]), F32)],
        compiler_params=_cparams(("arbitrary",), 56),
        name="in_proj",
    )(x2, g, wqt, wkv, wvt, wgt, wqkv, wz, wab, qg, kg, vone, conv_w)


def _compress_body(x_ref, pos_ref, w1_ref, b1_ref, w2_ref, b2_ref, w2t_ref, b2t_ref, g_ref, ok_ref, ovt_ref,
                   *, n_cmp):
    is_key = pl.program_id(1) == 0
    nch = x_ref.shape[3]
    hids = []
    for grp in range(NSA_GROUPS):
        x = x_ref[0, 0, grp].astype(F32)
        xa = (x + pos_ref[0, 0]).astype(BF16)
        xb = (x + pos_ref[0, 1]).astype(BF16)
        a = _dot(xa, w1_ref[0, 0])
        b = _dot(xb, w1_ref[0, 1])
        b_next = pltpu.roll(b, nch - 1, 0)
        hids.append(jax.nn.gelu(a + b_next + b1_ref[0]).astype(BF16))

    @pl.when(is_key)
    def _():
        row = lax.broadcasted_iota(jnp.int32, (nch, NSA_DH), 0)
        outs = []
        for grp in range(NSA_GROUPS):
            out = _dot(hids[grp], w2_ref[0]) + b2_ref[0]
            out = out * lax.rsqrt(jnp.mean(out * out, axis=-1, keepdims=True) + EPS) * g_ref[...]
            outs.append(jnp.where(row < n_cmp, out, 0.0))
        ok_ref[0] = jnp.concatenate(outs, axis=-1).astype(BF16)

    @pl.when(jnp.logical_not(is_key))
    def _():
        col = lax.broadcasted_iota(jnp.int32, (NSA_DH, nch), 1)
        outs = []
        for grp in range(NSA_GROUPS):
            out = _dot_nt(w2t_ref[0], hids[grp]) + b2t_ref[0]
            outs.append(jnp.where(col < n_cmp, out, 0.0))
        ovt_ref[0] = jnp.concatenate(outs, axis=0).astype(BF16)


def _compress(xflat, pos, w1, b1, w2, b2, w2t, b2t, kc_g, n_cmp):
    b, _, _, nch, flat = xflat.shape
    return pl.pallas_call(
        functools.partial(_compress_body, n_cmp=n_cmp),
        grid=(b, 2),
        in_specs=[
            pl.BlockSpec((1, 1, NSA_GROUPS, nch, flat), lambda i, j: (i, j, 0, 0, 0)),
            pl.BlockSpec((1, 2, 1, flat), lambda i, j: (j, 0, 0, 0)),
            pl.BlockSpec((1, 2, flat, CMP_HIDDEN), lambda i, j: (j, 0, 0, 0)),
            pl.BlockSpec((1, 1, CMP_HIDDEN), lambda i, j: (j, 0, 0)),
            pl.BlockSpec((1, CMP_HIDDEN, NSA_DH), lambda i, j: (j, 0, 0)),
            pl.BlockSpec((1, 1, NSA_DH), lambda i, j: (j, 0, 0)),
            pl.BlockSpec((1, NSA_DH, CMP_HIDDEN), lambda i, j: (j, 0, 0)),
            pl.BlockSpec((1, NSA_DH, 1), lambda i, j: (j, 0, 0)),
            pl.BlockSpec((1, NSA_DH), lambda i, j: (0, 0)),
        ],
        out_specs=[pl.BlockSpec((1, nch, LANES), lambda i, j: (i, 0, 0)),
                   pl.BlockSpec((1, LANES, nch), lambda i, j: (i, 0, 0))],
        out_shape=[jax.ShapeDtypeStruct((b, nch, LANES), BF16), jax.ShapeDtypeStruct((b, LANES, nch), BF16)],
        compiler_params=_cparams(("parallel", "arbitrary"), 32),
        name="nsa_compress",
    )(xflat, pos, w1, b1, w2, b2, w2t, b2t, kc_g)


def _tile_heads(a):
    return jnp.concatenate([a] * NSA_HPG, axis=1)


def _nsa_body(qt_ref, gt_ref, kc_ref, vct_ref, ks_ref, kw_ref, vst_ref, vwt_ref, et_ref, ov_ref, o_ref, acc_sc, s_sc,
              *, n_cmp, n_top, kt):
    grp = pl.program_id(1)
    s0 = pl.program_id(2) * Q_BLOCK
    nch = kc_ref.shape[1]
    nblk = ov_ref.shape[0]

    qt = jnp.concatenate([qt_ref[h * LANES:(h + 1) * LANES, :] for h in range(NSA_HPG)], axis=1)
    t_row = s0 + lax.broadcasted_iota(jnp.int32, (1, Q_BLOCK), 1)

    cidx = lax.broadcasted_iota(jnp.int32, (nch, 1), 0)
    cvalid = (cidx * CMP_STRIDE + (CMP_BLOCK - 1) <= t_row) & (cidx < n_cmp)
    sc = _dot(kc_ref[0], qt) + _tile_heads(jnp.where(cvalid, 0.0, NEG))
    pc = jnp.exp2(sc - jnp.max(sc, axis=0, keepdims=True)).astype(BF16)
    stacked = jnp.concatenate([vct_ref[0], ov_ref[...], jnp.ones((8, nch), BF16)], axis=0)
    res = _dot(stacked, pc)
    inv = jnp.where(_tile_heads(t_row >= CMP_BLOCK - 1), 1.0 / jnp.maximum(res[LANES + nblk:LANES + nblk + 1], 1e-30),
                    0.0)
    oc = res[:LANES] * inv
    imp4 = res[LANES:LANES + nblk] * inv
    imp = (imp4[:, 0:Q_BLOCK] + imp4[:, Q_BLOCK:2 * Q_BLOCK] + imp4[:, 2 * Q_BLOCK:3 * Q_BLOCK]
           + imp4[:, 3 * Q_BLOCK:4 * Q_BLOCK])
    blk = lax.broadcasted_iota(jnp.int32, (nblk, Q_BLOCK), 0)
    cur = t_row // SLC_BLOCK
    imp = jnp.where(blk * SLC_BLOCK > t_row, NEG, imp)
    imp = jnp.where((blk == 0) | (blk == cur) | (blk == cur - 1), -NEG, imp)

    def pick_rounds(v, rounds):
        for _ in range(rounds):
            mx = jnp.max(v, axis=0, keepdims=True)
            first = jnp.min(jnp.where(v == mx, blk, nblk), axis=0, keepdims=True)
            v = jnp.where(blk == first, -jnp.inf, v)
        return v

    quarter = n_top // 4
    picked = pick_rounds(imp, quarter)

    wlen = WINDOW + Q_BLOCK
    w0 = pl.multiple_of(jnp.maximum(s0 - WINDOW, 0), Q_BLOCK)
    kpos = w0 + lax.broadcasted_iota(jnp.int32, (wlen, 1), 0)
    wbias = jnp.where((kpos <= t_row) & (kpos > t_row - WINDOW), 0.0, NEG)
    sw = _dot(kw_ref[0, pl.ds(w0, wlen), :], qt) + _tile_heads(wbias)
    picked = pick_rounds(picked, quarter)
    pw = jnp.exp2(sw - jnp.max(sw, axis=0, keepdims=True)).astype(BF16)
    picked = pick_rounds(picked, quarter)
    ow = _dot(vwt_ref[:, pl.ds(w0, wlen)], pw)
    ow = ow[:NSA_DH] / ow[NSA_DH:NSA_DH + 1]

    d0 = pl.multiple_of(s0, Q_BLOCK)
    dpos = s0 + lax.broadcasted_iota(jnp.int32, (Q_BLOCK, 1), 0)
    sd = _dot(ks_ref[0, pl.ds(d0, Q_BLOCK), :], qt) + _tile_heads(jnp.where(dpos <= t_row, 0.0, NEG))
    m_diag = jnp.max(sd, axis=0, keepdims=True)
    acc_sc[0] = _dot(vst_ref[:, pl.ds(d0, Q_BLOCK)], jnp.exp2(sd - m_diag).astype(BF16))
    acc_sc[1] = jnp.zeros(acc_sc.shape[1:], F32)
    picked = pick_rounds(picked, n_top - 3 * quarter)

    before = blk * SLC_BLOCK < s0
    selb = jnp.where((picked == -jnp.inf) & before, 0.0, MASKED).astype(BF16)
    rhs = jnp.concatenate([qt, _tile_heads(selb)], axis=0)

    last_tile = ks_ref.shape[1] // kt - 1

    def scores(idx, slot):
        k0 = pl.multiple_of(jnp.minimum(idx, last_tile) * kt, kt)
        lhs = jnp.concatenate([ks_ref[0, pl.ds(k0, kt), :], et_ref[pl.ds(k0, kt), :]], axis=1)
        s_sc[slot] = _dot(lhs, rhs)

    def update(idx, slot, m_old, acc_ref):
        k0 = pl.multiple_of(idx * kt, kt)
        m_new = jnp.maximum(m_old, jnp.max(s_sc[slot], axis=0, keepdims=True))
        p = jnp.exp2(s_sc[slot] - m_new).astype(BF16)
        acc_ref[...] = jnp.exp2(m_old - m_new) * acc_ref[...] + _dot(vst_ref[:, pl.ds(k0, kt)], p)
        return m_new

    def four_tiles(j, carry):
        m0, m1 = carry
        i = 4 * j
        scores(i + 2, 2)
        m0 = update(i, 0, m0, acc_sc.at[0])
        scores(i + 3, 3)
        m1 = update(i + 1, 1, m1, acc_sc.at[1])
        scores(i + 4, 0)
        m0 = update(i + 2, 2, m0, acc_sc.at[0])
        scores(i + 5, 1)
        m1 = update(i + 3, 3, m1, acc_sc.at[1])
        return m0, m1

    n_tiles = (s0 + kt - 1) // kt
    scores(0, 0)
    scores(1, 1)
    m0, m1 = lax.fori_loop(0, (n_tiles + 3) // 4, four_tiles,
                           (m_diag, jnp.full((1, NSA_HPG * Q_BLOCK), NEG, F32)))
    m_fin = jnp.maximum(m0, m1)
    acc = acc_sc[0] * jnp.exp2(m0 - m_fin) + acc_sc[1] * jnp.exp2(m1 - m_fin)
    osl = acc[:NSA_DH] / acc[NSA_DH:NSA_DH + 1]

    oc = jnp.where(grp == 0, oc[:NSA_DH], oc[NSA_DH:])
    gts = jax.nn.sigmoid(gt_ref[...])
    for h in range(NSA_HPG):
        cols = slice(h * Q_BLOCK, (h + 1) * Q_BLOCK)
        o_ref[h * NSA_DH:(h + 1) * NSA_DH, :] = (
            gts[3 * h:3 * h + 1, :] * oc[:, cols] + gts[3 * h + 1:3 * h + 2, :] * osl[:, cols]
            + gts[3 * h + 2:3 * h + 3, :] * ow[:, cols])


def _nsa_attention(qt, gt, kc, vct, okv, vt, expand_t, overlap, b, t, n_cmp, n_top, kt):
    nch = kc.shape[1]
    nq = t // Q_BLOCK
    n = b * t
    return pl.pallas_call(
        functools.partial(_nsa_body, n_cmp=n_cmp, n_top=n_top, kt=kt),
        grid=(b, NSA_GROUPS, nq),
        in_specs=[
            pl.BlockSpec((NSA_HPG * LANES, Q_BLOCK), lambda bi, g, i: (g, bi * nq + i)),
            pl.BlockSpec((GATE_ROWS, Q_BLOCK), lambda bi, g, i: (g, bi * nq + i)),
            pl.BlockSpec((1, nch, LANES), lambda bi, g, i: (bi, 0, 0)),
            pl.BlockSpec((1, LANES, nch), lambda bi, g, i: (bi, 0, 0)),
            pl.BlockSpec((1, t, LANES), lambda bi, g, i: (bi, 0, 2)),
            pl.BlockSpec((1, t, LANES), lambda bi, g, i: (bi, 0, 4)),
            pl.BlockSpec((LANES, t), lambda bi, g, i: (g, bi)),
            pl.BlockSpec((LANES, t), lambda bi, g, i: (NSA_GROUPS + g, bi)),
            pl.BlockSpec(expand_t.shape, lambda bi, g, i: (0, 0)),
            pl.BlockSpec(overlap.shape, lambda bi, g, i: (0, 0)),
        ],
        out_specs=pl.BlockSpec((NSA_HPG * NSA_DH, Q_BLOCK), lambda bi, g, i: (g, bi * nq + i)),
        out_shape=jax.ShapeDtypeStruct((NSA_W, n), F32),
        scratch_shapes=[pltpu.VMEM((2, LANES, NSA_HPG * Q_BLOCK), F32),
                        pltpu.VMEM((4, kt, NSA_HPG * Q_BLOCK), F32)],
        compiler_params=_cparams(("parallel", "parallel", "arbitrary"), 56),
        name="nsa_attention",
    )(qt, gt, kc, vct, okv, okv, vt, vt, expand_t, overlap)


def _split_bf16(a):
    hi = a.astype(BF16)
    return hi, (a - hi.astype(F32)).astype(BF16)


def _unit_lower_inverses(lmats):
    c = lmats[0].shape[0]
    r = lax.broadcasted_iota(jnp.int32, (c, c), 0)
    col = lax.broadcasted_iota(jnp.int32, (c, c), 1)
    eye = jnp.where(r == col, 1.0, 0.0)
    xs = [eye - l for l in lmats]
    ps = []
    for l in lmats:
        l16 = l.astype(BF16)
        ps.append(_dot(l16, l16))
    steps = int(np.log2(c)) - 1
    for s in range(steps):
        last = s + 1 == steps
        for i in range(len(lmats)):
            rhs = ps[i].astype(BF16)
            if last:
                xs[i] = xs[i] + _dot(xs[i].astype(BF16), rhs)
            else:
                both = _dot(jnp.concatenate([xs[i], ps[i]], axis=0).astype(BF16), rhs)
                xs[i] = xs[i] + both[:c]
                ps[i] = both[c:]
    return xs


def _gdn_body(x_ref, z_ref, ab_ref, alog_ref, dtb_ref, og_ref, o_ref, s_sc, *, ct):
    nb = x_ref.shape[0]

    @pl.when(pl.program_id(0) == 0)
    def _():
        s_sc[...] = jnp.zeros(s_sc.shape, F32)

    ch = GDN_CHUNK
    r = lax.broadcasted_iota(jnp.int32, (ch, ch), 0)
    col = lax.broadcasted_iota(jnp.int32, (ch, ch), 1)
    incl = r >= col
    strict = r > col
    tril16 = jnp.concatenate([jnp.where(incl, 1.0, 0.0).astype(BF16)] * 3, axis=1)

    units = []
    for ci in range(ct // ch):
        rows = slice(ci * ch, (ci + 1) * ch)
        for bi in range(nb):
            ab = ab_ref[bi, rows, :]
            g_all = -jnp.exp(alog_ref[...]) * jax.nn.softplus(ab + dtb_ref[...])
            beta_all = jax.nn.sigmoid(ab)
            g_hi, g_lo = _split_bf16(g_all)
            g_lo2 = (g_all - g_hi.astype(F32) - g_lo.astype(F32)).astype(BF16)
            gc_all = _dot(tril16, jnp.concatenate([g_hi, g_lo, g_lo2], axis=0))
            gc_t = gc_all.T
            for h in range(GDN_HEADS):
                hs = slice(h * GDN_DH, (h + 1) * GDN_DH)
                q16 = x_ref[bi, rows, hs]
                k16 = x_ref[bi, rows, GDN_W + h * GDN_DH:GDN_W + (h + 1) * GDN_DH]
                qh, kh = q16.astype(F32), k16.astype(F32)
                vh = x_ref[bi, rows, 2 * GDN_W + h * GDN_DH:2 * GDN_W + (h + 1) * GDN_DH].astype(F32)
                gc = gc_all[:, h:h + 1]
                gr = gc_t[h:h + 1, :]
                g_last = gc_all[ch - 1:ch, h:h + 1]
                beta = beta_all[:, GDN_HEADS + h:GDN_HEADS + h + 1]
                eg = jnp.exp(gc)
                decay = jnp.where(incl, jnp.exp(jnp.minimum(gc - gr, 0.0)), 0.0)
                kb = kh * beta
                units.append(dict(
                    rows=rows, bi=bi, h=h,
                    lmat=jnp.where(strict, _dot_nt(kb.astype(BF16), k16) * decay, 0.0),
                    vb=(vh * beta).astype(BF16), kbg=(kb * eg).astype(BF16),
                    qk=jnp.where(incl, _dot_nt(q16, k16) * decay, 0.0).astype(BF16),
                    qg=(qh * eg).astype(BF16), kd_t=(kh * jnp.exp(g_last - gc)).T.astype(BF16),
                    gl=jnp.exp(g_last)))
    tinvs = _unit_lower_inverses([u["lmat"] for u in units])
    for u, tinv in zip(units, tinvs):
        tinv = tinv.astype(BF16)
        u["u"] = _dot(tinv, u["vb"])
        u["w"] = _dot(tinv, u["kbg"]).astype(BF16)

    for u in units:
        bi, h, rows = u["bi"], u["h"], u["rows"]
        hs = slice(h * GDN_DH, (h + 1) * GDN_DH)
        s_old = s_sc[bi * GDN_HEADS + h]
        s16 = s_old.astype(BF16)
        v_new = (u["u"] - _dot(u["w"], s16)).astype(BF16)
        s_sc[bi * GDN_HEADS + h] = s_old * u["gl"] + _dot(u["kd_t"], v_new)
        o = _dot(u["qg"], s16) + _dot(u["qk"], v_new)
        on = o * lax.rsqrt(jnp.mean(o * o, axis=-1, keepdims=True) + EPS) * og_ref[...]
        zh = z_ref[bi, rows, hs].astype(F32)
        o_ref[bi, rows, hs] = (on * (zh * jax.nn.sigmoid(zh))).astype(BF16)


def _gdn(oqkv, oz, oab, alog, dtb, og, ct=128):
    b, t, w3 = oqkv.shape
    full = lambda a: pl.BlockSpec(a.shape, lambda c: (0,) * a.ndim)
    return pl.pallas_call(
        functools.partial(_gdn_body, ct=ct),
        grid=(t // ct,),
        in_specs=[
            pl.BlockSpec((b, ct, w3), lambda c: (0, c, 0)),
            pl.BlockSpec((b, ct, GDN_W), lambda c: (0, c, 0)),
            pl.BlockSpec((b, ct, LANES), lambda c: (0, c, 0)),
            full(alog), full(dtb), full(og),
        ],
        out_specs=pl.BlockSpec((b, ct, GDN_W), lambda c: (0, c, 0)),
        out_shape=jax.ShapeDtypeStruct((b, t, GDN_W), BF16),
        scratch_shapes=[pltpu.VMEM((b * GDN_HEADS, GDN_DH, GDN_DH), F32)],
        compiler_params=_cparams(("arbitrary",), 32),
        name="gdn",
    )(oqkv, oz, oab, alog, dtb, og)


def _outproj_body(ont_ref, og_ref, x_ref, ng_ref, wo_ref, fg_ref, wr_ref, br_ref,
                  x1_ref, h2_ref, idx_ref, gate_ref, rank_ref, cnt_ref, cnt_sc):
    i = pl.program_id(0)
    tm = x_ref.shape[0]

    @pl.when(i == 0)
    def _():
        cnt_sc[...] = jnp.zeros(cnt_sc.shape, F32)

    a = ont_ref[...]
    a = (a * lax.rsqrt(jnp.mean(a * a, axis=0, keepdims=True) + EPS) * ng_ref[...]).astype(BF16)
    x1 = x_ref[...] + _dot_tn(a, wo_ref[0:NSA_W, :]) + _dot(og_ref[...], wo_ref[NSA_W:, :])
    x1_ref[...] = x1
    h2f = x1 * lax.rsqrt(jnp.mean(x1 * x1, axis=-1, keepdims=True) + EPS) * fg_ref[...]
    _store_pieces(h2_ref, h2f)
    h2 = h2f.astype(BF16)

    lane = lax.broadcasted_iota(jnp.int32, (tm, LANES), 1)
    logits = jnp.where(lane < N_EXPERTS, _dot(h2, wr_ref[...]) + br_ref[...], -jnp.inf)
    onehot = jnp.zeros((tm, LANES), F32)
    idx = jnp.zeros((tm, LANES), jnp.int32)
    vals = jnp.zeros((tm, LANES), F32)
    firsts = []
    v = logits
    top0 = None
    for k in range(TOP_K):
        mx = jnp.max(v, axis=-1, keepdims=True)
        first = jnp.min(jnp.where(v == mx, lane, LANES), axis=-1, keepdims=True)
        hit = lane == first
        v = jnp.where(hit, -jnp.inf, v)
        onehot = jnp.where(hit, 1.0, onehot)
        idx = jnp.where(lane == k, first, idx)
        top0 = mx if k == 0 else top0
        vals = jnp.where(lane == k, jnp.exp(mx - top0), vals)
        firsts.append(first)
    idx_ref[...] = idx
    gate_ref[...] = vals / jnp.sum(vals, axis=-1, keepdims=True)

    r = lax.broadcasted_iota(jnp.int32, (tm, tm), 0)
    c = lax.broadcasted_iota(jnp.int32, (tm, tm), 1)
    before = jnp.where(r > c, 1.0, 0.0).astype(BF16)
    excl = cnt_sc[...] + _dot(before, onehot.astype(BF16))
    rank = jnp.zeros((tm, LANES), F32)
    for k in range(TOP_K):
        rk = jnp.sum(jnp.where(lane == firsts[k], excl, 0.0), axis=-1, keepdims=True)
        rank = jnp.where(lane == k, rk, rank)
    rank_ref[...] = rank.astype(jnp.int32)
    cnt_sc[...] = cnt_sc[...] + jnp.sum(onehot, axis=0, keepdims=True)
    cnt_ref[...] = cnt_sc[...].astype(jnp.int32)


def _out_proj(o_nsa_t, o_gdn, x2, ng, wo, fg, wr, br, tm=512):
    n, d = x2.shape
    full = lambda a: pl.BlockSpec(a.shape, lambda i: (0,) * a.ndim)
    row = lambda w: pl.BlockSpec((tm, w), lambda i: (i, 0))
    return pl.pallas_call(
        _outproj_body,
        grid=(n // tm,),
        in_specs=[pl.BlockSpec((NSA_W, tm), lambda i: (0, i)), row(GDN_W), row(d), full(ng), full(wo), full(fg),
                  full(wr), full(br)],
        out_specs=[row(d), pl.BlockSpec((d // 2 // SC_SUBROW, tm, SC_SUBROW), lambda i: (0, i, 0)),
                   row(LANES), row(LANES), row(LANES), pl.BlockSpec((1, LANES), lambda i: (0, 0))],
        out_shape=[jax.ShapeDtypeStruct((n, d), F32),
                   jax.ShapeDtypeStruct((d // 2 // SC_SUBROW, n, SC_SUBROW), jnp.int32),
                   jax.ShapeDtypeStruct((n, LANES), jnp.int32), jax.ShapeDtypeStruct((n, LANES), F32),
                   jax.ShapeDtypeStruct((n, LANES), jnp.int32), jax.ShapeDtypeStruct((1, LANES), jnp.int32)],
        scratch_shapes=[pltpu.VMEM((1, LANES), F32)],
        compiler_params=_cparams(("arbitrary",), 48),
        name="out_proj_router",
    )(o_nsa_t, o_gdn, x2, ng, wo, fg, wr, br)


def _expert_body(be_ref, fresh_ref, slot_ref, next_ref, xs_ref, wg_hbm, bg_ref, wu_hbm, bu_ref, wd_hbm, bd_ref,
                 y_ref, wf32, w16, sems):
    i = pl.program_id(0)
    used = i < be_ref[pl.num_programs(0)]
    hbm = (wg_hbm, wu_hbm, wd_hbm)

    def weight_copy(expert, slot, j):
        return pltpu.make_async_copy(hbm[j].at[expert], wf32.at[slot, j], sems.at[slot, j])

    @pl.when((i == 0) & used)
    def _():
        for j in range(3):
            weight_copy(be_ref[0], 0, j).start()

    @pl.when(used & (fresh_ref[i] == 1))
    def _():
        slot = slot_ref[i]
        for j in range(3):
            weight_copy(be_ref[i], slot, j).wait()
            w16[j] = wf32[slot, j].astype(BF16)

        @pl.when(next_ref[i] >= 0)
        def _():
            for j in range(3):
                weight_copy(next_ref[i], 1 - slot, j).start()

    @pl.when(jnp.logical_not(used))
    def _():
        y_ref[...] = jnp.zeros(y_ref.shape, y_ref.dtype)

    @pl.when(used)
    def _():
        x = _join_pieces(xs_ref).astype(BF16)
        gate = jnp.minimum(_dot(x, w16[0]) + bg_ref[0], SWIGLU_LIMIT)
        up = jnp.clip(_dot(x, w16[1]) + bu_ref[0], -SWIGLU_LIMIT, SWIGLU_LIMIT)
        glu = gate * jax.nn.sigmoid(gate * SWIGLU_ALPHA)
        _store_pieces(y_ref, _dot(((up + 1.0) * glu).astype(BF16), w16[2]) + bd_ref[0])


def _experts(blk_e, fresh, slot, nxt, xs, wg, bg, wu, bu, wd, bd):
    pieces, n_rows, sub = xs.shape
    d, de = wg.shape[1], wg.shape[2]
    assert d == de
    r = MOE_ROW_BLOCK
    bspec = lambda w: pl.BlockSpec((1, 1, w), lambda i, be, *_: (be[i], 0, 0))
    hbm = pl.BlockSpec(memory_space=pl.ANY)
    grid_spec = pltpu.PrefetchScalarGridSpec(
        num_scalar_prefetch=4,
        grid=(n_rows // r,),
        in_specs=[pl.BlockSpec((pieces, r, sub), lambda i, *_: (0, i, 0)),
                  hbm, bspec(de), hbm, bspec(de), hbm, bspec(d)],
        out_specs=pl.BlockSpec((pieces, r, sub), lambda i, *_: (0, i, 0)),
        scratch_shapes=[pltpu.VMEM((2, 3, d, de), F32), pltpu.VMEM((3, d, de), BF16),
                        pltpu.SemaphoreType.DMA((2, 3))],
    )
    return pl.pallas_call(
        _expert_body,
        grid_spec=grid_spec,
        out_shape=jax.ShapeDtypeStruct((pieces, n_rows, sub), jnp.int32),
        compiler_params=_cparams(("arbitrary",), 56),
        name="moe_experts",
    )(blk_e, fresh, slot, nxt, xs, wg, bg, wu, bu, wd, bd)


SC_WINDOW = 128
SC_SUBROW = 256


def _sc_mesh():
    return plsc.VectorSubcoreMesh(core_axis_name="c", subcore_axis_name="s")


def _sc_dispatch(h2, dest_rows, n_rows):
    n, d = h2.shape

    @functools.partial(pl.kernel, out_type=jax.ShapeDtypeStruct((n_rows, d), h2.dtype), mesh=_sc_mesh())
    def dispatch(x_hbm, *refs):
        idx_hbm, o_hbm = refs[:TOP_K], refs[TOP_K]

        def body(x_vmem, *idx_vmem):
            for iv in idx_vmem:
                pltpu.sync_copy(x_vmem, o_hbm.at[iv.at[0]])

        pltpu.emit_pipeline(
            body,
            grid=(n // SC_WINDOW,),
            in_specs=[pl.BlockSpec((SC_WINDOW, d), lambda i: (i, 0))]
                     + [pl.BlockSpec((1, SC_WINDOW), lambda i: (0, i))] * TOP_K,
            out_specs=[],
            core_axis_name=("c", "s"),
            dimension_semantics=(pltpu.PARALLEL,),
        )(x_hbm, *idx_hbm)

    return dispatch(h2, *dest_rows)


def _sc_gather(table, idx):
    _, d = table.shape
    m = idx.shape[1]

    @functools.partial(pl.kernel, out_type=jax.ShapeDtypeStruct((m, d), table.dtype), mesh=_sc_mesh())
    def gather(t_hbm, i_hbm, o_hbm):
        def body(i_vmem, o_vmem):
            pltpu.sync_copy(t_hbm.at[i_vmem.at[0]], o_vmem)

        pltpu.emit_pipeline(
            body,
            grid=(m // SC_WINDOW,),
            in_specs=[pl.BlockSpec((1, SC_WINDOW), lambda i: (0, i))],
            out_specs=[pl.BlockSpec((SC_WINDOW, d), lambda i: (i, 0))],
            core_axis_name=("c", "s"),
            dimension_semantics=(pltpu.PARALLEL,),
        )(i_hbm, o_hbm)

    return gather(table, idx)


def _combine_body(x1_ref, y_ref, gate_ref, o_ref):
    acc = x1_ref[...]
    for k in range(TOP_K):
        acc = acc + gate_ref[:, k:k + 1] * _join_pieces(y_ref.at[k])
    o_ref[...] = acc


def _combine(x1, y4, gates, tm=512):
    n, d = x1.shape
    pieces, sub = y4.shape[1], y4.shape[3]
    row = lambda w: pl.BlockSpec((tm, w), lambda i: (i, 0))
    return pl.pallas_call(
        _combine_body,
        grid=(n // tm,),
        in_specs=[row(d), pl.BlockSpec((TOP_K, pieces, tm, sub), lambda i: (0, 0, i, 0)), row(LANES)],
        out_specs=row(d),
        out_shape=jax.ShapeDtypeStruct((n, d), F32),
        compiler_params=_cparams(("parallel",), 48),
        name="moe_combine",
    )(x1, y4, gates)


def _pad_lanes(a, width=LANES):
    return jnp.pad(a, ((0, 0), (0, width - a.shape[1])))


def _layer(x, attn_norm_g, w_in, q_g, kc_g, ks_g, kw_g, ck_pos, ck_w1, ck_b1, ck_w2, ck_b2,
           cv_pos, cv_w1, cv_b1, cv_w2, cv_b2, nsa_out_g, conv_w, a_log, dt_bias, gdn_out_g, w_out,
           ffn_g, router_w, router_b, e_wg, e_bg, e_wu, e_bu, e_wd, e_bd):
    b, t, d = x.shape
    n = b * t
    x2 = x.reshape(n, d)

    o = np.cumsum([0, NSA_W] + [NSA_GROUPS * NSA_DH] * 6 + [3 * NSA_HEADS, 3 * GDN_W, GDN_W, GDN_HEADS, GDN_HEADS])
    wq_t = w_in[:, o[0]:o[1]].T.reshape(NSA_GROUPS, NSA_HPG, NSA_DH, d)
    zq = jnp.zeros((NSA_HPG, NSA_DH, d), F32)
    wq_t = jnp.stack([jnp.concatenate([wq_t[0], zq], axis=1), jnp.concatenate([zq, wq_t[1]], axis=1)])
    wq_t = wq_t.reshape(NSA_HEADS * LANES, d).astype(BF16)
    qg1 = q_g * (NSA_DH ** -0.5 * np.log2(np.e))
    zg = jnp.zeros((NSA_DH,), F32)
    qg_col = jnp.concatenate([jnp.tile(jnp.concatenate([qg1, zg]), NSA_HPG),
                              jnp.tile(jnp.concatenate([zg, qg1]), NSA_HPG)]).reshape(NSA_HEADS * LANES, 1)
    wkv = w_in[:, o[1]:o[7]].astype(BF16)
    ones = jnp.ones((LANES,), F32)
    kg = jnp.concatenate([ones, ones, ks_g, ks_g, ones, kw_g, kw_g, ones]).reshape(1, 6 * LANES)
    wv_t = jnp.concatenate([w_in[:, o[4]:o[5]], w_in[:, o[6]:o[7]]], axis=1).T.reshape(2 * NSA_GROUPS, NSA_DH, d)
    wv_t = jnp.pad(wv_t, ((0, 0), (0, LANES - NSA_DH), (0, 0))).reshape(2 * NSA_GROUPS * LANES, d).astype(BF16)
    vone = jnp.asarray((np.arange(2 * NSA_GROUPS * LANES) % LANES == NSA_DH).astype(np.float32)[:, None])
    wg_t = w_in[:, o[7]:o[8]].T.reshape(NSA_GROUPS, NSA_HPG * 3, d)
    wg_t = jnp.pad(wg_t, ((0, 0), (0, GATE_ROWS - NSA_HPG * 3), (0, 0))).reshape(NSA_GROUPS * GATE_ROWS, d)
    wg_t = wg_t.astype(BF16)
    wab = _pad_lanes(w_in[:, o[10]:o[12]]).astype(BF16)
    wqkv = w_in[:, o[8]:o[9]].astype(BF16)
    wz = w_in[:, o[9]:o[10]].astype(BF16)

    tm = min(512, t)
    oqt, okv, ovt, ogt, oqkv, oz, oab = _in_proj(x2, attn_norm_g.reshape(1, d), wq_t, wkv, wv_t, wg_t, wqkv, wz,
                                                 wab, qg_col, kg, vone, conv_w, t // tm, tm)

    nch = t // CMP_STRIDE
    n_cmp = (t - CMP_BLOCK) // CMP_STRIDE + 1
    half = CMP_STRIDE * NSA_DH
    xflat = okv[:, :2 * LANES].reshape(b, nch, CMP_STRIDE, 2, NSA_GROUPS, NSA_DH)
    xflat = xflat.transpose(0, 3, 4, 1, 2, 5).reshape(b, 2, NSA_GROUPS, nch, half)
    pos = jnp.stack([ck_pos, cv_pos]).reshape(2, 2, 1, half)
    w1 = jnp.stack([ck_w1, cv_w1]).reshape(2, 2, half, CMP_HIDDEN).astype(BF16)
    b1 = jnp.stack([ck_b1, cv_b1]).reshape(2, 1, CMP_HIDDEN)
    w2 = jnp.stack([ck_w2, cv_w2]).astype(BF16)
    b2 = jnp.stack([ck_b2, cv_b2]).reshape(2, 1, NSA_DH)
    w2t = jnp.stack([ck_w2.T, cv_w2.T]).astype(BF16)
    b2t = jnp.stack([ck_b2, cv_b2]).reshape(2, NSA_DH, 1)
    kc, vct = _compress(xflat, pos, w1, b1, w2, b2, w2t, b2t, kc_g.reshape(1, NSA_DH), n_cmp)

    n_slc = t // SLC_BLOCK
    n_top = min(SLC_TOPK, n_slc)
    nblk = max(n_slc, LANES)
    kt = min(256, t // 4)
    assert (t // kt) % 4 == 0
    ci = np.arange(nch)[None, :] * CMP_STRIDE
    sj = np.arange(nblk)[:, None] * SLC_BLOCK
    overlap = ((ci < sj + SLC_BLOCK) & (ci + CMP_BLOCK > sj) & (np.arange(nch)[None, :] < n_cmp)
               & (np.arange(nblk)[:, None] < n_slc))
    expand_t = (np.arange(t)[:, None] // SLC_BLOCK) == np.arange(nblk)[None, :]
    o_nsa_t = _nsa_attention(oqt, ogt, kc, vct, okv.reshape(b, t, -1), ovt, jnp.asarray(expand_t, BF16),
                             jnp.asarray(overlap, BF16), b, t, n_cmp, n_top, kt)

    alog_row = _pad_lanes(a_log.reshape(1, GDN_HEADS))
    dtb_row = _pad_lanes(dt_bias.reshape(1, GDN_HEADS))
    o_gdn = _gdn(oqkv.reshape(b, t, -1), oz.reshape(b, t, -1), oab.reshape(b, t, -1),
                 alog_row, dtb_row, gdn_out_g.reshape(1, GDN_DH))

    wr = _pad_lanes(router_w).astype(BF16)
    br = _pad_lanes(router_b.reshape(1, N_EXPERTS))
    x1, h2, idx, gates, rank, counts = _out_proj(
        o_nsa_t, o_gdn.reshape(n, GDN_W), x2, nsa_out_g.reshape(NSA_W, 1),
        w_out.astype(BF16), ffn_g.reshape(1, d), wr, br)

    r = MOE_ROW_BLOCK
    nk = n * TOP_K
    counts = counts[0, :N_EXPERTS]
    pcounts = (counts + r - 1) // r * r
    pends = jnp.cumsum(pcounts)
    pstarts = pends - pcounts
    top_idx = idx[:, :TOP_K]
    dest = pstarts[top_idx] + rank[:, :TOP_K]
    n_rows = (nk + r - 1) // r * r + N_EXPERTS * r
    n_blocks = n_rows // r
    blk_start = jnp.arange(n_blocks, dtype=jnp.int32)[:, None] * r
    blk_e = jnp.minimum(jnp.sum(pends[None, :] <= blk_start, axis=1), N_EXPERTS - 1).astype(jnp.int32)
    n_used = (pends[-1] // r).astype(jnp.int32)
    fresh = (jnp.arange(n_blocks) < n_used) & (blk_e != jnp.concatenate([jnp.full((1,), -1, jnp.int32), blk_e[:-1]]))
    slot = ((jnp.cumsum(fresh) - 1) % 2).astype(jnp.int32)
    eid = jnp.arange(N_EXPERTS, dtype=jnp.int32)
    later = jnp.where((eid[None, :] > eid[:, None]) & (pcounts[None, :] > 0), eid[None, :], N_EXPERTS)
    next_expert = jnp.min(later, axis=1)
    next_expert = jnp.where(next_expert < N_EXPERTS, next_expert, -1).astype(jnp.int32)
    nxt = next_expert[blk_e]
    blk_e = jnp.concatenate([blk_e, n_used[None]])
    pieces = d // 2 // SC_SUBROW
    dest_p = (dest.T.astype(jnp.int32)[:, None, :]
              + (jnp.arange(pieces, dtype=jnp.int32) * n_rows)[None, :, None])
    xs = _sc_dispatch(h2.reshape(pieces * n, SC_SUBROW), [dest_p[k].reshape(1, pieces * n) for k in range(TOP_K)],
                      pieces * n_rows)
    ys = _experts(blk_e, fresh.astype(jnp.int32), slot, nxt, xs.reshape(pieces, n_rows, SC_SUBROW), e_wg, e_bg.reshape(N_EXPERTS, 1, -1), e_wu,
                  e_bu.reshape(N_EXPERTS, 1, -1), e_wd, e_bd.reshape(N_EXPERTS, 1, -1))
    y4 = _sc_gather(ys.reshape(pieces * n_rows, SC_SUBROW), dest_p.reshape(1, nk * pieces))
    return _combine(x1, y4.reshape(TOP_K, pieces, n, SC_SUBROW), gates).reshape(b, t, d)


def kernel(x, attn_norm_g, w_in, nsa_q_norm_g, nsa_kc_norm_g, nsa_ks_norm_g, nsa_kw_norm_g, cmp_k_pos, cmp_k_w1, cmp_k_b1, cmp_k_w2, cmp_k_b2, cmp_v_pos, cmp_v_w1, cmp_v_b1, cmp_v_w2, cmp_v_b2, nsa_out_norm_g, gdn_conv_w, gdn_a_log, gdn_dt_bias, gdn_out_norm_g, w_out, ffn_norm_g, router_w, router_b, exp_w_gate, exp_b_gate, exp_w_up, exp_b_up, exp_w_down, exp_b_down):
    params = (attn_norm_g, w_in, nsa_q_norm_g, nsa_kc_norm_g, nsa_ks_norm_g, nsa_kw_norm_g,
              cmp_k_pos, cmp_k_w1, cmp_k_b1, cmp_k_w2, cmp_k_b2, cmp_v_pos, cmp_v_w1, cmp_v_b1, cmp_v_w2, cmp_v_b2,
              nsa_out_norm_g, gdn_conv_w, gdn_a_log, gdn_dt_bias, gdn_out_norm_g, w_out, ffn_norm_g,
              router_w, router_b, exp_w_gate, exp_b_gate, exp_w_up, exp_b_up, exp_w_down, exp_b_down)
    for l in range(attn_norm_g.shape[0]):
        x = _layer(x, *(p[l] for p in params))
    return x
```

```python
import functools

import jax
import jax.numpy as jnp
import numpy as np
from jax import lax
from jax.experimental import pallas as pl
from jax.experimental.pallas import tpu as pltpu
from jax.experimental.pallas import tpu_sc as plsc

F32 = jnp.float32
BF16 = jnp.bfloat16

EPS = 1e-6
NEG = -1e30
MASKED = -2.0 ** 100

NSA_HEADS = 8
NSA_GROUPS = 2
NSA_HPG = 4
NSA_DH = 64
CMP_BLOCK = 32
CMP_STRIDE = 16
CMP_HIDDEN = 256
SLC_BLOCK = 64
SLC_TOPK = 16
WINDOW = 512
Q_BLOCK = 128
GDN_HEADS = 4
GDN_DH = 128
GDN_CONV = 4
GDN_CHUNK = 64
N_EXPERTS = 32
TOP_K = 4
SWIGLU_LIMIT = 7.0
SWIGLU_ALPHA = 1.702
MOE_ROW_BLOCK = 256

LANES = 128
GATE_ROWS = 16
NSA_W = NSA_HEADS * NSA_DH
GDN_W = GDN_HEADS * GDN_DH

_NT = (((1,), (1,)), ((), ()))
_TN = (((0,), (0,)), ((), ()))


def _cparams(sem, vmem_mb):
    return pltpu.CompilerParams(dimension_semantics=sem, vmem_limit_bytes=vmem_mb * 1024 * 1024)


def _dot(a, b):
    return jnp.dot(a, b, preferred_element_type=F32)


def _dot_nt(a, b):
    return lax.dot_general(a, b, _NT, preferred_element_type=F32)


def _dot_tn(a, b):
    return lax.dot_general(a, b, _TN, preferred_element_type=F32)


def _store_pieces(ref, val):
    half = val.shape[1] // 2
    hi = lax.bitcast_convert_type(val[:, :half].astype(BF16).astype(F32), jnp.uint32)
    lo = lax.bitcast_convert_type(val[:, half:].astype(BF16).astype(F32), jnp.uint32)
    words = lax.bitcast_convert_type(hi | (lo >> 16), jnp.int32)
    sub = ref.shape[2]
    for j in range(ref.shape[0]):
        ref[j] = words[:, j * sub:(j + 1) * sub]


def _join_pieces(ref):
    words = jnp.concatenate([ref[j] for j in range(ref.shape[0])], axis=1)
    words = lax.bitcast_convert_type(words, jnp.uint32)
    hi = lax.bitcast_convert_type(words & jnp.uint32(0xFFFF0000), F32)
    lo = lax.bitcast_convert_type(words << 16, F32)
    return jnp.concatenate([hi, lo], axis=1)


def _inproj_body(x_ref, g_ref, wqt_ref, wkv_ref, wvt_ref, wgt_ref, wqkv_ref, wz_ref, wab_ref, qg_ref, kg_ref,
                 vone_ref, cw_ref, oqt_ref, okv_ref, ovt_ref, ogt_ref, oqkv_ref, oz_ref, oab_ref, ybuf,
                 *, tiles_per_seq):
    x = x_ref[...]
    h = (x * lax.rsqrt(jnp.mean(x * x, axis=-1, keepdims=True) + EPS) * g_ref[...]).astype(BF16)
    tm = x.shape[0]

    yq = _dot_nt(wqt_ref[...], h)
    for s in range(NSA_HEADS):
        sl = slice(s * LANES, (s + 1) * LANES)
        ys = yq[sl, :]
        ms = jnp.sum(ys * ys, axis=0, keepdims=True) * (1.0 / NSA_DH)
        oqt_ref[sl, :] = (ys * lax.rsqrt(ms + EPS) * qg_ref[sl, :]).astype(BF16)

    ykv = _dot(h, wkv_ref[...])
    lane = lax.broadcasted_iota(jnp.int32, (tm, LANES), 1)
    low = lane < NSA_DH
    for s in range(6):
        sl = slice(s * LANES, (s + 1) * LANES)
        ys = ykv[:, sl]
        if s in (2, 4):
            y2 = ys * ys
            s0 = jnp.sum(jnp.where(low, y2, 0.0), axis=-1, keepdims=True)
            s1 = jnp.sum(jnp.where(low, 0.0, y2), axis=-1, keepdims=True)
            ms = jnp.where(low, s0, s1) * (1.0 / NSA_DH)
            ys = ys * lax.rsqrt(ms + EPS) * kg_ref[:, sl]
        okv_ref[:, sl] = ys.astype(BF16)

    ovt_ref[...] = (_dot_nt(wvt_ref[...], h) + vone_ref[...]).astype(BF16)
    ogt_ref[...] = _dot_nt(wgt_ref[...], h)
    oz_ref[...] = _dot(h, wz_ref[...]).astype(BF16)
    oab_ref[...] = _dot(h, wab_ref[...])

    halo = ybuf.shape[0] - tm
    first = pl.program_id(0) % tiles_per_seq == 0

    @pl.when(first)
    def _():
        ybuf[0:halo, :] = jnp.zeros((halo, ybuf.shape[1]), F32)

    @pl.when(jnp.logical_not(first))
    def _():
        ybuf[0:halo, :] = ybuf[tm:tm + halo, :]

    ybuf[halo:halo + tm, :] = _dot(h, wqkv_ref[...])
    taps = cw_ref.shape[0]
    y = cw_ref[0:1, :] * ybuf[pl.ds(halo - taps + 1, tm), :]
    for k in range(1, taps):
        y = y + cw_ref[k:k + 1, :] * ybuf[pl.ds(halo - taps + 1 + k, tm), :]
    hy = 0.5 * y
    y = hy + hy * jnp.tanh(hy)
    for s in range(3 * GDN_HEADS):
        sl = slice(s * GDN_DH, (s + 1) * GDN_DH)
        ys = y[:, sl]
        if s < 2 * GDN_HEADS:
            scale = GDN_DH ** -0.5 if s < GDN_HEADS else 1.0
            ys = ys * (lax.rsqrt(jnp.sum(ys * ys, axis=-1, keepdims=True) + EPS) * scale)
        oqkv_ref[:, sl] = ys.astype(BF16)


def _in_proj(x2, g, wqt, wkv, wvt, wgt, wqkv, wz, wab, qg, kg, vone, conv_w, tiles_per_seq, tm):
    n, d = x2.shape
    full = lambda a: pl.BlockSpec(a.shape, lambda i: (0,) * a.ndim)
    row = lambda w: pl.BlockSpec((tm, w), lambda i: (i, 0))
    colb = lambda r: pl.BlockSpec((r, tm), lambda i: (0, i))
    return pl.pallas_call(
        functools.partial(_inproj_body, tiles_per_seq=tiles_per_seq),
        grid=(n // tm,),
        in_specs=[row(d)] + [full(a) for a in (g, wqt, wkv, wvt, wgt, wqkv, wz, wab, qg, kg, vone, conv_w)],
        out_specs=[colb(wqt.shape[0]), row(wkv.shape[1]), colb(wvt.shape[0]), colb(wgt.shape[0]),
                   row(wqkv.shape[1]), row(wz.shape[1]), row(wab.shape[1])],
        out_shape=[jax.ShapeDtypeStruct((wqt.shape[0], n), BF16), jax.ShapeDtypeStruct((n, wkv.shape[1]), BF16),
                   jax.ShapeDtypeStruct((wvt.shape[0], n), BF16), jax.ShapeDtypeStruct((wgt.shape[0], n), F32),
                   jax.ShapeDtypeStruct((n, wqkv.shape[1]), BF16), jax.ShapeDtypeStruct((n, wz.shape[1]), BF16),
                   jax.ShapeDtypeStruct((n, wab.shape[1]), F32)],
        scratch_shapes=[pltpu.VMEM((tm + 8, wqkv.shape[1]), F32)],
        compiler_params=_cparams(("arbitrary",), 56),
        name="in_proj",
    )(x2, g, wqt, wkv, wvt, wgt, wqkv, wz, wab, qg, kg, vone, conv_w)


def _compress_body(x_ref, pos_ref, w1_ref, b1_ref, w2_ref, b2_ref, w2t_ref, b2t_ref, g_ref, ok_ref, ovt_ref,
                   *, n_cmp):
    is_key = pl.program_id(1) == 0
    nch = x_ref.shape[3]
    hids = []
    for grp in range(NSA_GROUPS):
        x = x_ref[0, 0, grp].astype(F32)
        xa = (x + pos_ref[0, 0]).astype(BF16)
        xb = (x + pos_ref[0, 1]).astype(BF16)
        a = _dot(xa, w1_ref[0, 0])
        b = _dot(xb, w1_ref[0, 1])
        b_next = pltpu.roll(b, nch - 1, 0)
        hids.append(jax.nn.gelu(a + b_next + b1_ref[0]).astype(BF16))

    @pl.when(is_key)
    def _():
        row = lax.broadcasted_iota(jnp.int32, (nch, NSA_DH), 0)
        outs = []
        for grp in range(NSA_GROUPS):
            out = _dot(hids[grp], w2_ref[0]) + b2_ref[0]
            out = out * lax.rsqrt(jnp.mean(out * out, axis=-1, keepdims=True) + EPS) * g_ref[...]
            outs.append(jnp.where(row < n_cmp, out, 0.0))
        ok_ref[0] = jnp.concatenate(outs, axis=-1).astype(BF16)

    @pl.when(jnp.logical_not(is_key))
    def _():
        col = lax.broadcasted_iota(jnp.int32, (NSA_DH, nch), 1)
        outs = []
        for grp in range(NSA_GROUPS):
            out = _dot_nt(w2t_ref[0], hids[grp]) + b2t_ref[0]
            outs.append(jnp.where(col < n_cmp, out, 0.0))
        ovt_ref[0] = jnp.concatenate(outs, axis=0).astype(BF16)


def _compress(xflat, pos, w1, b1, w2, b2, w2t, b2t, kc_g, n_cmp):
    b, _, _, nch, flat = xflat.shape
    return pl.pallas_call(
        functools.partial(_compress_body, n_cmp=n_cmp),
        grid=(b, 2),
        in_specs=[
            pl.BlockSpec((1, 1, NSA_GROUPS, nch, flat), lambda i, j: (i, j, 0, 0, 0)),
            pl.BlockSpec((1, 2, 1, flat), lambda i, j: (j, 0, 0, 0)),
            pl.BlockSpec((1, 2, flat, CMP_HIDDEN), lambda i, j: (j, 0, 0, 0)),
            pl.BlockSpec((1, 1, CMP_HIDDEN), lambda i, j: (j, 0, 0)),
            pl.BlockSpec((1, CMP_HIDDEN, NSA_DH), lambda i, j: (j, 0, 0)),
            pl.BlockSpec((1, 1, NSA_DH), lambda i, j: (j, 0, 0)),
            pl.BlockSpec((1, NSA_DH, CMP_HIDDEN), lambda i, j: (j, 0, 0)),
            pl.BlockSpec((1, NSA_DH, 1), lambda i, j: (j, 0, 0)),
            pl.BlockSpec((1, NSA_DH), lambda i, j: (0, 0)),
        ],
        out_specs=[pl.BlockSpec((1, nch, LANES), lambda i, j: (i, 0, 0)),
                   pl.BlockSpec((1, LANES, nch), lambda i, j: (i, 0, 0))],
        out_shape=[jax.ShapeDtypeStruct((b, nch, LANES), BF16), jax.ShapeDtypeStruct((b, LANES, nch), BF16)],
        compiler_params=_cparams(("parallel", "arbitrary"), 32),
        name="nsa_compress",
    )(xflat, pos, w1, b1, w2, b2, w2t, b2t, kc_g)


def _tile_heads(a):
    return jnp.concatenate([a] * NSA_HPG, axis=1)


def _nsa_body(qt_ref, gt_ref, kc_ref, vct_ref, ks_ref, kw_ref, vst_ref, vwt_ref, et_ref, ov_ref, o_ref, acc_sc, s_sc,
              *, n_cmp, n_top, kt):
    grp = pl.program_id(1)
    s0 = pl.program_id(2) * Q_BLOCK
    nch = kc_ref.shape[1]
    nblk = ov_ref.shape[0]

    qt = jnp.concatenate([qt_ref[h * LANES:(h + 1) * LANES, :] for h in range(NSA_HPG)], axis=1)
    t_row = s0 + lax.broadcasted_iota(jnp.int32, (1, Q_BLOCK), 1)

    cidx = lax.broadcasted_iota(jnp.int32, (nch, 1), 0)
    cvalid = (cidx * CMP_STRIDE + (CMP_BLOCK - 1) <= t_row) & (cidx < n_cmp)
    sc = _dot(kc_ref[0], qt) + _tile_heads(jnp.where(cvalid, 0.0, NEG))
    pc = jnp.exp2(sc - jnp.max(sc, axis=0, keepdims=True)).astype(BF16)
    stacked = jnp.concatenate([vct_ref[0], ov_ref[...], jnp.ones((8, nch), BF16)], axis=0)
    res = _dot(stacked, pc)
    inv = jnp.where(_tile_heads(t_row >= CMP_BLOCK - 1), 1.0 / jnp.maximum(res[LANES + nblk:LANES + nblk + 1], 1e-30),
                    0.0)
    oc = res[:LANES] * inv
    imp4 = res[LANES:LANES + nblk] * inv
    imp = (imp4[:, 0:Q_BLOCK] + imp4[:, Q_BLOCK:2 * Q_BLOCK] + imp4[:, 2 * Q_BLOCK:3 * Q_BLOCK]
           + imp4[:, 3 * Q_BLOCK:4 * Q_BLOCK])
    blk = lax.broadcasted_iota(jnp.int32, (nblk, Q_BLOCK), 0)
    cur = t_row // SLC_BLOCK
    imp = jnp.where(blk * SLC_BLOCK > t_row, NEG, imp)
    imp = jnp.where((blk == 0) | (blk == cur) | (blk == cur - 1), -NEG, imp)

    def pick_rounds(v, rounds):
        for _ in range(rounds):
            mx = jnp.max(v, axis=0, keepdims=True)
            first = jnp.min(jnp.where(v == mx, blk, nblk), axis=0, keepdims=True)
            v = jnp.where(blk == first, -jnp.inf, v)
        return v

    quarter = n_top // 4
    picked = pick_rounds(imp, quarter)

    wlen = WINDOW + Q_BLOCK
    w0 = pl.multiple_of(jnp.maximum(s0 - WINDOW, 0), Q_BLOCK)
    kpos = w0 + lax.broadcasted_iota(jnp.int32, (wlen, 1), 0)
    wbias = jnp.where((kpos <= t_row) & (kpos > t_row - WINDOW), 0.0, NEG)
    sw = _dot(kw_ref[0, pl.ds(w0, wlen), :], qt) + _tile_heads(wbias)
    picked = pick_rounds(picked, quarter)
    pw = jnp.exp2(sw - jnp.max(sw, axis=0, keepdims=True)).astype(BF16)
    picked = pick_rounds(picked, quarter)
    ow = _dot(vwt_ref[:, pl.ds(w0, wlen)], pw)
    ow = ow[:NSA_DH] / ow[NSA_DH:NSA_DH + 1]

    d0 = pl.multiple_of(s0, Q_BLOCK)
    dpos = s0 + lax.broadcasted_iota(jnp.int32, (Q_BLOCK, 1), 0)
    sd = _dot(ks_ref[0, pl.ds(d0, Q_BLOCK), :], qt) + _tile_heads(jnp.where(dpos <= t_row, 0.0, NEG))
    m_diag = jnp.max(sd, axis=0, keepdims=True)
    acc_sc[0] = _dot(vst_ref[:, pl.ds(d0, Q_BLOCK)], jnp.exp2(sd - m_diag).astype(BF16))
    acc_sc[1] = jnp.zeros(acc_sc.shape[1:], F32)
    picked = pick_rounds(picked, n_top - 3 * quarter)

    before = blk * SLC_BLOCK < s0
    selb = jnp.where((picked == -jnp.inf) & before, 0.0, MASKED).astype(BF16)
    rhs = jnp.concatenate([qt, _tile_heads(selb)], axis=0)

    last_tile = ks_ref.shape[1] // kt - 1

    def scores(idx, slot):
        k0 = pl.multiple_of(jnp.minimum(idx, last_tile) * kt, kt)
        lhs = jnp.concatenate([ks_ref[0, pl.ds(k0, kt), :], et_ref[pl.ds(k0, kt), :]], axis=1)
        s_sc[slot] = _dot(lhs, rhs)

    def update(idx, slot, m_old, acc_ref):
        k0 = pl.multiple_of(idx * kt, kt)
        m_new = jnp.maximum(m_old, jnp.max(s_sc[slot], axis=0, keepdims=True))
        p = jnp.exp2(s_sc[slot] - m_new).astype(BF16)
        acc_ref[...] = jnp.exp2(m_old - m_new) * acc_ref[...] + _dot(vst_ref[:, pl.ds(k0, kt)], p)
        return m_new

    def four_tiles(j, carry):
        m0, m1 = carry
        i = 4 * j
        scores(i + 2, 2)
        m0 = update(i, 0, m0, acc_sc.at[0])
        scores(i + 3, 3)
        m1 = update(i + 1, 1, m1, acc_sc.at[1])
        scores(i + 4, 0)
        m0 = update(i + 2, 2, m0, acc_sc.at[0])
        scores(i + 5, 1)
        m1 = update(i + 3, 3, m1, acc_sc.at[1])
        return m0, m1

    n_tiles = (s0 + kt - 1) // kt
    scores(0, 0)
    scores(1, 1)
    m0, m1 = lax.fori_loop(0, (n_tiles + 3) // 4, four_tiles,
                           (m_diag, jnp.full((1, NSA_HPG * Q_BLOCK), NEG, F32)))
    m_fin = jnp.maximum(m0, m1)
    acc = acc_sc[0] * jnp.exp2(m0 - m_fin) + acc_sc[1] * jnp.exp2(m1 - m_fin)
    osl = acc[:NSA_DH] / acc[NSA_DH:NSA_DH + 1]

    oc = jnp.where(grp == 0, oc[:NSA_DH], oc[NSA_DH:])
    gts = jax.nn.sigmoid(gt_ref[...])
    for h in range(NSA_HPG):
        cols = slice(h * Q_BLOCK, (h + 1) * Q_BLOCK)
        o_ref[h * NSA_DH:(h + 1) * NSA_DH, :] = (
            gts[3 * h:3 * h + 1, :] * oc[:, cols] + gts[3 * h + 1:3 * h + 2, :] * osl[:, cols]
            + gts[3 * h + 2:3 * h + 3, :] * ow[:, cols])


def _nsa_attention(qt, gt, kc, vct, okv, vt, expand_t, overlap, b, t, n_cmp, n_top, kt):
    nch = kc.shape[1]
    nq = t // Q_BLOCK
    n = b * t
    return pl.pallas_call(
        functools.partial(_nsa_body, n_cmp=n_cmp, n_top=n_top, kt=kt),
        grid=(b, NSA_GROUPS, nq),
        in_specs=[
            pl.BlockSpec((NSA_HPG * LANES, Q_BLOCK), lambda bi, g, i: (g, bi * nq + i)),
            pl.BlockSpec((GATE_ROWS, Q_BLOCK), lambda bi, g, i: (g, bi * nq + i)),
            pl.BlockSpec((1, nch, LANES), lambda bi, g, i: (bi, 0, 0)),
            pl.BlockSpec((1, LANES, nch), lambda bi, g, i: (bi, 0, 0)),
            pl.BlockSpec((1, t, LANES), lambda bi, g, i: (bi, 0, 2)),
            pl.BlockSpec((1, t, LANES), lambda bi, g, i: (bi, 0, 4)),
            pl.BlockSpec((LANES, t), lambda bi, g, i: (g, bi)),
            pl.BlockSpec((LANES, t), lambda bi, g, i: (NSA_GROUPS + g, bi)),
            pl.BlockSpec(expand_t.shape, lambda bi, g, i: (0, 0)),
            pl.BlockSpec(overlap.shape, lambda bi, g, i: (0, 0)),
        ],
        out_specs=pl.BlockSpec((NSA_HPG * NSA_DH, Q_BLOCK), lambda bi, g, i: (g, bi * nq + i)),
        out_shape=jax.ShapeDtypeStruct((NSA_W, n), F32),
        scratch_shapes=[pltpu.VMEM((2, LANES, NSA_HPG * Q_BLOCK), F32),
                        pltpu.VMEM((4, kt, NSA_HPG * Q_BLOCK), F32)],
        compiler_params=_cparams(("parallel", "parallel", "arbitrary"), 56),
        name="nsa_attention",
    )(qt, gt, kc, vct, okv, okv, vt, vt, expand_t, overlap)


def _split_bf16(a):
    hi = a.astype(BF16)
    return hi, (a - hi.astype(F32)).astype(BF16)


def _unit_lower_inverses(lmats):
    c = lmats[0].shape[0]
    r = lax.broadcasted_iota(jnp.int32, (c, c), 0)
    col = lax.broadcasted_iota(jnp.int32, (c, c), 1)
    eye = jnp.where(r == col, 1.0, 0.0)
    xs = [eye - l for l in lmats]
    ps = []
    for l in lmats:
        l16 = l.astype(BF16)
        ps.append(_dot(l16, l16))
    steps = int(np.log2(c)) - 1
    for s in range(steps):
        last = s + 1 == steps
        for i in range(len(lmats)):
            rhs = ps[i].astype(BF16)
            if last:
                xs[i] = xs[i] + _dot(xs[i].astype(BF16), rhs)
            else:
                both = _dot(jnp.concatenate([xs[i], ps[i]], axis=0).astype(BF16), rhs)
                xs[i] = xs[i] + both[:c]
                ps[i] = both[c:]
    return xs


def _gdn_body(x_ref, z_ref, ab_ref, alog_ref, dtb_ref, og_ref, o_ref, s_sc, *, ct):
    nb = x_ref.shape[0]

    @pl.when(pl.program_id(0) == 0)
    def _():
        s_sc[...] = jnp.zeros(s_sc.shape, F32)

    ch = GDN_CHUNK
    r = lax.broadcasted_iota(jnp.int32, (ch, ch), 0)
    col = lax.broadcasted_iota(jnp.int32, (ch, ch), 1)
    incl = r >= col
    strict = r > col
    tril16 = jnp.concatenate([jnp.where(incl, 1.0, 0.0).astype(BF16)] * 3, axis=1)

    units = []
    for ci in range(ct // ch):
        rows = slice(ci * ch, (ci + 1) * ch)
        for bi in range(nb):
            ab = ab_ref[bi, rows, :]
            g_all = -jnp.exp(alog_ref[...]) * jax.nn.softplus(ab + dtb_ref[...])
            beta_all = jax.nn.sigmoid(ab)
            g_hi, g_lo = _split_bf16(g_all)
            g_lo2 = (g_all - g_hi.astype(F32) - g_lo.astype(F32)).astype(BF16)
            gc_all = _dot(tril16, jnp.concatenate([g_hi, g_lo, g_lo2], axis=0))
            gc_t = gc_all.T
            for h in range(GDN_HEADS):
                hs = slice(h * GDN_DH, (h + 1) * GDN_DH)
                q16 = x_ref[bi, rows, hs]
                k16 = x_ref[bi, rows, GDN_W + h * GDN_DH:GDN_W + (h + 1) * GDN_DH]
                qh, kh = q16.astype(F32), k16.astype(F32)
                vh = x_ref[bi, rows, 2 * GDN_W + h * GDN_DH:2 * GDN_W + (h + 1) * GDN_DH].astype(F32)
                gc = gc_all[:, h:h + 1]
                gr = gc_t[h:h + 1, :]
                g_last = gc_all[ch - 1:ch, h:h + 1]
                beta = beta_all[:, GDN_HEADS + h:GDN_HEADS + h + 1]
                eg = jnp.exp(gc)
                decay = jnp.where(incl, jnp.exp(jnp.minimum(gc - gr, 0.0)), 0.0)
                kb = kh * beta
                with_k = (_dot_nt(jnp.concatenate([kb.astype(BF16), q16], axis=0), k16)
                          * jnp.concatenate([decay, decay], axis=0))
                units.append(dict(
                    rows=rows, bi=bi, h=h,
                    lmat=jnp.where(strict, with_k[:ch], 0.0),
                    vb_kbg=jnp.concatenate([(vh * beta).astype(BF16), (kb * eg).astype(BF16)], axis=1),
                    qk=jnp.where(incl, with_k[ch:], 0.0).astype(BF16),
                    qg=(qh * eg).astype(BF16), kd_t=(kh * jnp.exp(g_last - gc)).T.astype(BF16),
                    gl=jnp.exp(g_last)))
    tinvs = _unit_lower_inverses([u["lmat"] for u in units])
    for u, tinv in zip(units, tinvs):
        u_w = _dot(tinv.astype(BF16), u["vb_kbg"])
        u["u"] = u_w[:, :GDN_DH]
        u["w_qg"] = jnp.concatenate([u_w[:, GDN_DH:].astype(BF16), u["qg"]], axis=0)
        u["kd_qk"] = jnp.concatenate([u["kd_t"], u["qk"]], axis=0)

    for u in units:
        bi, h, rows = u["bi"], u["h"], u["rows"]
        hs = slice(h * GDN_DH, (h + 1) * GDN_DH)
        s_old = s_sc[bi * GDN_HEADS + h]
        from_state = _dot(u["w_qg"], s_old.astype(BF16))
        v_new = (u["u"] - from_state[:ch]).astype(BF16)
        from_v = _dot(u["kd_qk"], v_new)
        s_sc[bi * GDN_HEADS + h] = s_old * u["gl"] + from_v[:GDN_DH]
        o = from_state[ch:] + from_v[GDN_DH:]
        on = o * lax.rsqrt(jnp.mean(o * o, axis=-1, keepdims=True) + EPS) * og_ref[...]
        zh = z_ref[bi, rows, hs].astype(F32)
        o_ref[bi, rows, hs] = (on * (zh * jax.nn.sigmoid(zh))).astype(BF16)


def _gdn(oqkv, oz, oab, alog, dtb, og, ct=128):
    b, t, w3 = oqkv.shape
    full = lambda a: pl.BlockSpec(a.shape, lambda c: (0,) * a.ndim)
    return pl.pallas_call(
        functools.partial(_gdn_body, ct=ct),
        grid=(t // ct,),
        in_specs=[
            pl.BlockSpec((b, ct, w3), lambda c: (0, c, 0)),
            pl.BlockSpec((b, ct, GDN_W), lambda c: (0, c, 0)),
            pl.BlockSpec((b, ct, LANES), lambda c: (0, c, 0)),
            full(alog), full(dtb), full(og),
        ],
        out_specs=pl.BlockSpec((b, ct, GDN_W), lambda c: (0, c, 0)),
        out_shape=jax.ShapeDtypeStruct((b, t, GDN_W), BF16),
        scratch_shapes=[pltpu.VMEM((b * GDN_HEADS, GDN_DH, GDN_DH), F32)],
        compiler_params=_cparams(("arbitrary",), 32),
        name="gdn",
    )(oqkv, oz, oab, alog, dtb, og)


def _outproj_body(ont_ref, og_ref, x_ref, ng_ref, wo_ref, fg_ref, wr_ref, br_ref,
                  x1_ref, h2_ref, gate_ref, route_ref, cnt_ref, cnt_sc):
    i = pl.program_id(0)
    tm = x_ref.shape[0]

    @pl.when(i == 0)
    def _():
        cnt_sc[...] = jnp.zeros(cnt_sc.shape, F32)

    a = ont_ref[...]
    a = (a * lax.rsqrt(jnp.mean(a * a, axis=0, keepdims=True) + EPS) * ng_ref[...]).astype(BF16)
    x1 = x_ref[...] + _dot_tn(a, wo_ref[0:NSA_W, :]) + _dot(og_ref[...], wo_ref[NSA_W:, :])
    x1_ref[...] = x1
    h2f = x1 * lax.rsqrt(jnp.mean(x1 * x1, axis=-1, keepdims=True) + EPS) * fg_ref[...]
    _store_pieces(h2_ref, h2f)
    h2 = h2f.astype(BF16)

    lane = lax.broadcasted_iota(jnp.int32, (tm, LANES), 1)
    logits = jnp.where(lane < N_EXPERTS, _dot(h2, wr_ref[...]) + br_ref[...], -jnp.inf)
    onehot = jnp.zeros((tm, LANES), F32)
    idx = jnp.zeros((tm, LANES), jnp.int32)
    vals = jnp.zeros((tm, LANES), F32)
    firsts = []
    v = logits
    top0 = None
    for k in range(TOP_K):
        mx = jnp.max(v, axis=-1, keepdims=True)
        first = jnp.min(jnp.where(v == mx, lane, LANES), axis=-1, keepdims=True)
        hit = lane == first
        v = jnp.where(hit, -jnp.inf, v)
        onehot = jnp.where(hit, 1.0, onehot)
        idx = jnp.where(lane == k, first, idx)
        top0 = mx if k == 0 else top0
        vals = jnp.where(lane == k, jnp.exp(mx - top0), vals)
        firsts.append(first)
    gate_ref[...] = vals / jnp.sum(vals, axis=-1, keepdims=True)

    r = lax.broadcasted_iota(jnp.int32, (tm, tm), 0)
    c = lax.broadcasted_iota(jnp.int32, (tm, tm), 1)
    before = jnp.where(r > c, 1.0, 0.0).astype(BF16)
    excl = cnt_sc[...] + _dot(before, onehot.astype(BF16))
    rank = jnp.zeros((tm, LANES), F32)
    for k in range(TOP_K):
        rk = jnp.sum(jnp.where(lane == firsts[k], excl, 0.0), axis=-1, keepdims=True)
        rank = jnp.where(lane == TOP_K + k, rk, rank)
    route = (idx.astype(F32) + rank).T
    route_ref[...] = route[:2 * TOP_K].astype(jnp.int32)
    cnt_sc[...] = cnt_sc[...] + jnp.sum(onehot, axis=0, keepdims=True)
    cnt_ref[...] = cnt_sc[...].astype(jnp.int32)


def _out_proj(o_nsa_t, o_gdn, x2, ng, wo, fg, wr, br, tm=512):
    n, d = x2.shape
    full = lambda a: pl.BlockSpec(a.shape, lambda i: (0,) * a.ndim)
    row = lambda w: pl.BlockSpec((tm, w), lambda i: (i, 0))
    return pl.pallas_call(
        _outproj_body,
        grid=(n // tm,),
        in_specs=[pl.BlockSpec((NSA_W, tm), lambda i: (0, i)), row(GDN_W), row(d), full(ng), full(wo), full(fg),
                  full(wr), full(br)],
        out_specs=[row(d), pl.BlockSpec((d // 2 // SC_SUBROW, tm, SC_SUBROW), lambda i: (0, i, 0)),
                   row(LANES), pl.BlockSpec((2 * TOP_K, tm), lambda i: (0, i)),
                   pl.BlockSpec((1, LANES), lambda i: (0, 0))],
        out_shape=[jax.ShapeDtypeStruct((n, d), F32),
                   jax.ShapeDtypeStruct((d // 2 // SC_SUBROW, n, SC_SUBROW), jnp.int32),
                   jax.ShapeDtypeStruct((n, LANES), F32), jax.ShapeDtypeStruct((2 * TOP_K, n), jnp.int32),
                   jax.ShapeDtypeStruct((1, LANES), jnp.int32)],
        scratch_shapes=[pltpu.VMEM((1, LANES), F32)],
        compiler_params=_cparams(("arbitrary",), 48),
        name="out_proj_router",
    )(o_nsa_t, o_gdn, x2, ng, wo, fg, wr, br)


def _expert_body(be_ref, fresh_ref, slot_ref, next_ref, xs_ref, wg_hbm, bg_ref, wu_hbm, bu_ref, wd_hbm, bd_ref,
                 y_ref, wf32, w16, sems):
    i = pl.program_id(0)
    used = i < be_ref[pl.num_programs(0)]
    hbm = (wg_hbm, wu_hbm, wd_hbm)

    def weight_copy(expert, slot, j):
        return pltpu.make_async_copy(hbm[j].at[expert], wf32.at[slot, j], sems.at[slot, j])

    @pl.when((i == 0) & used)
    def _():
        for j in range(3):
            weight_copy(be_ref[0], 0, j).start()

    @pl.when(used & (fresh_ref[i] == 1))
    def _():
        slot = slot_ref[i]
        for j in range(3):
            weight_copy(be_ref[i], slot, j).wait()
            w16[j] = wf32[slot, j].astype(BF16)

        @pl.when(next_ref[i] >= 0)
        def _():
            for j in range(3):
                weight_copy(next_ref[i], 1 - slot, j).start()

    @pl.when(jnp.logical_not(used))
    def _():
        y_ref[...] = jnp.zeros(y_ref.shape, y_ref.dtype)

    @pl.when(used)
    def _():
        x = _join_pieces(xs_ref).astype(BF16)
        gate = jnp.minimum(_dot(x, w16[0]) + bg_ref[0], SWIGLU_LIMIT)
        up = jnp.clip(_dot(x, w16[1]) + bu_ref[0], -SWIGLU_LIMIT, SWIGLU_LIMIT)
        glu = gate * jax.nn.sigmoid(gate * SWIGLU_ALPHA)
        _store_pieces(y_ref, _dot(((up + 1.0) * glu).astype(BF16), w16[2]) + bd_ref[0])


def _experts(blk_e, fresh, slot, nxt, xs, wg, bg, wu, bu, wd, bd):
    pieces, n_rows, sub = xs.shape
    d, de = wg.shape[1], wg.shape[2]
    assert d == de
    r = MOE_ROW_BLOCK
    bspec = lambda w: pl.BlockSpec((1, 1, w), lambda i, be, *_: (be[i], 0, 0))
    hbm = pl.BlockSpec(memory_space=pl.ANY)
    grid_spec = pltpu.PrefetchScalarGridSpec(
        num_scalar_prefetch=4,
        grid=(n_rows // r,),
        in_specs=[pl.BlockSpec((pieces, r, sub), lambda i, *_: (0, i, 0)),
                  hbm, bspec(de), hbm, bspec(de), hbm, bspec(d)],
        out_specs=pl.BlockSpec((pieces, r, sub), lambda i, *_: (0, i, 0)),
        scratch_shapes=[pltpu.VMEM((2, 3, d, de), F32), pltpu.VMEM((3, d, de), BF16),
                        pltpu.SemaphoreType.DMA((2, 3))],
    )
    return pl.pallas_call(
        _expert_body,
        grid_spec=grid_spec,
        out_shape=jax.ShapeDtypeStruct((pieces, n_rows, sub), jnp.int32),
        compiler_params=_cparams(("arbitrary",), 56),
        name="moe_experts",
    )(blk_e, fresh, slot, nxt, xs, wg, bg, wu, bu, wd, bd)


SC_WINDOW = 128
SC_SUBROW = 256


def _sc_mesh():
    return plsc.VectorSubcoreMesh(core_axis_name="c", subcore_axis_name="s")


def _sc_dispatch(h2, dest_rows, n_rows):
    n, d = h2.shape

    @functools.partial(pl.kernel, out_type=jax.ShapeDtypeStruct((n_rows, d), h2.dtype), mesh=_sc_mesh())
    def dispatch(x_hbm, *refs):
        idx_hbm, o_hbm = refs[:TOP_K], refs[TOP_K]

        def body(x_vmem, *idx_vmem):
            for iv in idx_vmem:
                pltpu.sync_copy(x_vmem, o_hbm.at[iv.at[0]])

        pltpu.emit_pipeline(
            body,
            grid=(n // SC_WINDOW,),
            in_specs=[pl.BlockSpec((SC_WINDOW, d), lambda i: (i, 0))]
                     + [pl.BlockSpec((1, SC_WINDOW), lambda i: (0, i))] * TOP_K,
            out_specs=[],
            core_axis_name=("c", "s"),
            dimension_semantics=(pltpu.PARALLEL,),
        )(x_hbm, *idx_hbm)

    return dispatch(h2, *dest_rows)


def _sc_gather(table, idx):
    _, d = table.shape
    m = idx.shape[1]

    @functools.partial(pl.kernel, out_type=jax.ShapeDtypeStruct((m, d), table.dtype), mesh=_sc_mesh())
    def gather(t_hbm, i_hbm, o_hbm):
        def body(i_vmem, o_vmem):
            pltpu.sync_copy(t_hbm.at[i_vmem.at[0]], o_vmem)

        pltpu.emit_pipeline(
            body,
            grid=(m // SC_WINDOW,),
            in_specs=[pl.BlockSpec((1, SC_WINDOW), lambda i: (0, i))],
            out_specs=[pl.BlockSpec((SC_WINDOW, d), lambda i: (i, 0))],
            core_axis_name=("c", "s"),
            dimension_semantics=(pltpu.PARALLEL,),
        )(i_hbm, o_hbm)

    return gather(table, idx)


def _combine_body(x1_ref, y_ref, gate_ref, o_ref):
    acc = x1_ref[...]
    for k in range(TOP_K):
        acc = acc + gate_ref[:, k:k + 1] * _join_pieces(y_ref.at[k])
    o_ref[...] = acc


def _combine(x1, y4, gates, tm=512):
    n, d = x1.shape
    pieces, sub = y4.shape[1], y4.shape[3]
    row = lambda w: pl.BlockSpec((tm, w), lambda i: (i, 0))
    return pl.pallas_call(
        _combine_body,
        grid=(n // tm,),
        in_specs=[row(d), pl.BlockSpec((TOP_K, pieces, tm, sub), lambda i: (0, 0, i, 0)), row(LANES)],
        out_specs=row(d),
        out_shape=jax.ShapeDtypeStruct((n, d), F32),
        compiler_params=_cparams(("parallel",), 48),
        name="moe_combine",
    )(x1, y4, gates)


def _pad_lanes(a, width=LANES):
    return jnp.pad(a, ((0, 0), (0, width - a.shape[1])))


def _layer(x, attn_norm_g, w_in, q_g, kc_g, ks_g, kw_g, ck_pos, ck_w1, ck_b1, ck_w2, ck_b2,
           cv_pos, cv_w1, cv_b1, cv_w2, cv_b2, nsa_out_g, conv_w, a_log, dt_bias, gdn_out_g, w_out,
           ffn_g, router_w, router_b, e_wg, e_bg, e_wu, e_bu, e_wd, e_bd):
    b, t, d = x.shape
    n = b * t
    x2 = x.reshape(n, d)

    o = np.cumsum([0, NSA_W] + [NSA_GROUPS * NSA_DH] * 6 + [3 * NSA_HEADS, 3 * GDN_W, GDN_W, GDN_HEADS, GDN_HEADS])
    wq_t = w_in[:, o[0]:o[1]].T.reshape(NSA_GROUPS, NSA_HPG, NSA_DH, d)
    zq = jnp.zeros((NSA_HPG, NSA_DH, d), F32)
    wq_t = jnp.stack([jnp.concatenate([wq_t[0], zq], axis=1), jnp.concatenate([zq, wq_t[1]], axis=1)])
    wq_t = wq_t.reshape(NSA_HEADS * LANES, d).astype(BF16)
    qg1 = q_g * (NSA_DH ** -0.5 * np.log2(np.e))
    zg = jnp.zeros((NSA_DH,), F32)
    qg_col = jnp.concatenate([jnp.tile(jnp.concatenate([qg1, zg]), NSA_HPG),
                              jnp.tile(jnp.concatenate([zg, qg1]), NSA_HPG)]).reshape(NSA_HEADS * LANES, 1)
    wkv = w_in[:, o[1]:o[7]].astype(BF16)
    ones = jnp.ones((LANES,), F32)
    kg = jnp.concatenate([ones, ones, ks_g, ks_g, ones, kw_g, kw_g, ones]).reshape(1, 6 * LANES)
    wv_t = jnp.concatenate([w_in[:, o[4]:o[5]], w_in[:, o[6]:o[7]]], axis=1).T.reshape(2 * NSA_GROUPS, NSA_DH, d)
    wv_t = jnp.pad(wv_t, ((0, 0), (0, LANES - NSA_DH), (0, 0))).reshape(2 * NSA_GROUPS * LANES, d).astype(BF16)
    vone = jnp.asarray((np.arange(2 * NSA_GROUPS * LANES) % LANES == NSA_DH).astype(np.float32)[:, None])
    wg_t = w_in[:, o[7]:o[8]].T.reshape(NSA_GROUPS, NSA_HPG * 3, d)
    wg_t = jnp.pad(wg_t, ((0, 0), (0, GATE_ROWS - NSA_HPG * 3), (0, 0))).reshape(NSA_GROUPS * GATE_ROWS, d)
    wg_t = wg_t.astype(BF16)
    wab = _pad_lanes(w_in[:, o[10]:o[12]]).astype(BF16)
    wqkv = w_in[:, o[8]:o[9]].astype(BF16)
    wz = w_in[:, o[9]:o[10]].astype(BF16)

    tm = min(512, t)
    oqt, okv, ovt, ogt, oqkv, oz, oab = _in_proj(x2, attn_norm_g.reshape(1, d), wq_t, wkv, wv_t, wg_t, wqkv, wz,
                                                 wab, qg_col, kg, vone, conv_w, t // tm, tm)

    nch = t // CMP_STRIDE
    n_cmp = (t - CMP_BLOCK) // CMP_STRIDE + 1
    half = CMP_STRIDE * NSA_DH
    xflat = okv[:, :2 * LANES].reshape(b, nch, CMP_STRIDE, 2, NSA_GROUPS, NSA_DH)
    xflat = xflat.transpose(0, 3, 4, 1, 2, 5).reshape(b, 2, NSA_GROUPS, nch, half)
    pos = jnp.stack([ck_pos, cv_pos]).reshape(2, 2, 1, half)
    w1 = jnp.stack([ck_w1, cv_w1]).reshape(2, 2, half, CMP_HIDDEN).astype(BF16)
    b1 = jnp.stack([ck_b1, cv_b1]).reshape(2, 1, CMP_HIDDEN)
    w2 = jnp.stack([ck_w2, cv_w2]).astype(BF16)
    b2 = jnp.stack([ck_b2, cv_b2]).reshape(2, 1, NSA_DH)
    w2t = jnp.stack([ck_w2.T, cv_w2.T]).astype(BF16)
    b2t = jnp.stack([ck_b2, cv_b2]).reshape(2, NSA_DH, 1)
    kc, vct = _compress(xflat, pos, w1, b1, w2, b2, w2t, b2t, kc_g.reshape(1, NSA_DH), n_cmp)

    n_slc = t // SLC_BLOCK
    n_top = min(SLC_TOPK, n_slc)
    nblk = max(n_slc, LANES)
    kt = min(256, t // 4)
    assert (t // kt) % 4 == 0
    ci = np.arange(nch)[None, :] * CMP_STRIDE
    sj = np.arange(nblk)[:, None] * SLC_BLOCK
    overlap = ((ci < sj + SLC_BLOCK) & (ci + CMP_BLOCK > sj) & (np.arange(nch)[None, :] < n_cmp)
               & (np.arange(nblk)[:, None] < n_slc))
    expand_t = (np.arange(t)[:, None] // SLC_BLOCK) == np.arange(nblk)[None, :]
    o_nsa_t = _nsa_attention(oqt, ogt, kc, vct, okv.reshape(b, t, -1), ovt, jnp.asarray(expand_t, BF16),
                             jnp.asarray(overlap, BF16), b, t, n_cmp, n_top, kt)

    alog_row = _pad_lanes(a_log.reshape(1, GDN_HEADS))
    dtb_row = _pad_lanes(dt_bias.reshape(1, GDN_HEADS))
    o_gdn = _gdn(oqkv.reshape(b, t, -1), oz.reshape(b, t, -1), oab.reshape(b, t, -1),
                 alog_row, dtb_row, gdn_out_g.reshape(1, GDN_DH))

    wr = _pad_lanes(router_w).astype(BF16)
    br = _pad_lanes(router_b.reshape(1, N_EXPERTS))
    x1, h2, gates, route, counts = _out_proj(
        o_nsa_t, o_gdn.reshape(n, GDN_W), x2, nsa_out_g.reshape(NSA_W, 1),
        w_out.astype(BF16), ffn_g.reshape(1, d), wr, br)

    r = MOE_ROW_BLOCK
    nk = n * TOP_K
    counts = counts[0, :N_EXPERTS]
    pcounts = (counts + r - 1) // r * r
    pends = jnp.cumsum(pcounts)
    pstarts = pends - pcounts
    dest_t = pstarts[route[:TOP_K]] + route[TOP_K:]
    n_rows = (nk + r - 1) // r * r + N_EXPERTS * r
    n_blocks = n_rows // r
    blk_start = jnp.arange(n_blocks, dtype=jnp.int32)[:, None] * r
    blk_e = jnp.minimum(jnp.sum(pends[None, :] <= blk_start, axis=1), N_EXPERTS - 1).astype(jnp.int32)
    n_used = (pends[-1] // r).astype(jnp.int32)
    fresh = (jnp.arange(n_blocks) < n_used) & (blk_e != jnp.concatenate([jnp.full((1,), -1, jnp.int32), blk_e[:-1]]))
    slot = ((jnp.cumsum(fresh) - 1) % 2).astype(jnp.int32)
    eid = jnp.arange(N_EXPERTS, dtype=jnp.int32)
    later = jnp.where((eid[None, :] > eid[:, None]) & (pcounts[None, :] > 0), eid[None, :], N_EXPERTS)
    next_expert = jnp.min(later, axis=1)
    next_expert = jnp.where(next_expert < N_EXPERTS, next_expert, -1).astype(jnp.int32)
    nxt = next_expert[blk_e]
    blk_e = jnp.concatenate([blk_e, n_used[None]])
    pieces = d // 2 // SC_SUBROW
    dest_p = (dest_t.astype(jnp.int32)[:, None, :]
              + (jnp.arange(pieces, dtype=jnp.int32) * n_rows)[None, :, None])
    xs = _sc_dispatch(h2.reshape(pieces * n, SC_SUBROW), [dest_p[k].reshape(1, pieces * n) for k in range(TOP_K)],
                      pieces * n_rows)
    ys = _experts(blk_e, fresh.astype(jnp.int32), slot, nxt, xs.reshape(pieces, n_rows, SC_SUBROW), e_wg, e_bg.reshape(N_EXPERTS, 1, -1), e_wu,
                  e_bu.reshape(N_EXPERTS, 1, -1), e_wd, e_bd.reshape(N_EXPERTS, 1, -1))
    y4 = _sc_gather(ys.reshape(pieces * n_rows, SC_SUBROW), dest_p.reshape(1, nk * pieces))
    return _combine(x1, y4.reshape(TOP_K, pieces, n, SC_SUBROW), gates).reshape(b, t, d)


def kernel(x, attn_norm_g, w_in, nsa_q_norm_g, nsa_kc_norm_g, nsa_ks_norm_g, nsa_kw_norm_g, cmp_k_pos, cmp_k_w1, cmp_k_b1, cmp_k_w2, cmp_k_b2, cmp_v_pos, cmp_v_w1, cmp_v_b1, cmp_v_w2, cmp_v_b2, nsa_out_norm_g, gdn_conv_w, gdn_a_log, gdn_dt_bias, gdn_out_norm_g, w_out, ffn_norm_g, router_w, router_b, exp_w_gate, exp_b_gate, exp_w_up, exp_b_up, exp_w_down, exp_b_down):
    params = (attn_norm_g, w_in, nsa_q_norm_g, nsa_kc_norm_g, nsa_ks_norm_g, nsa_kw_norm_g,
              cmp_k_pos, cmp_k_w1, cmp_k_b1, cmp_k_w2, cmp_k_b2, cmp_v_pos, cmp_v_w1, cmp_v_b1, cmp_v_w2, cmp_v_b2,
              nsa_out_norm_g, gdn_conv_w, gdn_a_log, gdn_dt_bias, gdn_out_norm_g, w_out, ffn_norm_g,
              router_w, router_b, exp_w_gate, exp_b_gate, exp_w_up, exp_b_up, exp_w_down, exp_b_down)
    for l in range(attn_norm_g.shape[0]):
        x = _layer(x, *(p[l] for p in params))
    return x
```

```python
import functools

import jax
import jax.numpy as jnp
import numpy as np
from jax import lax
from jax.experimental import pallas as pl
from jax.experimental.pallas import tpu as pltpu
from jax.experimental.pallas import tpu_sc as plsc

F32 = jnp.float32
BF16 = jnp.bfloat16

EPS = 1e-6
NEG = -1e30
MASKED = -2.0 ** 100

NSA_HEADS = 8
NSA_GROUPS = 2
NSA_HPG = 4
NSA_DH = 64
CMP_BLOCK = 32
CMP_STRIDE = 16
CMP_HIDDEN = 256
SLC_BLOCK = 64
SLC_TOPK = 16
WINDOW = 512
Q_BLOCK = 128
GDN_HEADS = 4
GDN_DH = 128
GDN_CONV = 4
GDN_CHUNK = 64
N_EXPERTS = 32
TOP_K = 4
SWIGLU_LIMIT = 7.0
SWIGLU_ALPHA = 1.702
MOE_ROW_BLOCK = 256

LANES = 128
GATE_ROWS = 16
NSA_W = NSA_HEADS * NSA_DH
GDN_W = GDN_HEADS * GDN_DH

_NT = (((1,), (1,)), ((), ()))
_TN = (((0,), (0,)), ((), ()))


def _cparams(sem, vmem_mb):
    return pltpu.CompilerParams(dimension_semantics=sem, vmem_limit_bytes=vmem_mb * 1024 * 1024)


def _dot(a, b):
    return jnp.dot(a, b, preferred_element_type=F32)


def _dot_nt(a, b):
    return lax.dot_general(a, b, _NT, preferred_element_type=F32)


def _dot_tn(a, b):
    return lax.dot_general(a, b, _TN, preferred_element_type=F32)


def _store_pieces(ref, val):
    half = val.shape[1] // 2
    hi = lax.bitcast_convert_type(val[:, :half].astype(BF16).astype(F32), jnp.uint32)
    lo = lax.bitcast_convert_type(val[:, half:].astype(BF16).astype(F32), jnp.uint32)
    words = lax.bitcast_convert_type(hi | (lo >> 16), jnp.int32)
    sub = ref.shape[2]
    for j in range(ref.shape[0]):
        ref[j] = words[:, j * sub:(j + 1) * sub]


def _join_pieces(ref):
    words = jnp.concatenate([ref[j] for j in range(ref.shape[0])], axis=1)
    words = lax.bitcast_convert_type(words, jnp.uint32)
    hi = lax.bitcast_convert_type(words & jnp.uint32(0xFFFF0000), F32)
    lo = lax.bitcast_convert_type(words << 16, F32)
    return jnp.concatenate([hi, lo], axis=1)


def _inproj_body(x_ref, g_ref, wqt_ref, wkv_ref, wvt_ref, wgt_ref, wqkv_ref, wz_ref, wab_ref, qg_ref, kg_ref,
                 vone_ref, cw_ref, oqt_ref, okv_ref, ovt_ref, ogt_ref, oqkv_ref, oz_ref, oab_ref, ybuf,
                 *, tiles_per_seq):
    x = x_ref[...]
    h = (x * lax.rsqrt(jnp.mean(x * x, axis=-1, keepdims=True) + EPS) * g_ref[...]).astype(BF16)
    tm = x.shape[0]

    yq = _dot_nt(wqt_ref[...], h)
    for s in range(NSA_HEADS):
        sl = slice(s * LANES, (s + 1) * LANES)
        ys = yq[sl, :]
        ms = jnp.sum(ys * ys, axis=0, keepdims=True) * (1.0 / NSA_DH)
        oqt_ref[sl, :] = (ys * lax.rsqrt(ms + EPS) * qg_ref[sl, :]).astype(BF16)

    ykv = _dot(h, wkv_ref[...])
    lane = lax.broadcasted_iota(jnp.int32, (tm, LANES), 1)
    low = lane < NSA_DH
    for s in range(6):
        sl = slice(s * LANES, (s + 1) * LANES)
        ys = ykv[:, sl]
        if s in (2, 4):
            y2 = ys * ys
            s0 = jnp.sum(jnp.where(low, y2, 0.0), axis=-1, keepdims=True)
            s1 = jnp.sum(jnp.where(low, 0.0, y2), axis=-1, keepdims=True)
            ms = jnp.where(low, s0, s1) * (1.0 / NSA_DH)
            ys = ys * lax.rsqrt(ms + EPS) * kg_ref[:, sl]
        okv_ref[:, sl] = ys.astype(BF16)

    ovt_ref[...] = (_dot_nt(wvt_ref[...], h) + vone_ref[...]).astype(BF16)
    ogt_ref[...] = _dot_nt(wgt_ref[...], h)
    oz_ref[...] = _dot(h, wz_ref[...]).astype(BF16)
    oab_ref[...] = _dot(h, wab_ref[...])

    halo = ybuf.shape[0] - tm
    first = pl.program_id(0) % tiles_per_seq == 0

    @pl.when(first)
    def _():
        ybuf[0:halo, :] = jnp.zeros((halo, ybuf.shape[1]), F32)

    @pl.when(jnp.logical_not(first))
    def _():
        ybuf[0:halo, :] = ybuf[tm:tm + halo, :]

    ybuf[halo:halo + tm, :] = _dot(h, wqkv_ref[...])
    taps = cw_ref.shape[0]
    y = cw_ref[0:1, :] * ybuf[pl.ds(halo - taps + 1, tm), :]
    for k in range(1, taps):
        y = y + cw_ref[k:k + 1, :] * ybuf[pl.ds(halo - taps + 1 + k, tm), :]
    hy = 0.5 * y
    y = hy + hy * jnp.tanh(hy)
    for s in range(3 * GDN_HEADS):
        sl = slice(s * GDN_DH, (s + 1) * GDN_DH)
        ys = y[:, sl]
        if s < 2 * GDN_HEADS:
            scale = GDN_DH ** -0.5 if s < GDN_HEADS else 1.0
            ys = ys * (lax.rsqrt(jnp.sum(ys * ys, axis=-1, keepdims=True) + EPS) * scale)
        oqkv_ref[:, sl] = ys.astype(BF16)


def _in_proj(x2, g, wqt, wkv, wvt, wgt, wqkv, wz, wab, qg, kg, vone, conv_w, tiles_per_seq, tm):
    n, d = x2.shape
    full = lambda a: pl.BlockSpec(a.shape, lambda i: (0,) * a.ndim)
    row = lambda w: pl.BlockSpec((tm, w), lambda i: (i, 0))
    colb = lambda r: pl.BlockSpec((r, tm), lambda i: (0, i))
    return pl.pallas_call(
        functools.partial(_inproj_body, tiles_per_seq=tiles_per_seq),
        grid=(n // tm,),
        in_specs=[row(d)] + [full(a) for a in (g, wqt, wkv, wvt, wgt, wqkv, wz, wab, qg, kg, vone, conv_w)],
        out_specs=[colb(wqt.shape[0]), row(wkv.shape[1]), colb(wvt.shape[0]), colb(wgt.shape[0]),
                   row(wqkv.shape[1]), row(wz.shape[1]), row(wab.shape[1])],
        out_shape=[jax.ShapeDtypeStruct((wqt.shape[0], n), BF16), jax.ShapeDtypeStruct((n, wkv.shape[1]), BF16),
                   jax.ShapeDtypeStruct((wvt.shape[0], n), BF16), jax.ShapeDtypeStruct((wgt.shape[0], n), F32),
                   jax.ShapeDtypeStruct((n, wqkv.shape[1]), BF16), jax.ShapeDtypeStruct((n, wz.shape[1]), BF16),
                   jax.ShapeDtypeStruct((n, wab.shape[1]), F32)],
        scratch_shapes=[pltpu.VMEM((tm + 8, wqkv.shape[1]), F32)],
        compiler_params=_cparams(("arbitrary",), 56),
        name="in_proj",
    )(x2, g, wqt, wkv, wvt, wgt, wqkv, wz, wab, qg, kg, vone, conv_w)


def _compress_body(x_ref, pos_ref, w1_ref, b1_ref, w2_ref, b2_ref, w2t_ref, b2t_ref, g_ref, ok_ref, ovt_ref,
                   *, n_cmp):
    is_key = pl.program_id(1) == 0
    nch = x_ref.shape[3]
    hids = []
    for grp in range(NSA_GROUPS):
        x = x_ref[0, 0, grp].astype(F32)
        xa = (x + pos_ref[0, 0]).astype(BF16)
        xb = (x + pos_ref[0, 1]).astype(BF16)
        a = _dot(xa, w1_ref[0, 0])
        b = _dot(xb, w1_ref[0, 1])
        b_next = pltpu.roll(b, nch - 1, 0)
        hids.append(jax.nn.gelu(a + b_next + b1_ref[0]).astype(BF16))

    @pl.when(is_key)
    def _():
        row = lax.broadcasted_iota(jnp.int32, (nch, NSA_DH), 0)
        outs = []
        for grp in range(NSA_GROUPS):
            out = _dot(hids[grp], w2_ref[0]) + b2_ref[0]
            out = out * lax.rsqrt(jnp.mean(out * out, axis=-1, keepdims=True) + EPS) * g_ref[...]
            outs.append(jnp.where(row < n_cmp, out, 0.0))
        ok_ref[0] = jnp.concatenate(outs, axis=-1).astype(BF16)

    @pl.when(jnp.logical_not(is_key))
    def _():
        col = lax.broadcasted_iota(jnp.int32, (NSA_DH, nch), 1)
        outs = []
        for grp in range(NSA_GROUPS):
            out = _dot_nt(w2t_ref[0], hids[grp]) + b2t_ref[0]
            outs.append(jnp.where(col < n_cmp, out, 0.0))
        ovt_ref[0] = jnp.concatenate(outs, axis=0).astype(BF16)


def _compress(xflat, pos, w1, b1, w2, b2, w2t, b2t, kc_g, n_cmp):
    b, _, _, nch, flat = xflat.shape
    return pl.pallas_call(
        functools.partial(_compress_body, n_cmp=n_cmp),
        grid=(b, 2),
        in_specs=[
            pl.BlockSpec((1, 1, NSA_GROUPS, nch, flat), lambda i, j: (i, j, 0, 0, 0)),
            pl.BlockSpec((1, 2, 1, flat), lambda i, j: (j, 0, 0, 0)),
            pl.BlockSpec((1, 2, flat, CMP_HIDDEN), lambda i, j: (j, 0, 0, 0)),
            pl.BlockSpec((1, 1, CMP_HIDDEN), lambda i, j: (j, 0, 0)),
            pl.BlockSpec((1, CMP_HIDDEN, NSA_DH), lambda i, j: (j, 0, 0)),
            pl.BlockSpec((1, 1, NSA_DH), lambda i, j: (j, 0, 0)),
            pl.BlockSpec((1, NSA_DH, CMP_HIDDEN), lambda i, j: (j, 0, 0)),
            pl.BlockSpec((1, NSA_DH, 1), lambda i, j: (j, 0, 0)),
            pl.BlockSpec((1, NSA_DH), lambda i, j: (0, 0)),
        ],
        out_specs=[pl.BlockSpec((1, nch, LANES), lambda i, j: (i, 0, 0)),
                   pl.BlockSpec((1, LANES, nch), lambda i, j: (i, 0, 0))],
        out_shape=[jax.ShapeDtypeStruct((b, nch, LANES), BF16), jax.ShapeDtypeStruct((b, LANES, nch), BF16)],
        compiler_params=_cparams(("parallel", "arbitrary"), 32),
        name="nsa_compress",
    )(xflat, pos, w1, b1, w2, b2, w2t, b2t, kc_g)


def _tile_heads(a):
    return jnp.concatenate([a] * NSA_HPG, axis=1)


def _nsa_body(qt_ref, gt_ref, kc_ref, vct_ref, ks_ref, kw_ref, vst_ref, vwt_ref, et_ref, ov_ref, o_ref, acc_sc, s_sc,
              *, n_cmp, n_top, kt):
    grp = pl.program_id(1)
    s0 = pl.program_id(2) * Q_BLOCK
    nch = kc_ref.shape[1]
    nblk = ov_ref.shape[0]

    qt = jnp.concatenate([qt_ref[h * LANES:(h + 1) * LANES, :] for h in range(NSA_HPG)], axis=1)
    t_row = s0 + lax.broadcasted_iota(jnp.int32, (1, Q_BLOCK), 1)

    cidx = lax.broadcasted_iota(jnp.int32, (nch, 1), 0)
    cvalid = (cidx * CMP_STRIDE + (CMP_BLOCK - 1) <= t_row) & (cidx < n_cmp)
    sc = _dot(kc_ref[0], qt) + _tile_heads(jnp.where(cvalid, 0.0, NEG))
    pc = jnp.exp2(sc - jnp.max(sc, axis=0, keepdims=True)).astype(BF16)
    stacked = jnp.concatenate([vct_ref[0], ov_ref[...], jnp.ones((8, nch), BF16)], axis=0)
    res = _dot(stacked, pc)
    inv = jnp.where(_tile_heads(t_row >= CMP_BLOCK - 1), 1.0 / jnp.maximum(res[LANES + nblk:LANES + nblk + 1], 1e-30),
                    0.0)
    oc = res[:LANES] * inv
    imp4 = res[LANES:LANES + nblk] * inv
    imp = (imp4[:, 0:Q_BLOCK] + imp4[:, Q_BLOCK:2 * Q_BLOCK] + imp4[:, 2 * Q_BLOCK:3 * Q_BLOCK]
           + imp4[:, 3 * Q_BLOCK:4 * Q_BLOCK])
    blk = lax.broadcasted_iota(jnp.int32, (nblk, Q_BLOCK), 0)
    cur = t_row // SLC_BLOCK
    imp = jnp.where(blk * SLC_BLOCK > t_row, NEG, imp)
    imp = jnp.where((blk == 0) | (blk == cur) | (blk == cur - 1), -NEG, imp)

    def pick_rounds(v, rounds):
        for _ in range(rounds):
            mx = jnp.max(v, axis=0, keepdims=True)
            first = jnp.min(jnp.where(v == mx, blk, nblk), axis=0, keepdims=True)
            v = jnp.where(blk == first, -jnp.inf, v)
        return v

    quarter = n_top // 4
    picked = pick_rounds(imp, quarter)

    wlen = WINDOW + Q_BLOCK
    w0 = pl.multiple_of(jnp.maximum(s0 - WINDOW, 0), Q_BLOCK)
    kpos = w0 + lax.broadcasted_iota(jnp.int32, (wlen, 1), 0)
    wbias = jnp.where((kpos <= t_row) & (kpos > t_row - WINDOW), 0.0, NEG)
    sw = _dot(kw_ref[0, pl.ds(w0, wlen), :], qt) + _tile_heads(wbias)
    picked = pick_rounds(picked, quarter)
    pw = jnp.exp2(sw - jnp.max(sw, axis=0, keepdims=True)).astype(BF16)
    picked = pick_rounds(picked, quarter)
    ow = _dot(vwt_ref[:, pl.ds(w0, wlen)], pw)
    ow = ow[:NSA_DH] / ow[NSA_DH:NSA_DH + 1]

    d0 = pl.multiple_of(s0, Q_BLOCK)
    dpos = s0 + lax.broadcasted_iota(jnp.int32, (Q_BLOCK, 1), 0)
    sd = _dot(ks_ref[0, pl.ds(d0, Q_BLOCK), :], qt) + _tile_heads(jnp.where(dpos <= t_row, 0.0, NEG))
    m_diag = jnp.max(sd, axis=0, keepdims=True)
    acc_sc[0] = _dot(vst_ref[:, pl.ds(d0, Q_BLOCK)], jnp.exp2(sd - m_diag).astype(BF16))
    acc_sc[1] = jnp.zeros(acc_sc.shape[1:], F32)
    picked = pick_rounds(picked, n_top - 3 * quarter)

    before = blk * SLC_BLOCK < s0
    selb = jnp.where((picked == -jnp.inf) & before, 0.0, MASKED).astype(BF16)
    rhs = jnp.concatenate([qt, _tile_heads(selb)], axis=0)

    last_tile = ks_ref.shape[1] // kt - 1

    def scores(idx, slot):
        k0 = pl.multiple_of(jnp.minimum(idx, last_tile) * kt, kt)
        lhs = jnp.concatenate([ks_ref[0, pl.ds(k0, kt), :], et_ref[pl.ds(k0, kt), :]], axis=1)
        s_sc[slot] = _dot(lhs, rhs)

    def update(idx, slot, m_old, acc_ref):
        k0 = pl.multiple_of(idx * kt, kt)
        m_new = jnp.maximum(m_old, jnp.max(s_sc[slot], axis=0, keepdims=True))
        p = jnp.exp2(s_sc[slot] - m_new).astype(BF16)
        acc_ref[...] = jnp.exp2(m_old - m_new) * acc_ref[...] + _dot(vst_ref[:, pl.ds(k0, kt)], p)
        return m_new

    def four_tiles(j, carry):
        m0, m1 = carry
        i = 4 * j
        scores(i + 2, 2)
        m0 = update(i, 0, m0, acc_sc.at[0])
        scores(i + 3, 3)
        m1 = update(i + 1, 1, m1, acc_sc.at[1])
        scores(i + 4, 0)
        m0 = update(i + 2, 2, m0, acc_sc.at[0])
        scores(i + 5, 1)
        m1 = update(i + 3, 3, m1, acc_sc.at[1])
        return m0, m1

    n_tiles = (s0 + kt - 1) // kt
    scores(0, 0)
    scores(1, 1)
    m0, m1 = lax.fori_loop(0, (n_tiles + 3) // 4, four_tiles,
                           (m_diag, jnp.full((1, NSA_HPG * Q_BLOCK), NEG, F32)))
    m_fin = jnp.maximum(m0, m1)
    acc = acc_sc[0] * jnp.exp2(m0 - m_fin) + acc_sc[1] * jnp.exp2(m1 - m_fin)
    osl = acc[:NSA_DH] / acc[NSA_DH:NSA_DH + 1]

    oc = jnp.where(grp == 0, oc[:NSA_DH], oc[NSA_DH:])
    gts = jax.nn.sigmoid(gt_ref[...])
    for h in range(NSA_HPG):
        cols = slice(h * Q_BLOCK, (h + 1) * Q_BLOCK)
        o_ref[h * NSA_DH:(h + 1) * NSA_DH, :] = (
            gts[3 * h:3 * h + 1, :] * oc[:, cols] + gts[3 * h + 1:3 * h + 2, :] * osl[:, cols]
            + gts[3 * h + 2:3 * h + 3, :] * ow[:, cols])


def _nsa_attention(qt, gt, kc, vct, okv, vt, expand_t, overlap, b, t, n_cmp, n_top, kt):
    nch = kc.shape[1]
    nq = t // Q_BLOCK
    n = b * t
    return pl.pallas_call(
        functools.partial(_nsa_body, n_cmp=n_cmp, n_top=n_top, kt=kt),
        grid=(b, NSA_GROUPS, nq),
        in_specs=[
            pl.BlockSpec((NSA_HPG * LANES, Q_BLOCK), lambda bi, g, i: (g, bi * nq + i)),
            pl.BlockSpec((GATE_ROWS, Q_BLOCK), lambda bi, g, i: (g, bi * nq + i)),
            pl.BlockSpec((1, nch, LANES), lambda bi, g, i: (bi, 0, 0)),
            pl.BlockSpec((1, LANES, nch), lambda bi, g, i: (bi, 0, 0)),
            pl.BlockSpec((1, t, LANES), lambda bi, g, i: (bi, 0, 2)),
            pl.BlockSpec((1, t, LANES), lambda bi, g, i: (bi, 0, 4)),
            pl.BlockSpec((LANES, t), lambda bi, g, i: (g, bi)),
            pl.BlockSpec((LANES, t), lambda bi, g, i: (NSA_GROUPS + g, bi)),
            pl.BlockSpec(expand_t.shape, lambda bi, g, i: (0, 0)),
            pl.BlockSpec(overlap.shape, lambda bi, g, i: (0, 0)),
        ],
        out_specs=pl.BlockSpec((NSA_HPG * NSA_DH, Q_BLOCK), lambda bi, g, i: (g, bi * nq + i)),
        out_shape=jax.ShapeDtypeStruct((NSA_W, n), F32),
        scratch_shapes=[pltpu.VMEM((2, LANES, NSA_HPG * Q_BLOCK), F32),
                        pltpu.VMEM((4, kt, NSA_HPG * Q_BLOCK), F32)],
        compiler_params=_cparams(("parallel", "parallel", "arbitrary"), 56),
        name="nsa_attention",
    )(qt, gt, kc, vct, okv, okv, vt, vt, expand_t, overlap)


def _split_bf16(a):
    hi = a.astype(BF16)
    return hi, (a - hi.astype(F32)).astype(BF16)


def _unit_lower_inverses(lmats):
    c = lmats[0].shape[0]
    r = lax.broadcasted_iota(jnp.int32, (c, c), 0)
    col = lax.broadcasted_iota(jnp.int32, (c, c), 1)
    eye = jnp.where(r == col, 1.0, 0.0)
    xs = [eye - l for l in lmats]
    ps = []
    for l in lmats:
        l16 = l.astype(BF16)
        ps.append(_dot(l16, l16))
    steps = int(np.log2(c)) - 1
    for s in range(steps):
        last = s + 1 == steps
        for i in range(len(lmats)):
            rhs = ps[i].astype(BF16)
            if last:
                xs[i] = xs[i] + _dot(xs[i].astype(BF16), rhs)
            else:
                both = _dot(jnp.concatenate([xs[i], ps[i]], axis=0).astype(BF16), rhs)
                xs[i] = xs[i] + both[:c]
                ps[i] = both[c:]
    return xs


def _gdn_body(x_ref, z_ref, ab_ref, alog_ref, dtb_ref, og_ref, o_ref, s_sc, *, ct):
    nb = x_ref.shape[0]

    @pl.when(pl.program_id(0) == 0)
    def _():
        s_sc[...] = jnp.zeros(s_sc.shape, F32)

    ch = GDN_CHUNK
    r = lax.broadcasted_iota(jnp.int32, (ch, ch), 0)
    col = lax.broadcasted_iota(jnp.int32, (ch, ch), 1)
    incl = r >= col
    strict = r > col
    tril16 = jnp.concatenate([jnp.where(incl, 1.0, 0.0).astype(BF16)] * 3, axis=1)

    units = []
    for ci in range(ct // ch):
        rows = slice(ci * ch, (ci + 1) * ch)
        for bi in range(nb):
            ab = ab_ref[bi, rows, :]
            g_all = -jnp.exp(alog_ref[...]) * jax.nn.softplus(ab + dtb_ref[...])
            beta_all = jax.nn.sigmoid(ab)
            g_hi, g_lo = _split_bf16(g_all)
            g_lo2 = (g_all - g_hi.astype(F32) - g_lo.astype(F32)).astype(BF16)
            gc_all = _dot(tril16, jnp.concatenate([g_hi, g_lo, g_lo2], axis=0))
            gc_t = gc_all.T
            for h in range(GDN_HEADS):
                hs = slice(h * GDN_DH, (h + 1) * GDN_DH)
                q16 = x_ref[bi, rows, hs]
                k16 = x_ref[bi, rows, GDN_W + h * GDN_DH:GDN_W + (h + 1) * GDN_DH]
                qh, kh = q16.astype(F32), k16.astype(F32)
                vh = x_ref[bi, rows, 2 * GDN_W + h * GDN_DH:2 * GDN_W + (h + 1) * GDN_DH].astype(F32)
                gc = gc_all[:, h:h + 1]
                gr = gc_t[h:h + 1, :]
                g_last = gc_all[ch - 1:ch, h:h + 1]
                beta = beta_all[:, GDN_HEADS + h:GDN_HEADS + h + 1]
                eg = jnp.exp(gc)
                decay = jnp.where(incl, jnp.exp(jnp.minimum(gc - gr, 0.0)), 0.0)
                kb = kh * beta
                with_k = (_dot_nt(jnp.concatenate([kb.astype(BF16), q16], axis=0), k16)
                          * jnp.concatenate([decay, decay], axis=0))
                units.append(dict(
                    rows=rows, bi=bi, h=h,
                    lmat=jnp.where(strict, with_k[:ch], 0.0),
                    vb_kbg=jnp.concatenate([(vh * beta).astype(BF16), (kb * eg).astype(BF16)], axis=1),
                    qk=jnp.where(incl, with_k[ch:], 0.0).astype(BF16),
                    qg=(qh * eg).astype(BF16), kd_t=(kh * jnp.exp(g_last - gc)).T.astype(BF16),
                    gl=jnp.exp(g_last)))
    tinvs = _unit_lower_inverses([u["lmat"] for u in units])
    for u, tinv in zip(units, tinvs):
        u_w = _dot(tinv.astype(BF16), u["vb_kbg"])
        u["u"] = u_w[:, :GDN_DH]
        u["w_qg"] = jnp.concatenate([u_w[:, GDN_DH:].astype(BF16), u["qg"]], axis=0)
        u["kd_qk"] = jnp.concatenate([u["kd_t"], u["qk"]], axis=0)

    for u in units:
        bi, h, rows = u["bi"], u["h"], u["rows"]
        hs = slice(h * GDN_DH, (h + 1) * GDN_DH)
        s_old = s_sc[bi * GDN_HEADS + h]
        from_state = _dot(u["w_qg"], s_old.astype(BF16))
        v_new = (u["u"] - from_state[:ch]).astype(BF16)
        from_v = _dot(u["kd_qk"], v_new)
        s_sc[bi * GDN_HEADS + h] = s_old * u["gl"] + from_v[:GDN_DH]
        o = from_state[ch:] + from_v[GDN_DH:]
        on = o * lax.rsqrt(jnp.mean(o * o, axis=-1, keepdims=True) + EPS) * og_ref[...]
        zh = z_ref[bi, rows, hs].astype(F32)
        o_ref[bi, rows, hs] = (on * (zh * jax.nn.sigmoid(zh))).astype(BF16)


def _gdn(oqkv, oz, oab, alog, dtb, og, ct=128):
    b, t, w3 = oqkv.shape
    full = lambda a: pl.BlockSpec(a.shape, lambda c: (0,) * a.ndim)
    return pl.pallas_call(
        functools.partial(_gdn_body, ct=ct),
        grid=(t // ct,),
        in_specs=[
            pl.BlockSpec((b, ct, w3), lambda c: (0, c, 0)),
            pl.BlockSpec((b, ct, GDN_W), lambda c: (0, c, 0)),
            pl.BlockSpec((b, ct, LANES), lambda c: (0, c, 0)),
            full(alog), full(dtb), full(og),
        ],
        out_specs=pl.BlockSpec((b, ct, GDN_W), lambda c: (0, c, 0)),
        out_shape=jax.ShapeDtypeStruct((b, t, GDN_W), BF16),
        scratch_shapes=[pltpu.VMEM((b * GDN_HEADS, GDN_DH, GDN_DH), F32)],
        compiler_params=_cparams(("arbitrary",), 32),
        name="gdn",
    )(oqkv, oz, oab, alog, dtb, og)


def _outproj_body(ont_ref, og_ref, x_ref, ng_ref, wo_ref, fg_ref, wr_ref, br_ref,
                  x1_ref, h2_ref, gate_ref, route_ref, cnt_ref, cnt_sc):
    i = pl.program_id(0)
    tm = x_ref.shape[0]

    @pl.when(i == 0)
    def _():
        cnt_sc[...] = jnp.zeros(cnt_sc.shape, F32)

    a = ont_ref[...]
    a = (a * lax.rsqrt(jnp.mean(a * a, axis=0, keepdims=True) + EPS) * ng_ref[...]).astype(BF16)
    x1 = x_ref[...] + _dot_tn(a, wo_ref[0:NSA_W, :]) + _dot(og_ref[...], wo_ref[NSA_W:, :])
    x1_ref[...] = x1
    h2f = x1 * lax.rsqrt(jnp.mean(x1 * x1, axis=-1, keepdims=True) + EPS) * fg_ref[...]
    _store_pieces(h2_ref, h2f)
    h2 = h2f.astype(BF16)

    lane = lax.broadcasted_iota(jnp.int32, (tm, LANES), 1)
    logits = jnp.where(lane < N_EXPERTS, _dot(h2, wr_ref[...]) + br_ref[...], -jnp.inf)
    onehot = jnp.zeros((tm, LANES), F32)
    idx = jnp.zeros((tm, LANES), jnp.int32)
    vals = jnp.zeros((tm, LANES), F32)
    firsts = []
    v = logits
    top0 = None
    for k in range(TOP_K):
        mx = jnp.max(v, axis=-1, keepdims=True)
        first = jnp.min(jnp.where(v == mx, lane, LANES), axis=-1, keepdims=True)
        hit = lane == first
        v = jnp.where(hit, -jnp.inf, v)
        onehot = jnp.where(hit, 1.0, onehot)
        idx = jnp.where(lane == k, first, idx)
        top0 = mx if k == 0 else top0
        vals = jnp.where(lane == k, jnp.exp(mx - top0), vals)
        firsts.append(first)
    gate_ref[...] = vals / jnp.sum(vals, axis=-1, keepdims=True)

    r = lax.broadcasted_iota(jnp.int32, (tm, tm), 0)
    c = lax.broadcasted_iota(jnp.int32, (tm, tm), 1)
    before = jnp.where(r > c, 1.0, 0.0).astype(BF16)
    excl = cnt_sc[...] + _dot(before, onehot.astype(BF16))
    rank = jnp.zeros((tm, LANES), F32)
    for k in range(TOP_K):
        rk = jnp.sum(jnp.where(lane == firsts[k], excl, 0.0), axis=-1, keepdims=True)
        rank = jnp.where(lane == TOP_K + k, rk, rank)
    route = (idx.astype(F32) + rank).T
    route_ref[...] = route[:2 * TOP_K].astype(jnp.int32)
    cnt_sc[...] = cnt_sc[...] + jnp.sum(onehot, axis=0, keepdims=True)
    cnt_ref[...] = cnt_sc[...].astype(jnp.int32)


def _out_proj(o_nsa_t, o_gdn, x2, ng, wo, fg, wr, br, tm=512):
    n, d = x2.shape
    full = lambda a: pl.BlockSpec(a.shape, lambda i: (0,) * a.ndim)
    row = lambda w: pl.BlockSpec((tm, w), lambda i: (i, 0))
    return pl.pallas_call(
        _outproj_body,
        grid=(n // tm,),
        in_specs=[pl.BlockSpec((NSA_W, tm), lambda i: (0, i)), row(GDN_W), row(d), full(ng), full(wo), full(fg),
                  full(wr), full(br)],
        out_specs=[row(d), pl.BlockSpec((d // 2 // SC_SUBROW, tm, SC_SUBROW), lambda i: (0, i, 0)),
                   row(LANES), pl.BlockSpec((2 * TOP_K, tm), lambda i: (0, i)),
                   pl.BlockSpec((1, LANES), lambda i: (0, 0))],
        out_shape=[jax.ShapeDtypeStruct((n, d), F32),
                   jax.ShapeDtypeStruct((d // 2 // SC_SUBROW, n, SC_SUBROW), jnp.int32),
                   jax.ShapeDtypeStruct((n, LANES), F32), jax.ShapeDtypeStruct((2 * TOP_K, n), jnp.int32),
                   jax.ShapeDtypeStruct((1, LANES), jnp.int32)],
        scratch_shapes=[pltpu.VMEM((1, LANES), F32)],
        compiler_params=_cparams(("arbitrary",), 48),
        name="out_proj_router",
    )(o_nsa_t, o_gdn, x2, ng, wo, fg, wr, br)


def _dest_body(ps_ref, route_ref, o_ref, *, n_rows, pieces):
    expert = route_ref[0:TOP_K, :]
    start = jnp.zeros(expert.shape, jnp.int32)
    for e in range(N_EXPERTS):
        start = jnp.where(expert == e, ps_ref[e], start)
    dest = start + route_ref[TOP_K:2 * TOP_K, :]
    for k in range(TOP_K):
        for j in range(pieces):
            o_ref[k * pieces + j:k * pieces + j + 1, :] = dest[k:k + 1, :] + j * n_rows


def _dest_rows(pstarts, route, n_rows, pieces):
    n = route.shape[1]
    tn = min(2048, n)
    grid_spec = pltpu.PrefetchScalarGridSpec(
        num_scalar_prefetch=1,
        grid=(n // tn,),
        in_specs=[pl.BlockSpec((2 * TOP_K, tn), lambda i, ps: (0, i))],
        out_specs=pl.BlockSpec((TOP_K * pieces, tn), lambda i, ps: (0, i)),
    )
    return pl.pallas_call(
        functools.partial(_dest_body, n_rows=n_rows, pieces=pieces),
        grid_spec=grid_spec,
        out_shape=jax.ShapeDtypeStruct((TOP_K * pieces, n), jnp.int32),
        name="moe_dest_rows",
    )(pstarts, route)


def _expert_body(be_ref, fresh_ref, slot_ref, next_ref, xs_ref, wg_hbm, bg_ref, wu_hbm, bu_ref, wd_hbm, bd_ref,
                 y_ref, wf32, w16, sems):
    i = pl.program_id(0)
    used = i < be_ref[pl.num_programs(0)]
    hbm = (wg_hbm, wu_hbm, wd_hbm)

    def weight_copy(expert, slot, j):
        return pltpu.make_async_copy(hbm[j].at[expert], wf32.at[slot, j], sems.at[slot, j])

    @pl.when((i == 0) & used)
    def _():
        for j in range(3):
            weight_copy(be_ref[0], 0, j).start()

    @pl.when(used & (fresh_ref[i] == 1))
    def _():
        slot = slot_ref[i]
        for j in range(3):
            weight_copy(be_ref[i], slot, j).wait()
            w16[j] = wf32[slot, j].astype(BF16)

        @pl.when(next_ref[i] >= 0)
        def _():
            for j in range(3):
                weight_copy(next_ref[i], 1 - slot, j).start()

    @pl.when(jnp.logical_not(used))
    def _():
        y_ref[...] = jnp.zeros(y_ref.shape, y_ref.dtype)

    @pl.when(used)
    def _():
        x = _join_pieces(xs_ref).astype(BF16)
        gate = jnp.minimum(_dot(x, w16[0]) + bg_ref[0], SWIGLU_LIMIT)
        up = jnp.clip(_dot(x, w16[1]) + bu_ref[0], -SWIGLU_LIMIT, SWIGLU_LIMIT)
        glu = gate * jax.nn.sigmoid(gate * SWIGLU_ALPHA)
        _store_pieces(y_ref, _dot(((up + 1.0) * glu).astype(BF16), w16[2]) + bd_ref[0])


def _experts(blk_e, fresh, slot, nxt, xs, wg, bg, wu, bu, wd, bd):
    pieces, n_rows, sub = xs.shape
    d, de = wg.shape[1], wg.shape[2]
    assert d == de
    r = MOE_ROW_BLOCK
    bspec = lambda w: pl.BlockSpec((1, 1, w), lambda i, be, *_: (be[i], 0, 0))
    hbm = pl.BlockSpec(memory_space=pl.ANY)
    grid_spec = pltpu.PrefetchScalarGridSpec(
        num_scalar_prefetch=4,
        grid=(n_rows // r,),
        in_specs=[pl.BlockSpec((pieces, r, sub), lambda i, *_: (0, i, 0)),
                  hbm, bspec(de), hbm, bspec(de), hbm, bspec(d)],
        out_specs=pl.BlockSpec((pieces, r, sub), lambda i, *_: (0, i, 0)),
        scratch_shapes=[pltpu.VMEM((2, 3, d, de), F32), pltpu.VMEM((3, d, de), BF16),
                        pltpu.SemaphoreType.DMA((2, 3))],
    )
    return pl.pallas_call(
        _expert_body,
        grid_spec=grid_spec,
        out_shape=jax.ShapeDtypeStruct((pieces, n_rows, sub), jnp.int32),
        compiler_params=_cparams(("arbitrary",), 56),
        name="moe_experts",
    )(blk_e, fresh, slot, nxt, xs, wg, bg, wu, bu, wd, bd)


SC_WINDOW = 128
SC_SUBROW = 256


def _sc_mesh():
    return plsc.VectorSubcoreMesh(core_axis_name="c", subcore_axis_name="s")


def _sc_dispatch(h2, dest_rows, n_rows):
    n, d = h2.shape

    @functools.partial(pl.kernel, out_type=jax.ShapeDtypeStruct((n_rows, d), h2.dtype), mesh=_sc_mesh())
    def dispatch(x_hbm, *refs):
        idx_hbm, o_hbm = refs[:TOP_K], refs[TOP_K]

        def body(x_vmem, *idx_vmem):
            for iv in idx_vmem:
                pltpu.sync_copy(x_vmem, o_hbm.at[iv.at[0]])

        pltpu.emit_pipeline(
            body,
            grid=(n // SC_WINDOW,),
            in_specs=[pl.BlockSpec((SC_WINDOW, d), lambda i: (i, 0))]
                     + [pl.BlockSpec((1, SC_WINDOW), lambda i: (0, i))] * TOP_K,
            out_specs=[],
            core_axis_name=("c", "s"),
            dimension_semantics=(pltpu.PARALLEL,),
        )(x_hbm, *idx_hbm)

    return dispatch(h2, *dest_rows)


def _sc_gather(table, idx):
    _, d = table.shape
    m = idx.shape[1]

    @functools.partial(pl.kernel, out_type=jax.ShapeDtypeStruct((m, d), table.dtype), mesh=_sc_mesh())
    def gather(t_hbm, i_hbm, o_hbm):
        def body(i_vmem, o_vmem):
            pltpu.sync_copy(t_hbm.at[i_vmem.at[0]], o_vmem)

        pltpu.emit_pipeline(
            body,
            grid=(m // SC_WINDOW,),
            in_specs=[pl.BlockSpec((1, SC_WINDOW), lambda i: (0, i))],
            out_specs=[pl.BlockSpec((SC_WINDOW, d), lambda i: (i, 0))],
            core_axis_name=("c", "s"),
            dimension_semantics=(pltpu.PARALLEL,),
        )(i_hbm, o_hbm)

    return gather(table, idx)


def _combine_body(x1_ref, y_ref, gate_ref, o_ref):
    acc = x1_ref[...]
    for k in range(TOP_K):
        acc = acc + gate_ref[:, k:k + 1] * _join_pieces(y_ref.at[k])
    o_ref[...] = acc


def _combine(x1, y4, gates, tm=512):
    n, d = x1.shape
    pieces, sub = y4.shape[1], y4.shape[3]
    row = lambda w: pl.BlockSpec((tm, w), lambda i: (i, 0))
    return pl.pallas_call(
        _combine_body,
        grid=(n // tm,),
        in_specs=[row(d), pl.BlockSpec((TOP_K, pieces, tm, sub), lambda i: (0, 0, i, 0)), row(LANES)],
        out_specs=row(d),
        out_shape=jax.ShapeDtypeStruct((n, d), F32),
        compiler_params=_cparams(("parallel",), 48),
        name="moe_combine",
    )(x1, y4, gates)


def _pad_lanes(a, width=LANES):
    return jnp.pad(a, ((0, 0), (0, width - a.shape[1])))


def _layer(x, attn_norm_g, w_in, q_g, kc_g, ks_g, kw_g, ck_pos, ck_w1, ck_b1, ck_w2, ck_b2,
           cv_pos, cv_w1, cv_b1, cv_w2, cv_b2, nsa_out_g, conv_w, a_log, dt_bias, gdn_out_g, w_out,
           ffn_g, router_w, router_b, e_wg, e_bg, e_wu, e_bu, e_wd, e_bd):
    b, t, d = x.shape
    n = b * t
    x2 = x.reshape(n, d)

    o = np.cumsum([0, NSA_W] + [NSA_GROUPS * NSA_DH] * 6 + [3 * NSA_HEADS, 3 * GDN_W, GDN_W, GDN_HEADS, GDN_HEADS])
    wq_t = w_in[:, o[0]:o[1]].T.reshape(NSA_GROUPS, NSA_HPG, NSA_DH, d)
    zq = jnp.zeros((NSA_HPG, NSA_DH, d), F32)
    wq_t = jnp.stack([jnp.concatenate([wq_t[0], zq], axis=1), jnp.concatenate([zq, wq_t[1]], axis=1)])
    wq_t = wq_t.reshape(NSA_HEADS * LANES, d).astype(BF16)
    qg1 = q_g * (NSA_DH ** -0.5 * np.log2(np.e))
    zg = jnp.zeros((NSA_DH,), F32)
    qg_col = jnp.concatenate([jnp.tile(jnp.concatenate([qg1, zg]), NSA_HPG),
                              jnp.tile(jnp.concatenate([zg, qg1]), NSA_HPG)]).reshape(NSA_HEADS * LANES, 1)
    wkv = w_in[:, o[1]:o[7]].astype(BF16)
    ones = jnp.ones((LANES,), F32)
    kg = jnp.concatenate([ones, ones, ks_g, ks_g, ones, kw_g, kw_g, ones]).reshape(1, 6 * LANES)
    wv_t = jnp.concatenate([w_in[:, o[4]:o[5]], w_in[:, o[6]:o[7]]], axis=1).T.reshape(2 * NSA_GROUPS, NSA_DH, d)
    wv_t = jnp.pad(wv_t, ((0, 0), (0, LANES - NSA_DH), (0, 0))).reshape(2 * NSA_GROUPS * LANES, d).astype(BF16)
    vone = jnp.asarray((np.arange(2 * NSA_GROUPS * LANES) % LANES == NSA_DH).astype(np.float32)[:, None])
    wg_t = w_in[:, o[7]:o[8]].T.reshape(NSA_GROUPS, NSA_HPG * 3, d)
    wg_t = jnp.pad(wg_t, ((0, 0), (0, GATE_ROWS - NSA_HPG * 3), (0, 0))).reshape(NSA_GROUPS * GATE_ROWS, d)
    wg_t = wg_t.astype(BF16)
    wab = _pad_lanes(w_in[:, o[10]:o[12]]).astype(BF16)
    wqkv = w_in[:, o[8]:o[9]].astype(BF16)
    wz = w_in[:, o[9]:o[10]].astype(BF16)

    tm = min(512, t)
    oqt, okv, ovt, ogt, oqkv, oz, oab = _in_proj(x2, attn_norm_g.reshape(1, d), wq_t, wkv, wv_t, wg_t, wqkv, wz,
                                                 wab, qg_col, kg, vone, conv_w, t // tm, tm)

    nch = t // CMP_STRIDE
    n_cmp = (t - CMP_BLOCK) // CMP_STRIDE + 1
    half = CMP_STRIDE * NSA_DH
    xflat = okv[:, :2 * LANES].reshape(b, nch, CMP_STRIDE, 2, NSA_GROUPS, NSA_DH)
    xflat = xflat.transpose(0, 3, 4, 1, 2, 5).reshape(b, 2, NSA_GROUPS, nch, half)
    pos = jnp.stack([ck_pos, cv_pos]).reshape(2, 2, 1, half)
    w1 = jnp.stack([ck_w1, cv_w1]).reshape(2, 2, half, CMP_HIDDEN).astype(BF16)
    b1 = jnp.stack([ck_b1, cv_b1]).reshape(2, 1, CMP_HIDDEN)
    w2 = jnp.stack([ck_w2, cv_w2]).astype(BF16)
    b2 = jnp.stack([ck_b2, cv_b2]).reshape(2, 1, NSA_DH)
    w2t = jnp.stack([ck_w2.T, cv_w2.T]).astype(BF16)
    b2t = jnp.stack([ck_b2, cv_b2]).reshape(2, NSA_DH, 1)
    kc, vct = _compress(xflat, pos, w1, b1, w2, b2, w2t, b2t, kc_g.reshape(1, NSA_DH), n_cmp)

    n_slc = t // SLC_BLOCK
    n_top = min(SLC_TOPK, n_slc)
    nblk = max(n_slc, LANES)
    kt = min(256, t // 4)
    assert (t // kt) % 4 == 0
    ci = np.arange(nch)[None, :] * CMP_STRIDE
    sj = np.arange(nblk)[:, None] * SLC_BLOCK
    overlap = ((ci < sj + SLC_BLOCK) & (ci + CMP_BLOCK > sj) & (np.arange(nch)[None, :] < n_cmp)
               & (np.arange(nblk)[:, None] < n_slc))
    expand_t = (np.arange(t)[:, None] // SLC_BLOCK) == np.arange(nblk)[None, :]
    o_nsa_t = _nsa_attention(oqt, ogt, kc, vct, okv.reshape(b, t, -1), ovt, jnp.asarray(expand_t, BF16),
                             jnp.asarray(overlap, BF16), b, t, n_cmp, n_top, kt)

    alog_row = _pad_lanes(a_log.reshape(1, GDN_HEADS))
    dtb_row = _pad_lanes(dt_bias.reshape(1, GDN_HEADS))
    o_gdn = _gdn(oqkv.reshape(b, t, -1), oz.reshape(b, t, -1), oab.reshape(b, t, -1),
                 alog_row, dtb_row, gdn_out_g.reshape(1, GDN_DH))

    wr = _pad_lanes(router_w).astype(BF16)
    br = _pad_lanes(router_b.reshape(1, N_EXPERTS))
    x1, h2, gates, route, counts = _out_proj(
        o_nsa_t, o_gdn.reshape(n, GDN_W), x2, nsa_out_g.reshape(NSA_W, 1),
        w_out.astype(BF16), ffn_g.reshape(1, d), wr, br)

    r = MOE_ROW_BLOCK
    nk = n * TOP_K
    counts = counts[0, :N_EXPERTS]
    pcounts = (counts + r - 1) // r * r
    pends = jnp.cumsum(pcounts)
    pstarts = pends - pcounts
    n_rows = (nk + r - 1) // r * r + N_EXPERTS * r
    n_blocks = n_rows // r
    blk_start = jnp.arange(n_blocks, dtype=jnp.int32)[:, None] * r
    blk_e = jnp.minimum(jnp.sum(pends[None, :] <= blk_start, axis=1), N_EXPERTS - 1).astype(jnp.int32)
    n_used = (pends[-1] // r).astype(jnp.int32)
    fresh = (jnp.arange(n_blocks) < n_used) & (blk_e != jnp.concatenate([jnp.full((1,), -1, jnp.int32), blk_e[:-1]]))
    slot = ((jnp.cumsum(fresh) - 1) % 2).astype(jnp.int32)
    eid = jnp.arange(N_EXPERTS, dtype=jnp.int32)
    later = jnp.where((eid[None, :] > eid[:, None]) & (pcounts[None, :] > 0), eid[None, :], N_EXPERTS)
    next_expert = jnp.min(later, axis=1)
    next_expert = jnp.where(next_expert < N_EXPERTS, next_expert, -1).astype(jnp.int32)
    nxt = next_expert[blk_e]
    blk_e = jnp.concatenate([blk_e, n_used[None]])
    pieces = d // 2 // SC_SUBROW
    dest_p = _dest_rows(pstarts.astype(jnp.int32), route, n_rows, pieces).reshape(TOP_K, pieces, n)
    xs = _sc_dispatch(h2.reshape(pieces * n, SC_SUBROW), [dest_p[k].reshape(1, pieces * n) for k in range(TOP_K)],
                      pieces * n_rows)
    ys = _experts(blk_e, fresh.astype(jnp.int32), slot, nxt, xs.reshape(pieces, n_rows, SC_SUBROW), e_wg, e_bg.reshape(N_EXPERTS, 1, -1), e_wu,
                  e_bu.reshape(N_EXPERTS, 1, -1), e_wd, e_bd.reshape(N_EXPERTS, 1, -1))
    y4 = _sc_gather(ys.reshape(pieces * n_rows, SC_SUBROW), dest_p.reshape(1, nk * pieces))
    return _combine(x1, y4.reshape(TOP_K, pieces, n, SC_SUBROW), gates).reshape(b, t, d)


def kernel(x, attn_norm_g, w_in, nsa_q_norm_g, nsa_kc_norm_g, nsa_ks_norm_g, nsa_kw_norm_g, cmp_k_pos, cmp_k_w1, cmp_k_b1, cmp_k_w2, cmp_k_b2, cmp_v_pos, cmp_v_w1, cmp_v_b1, cmp_v_w2, cmp_v_b2, nsa_out_norm_g, gdn_conv_w, gdn_a_log, gdn_dt_bias, gdn_out_norm_g, w_out, ffn_norm_g, router_w, router_b, exp_w_gate, exp_b_gate, exp_w_up, exp_b_up, exp_w_down, exp_b_down):
    params = (attn_norm_g, w_in, nsa_q_norm_g, nsa_kc_norm_g, nsa_ks_norm_g, nsa_kw_norm_g,
              cmp_k_pos, cmp_k_w1, cmp_k_b1, cmp_k_w2, cmp_k_b2, cmp_v_pos, cmp_v_w1, cmp_v_b1, cmp_v_w2, cmp_v_b2,
              nsa_out_norm_g, gdn_conv_w, gdn_a_log, gdn_dt_bias, gdn_out_norm_g, w_out, ffn_norm_g,
              router_w, router_b, exp_w_gate, exp_b_gate, exp_w_up, exp_b_up, exp_w_down, exp_b_down)
    for l in range(attn_norm_g.shape[0]):
        x = _layer(x, *(p[l] for p in params))
    return x
```

```python
import functools

import jax
import jax.numpy as jnp
import numpy as np
from jax import lax
from jax.experimental import pallas as pl
from jax.experimental.pallas import tpu as pltpu
from jax.experimental.pallas import tpu_sc as plsc

F32 = jnp.float32
BF16 = jnp.bfloat16

EPS = 1e-6
NEG = -1e30
MASKED = -2.0 ** 100

NSA_HEADS = 8
NSA_GROUPS = 2
NSA_HPG = 4
NSA_DH = 64
CMP_BLOCK = 32
CMP_STRIDE = 16
CMP_HIDDEN = 256
SLC_BLOCK = 64
SLC_TOPK = 16
WINDOW = 512
Q_BLOCK = 128
GDN_HEADS = 4
GDN_DH = 128
GDN_CONV = 4
GDN_CHUNK = 64
N_EXPERTS = 32
TOP_K = 4
SWIGLU_LIMIT = 7.0
SWIGLU_ALPHA = 1.702
MOE_ROW_BLOCK = 256

LANES = 128
GATE_ROWS = 16
V_ROWS = 80
NSA_W = NSA_HEADS * NSA_DH
GDN_W = GDN_HEADS * GDN_DH

_NT = (((1,), (1,)), ((), ()))
_TN = (((0,), (0,)), ((), ()))


def _cparams(sem, vmem_mb):
    return pltpu.CompilerParams(dimension_semantics=sem, vmem_limit_bytes=vmem_mb * 1024 * 1024)


def _dot(a, b):
    return jnp.dot(a, b, preferred_element_type=F32)


def _dot_nt(a, b):
    return lax.dot_general(a, b, _NT, preferred_element_type=F32)


def _dot_tn(a, b):
    return lax.dot_general(a, b, _TN, preferred_element_type=F32)


def _store_pieces(ref, val):
    half = val.shape[1] // 2
    hi = lax.bitcast_convert_type(val[:, :half].astype(BF16).astype(F32), jnp.uint32)
    lo = lax.bitcast_convert_type(val[:, half:].astype(BF16).astype(F32), jnp.uint32)
    words = lax.bitcast_convert_type(hi | (lo >> 16), jnp.int32)
    sub = ref.shape[2]
    for j in range(ref.shape[0]):
        ref[j] = words[:, j * sub:(j + 1) * sub]


def _join_pieces(ref):
    words = jnp.concatenate([ref[j] for j in range(ref.shape[0])], axis=1)
    words = lax.bitcast_convert_type(words, jnp.uint32)
    hi = lax.bitcast_convert_type(words & jnp.uint32(0xFFFF0000), F32)
    lo = lax.bitcast_convert_type(words << 16, F32)
    return jnp.concatenate([hi, lo], axis=1)


def _inproj_body(x_ref, g_ref, wqt_ref, wkv_ref, wvt_ref, wgt_ref, wqkv_ref, wz_ref, wab_ref, qg_ref, kg_ref,
                 vone_ref, cw_ref, oqt_ref, okv_ref, ovt_ref, ogt_ref, oqkv_ref, oz_ref, oab_ref, ybuf,
                 *, tiles_per_seq):
    x = x_ref[...]
    h = (x * lax.rsqrt(jnp.mean(x * x, axis=-1, keepdims=True) + EPS) * g_ref[...]).astype(BF16)
    tm = x.shape[0]

    yq = _dot_nt(wqt_ref[...], h)
    for s in range(NSA_HEADS):
        sl = slice(s * LANES, (s + 1) * LANES)
        ys = yq[sl, :]
        ms = jnp.sum(ys * ys, axis=0, keepdims=True) * (1.0 / NSA_DH)
        oqt_ref[sl, :] = (ys * lax.rsqrt(ms + EPS) * qg_ref[sl, :]).astype(BF16)

    ykv = _dot(h, wkv_ref[...])
    lane = lax.broadcasted_iota(jnp.int32, (tm, LANES), 1)
    low = lane < NSA_DH
    for s in range(6):
        sl = slice(s * LANES, (s + 1) * LANES)
        ys = ykv[:, sl]
        if s in (2, 4):
            y2 = ys * ys
            s0 = jnp.sum(jnp.where(low, y2, 0.0), axis=-1, keepdims=True)
            s1 = jnp.sum(jnp.where(low, 0.0, y2), axis=-1, keepdims=True)
            ms = jnp.where(low, s0, s1) * (1.0 / NSA_DH)
            ys = ys * lax.rsqrt(ms + EPS) * kg_ref[:, sl]
        okv_ref[:, sl] = ys.astype(BF16)

    ovt_ref[...] = (_dot_nt(wvt_ref[...], h) + vone_ref[...]).astype(BF16)
    ogt_ref[...] = _dot_nt(wgt_ref[...], h)
    oz_ref[...] = _dot(h, wz_ref[...]).astype(BF16)
    oab_ref[...] = _dot(h, wab_ref[...])

    halo = ybuf.shape[0] - tm
    first = pl.program_id(0) % tiles_per_seq == 0

    @pl.when(first)
    def _():
        ybuf[0:halo, :] = jnp.zeros((halo, ybuf.shape[1]), F32)

    @pl.when(jnp.logical_not(first))
    def _():
        ybuf[0:halo, :] = ybuf[tm:tm + halo, :]

    ybuf[halo:halo + tm, :] = _dot(h, wqkv_ref[...])
    taps = cw_ref.shape[0]
    y = cw_ref[0:1, :] * ybuf[pl.ds(halo - taps + 1, tm), :]
    for k in range(1, taps):
        y = y + cw_ref[k:k + 1, :] * ybuf[pl.ds(halo - taps + 1 + k, tm), :]
    hy = 0.5 * y
    y = hy + hy * jnp.tanh(hy)
    for s in range(3 * GDN_HEADS):
        sl = slice(s * GDN_DH, (s + 1) * GDN_DH)
        ys = y[:, sl]
        if s < 2 * GDN_HEADS:
            scale = GDN_DH ** -0.5 if s < GDN_HEADS else 1.0
            ys = ys * (lax.rsqrt(jnp.sum(ys * ys, axis=-1, keepdims=True) + EPS) * scale)
        oqkv_ref[:, sl] = ys.astype(BF16)


def _in_proj(x2, g, wqt, wkv, wvt, wgt, wqkv, wz, wab, qg, kg, vone, conv_w, tiles_per_seq, tm):
    n, d = x2.shape
    full = lambda a: pl.BlockSpec(a.shape, lambda i: (0,) * a.ndim)
    row = lambda w: pl.BlockSpec((tm, w), lambda i: (i, 0))
    colb = lambda r: pl.BlockSpec((r, tm), lambda i: (0, i))
    return pl.pallas_call(
        functools.partial(_inproj_body, tiles_per_seq=tiles_per_seq),
        grid=(n // tm,),
        in_specs=[row(d)] + [full(a) for a in (g, wqt, wkv, wvt, wgt, wqkv, wz, wab, qg, kg, vone, conv_w)],
        out_specs=[colb(wqt.shape[0]), row(wkv.shape[1]), colb(wvt.shape[0]), colb(wgt.shape[0]),
                   row(wqkv.shape[1]), row(wz.shape[1]), row(wab.shape[1])],
        out_shape=[jax.ShapeDtypeStruct((wqt.shape[0], n), BF16), jax.ShapeDtypeStruct((n, wkv.shape[1]), BF16),
                   jax.ShapeDtypeStruct((wvt.shape[0], n), BF16), jax.ShapeDtypeStruct((wgt.shape[0], n), F32),
                   jax.ShapeDtypeStruct((n, wqkv.shape[1]), BF16), jax.ShapeDtypeStruct((n, wz.shape[1]), BF16),
                   jax.ShapeDtypeStruct((n, wab.shape[1]), F32)],
        scratch_shapes=[pltpu.VMEM((tm + 8, wqkv.shape[1]), F32)],
        compiler_params=_cparams(("arbitrary",), 56),
        name="in_proj",
    )(x2, g, wqt, wkv, wvt, wgt, wqkv, wz, wab, qg, kg, vone, conv_w)


def _compress_body(x_ref, pos_ref, w1_ref, b1_ref, w2_ref, b2_ref, w2t_ref, b2t_ref, g_ref, ok_ref, ovt_ref,
                   *, n_cmp):
    is_key = pl.program_id(1) == 0
    nch = x_ref.shape[3]
    hids = []
    for grp in range(NSA_GROUPS):
        x = x_ref[0, 0, grp].astype(F32)
        xa = (x + pos_ref[0, 0]).astype(BF16)
        xb = (x + pos_ref[0, 1]).astype(BF16)
        a = _dot(xa, w1_ref[0, 0])
        b = _dot(xb, w1_ref[0, 1])
        b_next = pltpu.roll(b, nch - 1, 0)
        hids.append(jax.nn.gelu(a + b_next + b1_ref[0]).astype(BF16))

    @pl.when(is_key)
    def _():
        row = lax.broadcasted_iota(jnp.int32, (nch, NSA_DH), 0)
        outs = []
        for grp in range(NSA_GROUPS):
            out = _dot(hids[grp], w2_ref[0]) + b2_ref[0]
            out = out * lax.rsqrt(jnp.mean(out * out, axis=-1, keepdims=True) + EPS) * g_ref[...]
            outs.append(jnp.where(row < n_cmp, out, 0.0))
        ok_ref[0] = jnp.concatenate(outs, axis=-1).astype(BF16)

    @pl.when(jnp.logical_not(is_key))
    def _():
        col = lax.broadcasted_iota(jnp.int32, (NSA_DH, nch), 1)
        outs = []
        for grp in range(NSA_GROUPS):
            out = _dot_nt(w2t_ref[0], hids[grp]) + b2t_ref[0]
            outs.append(jnp.where(col < n_cmp, out, 0.0))
        ovt_ref[0] = jnp.concatenate(outs, axis=0).astype(BF16)


def _compress(xflat, pos, w1, b1, w2, b2, w2t, b2t, kc_g, n_cmp):
    b, _, _, nch, flat = xflat.shape
    return pl.pallas_call(
        functools.partial(_compress_body, n_cmp=n_cmp),
        grid=(b, 2),
        in_specs=[
            pl.BlockSpec((1, 1, NSA_GROUPS, nch, flat), lambda i, j: (i, j, 0, 0, 0)),
            pl.BlockSpec((1, 2, 1, flat), lambda i, j: (j, 0, 0, 0)),
            pl.BlockSpec((1, 2, flat, CMP_HIDDEN), lambda i, j: (j, 0, 0, 0)),
            pl.BlockSpec((1, 1, CMP_HIDDEN), lambda i, j: (j, 0, 0)),
            pl.BlockSpec((1, CMP_HIDDEN, NSA_DH), lambda i, j: (j, 0, 0)),
            pl.BlockSpec((1, 1, NSA_DH), lambda i, j: (j, 0, 0)),
            pl.BlockSpec((1, NSA_DH, CMP_HIDDEN), lambda i, j: (j, 0, 0)),
            pl.BlockSpec((1, NSA_DH, 1), lambda i, j: (j, 0, 0)),
            pl.BlockSpec((1, NSA_DH), lambda i, j: (0, 0)),
        ],
        out_specs=[pl.BlockSpec((1, nch, LANES), lambda i, j: (i, 0, 0)),
                   pl.BlockSpec((1, LANES, nch), lambda i, j: (i, 0, 0))],
        out_shape=[jax.ShapeDtypeStruct((b, nch, LANES), BF16), jax.ShapeDtypeStruct((b, LANES, nch), BF16)],
        compiler_params=_cparams(("parallel", "arbitrary"), 32),
        name="nsa_compress",
    )(xflat, pos, w1, b1, w2, b2, w2t, b2t, kc_g)


def _tile_heads(a):
    return jnp.concatenate([a] * NSA_HPG, axis=1)


def _nsa_body(qt_ref, gt_ref, kc_ref, vct_ref, ks_ref, kw_ref, vst_ref, vwt_ref, et_ref, ov_ref, o_ref, acc_sc, s_sc,
              *, n_cmp, n_top, kt):
    grp = pl.program_id(1)
    s0 = pl.program_id(2) * Q_BLOCK
    nch = kc_ref.shape[1]
    nblk = ov_ref.shape[0]

    qt = jnp.concatenate([qt_ref[h * LANES:(h + 1) * LANES, :] for h in range(NSA_HPG)], axis=1)
    t_row = s0 + lax.broadcasted_iota(jnp.int32, (1, Q_BLOCK), 1)

    cidx = lax.broadcasted_iota(jnp.int32, (nch, 1), 0)
    cvalid = (cidx * CMP_STRIDE + (CMP_BLOCK - 1) <= t_row) & (cidx < n_cmp)
    sc = _dot(kc_ref[0], qt) + _tile_heads(jnp.where(cvalid, 0.0, NEG))
    pc = jnp.exp2(sc - jnp.max(sc, axis=0, keepdims=True)).astype(BF16)
    stacked = jnp.concatenate([vct_ref[0], ov_ref[...], jnp.ones((8, nch), BF16)], axis=0)
    res = _dot(stacked, pc)
    inv = jnp.where(_tile_heads(t_row >= CMP_BLOCK - 1), 1.0 / jnp.maximum(res[LANES + nblk:LANES + nblk + 1], 1e-30),
                    0.0)
    oc = res[:LANES] * inv
    imp4 = res[LANES:LANES + nblk] * inv
    imp = (imp4[:, 0:Q_BLOCK] + imp4[:, Q_BLOCK:2 * Q_BLOCK] + imp4[:, 2 * Q_BLOCK:3 * Q_BLOCK]
           + imp4[:, 3 * Q_BLOCK:4 * Q_BLOCK])
    blk = lax.broadcasted_iota(jnp.int32, (nblk, Q_BLOCK), 0)
    cur = t_row // SLC_BLOCK
    imp = jnp.where(blk * SLC_BLOCK > t_row, NEG, imp)
    imp = jnp.where((blk == 0) | (blk == cur) | (blk == cur - 1), -NEG, imp)

    def pick_rounds(v, rounds):
        for _ in range(rounds):
            mx = jnp.max(v, axis=0, keepdims=True)
            first = jnp.min(jnp.where(v == mx, blk, nblk), axis=0, keepdims=True)
            v = jnp.where(blk == first, -jnp.inf, v)
        return v

    quarter = n_top // 4
    picked = pick_rounds(imp, quarter)

    wlen = WINDOW + Q_BLOCK
    w0 = pl.multiple_of(jnp.maximum(s0 - WINDOW, 0), Q_BLOCK)
    kpos = w0 + lax.broadcasted_iota(jnp.int32, (wlen, 1), 0)
    wbias = jnp.where((kpos <= t_row) & (kpos > t_row - WINDOW), 0.0, NEG)
    sw = _dot(kw_ref[0, pl.ds(w0, wlen), :], qt) + _tile_heads(wbias)
    picked = pick_rounds(picked, quarter)
    pw = jnp.exp2(sw - jnp.max(sw, axis=0, keepdims=True)).astype(BF16)
    picked = pick_rounds(picked, quarter)
    ow = _dot(vwt_ref[0:V_ROWS, pl.ds(w0, wlen)], pw)
    ow = ow[:NSA_DH] / ow[NSA_DH:NSA_DH + 1]

    d0 = pl.multiple_of(s0, Q_BLOCK)
    dpos = s0 + lax.broadcasted_iota(jnp.int32, (Q_BLOCK, 1), 0)
    sd = _dot(ks_ref[0, pl.ds(d0, Q_BLOCK), :], qt) + _tile_heads(jnp.where(dpos <= t_row, 0.0, NEG))
    m_diag = jnp.max(sd, axis=0, keepdims=True)
    acc_sc[0] = _dot(vst_ref[0:V_ROWS, pl.ds(d0, Q_BLOCK)], jnp.exp2(sd - m_diag).astype(BF16))
    acc_sc[1] = jnp.zeros(acc_sc.shape[1:], F32)
    picked = pick_rounds(picked, n_top - 3 * quarter)

    before = blk * SLC_BLOCK < s0
    selb = jnp.where((picked == -jnp.inf) & before, 0.0, MASKED).astype(BF16)
    rhs = jnp.concatenate([qt, _tile_heads(selb)], axis=0)

    last_tile = ks_ref.shape[1] // kt - 1

    def scores(idx, slot):
        k0 = pl.multiple_of(jnp.minimum(idx, last_tile) * kt, kt)
        lhs = jnp.concatenate([ks_ref[0, pl.ds(k0, kt), :], et_ref[pl.ds(k0, kt), :]], axis=1)
        s_sc[slot] = _dot(lhs, rhs)

    def update(idx, slot, m_old, acc_ref):
        k0 = pl.multiple_of(idx * kt, kt)
        m_new = jnp.maximum(m_old, jnp.max(s_sc[slot], axis=0, keepdims=True))
        p = jnp.exp2(s_sc[slot] - m_new).astype(BF16)
        acc_ref[...] = jnp.exp2(m_old - m_new) * acc_ref[...] + _dot(vst_ref[0:V_ROWS, pl.ds(k0, kt)], p)
        return m_new

    def four_tiles(j, carry):
        m0, m1 = carry
        i = 4 * j
        scores(i + 2, 2)
        m0 = update(i, 0, m0, acc_sc.at[0])
        scores(i + 3, 3)
        m1 = update(i + 1, 1, m1, acc_sc.at[1])
        scores(i + 4, 0)
        m0 = update(i + 2, 2, m0, acc_sc.at[0])
        scores(i + 5, 1)
        m1 = update(i + 3, 3, m1, acc_sc.at[1])
        return m0, m1

    n_tiles = (s0 + kt - 1) // kt
    scores(0, 0)
    scores(1, 1)
    m0, m1 = lax.fori_loop(0, (n_tiles + 3) // 4, four_tiles,
                           (m_diag, jnp.full((1, NSA_HPG * Q_BLOCK), NEG, F32)))
    m_fin = jnp.maximum(m0, m1)
    acc = acc_sc[0] * jnp.exp2(m0 - m_fin) + acc_sc[1] * jnp.exp2(m1 - m_fin)
    osl = acc[:NSA_DH] / acc[NSA_DH:NSA_DH + 1]

    oc = jnp.where(grp == 0, oc[:NSA_DH], oc[NSA_DH:])
    gts = jax.nn.sigmoid(gt_ref[...])
    for h in range(NSA_HPG):
        cols = slice(h * Q_BLOCK, (h + 1) * Q_BLOCK)
        o_ref[h * NSA_DH:(h + 1) * NSA_DH, :] = (
            gts[3 * h:3 * h + 1, :] * oc[:, cols] + gts[3 * h + 1:3 * h + 2, :] * osl[:, cols]
            + gts[3 * h + 2:3 * h + 3, :] * ow[:, cols])


def _nsa_attention(qt, gt, kc, vct, okv, vt, expand_t, overlap, b, t, n_cmp, n_top, kt):
    nch = kc.shape[1]
    nq = t // Q_BLOCK
    n = b * t
    return pl.pallas_call(
        functools.partial(_nsa_body, n_cmp=n_cmp, n_top=n_top, kt=kt),
        grid=(b, NSA_GROUPS, nq),
        in_specs=[
            pl.BlockSpec((NSA_HPG * LANES, Q_BLOCK), lambda bi, g, i: (g, bi * nq + i)),
            pl.BlockSpec((GATE_ROWS, Q_BLOCK), lambda bi, g, i: (g, bi * nq + i)),
            pl.BlockSpec((1, nch, LANES), lambda bi, g, i: (bi, 0, 0)),
            pl.BlockSpec((1, LANES, nch), lambda bi, g, i: (bi, 0, 0)),
            pl.BlockSpec((1, t, LANES), lambda bi, g, i: (bi, 0, 2)),
            pl.BlockSpec((1, t, LANES), lambda bi, g, i: (bi, 0, 4)),
            pl.BlockSpec((LANES, t), lambda bi, g, i: (g, bi)),
            pl.BlockSpec((LANES, t), lambda bi, g, i: (NSA_GROUPS + g, bi)),
            pl.BlockSpec(expand_t.shape, lambda bi, g, i: (0, 0)),
            pl.BlockSpec(overlap.shape, lambda bi, g, i: (0, 0)),
        ],
        out_specs=pl.BlockSpec((NSA_HPG * NSA_DH, Q_BLOCK), lambda bi, g, i: (g, bi * nq + i)),
        out_shape=jax.ShapeDtypeStruct((NSA_W, n), F32),
        scratch_shapes=[pltpu.VMEM((2, V_ROWS, NSA_HPG * Q_BLOCK), F32),
                        pltpu.VMEM((4, kt, NSA_HPG * Q_BLOCK), F32)],
        compiler_params=_cparams(("parallel", "parallel", "arbitrary"), 56),
        name="nsa_attention",
    )(qt, gt, kc, vct, okv, okv, vt, vt, expand_t, overlap)


def _split_bf16(a):
    hi = a.astype(BF16)
    return hi, (a - hi.astype(F32)).astype(BF16)


def _unit_lower_inverses(lmats):
    c = lmats[0].shape[0]
    r = lax.broadcasted_iota(jnp.int32, (c, c), 0)
    col = lax.broadcasted_iota(jnp.int32, (c, c), 1)
    eye = jnp.where(r == col, 1.0, 0.0)
    xs = [eye - l for l in lmats]
    ps = []
    for l in lmats:
        l16 = l.astype(BF16)
        ps.append(_dot(l16, l16))
    steps = int(np.log2(c)) - 1
    for s in range(steps):
        last = s + 1 == steps
        for i in range(len(lmats)):
            rhs = ps[i].astype(BF16)
            if last:
                xs[i] = xs[i] + _dot(xs[i].astype(BF16), rhs)
            else:
                both = _dot(jnp.concatenate([xs[i], ps[i]], axis=0).astype(BF16), rhs)
                xs[i] = xs[i] + both[:c]
                ps[i] = both[c:]
    return xs


def _gdn_body(x_ref, z_ref, ab_ref, alog_ref, dtb_ref, og_ref, o_ref, s_sc, *, ct):
    nb = x_ref.shape[0]

    @pl.when(pl.program_id(0) == 0)
    def _():
        s_sc[...] = jnp.zeros(s_sc.shape, F32)

    ch = GDN_CHUNK
    r = lax.broadcasted_iota(jnp.int32, (ch, ch), 0)
    col = lax.broadcasted_iota(jnp.int32, (ch, ch), 1)
    incl = r >= col
    strict = r > col
    tril16 = jnp.concatenate([jnp.where(incl, 1.0, 0.0).astype(BF16)] * 3, axis=1)

    units = []
    for ci in range(ct // ch):
        rows = slice(ci * ch, (ci + 1) * ch)
        for bi in range(nb):
            ab = ab_ref[bi, rows, :]
            g_all = -jnp.exp(alog_ref[...]) * jax.nn.softplus(ab + dtb_ref[...])
            beta_all = jax.nn.sigmoid(ab)
            g_hi, g_lo = _split_bf16(g_all)
            g_lo2 = (g_all - g_hi.astype(F32) - g_lo.astype(F32)).astype(BF16)
            gc_all = _dot(tril16, jnp.concatenate([g_hi, g_lo, g_lo2], axis=0))
            gc_t = gc_all.T
            for h in range(GDN_HEADS):
                hs = slice(h * GDN_DH, (h + 1) * GDN_DH)
                q16 = x_ref[bi, rows, hs]
                k16 = x_ref[bi, rows, GDN_W + h * GDN_DH:GDN_W + (h + 1) * GDN_DH]
                qh, kh = q16.astype(F32), k16.astype(F32)
                vh = x_ref[bi, rows, 2 * GDN_W + h * GDN_DH:2 * GDN_W + (h + 1) * GDN_DH].astype(F32)
                gc = gc_all[:, h:h + 1]
                gr = gc_t[h:h + 1, :]
                g_last = gc_all[ch - 1:ch, h:h + 1]
                beta = beta_all[:, GDN_HEADS + h:GDN_HEADS + h + 1]
                eg = jnp.exp(gc)
                decay = jnp.where(incl, jnp.exp(jnp.minimum(gc - gr, 0.0)), 0.0)
                kb = kh * beta
                with_k = (_dot_nt(jnp.concatenate([kb.astype(BF16), q16], axis=0), k16)
                          * jnp.concatenate([decay, decay], axis=0))
                units.append(dict(
                    rows=rows, bi=bi, h=h,
                    lmat=jnp.where(strict, with_k[:ch], 0.0),
                    vb_kbg=jnp.concatenate([(vh * beta).astype(BF16), (kb * eg).astype(BF16)], axis=1),
                    qk=jnp.where(incl, with_k[ch:], 0.0).astype(BF16),
                    qg=(qh * eg).astype(BF16), kd_t=(kh * jnp.exp(g_last - gc)).T.astype(BF16),
                    gl=jnp.exp(g_last)))
    tinvs = _unit_lower_inverses([u["lmat"] for u in units])
    for u, tinv in zip(units, tinvs):
        u_w = _dot(tinv.astype(BF16), u["vb_kbg"])
        u["u"] = u_w[:, :GDN_DH]
        u["w_qg"] = jnp.concatenate([u_w[:, GDN_DH:].astype(BF16), u["qg"]], axis=0)
        u["kd_qk"] = jnp.concatenate([u["kd_t"], u["qk"]], axis=0)

    for u in units:
        bi, h, rows = u["bi"], u["h"], u["rows"]
        hs = slice(h * GDN_DH, (h + 1) * GDN_DH)
        s_old = s_sc[bi * GDN_HEADS + h]
        from_state = _dot(u["w_qg"], s_old.astype(BF16))
        v_new = (u["u"] - from_state[:ch]).astype(BF16)
        from_v = _dot(u["kd_qk"], v_new)
        s_sc[bi * GDN_HEADS + h] = s_old * u["gl"] + from_v[:GDN_DH]
        o = from_state[ch:] + from_v[GDN_DH:]
        on = o * lax.rsqrt(jnp.mean(o * o, axis=-1, keepdims=True) + EPS) * og_ref[...]
        zh = z_ref[bi, rows, hs].astype(F32)
        o_ref[bi, rows, hs] = (on * (zh * jax.nn.sigmoid(zh))).astype(BF16)


def _gdn(oqkv, oz, oab, alog, dtb, og, ct=128):
    b, t, w3 = oqkv.shape
    full = lambda a: pl.BlockSpec(a.shape, lambda c: (0,) * a.ndim)
    return pl.pallas_call(
        functools.partial(_gdn_body, ct=ct),
        grid=(t // ct,),
        in_specs=[
            pl.BlockSpec((b, ct, w3), lambda c: (0, c, 0)),
            pl.BlockSpec((b, ct, GDN_W), lambda c: (0, c, 0)),
            pl.BlockSpec((b, ct, LANES), lambda c: (0, c, 0)),
            full(alog), full(dtb), full(og),
        ],
        out_specs=pl.BlockSpec((b, ct, GDN_W), lambda c: (0, c, 0)),
        out_shape=jax.ShapeDtypeStruct((b, t, GDN_W), BF16),
        scratch_shapes=[pltpu.VMEM((b * GDN_HEADS, GDN_DH, GDN_DH), F32)],
        compiler_params=_cparams(("arbitrary",), 32),
        name="gdn",
    )(oqkv, oz, oab, alog, dtb, og)


def _outproj_body(ont_ref, og_ref, x_ref, ng_ref, wo_ref, fg_ref, wr_ref, br_ref, upper_ref,
                  x1_ref, h2_ref, gate_ref, route_ref, cnt_ref, cnt_sc):
    i = pl.program_id(0)
    tm = x_ref.shape[0]

    @pl.when(i == 0)
    def _():
        cnt_sc[...] = jnp.zeros(cnt_sc.shape, F32)

    a = ont_ref[...]
    a = (a * lax.rsqrt(jnp.mean(a * a, axis=0, keepdims=True) + EPS) * ng_ref[...]).astype(BF16)
    x1 = x_ref[...] + _dot_tn(a, wo_ref[0:NSA_W, :]) + _dot(og_ref[...], wo_ref[NSA_W:, :])
    x1_ref[...] = x1
    h2f = x1 * lax.rsqrt(jnp.mean(x1 * x1, axis=-1, keepdims=True) + EPS) * fg_ref[...]
    _store_pieces(h2_ref, h2f)
    h2 = h2f.astype(BF16)

    logits = (_dot(h2, wr_ref[...]) + br_ref[...]).T[:N_EXPERTS]
    erow = lax.broadcasted_iota(jnp.int32, (N_EXPERTS, tm), 0)
    onehot = jnp.zeros((N_EXPERTS, tm), F32)
    firsts, vals = [], []
    v = logits
    for k in range(TOP_K):
        mx = jnp.max(v, axis=0, keepdims=True)
        first = jnp.min(jnp.where(v == mx, erow, N_EXPERTS), axis=0, keepdims=True)
        hit = erow == first
        v = jnp.where(hit, -jnp.inf, v)
        onehot = jnp.where(hit, 1.0, onehot)
        firsts.append(first)
        vals.append(mx)
    vals = [jnp.exp(m - vals[0]) for m in vals]
    inv = 1.0 / (vals[0] + vals[1] + vals[2] + vals[3])
    gates_t = jnp.concatenate([m * inv for m in vals] + [jnp.zeros((LANES - TOP_K, tm), F32)], axis=0)
    gate_ref[...] = gates_t.T

    excl = cnt_sc[...] + _dot(onehot.astype(BF16), upper_ref[...])
    for k in range(TOP_K):
        route_ref[k:k + 1, :] = firsts[k]
        rank = jnp.sum(jnp.where(erow == firsts[k], excl, 0.0), axis=0, keepdims=True)
        route_ref[TOP_K + k:TOP_K + k + 1, :] = rank.astype(jnp.int32)
    cnt_sc[...] = cnt_sc[...] + jnp.sum(onehot, axis=1, keepdims=True)
    cnt_ref[...] = cnt_sc[...].astype(jnp.int32)


def _out_proj(o_nsa_t, o_gdn, x2, ng, wo, fg, wr, br, tm=512):
    upper = jnp.asarray(np.arange(tm)[:, None] < np.arange(tm)[None, :], BF16)
    n, d = x2.shape
    full = lambda a: pl.BlockSpec(a.shape, lambda i: (0,) * a.ndim)
    row = lambda w: pl.BlockSpec((tm, w), lambda i: (i, 0))
    return pl.pallas_call(
        _outproj_body,
        grid=(n // tm,),
        in_specs=[pl.BlockSpec((NSA_W, tm), lambda i: (0, i)), row(GDN_W), row(d), full(ng), full(wo), full(fg),
                  full(wr), full(br), full(upper)],
        out_specs=[row(d), pl.BlockSpec((d // 2 // SC_SUBROW, tm, SC_SUBROW), lambda i: (0, i, 0)),
                   row(LANES), pl.BlockSpec((2 * TOP_K, tm), lambda i: (0, i)),
                   pl.BlockSpec((N_EXPERTS, 1), lambda i: (0, 0))],
        out_shape=[jax.ShapeDtypeStruct((n, d), F32),
                   jax.ShapeDtypeStruct((d // 2 // SC_SUBROW, n, SC_SUBROW), jnp.int32),
                   jax.ShapeDtypeStruct((n, LANES), F32), jax.ShapeDtypeStruct((2 * TOP_K, n), jnp.int32),
                   jax.ShapeDtypeStruct((N_EXPERTS, 1), jnp.int32)],
        scratch_shapes=[pltpu.VMEM((N_EXPERTS, 1), F32)],
        compiler_params=_cparams(("arbitrary",), 48),
        name="out_proj_router",
    )(o_nsa_t, o_gdn, x2, ng, wo, fg, wr, br, upper)


def _dest_body(ps_ref, route_ref, o_ref, *, n_rows, pieces):
    expert = route_ref[0:TOP_K, :]
    start = jnp.zeros(expert.shape, jnp.int32)
    for e in range(N_EXPERTS):
        start = jnp.where(expert == e, ps_ref[e], start)
    dest = start + route_ref[TOP_K:2 * TOP_K, :]
    for k in range(TOP_K):
        for j in range(pieces):
            o_ref[k * pieces + j:k * pieces + j + 1, :] = dest[k:k + 1, :] + j * n_rows


def _dest_rows(pstarts, route, n_rows, pieces):
    n = route.shape[1]
    tn = min(2048, n)
    grid_spec = pltpu.PrefetchScalarGridSpec(
        num_scalar_prefetch=1,
        grid=(n // tn,),
        in_specs=[pl.BlockSpec((2 * TOP_K, tn), lambda i, ps: (0, i))],
        out_specs=pl.BlockSpec((TOP_K * pieces, tn), lambda i, ps: (0, i)),
    )
    return pl.pallas_call(
        functools.partial(_dest_body, n_rows=n_rows, pieces=pieces),
        grid_spec=grid_spec,
        out_shape=jax.ShapeDtypeStruct((TOP_K * pieces, n), jnp.int32),
        name="moe_dest_rows",
    )(pstarts, route)


def _expert_body(be_ref, fresh_ref, slot_ref, next_ref, xs_ref, wg_hbm, bg_ref, wu_hbm, bu_ref, wd_hbm, bd_ref,
                 y_ref, wf32, w16, sems):
    i = pl.program_id(0)
    used = i < be_ref[pl.num_programs(0)]
    hbm = (wg_hbm, wu_hbm, wd_hbm)

    def weight_copy(expert, slot, j):
        return pltpu.make_async_copy(hbm[j].at[expert], wf32.at[slot, j], sems.at[slot, j])

    @pl.when((i == 0) & used)
    def _():
        for j in range(3):
            weight_copy(be_ref[0], 0, j).start()

    @pl.when(used & (fresh_ref[i] == 1))
    def _():
        slot = slot_ref[i]
        for j in range(3):
            weight_copy(be_ref[i], slot, j).wait()
            w16[j] = wf32[slot, j].astype(BF16)

        @pl.when(next_ref[i] >= 0)
        def _():
            for j in range(3):
                weight_copy(next_ref[i], 1 - slot, j).start()

    @pl.when(jnp.logical_not(used))
    def _():
        y_ref[...] = jnp.zeros(y_ref.shape, y_ref.dtype)

    @pl.when(used)
    def _():
        x = _join_pieces(xs_ref).astype(BF16)
        gate = jnp.minimum(_dot(x, w16[0]) + bg_ref[0], SWIGLU_LIMIT)
        up = jnp.clip(_dot(x, w16[1]) + bu_ref[0], -SWIGLU_LIMIT, SWIGLU_LIMIT)
        glu = gate * jax.nn.sigmoid(gate * SWIGLU_ALPHA)
        _store_pieces(y_ref, _dot(((up + 1.0) * glu).astype(BF16), w16[2]) + bd_ref[0])


def _experts(blk_e, fresh, slot, nxt, xs, wg, bg, wu, bu, wd, bd):
    pieces, n_rows, sub = xs.shape
    d, de = wg.shape[1], wg.shape[2]
    assert d == de
    r = MOE_ROW_BLOCK
    bspec = lambda w: pl.BlockSpec((1, 1, w), lambda i, be, *_: (be[i], 0, 0))
    hbm = pl.BlockSpec(memory_space=pl.ANY)
    grid_spec = pltpu.PrefetchScalarGridSpec(
        num_scalar_prefetch=4,
        grid=(n_rows // r,),
        in_specs=[pl.BlockSpec((pieces, r, sub), lambda i, *_: (0, i, 0)),
                  hbm, bspec(de), hbm, bspec(de), hbm, bspec(d)],
        out_specs=pl.BlockSpec((pieces, r, sub), lambda i, *_: (0, i, 0)),
        scratch_shapes=[pltpu.VMEM((2, 3, d, de), F32), pltpu.VMEM((3, d, de), BF16),
                        pltpu.SemaphoreType.DMA((2, 3))],
    )
    return pl.pallas_call(
        _expert_body,
        grid_spec=grid_spec,
        out_shape=jax.ShapeDtypeStruct((pieces, n_rows, sub), jnp.int32),
        compiler_params=_cparams(("arbitrary",), 56),
        name="moe_experts",
    )(blk_e, fresh, slot, nxt, xs, wg, bg, wu, bu, wd, bd)


SC_WINDOW = 128
SC_SUBROW = 256


def _sc_mesh():
    return plsc.VectorSubcoreMesh(core_axis_name="c", subcore_axis_name="s")


def _sc_dispatch(h2, dest_rows, n_rows):
    n, d = h2.shape

    @functools.partial(pl.kernel, out_type=jax.ShapeDtypeStruct((n_rows, d), h2.dtype), mesh=_sc_mesh())
    def dispatch(x_hbm, *refs):
        idx_hbm, o_hbm = refs[:TOP_K], refs[TOP_K]

        def body(x_vmem, *idx_vmem):
            for iv in idx_vmem:
                pltpu.sync_copy(x_vmem, o_hbm.at[iv.at[0]])

        pltpu.emit_pipeline(
            body,
            grid=(n // SC_WINDOW,),
            in_specs=[pl.BlockSpec((SC_WINDOW, d), lambda i: (i, 0))]
                     + [pl.BlockSpec((1, SC_WINDOW), lambda i: (0, i))] * TOP_K,
            out_specs=[],
            core_axis_name=("c", "s"),
            dimension_semantics=(pltpu.PARALLEL,),
        )(x_hbm, *idx_hbm)

    return dispatch(h2, *dest_rows)


def _sc_gather(table, idx):
    _, d = table.shape
    m = idx.shape[1]

    @functools.partial(pl.kernel, out_type=jax.ShapeDtypeStruct((m, d), table.dtype), mesh=_sc_mesh())
    def gather(t_hbm, i_hbm, o_hbm):
        def body(i_vmem, o_vmem):
            pltpu.sync_copy(t_hbm.at[i_vmem.at[0]], o_vmem)

        pltpu.emit_pipeline(
            body,
            grid=(m // SC_WINDOW,),
            in_specs=[pl.BlockSpec((1, SC_WINDOW), lambda i: (0, i))],
            out_specs=[pl.BlockSpec((SC_WINDOW, d), lambda i: (i, 0))],
            core_axis_name=("c", "s"),
            dimension_semantics=(pltpu.PARALLEL,),
        )(i_hbm, o_hbm)

    return gather(table, idx)


def _combine_body(x1_ref, y_ref, gate_ref, o_ref):
    acc = x1_ref[...]
    for k in range(TOP_K):
        acc = acc + gate_ref[:, k:k + 1] * _join_pieces(y_ref.at[k])
    o_ref[...] = acc


def _combine(x1, y4, gates, tm=512):
    n, d = x1.shape
    pieces, sub = y4.shape[1], y4.shape[3]
    row = lambda w: pl.BlockSpec((tm, w), lambda i: (i, 0))
    return pl.pallas_call(
        _combine_body,
        grid=(n // tm,),
        in_specs=[row(d), pl.BlockSpec((TOP_K, pieces, tm, sub), lambda i: (0, 0, i, 0)), row(LANES)],
        out_specs=row(d),
        out_shape=jax.ShapeDtypeStruct((n, d), F32),
        compiler_params=_cparams(("parallel",), 48),
        name="moe_combine",
    )(x1, y4, gates)


def _pad_lanes(a, width=LANES):
    return jnp.pad(a, ((0, 0), (0, width - a.shape[1])))


def _layer(x, attn_norm_g, w_in, q_g, kc_g, ks_g, kw_g, ck_pos, ck_w1, ck_b1, ck_w2, ck_b2,
           cv_pos, cv_w1, cv_b1, cv_w2, cv_b2, nsa_out_g, conv_w, a_log, dt_bias, gdn_out_g, w_out,
           ffn_g, router_w, router_b, e_wg, e_bg, e_wu, e_bu, e_wd, e_bd):
    b, t, d = x.shape
    n = b * t
    x2 = x.reshape(n, d)

    o = np.cumsum([0, NSA_W] + [NSA_GROUPS * NSA_DH] * 6 + [3 * NSA_HEADS, 3 * GDN_W, GDN_W, GDN_HEADS, GDN_HEADS])
    wq_t = w_in[:, o[0]:o[1]].T.reshape(NSA_GROUPS, NSA_HPG, NSA_DH, d)
    zq = jnp.zeros((NSA_HPG, NSA_DH, d), F32)
    wq_t = jnp.stack([jnp.concatenate([wq_t[0], zq], axis=1), jnp.concatenate([zq, wq_t[1]], axis=1)])
    wq_t = wq_t.reshape(NSA_HEADS * LANES, d).astype(BF16)
    qg1 = q_g * (NSA_DH ** -0.5 * np.log2(np.e))
    zg = jnp.zeros((NSA_DH,), F32)
    qg_col = jnp.concatenate([jnp.tile(jnp.concatenate([qg1, zg]), NSA_HPG),
                              jnp.tile(jnp.concatenate([zg, qg1]), NSA_HPG)]).reshape(NSA_HEADS * LANES, 1)
    wkv = w_in[:, o[1]:o[7]].astype(BF16)
    ones = jnp.ones((LANES,), F32)
    kg = jnp.concatenate([ones, ones, ks_g, ks_g, ones, kw_g, kw_g, ones]).reshape(1, 6 * LANES)
    wv_t = jnp.concatenate([w_in[:, o[4]:o[5]], w_in[:, o[6]:o[7]]], axis=1).T.reshape(2 * NSA_GROUPS, NSA_DH, d)
    wv_t = jnp.pad(wv_t, ((0, 0), (0, LANES - NSA_DH), (0, 0))).reshape(2 * NSA_GROUPS * LANES, d).astype(BF16)
    vone = jnp.asarray((np.arange(2 * NSA_GROUPS * LANES) % LANES == NSA_DH).astype(np.float32)[:, None])
    wg_t = w_in[:, o[7]:o[8]].T.reshape(NSA_GROUPS, NSA_HPG * 3, d)
    wg_t = jnp.pad(wg_t, ((0, 0), (0, GATE_ROWS - NSA_HPG * 3), (0, 0))).reshape(NSA_GROUPS * GATE_ROWS, d)
    wg_t = wg_t.astype(BF16)
    wab = _pad_lanes(w_in[:, o[10]:o[12]]).astype(BF16)
    wqkv = w_in[:, o[8]:o[9]].astype(BF16)
    wz = w_in[:, o[9]:o[10]].astype(BF16)

    tm = min(512, t)
    oqt, okv, ovt, ogt, oqkv, oz, oab = _in_proj(x2, attn_norm_g.reshape(1, d), wq_t, wkv, wv_t, wg_t, wqkv, wz,
                                                 wab, qg_col, kg, vone, conv_w, t // tm, tm)

    nch = t // CMP_STRIDE
    n_cmp = (t - CMP_BLOCK) // CMP_STRIDE + 1
    half = CMP_STRIDE * NSA_DH
    xflat = okv[:, :2 * LANES].reshape(b, nch, CMP_STRIDE, 2, NSA_GROUPS, NSA_DH)
    xflat = xflat.transpose(0, 3, 4, 1, 2, 5).reshape(b, 2, NSA_GROUPS, nch, half)
    pos = jnp.stack([ck_pos, cv_pos]).reshape(2, 2, 1, half)
    w1 = jnp.stack([ck_w1, cv_w1]).reshape(2, 2, half, CMP_HIDDEN).astype(BF16)
    b1 = jnp.stack([ck_b1, cv_b1]).reshape(2, 1, CMP_HIDDEN)
    w2 = jnp.stack([ck_w2, cv_w2]).astype(BF16)
    b2 = jnp.stack([ck_b2, cv_b2]).reshape(2, 1, NSA_DH)
    w2t = jnp.stack([ck_w2.T, cv_w2.T]).astype(BF16)
    b2t = jnp.stack([ck_b2, cv_b2]).reshape(2, NSA_DH, 1)
    kc, vct = _compress(xflat, pos, w1, b1, w2, b2, w2t, b2t, kc_g.reshape(1, NSA_DH), n_cmp)

    n_slc = t // SLC_BLOCK
    n_top = min(SLC_TOPK, n_slc)
    nblk = max(n_slc, LANES)
    kt = min(256, t // 4)
    assert (t // kt) % 4 == 0
    ci = np.arange(nch)[None, :] * CMP_STRIDE
    sj = np.arange(nblk)[:, None] * SLC_BLOCK
    overlap = ((ci < sj + SLC_BLOCK) & (ci + CMP_BLOCK > sj) & (np.arange(nch)[None, :] < n_cmp)
               & (np.arange(nblk)[:, None] < n_slc))
    expand_t = (np.arange(t)[:, None] // SLC_BLOCK) == np.arange(nblk)[None, :]
    o_nsa_t = _nsa_attention(oqt, ogt, kc, vct, okv.reshape(b, t, -1), ovt, jnp.asarray(expand_t, BF16),
                             jnp.asarray(overlap, BF16), b, t, n_cmp, n_top, kt)

    alog_row = _pad_lanes(a_log.reshape(1, GDN_HEADS))
    dtb_row = _pad_lanes(dt_bias.reshape(1, GDN_HEADS))
    o_gdn = _gdn(oqkv.reshape(b, t, -1), oz.reshape(b, t, -1), oab.reshape(b, t, -1),
                 alog_row, dtb_row, gdn_out_g.reshape(1, GDN_DH))

    wr = _pad_lanes(router_w).astype(BF16)
    br = _pad_lanes(router_b.reshape(1, N_EXPERTS))
    x1, h2, gates, route, counts = _out_proj(
        o_nsa_t, o_gdn.reshape(n, GDN_W), x2, nsa_out_g.reshape(NSA_W, 1),
        w_out.astype(BF16), ffn_g.reshape(1, d), wr, br)

    r = MOE_ROW_BLOCK
    nk = n * TOP_K
    counts = counts[:, 0]
    pcounts = (counts + r - 1) // r * r
    pends = jnp.cumsum(pcounts)
    pstarts = pends - pcounts
    n_rows = (nk + r - 1) // r * r + N_EXPERTS * r
    n_blocks = n_rows // r
    blk_start = jnp.arange(n_blocks, dtype=jnp.int32)[:, None] * r
    blk_e = jnp.minimum(jnp.sum(pends[None, :] <= blk_start, axis=1), N_EXPERTS - 1).astype(jnp.int32)
    n_used = (pends[-1] // r).astype(jnp.int32)
    fresh = (jnp.arange(n_blocks) < n_used) & (blk_e != jnp.concatenate([jnp.full((1,), -1, jnp.int32), blk_e[:-1]]))
    slot = ((jnp.cumsum(fresh) - 1) % 2).astype(jnp.int32)
    eid = jnp.arange(N_EXPERTS, dtype=jnp.int32)
    later = jnp.where((eid[None, :] > eid[:, None]) & (pcounts[None, :] > 0), eid[None, :], N_EXPERTS)
    next_expert = jnp.min(later, axis=1)
    next_expert = jnp.where(next_expert < N_EXPERTS, next_expert, -1).astype(jnp.int32)
    nxt = next_expert[blk_e]
    blk_e = jnp.concatenate([blk_e, n_used[None]])
    pieces = d // 2 // SC_SUBROW
    dest_p = _dest_rows(pstarts.astype(jnp.int32), route, n_rows, pieces).reshape(TOP_K, pieces, n)
    xs = _sc_dispatch(h2.reshape(pieces * n, SC_SUBROW), [dest_p[k].reshape(1, pieces * n) for k in range(TOP_K)],
                      pieces * n_rows)
    ys = _experts(blk_e, fresh.astype(jnp.int32), slot, nxt, xs.reshape(pieces, n_rows, SC_SUBROW), e_wg, e_bg.reshape(N_EXPERTS, 1, -1), e_wu,
                  e_bu.reshape(N_EXPERTS, 1, -1), e_wd, e_bd.reshape(N_EXPERTS, 1, -1))
    y4 = _sc_gather(ys.reshape(pieces * n_rows, SC_SUBROW), dest_p.reshape(1, nk * pieces))
    return _combine(x1, y4.reshape(TOP_K, pieces, n, SC_SUBROW), gates).reshape(b, t, d)


def kernel(x, attn_norm_g, w_in, nsa_q_norm_g, nsa_kc_norm_g, nsa_ks_norm_g, nsa_kw_norm_g, cmp_k_pos, cmp_k_w1, cmp_k_b1, cmp_k_w2, cmp_k_b2, cmp_v_pos, cmp_v_w1, cmp_v_b1, cmp_v_w2, cmp_v_b2, nsa_out_norm_g, gdn_conv_w, gdn_a_log, gdn_dt_bias, gdn_out_norm_g, w_out, ffn_norm_g, router_w, router_b, exp_w_gate, exp_b_gate, exp_w_up, exp_b_up, exp_w_down, exp_b_down):
    params = (attn_norm_g, w_in, nsa_q_norm_g, nsa_kc_norm_g, nsa_ks_norm_g, nsa_kw_norm_g,
              cmp_k_pos, cmp_k_w1, cmp_k_b1, cmp_k_w2, cmp_k_b2, cmp_v_pos, cmp_v_w1, cmp_v_b1, cmp_v_w2, cmp_v_b2,
              nsa_out_norm_g, gdn_conv_w, gdn_a_log, gdn_dt_bias, gdn_out_norm_g, w_out, ffn_norm_g,
              router_w, router_b, exp_w_gate, exp_b_gate, exp_w_up, exp_b_up, exp_w_down, exp_b_down)
    for l in range(attn_norm_g.shape[0]):
        x = _layer(x, *(p[l] for p in params))
    return x
```

```python
import functools

import jax
import jax.numpy as jnp
import numpy as np
from jax import lax
from jax.experimental import pallas as pl
from jax.experimental.pallas import tpu as pltpu
from jax.experimental.pallas import tpu_sc as plsc

F32 = jnp.float32
BF16 = jnp.bfloat16

EPS = 1e-6
NEG = -1e30
MASKED = -2.0 ** 100

NSA_HEADS = 8
NSA_GROUPS = 2
NSA_HPG = 4
NSA_DH = 64
CMP_BLOCK = 32
CMP_STRIDE = 16
CMP_HIDDEN = 256
SLC_BLOCK = 64
SLC_TOPK = 16
WINDOW = 512
Q_BLOCK = 128
GDN_HEADS = 4
GDN_DH = 128
GDN_CONV = 4
GDN_CHUNK = 64
N_EXPERTS = 32
TOP_K = 4
SWIGLU_LIMIT = 7.0
SWIGLU_ALPHA = 1.702
MOE_ROW_BLOCK = 256

LANES = 128
GATE_ROWS = 16
NSA_W = NSA_HEADS * NSA_DH
GDN_W = GDN_HEADS * GDN_DH

_NT = (((1,), (1,)), ((), ()))
_TN = (((0,), (0,)), ((), ()))


def _cparams(sem, vmem_mb):
    return pltpu.CompilerParams(dimension_semantics=sem, vmem_limit_bytes=vmem_mb * 1024 * 1024)


def _dot(a, b):
    return jnp.dot(a, b, preferred_element_type=F32)


def _dot_nt(a, b):
    return lax.dot_general(a, b, _NT, preferred_element_type=F32)


def _dot_tn(a, b):
    return lax.dot_general(a, b, _TN, preferred_element_type=F32)


def _store_pieces(ref, val):
    half = val.shape[1] // 2
    hi = lax.bitcast_convert_type(val[:, :half].astype(BF16).astype(F32), jnp.uint32)
    lo = lax.bitcast_convert_type(val[:, half:].astype(BF16).astype(F32), jnp.uint32)
    words = lax.bitcast_convert_type(hi | (lo >> 16), jnp.int32)
    sub = ref.shape[2]
    for j in range(ref.shape[0]):
        ref[j] = words[:, j * sub:(j + 1) * sub]


def _join_pieces(ref):
    words = jnp.concatenate([ref[j] for j in range(ref.shape[0])], axis=1)
    words = lax.bitcast_convert_type(words, jnp.uint32)
    hi = lax.bitcast_convert_type(words & jnp.uint32(0xFFFF0000), F32)
    lo = lax.bitcast_convert_type(words << 16, F32)
    return jnp.concatenate([hi, lo], axis=1)


def _inproj_body(x_ref, g_ref, wqt_ref, wkv_ref, wvt_ref, wgt_ref, wqkv_ref, wz_ref, wab_ref, qg_ref, kg_ref,
                 vone_ref, cw_ref, oqt_ref, okv_ref, ovt_ref, ogt_ref, oqkv_ref, oz_ref, oab_ref, ybuf,
                 *, tiles_per_seq):
    x = x_ref[...]
    h = (x * lax.rsqrt(jnp.mean(x * x, axis=-1, keepdims=True) + EPS) * g_ref[...]).astype(BF16)
    tm = x.shape[0]

    yq = _dot_nt(wqt_ref[...], h)
    for s in range(NSA_HEADS):
        sl = slice(s * LANES, (s + 1) * LANES)
        ys = yq[sl, :]
        ms = jnp.sum(ys * ys, axis=0, keepdims=True) * (1.0 / NSA_DH)
        oqt_ref[sl, :] = (ys * lax.rsqrt(ms + EPS) * qg_ref[sl, :]).astype(BF16)

    ykv = _dot(h, wkv_ref[...])
    lane = lax.broadcasted_iota(jnp.int32, (tm, LANES), 1)
    low = lane < NSA_DH
    for s in range(6):
        sl = slice(s * LANES, (s + 1) * LANES)
        ys = ykv[:, sl]
        if s in (2, 4):
            y2 = ys * ys
            s0 = jnp.sum(jnp.where(low, y2, 0.0), axis=-1, keepdims=True)
            s1 = jnp.sum(jnp.where(low, 0.0, y2), axis=-1, keepdims=True)
            ms = jnp.where(low, s0, s1) * (1.0 / NSA_DH)
            ys = ys * lax.rsqrt(ms + EPS) * kg_ref[:, sl]
        okv_ref[:, sl] = ys.astype(BF16)

    ovt_ref[...] = (_dot_nt(wvt_ref[...], h) + vone_ref[...]).astype(BF16)
    ogt_ref[...] = _dot_nt(wgt_ref[...], h)
    oz_ref[...] = _dot(h, wz_ref[...]).astype(BF16)
    oab_ref[...] = _dot(h, wab_ref[...])

    halo = ybuf.shape[0] - tm
    first = pl.program_id(0) % tiles_per_seq == 0

    @pl.when(first)
    def _():
        ybuf[0:halo, :] = jnp.zeros((halo, ybuf.shape[1]), F32)

    @pl.when(jnp.logical_not(first))
    def _():
        ybuf[0:halo, :] = ybuf[tm:tm + halo, :]

    ybuf[halo:halo + tm, :] = _dot(h, wqkv_ref[...])
    taps = cw_ref.shape[0]
    y = cw_ref[0:1, :] * ybuf[pl.ds(halo - taps + 1, tm), :]
    for k in range(1, taps):
        y = y + cw_ref[k:k + 1, :] * ybuf[pl.ds(halo - taps + 1 + k, tm), :]
    hy = 0.5 * y
    y = hy + hy * jnp.tanh(hy)
    for s in range(3 * GDN_HEADS):
        sl = slice(s * GDN_DH, (s + 1) * GDN_DH)
        ys = y[:, sl]
        if s < 2 * GDN_HEADS:
            scale = GDN_DH ** -0.5 if s < GDN_HEADS else 1.0
            ys = ys * (lax.rsqrt(jnp.sum(ys * ys, axis=-1, keepdims=True) + EPS) * scale)
        oqkv_ref[:, sl] = ys.astype(BF16)


def _in_proj(x2, g, wqt, wkv, wvt, wgt, wqkv, wz, wab, qg, kg, vone, conv_w, tiles_per_seq, tm):
    n, d = x2.shape
    full = lambda a: pl.BlockSpec(a.shape, lambda i: (0,) * a.ndim)
    row = lambda w: pl.BlockSpec((tm, w), lambda i: (i, 0))
    colb = lambda r: pl.BlockSpec((r, tm), lambda i: (0, i))
    return pl.pallas_call(
        functools.partial(_inproj_body, tiles_per_seq=tiles_per_seq),
        grid=(n // tm,),
        in_specs=[row(d)] + [full(a) for a in (g, wqt, wkv, wvt, wgt, wqkv, wz, wab, qg, kg, vone, conv_w)],
        out_specs=[colb(wqt.shape[0]), row(wkv.shape[1]), colb(wvt.shape[0]), colb(wgt.shape[0]),
                   row(wqkv.shape[1]), row(wz.shape[1]), row(wab.shape[1])],
        out_shape=[jax.ShapeDtypeStruct((wqt.shape[0], n), BF16), jax.ShapeDtypeStruct((n, wkv.shape[1]), BF16),
                   jax.ShapeDtypeStruct((wvt.shape[0], n), BF16), jax.ShapeDtypeStruct((wgt.shape[0], n), F32),
                   jax.ShapeDtypeStruct((n, wqkv.shape[1]), BF16), jax.ShapeDtypeStruct((n, wz.shape[1]), BF16),
                   jax.ShapeDtypeStruct((n, wab.shape[1]), F32)],
        scratch_shapes=[pltpu.VMEM((tm + 8, wqkv.shape[1]), F32)],
        compiler_params=_cparams(("arbitrary",), 56),
        name="in_proj",
    )(x2, g, wqt, wkv, wvt, wgt, wqkv, wz, wab, qg, kg, vone, conv_w)


def _compress_body(x_ref, pos_ref, w1_ref, b1_ref, w2_ref, b2_ref, w2t_ref, b2t_ref, g_ref, ok_ref, ovt_ref,
                   *, n_cmp):
    is_key = pl.program_id(1) == 0
    nch = x_ref.shape[3]
    hids = []
    for grp in range(NSA_GROUPS):
        x = x_ref[0, 0, grp].astype(F32)
        xa = (x + pos_ref[0, 0]).astype(BF16)
        xb = (x + pos_ref[0, 1]).astype(BF16)
        a = _dot(xa, w1_ref[0, 0])
        b = _dot(xb, w1_ref[0, 1])
        b_next = pltpu.roll(b, nch - 1, 0)
        hids.append(jax.nn.gelu(a + b_next + b1_ref[0]).astype(BF16))

    @pl.when(is_key)
    def _():
        row = lax.broadcasted_iota(jnp.int32, (nch, NSA_DH), 0)
        outs = []
        for grp in range(NSA_GROUPS):
            out = _dot(hids[grp], w2_ref[0]) + b2_ref[0]
            out = out * lax.rsqrt(jnp.mean(out * out, axis=-1, keepdims=True) + EPS) * g_ref[...]
            outs.append(jnp.where(row < n_cmp, out, 0.0))
        ok_ref[0] = jnp.concatenate(outs, axis=-1).astype(BF16)

    @pl.when(jnp.logical_not(is_key))
    def _():
        col = lax.broadcasted_iota(jnp.int32, (NSA_DH, nch), 1)
        outs = []
        for grp in range(NSA_GROUPS):
            out = _dot_nt(w2t_ref[0], hids[grp]) + b2t_ref[0]
            outs.append(jnp.where(col < n_cmp, out, 0.0))
        ovt_ref[0] = jnp.concatenate(outs, axis=0).astype(BF16)


def _compress(xflat, pos, w1, b1, w2, b2, w2t, b2t, kc_g, n_cmp):
    b, _, _, nch, flat = xflat.shape
    return pl.pallas_call(
        functools.partial(_compress_body, n_cmp=n_cmp),
        grid=(b, 2),
        in_specs=[
            pl.BlockSpec((1, 1, NSA_GROUPS, nch, flat), lambda i, j: (i, j, 0, 0, 0)),
            pl.BlockSpec((1, 2, 1, flat), lambda i, j: (j, 0, 0, 0)),
            pl.BlockSpec((1, 2, flat, CMP_HIDDEN), lambda i, j: (j, 0, 0, 0)),
            pl.BlockSpec((1, 1, CMP_HIDDEN), lambda i, j: (j, 0, 0)),
            pl.BlockSpec((1, CMP_HIDDEN, NSA_DH), lambda i, j: (j, 0, 0)),
            pl.BlockSpec((1, 1, NSA_DH), lambda i, j: (j, 0, 0)),
            pl.BlockSpec((1, NSA_DH, CMP_HIDDEN), lambda i, j: (j, 0, 0)),
            pl.BlockSpec((1, NSA_DH, 1), lambda i, j: (j, 0, 0)),
            pl.BlockSpec((1, NSA_DH), lambda i, j: (0, 0)),
        ],
        out_specs=[pl.BlockSpec((1, nch, LANES), lambda i, j: (i, 0, 0)),
                   pl.BlockSpec((1, LANES, nch), lambda i, j: (i, 0, 0))],
        out_shape=[jax.ShapeDtypeStruct((b, nch, LANES), BF16), jax.ShapeDtypeStruct((b, LANES, nch), BF16)],
        compiler_params=_cparams(("parallel", "arbitrary"), 32),
        name="nsa_compress",
    )(xflat, pos, w1, b1, w2, b2, w2t, b2t, kc_g)


def _tile_heads(a):
    return jnp.concatenate([a] * NSA_HPG, axis=1)


def _nsa_body(qt_ref, gt_ref, kc_ref, vct_ref, ks_ref, kw_ref, vst_ref, vwt_ref, et_ref, ov_ref, o_ref, acc_sc, s_sc,
              *, n_cmp, n_top, kt):
    grp = pl.program_id(1)
    s0 = pl.program_id(2) * Q_BLOCK
    nch = kc_ref.shape[1]
    nblk = ov_ref.shape[0]

    qt = jnp.concatenate([qt_ref[h * LANES:(h + 1) * LANES, :] for h in range(NSA_HPG)], axis=1)
    t_row = s0 + lax.broadcasted_iota(jnp.int32, (1, Q_BLOCK), 1)

    cidx = lax.broadcasted_iota(jnp.int32, (nch, 1), 0)
    cvalid = (cidx * CMP_STRIDE + (CMP_BLOCK - 1) <= t_row) & (cidx < n_cmp)
    sc = _dot(kc_ref[0], qt) + _tile_heads(jnp.where(cvalid, 0.0, NEG))
    pc = jnp.exp2(sc - jnp.max(sc, axis=0, keepdims=True)).astype(BF16)
    stacked = jnp.concatenate([vct_ref[0], ov_ref[...], jnp.ones((8, nch), BF16)], axis=0)
    res = _dot(stacked, pc)
    inv = jnp.where(_tile_heads(t_row >= CMP_BLOCK - 1), 1.0 / jnp.maximum(res[LANES + nblk:LANES + nblk + 1], 1e-30),
                    0.0)
    oc = res[:LANES] * inv
    imp4 = res[LANES:LANES + nblk] * inv
    imp = (imp4[:, 0:Q_BLOCK] + imp4[:, Q_BLOCK:2 * Q_BLOCK] + imp4[:, 2 * Q_BLOCK:3 * Q_BLOCK]
           + imp4[:, 3 * Q_BLOCK:4 * Q_BLOCK])
    blk = lax.broadcasted_iota(jnp.int32, (nblk, Q_BLOCK), 0)
    cur = t_row // SLC_BLOCK
    imp = jnp.where(blk * SLC_BLOCK > t_row, NEG, imp)
    imp = jnp.where((blk == 0) | (blk == cur) | (blk == cur - 1), -NEG, imp)

    def pick_rounds(v, rounds):
        for _ in range(rounds):
            mx = jnp.max(v, axis=0, keepdims=True)
            first = jnp.min(jnp.where(v == mx, blk, nblk), axis=0, keepdims=True)
            v = jnp.where(blk == first, -jnp.inf, v)
        return v

    quarter = n_top // 4
    picked = pick_rounds(imp, quarter)

    wlen = WINDOW + Q_BLOCK
    w0 = pl.multiple_of(jnp.maximum(s0 - WINDOW, 0), Q_BLOCK)
    kpos = w0 + lax.broadcasted_iota(jnp.int32, (wlen, 1), 0)
    wbias = jnp.where((kpos <= t_row) & (kpos > t_row - WINDOW), 0.0, NEG)
    sw = _dot(kw_ref[0, pl.ds(w0, wlen), :], qt) + _tile_heads(wbias)
    picked = pick_rounds(picked, quarter)
    pw = jnp.exp2(sw - jnp.max(sw, axis=0, keepdims=True)).astype(BF16)
    picked = pick_rounds(picked, quarter)
    ow = _dot(vwt_ref[:, pl.ds(w0, wlen)], pw)
    ow = ow[:NSA_DH] / ow[NSA_DH:NSA_DH + 1]

    d0 = pl.multiple_of(s0, Q_BLOCK)
    dpos = s0 + lax.broadcasted_iota(jnp.int32, (Q_BLOCK, 1), 0)
    sd = _dot(ks_ref[0, pl.ds(d0, Q_BLOCK), :], qt) + _tile_heads(jnp.where(dpos <= t_row, 0.0, NEG))
    m_diag = jnp.max(sd, axis=0, keepdims=True)
    acc_sc[0] = _dot(vst_ref[:, pl.ds(d0, Q_BLOCK)], jnp.exp2(sd - m_diag).astype(BF16))
    acc_sc[1] = jnp.zeros(acc_sc.shape[1:], F32)
    picked = pick_rounds(picked, n_top - 3 * quarter)

    before = blk * SLC_BLOCK < s0
    selb = jnp.where((picked == -jnp.inf) & before, 0.0, MASKED).astype(BF16)
    rhs = jnp.concatenate([qt, _tile_heads(selb)], axis=0)

    last_tile = ks_ref.shape[1] // kt - 1

    def scores(idx, slot):
        k0 = pl.multiple_of(jnp.minimum(idx, last_tile) * kt, kt)
        lhs = jnp.concatenate([ks_ref[0, pl.ds(k0, kt), :], et_ref[pl.ds(k0, kt), :]], axis=1)
        s_sc[slot] = _dot(lhs, rhs)

    def update(idx, slot, m_old, acc_ref):
        k0 = pl.multiple_of(idx * kt, kt)
        m_new = jnp.maximum(m_old, jnp.max(s_sc[slot], axis=0, keepdims=True))
        p = jnp.exp2(s_sc[slot] - m_new).astype(BF16)
        acc_ref[...] = jnp.exp2(m_old - m_new) * acc_ref[...] + _dot(vst_ref[:, pl.ds(k0, kt)], p)
        return m_new

    def four_tiles(j, carry):
        m0, m1 = carry
        i = 4 * j
        scores(i + 2, 2)
        m0 = update(i, 0, m0, acc_sc.at[0])
        scores(i + 3, 3)
        m1 = update(i + 1, 1, m1, acc_sc.at[1])
        scores(i + 4, 0)
        m0 = update(i + 2, 2, m0, acc_sc.at[0])
        scores(i + 5, 1)
        m1 = update(i + 3, 3, m1, acc_sc.at[1])
        return m0, m1

    n_tiles = (s0 + kt - 1) // kt
    scores(0, 0)
    scores(1, 1)
    m0, m1 = lax.fori_loop(0, (n_tiles + 3) // 4, four_tiles,
                           (m_diag, jnp.full((1, NSA_HPG * Q_BLOCK), NEG, F32)))
    m_fin = jnp.maximum(m0, m1)
    acc = acc_sc[0] * jnp.exp2(m0 - m_fin) + acc_sc[1] * jnp.exp2(m1 - m_fin)
    osl = acc[:NSA_DH] / acc[NSA_DH:NSA_DH + 1]

    oc = jnp.where(grp == 0, oc[:NSA_DH], oc[NSA_DH:])
    gts = jax.nn.sigmoid(gt_ref[...])
    for h in range(NSA_HPG):
        cols = slice(h * Q_BLOCK, (h + 1) * Q_BLOCK)
        o_ref[h * NSA_DH:(h + 1) * NSA_DH, :] = (
            gts[3 * h:3 * h + 1, :] * oc[:, cols] + gts[3 * h + 1:3 * h + 2, :] * osl[:, cols]
            + gts[3 * h + 2:3 * h + 3, :] * ow[:, cols])


def _nsa_attention(qt, gt, kc, vct, okv, vt, expand_t, overlap, b, t, n_cmp, n_top, kt):
    nch = kc.shape[1]
    nq = t // Q_BLOCK
    n = b * t
    return pl.pallas_call(
        functools.partial(_nsa_body, n_cmp=n_cmp, n_top=n_top, kt=kt),
        grid=(b, NSA_GROUPS, nq),
        in_specs=[
            pl.BlockSpec((NSA_HPG * LANES, Q_BLOCK), lambda bi, g, i: (g, bi * nq + i)),
            pl.BlockSpec((GATE_ROWS, Q_BLOCK), lambda bi, g, i: (g, bi * nq + i)),
            pl.BlockSpec((1, nch, LANES), lambda bi, g, i: (bi, 0, 0)),
            pl.BlockSpec((1, LANES, nch), lambda bi, g, i: (bi, 0, 0)),
            pl.BlockSpec((1, t, LANES), lambda bi, g, i: (bi, 0, 2)),
            pl.BlockSpec((1, t, LANES), lambda bi, g, i: (bi, 0, 4)),
            pl.BlockSpec((LANES, t), lambda bi, g, i: (g, bi)),
            pl.BlockSpec((LANES, t), lambda bi, g, i: (NSA_GROUPS + g, bi)),
            pl.BlockSpec(expand_t.shape, lambda bi, g, i: (0, 0)),
            pl.BlockSpec(overlap.shape, lambda bi, g, i: (0, 0)),
        ],
        out_specs=pl.BlockSpec((NSA_HPG * NSA_DH, Q_BLOCK), lambda bi, g, i: (g, bi * nq + i)),
        out_shape=jax.ShapeDtypeStruct((NSA_W, n), F32),
        scratch_shapes=[pltpu.VMEM((2, LANES, NSA_HPG * Q_BLOCK), F32),
                        pltpu.VMEM((4, kt, NSA_HPG * Q_BLOCK), F32)],
        compiler_params=_cparams(("parallel", "parallel", "arbitrary"), 56),
        name="nsa_attention",
    )(qt, gt, kc, vct, okv, okv, vt, vt, expand_t, overlap)


def _split_bf16(a):
    hi = a.astype(BF16)
    return hi, (a - hi.astype(F32)).astype(BF16)


def _unit_lower_inverses(lmats):
    c = lmats[0].shape[0]
    r = lax.broadcasted_iota(jnp.int32, (c, c), 0)
    col = lax.broadcasted_iota(jnp.int32, (c, c), 1)
    eye = jnp.where(r == col, 1.0, 0.0)
    xs = [eye - l for l in lmats]
    ps = []
    for l in lmats:
        l16 = l.astype(BF16)
        ps.append(_dot(l16, l16))
    steps = int(np.log2(c)) - 1
    for s in range(steps):
        last = s + 1 == steps
        for i in range(len(lmats)):
            rhs = ps[i].astype(BF16)
            if last:
                xs[i] = xs[i] + _dot(xs[i].astype(BF16), rhs)
            else:
                both = _dot(jnp.concatenate([xs[i], ps[i]], axis=0).astype(BF16), rhs)
                xs[i] = xs[i] + both[:c]
                ps[i] = both[c:]
    return xs


def _gdn_body(x_ref, z_ref, ab_ref, alog_ref, dtb_ref, og_ref, o_ref, s_sc, *, ct):
    nb = x_ref.shape[0]

    @pl.when(pl.program_id(0) == 0)
    def _():
        s_sc[...] = jnp.zeros(s_sc.shape, F32)

    ch = GDN_CHUNK
    r = lax.broadcasted_iota(jnp.int32, (ch, ch), 0)
    col = lax.broadcasted_iota(jnp.int32, (ch, ch), 1)
    incl = r >= col
    strict = r > col
    tril16 = jnp.concatenate([jnp.where(incl, 1.0, 0.0).astype(BF16)] * 3, axis=1)

    units = []
    for ci in range(ct // ch):
        rows = slice(ci * ch, (ci + 1) * ch)
        for bi in range(nb):
            ab = ab_ref[bi, rows, :]
            g_all = -jnp.exp(alog_ref[...]) * jax.nn.softplus(ab + dtb_ref[...])
            beta_all = jax.nn.sigmoid(ab)
            g_hi, g_lo = _split_bf16(g_all)
            g_lo2 = (g_all - g_hi.astype(F32) - g_lo.astype(F32)).astype(BF16)
            gc_all = _dot(tril16, jnp.concatenate([g_hi, g_lo, g_lo2], axis=0))
            gc_t = gc_all.T
            for h in range(GDN_HEADS):
                hs = slice(h * GDN_DH, (h + 1) * GDN_DH)
                q16 = x_ref[bi, rows, hs]
                k16 = x_ref[bi, rows, GDN_W + h * GDN_DH:GDN_W + (h + 1) * GDN_DH]
                qh, kh = q16.astype(F32), k16.astype(F32)
                vh = x_ref[bi, rows, 2 * GDN_W + h * GDN_DH:2 * GDN_W + (h + 1) * GDN_DH].astype(F32)
                gc = gc_all[:, h:h + 1]
                gr = gc_t[h:h + 1, :]
                g_last = gc_all[ch - 1:ch, h:h + 1]
                beta = beta_all[:, GDN_HEADS + h:GDN_HEADS + h + 1]
                eg = jnp.exp(gc)
                decay = jnp.where(incl, jnp.exp(jnp.minimum(gc - gr, 0.0)), 0.0)
                kb = kh * beta
                with_k = (_dot_nt(jnp.concatenate([kb.astype(BF16), q16], axis=0), k16)
                          * jnp.concatenate([decay, decay], axis=0))
                units.append(dict(
                    rows=rows, bi=bi, h=h,
                    lmat=jnp.where(strict, with_k[:ch], 0.0),
                    vb_kbg=jnp.concatenate([(vh * beta).astype(BF16), (kb * eg).astype(BF16)], axis=1),
                    qk=jnp.where(incl, with_k[ch:], 0.0).astype(BF16),
                    qg=(qh * eg).astype(BF16), kd_t=(kh * jnp.exp(g_last - gc)).T.astype(BF16),
                    gl=jnp.exp(g_last)))
    tinvs = _unit_lower_inverses([u["lmat"] for u in units])
    for u, tinv in zip(units, tinvs):
        u_w = _dot(tinv.astype(BF16), u["vb_kbg"])
        u["u"] = u_w[:, :GDN_DH]
        u["w_qg"] = jnp.concatenate([u_w[:, GDN_DH:].astype(BF16), u["qg"]], axis=0)
        u["kd_qk"] = jnp.concatenate([u["kd_t"], u["qk"]], axis=0)

    for u in units:
        bi, h, rows = u["bi"], u["h"], u["rows"]
        hs = slice(h * GDN_DH, (h + 1) * GDN_DH)
        s_old = s_sc[bi * GDN_HEADS + h]
        from_state = _dot(u["w_qg"], s_old.astype(BF16))
        v_new = (u["u"] - from_state[:ch]).astype(BF16)
        from_v = _dot(u["kd_qk"], v_new)
        s_sc[bi * GDN_HEADS + h] = s_old * u["gl"] + from_v[:GDN_DH]
        o = from_state[ch:] + from_v[GDN_DH:]
        on = o * lax.rsqrt(jnp.mean(o * o, axis=-1, keepdims=True) + EPS) * og_ref[...]
        zh = z_ref[bi, rows, hs].astype(F32)
        o_ref[bi, rows, hs] = (on * (zh * jax.nn.sigmoid(zh))).astype(BF16)


def _gdn(oqkv, oz, oab, alog, dtb, og, ct=128):
    b, t, w3 = oqkv.shape
    full = lambda a: pl.BlockSpec(a.shape, lambda c: (0,) * a.ndim)
    return pl.pallas_call(
        functools.partial(_gdn_body, ct=ct),
        grid=(t // ct,),
        in_specs=[
            pl.BlockSpec((b, ct, w3), lambda c: (0, c, 0)),
            pl.BlockSpec((b, ct, GDN_W), lambda c: (0, c, 0)),
            pl.BlockSpec((b, ct, LANES), lambda c: (0, c, 0)),
            full(alog), full(dtb), full(og),
        ],
        out_specs=pl.BlockSpec((b, ct, GDN_W), lambda c: (0, c, 0)),
        out_shape=jax.ShapeDtypeStruct((b, t, GDN_W), BF16),
        scratch_shapes=[pltpu.VMEM((b * GDN_HEADS, GDN_DH, GDN_DH), F32)],
        compiler_params=_cparams(("arbitrary",), 32),
        name="gdn",
    )(oqkv, oz, oab, alog, dtb, og)


def _outproj_body(ont_ref, og_ref, x_ref, ng_ref, wo_ref, fg_ref, wr_ref, br_ref, upper_ref,
                  x1_ref, h2_ref, gate_ref, route_ref, cnt_ref, cnt_sc):
    i = pl.program_id(0)
    tm = x_ref.shape[0]

    @pl.when(i == 0)
    def _():
        cnt_sc[...] = jnp.zeros(cnt_sc.shape, F32)

    a = ont_ref[...]
    a = (a * lax.rsqrt(jnp.mean(a * a, axis=0, keepdims=True) + EPS) * ng_ref[...]).astype(BF16)
    x1 = x_ref[...] + _dot_tn(a, wo_ref[0:NSA_W, :]) + _dot(og_ref[...], wo_ref[NSA_W:, :])
    x1_ref[...] = x1
    h2f = x1 * lax.rsqrt(jnp.mean(x1 * x1, axis=-1, keepdims=True) + EPS) * fg_ref[...]
    _store_pieces(h2_ref, h2f)
    h2 = h2f.astype(BF16)

    logits = (_dot(h2, wr_ref[...]) + br_ref[...]).T[:N_EXPERTS]
    erow = lax.broadcasted_iota(jnp.int32, (N_EXPERTS, tm), 0)
    onehot = jnp.zeros((N_EXPERTS, tm), F32)
    firsts, vals = [], []
    v = logits
    for k in range(TOP_K):
        mx = jnp.max(v, axis=0, keepdims=True)
        first = jnp.min(jnp.where(v == mx, erow, N_EXPERTS), axis=0, keepdims=True)
        hit = erow == first
        v = jnp.where(hit, -jnp.inf, v)
        onehot = jnp.where(hit, 1.0, onehot)
        firsts.append(first)
        vals.append(mx)
    vals = [jnp.exp(m - vals[0]) for m in vals]
    inv = 1.0 / (vals[0] + vals[1] + vals[2] + vals[3])
    gates_t = jnp.concatenate([m * inv for m in vals] + [jnp.zeros((LANES - TOP_K, tm), F32)], axis=0)
    gate_ref[...] = gates_t.T

    excl = cnt_sc[...] + _dot(onehot.astype(BF16), upper_ref[...])
    for k in range(TOP_K):
        route_ref[k:k + 1, :] = firsts[k]
        rank = jnp.sum(jnp.where(erow == firsts[k], excl, 0.0), axis=0, keepdims=True)
        route_ref[TOP_K + k:TOP_K + k + 1, :] = rank.astype(jnp.int32)
    cnt_sc[...] = cnt_sc[...] + jnp.sum(onehot, axis=1, keepdims=True)
    cnt_ref[...] = cnt_sc[...].astype(jnp.int32)


def _out_proj(o_nsa_t, o_gdn, x2, ng, wo, fg, wr, br, tm=512):
    upper = jnp.asarray(np.arange(tm)[:, None] < np.arange(tm)[None, :], BF16)
    n, d = x2.shape
    full = lambda a: pl.BlockSpec(a.shape, lambda i: (0,) * a.ndim)
    row = lambda w: pl.BlockSpec((tm, w), lambda i: (i, 0))
    return pl.pallas_call(
        _outproj_body,
        grid=(n // tm,),
        in_specs=[pl.BlockSpec((NSA_W, tm), lambda i: (0, i)), row(GDN_W), row(d), full(ng), full(wo), full(fg),
                  full(wr), full(br), full(upper)],
        out_specs=[row(d), pl.BlockSpec((d // 2 // SC_SUBROW, tm, SC_SUBROW), lambda i: (0, i, 0)),
                   row(LANES), pl.BlockSpec((2 * TOP_K, tm), lambda i: (0, i)),
                   pl.BlockSpec((N_EXPERTS, 1), lambda i: (0, 0))],
        out_shape=[jax.ShapeDtypeStruct((n, d), F32),
                   jax.ShapeDtypeStruct((d // 2 // SC_SUBROW, n, SC_SUBROW), jnp.int32),
                   jax.ShapeDtypeStruct((n, LANES), F32), jax.ShapeDtypeStruct((2 * TOP_K, n), jnp.int32),
                   jax.ShapeDtypeStruct((N_EXPERTS, 1), jnp.int32)],
        scratch_shapes=[pltpu.VMEM((N_EXPERTS, 1), F32)],
        compiler_params=_cparams(("arbitrary",), 48),
        name="out_proj_router",
    )(o_nsa_t, o_gdn, x2, ng, wo, fg, wr, br, upper)


def _dest_body(ps_ref, route_ref, o_ref, *, n_rows, pieces):
    expert = route_ref[0:TOP_K, :]
    start = jnp.zeros(expert.shape, jnp.int32)
    for e in range(N_EXPERTS):
        start = jnp.where(expert == e, ps_ref[e], start)
    dest = start + route_ref[TOP_K:2 * TOP_K, :]
    for k in range(TOP_K):
        for j in range(pieces):
            o_ref[k * pieces + j:k * pieces + j + 1, :] = dest[k:k + 1, :] + j * n_rows


def _dest_rows(pstarts, route, n_rows, pieces):
    n = route.shape[1]
    tn = min(2048, n)
    grid_spec = pltpu.PrefetchScalarGridSpec(
        num_scalar_prefetch=1,
        grid=(n // tn,),
        in_specs=[pl.BlockSpec((2 * TOP_K, tn), lambda i, ps: (0, i))],
        out_specs=pl.BlockSpec((TOP_K * pieces, tn), lambda i, ps: (0, i)),
    )
    return pl.pallas_call(
        functools.partial(_dest_body, n_rows=n_rows, pieces=pieces),
        grid_spec=grid_spec,
        out_shape=jax.ShapeDtypeStruct((TOP_K * pieces, n), jnp.int32),
        name="moe_dest_rows",
    )(pstarts, route)


def _expert_body(be_ref, fresh_ref, slot_ref, next_ref, xs_ref, wg_hbm, bg_ref, wu_hbm, bu_ref, wd_hbm, bd_ref,
                 y_ref, wf32, w16, sems):
    i = pl.program_id(0)
    used = i < be_ref[pl.num_programs(0)]
    hbm = (wg_hbm, wu_hbm, wd_hbm)

    def weight_copy(expert, slot, j):
        return pltpu.make_async_copy(hbm[j].at[expert], wf32.at[slot, j], sems.at[slot, j])

    @pl.when((i == 0) & used)
    def _():
        for j in range(3):
            weight_copy(be_ref[0], 0, j).start()

    @pl.when(used & (fresh_ref[i] == 1))
    def _():
        slot = slot_ref[i]
        for j in range(3):
            weight_copy(be_ref[i], slot, j).wait()
            w16[j] = wf32[slot, j].astype(BF16)

        @pl.when(next_ref[i] >= 0)
        def _():
            for j in range(3):
                weight_copy(next_ref[i], 1 - slot, j).start()

    @pl.when(jnp.logical_not(used))
    def _():
        y_ref[...] = jnp.zeros(y_ref.shape, y_ref.dtype)

    @pl.when(used)
    def _():
        x = _join_pieces(xs_ref).astype(BF16)
        gate = jnp.minimum(_dot(x, w16[0]) + bg_ref[0], SWIGLU_LIMIT)
        up = jnp.clip(_dot(x, w16[1]) + bu_ref[0], -SWIGLU_LIMIT, SWIGLU_LIMIT)
        glu = gate * jax.nn.sigmoid(gate * SWIGLU_ALPHA)
        _store_pieces(y_ref, _dot(((up + 1.0) * glu).astype(BF16), w16[2]) + bd_ref[0])


def _experts(blk_e, fresh, slot, nxt, xs, wg, bg, wu, bu, wd, bd):
    pieces, n_rows, sub = xs.shape
    d, de = wg.shape[1], wg.shape[2]
    assert d == de
    r = MOE_ROW_BLOCK
    bspec = lambda w: pl.BlockSpec((1, 1, w), lambda i, be, *_: (be[i], 0, 0))
    hbm = pl.BlockSpec(memory_space=pl.ANY)
    grid_spec = pltpu.PrefetchScalarGridSpec(
        num_scalar_prefetch=4,
        grid=(n_rows // r,),
        in_specs=[pl.BlockSpec((pieces, r, sub), lambda i, *_: (0, i, 0)),
                  hbm, bspec(de), hbm, bspec(de), hbm, bspec(d)],
        out_specs=pl.BlockSpec((pieces, r, sub), lambda i, *_: (0, i, 0)),
        scratch_shapes=[pltpu.VMEM((2, 3, d, de), F32), pltpu.VMEM((3, d, de), BF16),
                        pltpu.SemaphoreType.DMA((2, 3))],
    )
    return pl.pallas_call(
        _expert_body,
        grid_spec=grid_spec,
        out_shape=jax.ShapeDtypeStruct((pieces, n_rows, sub), jnp.int32),
        compiler_params=_cparams(("arbitrary",), 56),
        name="moe_experts",
    )(blk_e, fresh, slot, nxt, xs, wg, bg, wu, bu, wd, bd)


SC_WINDOW = 128
SC_SUBROW = 256


def _sc_mesh():
    return plsc.VectorSubcoreMesh(core_axis_name="c", subcore_axis_name="s")


def _sc_dispatch(h2, dest_rows, n_rows):
    n, d = h2.shape

    @functools.partial(pl.kernel, out_type=jax.ShapeDtypeStruct((n_rows, d), h2.dtype), mesh=_sc_mesh())
    def dispatch(x_hbm, *refs):
        idx_hbm, o_hbm = refs[:TOP_K], refs[TOP_K]

        def body(x_vmem, *idx_vmem):
            for iv in idx_vmem:
                pltpu.sync_copy(x_vmem, o_hbm.at[iv.at[0]])

        pltpu.emit_pipeline(
            body,
            grid=(n // SC_WINDOW,),
            in_specs=[pl.BlockSpec((SC_WINDOW, d), lambda i: (i, 0))]
                     + [pl.BlockSpec((1, SC_WINDOW), lambda i: (0, i))] * TOP_K,
            out_specs=[],
            core_axis_name=("c", "s"),
            dimension_semantics=(pltpu.PARALLEL,),
        )(x_hbm, *idx_hbm)

    return dispatch(h2, *dest_rows)


def _sc_gather(table, idx):
    _, d = table.shape
    m = idx.shape[1]

    @functools.partial(pl.kernel, out_type=jax.ShapeDtypeStruct((m, d), table.dtype), mesh=_sc_mesh())
    def gather(t_hbm, i_hbm, o_hbm):
        def body(i_vmem, o_vmem):
            pltpu.sync_copy(t_hbm.at[i_vmem.at[0]], o_vmem)

        pltpu.emit_pipeline(
            body,
            grid=(m // SC_WINDOW,),
            in_specs=[pl.BlockSpec((1, SC_WINDOW), lambda i: (0, i))],
            out_specs=[pl.BlockSpec((SC_WINDOW, d), lambda i: (i, 0))],
            core_axis_name=("c", "s"),
            dimension_semantics=(pltpu.PARALLEL,),
        )(i_hbm, o_hbm)

    return gather(table, idx)


def _combine_body(x1_ref, y_ref, gate_ref, o_ref):
    acc = x1_ref[...]
    for k in range(TOP_K):
        acc = acc + gate_ref[:, k:k + 1] * _join_pieces(y_ref.at[k])
    o_ref[...] = acc


def _combine(x1, y4, gates, tm=512):
    n, d = x1.shape
    pieces, sub = y4.shape[1], y4.shape[3]
    row = lambda w: pl.BlockSpec((tm, w), lambda i: (i, 0))
    return pl.pallas_call(
        _combine_body,
        grid=(n // tm,),
        in_specs=[row(d), pl.BlockSpec((TOP_K, pieces, tm, sub), lambda i: (0, 0, i, 0)), row(LANES)],
        out_specs=row(d),
        out_shape=jax.ShapeDtypeStruct((n, d), F32),
        compiler_params=_cparams(("parallel",), 48),
        name="moe_combine",
    )(x1, y4, gates)


def _pad_lanes(a, width=LANES):
    return jnp.pad(a, ((0, 0), (0, width - a.shape[1])))


def _layer(x, attn_norm_g, w_in, q_g, kc_g, ks_g, kw_g, ck_pos, ck_w1, ck_b1, ck_w2, ck_b2,
           cv_pos, cv_w1, cv_b1, cv_w2, cv_b2, nsa_out_g, conv_w, a_log, dt_bias, gdn_out_g, w_out,
           ffn_g, router_w, router_b, e_wg, e_bg, e_wu, e_bu, e_wd, e_bd):
    b, t, d = x.shape
    n = b * t
    x2 = x.reshape(n, d)

    o = np.cumsum([0, NSA_W] + [NSA_GROUPS * NSA_DH] * 6 + [3 * NSA_HEADS, 3 * GDN_W, GDN_W, GDN_HEADS, GDN_HEADS])
    wq_t = w_in[:, o[0]:o[1]].T.reshape(NSA_GROUPS, NSA_HPG, NSA_DH, d)
    zq = jnp.zeros((NSA_HPG, NSA_DH, d), F32)
    wq_t = jnp.stack([jnp.concatenate([wq_t[0], zq], axis=1), jnp.concatenate([zq, wq_t[1]], axis=1)])
    wq_t = wq_t.reshape(NSA_HEADS * LANES, d).astype(BF16)
    qg1 = q_g * (NSA_DH ** -0.5 * np.log2(np.e))
    zg = jnp.zeros((NSA_DH,), F32)
    qg_col = jnp.concatenate([jnp.tile(jnp.concatenate([qg1, zg]), NSA_HPG),
                              jnp.tile(jnp.concatenate([zg, qg1]), NSA_HPG)]).reshape(NSA_HEADS * LANES, 1)
    wkv = w_in[:, o[1]:o[7]].astype(BF16)
    ones = jnp.ones((LANES,), F32)
    kg = jnp.concatenate([ones, ones, ks_g, ks_g, ones, kw_g, kw_g, ones]).reshape(1, 6 * LANES)
    wv_t = jnp.concatenate([w_in[:, o[4]:o[5]], w_in[:, o[6]:o[7]]], axis=1).T.reshape(2 * NSA_GROUPS, NSA_DH, d)
    wv_t = jnp.pad(wv_t, ((0, 0), (0, LANES - NSA_DH), (0, 0))).reshape(2 * NSA_GROUPS * LANES, d).astype(BF16)
    vone = jnp.asarray((np.arange(2 * NSA_GROUPS * LANES) % LANES == NSA_DH).astype(np.float32)[:, None])
    wg_t = w_in[:, o[7]:o[8]].T.reshape(NSA_GROUPS, NSA_HPG * 3, d)
    wg_t = jnp.pad(wg_t, ((0, 0), (0, GATE_ROWS - NSA_HPG * 3), (0, 0))).reshape(NSA_GROUPS * GATE_ROWS, d)
    wg_t = wg_t.astype(BF16)
    wab = _pad_lanes(w_in[:, o[10]:o[12]]).astype(BF16)
    wqkv = w_in[:, o[8]:o[9]].astype(BF16)
    wz = w_in[:, o[9]:o[10]].astype(BF16)

    tm = min(512, t)
    oqt, okv, ovt, ogt, oqkv, oz, oab = _in_proj(x2, attn_norm_g.reshape(1, d), wq_t, wkv, wv_t, wg_t, wqkv, wz,
                                                 wab, qg_col, kg, vone, conv_w, t // tm, tm)

    nch = t // CMP_STRIDE
    n_cmp = (t - CMP_BLOCK) // CMP_STRIDE + 1
    half = CMP_STRIDE * NSA_DH
    xflat = okv[:, :2 * LANES].reshape(b, nch, CMP_STRIDE, 2, NSA_GROUPS, NSA_DH)
    xflat = xflat.transpose(0, 3, 4, 1, 2, 5).reshape(b, 2, NSA_GROUPS, nch, half)
    pos = jnp.stack([ck_pos, cv_pos]).reshape(2, 2, 1, half)
    w1 = jnp.stack([ck_w1, cv_w1]).reshape(2, 2, half, CMP_HIDDEN).astype(BF16)
    b1 = jnp.stack([ck_b1, cv_b1]).reshape(2, 1, CMP_HIDDEN)
    w2 = jnp.stack([ck_w2, cv_w2]).astype(BF16)
    b2 = jnp.stack([ck_b2, cv_b2]).reshape(2, 1, NSA_DH)
    w2t = jnp.stack([ck_w2.T, cv_w2.T]).astype(BF16)
    b2t = jnp.stack([ck_b2, cv_b2]).reshape(2, NSA_DH, 1)
    kc, vct = _compress(xflat, pos, w1, b1, w2, b2, w2t, b2t, kc_g.reshape(1, NSA_DH), n_cmp)

    n_slc = t // SLC_BLOCK
    n_top = min(SLC_TOPK, n_slc)
    nblk = max(n_slc, LANES)
    kt = min(256, t // 4)
    assert (t // kt) % 4 == 0
    ci = np.arange(nch)[None, :] * CMP_STRIDE
    sj = np.arange(nblk)[:, None] * SLC_BLOCK
    overlap = ((ci < sj + SLC_BLOCK) & (ci + CMP_BLOCK > sj) & (np.arange(nch)[None, :] < n_cmp)
               & (np.arange(nblk)[:, None] < n_slc))
    expand_t = (np.arange(t)[:, None] // SLC_BLOCK) == np.arange(nblk)[None, :]
    o_nsa_t = _nsa_attention(oqt, ogt, kc, vct, okv.reshape(b, t, -1), ovt, jnp.asarray(expand_t, BF16),
                             jnp.asarray(overlap, BF16), b, t, n_cmp, n_top, kt)

    alog_row = _pad_lanes(a_log.reshape(1, GDN_HEADS))
    dtb_row = _pad_lanes(dt_bias.reshape(1, GDN_HEADS))
    o_gdn = _gdn(oqkv.reshape(b, t, -1), oz.reshape(b, t, -1), oab.reshape(b, t, -1),
                 alog_row, dtb_row, gdn_out_g.reshape(1, GDN_DH))

    wr = _pad_lanes(router_w).astype(BF16)
    br = _pad_lanes(router_b.reshape(1, N_EXPERTS))
    x1, h2, gates, route, counts = _out_proj(
        o_nsa_t, o_gdn.reshape(n, GDN_W), x2, nsa_out_g.reshape(NSA_W, 1),
        w_out.astype(BF16), ffn_g.reshape(1, d), wr, br)

    r = MOE_ROW_BLOCK
    nk = n * TOP_K
    counts = counts[:, 0]
    pcounts = (counts + r - 1) // r * r
    pends = jnp.cumsum(pcounts)
    pstarts = pends - pcounts
    n_rows = (nk + r - 1) // r * r + N_EXPERTS * r
    n_blocks = n_rows // r
    blk_start = jnp.arange(n_blocks, dtype=jnp.int32)[:, None] * r
    blk_e = jnp.minimum(jnp.sum(pends[None, :] <= blk_start, axis=1), N_EXPERTS - 1).astype(jnp.int32)
    n_used = (pends[-1] // r).astype(jnp.int32)
    fresh = (jnp.arange(n_blocks) < n_used) & (blk_e != jnp.concatenate([jnp.full((1,), -1, jnp.int32), blk_e[:-1]]))
    slot = ((jnp.cumsum(fresh) - 1) % 2).astype(jnp.int32)
    eid = jnp.arange(N_EXPERTS, dtype=jnp.int32)
    later = jnp.where((eid[None, :] > eid[:, None]) & (pcounts[None, :] > 0), eid[None, :], N_EXPERTS)
    next_expert = jnp.min(later, axis=1)
    next_expert = jnp.where(next_expert < N_EXPERTS, next_expert, -1).astype(jnp.int32)
    nxt = next_expert[blk_e]
    blk_e = jnp.concatenate([blk_e, n_used[None]])
    pieces = d // 2 // SC_SUBROW
    dest_p = _dest_rows(pstarts.astype(jnp.int32), route, n_rows, pieces).reshape(TOP_K, pieces, n)
    xs = _sc_dispatch(h2.reshape(pieces * n, SC_SUBROW), [dest_p[k].reshape(1, pieces * n) for k in range(TOP_K)],
                      pieces * n_rows)
    ys = _experts(blk_e, fresh.astype(jnp.int32), slot, nxt, xs.reshape(pieces, n_rows, SC_SUBROW), e_wg, e_bg.reshape(N_EXPERTS, 1, -1), e_wu,
                  e_bu.reshape(N_EXPERTS, 1, -1), e_wd, e_bd.reshape(N_EXPERTS, 1, -1))
    y4 = _sc_gather(ys.reshape(pieces * n_rows, SC_SUBROW), dest_p.reshape(1, nk * pieces))
    return _combine(x1, y4.reshape(TOP_K, pieces, n, SC_SUBROW), gates).reshape(b, t, d)


def kernel(x, attn_norm_g, w_in, nsa_q_norm_g, nsa_kc_norm_g, nsa_ks_norm_g, nsa_kw_norm_g, cmp_k_pos, cmp_k_w1, cmp_k_b1, cmp_k_w2, cmp_k_b2, cmp_v_pos, cmp_v_w1, cmp_v_b1, cmp_v_w2, cmp_v_b2, nsa_out_norm_g, gdn_conv_w, gdn_a_log, gdn_dt_bias, gdn_out_norm_g, w_out, ffn_norm_g, router_w, router_b, exp_w_gate, exp_b_gate, exp_w_up, exp_b_up, exp_w_down, exp_b_down):
    params = (attn_norm_g, w_in, nsa_q_norm_g, nsa_kc_norm_g, nsa_ks_norm_g, nsa_kw_norm_g,
              cmp_k_pos, cmp_k_w1, cmp_k_b1, cmp_k_w2, cmp_k_b2, cmp_v_pos, cmp_v_w1, cmp_v_b1, cmp_v_w2, cmp_v_b2,
              nsa_out_norm_g, gdn_conv_w, gdn_a_log, gdn_dt_bias, gdn_out_norm_g, w_out, ffn_norm_g,
              router_w, router_b, exp_w_gate, exp_b_gate, exp_w_up, exp_b_up, exp_w_down, exp_b_down)
    for l in range(attn_norm_g.shape[0]):
        x = _layer(x, *(p[l] for p in params))
    return x
```

```python
import functools

import jax
import jax.numpy as jnp
import numpy as np
from jax import lax
from jax.experimental import pallas as pl
from jax.experimental.pallas import tpu as pltpu
from jax.experimental.pallas import tpu_sc as plsc

F32 = jnp.float32
BF16 = jnp.bfloat16

EPS = 1e-6
NEG = -1e30
MASKED = -2.0 ** 100

NSA_HEADS = 8
NSA_GROUPS = 2
NSA_HPG = 4
NSA_DH = 64
CMP_BLOCK = 32
CMP_STRIDE = 16
CMP_HIDDEN = 256
SLC_BLOCK = 64
SLC_TOPK = 16
WINDOW = 512
Q_BLOCK = 128
GDN_HEADS = 4
GDN_DH = 128
GDN_CONV = 4
GDN_CHUNK = 64
N_EXPERTS = 32
TOP_K = 4
SWIGLU_LIMIT = 7.0
SWIGLU_ALPHA = 1.702
MOE_ROW_BLOCK = 256

LANES = 128
GATE_ROWS = 16
NSA_W = NSA_HEADS * NSA_DH
GDN_W = GDN_HEADS * GDN_DH

_NT = (((1,), (1,)), ((), ()))
_TN = (((0,), (0,)), ((), ()))


def _cparams(sem, vmem_mb):
    return pltpu.CompilerParams(dimension_semantics=sem, vmem_limit_bytes=vmem_mb * 1024 * 1024)


def _dot(a, b):
    return jnp.dot(a, b, preferred_element_type=F32)


def _dot_nt(a, b):
    return lax.dot_general(a, b, _NT, preferred_element_type=F32)


def _dot_tn(a, b):
    return lax.dot_general(a, b, _TN, preferred_element_type=F32)


def _store_pieces(ref, val):
    half = val.shape[1] // 2
    hi = lax.bitcast_convert_type(val[:, :half].astype(BF16).astype(F32), jnp.uint32)
    lo = lax.bitcast_convert_type(val[:, half:].astype(BF16).astype(F32), jnp.uint32)
    words = lax.bitcast_convert_type(hi | (lo >> 16), jnp.int32)
    sub = ref.shape[2]
    for j in range(ref.shape[0]):
        ref[j] = words[:, j * sub:(j + 1) * sub]


def _join_pieces(ref):
    words = jnp.concatenate([ref[j] for j in range(ref.shape[0])], axis=1)
    words = lax.bitcast_convert_type(words, jnp.uint32)
    hi = lax.bitcast_convert_type(words & jnp.uint32(0xFFFF0000), F32)
    lo = lax.bitcast_convert_type(words << 16, F32)
    return jnp.concatenate([hi, lo], axis=1)


def _inproj_body(x_ref, g_ref, wqt_ref, wkv_ref, wvt_ref, wgt_ref, wqkv_ref, wz_ref, wab_ref, qg_ref, kg_ref,
                 vone_ref, cw_ref, oqt_ref, okv_ref, ovt_ref, ogt_ref, oqkv_ref, oz_ref, oab_ref, ybuf,
                 *, tiles_per_seq):
    x = x_ref[...]
    h = (x * lax.rsqrt(jnp.mean(x * x, axis=-1, keepdims=True) + EPS) * g_ref[...]).astype(BF16)
    tm = x.shape[0]

    yq = _dot_nt(wqt_ref[...], h)
    for s in range(NSA_HEADS):
        sl = slice(s * LANES, (s + 1) * LANES)
        ys = yq[sl, :]
        ms = jnp.sum(ys * ys, axis=0, keepdims=True) * (1.0 / NSA_DH)
        oqt_ref[sl, :] = (ys * lax.rsqrt(ms + EPS) * qg_ref[sl, :]).astype(BF16)

    ykv = _dot(h, wkv_ref[...])
    lane = lax.broadcasted_iota(jnp.int32, (tm, LANES), 1)
    low = lane < NSA_DH
    for s in range(6):
        sl = slice(s * LANES, (s + 1) * LANES)
        ys = ykv[:, sl]
        if s in (2, 4):
            y2 = ys * ys
            s0 = jnp.sum(jnp.where(low, y2, 0.0), axis=-1, keepdims=True)
            s1 = jnp.sum(jnp.where(low, 0.0, y2), axis=-1, keepdims=True)
            ms = jnp.where(low, s0, s1) * (1.0 / NSA_DH)
            ys = ys * lax.rsqrt(ms + EPS) * kg_ref[:, sl]
        okv_ref[:, sl] = ys.astype(BF16)

    ovt_ref[...] = (_dot_nt(wvt_ref[...], h) + vone_ref[...]).astype(BF16)
    ogt_ref[...] = _dot_nt(wgt_ref[...], h)
    oz_ref[...] = _dot(h, wz_ref[...]).astype(BF16)
    oab_ref[...] = _dot(h, wab_ref[...])

    halo = ybuf.shape[0] - tm
    first = pl.program_id(0) % tiles_per_seq == 0

    @pl.when(first)
    def _():
        ybuf[0:halo, :] = jnp.zeros((halo, ybuf.shape[1]), F32)

    @pl.when(jnp.logical_not(first))
    def _():
        ybuf[0:halo, :] = ybuf[tm:tm + halo, :]

    ybuf[halo:halo + tm, :] = _dot(h, wqkv_ref[...])
    taps = cw_ref.shape[0]
    y = cw_ref[0:1, :] * ybuf[pl.ds(halo - taps + 1, tm), :]
    for k in range(1, taps):
        y = y + cw_ref[k:k + 1, :] * ybuf[pl.ds(halo - taps + 1 + k, tm), :]
    hy = 0.5 * y
    y = hy + hy * jnp.tanh(hy)
    for s in range(3 * GDN_HEADS):
        sl = slice(s * GDN_DH, (s + 1) * GDN_DH)
        ys = y[:, sl]
        if s < 2 * GDN_HEADS:
            scale = GDN_DH ** -0.5 if s < GDN_HEADS else 1.0
            ys = ys * (lax.rsqrt(jnp.sum(ys * ys, axis=-1, keepdims=True) + EPS) * scale)
        oqkv_ref[:, sl] = ys.astype(BF16)


def _in_proj(x2, g, wqt, wkv, wvt, wgt, wqkv, wz, wab, qg, kg, vone, conv_w, tiles_per_seq, tm):
    n, d = x2.shape
    full = lambda a: pl.BlockSpec(a.shape, lambda i: (0,) * a.ndim)
    row = lambda w: pl.BlockSpec((tm, w), lambda i: (i, 0))
    colb = lambda r: pl.BlockSpec((r, tm), lambda i: (0, i))
    return pl.pallas_call(
        functools.partial(_inproj_body, tiles_per_seq=tiles_per_seq),
        grid=(n // tm,),
        in_specs=[row(d)] + [full(a) for a in (g, wqt, wkv, wvt, wgt, wqkv, wz, wab, qg, kg, vone, conv_w)],
        out_specs=[colb(wqt.shape[0]), row(wkv.shape[1]), colb(wvt.shape[0]), colb(wgt.shape[0]),
                   row(wqkv.shape[1]), row(wz.shape[1]), row(wab.shape[1])],
        out_shape=[jax.ShapeDtypeStruct((wqt.shape[0], n), BF16), jax.ShapeDtypeStruct((n, wkv.shape[1]), BF16),
                   jax.ShapeDtypeStruct((wvt.shape[0], n), BF16), jax.ShapeDtypeStruct((wgt.shape[0], n), F32),
                   jax.ShapeDtypeStruct((n, wqkv.shape[1]), BF16), jax.ShapeDtypeStruct((n, wz.shape[1]), BF16),
                   jax.ShapeDtypeStruct((n, wab.shape[1]), F32)],
        scratch_shapes=[pltpu.VMEM((tm + 8, wqkv.shape[1]), F32)],
        compiler_params=_cparams(("arbitrary",), 56),
        name="in_proj",
    )(x2, g, wqt, wkv, wvt, wgt, wqkv, wz, wab, qg, kg, vone, conv_w)


def _compress_body(x_ref, pos_ref, w1_ref, b1_ref, w2_ref, b2_ref, w2t_ref, b2t_ref, g_ref, ok_ref, ovt_ref,
                   *, n_cmp):
    is_key = pl.program_id(1) == 0
    nch = x_ref.shape[3]
    hids = []
    for grp in range(NSA_GROUPS):
        x = x_ref[0, 0, grp].astype(F32)
        xa = (x + pos_ref[0, 0]).astype(BF16)
        xb = (x + pos_ref[0, 1]).astype(BF16)
        a = _dot(xa, w1_ref[0, 0])
        b = _dot(xb, w1_ref[0, 1])
        b_next = pltpu.roll(b, nch - 1, 0)
        hids.append(jax.nn.gelu(a + b_next + b1_ref[0]).astype(BF16))

    @pl.when(is_key)
    def _():
        row = lax.broadcasted_iota(jnp.int32, (nch, NSA_DH), 0)
        outs = []
        for grp in range(NSA_GROUPS):
            out = _dot(hids[grp], w2_ref[0]) + b2_ref[0]
            out = out * lax.rsqrt(jnp.mean(out * out, axis=-1, keepdims=True) + EPS) * g_ref[...]
            outs.append(jnp.where(row < n_cmp, out, 0.0))
        ok_ref[0] = jnp.concatenate(outs, axis=-1).astype(BF16)

    @pl.when(jnp.logical_not(is_key))
    def _():
        col = lax.broadcasted_iota(jnp.int32, (NSA_DH, nch), 1)
        outs = []
        for grp in range(NSA_GROUPS):
            out = _dot_nt(w2t_ref[0], hids[grp]) + b2t_ref[0]
            outs.append(jnp.where(col < n_cmp, out, 0.0))
        ovt_ref[0] = jnp.concatenate(outs, axis=0).astype(BF16)


def _compress(xflat, pos, w1, b1, w2, b2, w2t, b2t, kc_g, n_cmp):
    b, _, _, nch, flat = xflat.shape
    return pl.pallas_call(
        functools.partial(_compress_body, n_cmp=n_cmp),
        grid=(b, 2),
        in_specs=[
            pl.BlockSpec((1, 1, NSA_GROUPS, nch, flat), lambda i, j: (i, j, 0, 0, 0)),
            pl.BlockSpec((1, 2, 1, flat), lambda i, j: (j, 0, 0, 0)),
            pl.BlockSpec((1, 2, flat, CMP_HIDDEN), lambda i, j: (j, 0, 0, 0)),
            pl.BlockSpec((1, 1, CMP_HIDDEN), lambda i, j: (j, 0, 0)),
            pl.BlockSpec((1, CMP_HIDDEN, NSA_DH), lambda i, j: (j, 0, 0)),
            pl.BlockSpec((1, 1, NSA_DH), lambda i, j: (j, 0, 0)),
            pl.BlockSpec((1, NSA_DH, CMP_HIDDEN), lambda i, j: (j, 0, 0)),
            pl.BlockSpec((1, NSA_DH, 1), lambda i, j: (j, 0, 0)),
            pl.BlockSpec((1, NSA_DH), lambda i, j: (0, 0)),
        ],
        out_specs=[pl.BlockSpec((1, nch, LANES), lambda i, j: (i, 0, 0)),
                   pl.BlockSpec((1, LANES, nch), lambda i, j: (i, 0, 0))],
        out_shape=[jax.ShapeDtypeStruct((b, nch, LANES), BF16), jax.ShapeDtypeStruct((b, LANES, nch), BF16)],
        compiler_params=_cparams(("parallel", "arbitrary"), 32),
        name="nsa_compress",
    )(xflat, pos, w1, b1, w2, b2, w2t, b2t, kc_g)


def _tile_heads(a):
    return jnp.concatenate([a] * NSA_HPG, axis=1)


def _nsa_body(qt_ref, gt_ref, kc_ref, vct_ref, ks_ref, kw_ref, vst_ref, vwt_ref, et_ref, ov_ref, o_ref, acc_sc, s_sc,
              *, n_cmp, n_top, kt):
    grp = pl.program_id(1)
    s0 = pl.program_id(2) * Q_BLOCK
    nch = kc_ref.shape[1]
    nblk = ov_ref.shape[0]

    qt = jnp.concatenate([qt_ref[h * LANES:(h + 1) * LANES, :] for h in range(NSA_HPG)], axis=1)
    t_row = s0 + lax.broadcasted_iota(jnp.int32, (1, Q_BLOCK), 1)

    cidx = lax.broadcasted_iota(jnp.int32, (nch, 1), 0)
    cvalid = (cidx * CMP_STRIDE + (CMP_BLOCK - 1) <= t_row) & (cidx < n_cmp)
    sc = _dot(kc_ref[0], qt) + _tile_heads(jnp.where(cvalid, 0.0, NEG))
    pc = jnp.exp2(sc - jnp.max(sc, axis=0, keepdims=True)).astype(BF16)
    stacked = jnp.concatenate([vct_ref[0], ov_ref[...], jnp.ones((8, nch), BF16)], axis=0)
    res = _dot(stacked, pc)
    inv = jnp.where(_tile_heads(t_row >= CMP_BLOCK - 1), 1.0 / jnp.maximum(res[LANES + nblk:LANES + nblk + 1], 1e-30),
                    0.0)
    oc = res[:LANES] * inv
    imp4 = res[LANES:LANES + nblk] * inv
    imp = (imp4[:, 0:Q_BLOCK] + imp4[:, Q_BLOCK:2 * Q_BLOCK] + imp4[:, 2 * Q_BLOCK:3 * Q_BLOCK]
           + imp4[:, 3 * Q_BLOCK:4 * Q_BLOCK])
    blk = lax.broadcasted_iota(jnp.int32, (nblk, Q_BLOCK), 0)
    cur = t_row // SLC_BLOCK
    imp = jnp.where(blk * SLC_BLOCK > t_row, NEG, imp)
    imp = jnp.where((blk == 0) | (blk == cur) | (blk == cur - 1), -NEG, imp)

    def pick_rounds(v, rounds):
        for _ in range(rounds):
            mx = jnp.max(v, axis=0, keepdims=True)
            first = jnp.min(jnp.where(v == mx, blk, nblk), axis=0, keepdims=True)
            v = jnp.where(blk == first, -jnp.inf, v)
        return v

    quarter = n_top // 4
    picked = pick_rounds(imp, quarter)

    wlen = WINDOW + Q_BLOCK
    w0 = pl.multiple_of(jnp.maximum(s0 - WINDOW, 0), Q_BLOCK)
    kpos = w0 + lax.broadcasted_iota(jnp.int32, (wlen, 1), 0)
    wbias = jnp.where((kpos <= t_row) & (kpos > t_row - WINDOW), 0.0, NEG)
    sw = _dot(kw_ref[0, pl.ds(w0, wlen), :], qt) + _tile_heads(wbias)
    picked = pick_rounds(picked, quarter)
    pw = jnp.exp2(sw - jnp.max(sw, axis=0, keepdims=True)).astype(BF16)
    picked = pick_rounds(picked, quarter)
    ow = _dot(vwt_ref[:, pl.ds(w0, wlen)], pw)
    ow = ow[:NSA_DH] / ow[NSA_DH:NSA_DH + 1]

    d0 = pl.multiple_of(s0, Q_BLOCK)
    dpos = s0 + lax.broadcasted_iota(jnp.int32, (Q_BLOCK, 1), 0)
    sd = _dot(ks_ref[0, pl.ds(d0, Q_BLOCK), :], qt) + _tile_heads(jnp.where(dpos <= t_row, 0.0, NEG))
    m_diag = jnp.max(sd, axis=0, keepdims=True)
    acc_sc[0] = _dot(vst_ref[:, pl.ds(d0, Q_BLOCK)], jnp.exp2(sd - m_diag).astype(BF16))
    acc_sc[1] = jnp.zeros(acc_sc.shape[1:], F32)
    picked = pick_rounds(picked, n_top - 3 * quarter)

    before = blk * SLC_BLOCK < s0
    selb = jnp.where((picked == -jnp.inf) & before, 0.0, MASKED).astype(BF16)
    rhs = jnp.concatenate([qt, _tile_heads(selb)], axis=0)

    last_tile = ks_ref.shape[1] // kt - 1

    def scores(idx, slot):
        k0 = pl.multiple_of(jnp.minimum(idx, last_tile) * kt, kt)
        lhs = jnp.concatenate([ks_ref[0, pl.ds(k0, kt), :], et_ref[pl.ds(k0, kt), :]], axis=1)
        s_sc[slot] = _dot(lhs, rhs)

    def update(idx, slot, m_old, acc_ref):
        k0 = pl.multiple_of(idx * kt, kt)
        m_new = jnp.maximum(m_old, jnp.max(s_sc[slot], axis=0, keepdims=True))
        p = jnp.exp2(s_sc[slot] - m_new).astype(BF16)
        acc_ref[...] = jnp.exp2(m_old - m_new) * acc_ref[...] + _dot(vst_ref[:, pl.ds(k0, kt)], p)
        return m_new

    def tile_group(first, carry, count):
        ms = list(carry)
        for t in range(count):
            scores(first + t + 2, (t + 2) % 4)
            ms[t % 2] = update(first + t, t % 4, ms[t % 2], acc_sc.at[t % 2])
        return tuple(ms)

    n_tiles = (s0 + kt - 1) // kt
    long_groups = n_tiles // 8
    short_groups = (n_tiles - 8 * long_groups + 3) // 4
    scores(0, 0)
    scores(1, 1)
    carry = lax.fori_loop(0, long_groups, lambda j, c: tile_group(8 * j, c, 8),
                          (m_diag, jnp.full((1, NSA_HPG * Q_BLOCK), NEG, F32)))
    m0, m1 = lax.fori_loop(0, short_groups, lambda j, c: tile_group(8 * long_groups + 4 * j, c, 4), carry)
    m_fin = jnp.maximum(m0, m1)
    acc = acc_sc[0] * jnp.exp2(m0 - m_fin) + acc_sc[1] * jnp.exp2(m1 - m_fin)
    osl = acc[:NSA_DH] / acc[NSA_DH:NSA_DH + 1]

    oc = jnp.where(grp == 0, oc[:NSA_DH], oc[NSA_DH:])
    gts = jax.nn.sigmoid(gt_ref[...])
    for h in range(NSA_HPG):
        cols = slice(h * Q_BLOCK, (h + 1) * Q_BLOCK)
        o_ref[h * NSA_DH:(h + 1) * NSA_DH, :] = (
            gts[3 * h:3 * h + 1, :] * oc[:, cols] + gts[3 * h + 1:3 * h + 2, :] * osl[:, cols]
            + gts[3 * h + 2:3 * h + 3, :] * ow[:, cols])


def _nsa_attention(qt, gt, kc, vct, okv, vt, expand_t, overlap, b, t, n_cmp, n_top, kt):
    nch = kc.shape[1]
    nq = t // Q_BLOCK
    n = b * t
    return pl.pallas_call(
        functools.partial(_nsa_body, n_cmp=n_cmp, n_top=n_top, kt=kt),
        grid=(b, NSA_GROUPS, nq),
        in_specs=[
            pl.BlockSpec((NSA_HPG * LANES, Q_BLOCK), lambda bi, g, i: (g, bi * nq + i)),
            pl.BlockSpec((GATE_ROWS, Q_BLOCK), lambda bi, g, i: (g, bi * nq + i)),
            pl.BlockSpec((1, nch, LANES), lambda bi, g, i: (bi, 0, 0)),
            pl.BlockSpec((1, LANES, nch), lambda bi, g, i: (bi, 0, 0)),
            pl.BlockSpec((1, t, LANES), lambda bi, g, i: (bi, 0, 2)),
            pl.BlockSpec((1, t, LANES), lambda bi, g, i: (bi, 0, 4)),
            pl.BlockSpec((LANES, t), lambda bi, g, i: (g, bi)),
            pl.BlockSpec((LANES, t), lambda bi, g, i: (NSA_GROUPS + g, bi)),
            pl.BlockSpec(expand_t.shape, lambda bi, g, i: (0, 0)),
            pl.BlockSpec(overlap.shape, lambda bi, g, i: (0, 0)),
        ],
        out_specs=pl.BlockSpec((NSA_HPG * NSA_DH, Q_BLOCK), lambda bi, g, i: (g, bi * nq + i)),
        out_shape=jax.ShapeDtypeStruct((NSA_W, n), F32),
        scratch_shapes=[pltpu.VMEM((2, LANES, NSA_HPG * Q_BLOCK), F32),
                        pltpu.VMEM((4, kt, NSA_HPG * Q_BLOCK), F32)],
        compiler_params=_cparams(("parallel", "parallel", "arbitrary"), 56),
        name="nsa_attention",
    )(qt, gt, kc, vct, okv, okv, vt, vt, expand_t, overlap)


def _split_bf16(a):
    hi = a.astype(BF16)
    return hi, (a - hi.astype(F32)).astype(BF16)


def _unit_lower_inverses(lmats):
    c = lmats[0].shape[0]
    r = lax.broadcasted_iota(jnp.int32, (c, c), 0)
    col = lax.broadcasted_iota(jnp.int32, (c, c), 1)
    eye = jnp.where(r == col, 1.0, 0.0)
    xs = [eye - l for l in lmats]
    ps = []
    for l in lmats:
        l16 = l.astype(BF16)
        ps.append(_dot(l16, l16))
    steps = int(np.log2(c)) - 1
    for s in range(steps):
        last = s + 1 == steps
        for i in range(len(lmats)):
            rhs = ps[i].astype(BF16)
            if last:
                xs[i] = xs[i] + _dot(xs[i].astype(BF16), rhs)
            else:
                both = _dot(jnp.concatenate([xs[i], ps[i]], axis=0).astype(BF16), rhs)
                xs[i] = xs[i] + both[:c]
                ps[i] = both[c:]
    return xs


def _gdn_body(x_ref, z_ref, ab_ref, alog_ref, dtb_ref, og_ref, o_ref, s_sc, *, ct):
    nb = x_ref.shape[0]

    @pl.when(pl.program_id(0) == 0)
    def _():
        s_sc[...] = jnp.zeros(s_sc.shape, F32)

    ch = GDN_CHUNK
    r = lax.broadcasted_iota(jnp.int32, (ch, ch), 0)
    col = lax.broadcasted_iota(jnp.int32, (ch, ch), 1)
    incl = r >= col
    strict = r > col
    tril16 = jnp.concatenate([jnp.where(incl, 1.0, 0.0).astype(BF16)] * 3, axis=1)

    units = []
    for ci in range(ct // ch):
        rows = slice(ci * ch, (ci + 1) * ch)
        for bi in range(nb):
            ab = ab_ref[bi, rows, :]
            g_all = -jnp.exp(alog_ref[...]) * jax.nn.softplus(ab + dtb_ref[...])
            beta_all = jax.nn.sigmoid(ab)
            g_hi, g_lo = _split_bf16(g_all)
            g_lo2 = (g_all - g_hi.astype(F32) - g_lo.astype(F32)).astype(BF16)
            gc_all = _dot(tril16, jnp.concatenate([g_hi, g_lo, g_lo2], axis=0))
            gc_t = gc_all.T
            for h in range(GDN_HEADS):
                hs = slice(h * GDN_DH, (h + 1) * GDN_DH)
                q16 = x_ref[bi, rows, hs]
                k16 = x_ref[bi, rows, GDN_W + h * GDN_DH:GDN_W + (h + 1) * GDN_DH]
                qh, kh = q16.astype(F32), k16.astype(F32)
                vh = x_ref[bi, rows, 2 * GDN_W + h * GDN_DH:2 * GDN_W + (h + 1) * GDN_DH].astype(F32)
                gc = gc_all[:, h:h + 1]
                gr = gc_t[h:h + 1, :]
                g_last = gc_all[ch - 1:ch, h:h + 1]
                beta = beta_all[:, GDN_HEADS + h:GDN_HEADS + h + 1]
                eg = jnp.exp(gc)
                decay = jnp.where(incl, jnp.exp(jnp.minimum(gc - gr, 0.0)), 0.0)
                kb = kh * beta
                with_k = (_dot_nt(jnp.concatenate([kb.astype(BF16), q16], axis=0), k16)
                          * jnp.concatenate([decay, decay], axis=0))
                units.append(dict(
                    rows=rows, bi=bi, h=h,
                    lmat=jnp.where(strict, with_k[:ch], 0.0),
                    vb_kbg=jnp.concatenate([(vh * beta).astype(BF16), (kb * eg).astype(BF16)], axis=1),
                    qk=jnp.where(incl, with_k[ch:], 0.0).astype(BF16),
                    qg=(qh * eg).astype(BF16), kd_t=(kh * jnp.exp(g_last - gc)).T.astype(BF16),
                    gl=jnp.exp(g_last)))
    tinvs = _unit_lower_inverses([u["lmat"] for u in units])
    for u, tinv in zip(units, tinvs):
        u_w = _dot(tinv.astype(BF16), u["vb_kbg"])
        u["u"] = u_w[:, :GDN_DH]
        u["w_qg"] = jnp.concatenate([u_w[:, GDN_DH:].astype(BF16), u["qg"]], axis=0)
        u["kd_qk"] = jnp.concatenate([u["kd_t"], u["qk"]], axis=0)

    for u in units:
        bi, h, rows = u["bi"], u["h"], u["rows"]
        hs = slice(h * GDN_DH, (h + 1) * GDN_DH)
        s_old = s_sc[bi * GDN_HEADS + h]
        from_state = _dot(u["w_qg"], s_old.astype(BF16))
        v_new = (u["u"] - from_state[:ch]).astype(BF16)
        from_v = _dot(u["kd_qk"], v_new)
        s_sc[bi * GDN_HEADS + h] = s_old * u["gl"] + from_v[:GDN_DH]
        o = from_state[ch:] + from_v[GDN_DH:]
        on = o * lax.rsqrt(jnp.mean(o * o, axis=-1, keepdims=True) + EPS) * og_ref[...]
        zh = z_ref[bi, rows, hs].astype(F32)
        o_ref[bi, rows, hs] = (on * (zh * jax.nn.sigmoid(zh))).astype(BF16)


def _gdn(oqkv, oz, oab, alog, dtb, og, ct=128):
    b, t, w3 = oqkv.shape
    full = lambda a: pl.BlockSpec(a.shape, lambda c: (0,) * a.ndim)
    return pl.pallas_call(
        functools.partial(_gdn_body, ct=ct),
        grid=(t // ct,),
        in_specs=[
            pl.BlockSpec((b, ct, w3), lambda c: (0, c, 0)),
            pl.BlockSpec((b, ct, GDN_W), lambda c: (0, c, 0)),
            pl.BlockSpec((b, ct, LANES), lambda c: (0, c, 0)),
            full(alog), full(dtb), full(og),
        ],
        out_specs=pl.BlockSpec((b, ct, GDN_W), lambda c: (0, c, 0)),
        out_shape=jax.ShapeDtypeStruct((b, t, GDN_W), BF16),
        scratch_shapes=[pltpu.VMEM((b * GDN_HEADS, GDN_DH, GDN_DH), F32)],
        compiler_params=_cparams(("arbitrary",), 32),
        name="gdn",
    )(oqkv, oz, oab, alog, dtb, og)


def _outproj_body(ont_ref, og_ref, x_ref, ng_ref, wo_ref, fg_ref, wr_ref, br_ref, upper_ref,
                  x1_ref, h2_ref, gate_ref, route_ref, cnt_ref, cnt_sc):
    i = pl.program_id(0)
    tm = x_ref.shape[0]

    @pl.when(i == 0)
    def _():
        cnt_sc[...] = jnp.zeros(cnt_sc.shape, F32)

    a = ont_ref[...]
    a = (a * lax.rsqrt(jnp.mean(a * a, axis=0, keepdims=True) + EPS) * ng_ref[...]).astype(BF16)
    x1 = x_ref[...] + _dot_tn(a, wo_ref[0:NSA_W, :]) + _dot(og_ref[...], wo_ref[NSA_W:, :])
    x1_ref[...] = x1
    h2f = x1 * lax.rsqrt(jnp.mean(x1 * x1, axis=-1, keepdims=True) + EPS) * fg_ref[...]
    _store_pieces(h2_ref, h2f)
    h2 = h2f.astype(BF16)

    logits = (_dot(h2, wr_ref[...]) + br_ref[...]).T[:N_EXPERTS]
    erow = lax.broadcasted_iota(jnp.int32, (N_EXPERTS, tm), 0)
    onehot = jnp.zeros((N_EXPERTS, tm), F32)
    firsts, vals = [], []
    v = logits
    for k in range(TOP_K):
        mx = jnp.max(v, axis=0, keepdims=True)
        first = jnp.min(jnp.where(v == mx, erow, N_EXPERTS), axis=0, keepdims=True)
        hit = erow == first
        v = jnp.where(hit, -jnp.inf, v)
        onehot = jnp.where(hit, 1.0, onehot)
        firsts.append(first)
        vals.append(mx)
    vals = [jnp.exp(m - vals[0]) for m in vals]
    inv = 1.0 / (vals[0] + vals[1] + vals[2] + vals[3])
    gates_t = jnp.concatenate([m * inv for m in vals] + [jnp.zeros((LANES - TOP_K, tm), F32)], axis=0)
    gate_ref[...] = gates_t.T

    excl = cnt_sc[...] + _dot(onehot.astype(BF16), upper_ref[...])
    for k in range(TOP_K):
        route_ref[k:k + 1, :] = firsts[k]
        rank = jnp.sum(jnp.where(erow == firsts[k], excl, 0.0), axis=0, keepdims=True)
        route_ref[TOP_K + k:TOP_K + k + 1, :] = rank.astype(jnp.int32)
    cnt_sc[...] = cnt_sc[...] + jnp.sum(onehot, axis=1, keepdims=True)
    cnt_ref[...] = cnt_sc[...].astype(jnp.int32)


def _out_proj(o_nsa_t, o_gdn, x2, ng, wo, fg, wr, br, tm=512):
    upper = jnp.asarray(np.arange(tm)[:, None] < np.arange(tm)[None, :], BF16)
    n, d = x2.shape
    full = lambda a: pl.BlockSpec(a.shape, lambda i: (0,) * a.ndim)
    row = lambda w: pl.BlockSpec((tm, w), lambda i: (i, 0))
    return pl.pallas_call(
        _outproj_body,
        grid=(n // tm,),
        in_specs=[pl.BlockSpec((NSA_W, tm), lambda i: (0, i)), row(GDN_W), row(d), full(ng), full(wo), full(fg),
                  full(wr), full(br), full(upper)],
        out_specs=[row(d), pl.BlockSpec((d // 2 // SC_SUBROW, tm, SC_SUBROW), lambda i: (0, i, 0)),
                   row(LANES), pl.BlockSpec((2 * TOP_K, tm), lambda i: (0, i)),
                   pl.BlockSpec((N_EXPERTS, 1), lambda i: (0, 0))],
        out_shape=[jax.ShapeDtypeStruct((n, d), F32),
                   jax.ShapeDtypeStruct((d // 2 // SC_SUBROW, n, SC_SUBROW), jnp.int32),
                   jax.ShapeDtypeStruct((n, LANES), F32), jax.ShapeDtypeStruct((2 * TOP_K, n), jnp.int32),
                   jax.ShapeDtypeStruct((N_EXPERTS, 1), jnp.int32)],
        scratch_shapes=[pltpu.VMEM((N_EXPERTS, 1), F32)],
        compiler_params=_cparams(("arbitrary",), 48),
        name="out_proj_router",
    )(o_nsa_t, o_gdn, x2, ng, wo, fg, wr, br, upper)


def _dest_body(ps_ref, route_ref, o_ref, *, n_rows, pieces):
    expert = route_ref[0:TOP_K, :]
    start = jnp.zeros(expert.shape, jnp.int32)
    for e in range(N_EXPERTS):
        start = jnp.where(expert == e, ps_ref[e], start)
    dest = start + route_ref[TOP_K:2 * TOP_K, :]
    for k in range(TOP_K):
        for j in range(pieces):
            o_ref[k * pieces + j:k * pieces + j + 1, :] = dest[k:k + 1, :] + j * n_rows


def _dest_rows(pstarts, route, n_rows, pieces):
    n = route.shape[1]
    tn = min(2048, n)
    grid_spec = pltpu.PrefetchScalarGridSpec(
        num_scalar_prefetch=1,
        grid=(n // tn,),
        in_specs=[pl.BlockSpec((2 * TOP_K, tn), lambda i, ps: (0, i))],
        out_specs=pl.BlockSpec((TOP_K * pieces, tn), lambda i, ps: (0, i)),
    )
    return pl.pallas_call(
        functools.partial(_dest_body, n_rows=n_rows, pieces=pieces),
        grid_spec=grid_spec,
        out_shape=jax.ShapeDtypeStruct((TOP_K * pieces, n), jnp.int32),
        name="moe_dest_rows",
    )(pstarts, route)


def _expert_body(be_ref, fresh_ref, slot_ref, next_ref, xs_ref, wg_hbm, bg_ref, wu_hbm, bu_ref, wd_hbm, bd_ref,
                 y_ref, wf32, w16, sems):
    i = pl.program_id(0)
    used = i < be_ref[pl.num_programs(0)]
    hbm = (wg_hbm, wu_hbm, wd_hbm)

    def weight_copy(expert, slot, j):
        return pltpu.make_async_copy(hbm[j].at[expert], wf32.at[slot, j], sems.at[slot, j])

    @pl.when((i == 0) & used)
    def _():
        for j in range(3):
            weight_copy(be_ref[0], 0, j).start()

    @pl.when(used & (fresh_ref[i] == 1))
    def _():
        slot = slot_ref[i]
        for j in range(3):
            weight_copy(be_ref[i], slot, j).wait()
            w16[j] = wf32[slot, j].astype(BF16)

        @pl.when(next_ref[i] >= 0)
        def _():
            for j in range(3):
                weight_copy(next_ref[i], 1 - slot, j).start()

    @pl.when(jnp.logical_not(used))
    def _():
        y_ref[...] = jnp.zeros(y_ref.shape, y_ref.dtype)

    @pl.when(used)
    def _():
        x = _join_pieces(xs_ref).astype(BF16)
        gate = jnp.minimum(_dot(x, w16[0]) + bg_ref[0], SWIGLU_LIMIT)
        up = jnp.clip(_dot(x, w16[1]) + bu_ref[0], -SWIGLU_LIMIT, SWIGLU_LIMIT)
        glu = gate * jax.nn.sigmoid(gate * SWIGLU_ALPHA)
        _store_pieces(y_ref, _dot(((up + 1.0) * glu).astype(BF16), w16[2]) + bd_ref[0])


def _experts(blk_e, fresh, slot, nxt, xs, wg, bg, wu, bu, wd, bd):
    pieces, n_rows, sub = xs.shape
    d, de = wg.shape[1], wg.shape[2]
    assert d == de
    r = MOE_ROW_BLOCK
    bspec = lambda w: pl.BlockSpec((1, 1, w), lambda i, be, *_: (be[i], 0, 0))
    hbm = pl.BlockSpec(memory_space=pl.ANY)
    grid_spec = pltpu.PrefetchScalarGridSpec(
        num_scalar_prefetch=4,
        grid=(n_rows // r,),
        in_specs=[pl.BlockSpec((pieces, r, sub), lambda i, *_: (0, i, 0)),
                  hbm, bspec(de), hbm, bspec(de), hbm, bspec(d)],
        out_specs=pl.BlockSpec((pieces, r, sub), lambda i, *_: (0, i, 0)),
        scratch_shapes=[pltpu.VMEM((2, 3, d, de), F32), pltpu.VMEM((3, d, de), BF16),
                        pltpu.SemaphoreType.DMA((2, 3))],
    )
    return pl.pallas_call(
        _expert_body,
        grid_spec=grid_spec,
        out_shape=jax.ShapeDtypeStruct((pieces, n_rows, sub), jnp.int32),
        compiler_params=_cparams(("arbitrary",), 56),
        name="moe_experts",
    )(blk_e, fresh, slot, nxt, xs, wg, bg, wu, bu, wd, bd)


SC_WINDOW = 128
SC_SUBROW = 256


def _sc_mesh():
    return plsc.VectorSubcoreMesh(core_axis_name="c", subcore_axis_name="s")


def _sc_dispatch(h2, dest_rows, n_rows):
    n, d = h2.shape

    @functools.partial(pl.kernel, out_type=jax.ShapeDtypeStruct((n_rows, d), h2.dtype), mesh=_sc_mesh())
    def dispatch(x_hbm, *refs):
        idx_hbm, o_hbm = refs[:TOP_K], refs[TOP_K]

        def body(x_vmem, *idx_vmem):
            for iv in idx_vmem:
                pltpu.sync_copy(x_vmem, o_hbm.at[iv.at[0]])

        pltpu.emit_pipeline(
            body,
            grid=(n // SC_WINDOW,),
            in_specs=[pl.BlockSpec((SC_WINDOW, d), lambda i: (i, 0))]
                     + [pl.BlockSpec((1, SC_WINDOW), lambda i: (0, i))] * TOP_K,
            out_specs=[],
            core_axis_name=("c", "s"),
            dimension_semantics=(pltpu.PARALLEL,),
        )(x_hbm, *idx_hbm)

    return dispatch(h2, *dest_rows)


def _sc_gather(table, idx):
    _, d = table.shape
    m = idx.shape[1]

    @functools.partial(pl.kernel, out_type=jax.ShapeDtypeStruct((m, d), table.dtype), mesh=_sc_mesh())
    def gather(t_hbm, i_hbm, o_hbm):
        def body(i_vmem, o_vmem):
            pltpu.sync_copy(t_hbm.at[i_vmem.at[0]], o_vmem)

        pltpu.emit_pipeline(
            body,
            grid=(m // SC_WINDOW,),
            in_specs=[pl.BlockSpec((1, SC_WINDOW), lambda i: (0, i))],
            out_specs=[pl.BlockSpec((SC_WINDOW, d), lambda i: (i, 0))],
            core_axis_name=("c", "s"),
            dimension_semantics=(pltpu.PARALLEL,),
        )(i_hbm, o_hbm)

    return gather(table, idx)


def _combine_body(x1_ref, y_ref, gate_ref, o_ref):
    acc = x1_ref[...]
    for k in range(TOP_K):
        acc = acc + gate_ref[:, k:k + 1] * _join_pieces(y_ref.at[k])
    o_ref[...] = acc


def _combine(x1, y4, gates, tm=512):
    n, d = x1.shape
    pieces, sub = y4.shape[1], y4.shape[3]
    row = lambda w: pl.BlockSpec((tm, w), lambda i: (i, 0))
    return pl.pallas_call(
        _combine_body,
        grid=(n // tm,),
        in_specs=[row(d), pl.BlockSpec((TOP_K, pieces, tm, sub), lambda i: (0, 0, i, 0)), row(LANES)],
        out_specs=row(d),
        out_shape=jax.ShapeDtypeStruct((n, d), F32),
        compiler_params=_cparams(("parallel",), 48),
        name="moe_combine",
    )(x1, y4, gates)


def _pad_lanes(a, width=LANES):
    return jnp.pad(a, ((0, 0), (0, width - a.shape[1])))


def _layer(x, attn_norm_g, w_in, q_g, kc_g, ks_g, kw_g, ck_pos, ck_w1, ck_b1, ck_w2, ck_b2,
           cv_pos, cv_w1, cv_b1, cv_w2, cv_b2, nsa_out_g, conv_w, a_log, dt_bias, gdn_out_g, w_out,
           ffn_g, router_w, router_b, e_wg, e_bg, e_wu, e_bu, e_wd, e_bd):
    b, t, d = x.shape
    n = b * t
    x2 = x.reshape(n, d)

    o = np.cumsum([0, NSA_W] + [NSA_GROUPS * NSA_DH] * 6 + [3 * NSA_HEADS, 3 * GDN_W, GDN_W, GDN_HEADS, GDN_HEADS])
    wq_t = w_in[:, o[0]:o[1]].T.reshape(NSA_GROUPS, NSA_HPG, NSA_DH, d)
    zq = jnp.zeros((NSA_HPG, NSA_DH, d), F32)
    wq_t = jnp.stack([jnp.concatenate([wq_t[0], zq], axis=1), jnp.concatenate([zq, wq_t[1]], axis=1)])
    wq_t = wq_t.reshape(NSA_HEADS * LANES, d).astype(BF16)
    qg1 = q_g * (NSA_DH ** -0.5 * np.log2(np.e))
    zg = jnp.zeros((NSA_DH,), F32)
    qg_col = jnp.concatenate([jnp.tile(jnp.concatenate([qg1, zg]), NSA_HPG),
                              jnp.tile(jnp.concatenate([zg, qg1]), NSA_HPG)]).reshape(NSA_HEADS * LANES, 1)
    wkv = w_in[:, o[1]:o[7]].astype(BF16)
    ones = jnp.ones((LANES,), F32)
    kg = jnp.concatenate([ones, ones, ks_g, ks_g, ones, kw_g, kw_g, ones]).reshape(1, 6 * LANES)
    wv_t = jnp.concatenate([w_in[:, o[4]:o[5]], w_in[:, o[6]:o[7]]], axis=1).T.reshape(2 * NSA_GROUPS, NSA_DH, d)
    wv_t = jnp.pad(wv_t, ((0, 0), (0, LANES - NSA_DH), (0, 0))).reshape(2 * NSA_GROUPS * LANES, d).astype(BF16)
    vone = jnp.asarray((np.arange(2 * NSA_GROUPS * LANES) % LANES == NSA_DH).astype(np.float32)[:, None])
    wg_t = w_in[:, o[7]:o[8]].T.reshape(NSA_GROUPS, NSA_HPG * 3, d)
    wg_t = jnp.pad(wg_t, ((0, 0), (0, GATE_ROWS - NSA_HPG * 3), (0, 0))).reshape(NSA_GROUPS * GATE_ROWS, d)
    wg_t = wg_t.astype(BF16)
    wab = _pad_lanes(w_in[:, o[10]:o[12]]).astype(BF16)
    wqkv = w_in[:, o[8]:o[9]].astype(BF16)
    wz = w_in[:, o[9]:o[10]].astype(BF16)

    tm = min(512, t)
    oqt, okv, ovt, ogt, oqkv, oz, oab = _in_proj(x2, attn_norm_g.reshape(1, d), wq_t, wkv, wv_t, wg_t, wqkv, wz,
                                                 wab, qg_col, kg, vone, conv_w, t // tm, tm)

    nch = t // CMP_STRIDE
    n_cmp = (t - CMP_BLOCK) // CMP_STRIDE + 1
    half = CMP_STRIDE * NSA_DH
    xflat = okv[:, :2 * LANES].reshape(b, nch, CMP_STRIDE, 2, NSA_GROUPS, NSA_DH)
    xflat = xflat.transpose(0, 3, 4, 1, 2, 5).reshape(b, 2, NSA_GROUPS, nch, half)
    pos = jnp.stack([ck_pos, cv_pos]).reshape(2, 2, 1, half)
    w1 = jnp.stack([ck_w1, cv_w1]).reshape(2, 2, half, CMP_HIDDEN).astype(BF16)
    b1 = jnp.stack([ck_b1, cv_b1]).reshape(2, 1, CMP_HIDDEN)
    w2 = jnp.stack([ck_w2, cv_w2]).astype(BF16)
    b2 = jnp.stack([ck_b2, cv_b2]).reshape(2, 1, NSA_DH)
    w2t = jnp.stack([ck_w2.T, cv_w2.T]).astype(BF16)
    b2t = jnp.stack([ck_b2, cv_b2]).reshape(2, NSA_DH, 1)
    kc, vct = _compress(xflat, pos, w1, b1, w2, b2, w2t, b2t, kc_g.reshape(1, NSA_DH), n_cmp)

    n_slc = t // SLC_BLOCK
    n_top = min(SLC_TOPK, n_slc)
    nblk = max(n_slc, LANES)
    kt = min(256, t // 4)
    assert (t // kt) % 4 == 0
    ci = np.arange(nch)[None, :] * CMP_STRIDE
    sj = np.arange(nblk)[:, None] * SLC_BLOCK
    overlap = ((ci < sj + SLC_BLOCK) & (ci + CMP_BLOCK > sj) & (np.arange(nch)[None, :] < n_cmp)
               & (np.arange(nblk)[:, None] < n_slc))
    expand_t = (np.arange(t)[:, None] // SLC_BLOCK) == np.arange(nblk)[None, :]
    o_nsa_t = _nsa_attention(oqt, ogt, kc, vct, okv.reshape(b, t, -1), ovt, jnp.asarray(expand_t, BF16),
                             jnp.asarray(overlap, BF16), b, t, n_cmp, n_top, kt)

    alog_row = _pad_lanes(a_log.reshape(1, GDN_HEADS))
    dtb_row = _pad_lanes(dt_bias.reshape(1, GDN_HEADS))
    o_gdn = _gdn(oqkv.reshape(b, t, -1), oz.reshape(b, t, -1), oab.reshape(b, t, -1),
                 alog_row, dtb_row, gdn_out_g.reshape(1, GDN_DH))

    wr = _pad_lanes(router_w).astype(BF16)
    br = _pad_lanes(router_b.reshape(1, N_EXPERTS))
    x1, h2, gates, route, counts = _out_proj(
        o_nsa_t, o_gdn.reshape(n, GDN_W), x2, nsa_out_g.reshape(NSA_W, 1),
        w_out.astype(BF16), ffn_g.reshape(1, d), wr, br)

    r = MOE_ROW_BLOCK
    nk = n * TOP_K
    counts = counts[:, 0]
    pcounts = (counts + r - 1) // r * r
    pends = jnp.cumsum(pcounts)
    pstarts = pends - pcounts
    n_rows = (nk + r - 1) // r * r + N_EXPERTS * r
    n_blocks = n_rows // r
    blk_start = jnp.arange(n_blocks, dtype=jnp.int32)[:, None] * r
    blk_e = jnp.minimum(jnp.sum(pends[None, :] <= blk_start, axis=1), N_EXPERTS - 1).astype(jnp.int32)
    n_used = (pends[-1] // r).astype(jnp.int32)
    fresh = (jnp.arange(n_blocks) < n_used) & (blk_e != jnp.concatenate([jnp.full((1,), -1, jnp.int32), blk_e[:-1]]))
    slot = ((jnp.cumsum(fresh) - 1) % 2).astype(jnp.int32)
    eid = jnp.arange(N_EXPERTS, dtype=jnp.int32)
    later = jnp.where((eid[None, :] > eid[:, None]) & (pcounts[None, :] > 0), eid[None, :], N_EXPERTS)
    next_expert = jnp.min(later, axis=1)
    next_expert = jnp.where(next_expert < N_EXPERTS, next_expert, -1).astype(jnp.int32)
    nxt = next_expert[blk_e]
    blk_e = jnp.concatenate([blk_e, n_used[None]])
    pieces = d // 2 // SC_SUBROW
    dest_p = _dest_rows(pstarts.astype(jnp.int32), route, n_rows, pieces).reshape(TOP_K, pieces, n)
    xs = _sc_dispatch(h2.reshape(pieces * n, SC_SUBROW), [dest_p[k].reshape(1, pieces * n) for k in range(TOP_K)],
                      pieces * n_rows)
    ys = _experts(blk_e, fresh.astype(jnp.int32), slot, nxt, xs.reshape(pieces, n_rows, SC_SUBROW), e_wg, e_bg.reshape(N_EXPERTS, 1, -1), e_wu,
                  e_bu.reshape(N_EXPERTS, 1, -1), e_wd, e_bd.reshape(N_EXPERTS, 1, -1))
    y4 = _sc_gather(ys.reshape(pieces * n_rows, SC_SUBROW), dest_p.reshape(1, nk * pieces))
    return _combine(x1, y4.reshape(TOP_K, pieces, n, SC_SUBROW), gates).reshape(b, t, d)


def kernel(x, attn_norm_g, w_in, nsa_q_norm_g, nsa_kc_norm_g, nsa_ks_norm_g, nsa_kw_norm_g, cmp_k_pos, cmp_k_w1, cmp_k_b1, cmp_k_w2, cmp_k_b2, cmp_v_pos, cmp_v_w1, cmp_v_b1, cmp_v_w2, cmp_v_b2, nsa_out_norm_g, gdn_conv_w, gdn_a_log, gdn_dt_bias, gdn_out_norm_g, w_out, ffn_norm_g, router_w, router_b, exp_w_gate, exp_b_gate, exp_w_up, exp_b_up, exp_w_down, exp_b_down):
    params = (attn_norm_g, w_in, nsa_q_norm_g, nsa_kc_norm_g, nsa_ks_norm_g, nsa_kw_norm_g,
              cmp_k_pos, cmp_k_w1, cmp_k_b1, cmp_k_w2, cmp_k_b2, cmp_v_pos, cmp_v_w1, cmp_v_b1, cmp_v_w2, cmp_v_b2,
              nsa_out_norm_g, gdn_conv_w, gdn_a_log, gdn_dt_bias, gdn_out_norm_g, w_out, ffn_norm_g,
              router_w, router_b, exp_w_gate, exp_b_gate, exp_w_up, exp_b_up, exp_w_down, exp_b_down)
    for l in range(attn_norm_g.shape[0]):
        x = _layer(x, *(p[l] for p in params))
    return x
```

```python
import functools

import jax
import jax.numpy as jnp
import numpy as np
from jax import lax
from jax.experimental import pallas as pl
from jax.experimental.pallas import tpu as pltpu
from jax.experimental.pallas import tpu_sc as plsc

F32 = jnp.float32
BF16 = jnp.bfloat16

EPS = 1e-6
NEG = -1e30
MASKED = -2.0 ** 100

NSA_HEADS = 8
NSA_GROUPS = 2
NSA_HPG = 4
NSA_DH = 64
CMP_BLOCK = 32
CMP_STRIDE = 16
CMP_HIDDEN = 256
SLC_BLOCK = 64
SLC_TOPK = 16
WINDOW = 512
Q_BLOCK = 128
GDN_HEADS = 4
GDN_DH = 128
GDN_CONV = 4
GDN_CHUNK = 64
N_EXPERTS = 32
TOP_K = 4
SWIGLU_LIMIT = 7.0
SWIGLU_ALPHA = 1.702
MOE_ROW_BLOCK = 256

LANES = 128
GATE_ROWS = 16
FLASH_BODY_TILES = (16, 8, 4)
NSA_W = NSA_HEADS * NSA_DH
GDN_W = GDN_HEADS * GDN_DH

_NT = (((1,), (1,)), ((), ()))
_TN = (((0,), (0,)), ((), ()))


def _cparams(sem, vmem_mb):
    return pltpu.CompilerParams(dimension_semantics=sem, vmem_limit_bytes=vmem_mb * 1024 * 1024)


def _dot(a, b):
    return jnp.dot(a, b, preferred_element_type=F32)


def _dot_nt(a, b):
    return lax.dot_general(a, b, _NT, preferred_element_type=F32)


def _dot_tn(a, b):
    return lax.dot_general(a, b, _TN, preferred_element_type=F32)


def _store_pieces(ref, val):
    half = val.shape[1] // 2
    hi = lax.bitcast_convert_type(val[:, :half].astype(BF16).astype(F32), jnp.uint32)
    lo = lax.bitcast_convert_type(val[:, half:].astype(BF16).astype(F32), jnp.uint32)
    words = lax.bitcast_convert_type(hi | (lo >> 16), jnp.int32)
    sub = ref.shape[2]
    for j in range(ref.shape[0]):
        ref[j] = words[:, j * sub:(j + 1) * sub]


def _join_pieces(ref):
    words = jnp.concatenate([ref[j] for j in range(ref.shape[0])], axis=1)
    words = lax.bitcast_convert_type(words, jnp.uint32)
    hi = lax.bitcast_convert_type(words & jnp.uint32(0xFFFF0000), F32)
    lo = lax.bitcast_convert_type(words << 16, F32)
    return jnp.concatenate([hi, lo], axis=1)


def _inproj_body(x_ref, g_ref, wqt_ref, wkv_ref, wvt_ref, wgt_ref, wqkv_ref, wz_ref, wab_ref, qg_ref, kg_ref,
                 vone_ref, cw_ref, oqt_ref, okv_ref, ovt_ref, ogt_ref, oqkv_ref, oz_ref, oab_ref, ybuf,
                 *, tiles_per_seq):
    x = x_ref[...]
    h = (x * lax.rsqrt(jnp.mean(x * x, axis=-1, keepdims=True) + EPS) * g_ref[...]).astype(BF16)
    tm = x.shape[0]

    yq = _dot_nt(wqt_ref[...], h)
    for s in range(NSA_HEADS):
        sl = slice(s * LANES, (s + 1) * LANES)
        ys = yq[sl, :]
        ms = jnp.sum(ys * ys, axis=0, keepdims=True) * (1.0 / NSA_DH)
        oqt_ref[sl, :] = (ys * lax.rsqrt(ms + EPS) * qg_ref[sl, :]).astype(BF16)

    ykv = _dot(h, wkv_ref[...])
    lane = lax.broadcasted_iota(jnp.int32, (tm, LANES), 1)
    low = lane < NSA_DH
    for s in range(6):
        sl = slice(s * LANES, (s + 1) * LANES)
        ys = ykv[:, sl]
        if s in (2, 4):
            y2 = ys * ys
            s0 = jnp.sum(jnp.where(low, y2, 0.0), axis=-1, keepdims=True)
            s1 = jnp.sum(jnp.where(low, 0.0, y2), axis=-1, keepdims=True)
            ms = jnp.where(low, s0, s1) * (1.0 / NSA_DH)
            ys = ys * lax.rsqrt(ms + EPS) * kg_ref[:, sl]
        okv_ref[:, sl] = ys.astype(BF16)

    ovt_ref[...] = (_dot_nt(wvt_ref[...], h) + vone_ref[...]).astype(BF16)
    ogt_ref[...] = _dot_nt(wgt_ref[...], h)
    oz_ref[...] = _dot(h, wz_ref[...]).astype(BF16)
    oab_ref[...] = _dot(h, wab_ref[...])

    halo = ybuf.shape[0] - tm
    first = pl.program_id(0) % tiles_per_seq == 0

    @pl.when(first)
    def _():
        ybuf[0:halo, :] = jnp.zeros((halo, ybuf.shape[1]), F32)

    @pl.when(jnp.logical_not(first))
    def _():
        ybuf[0:halo, :] = ybuf[tm:tm + halo, :]

    ybuf[halo:halo + tm, :] = _dot(h, wqkv_ref[...])
    taps = cw_ref.shape[0]
    y = cw_ref[0:1, :] * ybuf[pl.ds(halo - taps + 1, tm), :]
    for k in range(1, taps):
        y = y + cw_ref[k:k + 1, :] * ybuf[pl.ds(halo - taps + 1 + k, tm), :]
    hy = 0.5 * y
    y = hy + hy * jnp.tanh(hy)
    for s in range(3 * GDN_HEADS):
        sl = slice(s * GDN_DH, (s + 1) * GDN_DH)
        ys = y[:, sl]
        if s < 2 * GDN_HEADS:
            scale = GDN_DH ** -0.5 if s < GDN_HEADS else 1.0
            ys = ys * (lax.rsqrt(jnp.sum(ys * ys, axis=-1, keepdims=True) + EPS) * scale)
        oqkv_ref[:, sl] = ys.astype(BF16)


def _in_proj(x2, g, wqt, wkv, wvt, wgt, wqkv, wz, wab, qg, kg, vone, conv_w, tiles_per_seq, tm):
    n, d = x2.shape
    full = lambda a: pl.BlockSpec(a.shape, lambda i: (0,) * a.ndim)
    row = lambda w: pl.BlockSpec((tm, w), lambda i: (i, 0))
    colb = lambda r: pl.BlockSpec((r, tm), lambda i: (0, i))
    return pl.pallas_call(
        functools.partial(_inproj_body, tiles_per_seq=tiles_per_seq),
        grid=(n // tm,),
        in_specs=[row(d)] + [full(a) for a in (g, wqt, wkv, wvt, wgt, wqkv, wz, wab, qg, kg, vone, conv_w)],
        out_specs=[colb(wqt.shape[0]), row(wkv.shape[1]), colb(wvt.shape[0]), colb(wgt.shape[0]),
                   row(wqkv.shape[1]), row(wz.shape[1]), row(wab.shape[1])],
        out_shape=[jax.ShapeDtypeStruct((wqt.shape[0], n), BF16), jax.ShapeDtypeStruct((n, wkv.shape[1]), BF16),
                   jax.ShapeDtypeStruct((wvt.shape[0], n), BF16), jax.ShapeDtypeStruct((wgt.shape[0], n), F32),
                   jax.ShapeDtypeStruct((n, wqkv.shape[1]), BF16), jax.ShapeDtypeStruct((n, wz.shape[1]), BF16),
                   jax.ShapeDtypeStruct((n, wab.shape[1]), F32)],
        scratch_shapes=[pltpu.VMEM((tm + 8, wqkv.shape[1]), F32)],
        compiler_params=_cparams(("arbitrary",), 56),
        name="in_proj",
    )(x2, g, wqt, wkv, wvt, wgt, wqkv, wz, wab, qg, kg, vone, conv_w)


def _compress_body(x_ref, pos_ref, w1_ref, b1_ref, w2_ref, b2_ref, w2t_ref, b2t_ref, g_ref, ok_ref, ovt_ref,
                   *, n_cmp):
    is_key = pl.program_id(1) == 0
    nch = x_ref.shape[3]
    hids = []
    for grp in range(NSA_GROUPS):
        x = x_ref[0, 0, grp].astype(F32)
        xa = (x + pos_ref[0, 0]).astype(BF16)
        xb = (x + pos_ref[0, 1]).astype(BF16)
        a = _dot(xa, w1_ref[0, 0])
        b = _dot(xb, w1_ref[0, 1])
        b_next = pltpu.roll(b, nch - 1, 0)
        hids.append(jax.nn.gelu(a + b_next + b1_ref[0]).astype(BF16))

    @pl.when(is_key)
    def _():
        row = lax.broadcasted_iota(jnp.int32, (nch, NSA_DH), 0)
        outs = []
        for grp in range(NSA_GROUPS):
            out = _dot(hids[grp], w2_ref[0]) + b2_ref[0]
            out = out * lax.rsqrt(jnp.mean(out * out, axis=-1, keepdims=True) + EPS) * g_ref[...]
            outs.append(jnp.where(row < n_cmp, out, 0.0))
        ok_ref[0] = jnp.concatenate(outs, axis=-1).astype(BF16)

    @pl.when(jnp.logical_not(is_key))
    def _():
        col = lax.broadcasted_iota(jnp.int32, (NSA_DH, nch), 1)
        outs = []
        for grp in range(NSA_GROUPS):
            out = _dot_nt(w2t_ref[0], hids[grp]) + b2t_ref[0]
            outs.append(jnp.where(col < n_cmp, out, 0.0))
        ovt_ref[0] = jnp.concatenate(outs, axis=0).astype(BF16)


def _compress(xflat, pos, w1, b1, w2, b2, w2t, b2t, kc_g, n_cmp):
    b, _, _, nch, flat = xflat.shape
    return pl.pallas_call(
        functools.partial(_compress_body, n_cmp=n_cmp),
        grid=(b, 2),
        in_specs=[
            pl.BlockSpec((1, 1, NSA_GROUPS, nch, flat), lambda i, j: (i, j, 0, 0, 0)),
            pl.BlockSpec((1, 2, 1, flat), lambda i, j: (j, 0, 0, 0)),
            pl.BlockSpec((1, 2, flat, CMP_HIDDEN), lambda i, j: (j, 0, 0, 0)),
            pl.BlockSpec((1, 1, CMP_HIDDEN), lambda i, j: (j, 0, 0)),
            pl.BlockSpec((1, CMP_HIDDEN, NSA_DH), lambda i, j: (j, 0, 0)),
            pl.BlockSpec((1, 1, NSA_DH), lambda i, j: (j, 0, 0)),
            pl.BlockSpec((1, NSA_DH, CMP_HIDDEN), lambda i, j: (j, 0, 0)),
            pl.BlockSpec((1, NSA_DH, 1), lambda i, j: (j, 0, 0)),
            pl.BlockSpec((1, NSA_DH), lambda i, j: (0, 0)),
        ],
        out_specs=[pl.BlockSpec((1, nch, LANES), lambda i, j: (i, 0, 0)),
                   pl.BlockSpec((1, LANES, nch), lambda i, j: (i, 0, 0))],
        out_shape=[jax.ShapeDtypeStruct((b, nch, LANES), BF16), jax.ShapeDtypeStruct((b, LANES, nch), BF16)],
        compiler_params=_cparams(("parallel", "arbitrary"), 32),
        name="nsa_compress",
    )(xflat, pos, w1, b1, w2, b2, w2t, b2t, kc_g)


def _tile_heads(a):
    return jnp.concatenate([a] * NSA_HPG, axis=1)


def _nsa_body(qt_ref, gt_ref, kc_ref, vct_ref, ks_ref, kw_ref, vst_ref, vwt_ref, et_ref, ov_ref, o_ref, acc_sc, s_sc,
              *, n_cmp, n_top, kt):
    grp = pl.program_id(1)
    s0 = pl.program_id(2) * Q_BLOCK
    nch = kc_ref.shape[1]
    nblk = ov_ref.shape[0]

    qt = jnp.concatenate([qt_ref[h * LANES:(h + 1) * LANES, :] for h in range(NSA_HPG)], axis=1)
    t_row = s0 + lax.broadcasted_iota(jnp.int32, (1, Q_BLOCK), 1)

    cidx = lax.broadcasted_iota(jnp.int32, (nch, 1), 0)
    cvalid = (cidx * CMP_STRIDE + (CMP_BLOCK - 1) <= t_row) & (cidx < n_cmp)
    sc = _dot(kc_ref[0], qt) + _tile_heads(jnp.where(cvalid, 0.0, NEG))
    pc = jnp.exp2(sc - jnp.max(sc, axis=0, keepdims=True)).astype(BF16)
    stacked = jnp.concatenate([vct_ref[0], ov_ref[...], jnp.ones((8, nch), BF16)], axis=0)
    res = _dot(stacked, pc)
    inv = jnp.where(_tile_heads(t_row >= CMP_BLOCK - 1), 1.0 / jnp.maximum(res[LANES + nblk:LANES + nblk + 1], 1e-30),
                    0.0)
    oc = res[:LANES] * inv
    imp4 = res[LANES:LANES + nblk] * inv
    imp = (imp4[:, 0:Q_BLOCK] + imp4[:, Q_BLOCK:2 * Q_BLOCK] + imp4[:, 2 * Q_BLOCK:3 * Q_BLOCK]
           + imp4[:, 3 * Q_BLOCK:4 * Q_BLOCK])
    blk = lax.broadcasted_iota(jnp.int32, (nblk, Q_BLOCK), 0)
    cur = t_row // SLC_BLOCK
    imp = jnp.where(blk * SLC_BLOCK > t_row, NEG, imp)
    imp = jnp.where((blk == 0) | (blk == cur) | (blk == cur - 1), -NEG, imp)

    def pick_rounds(v, rounds):
        for _ in range(rounds):
            mx = jnp.max(v, axis=0, keepdims=True)
            first = jnp.min(jnp.where(v == mx, blk, nblk), axis=0, keepdims=True)
            v = jnp.where(blk == first, -jnp.inf, v)
        return v

    quarter = n_top // 4
    picked = pick_rounds(imp, quarter)

    wlen = WINDOW + Q_BLOCK
    w0 = pl.multiple_of(jnp.maximum(s0 - WINDOW, 0), Q_BLOCK)
    kpos = w0 + lax.broadcasted_iota(jnp.int32, (wlen, 1), 0)
    wbias = jnp.where((kpos <= t_row) & (kpos > t_row - WINDOW), 0.0, NEG)
    sw = _dot(kw_ref[0, pl.ds(w0, wlen), :], qt) + _tile_heads(wbias)
    picked = pick_rounds(picked, quarter)
    pw = jnp.exp2(sw - jnp.max(sw, axis=0, keepdims=True)).astype(BF16)
    picked = pick_rounds(picked, quarter)
    ow = _dot(vwt_ref[:, pl.ds(w0, wlen)], pw)
    ow = ow[:NSA_DH] / ow[NSA_DH:NSA_DH + 1]

    d0 = pl.multiple_of(s0, Q_BLOCK)
    dpos = s0 + lax.broadcasted_iota(jnp.int32, (Q_BLOCK, 1), 0)
    sd = _dot(ks_ref[0, pl.ds(d0, Q_BLOCK), :], qt) + _tile_heads(jnp.where(dpos <= t_row, 0.0, NEG))
    m_diag = jnp.max(sd, axis=0, keepdims=True)
    acc_sc[0] = _dot(vst_ref[:, pl.ds(d0, Q_BLOCK)], jnp.exp2(sd - m_diag).astype(BF16))
    acc_sc[1] = jnp.zeros(acc_sc.shape[1:], F32)
    picked = pick_rounds(picked, n_top - 3 * quarter)

    before = blk * SLC_BLOCK < s0
    selb = jnp.where((picked == -jnp.inf) & before, 0.0, MASKED).astype(BF16)
    rhs = jnp.concatenate([qt, _tile_heads(selb)], axis=0)

    last_tile = ks_ref.shape[1] // kt - 1

    def scores(idx, slot):
        k0 = pl.multiple_of(jnp.minimum(idx, last_tile) * kt, kt)
        lhs = jnp.concatenate([ks_ref[0, pl.ds(k0, kt), :], et_ref[pl.ds(k0, kt), :]], axis=1)
        s_sc[slot] = _dot(lhs, rhs)

    def update(idx, slot, m_old, acc_ref):
        k0 = pl.multiple_of(idx * kt, kt)
        m_new = jnp.maximum(m_old, jnp.max(s_sc[slot], axis=0, keepdims=True))
        p = jnp.exp2(s_sc[slot] - m_new).astype(BF16)
        acc_ref[...] = jnp.exp2(m_old - m_new) * acc_ref[...] + _dot(vst_ref[:, pl.ds(k0, kt)], p)
        return m_new

    def tile_group(first, carry, count):
        ms = list(carry)
        for t in range(count):
            scores(first + t + 2, (t + 2) % 4)
            ms[t % 2] = update(first + t, t % 4, ms[t % 2], acc_sc.at[t % 2])
        return tuple(ms)

    n_tiles = (s0 + kt - 1) // kt
    scores(0, 0)
    scores(1, 1)
    carry = (m_diag, jnp.full((1, NSA_HPG * Q_BLOCK), NEG, F32))
    done = 0
    for size in FLASH_BODY_TILES:
        left = n_tiles - done
        groups = (left + size - 1) // size if size == FLASH_BODY_TILES[-1] else left // size
        carry = lax.fori_loop(0, groups, lambda j, c, done=done, size=size: tile_group(done + size * j, c, size),
                              carry)
        done = done + size * groups
    m0, m1 = carry
    m_fin = jnp.maximum(m0, m1)
    acc = acc_sc[0] * jnp.exp2(m0 - m_fin) + acc_sc[1] * jnp.exp2(m1 - m_fin)
    osl = acc[:NSA_DH] / acc[NSA_DH:NSA_DH + 1]

    oc = jnp.where(grp == 0, oc[:NSA_DH], oc[NSA_DH:])
    gts = jax.nn.sigmoid(gt_ref[...])
    for h in range(NSA_HPG):
        cols = slice(h * Q_BLOCK, (h + 1) * Q_BLOCK)
        o_ref[h * NSA_DH:(h + 1) * NSA_DH, :] = (
            gts[3 * h:3 * h + 1, :] * oc[:, cols] + gts[3 * h + 1:3 * h + 2, :] * osl[:, cols]
            + gts[3 * h + 2:3 * h + 3, :] * ow[:, cols])


def _nsa_attention(qt, gt, kc, vct, okv, vt, expand_t, overlap, b, t, n_cmp, n_top, kt):
    nch = kc.shape[1]
    nq = t // Q_BLOCK
    n = b * t
    return pl.pallas_call(
        functools.partial(_nsa_body, n_cmp=n_cmp, n_top=n_top, kt=kt),
        grid=(b, NSA_GROUPS, nq),
        in_specs=[
            pl.BlockSpec((NSA_HPG * LANES, Q_BLOCK), lambda bi, g, i: (g, bi * nq + i)),
            pl.BlockSpec((GATE_ROWS, Q_BLOCK), lambda bi, g, i: (g, bi * nq + i)),
            pl.BlockSpec((1, nch, LANES), lambda bi, g, i: (bi, 0, 0)),
            pl.BlockSpec((1, LANES, nch), lambda bi, g, i: (bi, 0, 0)),
            pl.BlockSpec((1, t, LANES), lambda bi, g, i: (bi, 0, 2)),
            pl.BlockSpec((1, t, LANES), lambda bi, g, i: (bi, 0, 4)),
            pl.BlockSpec((LANES, t), lambda bi, g, i: (g, bi)),
            pl.BlockSpec((LANES, t), lambda bi, g, i: (NSA_GROUPS + g, bi)),
            pl.BlockSpec(expand_t.shape, lambda bi, g, i: (0, 0)),
            pl.BlockSpec(overlap.shape, lambda bi, g, i: (0, 0)),
        ],
        out_specs=pl.BlockSpec((NSA_HPG * NSA_DH, Q_BLOCK), lambda bi, g, i: (g, bi * nq + i)),
        out_shape=jax.ShapeDtypeStruct((NSA_W, n), F32),
        scratch_shapes=[pltpu.VMEM((2, LANES, NSA_HPG * Q_BLOCK), F32),
                        pltpu.VMEM((4, kt, NSA_HPG * Q_BLOCK), F32)],
        compiler_params=_cparams(("parallel", "parallel", "arbitrary"), 56),
        name="nsa_attention",
    )(qt, gt, kc, vct, okv, okv, vt, vt, expand_t, overlap)


def _split_bf16(a):
    hi = a.astype(BF16)
    return hi, (a - hi.astype(F32)).astype(BF16)


def _unit_lower_inverses(lmats):
    c = lmats[0].shape[0]
    r = lax.broadcasted_iota(jnp.int32, (c, c), 0)
    col = lax.broadcasted_iota(jnp.int32, (c, c), 1)
    eye = jnp.where(r == col, 1.0, 0.0)
    xs = [eye - l for l in lmats]
    ps = []
    for l in lmats:
        l16 = l.astype(BF16)
        ps.append(_dot(l16, l16))
    steps = int(np.log2(c)) - 1
    for s in range(steps):
        last = s + 1 == steps
        for i in range(len(lmats)):
            rhs = ps[i].astype(BF16)
            if last:
                xs[i] = xs[i] + _dot(xs[i].astype(BF16), rhs)
            else:
                both = _dot(jnp.concatenate([xs[i], ps[i]], axis=0).astype(BF16), rhs)
                xs[i] = xs[i] + both[:c]
                ps[i] = both[c:]
    return xs


def _gdn_body(x_ref, z_ref, ab_ref, alog_ref, dtb_ref, og_ref, o_ref, s_sc, *, ct):
    nb = x_ref.shape[0]

    @pl.when(pl.program_id(0) == 0)
    def _():
        s_sc[...] = jnp.zeros(s_sc.shape, F32)

    ch = GDN_CHUNK
    r = lax.broadcasted_iota(jnp.int32, (ch, ch), 0)
    col = lax.broadcasted_iota(jnp.int32, (ch, ch), 1)
    incl = r >= col
    strict = r > col
    tril16 = jnp.concatenate([jnp.where(incl, 1.0, 0.0).astype(BF16)] * 3, axis=1)

    units = []
    for ci in range(ct // ch):
        rows = slice(ci * ch, (ci + 1) * ch)
        for bi in range(nb):
            ab = ab_ref[bi, rows, :]
            g_all = -jnp.exp(alog_ref[...]) * jax.nn.softplus(ab + dtb_ref[...])
            beta_all = jax.nn.sigmoid(ab)
            g_hi, g_lo = _split_bf16(g_all)
            g_lo2 = (g_all - g_hi.astype(F32) - g_lo.astype(F32)).astype(BF16)
            gc_all = _dot(tril16, jnp.concatenate([g_hi, g_lo, g_lo2], axis=0))
            gc_t = gc_all.T
            for h in range(GDN_HEADS):
                hs = slice(h * GDN_DH, (h + 1) * GDN_DH)
                q16 = x_ref[bi, rows, hs]
                k16 = x_ref[bi, rows, GDN_W + h * GDN_DH:GDN_W + (h + 1) * GDN_DH]
                qh, kh = q16.astype(F32), k16.astype(F32)
                vh = x_ref[bi, rows, 2 * GDN_W + h * GDN_DH:2 * GDN_W + (h + 1) * GDN_DH].astype(F32)
                gc = gc_all[:, h:h + 1]
                gr = gc_t[h:h + 1, :]
                g_last = gc_all[ch - 1:ch, h:h + 1]
                beta = beta_all[:, GDN_HEADS + h:GDN_HEADS + h + 1]
                eg = jnp.exp(gc)
                decay = jnp.where(incl, jnp.exp(jnp.minimum(gc - gr, 0.0)), 0.0)
                kb = kh * beta
                with_k = (_dot_nt(jnp.concatenate([kb.astype(BF16), q16], axis=0), k16)
                          * jnp.concatenate([decay, decay], axis=0))
                units.append(dict(
                    rows=rows, bi=bi, h=h,
                    lmat=jnp.where(strict, with_k[:ch], 0.0),
                    vb_kbg=jnp.concatenate([(vh * beta).astype(BF16), (kb * eg).astype(BF16)], axis=1),
                    qk=jnp.where(incl, with_k[ch:], 0.0).astype(BF16),
                    qg=(qh * eg).astype(BF16), kd_t=(kh * jnp.exp(g_last - gc)).T.astype(BF16),
                    gl=jnp.exp(g_last)))
    tinvs = _unit_lower_inverses([u["lmat"] for u in units])
    for u, tinv in zip(units, tinvs):
        u_w = _dot(tinv.astype(BF16), u["vb_kbg"])
        u["u"] = u_w[:, :GDN_DH]
        u["w_qg"] = jnp.concatenate([u_w[:, GDN_DH:].astype(BF16), u["qg"]], axis=0)
        u["kd_qk"] = jnp.concatenate([u["kd_t"], u["qk"]], axis=0)

    for u in units:
        bi, h, rows = u["bi"], u["h"], u["rows"]
        hs = slice(h * GDN_DH, (h + 1) * GDN_DH)
        s_old = s_sc[bi * GDN_HEADS + h]
        from_state = _dot(u["w_qg"], s_old.astype(BF16))
        v_new = (u["u"] - from_state[:ch]).astype(BF16)
        from_v = _dot(u["kd_qk"], v_new)
        s_sc[bi * GDN_HEADS + h] = s_old * u["gl"] + from_v[:GDN_DH]
        o = from_state[ch:] + from_v[GDN_DH:]
        on = o * lax.rsqrt(jnp.mean(o * o, axis=-1, keepdims=True) + EPS) * og_ref[...]
        zh = z_ref[bi, rows, hs].astype(F32)
        o_ref[bi, rows, hs] = (on * (zh * jax.nn.sigmoid(zh))).astype(BF16)


def _gdn(oqkv, oz, oab, alog, dtb, og, ct=128):
    b, t, w3 = oqkv.shape
    full = lambda a: pl.BlockSpec(a.shape, lambda c: (0,) * a.ndim)
    return pl.pallas_call(
        functools.partial(_gdn_body, ct=ct),
        grid=(t // ct,),
        in_specs=[
            pl.BlockSpec((b, ct, w3), lambda c: (0, c, 0)),
            pl.BlockSpec((b, ct, GDN_W), lambda c: (0, c, 0)),
            pl.BlockSpec((b, ct, LANES), lambda c: (0, c, 0)),
            full(alog), full(dtb), full(og),
        ],
        out_specs=pl.BlockSpec((b, ct, GDN_W), lambda c: (0, c, 0)),
        out_shape=jax.ShapeDtypeStruct((b, t, GDN_W), BF16),
        scratch_shapes=[pltpu.VMEM((b * GDN_HEADS, GDN_DH, GDN_DH), F32)],
        compiler_params=_cparams(("arbitrary",), 32),
        name="gdn",
    )(oqkv, oz, oab, alog, dtb, og)


def _outproj_body(ont_ref, og_ref, x_ref, ng_ref, wo_ref, fg_ref, wr_ref, br_ref, upper_ref,
                  x1_ref, h2_ref, gate_ref, route_ref, cnt_ref, cnt_sc):
    i = pl.program_id(0)
    tm = x_ref.shape[0]

    @pl.when(i == 0)
    def _():
        cnt_sc[...] = jnp.zeros(cnt_sc.shape, F32)

    a = ont_ref[...]
    a = (a * lax.rsqrt(jnp.mean(a * a, axis=0, keepdims=True) + EPS) * ng_ref[...]).astype(BF16)
    x1 = x_ref[...] + _dot_tn(a, wo_ref[0:NSA_W, :]) + _dot(og_ref[...], wo_ref[NSA_W:, :])
    x1_ref[...] = x1
    h2f = x1 * lax.rsqrt(jnp.mean(x1 * x1, axis=-1, keepdims=True) + EPS) * fg_ref[...]
    _store_pieces(h2_ref, h2f)
    h2 = h2f.astype(BF16)

    logits = (_dot(h2, wr_ref[...]) + br_ref[...]).T[:N_EXPERTS]
    erow = lax.broadcasted_iota(jnp.int32, (N_EXPERTS, tm), 0)
    onehot = jnp.zeros((N_EXPERTS, tm), F32)
    firsts, vals = [], []
    v = logits
    for k in range(TOP_K):
        mx = jnp.max(v, axis=0, keepdims=True)
        first = jnp.min(jnp.where(v == mx, erow, N_EXPERTS), axis=0, keepdims=True)
        hit = erow == first
        v = jnp.where(hit, -jnp.inf, v)
        onehot = jnp.where(hit, 1.0, onehot)
        firsts.append(first)
        vals.append(mx)
    vals = [jnp.exp(m - vals[0]) for m in vals]
    inv = 1.0 / (vals[0] + vals[1] + vals[2] + vals[3])
    gates_t = jnp.concatenate([m * inv for m in vals] + [jnp.zeros((LANES - TOP_K, tm), F32)], axis=0)
    gate_ref[...] = gates_t.T

    excl = cnt_sc[...] + _dot(onehot.astype(BF16), upper_ref[...])
    for k in range(TOP_K):
        route_ref[k:k + 1, :] = firsts[k]
        rank = jnp.sum(jnp.where(erow == firsts[k], excl, 0.0), axis=0, keepdims=True)
        route_ref[TOP_K + k:TOP_K + k + 1, :] = rank.astype(jnp.int32)
    cnt_sc[...] = cnt_sc[...] + jnp.sum(onehot, axis=1, keepdims=True)
    cnt_ref[...] = cnt_sc[...].astype(jnp.int32)


def _out_proj(o_nsa_t, o_gdn, x2, ng, wo, fg, wr, br, tm=512):
    upper = jnp.asarray(np.arange(tm)[:, None] < np.arange(tm)[None, :], BF16)
    n, d = x2.shape
    full = lambda a: pl.BlockSpec(a.shape, lambda i: (0,) * a.ndim)
    row = lambda w: pl.BlockSpec((tm, w), lambda i: (i, 0))
    return pl.pallas_call(
        _outproj_body,
        grid=(n // tm,),
        in_specs=[pl.BlockSpec((NSA_W, tm), lambda i: (0, i)), row(GDN_W), row(d), full(ng), full(wo), full(fg),
                  full(wr), full(br), full(upper)],
        out_specs=[row(d), pl.BlockSpec((d // 2 // SC_SUBROW, tm, SC_SUBROW), lambda i: (0, i, 0)),
                   row(LANES), pl.BlockSpec((2 * TOP_K, tm), lambda i: (0, i)),
                   pl.BlockSpec((N_EXPERTS, 1), lambda i: (0, 0))],
        out_shape=[jax.ShapeDtypeStruct((n, d), F32),
                   jax.ShapeDtypeStruct((d // 2 // SC_SUBROW, n, SC_SUBROW), jnp.int32),
                   jax.ShapeDtypeStruct((n, LANES), F32), jax.ShapeDtypeStruct((2 * TOP_K, n), jnp.int32),
                   jax.ShapeDtypeStruct((N_EXPERTS, 1), jnp.int32)],
        scratch_shapes=[pltpu.VMEM((N_EXPERTS, 1), F32)],
        compiler_params=_cparams(("arbitrary",), 48),
        name="out_proj_router",
    )(o_nsa_t, o_gdn, x2, ng, wo, fg, wr, br, upper)


def _dest_body(ps_ref, route_ref, o_ref, *, n_rows, pieces):
    expert = route_ref[0:TOP_K, :]
    start = jnp.zeros(expert.shape, jnp.int32)
    for e in range(N_EXPERTS):
        start = jnp.where(expert == e, ps_ref[e], start)
    dest = start + route_ref[TOP_K:2 * TOP_K, :]
    for k in range(TOP_K):
        for j in range(pieces):
            o_ref[k * pieces + j:k * pieces + j + 1, :] = dest[k:k + 1, :] + j * n_rows


def _dest_rows(pstarts, route, n_rows, pieces):
    n = route.shape[1]
    tn = min(2048, n)
    grid_spec = pltpu.PrefetchScalarGridSpec(
        num_scalar_prefetch=1,
        grid=(n // tn,),
        in_specs=[pl.BlockSpec((2 * TOP_K, tn), lambda i, ps: (0, i))],
        out_specs=pl.BlockSpec((TOP_K * pieces, tn), lambda i, ps: (0, i)),
    )
    return pl.pallas_call(
        functools.partial(_dest_body, n_rows=n_rows, pieces=pieces),
        grid_spec=grid_spec,
        out_shape=jax.ShapeDtypeStruct((TOP_K * pieces, n), jnp.int32),
        name="moe_dest_rows",
    )(pstarts, route)


def _expert_body(be_ref, fresh_ref, slot_ref, next_ref, xs_ref, wg_hbm, bg_ref, wu_hbm, bu_ref, wd_hbm, bd_ref,
                 y_ref, wf32, w16, sems):
    i = pl.program_id(0)
    used = i < be_ref[pl.num_programs(0)]
    hbm = (wg_hbm, wu_hbm, wd_hbm)

    def weight_copy(expert, slot, j):
        return pltpu.make_async_copy(hbm[j].at[expert], wf32.at[slot, j], sems.at[slot, j])

    @pl.when((i == 0) & used)
    def _():
        for j in range(3):
            weight_copy(be_ref[0], 0, j).start()

    @pl.when(used & (fresh_ref[i] == 1))
    def _():
        slot = slot_ref[i]
        for j in range(3):
            weight_copy(be_ref[i], slot, j).wait()
            w16[j] = wf32[slot, j].astype(BF16)

        @pl.when(next_ref[i] >= 0)
        def _():
            for j in range(3):
                weight_copy(next_ref[i], 1 - slot, j).start()

    @pl.when(jnp.logical_not(used))
    def _():
        y_ref[...] = jnp.zeros(y_ref.shape, y_ref.dtype)

    @pl.when(used)
    def _():
        x = _join_pieces(xs_ref).astype(BF16)
        gate = jnp.minimum(_dot(x, w16[0]) + bg_ref[0], SWIGLU_LIMIT)
        up = jnp.clip(_dot(x, w16[1]) + bu_ref[0], -SWIGLU_LIMIT, SWIGLU_LIMIT)
        glu = gate * jax.nn.sigmoid(gate * SWIGLU_ALPHA)
        _store_pieces(y_ref, _dot(((up + 1.0) * glu).astype(BF16), w16[2]) + bd_ref[0])


def _experts(blk_e, fresh, slot, nxt, xs, wg, bg, wu, bu, wd, bd):
    pieces, n_rows, sub = xs.shape
    d, de = wg.shape[1], wg.shape[2]
    assert d == de
    r = MOE_ROW_BLOCK
    bspec = lambda w: pl.BlockSpec((1, 1, w), lambda i, be, *_: (be[i], 0, 0))
    hbm = pl.BlockSpec(memory_space=pl.ANY)
    grid_spec = pltpu.PrefetchScalarGridSpec(
        num_scalar_prefetch=4,
        grid=(n_rows // r,),
        in_specs=[pl.BlockSpec((pieces, r, sub), lambda i, *_: (0, i, 0)),
                  hbm, bspec(de), hbm, bspec(de), hbm, bspec(d)],
        out_specs=pl.BlockSpec((pieces, r, sub), lambda i, *_: (0, i, 0)),
        scratch_shapes=[pltpu.VMEM((2, 3, d, de), F32), pltpu.VMEM((3, d, de), BF16),
                        pltpu.SemaphoreType.DMA((2, 3))],
    )
    return pl.pallas_call(
        _expert_body,
        grid_spec=grid_spec,
        out_shape=jax.ShapeDtypeStruct((pieces, n_rows, sub), jnp.int32),
        compiler_params=_cparams(("arbitrary",), 56),
        name="moe_experts",
    )(blk_e, fresh, slot, nxt, xs, wg, bg, wu, bu, wd, bd)


SC_WINDOW = 128
SC_SUBROW = 256


def _sc_mesh():
    return plsc.VectorSubcoreMesh(core_axis_name="c", subcore_axis_name="s")


def _sc_dispatch(h2, dest_rows, n_rows):
    n, d = h2.shape

    @functools.partial(pl.kernel, out_type=jax.ShapeDtypeStruct((n_rows, d), h2.dtype), mesh=_sc_mesh())
    def dispatch(x_hbm, *refs):
        idx_hbm, o_hbm = refs[:TOP_K], refs[TOP_K]

        def body(x_vmem, *idx_vmem):
            for iv in idx_vmem:
                pltpu.sync_copy(x_vmem, o_hbm.at[iv.at[0]])

        pltpu.emit_pipeline(
            body,
            grid=(n // SC_WINDOW,),
            in_specs=[pl.BlockSpec((SC_WINDOW, d), lambda i: (i, 0))]
                     + [pl.BlockSpec((1, SC_WINDOW), lambda i: (0, i))] * TOP_K,
            out_specs=[],
            core_axis_name=("c", "s"),
            dimension_semantics=(pltpu.PARALLEL,),
        )(x_hbm, *idx_hbm)

    return dispatch(h2, *dest_rows)


def _sc_gather(table, idx):
    _, d = table.shape
    m = idx.shape[1]

    @functools.partial(pl.kernel, out_type=jax.ShapeDtypeStruct((m, d), table.dtype), mesh=_sc_mesh())
    def gather(t_hbm, i_hbm, o_hbm):
        def body(i_vmem, o_vmem):
            pltpu.sync_copy(t_hbm.at[i_vmem.at[0]], o_vmem)

        pltpu.emit_pipeline(
            body,
            grid=(m // SC_WINDOW,),
            in_specs=[pl.BlockSpec((1, SC_WINDOW), lambda i: (0, i))],
            out_specs=[pl.BlockSpec((SC_WINDOW, d), lambda i: (i, 0))],
            core_axis_name=("c", "s"),
            dimension_semantics=(pltpu.PARALLEL,),
        )(i_hbm, o_hbm)

    return gather(table, idx)


def _combine_body(x1_ref, y_ref, gate_ref, o_ref):
    acc = x1_ref[...]
    for k in range(TOP_K):
        acc = acc + gate_ref[:, k:k + 1] * _join_pieces(y_ref.at[k])
    o_ref[...] = acc


def _combine(x1, y4, gates, tm=512):
    n, d = x1.shape
    pieces, sub = y4.shape[1], y4.shape[3]
    row = lambda w: pl.BlockSpec((tm, w), lambda i: (i, 0))
    return pl.pallas_call(
        _combine_body,
        grid=(n // tm,),
        in_specs=[row(d), pl.BlockSpec((TOP_K, pieces, tm, sub), lambda i: (0, 0, i, 0)), row(LANES)],
        out_specs=row(d),
        out_shape=jax.ShapeDtypeStruct((n, d), F32),
        compiler_params=_cparams(("parallel",), 48),
        name="moe_combine",
    )(x1, y4, gates)


def _pad_lanes(a, width=LANES):
    return jnp.pad(a, ((0, 0), (0, width - a.shape[1])))


def _layer(x, attn_norm_g, w_in, q_g, kc_g, ks_g, kw_g, ck_pos, ck_w1, ck_b1, ck_w2, ck_b2,
           cv_pos, cv_w1, cv_b1, cv_w2, cv_b2, nsa_out_g, conv_w, a_log, dt_bias, gdn_out_g, w_out,
           ffn_g, router_w, router_b, e_wg, e_bg, e_wu, e_bu, e_wd, e_bd):
    b, t, d = x.shape
    n = b * t
    x2 = x.reshape(n, d)

    o = np.cumsum([0, NSA_W] + [NSA_GROUPS * NSA_DH] * 6 + [3 * NSA_HEADS, 3 * GDN_W, GDN_W, GDN_HEADS, GDN_HEADS])
    wq_t = w_in[:, o[0]:o[1]].T.reshape(NSA_GROUPS, NSA_HPG, NSA_DH, d)
    zq = jnp.zeros((NSA_HPG, NSA_DH, d), F32)
    wq_t = jnp.stack([jnp.concatenate([wq_t[0], zq], axis=1), jnp.concatenate([zq, wq_t[1]], axis=1)])
    wq_t = wq_t.reshape(NSA_HEADS * LANES, d).astype(BF16)
    qg1 = q_g * (NSA_DH ** -0.5 * np.log2(np.e))
    zg = jnp.zeros((NSA_DH,), F32)
    qg_col = jnp.concatenate([jnp.tile(jnp.concatenate([qg1, zg]), NSA_HPG),
                              jnp.tile(jnp.concatenate([zg, qg1]), NSA_HPG)]).reshape(NSA_HEADS * LANES, 1)
    wkv = w_in[:, o[1]:o[7]].astype(BF16)
    ones = jnp.ones((LANES,), F32)
    kg = jnp.concatenate([ones, ones, ks_g, ks_g, ones, kw_g, kw_g, ones]).reshape(1, 6 * LANES)
    wv_t = jnp.concatenate([w_in[:, o[4]:o[5]], w_in[:, o[6]:o[7]]], axis=1).T.reshape(2 * NSA_GROUPS, NSA_DH, d)
    wv_t = jnp.pad(wv_t, ((0, 0), (0, LANES - NSA_DH), (0, 0))).reshape(2 * NSA_GROUPS * LANES, d).astype(BF16)
    vone = jnp.asarray((np.arange(2 * NSA_GROUPS * LANES) % LANES == NSA_DH).astype(np.float32)[:, None])
    wg_t = w_in[:, o[7]:o[8]].T.reshape(NSA_GROUPS, NSA_HPG * 3, d)
    wg_t = jnp.pad(wg_t, ((0, 0), (0, GATE_ROWS - NSA_HPG * 3), (0, 0))).reshape(NSA_GROUPS * GATE_ROWS, d)
    wg_t = wg_t.astype(BF16)
    wab = _pad_lanes(w_in[:, o[10]:o[12]]).astype(BF16)
    wqkv = w_in[:, o[8]:o[9]].astype(BF16)
    wz = w_in[:, o[9]:o[10]].astype(BF16)

    tm = min(512, t)
    oqt, okv, ovt, ogt, oqkv, oz, oab = _in_proj(x2, attn_norm_g.reshape(1, d), wq_t, wkv, wv_t, wg_t, wqkv, wz,
                                                 wab, qg_col, kg, vone, conv_w, t // tm, tm)

    nch = t // CMP_STRIDE
    n_cmp = (t - CMP_BLOCK) // CMP_STRIDE + 1
    half = CMP_STRIDE * NSA_DH
    xflat = okv[:, :2 * LANES].reshape(b, nch, CMP_STRIDE, 2, NSA_GROUPS, NSA_DH)
    xflat = xflat.transpose(0, 3, 4, 1, 2, 5).reshape(b, 2, NSA_GROUPS, nch, half)
    pos = jnp.stack([ck_pos, cv_pos]).reshape(2, 2, 1, half)
    w1 = jnp.stack([ck_w1, cv_w1]).reshape(2, 2, half, CMP_HIDDEN).astype(BF16)
    b1 = jnp.stack([ck_b1, cv_b1]).reshape(2, 1, CMP_HIDDEN)
    w2 = jnp.stack([ck_w2, cv_w2]).astype(BF16)
    b2 = jnp.stack([ck_b2, cv_b2]).reshape(2, 1, NSA_DH)
    w2t = jnp.stack([ck_w2.T, cv_w2.T]).astype(BF16)
    b2t = jnp.stack([ck_b2, cv_b2]).reshape(2, NSA_DH, 1)
    kc, vct = _compress(xflat, pos, w1, b1, w2, b2, w2t, b2t, kc_g.reshape(1, NSA_DH), n_cmp)

    n_slc = t // SLC_BLOCK
    n_top = min(SLC_TOPK, n_slc)
    nblk = max(n_slc, LANES)
    kt = min(256, t // 4)
    assert (t // kt) % 4 == 0
    ci = np.arange(nch)[None, :] * CMP_STRIDE
    sj = np.arange(nblk)[:, None] * SLC_BLOCK
    overlap = ((ci < sj + SLC_BLOCK) & (ci + CMP_BLOCK > sj) & (np.arange(nch)[None, :] < n_cmp)
               & (np.arange(nblk)[:, None] < n_slc))
    expand_t = (np.arange(t)[:, None] // SLC_BLOCK) == np.arange(nblk)[None, :]
    o_nsa_t = _nsa_attention(oqt, ogt, kc, vct, okv.reshape(b, t, -1), ovt, jnp.asarray(expand_t, BF16),
                             jnp.asarray(overlap, BF16), b, t, n_cmp, n_top, kt)

    alog_row = _pad_lanes(a_log.reshape(1, GDN_HEADS))
    dtb_row = _pad_lanes(dt_bias.reshape(1, GDN_HEADS))
    o_gdn = _gdn(oqkv.reshape(b, t, -1), oz.reshape(b, t, -1), oab.reshape(b, t, -1),
                 alog_row, dtb_row, gdn_out_g.reshape(1, GDN_DH))

    wr = _pad_lanes(router_w).astype(BF16)
    br = _pad_lanes(router_b.reshape(1, N_EXPERTS))
    x1, h2, gates, route, counts = _out_proj(
        o_nsa_t, o_gdn.reshape(n, GDN_W), x2, nsa_out_g.reshape(NSA_W, 1),
        w_out.astype(BF16), ffn_g.reshape(1, d), wr, br)

    r = MOE_ROW_BLOCK
    nk = n * TOP_K
    counts = counts[:, 0]
    pcounts = (counts + r - 1) // r * r
    pends = jnp.cumsum(pcounts)
    pstarts = pends - pcounts
    n_rows = (nk + r - 1) // r * r + N_EXPERTS * r
    n_blocks = n_rows // r
    blk_start = jnp.arange(n_blocks, dtype=jnp.int32)[:, None] * r
    blk_e = jnp.minimum(jnp.sum(pends[None, :] <= blk_start, axis=1), N_EXPERTS - 1).astype(jnp.int32)
    n_used = (pends[-1] // r).astype(jnp.int32)
    fresh = (jnp.arange(n_blocks) < n_used) & (blk_e != jnp.concatenate([jnp.full((1,), -1, jnp.int32), blk_e[:-1]]))
    slot = ((jnp.cumsum(fresh) - 1) % 2).astype(jnp.int32)
    eid = jnp.arange(N_EXPERTS, dtype=jnp.int32)
    later = jnp.where((eid[None, :] > eid[:, None]) & (pcounts[None, :] > 0), eid[None, :], N_EXPERTS)
    next_expert = jnp.min(later, axis=1)
    next_expert = jnp.where(next_expert < N_EXPERTS, next_expert, -1).astype(jnp.int32)
    nxt = next_expert[blk_e]
    blk_e = jnp.concatenate([blk_e, n_used[None]])
    pieces = d // 2 // SC_SUBROW
    dest_p = _dest_rows(pstarts.astype(jnp.int32), route, n_rows, pieces).reshape(TOP_K, pieces, n)
    xs = _sc_dispatch(h2.reshape(pieces * n, SC_SUBROW), [dest_p[k].reshape(1, pieces * n) for k in range(TOP_K)],
                      pieces * n_rows)
    ys = _experts(blk_e, fresh.astype(jnp.int32), slot, nxt, xs.reshape(pieces, n_rows, SC_SUBROW), e_wg, e_bg.reshape(N_EXPERTS, 1, -1), e_wu,
                  e_bu.reshape(N_EXPERTS, 1, -1), e_wd, e_bd.reshape(N_EXPERTS, 1, -1))
    y4 = _sc_gather(ys.reshape(pieces * n_rows, SC_SUBROW), dest_p.reshape(1, nk * pieces))
    return _combine(x1, y4.reshape(TOP_K, pieces, n, SC_SUBROW), gates).reshape(b, t, d)


def kernel(x, attn_norm_g, w_in, nsa_q_norm_g, nsa_kc_norm_g, nsa_ks_norm_g, nsa_kw_norm_g, cmp_k_pos, cmp_k_w1, cmp_k_b1, cmp_k_w2, cmp_k_b2, cmp_v_pos, cmp_v_w1, cmp_v_b1, cmp_v_w2, cmp_v_b2, nsa_out_norm_g, gdn_conv_w, gdn_a_log, gdn_dt_bias, gdn_out_norm_g, w_out, ffn_norm_g, router_w, router_b, exp_w_gate, exp_b_gate, exp_w_up, exp_b_up, exp_w_down, exp_b_down):
    params = (attn_norm_g, w_in, nsa_q_norm_g, nsa_kc_norm_g, nsa_ks_norm_g, nsa_kw_norm_g,
              cmp_k_pos, cmp_k_w1, cmp_k_b1, cmp_k_w2, cmp_k_b2, cmp_v_pos, cmp_v_w1, cmp_v_b1, cmp_v_w2, cmp_v_b2,
              nsa_out_norm_g, gdn_conv_w, gdn_a_log, gdn_dt_bias, gdn_out_norm_g, w_out, ffn_norm_g,
              router_w, router_b, exp_w_gate, exp_b_gate, exp_w_up, exp_b_up, exp_w_down, exp_b_down)
    for l in range(attn_norm_g.shape[0]):
        x = _layer(x, *(p[l] for p in params))
    return x
```

```python
import functools

import jax
import jax.numpy as jnp
import numpy as np
from jax import lax
from jax.experimental import pallas as pl
from jax.experimental.pallas import tpu as pltpu
from jax.experimental.pallas import tpu_sc as plsc

F32 = jnp.float32
BF16 = jnp.bfloat16

EPS = 1e-6
NEG = -1e30
MASKED = -2.0 ** 100

NSA_HEADS = 8
NSA_GROUPS = 2
NSA_HPG = 4
NSA_DH = 64
CMP_BLOCK = 32
CMP_STRIDE = 16
CMP_HIDDEN = 256
SLC_BLOCK = 64
SLC_TOPK = 16
WINDOW = 512
NSA_Q = 256
GDN_HEADS = 4
GDN_DH = 128
GDN_CONV = 4
GDN_CHUNK = 64
N_EXPERTS = 32
TOP_K = 4
SWIGLU_LIMIT = 7.0
SWIGLU_ALPHA = 1.702
MOE_ROW_BLOCK = 256

LANES = 128
GATE_ROWS = 16
FLASH_BODY_TILES = (16, 8, 4)
NSA_W = NSA_HEADS * NSA_DH
GDN_W = GDN_HEADS * GDN_DH

_NT = (((1,), (1,)), ((), ()))
_TN = (((0,), (0,)), ((), ()))


def _cparams(sem, vmem_mb):
    return pltpu.CompilerParams(dimension_semantics=sem, vmem_limit_bytes=vmem_mb * 1024 * 1024)


def _dot(a, b):
    return jnp.dot(a, b, preferred_element_type=F32)


def _dot_nt(a, b):
    return lax.dot_general(a, b, _NT, preferred_element_type=F32)


def _dot_tn(a, b):
    return lax.dot_general(a, b, _TN, preferred_element_type=F32)


def _store_pieces(ref, val):
    half = val.shape[1] // 2
    hi = lax.bitcast_convert_type(val[:, :half].astype(BF16).astype(F32), jnp.uint32)
    lo = lax.bitcast_convert_type(val[:, half:].astype(BF16).astype(F32), jnp.uint32)
    words = lax.bitcast_convert_type(hi | (lo >> 16), jnp.int32)
    sub = ref.shape[2]
    for j in range(ref.shape[0]):
        ref[j] = words[:, j * sub:(j + 1) * sub]


def _join_pieces(ref):
    words = jnp.concatenate([ref[j] for j in range(ref.shape[0])], axis=1)
    words = lax.bitcast_convert_type(words, jnp.uint32)
    hi = lax.bitcast_convert_type(words & jnp.uint32(0xFFFF0000), F32)
    lo = lax.bitcast_convert_type(words << 16, F32)
    return jnp.concatenate([hi, lo], axis=1)


def _inproj_body(x_ref, g_ref, wqt_ref, wkv_ref, wvt_ref, wgt_ref, wqkv_ref, wz_ref, wab_ref, qg_ref, kg_ref,
                 vone_ref, cw_ref, oqt_ref, okv_ref, ovt_ref, ogt_ref, oqkv_ref, oz_ref, oab_ref, ybuf,
                 *, tiles_per_seq):
    x = x_ref[...]
    h = (x * lax.rsqrt(jnp.mean(x * x, axis=-1, keepdims=True) + EPS) * g_ref[...]).astype(BF16)
    tm = x.shape[0]

    yq = _dot_nt(wqt_ref[...], h)
    for s in range(NSA_HEADS):
        sl = slice(s * LANES, (s + 1) * LANES)
        ys = yq[sl, :]
        ms = jnp.sum(ys * ys, axis=0, keepdims=True) * (1.0 / NSA_DH)
        oqt_ref[sl, :] = (ys * lax.rsqrt(ms + EPS) * qg_ref[sl, :]).astype(BF16)

    ykv = _dot(h, wkv_ref[...])
    lane = lax.broadcasted_iota(jnp.int32, (tm, LANES), 1)
    low = lane < NSA_DH
    for s in range(6):
        sl = slice(s * LANES, (s + 1) * LANES)
        ys = ykv[:, sl]
        if s in (2, 4):
            y2 = ys * ys
            s0 = jnp.sum(jnp.where(low, y2, 0.0), axis=-1, keepdims=True)
            s1 = jnp.sum(jnp.where(low, 0.0, y2), axis=-1, keepdims=True)
            ms = jnp.where(low, s0, s1) * (1.0 / NSA_DH)
            ys = ys * lax.rsqrt(ms + EPS) * kg_ref[:, sl]
        okv_ref[:, sl] = ys.astype(BF16)

    ovt_ref[...] = (_dot_nt(wvt_ref[...], h) + vone_ref[...]).astype(BF16)
    ogt_ref[...] = _dot_nt(wgt_ref[...], h)
    oz_ref[...] = _dot(h, wz_ref[...]).astype(BF16)
    oab_ref[...] = _dot(h, wab_ref[...])

    halo = ybuf.shape[0] - tm
    first = pl.program_id(0) % tiles_per_seq == 0

    @pl.when(first)
    def _():
        ybuf[0:halo, :] = jnp.zeros((halo, ybuf.shape[1]), F32)

    @pl.when(jnp.logical_not(first))
    def _():
        ybuf[0:halo, :] = ybuf[tm:tm + halo, :]

    ybuf[halo:halo + tm, :] = _dot(h, wqkv_ref[...])
    taps = cw_ref.shape[0]
    y = cw_ref[0:1, :] * ybuf[pl.ds(halo - taps + 1, tm), :]
    for k in range(1, taps):
        y = y + cw_ref[k:k + 1, :] * ybuf[pl.ds(halo - taps + 1 + k, tm), :]
    hy = 0.5 * y
    y = hy + hy * jnp.tanh(hy)
    for s in range(3 * GDN_HEADS):
        sl = slice(s * GDN_DH, (s + 1) * GDN_DH)
        ys = y[:, sl]
        if s < 2 * GDN_HEADS:
            scale = GDN_DH ** -0.5 if s < GDN_HEADS else 1.0
            ys = ys * (lax.rsqrt(jnp.sum(ys * ys, axis=-1, keepdims=True) + EPS) * scale)
        oqkv_ref[:, sl] = ys.astype(BF16)


def _in_proj(x2, g, wqt, wkv, wvt, wgt, wqkv, wz, wab, qg, kg, vone, conv_w, tiles_per_seq, tm):
    n, d = x2.shape
    full = lambda a: pl.BlockSpec(a.shape, lambda i: (0,) * a.ndim)
    row = lambda w: pl.BlockSpec((tm, w), lambda i: (i, 0))
    colb = lambda r: pl.BlockSpec((r, tm), lambda i: (0, i))
    return pl.pallas_call(
        functools.partial(_inproj_body, tiles_per_seq=tiles_per_seq),
        grid=(n // tm,),
        in_specs=[row(d)] + [full(a) for a in (g, wqt, wkv, wvt, wgt, wqkv, wz, wab, qg, kg, vone, conv_w)],
        out_specs=[colb(wqt.shape[0]), row(wkv.shape[1]), colb(wvt.shape[0]), colb(wgt.shape[0]),
                   row(wqkv.shape[1]), row(wz.shape[1]), row(wab.shape[1])],
        out_shape=[jax.ShapeDtypeStruct((wqt.shape[0], n), BF16), jax.ShapeDtypeStruct((n, wkv.shape[1]), BF16),
                   jax.ShapeDtypeStruct((wvt.shape[0], n), BF16), jax.ShapeDtypeStruct((wgt.shape[0], n), F32),
                   jax.ShapeDtypeStruct((n, wqkv.shape[1]), BF16), jax.ShapeDtypeStruct((n, wz.shape[1]), BF16),
                   jax.ShapeDtypeStruct((n, wab.shape[1]), F32)],
        scratch_shapes=[pltpu.VMEM((tm + 8, wqkv.shape[1]), F32)],
        compiler_params=_cparams(("arbitrary",), 56),
        name="in_proj",
    )(x2, g, wqt, wkv, wvt, wgt, wqkv, wz, wab, qg, kg, vone, conv_w)


def _compress_body(x_ref, pos_ref, w1_ref, b1_ref, w2_ref, b2_ref, w2t_ref, b2t_ref, g_ref, ok_ref, ovt_ref,
                   *, n_cmp):
    is_key = pl.program_id(1) == 0
    nch = x_ref.shape[3]
    hids = []
    for grp in range(NSA_GROUPS):
        x = x_ref[0, 0, grp].astype(F32)
        xa = (x + pos_ref[0, 0]).astype(BF16)
        xb = (x + pos_ref[0, 1]).astype(BF16)
        a = _dot(xa, w1_ref[0, 0])
        b = _dot(xb, w1_ref[0, 1])
        b_next = pltpu.roll(b, nch - 1, 0)
        hids.append(jax.nn.gelu(a + b_next + b1_ref[0]).astype(BF16))

    @pl.when(is_key)
    def _():
        row = lax.broadcasted_iota(jnp.int32, (nch, NSA_DH), 0)
        outs = []
        for grp in range(NSA_GROUPS):
            out = _dot(hids[grp], w2_ref[0]) + b2_ref[0]
            out = out * lax.rsqrt(jnp.mean(out * out, axis=-1, keepdims=True) + EPS) * g_ref[...]
            outs.append(jnp.where(row < n_cmp, out, 0.0))
        ok_ref[0] = jnp.concatenate(outs, axis=-1).astype(BF16)

    @pl.when(jnp.logical_not(is_key))
    def _():
        col = lax.broadcasted_iota(jnp.int32, (NSA_DH, nch), 1)
        outs = []
        for grp in range(NSA_GROUPS):
            out = _dot_nt(w2t_ref[0], hids[grp]) + b2t_ref[0]
            outs.append(jnp.where(col < n_cmp, out, 0.0))
        ovt_ref[0] = jnp.concatenate(outs, axis=0).astype(BF16)


def _compress(xflat, pos, w1, b1, w2, b2, w2t, b2t, kc_g, n_cmp):
    b, _, _, nch, flat = xflat.shape
    return pl.pallas_call(
        functools.partial(_compress_body, n_cmp=n_cmp),
        grid=(b, 2),
        in_specs=[
            pl.BlockSpec((1, 1, NSA_GROUPS, nch, flat), lambda i, j: (i, j, 0, 0, 0)),
            pl.BlockSpec((1, 2, 1, flat), lambda i, j: (j, 0, 0, 0)),
            pl.BlockSpec((1, 2, flat, CMP_HIDDEN), lambda i, j: (j, 0, 0, 0)),
            pl.BlockSpec((1, 1, CMP_HIDDEN), lambda i, j: (j, 0, 0)),
            pl.BlockSpec((1, CMP_HIDDEN, NSA_DH), lambda i, j: (j, 0, 0)),
            pl.BlockSpec((1, 1, NSA_DH), lambda i, j: (j, 0, 0)),
            pl.BlockSpec((1, NSA_DH, CMP_HIDDEN), lambda i, j: (j, 0, 0)),
            pl.BlockSpec((1, NSA_DH, 1), lambda i, j: (j, 0, 0)),
            pl.BlockSpec((1, NSA_DH), lambda i, j: (0, 0)),
        ],
        out_specs=[pl.BlockSpec((1, nch, LANES), lambda i, j: (i, 0, 0)),
                   pl.BlockSpec((1, LANES, nch), lambda i, j: (i, 0, 0))],
        out_shape=[jax.ShapeDtypeStruct((b, nch, LANES), BF16), jax.ShapeDtypeStruct((b, LANES, nch), BF16)],
        compiler_params=_cparams(("parallel", "arbitrary"), 32),
        name="nsa_compress",
    )(xflat, pos, w1, b1, w2, b2, w2t, b2t, kc_g)


def _tile_heads(a):
    return jnp.concatenate([a] * NSA_HPG, axis=1)


def _nsa_body(qt_ref, gt_ref, kc_ref, vct_ref, ks_ref, kw_ref, vst_ref, vwt_ref, et_ref, ov_ref, o_ref, acc_sc, s_sc,
              *, n_cmp, n_top, kt):
    grp = pl.program_id(1)
    s0 = pl.program_id(2) * NSA_Q
    nch = kc_ref.shape[1]
    nblk = ov_ref.shape[0]

    qt = jnp.concatenate([qt_ref[h * LANES:(h + 1) * LANES, :] for h in range(NSA_HPG)], axis=1)
    t_row = s0 + lax.broadcasted_iota(jnp.int32, (1, NSA_Q), 1)

    cidx = lax.broadcasted_iota(jnp.int32, (nch, 1), 0)
    cvalid = (cidx * CMP_STRIDE + (CMP_BLOCK - 1) <= t_row) & (cidx < n_cmp)
    sc = _dot(kc_ref[0], qt) + _tile_heads(jnp.where(cvalid, 0.0, NEG))
    pc = jnp.exp2(sc - jnp.max(sc, axis=0, keepdims=True)).astype(BF16)
    stacked = jnp.concatenate([vct_ref[0], ov_ref[...], jnp.ones((8, nch), BF16)], axis=0)
    res = _dot(stacked, pc)
    inv = jnp.where(_tile_heads(t_row >= CMP_BLOCK - 1), 1.0 / jnp.maximum(res[LANES + nblk:LANES + nblk + 1], 1e-30),
                    0.0)
    oc = res[:LANES] * inv
    imp4 = res[LANES:LANES + nblk] * inv
    imp = (imp4[:, 0:NSA_Q] + imp4[:, NSA_Q:2 * NSA_Q] + imp4[:, 2 * NSA_Q:3 * NSA_Q]
           + imp4[:, 3 * NSA_Q:4 * NSA_Q])
    blk = lax.broadcasted_iota(jnp.int32, (nblk, NSA_Q), 0)
    cur = t_row // SLC_BLOCK
    imp = jnp.where(blk * SLC_BLOCK > t_row, NEG, imp)
    imp = jnp.where((blk == 0) | (blk == cur) | (blk == cur - 1), -NEG, imp)

    def pick_rounds(v, rounds):
        for _ in range(rounds):
            mx = jnp.max(v, axis=0, keepdims=True)
            first = jnp.min(jnp.where(v == mx, blk, nblk), axis=0, keepdims=True)
            v = jnp.where(blk == first, -jnp.inf, v)
        return v

    quarter = n_top // 4
    picked = pick_rounds(imp, quarter)

    wlen = WINDOW + NSA_Q
    w0 = pl.multiple_of(jnp.maximum(s0 - WINDOW, 0), NSA_Q)
    kpos = w0 + lax.broadcasted_iota(jnp.int32, (wlen, 1), 0)
    wbias = jnp.where((kpos <= t_row) & (kpos > t_row - WINDOW), 0.0, NEG)
    sw = _dot(kw_ref[0, pl.ds(w0, wlen), :], qt) + _tile_heads(wbias)
    picked = pick_rounds(picked, quarter)
    pw = jnp.exp2(sw - jnp.max(sw, axis=0, keepdims=True)).astype(BF16)
    picked = pick_rounds(picked, quarter)
    ow = _dot(vwt_ref[:, pl.ds(w0, wlen)], pw)
    ow = ow[:NSA_DH] / ow[NSA_DH:NSA_DH + 1]

    picked = pick_rounds(picked, n_top - 3 * quarter)
    chosen = picked == -jnp.inf

    selb = jnp.where(chosen & (blk * SLC_BLOCK < s0), 0.0, MASKED).astype(BF16)
    rhs = jnp.concatenate([qt, _tile_heads(selb)], axis=0)

    d0 = pl.multiple_of(s0, NSA_Q)
    dpos = s0 + lax.broadcasted_iota(jnp.int32, (NSA_Q, 1), 0)
    selb_d = jnp.where(chosen & (blk <= cur), 0.0, MASKED).astype(BF16)
    sd = (_dot(jnp.concatenate([ks_ref[0, pl.ds(d0, NSA_Q), :], et_ref[pl.ds(d0, NSA_Q), :]], axis=1),
               jnp.concatenate([qt, _tile_heads(selb_d)], axis=0))
          + _tile_heads(jnp.where(dpos <= t_row, 0.0, NEG)))
    m_diag = jnp.max(sd, axis=0, keepdims=True)
    acc_sc[0] = _dot(vst_ref[:, pl.ds(d0, NSA_Q)], jnp.exp2(sd - m_diag).astype(BF16))
    acc_sc[1] = jnp.zeros(acc_sc.shape[1:], F32)

    last_tile = ks_ref.shape[1] // kt - 1

    def scores(idx, slot):
        k0 = pl.multiple_of(jnp.minimum(idx, last_tile) * kt, kt)
        lhs = jnp.concatenate([ks_ref[0, pl.ds(k0, kt), :], et_ref[pl.ds(k0, kt), :]], axis=1)
        s_sc[slot] = _dot(lhs, rhs)

    def update(idx, slot, m_old, acc_ref):
        k0 = pl.multiple_of(idx * kt, kt)
        m_new = jnp.maximum(m_old, jnp.max(s_sc[slot], axis=0, keepdims=True))
        p = jnp.exp2(s_sc[slot] - m_new).astype(BF16)
        acc_ref[...] = jnp.exp2(m_old - m_new) * acc_ref[...] + _dot(vst_ref[:, pl.ds(k0, kt)], p)
        return m_new

    def tile_group(first, carry, count):
        ms = list(carry)
        for t in range(count):
            scores(first + t + 2, (t + 2) % 4)
            ms[t % 2] = update(first + t, t % 4, ms[t % 2], acc_sc.at[t % 2])
        return tuple(ms)

    n_tiles = (s0 + kt - 1) // kt
    scores(0, 0)
    scores(1, 1)
    carry = (m_diag, jnp.full((1, NSA_HPG * NSA_Q), NEG, F32))
    done = 0
    for size in FLASH_BODY_TILES:
        left = n_tiles - done
        groups = (left + size - 1) // size if size == FLASH_BODY_TILES[-1] else left // size
        carry = lax.fori_loop(0, groups, lambda j, c, done=done, size=size: tile_group(done + size * j, c, size),
                              carry)
        done = done + size * groups
    m0, m1 = carry
    m_fin = jnp.maximum(m0, m1)
    acc = acc_sc[0] * jnp.exp2(m0 - m_fin) + acc_sc[1] * jnp.exp2(m1 - m_fin)
    osl = acc[:NSA_DH] / acc[NSA_DH:NSA_DH + 1]

    oc = jnp.where(grp == 0, oc[:NSA_DH], oc[NSA_DH:])
    gts = jax.nn.sigmoid(gt_ref[...])
    for h in range(NSA_HPG):
        cols = slice(h * NSA_Q, (h + 1) * NSA_Q)
        o_ref[h * NSA_DH:(h + 1) * NSA_DH, :] = (
            gts[3 * h:3 * h + 1, :] * oc[:, cols] + gts[3 * h + 1:3 * h + 2, :] * osl[:, cols]
            + gts[3 * h + 2:3 * h + 3, :] * ow[:, cols])


def _nsa_attention(qt, gt, kc, vct, okv, vt, expand_t, overlap, b, t, n_cmp, n_top, kt):
    nch = kc.shape[1]
    nq = t // NSA_Q
    n = b * t
    return pl.pallas_call(
        functools.partial(_nsa_body, n_cmp=n_cmp, n_top=n_top, kt=kt),
        grid=(b, NSA_GROUPS, nq),
        in_specs=[
            pl.BlockSpec((NSA_HPG * LANES, NSA_Q), lambda bi, g, i: (g, bi * nq + i)),
            pl.BlockSpec((GATE_ROWS, NSA_Q), lambda bi, g, i: (g, bi * nq + i)),
            pl.BlockSpec((1, nch, LANES), lambda bi, g, i: (bi, 0, 0)),
            pl.BlockSpec((1, LANES, nch), lambda bi, g, i: (bi, 0, 0)),
            pl.BlockSpec((1, t, LANES), lambda bi, g, i: (bi, 0, 2)),
            pl.BlockSpec((1, t, LANES), lambda bi, g, i: (bi, 0, 4)),
            pl.BlockSpec((LANES, t), lambda bi, g, i: (g, bi)),
            pl.BlockSpec((LANES, t), lambda bi, g, i: (NSA_GROUPS + g, bi)),
            pl.BlockSpec(expand_t.shape, lambda bi, g, i: (0, 0)),
            pl.BlockSpec(overlap.shape, lambda bi, g, i: (0, 0)),
        ],
        out_specs=pl.BlockSpec((NSA_HPG * NSA_DH, NSA_Q), lambda bi, g, i: (g, bi * nq + i)),
        out_shape=jax.ShapeDtypeStruct((NSA_W, n), F32),
        scratch_shapes=[pltpu.VMEM((2, LANES, NSA_HPG * NSA_Q), F32),
                        pltpu.VMEM((4, kt, NSA_HPG * NSA_Q), F32)],
        compiler_params=_cparams(("parallel", "parallel", "arbitrary"), 56),
        name="nsa_attention",
    )(qt, gt, kc, vct, okv, okv, vt, vt, expand_t, overlap)


def _split_bf16(a):
    hi = a.astype(BF16)
    return hi, (a - hi.astype(F32)).astype(BF16)


def _unit_lower_inverses(lmats):
    c = lmats[0].shape[0]
    r = lax.broadcasted_iota(jnp.int32, (c, c), 0)
    col = lax.broadcasted_iota(jnp.int32, (c, c), 1)
    eye = jnp.where(r == col, 1.0, 0.0)
    xs = [eye - l for l in lmats]
    ps = []
    for l in lmats:
        l16 = l.astype(BF16)
        ps.append(_dot(l16, l16))
    steps = int(np.log2(c)) - 1
    for s in range(steps):
        last = s + 1 == steps
        for i in range(len(lmats)):
            rhs = ps[i].astype(BF16)
            if last:
                xs[i] = xs[i] + _dot(xs[i].astype(BF16), rhs)
            else:
                both = _dot(jnp.concatenate([xs[i], ps[i]], axis=0).astype(BF16), rhs)
                xs[i] = xs[i] + both[:c]
                ps[i] = both[c:]
    return xs


def _gdn_body(x_ref, z_ref, ab_ref, alog_ref, dtb_ref, og_ref, o_ref, s_sc, *, ct):
    nb = x_ref.shape[0]

    @pl.when(pl.program_id(0) == 0)
    def _():
        s_sc[...] = jnp.zeros(s_sc.shape, F32)

    ch = GDN_CHUNK
    r = lax.broadcasted_iota(jnp.int32, (ch, ch), 0)
    col = lax.broadcasted_iota(jnp.int32, (ch, ch), 1)
    incl = r >= col
    strict = r > col
    tril16 = jnp.concatenate([jnp.where(incl, 1.0, 0.0).astype(BF16)] * 3, axis=1)

    units = []
    for ci in range(ct // ch):
        rows = slice(ci * ch, (ci + 1) * ch)
        for bi in range(nb):
            ab = ab_ref[bi, rows, :]
            g_all = -jnp.exp(alog_ref[...]) * jax.nn.softplus(ab + dtb_ref[...])
            beta_all = jax.nn.sigmoid(ab)
            g_hi, g_lo = _split_bf16(g_all)
            g_lo2 = (g_all - g_hi.astype(F32) - g_lo.astype(F32)).astype(BF16)
            gc_all = _dot(tril16, jnp.concatenate([g_hi, g_lo, g_lo2], axis=0))
            gc_t = gc_all.T
            for h in range(GDN_HEADS):
                hs = slice(h * GDN_DH, (h + 1) * GDN_DH)
                q16 = x_ref[bi, rows, hs]
                k16 = x_ref[bi, rows, GDN_W + h * GDN_DH:GDN_W + (h + 1) * GDN_DH]
                qh, kh = q16.astype(F32), k16.astype(F32)
                vh = x_ref[bi, rows, 2 * GDN_W + h * GDN_DH:2 * GDN_W + (h + 1) * GDN_DH].astype(F32)
                gc = gc_all[:, h:h + 1]
                gr = gc_t[h:h + 1, :]
                g_last = gc_all[ch - 1:ch, h:h + 1]
                beta = beta_all[:, GDN_HEADS + h:GDN_HEADS + h + 1]
                eg = jnp.exp(gc)
                decay = jnp.where(incl, jnp.exp(jnp.minimum(gc - gr, 0.0)), 0.0)
                kb = kh * beta
                with_k = (_dot_nt(jnp.concatenate([kb.astype(BF16), q16], axis=0), k16)
                          * jnp.concatenate([decay, decay], axis=0))
                units.append(dict(
                    rows=rows, bi=bi, h=h,
                    lmat=jnp.where(strict, with_k[:ch], 0.0),
                    vb_kbg=jnp.concatenate([(vh * beta).astype(BF16), (kb * eg).astype(BF16)], axis=1),
                    qk=jnp.where(incl, with_k[ch:], 0.0).astype(BF16),
                    qg=(qh * eg).astype(BF16), kd_t=(kh * jnp.exp(g_last - gc)).T.astype(BF16),
                    gl=jnp.exp(g_last)))
    tinvs = _unit_lower_inverses([u["lmat"] for u in units])
    for u, tinv in zip(units, tinvs):
        u_w = _dot(tinv.astype(BF16), u["vb_kbg"])
        u["u"] = u_w[:, :GDN_DH]
        u["w_qg"] = jnp.concatenate([u_w[:, GDN_DH:].astype(BF16), u["qg"]], axis=0)
        u["kd_qk"] = jnp.concatenate([u["kd_t"], u["qk"]], axis=0)

    for u in units:
        bi, h, rows = u["bi"], u["h"], u["rows"]
        hs = slice(h * GDN_DH, (h + 1) * GDN_DH)
        s_old = s_sc[bi * GDN_HEADS + h]
        from_state = _dot(u["w_qg"], s_old.astype(BF16))
        v_new = (u["u"] - from_state[:ch]).astype(BF16)
        from_v = _dot(u["kd_qk"], v_new)
        s_sc[bi * GDN_HEADS + h] = s_old * u["gl"] + from_v[:GDN_DH]
        o = from_state[ch:] + from_v[GDN_DH:]
        on = o * lax.rsqrt(jnp.mean(o * o, axis=-1, keepdims=True) + EPS) * og_ref[...]
        zh = z_ref[bi, rows, hs].astype(F32)
        o_ref[bi, rows, hs] = (on * (zh * jax.nn.sigmoid(zh))).astype(BF16)


def _gdn(oqkv, oz, oab, alog, dtb, og, ct=128):
    b, t, w3 = oqkv.shape
    full = lambda a: pl.BlockSpec(a.shape, lambda c: (0,) * a.ndim)
    return pl.pallas_call(
        functools.partial(_gdn_body, ct=ct),
        grid=(t // ct,),
        in_specs=[
            pl.BlockSpec((b, ct, w3), lambda c: (0, c, 0)),
            pl.BlockSpec((b, ct, GDN_W), lambda c: (0, c, 0)),
            pl.BlockSpec((b, ct, LANES), lambda c: (0, c, 0)),
            full(alog), full(dtb), full(og),
        ],
        out_specs=pl.BlockSpec((b, ct, GDN_W), lambda c: (0, c, 0)),
        out_shape=jax.ShapeDtypeStruct((b, t, GDN_W), BF16),
        scratch_shapes=[pltpu.VMEM((b * GDN_HEADS, GDN_DH, GDN_DH), F32)],
        compiler_params=_cparams(("arbitrary",), 32),
        name="gdn",
    )(oqkv, oz, oab, alog, dtb, og)


def _outproj_body(ont_ref, og_ref, x_ref, ng_ref, wo_ref, fg_ref, wr_ref, br_ref, upper_ref,
                  x1_ref, h2_ref, gate_ref, route_ref, cnt_ref, cnt_sc):
    i = pl.program_id(0)
    tm = x_ref.shape[0]

    @pl.when(i == 0)
    def _():
        cnt_sc[...] = jnp.zeros(cnt_sc.shape, F32)

    a = ont_ref[...]
    a = (a * lax.rsqrt(jnp.mean(a * a, axis=0, keepdims=True) + EPS) * ng_ref[...]).astype(BF16)
    x1 = x_ref[...] + _dot_tn(a, wo_ref[0:NSA_W, :]) + _dot(og_ref[...], wo_ref[NSA_W:, :])
    x1_ref[...] = x1
    h2f = x1 * lax.rsqrt(jnp.mean(x1 * x1, axis=-1, keepdims=True) + EPS) * fg_ref[...]
    _store_pieces(h2_ref, h2f)
    h2 = h2f.astype(BF16)

    logits = (_dot(h2, wr_ref[...]) + br_ref[...]).T[:N_EXPERTS]
    erow = lax.broadcasted_iota(jnp.int32, (N_EXPERTS, tm), 0)
    onehot = jnp.zeros((N_EXPERTS, tm), F32)
    firsts, vals = [], []
    v = logits
    for k in range(TOP_K):
        mx = jnp.max(v, axis=0, keepdims=True)
        first = jnp.min(jnp.where(v == mx, erow, N_EXPERTS), axis=0, keepdims=True)
        hit = erow == first
        v = jnp.where(hit, -jnp.inf, v)
        onehot = jnp.where(hit, 1.0, onehot)
        firsts.append(first)
        vals.append(mx)
    vals = [jnp.exp(m - vals[0]) for m in vals]
    inv = 1.0 / (vals[0] + vals[1] + vals[2] + vals[3])
    gates_t = jnp.concatenate([m * inv for m in vals] + [jnp.zeros((LANES - TOP_K, tm), F32)], axis=0)
    gate_ref[...] = gates_t.T

    excl = cnt_sc[...] + _dot(onehot.astype(BF16), upper_ref[...])
    for k in range(TOP_K):
        route_ref[k:k + 1, :] = firsts[k]
        rank = jnp.sum(jnp.where(erow == firsts[k], excl, 0.0), axis=0, keepdims=True)
        route_ref[TOP_K + k:TOP_K + k + 1, :] = rank.astype(jnp.int32)
    cnt_sc[...] = cnt_sc[...] + jnp.sum(onehot, axis=1, keepdims=True)
    cnt_ref[...] = cnt_sc[...].astype(jnp.int32)


def _out_proj(o_nsa_t, o_gdn, x2, ng, wo, fg, wr, br, tm=512):
    upper = jnp.asarray(np.arange(tm)[:, None] < np.arange(tm)[None, :], BF16)
    n, d = x2.shape
    full = lambda a: pl.BlockSpec(a.shape, lambda i: (0,) * a.ndim)
    row = lambda w: pl.BlockSpec((tm, w), lambda i: (i, 0))
    return pl.pallas_call(
        _outproj_body,
        grid=(n // tm,),
        in_specs=[pl.BlockSpec((NSA_W, tm), lambda i: (0, i)), row(GDN_W), row(d), full(ng), full(wo), full(fg),
                  full(wr), full(br), full(upper)],
        out_specs=[row(d), pl.BlockSpec((d // 2 // SC_SUBROW, tm, SC_SUBROW), lambda i: (0, i, 0)),
                   row(LANES), pl.BlockSpec((2 * TOP_K, tm), lambda i: (0, i)),
                   pl.BlockSpec((N_EXPERTS, 1), lambda i: (0, 0))],
        out_shape=[jax.ShapeDtypeStruct((n, d), F32),
                   jax.ShapeDtypeStruct((d // 2 // SC_SUBROW, n, SC_SUBROW), jnp.int32),
                   jax.ShapeDtypeStruct((n, LANES), F32), jax.ShapeDtypeStruct((2 * TOP_K, n), jnp.int32),
                   jax.ShapeDtypeStruct((N_EXPERTS, 1), jnp.int32)],
        scratch_shapes=[pltpu.VMEM((N_EXPERTS, 1), F32)],
        compiler_params=_cparams(("arbitrary",), 48),
        name="out_proj_router",
    )(o_nsa_t, o_gdn, x2, ng, wo, fg, wr, br, upper)


def _dest_body(ps_ref, route_ref, o_ref, *, n_rows, pieces):
    expert = route_ref[0:TOP_K, :]
    start = jnp.zeros(expert.shape, jnp.int32)
    for e in range(N_EXPERTS):
        start = jnp.where(expert == e, ps_ref[e], start)
    dest = start + route_ref[TOP_K:2 * TOP_K, :]
    for k in range(TOP_K):
        for j in range(pieces):
            o_ref[k * pieces + j:k * pieces + j + 1, :] = dest[k:k + 1, :] + j * n_rows


def _dest_rows(pstarts, route, n_rows, pieces):
    n = route.shape[1]
    tn = min(2048, n)
    grid_spec = pltpu.PrefetchScalarGridSpec(
        num_scalar_prefetch=1,
        grid=(n // tn,),
        in_specs=[pl.BlockSpec((2 * TOP_K, tn), lambda i, ps: (0, i))],
        out_specs=pl.BlockSpec((TOP_K * pieces, tn), lambda i, ps: (0, i)),
    )
    return pl.pallas_call(
        functools.partial(_dest_body, n_rows=n_rows, pieces=pieces),
        grid_spec=grid_spec,
        out_shape=jax.ShapeDtypeStruct((TOP_K * pieces, n), jnp.int32),
        name="moe_dest_rows",
    )(pstarts, route)


def _expert_body(be_ref, fresh_ref, slot_ref, next_ref, xs_ref, wg_hbm, bg_ref, wu_hbm, bu_ref, wd_hbm, bd_ref,
                 y_ref, wf32, w16, sems):
    i = pl.program_id(0)
    used = i < be_ref[pl.num_programs(0)]
    hbm = (wg_hbm, wu_hbm, wd_hbm)

    def weight_copy(expert, slot, j):
        return pltpu.make_async_copy(hbm[j].at[expert], wf32.at[slot, j], sems.at[slot, j])

    @pl.when((i == 0) & used)
    def _():
        for j in range(3):
            weight_copy(be_ref[0], 0, j).start()

    @pl.when(used & (fresh_ref[i] == 1))
    def _():
        slot = slot_ref[i]
        for j in range(3):
            weight_copy(be_ref[i], slot, j).wait()
            w16[j] = wf32[slot, j].astype(BF16)

        @pl.when(next_ref[i] >= 0)
        def _():
            for j in range(3):
                weight_copy(next_ref[i], 1 - slot, j).start()

    @pl.when(jnp.logical_not(used))
    def _():
        y_ref[...] = jnp.zeros(y_ref.shape, y_ref.dtype)

    @pl.when(used)
    def _():
        x = _join_pieces(xs_ref).astype(BF16)
        gate = jnp.minimum(_dot(x, w16[0]) + bg_ref[0], SWIGLU_LIMIT)
        up = jnp.clip(_dot(x, w16[1]) + bu_ref[0], -SWIGLU_LIMIT, SWIGLU_LIMIT)
        glu = gate * jax.nn.sigmoid(gate * SWIGLU_ALPHA)
        _store_pieces(y_ref, _dot(((up + 1.0) * glu).astype(BF16), w16[2]) + bd_ref[0])


def _experts(blk_e, fresh, slot, nxt, xs, wg, bg, wu, bu, wd, bd):
    pieces, n_rows, sub = xs.shape
    d, de = wg.shape[1], wg.shape[2]
    assert d == de
    r = MOE_ROW_BLOCK
    bspec = lambda w: pl.BlockSpec((1, 1, w), lambda i, be, *_: (be[i], 0, 0))
    hbm = pl.BlockSpec(memory_space=pl.ANY)
    grid_spec = pltpu.PrefetchScalarGridSpec(
        num_scalar_prefetch=4,
        grid=(n_rows // r,),
        in_specs=[pl.BlockSpec((pieces, r, sub), lambda i, *_: (0, i, 0)),
                  hbm, bspec(de), hbm, bspec(de), hbm, bspec(d)],
        out_specs=pl.BlockSpec((pieces, r, sub), lambda i, *_: (0, i, 0)),
        scratch_shapes=[pltpu.VMEM((2, 3, d, de), F32), pltpu.VMEM((3, d, de), BF16),
                        pltpu.SemaphoreType.DMA((2, 3))],
    )
    return pl.pallas_call(
        _expert_body,
        grid_spec=grid_spec,
        out_shape=jax.ShapeDtypeStruct((pieces, n_rows, sub), jnp.int32),
        compiler_params=_cparams(("arbitrary",), 56),
        name="moe_experts",
    )(blk_e, fresh, slot, nxt, xs, wg, bg, wu, bu, wd, bd)


SC_WINDOW = 128
SC_SUBROW = 256


def _sc_mesh():
    return plsc.VectorSubcoreMesh(core_axis_name="c", subcore_axis_name="s")


def _sc_dispatch(h2, dest_rows, n_rows):
    n, d = h2.shape

    @functools.partial(pl.kernel, out_type=jax.ShapeDtypeStruct((n_rows, d), h2.dtype), mesh=_sc_mesh())
    def dispatch(x_hbm, *refs):
        idx_hbm, o_hbm = refs[:TOP_K], refs[TOP_K]

        def body(x_vmem, *idx_vmem):
            for iv in idx_vmem:
                pltpu.sync_copy(x_vmem, o_hbm.at[iv.at[0]])

        pltpu.emit_pipeline(
            body,
            grid=(n // SC_WINDOW,),
            in_specs=[pl.BlockSpec((SC_WINDOW, d), lambda i: (i, 0))]
                     + [pl.BlockSpec((1, SC_WINDOW), lambda i: (0, i))] * TOP_K,
            out_specs=[],
            core_axis_name=("c", "s"),
            dimension_semantics=(pltpu.PARALLEL,),
        )(x_hbm, *idx_hbm)

    return dispatch(h2, *dest_rows)


def _sc_gather(table, idx):
    _, d = table.shape
    m = idx.shape[1]

    @functools.partial(pl.kernel, out_type=jax.ShapeDtypeStruct((m, d), table.dtype), mesh=_sc_mesh())
    def gather(t_hbm, i_hbm, o_hbm):
        def body(i_vmem, o_vmem):
            pltpu.sync_copy(t_hbm.at[i_vmem.at[0]], o_vmem)

        pltpu.emit_pipeline(
            body,
            grid=(m // SC_WINDOW,),
            in_specs=[pl.BlockSpec((1, SC_WINDOW), lambda i: (0, i))],
            out_specs=[pl.BlockSpec((SC_WINDOW, d), lambda i: (i, 0))],
            core_axis_name=("c", "s"),
            dimension_semantics=(pltpu.PARALLEL,),
        )(i_hbm, o_hbm)

    return gather(table, idx)


def _combine_body(x1_ref, y_ref, gate_ref, o_ref):
    acc = x1_ref[...]
    for k in range(TOP_K):
        acc = acc + gate_ref[:, k:k + 1] * _join_pieces(y_ref.at[k])
    o_ref[...] = acc


def _combine(x1, y4, gates, tm=512):
    n, d = x1.shape
    pieces, sub = y4.shape[1], y4.shape[3]
    row = lambda w: pl.BlockSpec((tm, w), lambda i: (i, 0))
    return pl.pallas_call(
        _combine_body,
        grid=(n // tm,),
        in_specs=[row(d), pl.BlockSpec((TOP_K, pieces, tm, sub), lambda i: (0, 0, i, 0)), row(LANES)],
        out_specs=row(d),
        out_shape=jax.ShapeDtypeStruct((n, d), F32),
        compiler_params=_cparams(("parallel",), 48),
        name="moe_combine",
    )(x1, y4, gates)


def _pad_lanes(a, width=LANES):
    return jnp.pad(a, ((0, 0), (0, width - a.shape[1])))


def _layer(x, attn_norm_g, w_in, q_g, kc_g, ks_g, kw_g, ck_pos, ck_w1, ck_b1, ck_w2, ck_b2,
           cv_pos, cv_w1, cv_b1, cv_w2, cv_b2, nsa_out_g, conv_w, a_log, dt_bias, gdn_out_g, w_out,
           ffn_g, router_w, router_b, e_wg, e_bg, e_wu, e_bu, e_wd, e_bd):
    b, t, d = x.shape
    n = b * t
    x2 = x.reshape(n, d)

    o = np.cumsum([0, NSA_W] + [NSA_GROUPS * NSA_DH] * 6 + [3 * NSA_HEADS, 3 * GDN_W, GDN_W, GDN_HEADS, GDN_HEADS])
    wq_t = w_in[:, o[0]:o[1]].T.reshape(NSA_GROUPS, NSA_HPG, NSA_DH, d)
    zq = jnp.zeros((NSA_HPG, NSA_DH, d), F32)
    wq_t = jnp.stack([jnp.concatenate([wq_t[0], zq], axis=1), jnp.concatenate([zq, wq_t[1]], axis=1)])
    wq_t = wq_t.reshape(NSA_HEADS * LANES, d).astype(BF16)
    qg1 = q_g * (NSA_DH ** -0.5 * np.log2(np.e))
    zg = jnp.zeros((NSA_DH,), F32)
    qg_col = jnp.concatenate([jnp.tile(jnp.concatenate([qg1, zg]), NSA_HPG),
                              jnp.tile(jnp.concatenate([zg, qg1]), NSA_HPG)]).reshape(NSA_HEADS * LANES, 1)
    wkv = w_in[:, o[1]:o[7]].astype(BF16)
    ones = jnp.ones((LANES,), F32)
    kg = jnp.concatenate([ones, ones, ks_g, ks_g, ones, kw_g, kw_g, ones]).reshape(1, 6 * LANES)
    wv_t = jnp.concatenate([w_in[:, o[4]:o[5]], w_in[:, o[6]:o[7]]], axis=1).T.reshape(2 * NSA_GROUPS, NSA_DH, d)
    wv_t = jnp.pad(wv_t, ((0, 0), (0, LANES - NSA_DH), (0, 0))).reshape(2 * NSA_GROUPS * LANES, d).astype(BF16)
    vone = jnp.asarray((np.arange(2 * NSA_GROUPS * LANES) % LANES == NSA_DH).astype(np.float32)[:, None])
    wg_t = w_in[:, o[7]:o[8]].T.reshape(NSA_GROUPS, NSA_HPG * 3, d)
    wg_t = jnp.pad(wg_t, ((0, 0), (0, GATE_ROWS - NSA_HPG * 3), (0, 0))).reshape(NSA_GROUPS * GATE_ROWS, d)
    wg_t = wg_t.astype(BF16)
    wab = _pad_lanes(w_in[:, o[10]:o[12]]).astype(BF16)
    wqkv = w_in[:, o[8]:o[9]].astype(BF16)
    wz = w_in[:, o[9]:o[10]].astype(BF16)

    tm = min(512, t)
    oqt, okv, ovt, ogt, oqkv, oz, oab = _in_proj(x2, attn_norm_g.reshape(1, d), wq_t, wkv, wv_t, wg_t, wqkv, wz,
                                                 wab, qg_col, kg, vone, conv_w, t // tm, tm)

    nch = t // CMP_STRIDE
    n_cmp = (t - CMP_BLOCK) // CMP_STRIDE + 1
    half = CMP_STRIDE * NSA_DH
    xflat = okv[:, :2 * LANES].reshape(b, nch, CMP_STRIDE, 2, NSA_GROUPS, NSA_DH)
    xflat = xflat.transpose(0, 3, 4, 1, 2, 5).reshape(b, 2, NSA_GROUPS, nch, half)
    pos = jnp.stack([ck_pos, cv_pos]).reshape(2, 2, 1, half)
    w1 = jnp.stack([ck_w1, cv_w1]).reshape(2, 2, half, CMP_HIDDEN).astype(BF16)
    b1 = jnp.stack([ck_b1, cv_b1]).reshape(2, 1, CMP_HIDDEN)
    w2 = jnp.stack([ck_w2, cv_w2]).astype(BF16)
    b2 = jnp.stack([ck_b2, cv_b2]).reshape(2, 1, NSA_DH)
    w2t = jnp.stack([ck_w2.T, cv_w2.T]).astype(BF16)
    b2t = jnp.stack([ck_b2, cv_b2]).reshape(2, NSA_DH, 1)
    kc, vct = _compress(xflat, pos, w1, b1, w2, b2, w2t, b2t, kc_g.reshape(1, NSA_DH), n_cmp)

    n_slc = t // SLC_BLOCK
    n_top = min(SLC_TOPK, n_slc)
    nblk = max(n_slc, LANES)
    kt = min(256, t // 4)
    assert (t // kt) % 4 == 0
    ci = np.arange(nch)[None, :] * CMP_STRIDE
    sj = np.arange(nblk)[:, None] * SLC_BLOCK
    overlap = ((ci < sj + SLC_BLOCK) & (ci + CMP_BLOCK > sj) & (np.arange(nch)[None, :] < n_cmp)
               & (np.arange(nblk)[:, None] < n_slc))
    expand_t = (np.arange(t)[:, None] // SLC_BLOCK) == np.arange(nblk)[None, :]
    o_nsa_t = _nsa_attention(oqt, ogt, kc, vct, okv.reshape(b, t, -1), ovt, jnp.asarray(expand_t, BF16),
                             jnp.asarray(overlap, BF16), b, t, n_cmp, n_top, kt)

    alog_row = _pad_lanes(a_log.reshape(1, GDN_HEADS))
    dtb_row = _pad_lanes(dt_bias.reshape(1, GDN_HEADS))
    o_gdn = _gdn(oqkv.reshape(b, t, -1), oz.reshape(b, t, -1), oab.reshape(b, t, -1),
                 alog_row, dtb_row, gdn_out_g.reshape(1, GDN_DH))

    wr = _pad_lanes(router_w).astype(BF16)
    br = _pad_lanes(router_b.reshape(1, N_EXPERTS))
    x1, h2, gates, route, counts = _out_proj(
        o_nsa_t, o_gdn.reshape(n, GDN_W), x2, nsa_out_g.reshape(NSA_W, 1),
        w_out.astype(BF16), ffn_g.reshape(1, d), wr, br)

    r = MOE_ROW_BLOCK
    nk = n * TOP_K
    counts = counts[:, 0]
    pcounts = (counts + r - 1) // r * r
    pends = jnp.cumsum(pcounts)
    pstarts = pends - pcounts
    n_rows = (nk + r - 1) // r * r + N_EXPERTS * r
    n_blocks = n_rows // r
    blk_start = jnp.arange(n_blocks, dtype=jnp.int32)[:, None] * r
    blk_e = jnp.minimum(jnp.sum(pends[None, :] <= blk_start, axis=1), N_EXPERTS - 1).astype(jnp.int32)
    n_used = (pends[-1] // r).astype(jnp.int32)
    fresh = (jnp.arange(n_blocks) < n_used) & (blk_e != jnp.concatenate([jnp.full((1,), -1, jnp.int32), blk_e[:-1]]))
    slot = ((jnp.cumsum(fresh) - 1) % 2).astype(jnp.int32)
    eid = jnp.arange(N_EXPERTS, dtype=jnp.int32)
    later = jnp.where((eid[None, :] > eid[:, None]) & (pcounts[None, :] > 0), eid[None, :], N_EXPERTS)
    next_expert = jnp.min(later, axis=1)
    next_expert = jnp.where(next_expert < N_EXPERTS, next_expert, -1).astype(jnp.int32)
    nxt = next_expert[blk_e]
    blk_e = jnp.concatenate([blk_e, n_used[None]])
    pieces = d // 2 // SC_SUBROW
    dest_p = _dest_rows(pstarts.astype(jnp.int32), route, n_rows, pieces).reshape(TOP_K, pieces, n)
    xs = _sc_dispatch(h2.reshape(pieces * n, SC_SUBROW), [dest_p[k].reshape(1, pieces * n) for k in range(TOP_K)],
                      pieces * n_rows)
    ys = _experts(blk_e, fresh.astype(jnp.int32), slot, nxt, xs.reshape(pieces, n_rows, SC_SUBROW), e_wg, e_bg.reshape(N_EXPERTS, 1, -1), e_wu,
                  e_bu.reshape(N_EXPERTS, 1, -1), e_wd, e_bd.reshape(N_EXPERTS, 1, -1))
    y4 = _sc_gather(ys.reshape(pieces * n_rows, SC_SUBROW), dest_p.reshape(1, nk * pieces))
    return _combine(x1, y4.reshape(TOP_K, pieces, n, SC_SUBROW), gates).reshape(b, t, d)


def kernel(x, attn_norm_g, w_in, nsa_q_norm_g, nsa_kc_norm_g, nsa_ks_norm_g, nsa_kw_norm_g, cmp_k_pos, cmp_k_w1, cmp_k_b1, cmp_k_w2, cmp_k_b2, cmp_v_pos, cmp_v_w1, cmp_v_b1, cmp_v_w2, cmp_v_b2, nsa_out_norm_g, gdn_conv_w, gdn_a_log, gdn_dt_bias, gdn_out_norm_g, w_out, ffn_norm_g, router_w, router_b, exp_w_gate, exp_b_gate, exp_w_up, exp_b_up, exp_w_down, exp_b_down):
    params = (attn_norm_g, w_in, nsa_q_norm_g, nsa_kc_norm_g, nsa_ks_norm_g, nsa_kw_norm_g,
              cmp_k_pos, cmp_k_w1, cmp_k_b1, cmp_k_w2, cmp_k_b2, cmp_v_pos, cmp_v_w1, cmp_v_b1, cmp_v_w2, cmp_v_b2,
              nsa_out_norm_g, gdn_conv_w, gdn_a_log, gdn_dt_bias, gdn_out_norm_g, w_out, ffn_norm_g,
              router_w, router_b, exp_w_gate, exp_b_gate, exp_w_up, exp_b_up, exp_w_down, exp_b_down)
    for l in range(attn_norm_g.shape[0]):
        x = _layer(x, *(p[l] for p in params))
    return x
```

```python
import functools

import jax
import jax.numpy as jnp
import numpy as np
from jax import lax
from jax.experimental import pallas as pl
from jax.experimental.pallas import tpu as pltpu
from jax.experimental.pallas import tpu_sc as plsc

F32 = jnp.float32
BF16 = jnp.bfloat16

EPS = 1e-6
NEG = -1e30
MASKED = -2.0 ** 100

NSA_HEADS = 8
NSA_GROUPS = 2
NSA_HPG = 4
NSA_DH = 64
CMP_BLOCK = 32
CMP_STRIDE = 16
CMP_HIDDEN = 256
SLC_BLOCK = 64
SLC_TOPK = 16
WINDOW = 512
NSA_Q = 256
GDN_HEADS = 4
GDN_DH = 128
GDN_CHUNK = 64
N_EXPERTS = 32
TOP_K = 4
SWIGLU_LIMIT = 7.0
SWIGLU_ALPHA = 1.702
MOE_ROW_BLOCK = 256

LANES = 128
GATE_ROWS = 16
FLASH_BODY_TILES = (16, 8, 4)
NSA_W = NSA_HEADS * NSA_DH
GDN_W = GDN_HEADS * GDN_DH

_NT = (((1,), (1,)), ((), ()))
_TN = (((0,), (0,)), ((), ()))


def _cparams(sem, vmem_mb):
    return pltpu.CompilerParams(dimension_semantics=sem, vmem_limit_bytes=vmem_mb * 1024 * 1024)


def _dot(a, b):
    return jnp.dot(a, b, preferred_element_type=F32)


def _dot_nt(a, b):
    return lax.dot_general(a, b, _NT, preferred_element_type=F32)


def _dot_tn(a, b):
    return lax.dot_general(a, b, _TN, preferred_element_type=F32)


def _store_pieces(ref, val):
    half = val.shape[1] // 2
    hi = lax.bitcast_convert_type(val[:, :half].astype(BF16).astype(F32), jnp.uint32)
    lo = lax.bitcast_convert_type(val[:, half:].astype(BF16).astype(F32), jnp.uint32)
    words = lax.bitcast_convert_type(hi | (lo >> 16), jnp.int32)
    sub = ref.shape[2]
    for j in range(ref.shape[0]):
        ref[j] = words[:, j * sub:(j + 1) * sub]


def _join_pieces(ref):
    words = jnp.concatenate([ref[j] for j in range(ref.shape[0])], axis=1)
    words = lax.bitcast_convert_type(words, jnp.uint32)
    hi = lax.bitcast_convert_type(words & jnp.uint32(0xFFFF0000), F32)
    lo = lax.bitcast_convert_type(words << 16, F32)
    return jnp.concatenate([hi, lo], axis=1)


def _inproj_body(x_ref, g_ref, wqt_ref, wkv_ref, wvt_ref, wgt_ref, wqkv_ref, wz_ref, wab_ref, qg_ref, kg_ref,
                 vone_ref, cw_ref, oqt_ref, okn_ref, ocf_ref, ovt_ref, ogt_ref, oqkv_ref, oz_ref, oab_ref, ybuf, cbuf,
                 *, tiles_per_seq):
    x = x_ref[...]
    h = (x * lax.rsqrt(jnp.mean(x * x, axis=-1, keepdims=True) + EPS) * g_ref[...]).astype(BF16)
    tm = x.shape[0]

    yq = _dot_nt(wqt_ref[...], h)
    for s in range(NSA_HEADS):
        sl = slice(s * LANES, (s + 1) * LANES)
        ys = yq[sl, :]
        ms = jnp.sum(ys * ys, axis=0, keepdims=True) * (1.0 / NSA_DH)
        oqt_ref[sl, :] = (ys * lax.rsqrt(ms + EPS) * qg_ref[sl, :]).astype(BF16)

    ykv = _dot(h, wkv_ref[...])
    lane = lax.broadcasted_iota(jnp.int32, (tm, LANES), 1)
    low = lane < NSA_DH
    for s in range(2):
        sl = slice(s * LANES, (s + 1) * LANES)
        ys = ykv[:, (2 + s) * LANES:(3 + s) * LANES]
        y2 = ys * ys
        s0 = jnp.sum(jnp.where(low, y2, 0.0), axis=-1, keepdims=True)
        s1 = jnp.sum(jnp.where(low, 0.0, y2), axis=-1, keepdims=True)
        ms = jnp.where(low, s0, s1) * (1.0 / NSA_DH)
        okn_ref[:, sl] = (ys * lax.rsqrt(ms + EPS) * kg_ref[:, sl]).astype(BF16)

    chunks = tm // CMP_STRIDE
    for br in range(2):
        cbuf[br] = ykv[:, br * LANES:(br + 1) * LANES]
        taken = [cbuf[br, pl.ds(l, chunks, stride=CMP_STRIDE), :] for l in range(CMP_STRIDE)]
        for grp in range(NSA_GROUPS):
            flat = jnp.concatenate([r[:, grp * NSA_DH:(grp + 1) * NSA_DH] for r in taken], axis=1)
            ocf_ref[0, br, grp] = flat.astype(BF16)

    ovt_ref[...] = (_dot_nt(wvt_ref[...], h) + vone_ref[...]).astype(BF16)
    ogt_ref[...] = _dot_nt(wgt_ref[...], h)
    oz_ref[...] = _dot(h, wz_ref[...]).astype(BF16)
    oab_ref[...] = _dot(h, wab_ref[...])

    halo = ybuf.shape[0] - tm
    first = pl.program_id(0) % tiles_per_seq == 0

    @pl.when(first)
    def _():
        ybuf[0:halo, :] = jnp.zeros((halo, ybuf.shape[1]), F32)

    @pl.when(jnp.logical_not(first))
    def _():
        ybuf[0:halo, :] = ybuf[tm:tm + halo, :]

    ybuf[halo:halo + tm, :] = _dot(h, wqkv_ref[...])
    taps = cw_ref.shape[0]
    y = cw_ref[0:1, :] * ybuf[pl.ds(halo - taps + 1, tm), :]
    for k in range(1, taps):
        y = y + cw_ref[k:k + 1, :] * ybuf[pl.ds(halo - taps + 1 + k, tm), :]
    hy = 0.5 * y
    y = hy + hy * jnp.tanh(hy)
    for s in range(3 * GDN_HEADS):
        sl = slice(s * GDN_DH, (s + 1) * GDN_DH)
        ys = y[:, sl]
        if s < 2 * GDN_HEADS:
            scale = GDN_DH ** -0.5 if s < GDN_HEADS else 1.0
            ys = ys * (lax.rsqrt(jnp.sum(ys * ys, axis=-1, keepdims=True) + EPS) * scale)
        oqkv_ref[:, sl] = ys.astype(BF16)


def _in_proj(x2, g, wqt, wkv, wvt, wgt, wqkv, wz, wab, qg, kg, vone, conv_w, tiles_per_seq, tm):
    n, d = x2.shape
    chunks, flat = tm // CMP_STRIDE, CMP_STRIDE * NSA_DH
    full = lambda a: pl.BlockSpec(a.shape, lambda i: (0,) * a.ndim)
    row = lambda w: pl.BlockSpec((tm, w), lambda i: (i, 0))
    colb = lambda r: pl.BlockSpec((r, tm), lambda i: (0, i))
    return pl.pallas_call(
        functools.partial(_inproj_body, tiles_per_seq=tiles_per_seq),
        grid=(n // tm,),
        in_specs=[row(d)] + [full(a) for a in (g, wqt, wkv, wvt, wgt, wqkv, wz, wab, qg, kg, vone, conv_w)],
        out_specs=[colb(wqt.shape[0]), row(2 * LANES),
                   pl.BlockSpec((1, 2, NSA_GROUPS, chunks, flat), lambda i: (i // tiles_per_seq, 0, 0, i % tiles_per_seq, 0)),
                   colb(wvt.shape[0]), colb(wgt.shape[0]), row(wqkv.shape[1]), row(wz.shape[1]), row(wab.shape[1])],
        out_shape=[jax.ShapeDtypeStruct((wqt.shape[0], n), BF16), jax.ShapeDtypeStruct((n, 2 * LANES), BF16),
                   jax.ShapeDtypeStruct((n // (tm * tiles_per_seq), 2, NSA_GROUPS, chunks * tiles_per_seq, flat), BF16),
                   jax.ShapeDtypeStruct((wvt.shape[0], n), BF16), jax.ShapeDtypeStruct((wgt.shape[0], n), F32),
                   jax.ShapeDtypeStruct((n, wqkv.shape[1]), BF16), jax.ShapeDtypeStruct((n, wz.shape[1]), BF16),
                   jax.ShapeDtypeStruct((n, wab.shape[1]), F32)],
        scratch_shapes=[pltpu.VMEM((tm + 8, wqkv.shape[1]), F32), pltpu.VMEM((2, tm, LANES), F32)],
        compiler_params=_cparams(("arbitrary",), 56),
        name="in_proj",
    )(x2, g, wqt, wkv, wvt, wgt, wqkv, wz, wab, qg, kg, vone, conv_w)


def _compress_body(x_ref, pos_ref, w1_ref, b1_ref, w2_ref, b2_ref, w2t_ref, b2t_ref, g_ref, ok_ref, ovt_ref,
                   *, n_cmp):
    is_key = pl.program_id(1) == 0
    nch = x_ref.shape[3]
    hids = []
    for grp in range(NSA_GROUPS):
        x = x_ref[0, 0, grp].astype(F32)
        xa = (x + pos_ref[0, 0]).astype(BF16)
        xb = (x + pos_ref[0, 1]).astype(BF16)
        a = _dot(xa, w1_ref[0, 0])
        b = _dot(xb, w1_ref[0, 1])
        b_next = pltpu.roll(b, nch - 1, 0)
        hids.append(jax.nn.gelu(a + b_next + b1_ref[0]).astype(BF16))

    @pl.when(is_key)
    def _():
        row = lax.broadcasted_iota(jnp.int32, (nch, NSA_DH), 0)
        outs = []
        for grp in range(NSA_GROUPS):
            out = _dot(hids[grp], w2_ref[0]) + b2_ref[0]
            out = out * lax.rsqrt(jnp.mean(out * out, axis=-1, keepdims=True) + EPS) * g_ref[...]
            outs.append(jnp.where(row < n_cmp, out, 0.0))
        ok_ref[0] = jnp.concatenate(outs, axis=-1).astype(BF16)

    @pl.when(jnp.logical_not(is_key))
    def _():
        col = lax.broadcasted_iota(jnp.int32, (NSA_DH, nch), 1)
        outs = []
        for grp in range(NSA_GROUPS):
            out = _dot_nt(w2t_ref[0], hids[grp]) + b2t_ref[0]
            outs.append(jnp.where(col < n_cmp, out, 0.0))
        ovt_ref[0] = jnp.concatenate(outs, axis=0).astype(BF16)


def _compress(xflat, pos, w1, b1, w2, b2, w2t, b2t, kc_g, n_cmp):
    b, _, _, nch, flat = xflat.shape
    return pl.pallas_call(
        functools.partial(_compress_body, n_cmp=n_cmp),
        grid=(b, 2),
        in_specs=[
            pl.BlockSpec((1, 1, NSA_GROUPS, nch, flat), lambda i, j: (i, j, 0, 0, 0)),
            pl.BlockSpec((1, 2, 1, flat), lambda i, j: (j, 0, 0, 0)),
            pl.BlockSpec((1, 2, flat, CMP_HIDDEN), lambda i, j: (j, 0, 0, 0)),
            pl.BlockSpec((1, 1, CMP_HIDDEN), lambda i, j: (j, 0, 0)),
            pl.BlockSpec((1, CMP_HIDDEN, NSA_DH), lambda i, j: (j, 0, 0)),
            pl.BlockSpec((1, 1, NSA_DH), lambda i, j: (j, 0, 0)),
            pl.BlockSpec((1, NSA_DH, CMP_HIDDEN), lambda i, j: (j, 0, 0)),
            pl.BlockSpec((1, NSA_DH, 1), lambda i, j: (j, 0, 0)),
            pl.BlockSpec((1, NSA_DH), lambda i, j: (0, 0)),
        ],
        out_specs=[pl.BlockSpec((1, nch, LANES), lambda i, j: (i, 0, 0)),
                   pl.BlockSpec((1, LANES, nch), lambda i, j: (i, 0, 0))],
        out_shape=[jax.ShapeDtypeStruct((b, nch, LANES), BF16), jax.ShapeDtypeStruct((b, LANES, nch), BF16)],
        compiler_params=_cparams(("parallel", "arbitrary"), 32),
        name="nsa_compress",
    )(xflat, pos, w1, b1, w2, b2, w2t, b2t, kc_g)


def _tile_heads(a):
    return jnp.concatenate([a] * NSA_HPG, axis=1)


def _nsa_body(qt_ref, gt_ref, kc_ref, vct_ref, ks_ref, kw_ref, vst_ref, vwt_ref, et_ref, ov_ref, o_ref, acc_sc, s_sc,
              *, n_cmp, n_top, kt):
    grp = pl.program_id(1)
    s0 = pl.program_id(2) * NSA_Q
    nch = kc_ref.shape[1]
    nblk = ov_ref.shape[0]

    qt = jnp.concatenate([qt_ref[h * LANES:(h + 1) * LANES, :] for h in range(NSA_HPG)], axis=1)
    t_row = s0 + lax.broadcasted_iota(jnp.int32, (1, NSA_Q), 1)

    cidx = lax.broadcasted_iota(jnp.int32, (nch, 1), 0)
    cvalid = (cidx * CMP_STRIDE + (CMP_BLOCK - 1) <= t_row) & (cidx < n_cmp)
    sc = _dot(kc_ref[0], qt) + _tile_heads(jnp.where(cvalid, 0.0, NEG))
    pc = jnp.exp2(sc - jnp.max(sc, axis=0, keepdims=True)).astype(BF16)
    stacked = jnp.concatenate([vct_ref[0], ov_ref[...], jnp.ones((8, nch), BF16)], axis=0)
    res = _dot(stacked, pc)
    inv = jnp.where(_tile_heads(t_row >= CMP_BLOCK - 1), 1.0 / jnp.maximum(res[LANES + nblk:LANES + nblk + 1], 1e-30),
                    0.0)
    oc = res[:LANES] * inv
    imp4 = res[LANES:LANES + nblk] * inv
    imp = (imp4[:, 0:NSA_Q] + imp4[:, NSA_Q:2 * NSA_Q] + imp4[:, 2 * NSA_Q:3 * NSA_Q]
           + imp4[:, 3 * NSA_Q:4 * NSA_Q])
    blk = lax.broadcasted_iota(jnp.int32, (nblk, NSA_Q), 0)
    cur = t_row // SLC_BLOCK
    imp = jnp.where(blk * SLC_BLOCK > t_row, NEG, imp)
    imp = jnp.where((blk == 0) | (blk == cur) | (blk == cur - 1), -NEG, imp)

    def pick_rounds(v, rounds):
        for _ in range(rounds):
            mx = jnp.max(v, axis=0, keepdims=True)
            first = jnp.min(jnp.where(v == mx, blk, nblk), axis=0, keepdims=True)
            v = jnp.where(blk == first, -jnp.inf, v)
        return v

    quarter = n_top // 4
    picked = pick_rounds(imp, quarter)

    wlen = WINDOW + NSA_Q
    w0 = pl.multiple_of(jnp.maximum(s0 - WINDOW, 0), NSA_Q)
    kpos = w0 + lax.broadcasted_iota(jnp.int32, (wlen, 1), 0)
    wbias = jnp.where((kpos <= t_row) & (kpos > t_row - WINDOW), 0.0, NEG)
    sw = _dot(kw_ref[0, pl.ds(w0, wlen), :], qt) + _tile_heads(wbias)
    picked = pick_rounds(picked, quarter)
    pw = jnp.exp2(sw - jnp.max(sw, axis=0, keepdims=True)).astype(BF16)
    picked = pick_rounds(picked, quarter)
    ow = _dot(vwt_ref[:, pl.ds(w0, wlen)], pw)
    ow = ow[:NSA_DH] / ow[NSA_DH:NSA_DH + 1]

    picked = pick_rounds(picked, n_top - 3 * quarter)
    chosen = picked == -jnp.inf

    selb = jnp.where(chosen & (blk * SLC_BLOCK < s0), 0.0, MASKED).astype(BF16)
    rhs = jnp.concatenate([qt, _tile_heads(selb)], axis=0)

    d0 = pl.multiple_of(s0, NSA_Q)
    dpos = s0 + lax.broadcasted_iota(jnp.int32, (NSA_Q, 1), 0)
    selb_d = jnp.where(chosen & (blk <= cur), 0.0, MASKED).astype(BF16)
    sd = (_dot(jnp.concatenate([ks_ref[0, pl.ds(d0, NSA_Q), :], et_ref[pl.ds(d0, NSA_Q), :]], axis=1),
               jnp.concatenate([qt, _tile_heads(selb_d)], axis=0))
          + _tile_heads(jnp.where(dpos <= t_row, 0.0, NEG)))
    m_diag = jnp.max(sd, axis=0, keepdims=True)
    acc_sc[0] = _dot(vst_ref[:, pl.ds(d0, NSA_Q)], jnp.exp2(sd - m_diag).astype(BF16))
    acc_sc[1] = jnp.zeros(acc_sc.shape[1:], F32)

    last_tile = ks_ref.shape[1] // kt - 1

    def scores(idx, slot):
        k0 = pl.multiple_of(jnp.minimum(idx, last_tile) * kt, kt)
        lhs = jnp.concatenate([ks_ref[0, pl.ds(k0, kt), :], et_ref[pl.ds(k0, kt), :]], axis=1)
        s_sc[slot] = _dot(lhs, rhs)

    def update(idx, slot, m_old, acc_ref):
        k0 = pl.multiple_of(idx * kt, kt)
        m_new = jnp.maximum(m_old, jnp.max(s_sc[slot], axis=0, keepdims=True))
        p = jnp.exp2(s_sc[slot] - m_new).astype(BF16)
        acc_ref[...] = jnp.exp2(m_old - m_new) * acc_ref[...] + _dot(vst_ref[:, pl.ds(k0, kt)], p)
        return m_new

    def tile_group(first, carry, count):
        ms = list(carry)
        for t in range(count):
            scores(first + t + 2, (t + 2) % 4)
            ms[t % 2] = update(first + t, t % 4, ms[t % 2], acc_sc.at[t % 2])
        return tuple(ms)

    n_tiles = (s0 + kt - 1) // kt
    scores(0, 0)
    scores(1, 1)
    carry = (m_diag, jnp.full((1, NSA_HPG * NSA_Q), NEG, F32))
    done = 0
    for size in FLASH_BODY_TILES:
        left = n_tiles - done
        groups = (left + size - 1) // size if size == FLASH_BODY_TILES[-1] else left // size
        carry = lax.fori_loop(0, groups, lambda j, c, done=done, size=size: tile_group(done + size * j, c, size),
                              carry)
        done = done + size * groups
    m0, m1 = carry
    m_fin = jnp.maximum(m0, m1)
    acc = acc_sc[0] * jnp.exp2(m0 - m_fin) + acc_sc[1] * jnp.exp2(m1 - m_fin)
    osl = acc[:NSA_DH] / acc[NSA_DH:NSA_DH + 1]

    oc = jnp.where(grp == 0, oc[:NSA_DH], oc[NSA_DH:])
    gts = jax.nn.sigmoid(gt_ref[...])
    for h in range(NSA_HPG):
        cols = slice(h * NSA_Q, (h + 1) * NSA_Q)
        o_ref[h * NSA_DH:(h + 1) * NSA_DH, :] = (
            gts[3 * h:3 * h + 1, :] * oc[:, cols] + gts[3 * h + 1:3 * h + 2, :] * osl[:, cols]
            + gts[3 * h + 2:3 * h + 3, :] * ow[:, cols])


def _nsa_attention(qt, gt, kc, vct, okv, vt, expand_t, overlap, b, t, n_cmp, n_top, kt):
    nch = kc.shape[1]
    nq = t // NSA_Q
    n = b * t
    return pl.pallas_call(
        functools.partial(_nsa_body, n_cmp=n_cmp, n_top=n_top, kt=kt),
        grid=(b, NSA_GROUPS, nq),
        in_specs=[
            pl.BlockSpec((NSA_HPG * LANES, NSA_Q), lambda bi, g, i: (g, bi * nq + i)),
            pl.BlockSpec((GATE_ROWS, NSA_Q), lambda bi, g, i: (g, bi * nq + i)),
            pl.BlockSpec((1, nch, LANES), lambda bi, g, i: (bi, 0, 0)),
            pl.BlockSpec((1, LANES, nch), lambda bi, g, i: (bi, 0, 0)),
            pl.BlockSpec((1, t, LANES), lambda bi, g, i: (bi, 0, 0)),
            pl.BlockSpec((1, t, LANES), lambda bi, g, i: (bi, 0, 1)),
            pl.BlockSpec((LANES, t), lambda bi, g, i: (g, bi)),
            pl.BlockSpec((LANES, t), lambda bi, g, i: (NSA_GROUPS + g, bi)),
            pl.BlockSpec(expand_t.shape, lambda bi, g, i: (0, 0)),
            pl.BlockSpec(overlap.shape, lambda bi, g, i: (0, 0)),
        ],
        out_specs=pl.BlockSpec((NSA_HPG * NSA_DH, NSA_Q), lambda bi, g, i: (g, bi * nq + i)),
        out_shape=jax.ShapeDtypeStruct((NSA_W, n), F32),
        scratch_shapes=[pltpu.VMEM((2, LANES, NSA_HPG * NSA_Q), F32),
                        pltpu.VMEM((4, kt, NSA_HPG * NSA_Q), F32)],
        compiler_params=_cparams(("parallel", "parallel", "arbitrary"), 56),
        name="nsa_attention",
    )(qt, gt, kc, vct, okv, okv, vt, vt, expand_t, overlap)


def _split_bf16(a):
    hi = a.astype(BF16)
    return hi, (a - hi.astype(F32)).astype(BF16)


def _unit_lower_inverses(lmats):
    c = lmats[0].shape[0]
    r = lax.broadcasted_iota(jnp.int32, (c, c), 0)
    col = lax.broadcasted_iota(jnp.int32, (c, c), 1)
    eye = jnp.where(r == col, 1.0, 0.0)
    xs = [eye - l for l in lmats]
    ps = []
    for l in lmats:
        l16 = l.astype(BF16)
        ps.append(_dot(l16, l16))
    steps = int(np.log2(c)) - 1
    for s in range(steps):
        last = s + 1 == steps
        for i in range(len(lmats)):
            rhs = ps[i].astype(BF16)
            if last:
                xs[i] = xs[i] + _dot(xs[i].astype(BF16), rhs)
            else:
                both = _dot(jnp.concatenate([xs[i], ps[i]], axis=0).astype(BF16), rhs)
                xs[i] = xs[i] + both[:c]
                ps[i] = both[c:]
    return xs


def _gdn_body(x_ref, z_ref, ab_ref, alog_ref, dtb_ref, og_ref, o_ref, s_sc, *, ct):
    nb = x_ref.shape[0]

    @pl.when(pl.program_id(0) == 0)
    def _():
        s_sc[...] = jnp.zeros(s_sc.shape, F32)

    ch = GDN_CHUNK
    r = lax.broadcasted_iota(jnp.int32, (ch, ch), 0)
    col = lax.broadcasted_iota(jnp.int32, (ch, ch), 1)
    incl = r >= col
    strict = r > col
    tril16 = jnp.concatenate([jnp.where(incl, 1.0, 0.0).astype(BF16)] * 3, axis=1)

    units = []
    for ci in range(ct // ch):
        rows = slice(ci * ch, (ci + 1) * ch)
        for bi in range(nb):
            ab = ab_ref[bi, rows, :]
            g_all = -jnp.exp(alog_ref[...]) * jax.nn.softplus(ab + dtb_ref[...])
            beta_all = jax.nn.sigmoid(ab)
            g_hi, g_lo = _split_bf16(g_all)
            g_lo2 = (g_all - g_hi.astype(F32) - g_lo.astype(F32)).astype(BF16)
            gc_all = _dot(tril16, jnp.concatenate([g_hi, g_lo, g_lo2], axis=0))
            gc_t = gc_all.T
            for h in range(GDN_HEADS):
                hs = slice(h * GDN_DH, (h + 1) * GDN_DH)
                q16 = x_ref[bi, rows, hs]
                k16 = x_ref[bi, rows, GDN_W + h * GDN_DH:GDN_W + (h + 1) * GDN_DH]
                qh, kh = q16.astype(F32), k16.astype(F32)
                vh = x_ref[bi, rows, 2 * GDN_W + h * GDN_DH:2 * GDN_W + (h + 1) * GDN_DH].astype(F32)
                gc = gc_all[:, h:h + 1]
                gr = gc_t[h:h + 1, :]
                g_last = gc_all[ch - 1:ch, h:h + 1]
                beta = beta_all[:, GDN_HEADS + h:GDN_HEADS + h + 1]
                eg = jnp.exp(gc)
                decay = jnp.where(incl, jnp.exp(jnp.minimum(gc - gr, 0.0)), 0.0)
                kb = kh * beta
                with_k = (_dot_nt(jnp.concatenate([kb.astype(BF16), q16], axis=0), k16)
                          * jnp.concatenate([decay, decay], axis=0))
                units.append(dict(
                    rows=rows, bi=bi, h=h,
                    lmat=jnp.where(strict, with_k[:ch], 0.0),
                    vb_kbg=jnp.concatenate([(vh * beta).astype(BF16), (kb * eg).astype(BF16)], axis=1),
                    qk=jnp.where(incl, with_k[ch:], 0.0).astype(BF16),
                    qg=(qh * eg).astype(BF16), kd_t=(kh * jnp.exp(g_last - gc)).T.astype(BF16),
                    gl=jnp.exp(g_last)))
    tinvs = _unit_lower_inverses([u["lmat"] for u in units])
    for u, tinv in zip(units, tinvs):
        u_w = _dot(tinv.astype(BF16), u["vb_kbg"])
        u["u"] = u_w[:, :GDN_DH]
        u["w_qg"] = jnp.concatenate([u_w[:, GDN_DH:].astype(BF16), u["qg"]], axis=0)
        u["kd_qk"] = jnp.concatenate([u["kd_t"], u["qk"]], axis=0)

    for u in units:
        bi, h, rows = u["bi"], u["h"], u["rows"]
        hs = slice(h * GDN_DH, (h + 1) * GDN_DH)
        s_old = s_sc[bi * GDN_HEADS + h]
        from_state = _dot(u["w_qg"], s_old.astype(BF16))
        v_new = (u["u"] - from_state[:ch]).astype(BF16)
        from_v = _dot(u["kd_qk"], v_new)
        s_sc[bi * GDN_HEADS + h] = s_old * u["gl"] + from_v[:GDN_DH]
        o = from_state[ch:] + from_v[GDN_DH:]
        on = o * lax.rsqrt(jnp.mean(o * o, axis=-1, keepdims=True) + EPS) * og_ref[...]
        zh = z_ref[bi, rows, hs].astype(F32)
        o_ref[bi, rows, hs] = (on * (zh * jax.nn.sigmoid(zh))).astype(BF16)


def _gdn(oqkv, oz, oab, alog, dtb, og, ct=128):
    b, t, w3 = oqkv.shape
    full = lambda a: pl.BlockSpec(a.shape, lambda c: (0,) * a.ndim)
    return pl.pallas_call(
        functools.partial(_gdn_body, ct=ct),
        grid=(t // ct,),
        in_specs=[
            pl.BlockSpec((b, ct, w3), lambda c: (0, c, 0)),
            pl.BlockSpec((b, ct, GDN_W), lambda c: (0, c, 0)),
            pl.BlockSpec((b, ct, LANES), lambda c: (0, c, 0)),
            full(alog), full(dtb), full(og),
        ],
        out_specs=pl.BlockSpec((b, ct, GDN_W), lambda c: (0, c, 0)),
        out_shape=jax.ShapeDtypeStruct((b, t, GDN_W), BF16),
        scratch_shapes=[pltpu.VMEM((b * GDN_HEADS, GDN_DH, GDN_DH), F32)],
        compiler_params=_cparams(("arbitrary",), 32),
        name="gdn",
    )(oqkv, oz, oab, alog, dtb, og)


def _outproj_body(ont_ref, og_ref, x_ref, ng_ref, wo_ref, fg_ref, wr_ref, br_ref, upper_ref,
                  x1_ref, h2_ref, gate_ref, route_ref, cnt_ref, cnt_sc):
    i = pl.program_id(0)
    tm = x_ref.shape[0]

    @pl.when(i == 0)
    def _():
        cnt_sc[...] = jnp.zeros(cnt_sc.shape, F32)

    a = ont_ref[...]
    a = (a * lax.rsqrt(jnp.mean(a * a, axis=0, keepdims=True) + EPS) * ng_ref[...]).astype(BF16)
    x1 = x_ref[...] + _dot_tn(a, wo_ref[0:NSA_W, :]) + _dot(og_ref[...], wo_ref[NSA_W:, :])
    x1_ref[...] = x1
    h2f = x1 * lax.rsqrt(jnp.mean(x1 * x1, axis=-1, keepdims=True) + EPS) * fg_ref[...]
    _store_pieces(h2_ref, h2f)
    h2 = h2f.astype(BF16)

    logits = (_dot(h2, wr_ref[...]) + br_ref[...]).T[:N_EXPERTS]
    erow = lax.broadcasted_iota(jnp.int32, (N_EXPERTS, tm), 0)
    onehot = jnp.zeros((N_EXPERTS, tm), F32)
    firsts, vals = [], []
    v = logits
    for k in range(TOP_K):
        mx = jnp.max(v, axis=0, keepdims=True)
        first = jnp.min(jnp.where(v == mx, erow, N_EXPERTS), axis=0, keepdims=True)
        hit = erow == first
        v = jnp.where(hit, -jnp.inf, v)
        onehot = jnp.where(hit, 1.0, onehot)
        firsts.append(first)
        vals.append(mx)
    vals = [jnp.exp(m - vals[0]) for m in vals]
    inv = 1.0 / (vals[0] + vals[1] + vals[2] + vals[3])
    gates_t = jnp.concatenate([m * inv for m in vals] + [jnp.zeros((LANES - TOP_K, tm), F32)], axis=0)
    gate_ref[...] = gates_t.T

    excl = cnt_sc[...] + _dot(onehot.astype(BF16), upper_ref[...])
    for k in range(TOP_K):
        route_ref[k:k + 1, :] = firsts[k]
        rank = jnp.sum(jnp.where(erow == firsts[k], excl, 0.0), axis=0, keepdims=True)
        route_ref[TOP_K + k:TOP_K + k + 1, :] = rank.astype(jnp.int32)
    cnt_sc[...] = cnt_sc[...] + jnp.sum(onehot, axis=1, keepdims=True)
    cnt_ref[...] = cnt_sc[...].astype(jnp.int32)


def _out_proj(o_nsa_t, o_gdn, x2, ng, wo, fg, wr, br, tm=512):
    upper = jnp.asarray(np.arange(tm)[:, None] < np.arange(tm)[None, :], BF16)
    n, d = x2.shape
    full = lambda a: pl.BlockSpec(a.shape, lambda i: (0,) * a.ndim)
    row = lambda w: pl.BlockSpec((tm, w), lambda i: (i, 0))
    return pl.pallas_call(
        _outproj_body,
        grid=(n // tm,),
        in_specs=[pl.BlockSpec((NSA_W, tm), lambda i: (0, i)), row(GDN_W), row(d), full(ng), full(wo), full(fg),
                  full(wr), full(br), full(upper)],
        out_specs=[row(d), pl.BlockSpec((d // 2 // SC_SUBROW, tm, SC_SUBROW), lambda i: (0, i, 0)),
                   row(LANES), pl.BlockSpec((2 * TOP_K, tm), lambda i: (0, i)),
                   pl.BlockSpec((N_EXPERTS, 1), lambda i: (0, 0))],
        out_shape=[jax.ShapeDtypeStruct((n, d), F32),
                   jax.ShapeDtypeStruct((d // 2 // SC_SUBROW, n, SC_SUBROW), jnp.int32),
                   jax.ShapeDtypeStruct((n, LANES), F32), jax.ShapeDtypeStruct((2 * TOP_K, n), jnp.int32),
                   jax.ShapeDtypeStruct((N_EXPERTS, 1), jnp.int32)],
        scratch_shapes=[pltpu.VMEM((N_EXPERTS, 1), F32)],
        compiler_params=_cparams(("arbitrary",), 48),
        name="out_proj_router",
    )(o_nsa_t, o_gdn, x2, ng, wo, fg, wr, br, upper)


def _dest_body(ps_ref, route_ref, o_ref, *, n_rows, pieces):
    expert = route_ref[0:TOP_K, :]
    start = jnp.zeros(expert.shape, jnp.int32)
    for e in range(N_EXPERTS):
        start = jnp.where(expert == e, ps_ref[e], start)
    dest = start + route_ref[TOP_K:2 * TOP_K, :]
    for k in range(TOP_K):
        for j in range(pieces):
            o_ref[k * pieces + j:k * pieces + j + 1, :] = dest[k:k + 1, :] + j * n_rows


def _dest_rows(pstarts, route, n_rows, pieces):
    n = route.shape[1]
    tn = min(2048, n)
    grid_spec = pltpu.PrefetchScalarGridSpec(
        num_scalar_prefetch=1,
        grid=(n // tn,),
        in_specs=[pl.BlockSpec((2 * TOP_K, tn), lambda i, ps: (0, i))],
        out_specs=pl.BlockSpec((TOP_K * pieces, tn), lambda i, ps: (0, i)),
    )
    return pl.pallas_call(
        functools.partial(_dest_body, n_rows=n_rows, pieces=pieces),
        grid_spec=grid_spec,
        out_shape=jax.ShapeDtypeStruct((TOP_K * pieces, n), jnp.int32),
        name="moe_dest_rows",
    )(pstarts, route)


def _expert_body(be_ref, fresh_ref, slot_ref, next_ref, xs_ref, wg_hbm, bg_ref, wu_hbm, bu_ref, wd_hbm, bd_ref,
                 y_ref, wf32, w16, sems):
    i = pl.program_id(0)
    used = i < be_ref[pl.num_programs(0)]
    hbm = (wg_hbm, wu_hbm, wd_hbm)

    def weight_copy(expert, slot, j):
        return pltpu.make_async_copy(hbm[j].at[expert], wf32.at[slot, j], sems.at[slot, j])

    @pl.when((i == 0) & used)
    def _():
        for j in range(3):
            weight_copy(be_ref[0], 0, j).start()

    @pl.when(used & (fresh_ref[i] == 1))
    def _():
        slot = slot_ref[i]
        for j in range(3):
            weight_copy(be_ref[i], slot, j).wait()
            w16[j] = wf32[slot, j].astype(BF16)

        @pl.when(next_ref[i] >= 0)
        def _():
            for j in range(3):
                weight_copy(next_ref[i], 1 - slot, j).start()

    @pl.when(jnp.logical_not(used))
    def _():
        y_ref[...] = jnp.zeros(y_ref.shape, y_ref.dtype)

    @pl.when(used)
    def _():
        x = _join_pieces(xs_ref).astype(BF16)
        gate = jnp.minimum(_dot(x, w16[0]) + bg_ref[0], SWIGLU_LIMIT)
        up = jnp.clip(_dot(x, w16[1]) + bu_ref[0], -SWIGLU_LIMIT, SWIGLU_LIMIT)
        glu = gate * jax.nn.sigmoid(gate * SWIGLU_ALPHA)
        _store_pieces(y_ref, _dot(((up + 1.0) * glu).astype(BF16), w16[2]) + bd_ref[0])


def _experts(blk_e, fresh, slot, nxt, xs, wg, bg, wu, bu, wd, bd):
    pieces, n_rows, sub = xs.shape
    d, de = wg.shape[1], wg.shape[2]
    assert d == de
    r = MOE_ROW_BLOCK
    bspec = lambda w: pl.BlockSpec((1, 1, w), lambda i, be, *_: (be[i], 0, 0))
    hbm = pl.BlockSpec(memory_space=pl.ANY)
    grid_spec = pltpu.PrefetchScalarGridSpec(
        num_scalar_prefetch=4,
        grid=(n_rows // r,),
        in_specs=[pl.BlockSpec((pieces, r, sub), lambda i, *_: (0, i, 0)),
                  hbm, bspec(de), hbm, bspec(de), hbm, bspec(d)],
        out_specs=pl.BlockSpec((pieces, r, sub), lambda i, *_: (0, i, 0)),
        scratch_shapes=[pltpu.VMEM((2, 3, d, de), F32), pltpu.VMEM((3, d, de), BF16),
                        pltpu.SemaphoreType.DMA((2, 3))],
    )
    return pl.pallas_call(
        _expert_body,
        grid_spec=grid_spec,
        out_shape=jax.ShapeDtypeStruct((pieces, n_rows, sub), jnp.int32),
        compiler_params=_cparams(("arbitrary",), 56),
        name="moe_experts",
    )(blk_e, fresh, slot, nxt, xs, wg, bg, wu, bu, wd, bd)


SC_WINDOW = 128
SC_SUBROW = 256


def _sc_mesh():
    return plsc.VectorSubcoreMesh(core_axis_name="c", subcore_axis_name="s")


def _sc_dispatch(h2, dest_rows, n_rows):
    n, d = h2.shape

    @functools.partial(pl.kernel, out_type=jax.ShapeDtypeStruct((n_rows, d), h2.dtype), mesh=_sc_mesh())
    def dispatch(x_hbm, *refs):
        idx_hbm, o_hbm = refs[:TOP_K], refs[TOP_K]

        def body(x_vmem, *idx_vmem):
            for iv in idx_vmem:
                pltpu.sync_copy(x_vmem, o_hbm.at[iv.at[0]])

        pltpu.emit_pipeline(
            body,
            grid=(n // SC_WINDOW,),
            in_specs=[pl.BlockSpec((SC_WINDOW, d), lambda i: (i, 0))]
                     + [pl.BlockSpec((1, SC_WINDOW), lambda i: (0, i))] * TOP_K,
            out_specs=[],
            core_axis_name=("c", "s"),
            dimension_semantics=(pltpu.PARALLEL,),
        )(x_hbm, *idx_hbm)

    return dispatch(h2, *dest_rows)


def _sc_gather(table, idx):
    _, d = table.shape
    m = idx.shape[1]

    @functools.partial(pl.kernel, out_type=jax.ShapeDtypeStruct((m, d), table.dtype), mesh=_sc_mesh())
    def gather(t_hbm, i_hbm, o_hbm):
        def body(i_vmem, o_vmem):
            pltpu.sync_copy(t_hbm.at[i_vmem.at[0]], o_vmem)

        pltpu.emit_pipeline(
            body,
            grid=(m // SC_WINDOW,),
            in_specs=[pl.BlockSpec((1, SC_WINDOW), lambda i: (0, i))],
            out_specs=[pl.BlockSpec((SC_WINDOW, d), lambda i: (i, 0))],
            core_axis_name=("c", "s"),
            dimension_semantics=(pltpu.PARALLEL,),
        )(i_hbm, o_hbm)

    return gather(table, idx)


def _combine_body(x1_ref, y_ref, gate_ref, o_ref):
    acc = x1_ref[...]
    for k in range(TOP_K):
        acc = acc + gate_ref[:, k:k + 1] * _join_pieces(y_ref.at[k])
    o_ref[...] = acc


def _combine(x1, y4, gates, tm=512):
    n, d = x1.shape
    pieces, sub = y4.shape[1], y4.shape[3]
    row = lambda w: pl.BlockSpec((tm, w), lambda i: (i, 0))
    return pl.pallas_call(
        _combine_body,
        grid=(n // tm,),
        in_specs=[row(d), pl.BlockSpec((TOP_K, pieces, tm, sub), lambda i: (0, 0, i, 0)), row(LANES)],
        out_specs=row(d),
        out_shape=jax.ShapeDtypeStruct((n, d), F32),
        compiler_params=_cparams(("parallel",), 48),
        name="moe_combine",
    )(x1, y4, gates)


def _pad_lanes(a, width=LANES):
    return jnp.pad(a, ((0, 0), (0, width - a.shape[1])))


def _layer(x, attn_norm_g, w_in, q_g, kc_g, ks_g, kw_g, ck_pos, ck_w1, ck_b1, ck_w2, ck_b2,
           cv_pos, cv_w1, cv_b1, cv_w2, cv_b2, nsa_out_g, conv_w, a_log, dt_bias, gdn_out_g, w_out,
           ffn_g, router_w, router_b, e_wg, e_bg, e_wu, e_bu, e_wd, e_bd):
    b, t, d = x.shape
    n = b * t
    x2 = x.reshape(n, d)

    o = np.cumsum([0, NSA_W] + [NSA_GROUPS * NSA_DH] * 6 + [3 * NSA_HEADS, 3 * GDN_W, GDN_W, GDN_HEADS, GDN_HEADS])
    wq_t = w_in[:, o[0]:o[1]].T.reshape(NSA_GROUPS, NSA_HPG, NSA_DH, d)
    zq = jnp.zeros((NSA_HPG, NSA_DH, d), F32)
    wq_t = jnp.stack([jnp.concatenate([wq_t[0], zq], axis=1), jnp.concatenate([zq, wq_t[1]], axis=1)])
    wq_t = wq_t.reshape(NSA_HEADS * LANES, d).astype(BF16)
    qg1 = q_g * (NSA_DH ** -0.5 * np.log2(np.e))
    zg = jnp.zeros((NSA_DH,), F32)
    qg_col = jnp.concatenate([jnp.tile(jnp.concatenate([qg1, zg]), NSA_HPG),
                              jnp.tile(jnp.concatenate([zg, qg1]), NSA_HPG)]).reshape(NSA_HEADS * LANES, 1)
    wkv = jnp.concatenate([w_in[:, o[1]:o[4]], w_in[:, o[5]:o[6]]], axis=1).astype(BF16)
    kg = jnp.concatenate([ks_g, ks_g, kw_g, kw_g]).reshape(1, 2 * LANES)
    wv_t = jnp.concatenate([w_in[:, o[4]:o[5]], w_in[:, o[6]:o[7]]], axis=1).T.reshape(2 * NSA_GROUPS, NSA_DH, d)
    wv_t = jnp.pad(wv_t, ((0, 0), (0, LANES - NSA_DH), (0, 0))).reshape(2 * NSA_GROUPS * LANES, d).astype(BF16)
    vone = jnp.asarray((np.arange(2 * NSA_GROUPS * LANES) % LANES == NSA_DH).astype(np.float32)[:, None])
    wg_t = w_in[:, o[7]:o[8]].T.reshape(NSA_GROUPS, NSA_HPG * 3, d)
    wg_t = jnp.pad(wg_t, ((0, 0), (0, GATE_ROWS - NSA_HPG * 3), (0, 0))).reshape(NSA_GROUPS * GATE_ROWS, d)
    wg_t = wg_t.astype(BF16)
    wab = _pad_lanes(w_in[:, o[10]:o[12]]).astype(BF16)
    wqkv = w_in[:, o[8]:o[9]].astype(BF16)
    wz = w_in[:, o[9]:o[10]].astype(BF16)

    tm = min(512, t)
    oqt, okn, xflat, ovt, ogt, oqkv, oz, oab = _in_proj(x2, attn_norm_g.reshape(1, d), wq_t, wkv, wv_t, wg_t, wqkv,
                                                        wz, wab, qg_col, kg, vone, conv_w, t // tm, tm)

    nch = t // CMP_STRIDE
    n_cmp = (t - CMP_BLOCK) // CMP_STRIDE + 1
    half = CMP_STRIDE * NSA_DH
    pos = jnp.stack([ck_pos, cv_pos]).reshape(2, 2, 1, half)
    w1 = jnp.stack([ck_w1, cv_w1]).reshape(2, 2, half, CMP_HIDDEN).astype(BF16)
    b1 = jnp.stack([ck_b1, cv_b1]).reshape(2, 1, CMP_HIDDEN)
    w2 = jnp.stack([ck_w2, cv_w2]).astype(BF16)
    b2 = jnp.stack([ck_b2, cv_b2]).reshape(2, 1, NSA_DH)
    w2t = jnp.stack([ck_w2.T, cv_w2.T]).astype(BF16)
    b2t = jnp.stack([ck_b2, cv_b2]).reshape(2, NSA_DH, 1)
    kc, vct = _compress(xflat, pos, w1, b1, w2, b2, w2t, b2t, kc_g.reshape(1, NSA_DH), n_cmp)

    n_slc = t // SLC_BLOCK
    n_top = min(SLC_TOPK, n_slc)
    nblk = max(n_slc, LANES)
    kt = min(256, t // 4)
    assert (t // kt) % 4 == 0
    ci = np.arange(nch)[None, :] * CMP_STRIDE
    sj = np.arange(nblk)[:, None] * SLC_BLOCK
    overlap = ((ci < sj + SLC_BLOCK) & (ci + CMP_BLOCK > sj) & (np.arange(nch)[None, :] < n_cmp)
               & (np.arange(nblk)[:, None] < n_slc))
    expand_t = (np.arange(t)[:, None] // SLC_BLOCK) == np.arange(nblk)[None, :]
    o_nsa_t = _nsa_attention(oqt, ogt, kc, vct, okn.reshape(b, t, -1), ovt, jnp.asarray(expand_t, BF16),
                             jnp.asarray(overlap, BF16), b, t, n_cmp, n_top, kt)

    alog_row = _pad_lanes(a_log.reshape(1, GDN_HEADS))
    dtb_row = _pad_lanes(dt_bias.reshape(1, GDN_HEADS))
    o_gdn = _gdn(oqkv.reshape(b, t, -1), oz.reshape(b, t, -1), oab.reshape(b, t, -1),
                 alog_row, dtb_row, gdn_out_g.reshape(1, GDN_DH))

    wr = _pad_lanes(router_w).astype(BF16)
    br = _pad_lanes(router_b.reshape(1, N_EXPERTS))
    x1, h2, gates, route, counts = _out_proj(
        o_nsa_t, o_gdn.reshape(n, GDN_W), x2, nsa_out_g.reshape(NSA_W, 1),
        w_out.astype(BF16), ffn_g.reshape(1, d), wr, br)

    r = MOE_ROW_BLOCK
    nk = n * TOP_K
    counts = counts[:, 0]
    pcounts = (counts + r - 1) // r * r
    pends = jnp.cumsum(pcounts)
    pstarts = pends - pcounts
    n_rows = (nk + r - 1) // r * r + N_EXPERTS * r
    n_blocks = n_rows // r
    blk_start = jnp.arange(n_blocks, dtype=jnp.int32)[:, None] * r
    blk_e = jnp.minimum(jnp.sum(pends[None, :] <= blk_start, axis=1), N_EXPERTS - 1).astype(jnp.int32)
    n_used = (pends[-1] // r).astype(jnp.int32)
    fresh = (jnp.arange(n_blocks) < n_used) & (blk_e != jnp.concatenate([jnp.full((1,), -1, jnp.int32), blk_e[:-1]]))
    slot = ((jnp.cumsum(fresh) - 1) % 2).astype(jnp.int32)
    eid = jnp.arange(N_EXPERTS, dtype=jnp.int32)
    later = jnp.where((eid[None, :] > eid[:, None]) & (pcounts[None, :] > 0), eid[None, :], N_EXPERTS)
    next_expert = jnp.min(later, axis=1)
    next_expert = jnp.where(next_expert < N_EXPERTS, next_expert, -1).astype(jnp.int32)
    nxt = next_expert[blk_e]
    blk_e = jnp.concatenate([blk_e, n_used[None]])
    pieces = d // 2 // SC_SUBROW
    dest_p = _dest_rows(pstarts.astype(jnp.int32), route, n_rows, pieces).reshape(TOP_K, pieces, n)
    xs = _sc_dispatch(h2.reshape(pieces * n, SC_SUBROW), [dest_p[k].reshape(1, pieces * n) for k in range(TOP_K)],
                      pieces * n_rows)
    ys = _experts(blk_e, fresh.astype(jnp.int32), slot, nxt, xs.reshape(pieces, n_rows, SC_SUBROW), e_wg, e_bg.reshape(N_EXPERTS, 1, -1), e_wu,
                  e_bu.reshape(N_EXPERTS, 1, -1), e_wd, e_bd.reshape(N_EXPERTS, 1, -1))
    y4 = _sc_gather(ys.reshape(pieces * n_rows, SC_SUBROW), dest_p.reshape(1, nk * pieces))
    return _combine(x1, y4.reshape(TOP_K, pieces, n, SC_SUBROW), gates).reshape(b, t, d)


def kernel(x, attn_norm_g, w_in, nsa_q_norm_g, nsa_kc_norm_g, nsa_ks_norm_g, nsa_kw_norm_g, cmp_k_pos, cmp_k_w1, cmp_k_b1, cmp_k_w2, cmp_k_b2, cmp_v_pos, cmp_v_w1, cmp_v_b1, cmp_v_w2, cmp_v_b2, nsa_out_norm_g, gdn_conv_w, gdn_a_log, gdn_dt_bias, gdn_out_norm_g, w_out, ffn_norm_g, router_w, router_b, exp_w_gate, exp_b_gate, exp_w_up, exp_b_up, exp_w_down, exp_b_down):
    params = (attn_norm_g, w_in, nsa_q_norm_g, nsa_kc_norm_g, nsa_ks_norm_g, nsa_kw_norm_g,
              cmp_k_pos, cmp_k_w1, cmp_k_b1, cmp_k_w2, cmp_k_b2, cmp_v_pos, cmp_v_w1, cmp_v_b1, cmp_v_w2, cmp_v_b2,
              nsa_out_norm_g, gdn_conv_w, gdn_a_log, gdn_dt_bias, gdn_out_norm_g, w_out, ffn_norm_g,
              router_w, router_b, exp_w_gate, exp_b_gate, exp_w_up, exp_b_up, exp_w_down, exp_b_down)
    for l in range(attn_norm_g.shape[0]):
        x = _layer(x, *(p[l] for p in params))
    return x
```

```python
import functools

import jax
import jax.numpy as jnp
import numpy as np
from jax import lax
from jax.experimental import pallas as pl
from jax.experimental.pallas import tpu as pltpu
from jax.experimental.pallas import tpu_sc as plsc

F32 = jnp.float32
BF16 = jnp.bfloat16

EPS = 1e-6
NEG = -1e30
MASKED = -2.0 ** 100

NSA_HEADS = 8
NSA_GROUPS = 2
NSA_HPG = 4
NSA_DH = 64
CMP_BLOCK = 32
CMP_STRIDE = 16
CMP_HIDDEN = 256
SLC_BLOCK = 64
SLC_TOPK = 16
WINDOW = 512
NSA_Q = 256
GDN_HEADS = 4
GDN_DH = 128
GDN_CHUNK = 64
N_EXPERTS = 32
TOP_K = 4
SWIGLU_LIMIT = 7.0
SWIGLU_ALPHA = 1.702
MOE_ROW_BLOCK = 256

LANES = 128
GATE_ROWS = 16
FLASH_BODY_TILES = (8, 4)
NSA_W = NSA_HEADS * NSA_DH
GDN_W = GDN_HEADS * GDN_DH

_NT = (((1,), (1,)), ((), ()))
_TN = (((0,), (0,)), ((), ()))


def _cparams(sem, vmem_mb):
    return pltpu.CompilerParams(dimension_semantics=sem, vmem_limit_bytes=vmem_mb * 1024 * 1024)


def _dot(a, b):
    return jnp.dot(a, b, preferred_element_type=F32)


def _dot_nt(a, b):
    return lax.dot_general(a, b, _NT, preferred_element_type=F32)


def _dot_tn(a, b):
    return lax.dot_general(a, b, _TN, preferred_element_type=F32)


def _pack_piece(block):
    words = block.shape[1] // 2
    hi = lax.bitcast_convert_type(block[:, :words].astype(BF16).astype(F32), jnp.uint32)
    lo = lax.bitcast_convert_type(block[:, words:].astype(BF16).astype(F32), jnp.uint32)
    return lax.bitcast_convert_type(hi | (lo >> 16), jnp.int32)


def _store_pieces(ref, val):
    cols = 2 * ref.shape[2]
    for j in range(ref.shape[0]):
        ref[j] = _pack_piece(val[:, j * cols:(j + 1) * cols])


def _join_pieces(ref):
    out = []
    for j in range(ref.shape[0]):
        words = lax.bitcast_convert_type(ref[j], jnp.uint32)
        out.append(lax.bitcast_convert_type(words & jnp.uint32(0xFFFF0000), F32))
        out.append(lax.bitcast_convert_type(words << 16, F32))
    return jnp.concatenate(out, axis=1)


def _inproj_body(x_ref, g_ref, wqt_ref, wkv_ref, wvt_ref, wgt_ref, wqkv_ref, wz_ref, wab_ref, qg_ref, kg_ref,
                 vone_ref, cw_ref, oqt_ref, okn_ref, ocf_ref, ovt_ref, ogt_ref, oqkv_ref, oz_ref, oab_ref, ybuf, cbuf,
                 *, tiles_per_seq):
    x = x_ref[...]
    h = (x * lax.rsqrt(jnp.mean(x * x, axis=-1, keepdims=True) + EPS) * g_ref[...]).astype(BF16)
    tm = x.shape[0]

    yq = _dot_nt(wqt_ref[...], h)
    for s in range(NSA_HEADS):
        sl = slice(s * NSA_DH, (s + 1) * NSA_DH)
        ys = yq[sl, :]
        ms = jnp.sum(ys * ys, axis=0, keepdims=True) * (1.0 / NSA_DH)
        oqt_ref[sl, :] = (ys * lax.rsqrt(ms + EPS) * qg_ref[sl, :]).astype(BF16)

    ykv = _dot(h, wkv_ref[...])
    lane = lax.broadcasted_iota(jnp.int32, (tm, LANES), 1)
    low = lane < NSA_DH
    for s in range(2):
        sl = slice(s * LANES, (s + 1) * LANES)
        ys = ykv[:, (2 + s) * LANES:(3 + s) * LANES]
        y2 = ys * ys
        s0 = jnp.sum(jnp.where(low, y2, 0.0), axis=-1, keepdims=True)
        s1 = jnp.sum(jnp.where(low, 0.0, y2), axis=-1, keepdims=True)
        ms = jnp.where(low, s0, s1) * (1.0 / NSA_DH)
        okn_ref[:, sl] = (ys * lax.rsqrt(ms + EPS) * kg_ref[:, sl]).astype(BF16)

    chunks = tm // CMP_STRIDE
    for br in range(2):
        cbuf[br] = ykv[:, br * LANES:(br + 1) * LANES]
        taken = [cbuf[br, pl.ds(l, chunks, stride=CMP_STRIDE), :] for l in range(CMP_STRIDE)]
        for grp in range(NSA_GROUPS):
            flat = jnp.concatenate([r[:, grp * NSA_DH:(grp + 1) * NSA_DH] for r in taken], axis=1)
            ocf_ref[0, br, grp] = flat.astype(BF16)

    ovt_ref[...] = (_dot_nt(wvt_ref[...], h) + vone_ref[...]).astype(BF16)
    ogt_ref[...] = _dot_nt(wgt_ref[...], h)
    oz_ref[...] = _dot(h, wz_ref[...]).astype(BF16)
    oab_ref[...] = _dot(h, wab_ref[...])

    halo = ybuf.shape[0] - tm
    first = pl.program_id(0) % tiles_per_seq == 0

    @pl.when(first)
    def _():
        ybuf[0:halo, :] = jnp.zeros((halo, ybuf.shape[1]), F32)

    @pl.when(jnp.logical_not(first))
    def _():
        ybuf[0:halo, :] = ybuf[tm:tm + halo, :]

    ybuf[halo:halo + tm, :] = _dot(h, wqkv_ref[...])
    taps = cw_ref.shape[0]
    y = cw_ref[0:1, :] * ybuf[pl.ds(halo - taps + 1, tm), :]
    for k in range(1, taps):
        y = y + cw_ref[k:k + 1, :] * ybuf[pl.ds(halo - taps + 1 + k, tm), :]
    hy = 0.5 * y
    y = hy + hy * jnp.tanh(hy)
    for s in range(3 * GDN_HEADS):
        sl = slice(s * GDN_DH, (s + 1) * GDN_DH)
        ys = y[:, sl]
        if s < 2 * GDN_HEADS:
            scale = GDN_DH ** -0.5 if s < GDN_HEADS else 1.0
            ys = ys * (lax.rsqrt(jnp.sum(ys * ys, axis=-1, keepdims=True) + EPS) * scale)
        oqkv_ref[:, sl] = ys.astype(BF16)


def _in_proj(x2, g, wqt, wkv, wvt, wgt, wqkv, wz, wab, qg, kg, vone, conv_w, tiles_per_seq, tm):
    n, d = x2.shape
    chunks, flat = tm // CMP_STRIDE, CMP_STRIDE * NSA_DH
    full = lambda a: pl.BlockSpec(a.shape, lambda i: (0,) * a.ndim)
    row = lambda w: pl.BlockSpec((tm, w), lambda i: (i, 0))
    colb = lambda r: pl.BlockSpec((r, tm), lambda i: (0, i))
    return pl.pallas_call(
        functools.partial(_inproj_body, tiles_per_seq=tiles_per_seq),
        grid=(n // tm,),
        in_specs=[row(d)] + [full(a) for a in (g, wqt, wkv, wvt, wgt, wqkv, wz, wab, qg, kg, vone, conv_w)],
        out_specs=[colb(wqt.shape[0]), row(2 * LANES),
                   pl.BlockSpec((1, 2, NSA_GROUPS, chunks, flat), lambda i: (i // tiles_per_seq, 0, 0, i % tiles_per_seq, 0)),
                   colb(wvt.shape[0]), colb(wgt.shape[0]), row(wqkv.shape[1]), row(wz.shape[1]), row(wab.shape[1])],
        out_shape=[jax.ShapeDtypeStruct((wqt.shape[0], n), BF16), jax.ShapeDtypeStruct((n, 2 * LANES), BF16),
                   jax.ShapeDtypeStruct((n // (tm * tiles_per_seq), 2, NSA_GROUPS, chunks * tiles_per_seq, flat), BF16),
                   jax.ShapeDtypeStruct((wvt.shape[0], n), BF16), jax.ShapeDtypeStruct((wgt.shape[0], n), F32),
                   jax.ShapeDtypeStruct((n, wqkv.shape[1]), BF16), jax.ShapeDtypeStruct((n, wz.shape[1]), BF16),
                   jax.ShapeDtypeStruct((n, wab.shape[1]), F32)],
        scratch_shapes=[pltpu.VMEM((tm + 8, wqkv.shape[1]), F32), pltpu.VMEM((2, tm, LANES), F32)],
        compiler_params=_cparams(("arbitrary",), 56),
        name="in_proj",
    )(x2, g, wqt, wkv, wvt, wgt, wqkv, wz, wab, qg, kg, vone, conv_w)


def _compress_body(x_ref, pos_ref, w1_ref, b1_ref, w2_ref, b2_ref, w2t_ref, b2t_ref, g_ref, ok_ref, ovt_ref,
                   *, n_cmp):
    is_key = pl.program_id(1) == 0
    nch = x_ref.shape[3]
    hids = []
    for grp in range(NSA_GROUPS):
        x = x_ref[0, 0, grp].astype(F32)
        xa = (x + pos_ref[0, 0]).astype(BF16)
        xb = (x + pos_ref[0, 1]).astype(BF16)
        a = _dot(xa, w1_ref[0, 0])
        b = _dot(xb, w1_ref[0, 1])
        b_next = pltpu.roll(b, nch - 1, 0)
        hids.append(jax.nn.gelu(a + b_next + b1_ref[0]).astype(BF16))

    @pl.when(is_key)
    def _():
        row = lax.broadcasted_iota(jnp.int32, (nch, NSA_DH), 0)
        outs = []
        for grp in range(NSA_GROUPS):
            out = _dot(hids[grp], w2_ref[0]) + b2_ref[0]
            out = out * lax.rsqrt(jnp.mean(out * out, axis=-1, keepdims=True) + EPS) * g_ref[...]
            outs.append(jnp.where(row < n_cmp, out, 0.0))
        ok_ref[0] = jnp.concatenate(outs, axis=-1).astype(BF16)

    @pl.when(jnp.logical_not(is_key))
    def _():
        col = lax.broadcasted_iota(jnp.int32, (NSA_DH, nch), 1)
        outs = []
        for grp in range(NSA_GROUPS):
            out = _dot_nt(w2t_ref[0], hids[grp]) + b2t_ref[0]
            outs.append(jnp.where(col < n_cmp, out, 0.0))
        ovt_ref[0] = jnp.concatenate(outs, axis=0).astype(BF16)


def _compress(xflat, pos, w1, b1, w2, b2, w2t, b2t, kc_g, n_cmp):
    b, _, _, nch, flat = xflat.shape
    return pl.pallas_call(
        functools.partial(_compress_body, n_cmp=n_cmp),
        grid=(b, 2),
        in_specs=[
            pl.BlockSpec((1, 1, NSA_GROUPS, nch, flat), lambda i, j: (i, j, 0, 0, 0)),
            pl.BlockSpec((1, 2, 1, flat), lambda i, j: (j, 0, 0, 0)),
            pl.BlockSpec((1, 2, flat, CMP_HIDDEN), lambda i, j: (j, 0, 0, 0)),
            pl.BlockSpec((1, 1, CMP_HIDDEN), lambda i, j: (j, 0, 0)),
            pl.BlockSpec((1, CMP_HIDDEN, NSA_DH), lambda i, j: (j, 0, 0)),
            pl.BlockSpec((1, 1, NSA_DH), lambda i, j: (j, 0, 0)),
            pl.BlockSpec((1, NSA_DH, CMP_HIDDEN), lambda i, j: (j, 0, 0)),
            pl.BlockSpec((1, NSA_DH, 1), lambda i, j: (j, 0, 0)),
            pl.BlockSpec((1, NSA_DH), lambda i, j: (0, 0)),
        ],
        out_specs=[pl.BlockSpec((1, nch, LANES), lambda i, j: (i, 0, 0)),
                   pl.BlockSpec((1, LANES, nch), lambda i, j: (i, 0, 0))],
        out_shape=[jax.ShapeDtypeStruct((b, nch, LANES), BF16), jax.ShapeDtypeStruct((b, LANES, nch), BF16)],
        compiler_params=_cparams(("parallel", "arbitrary"), 32),
        name="nsa_compress",
    )(xflat, pos, w1, b1, w2, b2, w2t, b2t, kc_g)


def _tile_heads(a):
    return jnp.concatenate([a] * NSA_HPG, axis=1)


def _nsa_body(qt_ref, gt_ref, kc_ref, vct_ref, ks_ref, kw_ref, vst_ref, vwt_ref, et_ref, ov_ref, o_ref, acc_sc, s_sc,
              *, n_cmp, n_top, kt):
    grp = pl.program_id(1)
    s0 = pl.program_id(2) * NSA_Q
    nch = kc_ref.shape[1]
    nblk = ov_ref.shape[0]

    qh = jnp.concatenate([qt_ref[h * NSA_DH:(h + 1) * NSA_DH, :] for h in range(NSA_HPG)], axis=1)
    zq = jnp.zeros_like(qh)
    qt = jnp.where(grp == 0, jnp.concatenate([qh, zq], axis=0), jnp.concatenate([zq, qh], axis=0))
    t_row = s0 + lax.broadcasted_iota(jnp.int32, (1, NSA_Q), 1)

    cidx = lax.broadcasted_iota(jnp.int32, (nch, 1), 0)
    cvalid = (cidx * CMP_STRIDE + (CMP_BLOCK - 1) <= t_row) & (cidx < n_cmp)
    sc = _dot(kc_ref[0], qt) + _tile_heads(jnp.where(cvalid, 0.0, NEG))
    pc = jnp.exp2(sc - jnp.max(sc, axis=0, keepdims=True)).astype(BF16)
    stacked = jnp.concatenate([vct_ref[0], ov_ref[...], jnp.ones((8, nch), BF16)], axis=0)
    res = _dot(stacked, pc)
    inv = jnp.where(_tile_heads(t_row >= CMP_BLOCK - 1), 1.0 / jnp.maximum(res[LANES + nblk:LANES + nblk + 1], 1e-30),
                    0.0)
    oc = res[:LANES] * inv
    imp4 = res[LANES:LANES + nblk] * inv
    imp = (imp4[:, 0:NSA_Q] + imp4[:, NSA_Q:2 * NSA_Q] + imp4[:, 2 * NSA_Q:3 * NSA_Q]
           + imp4[:, 3 * NSA_Q:4 * NSA_Q])
    blk = lax.broadcasted_iota(jnp.int32, (nblk, NSA_Q), 0)
    cur = t_row // SLC_BLOCK
    imp = jnp.where(blk * SLC_BLOCK > t_row, NEG, imp)
    imp = jnp.where((blk == 0) | (blk == cur) | (blk == cur - 1), -NEG, imp)

    def pick_rounds(v, rounds):
        for _ in range(rounds):
            mx = jnp.max(v, axis=0, keepdims=True)
            first = jnp.min(jnp.where(v == mx, blk, nblk), axis=0, keepdims=True)
            v = jnp.where(blk == first, -jnp.inf, v)
        return v

    quarter = n_top // 4
    picked = pick_rounds(imp, quarter)

    wlen = WINDOW + NSA_Q
    w0 = pl.multiple_of(jnp.maximum(s0 - WINDOW, 0), NSA_Q)
    kpos = w0 + lax.broadcasted_iota(jnp.int32, (wlen, 1), 0)
    wbias = jnp.where((kpos <= t_row) & (kpos > t_row - WINDOW), 0.0, NEG)
    sw = _dot(kw_ref[0, pl.ds(w0, wlen), :], qt) + _tile_heads(wbias)
    picked = pick_rounds(picked, quarter)
    pw = jnp.exp2(sw - jnp.max(sw, axis=0, keepdims=True)).astype(BF16)
    picked = pick_rounds(picked, quarter)
    ow = _dot(vwt_ref[:, pl.ds(w0, wlen)], pw)
    ow = ow[:NSA_DH] / ow[NSA_DH:NSA_DH + 1]

    picked = pick_rounds(picked, n_top - 3 * quarter)
    chosen = picked == -jnp.inf

    selb = jnp.where(chosen & (blk * SLC_BLOCK < s0), 0.0, MASKED).astype(BF16)
    rhs = jnp.concatenate([qt, _tile_heads(selb)], axis=0)

    d0 = pl.multiple_of(s0, NSA_Q)
    dpos = s0 + lax.broadcasted_iota(jnp.int32, (NSA_Q, 1), 0)
    selb_d = jnp.where(chosen & (blk <= cur), 0.0, MASKED).astype(BF16)
    sd = (_dot(jnp.concatenate([ks_ref[0, pl.ds(d0, NSA_Q), :], et_ref[pl.ds(d0, NSA_Q), :]], axis=1),
               jnp.concatenate([qt, _tile_heads(selb_d)], axis=0))
          + _tile_heads(jnp.where(dpos <= t_row, 0.0, NEG)))
    m_diag = jnp.max(sd, axis=0, keepdims=True)
    acc_sc[0] = _dot(vst_ref[:, pl.ds(d0, NSA_Q)], jnp.exp2(sd - m_diag).astype(BF16))
    acc_sc[1] = jnp.zeros(acc_sc.shape[1:], F32)

    last_tile = ks_ref.shape[1] // kt - 1

    def scores(idx, slot):
        k0 = pl.multiple_of(jnp.minimum(idx, last_tile) * kt, kt)
        lhs = jnp.concatenate([ks_ref[0, pl.ds(k0, kt), :], et_ref[pl.ds(k0, kt), :]], axis=1)
        s_sc[slot] = _dot(lhs, rhs)

    def update(idx, slot, m_old, acc_ref):
        k0 = pl.multiple_of(idx * kt, kt)
        m_new = jnp.maximum(m_old, jnp.max(s_sc[slot], axis=0, keepdims=True))
        p = jnp.exp2(s_sc[slot] - m_new).astype(BF16)
        acc_ref[...] = jnp.exp2(m_old - m_new) * acc_ref[...] + _dot(vst_ref[:, pl.ds(k0, kt)], p)
        return m_new

    def tile_group(first, carry, count):
        ms = list(carry)
        for t in range(count):
            scores(first + t + 2, (t + 2) % 4)
            ms[t % 2] = update(first + t, t % 4, ms[t % 2], acc_sc.at[t % 2])
        return tuple(ms)

    n_tiles = (s0 + kt - 1) // kt
    scores(0, 0)
    scores(1, 1)
    carry = (m_diag, jnp.full((1, NSA_HPG * NSA_Q), NEG, F32))
    done = 0
    for size in FLASH_BODY_TILES:
        left = n_tiles - done
        groups = (left + size - 1) // size if size == FLASH_BODY_TILES[-1] else left // size
        carry = lax.fori_loop(0, groups, lambda j, c, done=done, size=size: tile_group(done + size * j, c, size),
                              carry)
        done = done + size * groups
    m0, m1 = carry
    m_fin = jnp.maximum(m0, m1)
    acc = acc_sc[0] * jnp.exp2(m0 - m_fin) + acc_sc[1] * jnp.exp2(m1 - m_fin)
    osl = acc[:NSA_DH] / acc[NSA_DH:NSA_DH + 1]

    oc = jnp.where(grp == 0, oc[:NSA_DH], oc[NSA_DH:])
    gts = jax.nn.sigmoid(gt_ref[...])
    for h in range(NSA_HPG):
        cols = slice(h * NSA_Q, (h + 1) * NSA_Q)
        o_ref[h * NSA_DH:(h + 1) * NSA_DH, :] = (
            gts[3 * h:3 * h + 1, :] * oc[:, cols] + gts[3 * h + 1:3 * h + 2, :] * osl[:, cols]
            + gts[3 * h + 2:3 * h + 3, :] * ow[:, cols])


def _nsa_attention(qt, gt, kc, vct, okv, vt, expand_t, overlap, b, t, n_cmp, n_top, kt):
    nch = kc.shape[1]
    nq = t // NSA_Q
    n = b * t
    return pl.pallas_call(
        functools.partial(_nsa_body, n_cmp=n_cmp, n_top=n_top, kt=kt),
        grid=(b, NSA_GROUPS, nq),
        in_specs=[
            pl.BlockSpec((NSA_HPG * NSA_DH, NSA_Q), lambda bi, g, i: (g, bi * nq + i)),
            pl.BlockSpec((GATE_ROWS, NSA_Q), lambda bi, g, i: (g, bi * nq + i)),
            pl.BlockSpec((1, nch, LANES), lambda bi, g, i: (bi, 0, 0)),
            pl.BlockSpec((1, LANES, nch), lambda bi, g, i: (bi, 0, 0)),
            pl.BlockSpec((1, t, LANES), lambda bi, g, i: (bi, 0, 0)),
            pl.BlockSpec((1, t, LANES), lambda bi, g, i: (bi, 0, 1)),
            pl.BlockSpec((LANES, t), lambda bi, g, i: (g, bi)),
            pl.BlockSpec((LANES, t), lambda bi, g, i: (NSA_GROUPS + g, bi)),
            pl.BlockSpec(expand_t.shape, lambda bi, g, i: (0, 0)),
            pl.BlockSpec(overlap.shape, lambda bi, g, i: (0, 0)),
        ],
        out_specs=pl.BlockSpec((NSA_HPG * NSA_DH, NSA_Q), lambda bi, g, i: (g, bi * nq + i)),
        out_shape=jax.ShapeDtypeStruct((NSA_W, n), F32),
        scratch_shapes=[pltpu.VMEM((2, LANES, NSA_HPG * NSA_Q), F32),
                        pltpu.VMEM((4, kt, NSA_HPG * NSA_Q), F32)],
        compiler_params=_cparams(("parallel", "parallel", "arbitrary"), 56),
        name="nsa_attention",
    )(qt, gt, kc, vct, okv, okv, vt, vt, expand_t, overlap)


def _split_bf16(a):
    hi = a.astype(BF16)
    return hi, (a - hi.astype(F32)).astype(BF16)


def _unit_lower_inverses(lmats):
    c = lmats[0].shape[0]
    r = lax.broadcasted_iota(jnp.int32, (c, c), 0)
    col = lax.broadcasted_iota(jnp.int32, (c, c), 1)
    eye = jnp.where(r == col, 1.0, 0.0)
    xs = [eye - l for l in lmats]
    ps = []
    for l in lmats:
        l16 = l.astype(BF16)
        ps.append(_dot(l16, l16))
    steps = int(np.log2(c)) - 1
    for s in range(steps):
        last = s + 1 == steps
        for i in range(len(lmats)):
            rhs = ps[i].astype(BF16)
            if last:
                xs[i] = xs[i] + _dot(xs[i].astype(BF16), rhs)
            else:
                both = _dot(jnp.concatenate([xs[i], ps[i]], axis=0).astype(BF16), rhs)
                xs[i] = xs[i] + both[:c]
                ps[i] = both[c:]
    return xs


def _gdn_body(x_ref, z_ref, ab_ref, alog_ref, dtb_ref, og_ref, o_ref, s_sc, *, ct):
    nb = x_ref.shape[0]

    @pl.when(pl.program_id(0) == 0)
    def _():
        s_sc[...] = jnp.zeros(s_sc.shape, F32)

    ch = GDN_CHUNK
    r = lax.broadcasted_iota(jnp.int32, (ch, ch), 0)
    col = lax.broadcasted_iota(jnp.int32, (ch, ch), 1)
    incl = r >= col
    strict = r > col
    tril16 = jnp.concatenate([jnp.where(incl, 1.0, 0.0).astype(BF16)] * 3, axis=1)

    units = []
    for ci in range(ct // ch):
        rows = slice(ci * ch, (ci + 1) * ch)
        for bi in range(nb):
            ab = ab_ref[bi, rows, :]
            g_all = -jnp.exp(alog_ref[...]) * jax.nn.softplus(ab + dtb_ref[...])
            beta_all = jax.nn.sigmoid(ab)
            g_hi, g_lo = _split_bf16(g_all)
            g_lo2 = (g_all - g_hi.astype(F32) - g_lo.astype(F32)).astype(BF16)
            gc_all = _dot(tril16, jnp.concatenate([g_hi, g_lo, g_lo2], axis=0))
            gc_t = gc_all.T
            for h in range(GDN_HEADS):
                hs = slice(h * GDN_DH, (h + 1) * GDN_DH)
                q16 = x_ref[bi, rows, hs]
                k16 = x_ref[bi, rows, GDN_W + h * GDN_DH:GDN_W + (h + 1) * GDN_DH]
                qh, kh = q16.astype(F32), k16.astype(F32)
                vh = x_ref[bi, rows, 2 * GDN_W + h * GDN_DH:2 * GDN_W + (h + 1) * GDN_DH].astype(F32)
                gc = gc_all[:, h:h + 1]
                gr = gc_t[h:h + 1, :]
                g_last = gc_all[ch - 1:ch, h:h + 1]
                beta = beta_all[:, GDN_HEADS + h:GDN_HEADS + h + 1]
                eg = jnp.exp(gc)
                decay = jnp.where(incl, jnp.exp(jnp.minimum(gc - gr, 0.0)), 0.0)
                kb = kh * beta
                with_k = (_dot_nt(jnp.concatenate([kb.astype(BF16), q16], axis=0), k16)
                          * jnp.concatenate([decay, decay], axis=0))
                units.append(dict(
                    rows=rows, bi=bi, h=h,
                    lmat=jnp.where(strict, with_k[:ch], 0.0),
                    vb_kbg=jnp.concatenate([(vh * beta).astype(BF16), (kb * eg).astype(BF16)], axis=1),
                    qk=jnp.where(incl, with_k[ch:], 0.0).astype(BF16),
                    qg=(qh * eg).astype(BF16), kd_t=(kh * jnp.exp(g_last - gc)).T.astype(BF16),
                    gl=jnp.exp(g_last)))
    tinvs = _unit_lower_inverses([u["lmat"] for u in units])
    for u, tinv in zip(units, tinvs):
        u_w = _dot(tinv.astype(BF16), u["vb_kbg"])
        u["u"] = u_w[:, :GDN_DH]
        u["w_qg"] = jnp.concatenate([u_w[:, GDN_DH:].astype(BF16), u["qg"]], axis=0)
        u["kd_qk"] = jnp.concatenate([u["kd_t"], u["qk"]], axis=0)

    for u in units:
        bi, h, rows = u["bi"], u["h"], u["rows"]
        hs = slice(h * GDN_DH, (h + 1) * GDN_DH)
        s_old = s_sc[bi * GDN_HEADS + h]
        from_state = _dot(u["w_qg"], s_old.astype(BF16))
        v_new = (u["u"] - from_state[:ch]).astype(BF16)
        from_v = _dot(u["kd_qk"], v_new)
        s_sc[bi * GDN_HEADS + h] = s_old * u["gl"] + from_v[:GDN_DH]
        o = from_state[ch:] + from_v[GDN_DH:]
        on = o * lax.rsqrt(jnp.mean(o * o, axis=-1, keepdims=True) + EPS) * og_ref[...]
        zh = z_ref[bi, rows, hs].astype(F32)
        o_ref[bi, rows, hs] = (on * (zh * jax.nn.sigmoid(zh))).astype(BF16)


def _gdn(oqkv, oz, oab, alog, dtb, og, ct=128):
    b, t, w3 = oqkv.shape
    full = lambda a: pl.BlockSpec(a.shape, lambda c: (0,) * a.ndim)
    return pl.pallas_call(
        functools.partial(_gdn_body, ct=ct),
        grid=(t // ct,),
        in_specs=[
            pl.BlockSpec((b, ct, w3), lambda c: (0, c, 0)),
            pl.BlockSpec((b, ct, GDN_W), lambda c: (0, c, 0)),
            pl.BlockSpec((b, ct, LANES), lambda c: (0, c, 0)),
            full(alog), full(dtb), full(og),
        ],
        out_specs=pl.BlockSpec((b, ct, GDN_W), lambda c: (0, c, 0)),
        out_shape=jax.ShapeDtypeStruct((b, t, GDN_W), BF16),
        scratch_shapes=[pltpu.VMEM((b * GDN_HEADS, GDN_DH, GDN_DH), F32)],
        compiler_params=_cparams(("arbitrary",), 32),
        name="gdn",
    )(oqkv, oz, oab, alog, dtb, og)


def _outproj_body(ont_ref, og_ref, x_ref, ng_ref, wo_ref, fg_ref, wr_ref, br_ref, upper_ref,
                  x1_ref, h2_ref, gate_ref, route_ref, cnt_ref, cnt_sc):
    i = pl.program_id(0)
    tm = x_ref.shape[0]

    @pl.when(i == 0)
    def _():
        cnt_sc[...] = jnp.zeros(cnt_sc.shape, F32)

    a = ont_ref[...]
    a = (a * lax.rsqrt(jnp.mean(a * a, axis=0, keepdims=True) + EPS) * ng_ref[...]).astype(BF16)
    x1 = x_ref[...] + _dot_tn(a, wo_ref[0:NSA_W, :]) + _dot(og_ref[...], wo_ref[NSA_W:, :])
    x1_ref[...] = x1
    h2f = x1 * lax.rsqrt(jnp.mean(x1 * x1, axis=-1, keepdims=True) + EPS) * fg_ref[...]
    _store_pieces(h2_ref, h2f)
    h2 = h2f.astype(BF16)

    logits = (_dot(h2, wr_ref[...]) + br_ref[...]).T[:N_EXPERTS]
    erow = lax.broadcasted_iota(jnp.int32, (N_EXPERTS, tm), 0)
    onehot = jnp.zeros((N_EXPERTS, tm), F32)
    firsts, vals = [], []
    v = logits
    for k in range(TOP_K):
        mx = jnp.max(v, axis=0, keepdims=True)
        first = jnp.min(jnp.where(v == mx, erow, N_EXPERTS), axis=0, keepdims=True)
        hit = erow == first
        v = jnp.where(hit, -jnp.inf, v)
        onehot = jnp.where(hit, 1.0, onehot)
        firsts.append(first)
        vals.append(mx)
    vals = [jnp.exp(m - vals[0]) for m in vals]
    inv = 1.0 / (vals[0] + vals[1] + vals[2] + vals[3])
    gates_t = jnp.concatenate([m * inv for m in vals] + [jnp.zeros((LANES - TOP_K, tm), F32)], axis=0)
    gate_ref[...] = gates_t.T

    excl = cnt_sc[...] + _dot(onehot.astype(BF16), upper_ref[...])
    for k in range(TOP_K):
        route_ref[k:k + 1, :] = firsts[k]
        rank = jnp.sum(jnp.where(erow == firsts[k], excl, 0.0), axis=0, keepdims=True)
        route_ref[TOP_K + k:TOP_K + k + 1, :] = rank.astype(jnp.int32)
    cnt_sc[...] = cnt_sc[...] + jnp.sum(onehot, axis=1, keepdims=True)
    cnt_ref[...] = cnt_sc[...].astype(jnp.int32)


def _out_proj(o_nsa_t, o_gdn, x2, ng, wo, fg, wr, br, tm=512):
    upper = jnp.asarray(np.arange(tm)[:, None] < np.arange(tm)[None, :], BF16)
    n, d = x2.shape
    full = lambda a: pl.BlockSpec(a.shape, lambda i: (0,) * a.ndim)
    row = lambda w: pl.BlockSpec((tm, w), lambda i: (i, 0))
    return pl.pallas_call(
        _outproj_body,
        grid=(n // tm,),
        in_specs=[pl.BlockSpec((NSA_W, tm), lambda i: (0, i)), row(GDN_W), row(d), full(ng), full(wo), full(fg),
                  full(wr), full(br), full(upper)],
        out_specs=[row(d), pl.BlockSpec((d // 2 // SC_SUBROW, tm, SC_SUBROW), lambda i: (0, i, 0)),
                   row(LANES), pl.BlockSpec((2 * TOP_K, tm), lambda i: (0, i)),
                   pl.BlockSpec((N_EXPERTS, 1), lambda i: (0, 0))],
        out_shape=[jax.ShapeDtypeStruct((n, d), F32),
                   jax.ShapeDtypeStruct((d // 2 // SC_SUBROW, n, SC_SUBROW), jnp.int32),
                   jax.ShapeDtypeStruct((n, LANES), F32), jax.ShapeDtypeStruct((2 * TOP_K, n), jnp.int32),
                   jax.ShapeDtypeStruct((N_EXPERTS, 1), jnp.int32)],
        scratch_shapes=[pltpu.VMEM((N_EXPERTS, 1), F32)],
        compiler_params=_cparams(("arbitrary",), 48),
        name="out_proj_router",
    )(o_nsa_t, o_gdn, x2, ng, wo, fg, wr, br, upper)


def _dest_body(ps_ref, route_ref, o_ref, *, n_rows, pieces):
    expert = route_ref[0:TOP_K, :]
    start = jnp.zeros(expert.shape, jnp.int32)
    for e in range(N_EXPERTS):
        start = jnp.where(expert == e, ps_ref[e], start)
    dest = start + route_ref[TOP_K:2 * TOP_K, :]
    for k in range(TOP_K):
        for j in range(pieces):
            o_ref[k * pieces + j:k * pieces + j + 1, :] = dest[k:k + 1, :] + j * n_rows


def _dest_rows(pstarts, route, n_rows, pieces):
    n = route.shape[1]
    tn = min(2048, n)
    grid_spec = pltpu.PrefetchScalarGridSpec(
        num_scalar_prefetch=1,
        grid=(n // tn,),
        in_specs=[pl.BlockSpec((2 * TOP_K, tn), lambda i, ps: (0, i))],
        out_specs=pl.BlockSpec((TOP_K * pieces, tn), lambda i, ps: (0, i)),
    )
    return pl.pallas_call(
        functools.partial(_dest_body, n_rows=n_rows, pieces=pieces),
        grid_spec=grid_spec,
        out_shape=jax.ShapeDtypeStruct((TOP_K * pieces, n), jnp.int32),
        name="moe_dest_rows",
    )(pstarts, route)


def _expert_body(be_ref, end_ref, xs_ref, wg_hbm, bg_ref, wu_hbm, bu_ref, wd_hbm, bd_ref,
                 y_ref, wf32, w16, sems, slot_sc):
    i = pl.program_id(0)
    n_used = be_ref[pl.num_programs(0)]
    used = i < n_used
    expert = be_ref[i]
    fresh = used & ((i == 0) | (expert != be_ref[jnp.maximum(i - 1, 0)]))
    hbm = (wg_hbm, wu_hbm, wd_hbm)

    def weight_copy(e, slot, j):
        return pltpu.make_async_copy(hbm[j].at[e], wf32.at[slot, j], sems.at[slot, j])

    @pl.when(i == 0)
    def _():
        slot_sc[0] = 0

    @pl.when((i == 0) & used)
    def _():
        for j in range(3):
            weight_copy(expert, 0, j).start()

    @pl.when(fresh)
    def _():
        slot = slot_sc[0]
        for j in range(3):
            weight_copy(expert, slot, j).wait()
            w16[j] = wf32[slot, j].astype(BF16)
        following = end_ref[expert]

        @pl.when(following < n_used)
        def _():
            for j in range(3):
                weight_copy(be_ref[following], 1 - slot, j).start()

        slot_sc[0] = 1 - slot

    @pl.when(jnp.logical_not(used))
    def _():
        y_ref[...] = jnp.zeros(y_ref.shape, y_ref.dtype)

    @pl.when(used)
    def _():
        x = _join_pieces(xs_ref).astype(BF16)
        gate = jnp.minimum(_dot(x, w16[0]) + bg_ref[0], SWIGLU_LIMIT)
        up = jnp.clip(_dot(x, w16[1]) + bu_ref[0], -SWIGLU_LIMIT, SWIGLU_LIMIT)
        glu = gate * jax.nn.sigmoid(gate * SWIGLU_ALPHA)
        act = ((up + 1.0) * glu).astype(BF16)
        cols = 2 * y_ref.shape[2]
        for j in range(y_ref.shape[0]):
            sl = slice(j * cols, (j + 1) * cols)
            y_ref[j] = _pack_piece(_dot(act, w16[2, :, sl]) + bd_ref[0, :, sl])


def _experts(blk_e, end_blk, xs, wg, bg, wu, bu, wd, bd):
    pieces, n_rows, sub = xs.shape
    d, de = wg.shape[1], wg.shape[2]
    assert d == de
    r = MOE_ROW_BLOCK
    bspec = lambda w: pl.BlockSpec((1, 1, w), lambda i, be, *_: (be[i], 0, 0))
    hbm = pl.BlockSpec(memory_space=pl.ANY)
    grid_spec = pltpu.PrefetchScalarGridSpec(
        num_scalar_prefetch=2,
        grid=(n_rows // r,),
        in_specs=[pl.BlockSpec((pieces, r, sub), lambda i, *_: (0, i, 0)),
                  hbm, bspec(de), hbm, bspec(de), hbm, bspec(d)],
        out_specs=pl.BlockSpec((pieces, r, sub), lambda i, *_: (0, i, 0)),
        scratch_shapes=[pltpu.VMEM((2, 3, d, de), F32), pltpu.VMEM((3, d, de), BF16),
                        pltpu.SemaphoreType.DMA((2, 3)), pltpu.SMEM((1,), jnp.int32)],
    )
    return pl.pallas_call(
        _expert_body,
        grid_spec=grid_spec,
        out_shape=jax.ShapeDtypeStruct((pieces, n_rows, sub), jnp.int32),
        compiler_params=_cparams(("arbitrary",), 56),
        name="moe_experts",
    )(blk_e, end_blk, xs, wg, bg, wu, bu, wd, bd)


SC_WINDOW = 128
SC_SUBROW = 256


def _sc_mesh():
    return plsc.VectorSubcoreMesh(core_axis_name="c", subcore_axis_name="s")


def _sc_dispatch(h2, dest_rows, n_rows):
    n, d = h2.shape

    @functools.partial(pl.kernel, out_type=jax.ShapeDtypeStruct((n_rows, d), h2.dtype), mesh=_sc_mesh())
    def dispatch(x_hbm, *refs):
        idx_hbm, o_hbm = refs[:TOP_K], refs[TOP_K]

        def body(x_vmem, *idx_vmem):
            for iv in idx_vmem:
                pltpu.sync_copy(x_vmem, o_hbm.at[iv.at[0]])

        pltpu.emit_pipeline(
            body,
            grid=(n // SC_WINDOW,),
            in_specs=[pl.BlockSpec((SC_WINDOW, d), lambda i: (i, 0))]
                     + [pl.BlockSpec((1, SC_WINDOW), lambda i: (0, i))] * TOP_K,
            out_specs=[],
            core_axis_name=("c", "s"),
            dimension_semantics=(pltpu.PARALLEL,),
        )(x_hbm, *idx_hbm)

    return dispatch(h2, *dest_rows)


def _sc_gather(table, idx):
    _, d = table.shape
    m = idx.shape[1]

    @functools.partial(pl.kernel, out_type=jax.ShapeDtypeStruct((m, d), table.dtype), mesh=_sc_mesh())
    def gather(t_hbm, i_hbm, o_hbm):
        def body(i_vmem, o_vmem):
            pltpu.sync_copy(t_hbm.at[i_vmem.at[0]], o_vmem)

        pltpu.emit_pipeline(
            body,
            grid=(m // SC_WINDOW,),
            in_specs=[pl.BlockSpec((1, SC_WINDOW), lambda i: (0, i))],
            out_specs=[pl.BlockSpec((SC_WINDOW, d), lambda i: (i, 0))],
            core_axis_name=("c", "s"),
            dimension_semantics=(pltpu.PARALLEL,),
        )(i_hbm, o_hbm)

    return gather(table, idx)


def _combine_body(x1_ref, y_ref, gate_ref, o_ref):
    acc = x1_ref[...]
    for k in range(TOP_K):
        acc = acc + gate_ref[:, k:k + 1] * _join_pieces(y_ref.at[k])
    o_ref[...] = acc


def _combine(x1, y4, gates, tm=512):
    n, d = x1.shape
    pieces, sub = y4.shape[1], y4.shape[3]
    row = lambda w: pl.BlockSpec((tm, w), lambda i: (i, 0))
    return pl.pallas_call(
        _combine_body,
        grid=(n // tm,),
        in_specs=[row(d), pl.BlockSpec((TOP_K, pieces, tm, sub), lambda i: (0, 0, i, 0)), row(LANES)],
        out_specs=row(d),
        out_shape=jax.ShapeDtypeStruct((n, d), F32),
        compiler_params=_cparams(("parallel",), 48),
        name="moe_combine",
    )(x1, y4, gates)


def _pad_lanes(a, width=LANES):
    return jnp.pad(a, ((0, 0), (0, width - a.shape[1])))


def _layer(x, attn_norm_g, w_in, q_g, kc_g, ks_g, kw_g, ck_pos, ck_w1, ck_b1, ck_w2, ck_b2,
           cv_pos, cv_w1, cv_b1, cv_w2, cv_b2, nsa_out_g, conv_w, a_log, dt_bias, gdn_out_g, w_out,
           ffn_g, router_w, router_b, e_wg, e_bg, e_wu, e_bu, e_wd, e_bd):
    b, t, d = x.shape
    n = b * t
    x2 = x.reshape(n, d)

    o = np.cumsum([0, NSA_W] + [NSA_GROUPS * NSA_DH] * 6 + [3 * NSA_HEADS, 3 * GDN_W, GDN_W, GDN_HEADS, GDN_HEADS])
    wq_t = w_in[:, o[0]:o[1]].T.astype(BF16)
    qg_col = jnp.tile(q_g * (NSA_DH ** -0.5 * np.log2(np.e)), NSA_HEADS).reshape(NSA_W, 1)
    wkv = jnp.concatenate([w_in[:, o[1]:o[4]], w_in[:, o[5]:o[6]]], axis=1).astype(BF16)
    kg = jnp.concatenate([ks_g, ks_g, kw_g, kw_g]).reshape(1, 2 * LANES)
    wv_t = jnp.concatenate([w_in[:, o[4]:o[5]], w_in[:, o[6]:o[7]]], axis=1).T.reshape(2 * NSA_GROUPS, NSA_DH, d)
    wv_t = jnp.pad(wv_t, ((0, 0), (0, LANES - NSA_DH), (0, 0))).reshape(2 * NSA_GROUPS * LANES, d).astype(BF16)
    vone = jnp.asarray((np.arange(2 * NSA_GROUPS * LANES) % LANES == NSA_DH).astype(np.float32)[:, None])
    wg_t = w_in[:, o[7]:o[8]].T.reshape(NSA_GROUPS, NSA_HPG * 3, d)
    wg_t = jnp.pad(wg_t, ((0, 0), (0, GATE_ROWS - NSA_HPG * 3), (0, 0))).reshape(NSA_GROUPS * GATE_ROWS, d)
    wg_t = wg_t.astype(BF16)
    wab = _pad_lanes(w_in[:, o[10]:o[12]]).astype(BF16)
    wqkv = w_in[:, o[8]:o[9]].astype(BF16)
    wz = w_in[:, o[9]:o[10]].astype(BF16)

    tm = min(512, t)
    oqt, okn, xflat, ovt, ogt, oqkv, oz, oab = _in_proj(x2, attn_norm_g.reshape(1, d), wq_t, wkv, wv_t, wg_t, wqkv,
                                                        wz, wab, qg_col, kg, vone, conv_w, t // tm, tm)

    nch = t // CMP_STRIDE
    n_cmp = (t - CMP_BLOCK) // CMP_STRIDE + 1
    half = CMP_STRIDE * NSA_DH
    pos = jnp.stack([ck_pos, cv_pos]).reshape(2, 2, 1, half)
    w1 = jnp.stack([ck_w1, cv_w1]).reshape(2, 2, half, CMP_HIDDEN).astype(BF16)
    b1 = jnp.stack([ck_b1, cv_b1]).reshape(2, 1, CMP_HIDDEN)
    w2 = jnp.stack([ck_w2, cv_w2]).astype(BF16)
    b2 = jnp.stack([ck_b2, cv_b2]).reshape(2, 1, NSA_DH)
    w2t = jnp.stack([ck_w2.T, cv_w2.T]).astype(BF16)
    b2t = jnp.stack([ck_b2, cv_b2]).reshape(2, NSA_DH, 1)
    kc, vct = _compress(xflat, pos, w1, b1, w2, b2, w2t, b2t, kc_g.reshape(1, NSA_DH), n_cmp)

    n_slc = t // SLC_BLOCK
    n_top = min(SLC_TOPK, n_slc)
    nblk = max(n_slc, LANES)
    kt = min(256, t // 4)
    assert (t // kt) % 4 == 0
    ci = np.arange(nch)[None, :] * CMP_STRIDE
    sj = np.arange(nblk)[:, None] * SLC_BLOCK
    overlap = ((ci < sj + SLC_BLOCK) & (ci + CMP_BLOCK > sj) & (np.arange(nch)[None, :] < n_cmp)
               & (np.arange(nblk)[:, None] < n_slc))
    expand_t = (np.arange(t)[:, None] // SLC_BLOCK) == np.arange(nblk)[None, :]
    o_nsa_t = _nsa_attention(oqt, ogt, kc, vct, okn.reshape(b, t, -1), ovt, jnp.asarray(expand_t, BF16),
                             jnp.asarray(overlap, BF16), b, t, n_cmp, n_top, kt)

    alog_row = _pad_lanes(a_log.reshape(1, GDN_HEADS))
    dtb_row = _pad_lanes(dt_bias.reshape(1, GDN_HEADS))
    o_gdn = _gdn(oqkv.reshape(b, t, -1), oz.reshape(b, t, -1), oab.reshape(b, t, -1),
                 alog_row, dtb_row, gdn_out_g.reshape(1, GDN_DH))

    wr = _pad_lanes(router_w).astype(BF16)
    br = _pad_lanes(router_b.reshape(1, N_EXPERTS))
    x1, h2, gates, route, counts = _out_proj(
        o_nsa_t, o_gdn.reshape(n, GDN_W), x2, nsa_out_g.reshape(NSA_W, 1),
        w_out.astype(BF16), ffn_g.reshape(1, d), wr, br)

    r = MOE_ROW_BLOCK
    nk = n * TOP_K
    counts = counts[:, 0]
    pcounts = (counts + r - 1) // r * r
    pends = jnp.cumsum(pcounts)
    pstarts = pends - pcounts
    n_rows = (nk + r - 1) // r * r + N_EXPERTS * r
    n_blocks = n_rows // r
    blk_start = jnp.arange(n_blocks, dtype=jnp.int32)[:, None] * r
    blk_e = jnp.minimum(jnp.sum(pends[None, :] <= blk_start, axis=1), N_EXPERTS - 1).astype(jnp.int32)
    n_used = (pends[-1] // r).astype(jnp.int32)
    blk_e = jnp.concatenate([blk_e, n_used[None]])
    end_blk = (pends // r).astype(jnp.int32)
    pieces = d // 2 // SC_SUBROW
    dest_p = _dest_rows(pstarts.astype(jnp.int32), route, n_rows, pieces).reshape(TOP_K, pieces, n)
    xs = _sc_dispatch(h2.reshape(pieces * n, SC_SUBROW), [dest_p[k].reshape(1, pieces * n) for k in range(TOP_K)],
                      pieces * n_rows)
    ys = _experts(blk_e, end_blk, xs.reshape(pieces, n_rows, SC_SUBROW), e_wg, e_bg.reshape(N_EXPERTS, 1, -1),
                  e_wu, e_bu.reshape(N_EXPERTS, 1, -1), e_wd, e_bd.reshape(N_EXPERTS, 1, -1))
    y4 = _sc_gather(ys.reshape(pieces * n_rows, SC_SUBROW), dest_p.reshape(1, nk * pieces))
    return _combine(x1, y4.reshape(TOP_K, pieces, n, SC_SUBROW), gates).reshape(b, t, d)


def kernel(x, attn_norm_g, w_in, nsa_q_norm_g, nsa_kc_norm_g, nsa_ks_norm_g, nsa_kw_norm_g, cmp_k_pos, cmp_k_w1, cmp_k_b1, cmp_k_w2, cmp_k_b2, cmp_v_pos, cmp_v_w1, cmp_v_b1, cmp_v_w2, cmp_v_b2, nsa_out_norm_g, gdn_conv_w, gdn_a_log, gdn_dt_bias, gdn_out_norm_g, w_out, ffn_norm_g, router_w, router_b, exp_w_gate, exp_b_gate, exp_w_up, exp_b_up, exp_w_down, exp_b_down):
    params = (attn_norm_g, w_in, nsa_q_norm_g, nsa_kc_norm_g, nsa_ks_norm_g, nsa_kw_norm_g,
              cmp_k_pos, cmp_k_w1, cmp_k_b1, cmp_k_w2, cmp_k_b2, cmp_v_pos, cmp_v_w1, cmp_v_b1, cmp_v_w2, cmp_v_b2,
              nsa_out_norm_g, gdn_conv_w, gdn_a_log, gdn_dt_bias, gdn_out_norm_g, w_out, ffn_norm_g,
              router_w, router_b, exp_w_gate, exp_b_gate, exp_w_up, exp_b_up, exp_w_down, exp_b_down)
    for l in range(attn_norm_g.shape[0]):
        x = _layer(x, *(p[l] for p in params))
    return x
```

```python
import functools

import jax
import jax.numpy as jnp
import numpy as np
from jax import lax
from jax.experimental import pallas as pl
from jax.experimental.pallas import tpu as pltpu
from jax.experimental.pallas import tpu_sc as plsc

F32 = jnp.float32
BF16 = jnp.bfloat16

EPS = 1e-6
NEG = -1e30
MASKED = -2.0 ** 100

NSA_HEADS = 8
NSA_GROUPS = 2
NSA_HPG = 4
NSA_DH = 64
CMP_BLOCK = 32
CMP_STRIDE = 16
CMP_HIDDEN = 256
SLC_BLOCK = 64
SLC_TOPK = 16
WINDOW = 512
NSA_Q = 256
GDN_HEADS = 4
GDN_DH = 128
GDN_CHUNK = 64
N_EXPERTS = 32
TOP_K = 4
SWIGLU_LIMIT = 7.0
SWIGLU_ALPHA = 1.702
MOE_ROW_BLOCK = 256

LANES = 128
GATE_ROWS = 16
FLASH_BODY_TILES = (8, 4)
NSA_W = NSA_HEADS * NSA_DH
GDN_W = GDN_HEADS * GDN_DH

_NT = (((1,), (1,)), ((), ()))
_TN = (((0,), (0,)), ((), ()))


def _cparams(sem, vmem_mb):
    return pltpu.CompilerParams(dimension_semantics=sem, vmem_limit_bytes=vmem_mb * 1024 * 1024)


def _dot(a, b):
    return jnp.dot(a, b, preferred_element_type=F32)


def _dot_nt(a, b):
    return lax.dot_general(a, b, _NT, preferred_element_type=F32)


def _dot_tn(a, b):
    return lax.dot_general(a, b, _TN, preferred_element_type=F32)


def _pack_piece(block):
    words = block.shape[1] // 2
    hi = lax.bitcast_convert_type(block[:, :words].astype(BF16).astype(F32), jnp.uint32)
    lo = lax.bitcast_convert_type(block[:, words:].astype(BF16).astype(F32), jnp.uint32)
    return lax.bitcast_convert_type(hi | (lo >> 16), jnp.int32)


def _store_pieces(ref, val):
    cols = 2 * ref.shape[2]
    for j in range(ref.shape[0]):
        ref[j] = _pack_piece(val[:, j * cols:(j + 1) * cols])


def _join_pieces(ref):
    out = []
    for j in range(ref.shape[0]):
        words = lax.bitcast_convert_type(ref[j], jnp.uint32)
        out.append(lax.bitcast_convert_type(words & jnp.uint32(0xFFFF0000), F32))
        out.append(lax.bitcast_convert_type(words << 16, F32))
    return jnp.concatenate(out, axis=1)


def _inproj_body(x_ref, g_ref, wqt_ref, wkv_ref, wvt_ref, wgt_ref, wqkv_ref, wz_ref, wab_ref, qg_ref, kg_ref,
                 vone_ref, cw_ref, oqt_ref, okn_ref, ocf_ref, ovt_ref, ogt_ref, oqkv_ref, oz_ref, oab_ref, ybuf, cbuf,
                 *, tiles_per_seq):
    x = x_ref[...]
    h = (x * lax.rsqrt(jnp.mean(x * x, axis=-1, keepdims=True) + EPS) * g_ref[...]).astype(BF16)
    tm = x.shape[0]

    yq = _dot_nt(wqt_ref[...], h)
    for s in range(NSA_HEADS):
        sl = slice(s * NSA_DH, (s + 1) * NSA_DH)
        ys = yq[sl, :]
        ms = jnp.sum(ys * ys, axis=0, keepdims=True) * (1.0 / NSA_DH)
        oqt_ref[sl, :] = (ys * lax.rsqrt(ms + EPS) * qg_ref[sl, :]).astype(BF16)

    ykv = _dot(h, wkv_ref[...])
    lane = lax.broadcasted_iota(jnp.int32, (tm, LANES), 1)
    low = lane < NSA_DH
    for s in range(2):
        sl = slice(s * LANES, (s + 1) * LANES)
        ys = ykv[:, (2 + s) * LANES:(3 + s) * LANES]
        y2 = ys * ys
        s0 = jnp.sum(jnp.where(low, y2, 0.0), axis=-1, keepdims=True)
        s1 = jnp.sum(jnp.where(low, 0.0, y2), axis=-1, keepdims=True)
        ms = jnp.where(low, s0, s1) * (1.0 / NSA_DH)
        okn_ref[:, sl] = (ys * lax.rsqrt(ms + EPS) * kg_ref[:, sl]).astype(BF16)

    chunks = tm // CMP_STRIDE
    for br in range(2):
        cbuf[br] = ykv[:, br * LANES:(br + 1) * LANES]
        taken = [cbuf[br, pl.ds(l, chunks, stride=CMP_STRIDE), :] for l in range(CMP_STRIDE)]
        for grp in range(NSA_GROUPS):
            flat = jnp.concatenate([r[:, grp * NSA_DH:(grp + 1) * NSA_DH] for r in taken], axis=1)
            ocf_ref[0, br, grp] = flat.astype(BF16)

    ovt_ref[...] = (_dot_nt(wvt_ref[...], h) + vone_ref[...]).astype(BF16)
    ogt_ref[...] = _dot_nt(wgt_ref[...], h)
    oz_ref[...] = _dot(h, wz_ref[...]).astype(BF16)
    oab_ref[...] = _dot(h, wab_ref[...])

    halo = ybuf.shape[0] - tm
    first = pl.program_id(0) % tiles_per_seq == 0

    @pl.when(first)
    def _():
        ybuf[0:halo, :] = jnp.zeros((halo, ybuf.shape[1]), F32)

    @pl.when(jnp.logical_not(first))
    def _():
        ybuf[0:halo, :] = ybuf[tm:tm + halo, :]

    ybuf[halo:halo + tm, :] = _dot(h, wqkv_ref[...])
    taps = cw_ref.shape[0]
    y = cw_ref[0:1, :] * ybuf[pl.ds(halo - taps + 1, tm), :]
    for k in range(1, taps):
        y = y + cw_ref[k:k + 1, :] * ybuf[pl.ds(halo - taps + 1 + k, tm), :]
    hy = 0.5 * y
    y = hy + hy * jnp.tanh(hy)
    for s in range(3 * GDN_HEADS):
        sl = slice(s * GDN_DH, (s + 1) * GDN_DH)
        ys = y[:, sl]
        if s < 2 * GDN_HEADS:
            scale = GDN_DH ** -0.5 if s < GDN_HEADS else 1.0
            ys = ys * (lax.rsqrt(jnp.sum(ys * ys, axis=-1, keepdims=True) + EPS) * scale)
        oqkv_ref[:, sl] = ys.astype(BF16)


def _in_proj(x2, g, wqt, wkv, wvt, wgt, wqkv, wz, wab, qg, kg, vone, conv_w, tiles_per_seq, tm):
    n, d = x2.shape
    chunks, flat = tm // CMP_STRIDE, CMP_STRIDE * NSA_DH
    full = lambda a: pl.BlockSpec(a.shape, lambda i: (0,) * a.ndim)
    row = lambda w: pl.BlockSpec((tm, w), lambda i: (i, 0))
    colb = lambda r: pl.BlockSpec((r, tm), lambda i: (0, i))
    return pl.pallas_call(
        functools.partial(_inproj_body, tiles_per_seq=tiles_per_seq),
        grid=(n // tm,),
        in_specs=[row(d)] + [full(a) for a in (g, wqt, wkv, wvt, wgt, wqkv, wz, wab, qg, kg, vone, conv_w)],
        out_specs=[colb(wqt.shape[0]), row(2 * LANES),
                   pl.BlockSpec((1, 2, NSA_GROUPS, chunks, flat), lambda i: (i // tiles_per_seq, 0, 0, i % tiles_per_seq, 0)),
                   colb(wvt.shape[0]), colb(wgt.shape[0]), row(wqkv.shape[1]), row(wz.shape[1]), row(wab.shape[1])],
        out_shape=[jax.ShapeDtypeStruct((wqt.shape[0], n), BF16), jax.ShapeDtypeStruct((n, 2 * LANES), BF16),
                   jax.ShapeDtypeStruct((n // (tm * tiles_per_seq), 2, NSA_GROUPS, chunks * tiles_per_seq, flat), BF16),
                   jax.ShapeDtypeStruct((wvt.shape[0], n), BF16), jax.ShapeDtypeStruct((wgt.shape[0], n), F32),
                   jax.ShapeDtypeStruct((n, wqkv.shape[1]), BF16), jax.ShapeDtypeStruct((n, wz.shape[1]), BF16),
                   jax.ShapeDtypeStruct((n, wab.shape[1]), F32)],
        scratch_shapes=[pltpu.VMEM((tm + 8, wqkv.shape[1]), F32), pltpu.VMEM((2, tm, LANES), F32)],
        compiler_params=_cparams(("arbitrary",), 56),
        name="in_proj",
    )(x2, g, wqt, wkv, wvt, wgt, wqkv, wz, wab, qg, kg, vone, conv_w)


def _compress_body(x_ref, pos_ref, w1_ref, b1_ref, w2_ref, b2_ref, w2t_ref, b2t_ref, g_ref, ok_ref, ovt_ref,
                   *, n_cmp):
    is_key = pl.program_id(1) == 0
    nch = x_ref.shape[3]
    hids = []
    for grp in range(NSA_GROUPS):
        x = x_ref[0, 0, grp].astype(F32)
        xa = (x + pos_ref[0, 0]).astype(BF16)
        xb = (x + pos_ref[0, 1]).astype(BF16)
        a = _dot(xa, w1_ref[0, 0])
        b = _dot(xb, w1_ref[0, 1])
        b_next = pltpu.roll(b, nch - 1, 0)
        hids.append(jax.nn.gelu(a + b_next + b1_ref[0]).astype(BF16))

    @pl.when(is_key)
    def _():
        row = lax.broadcasted_iota(jnp.int32, (nch, NSA_DH), 0)
        outs = []
        for grp in range(NSA_GROUPS):
            out = _dot(hids[grp], w2_ref[0]) + b2_ref[0]
            out = out * lax.rsqrt(jnp.mean(out * out, axis=-1, keepdims=True) + EPS) * g_ref[...]
            outs.append(jnp.where(row < n_cmp, out, 0.0))
        ok_ref[0] = jnp.concatenate(outs, axis=-1).astype(BF16)

    @pl.when(jnp.logical_not(is_key))
    def _():
        col = lax.broadcasted_iota(jnp.int32, (NSA_DH, nch), 1)
        outs = []
        for grp in range(NSA_GROUPS):
            out = _dot_nt(w2t_ref[0], hids[grp]) + b2t_ref[0]
            outs.append(jnp.where(col < n_cmp, out, 0.0))
        ovt_ref[0] = jnp.concatenate(outs, axis=0).astype(BF16)


def _compress(xflat, pos, w1, b1, w2, b2, w2t, b2t, kc_g, n_cmp):
    b, _, _, nch, flat = xflat.shape
    return pl.pallas_call(
        functools.partial(_compress_body, n_cmp=n_cmp),
        grid=(b, 2),
        in_specs=[
            pl.BlockSpec((1, 1, NSA_GROUPS, nch, flat), lambda i, j: (i, j, 0, 0, 0)),
            pl.BlockSpec((1, 2, 1, flat), lambda i, j: (j, 0, 0, 0)),
            pl.BlockSpec((1, 2, flat, CMP_HIDDEN), lambda i, j: (j, 0, 0, 0)),
            pl.BlockSpec((1, 1, CMP_HIDDEN), lambda i, j: (j, 0, 0)),
            pl.BlockSpec((1, CMP_HIDDEN, NSA_DH), lambda i, j: (j, 0, 0)),
            pl.BlockSpec((1, 1, NSA_DH), lambda i, j: (j, 0, 0)),
            pl.BlockSpec((1, NSA_DH, CMP_HIDDEN), lambda i, j: (j, 0, 0)),
            pl.BlockSpec((1, NSA_DH, 1), lambda i, j: (j, 0, 0)),
            pl.BlockSpec((1, NSA_DH), lambda i, j: (0, 0)),
        ],
        out_specs=[pl.BlockSpec((1, nch, LANES), lambda i, j: (i, 0, 0)),
                   pl.BlockSpec((1, LANES, nch), lambda i, j: (i, 0, 0))],
        out_shape=[jax.ShapeDtypeStruct((b, nch, LANES), BF16), jax.ShapeDtypeStruct((b, LANES, nch), BF16)],
        compiler_params=_cparams(("parallel", "arbitrary"), 32),
        name="nsa_compress",
    )(xflat, pos, w1, b1, w2, b2, w2t, b2t, kc_g)


def _tile_heads(a):
    return jnp.concatenate([a] * NSA_HPG, axis=1)


def _nsa_body(qt_ref, gt_ref, kc_ref, vct_ref, ks_ref, kw_ref, vst_ref, vwt_ref, et_ref, ov_ref, cpat_ref, wpat_ref,
              dpat_ref, o_ref, acc_sc, s_sc, *, n_top, kt):
    grp = pl.program_id(1)
    s0 = pl.program_id(2) * NSA_Q
    nch = kc_ref.shape[1]
    nblk = ov_ref.shape[0]

    qh = jnp.concatenate([qt_ref[h * NSA_DH:(h + 1) * NSA_DH, :] for h in range(NSA_HPG)], axis=1)
    zq = jnp.zeros_like(qh)
    qt = jnp.where(grp == 0, jnp.concatenate([qh, zq], axis=0), jnp.concatenate([zq, qh], axis=0))
    t_row = s0 + lax.broadcasted_iota(jnp.int32, (1, NSA_Q), 1)

    cbias = cpat_ref[pl.ds(pl.multiple_of(nch - s0 // CMP_STRIDE, CMP_STRIDE), nch), :]
    sc = _dot(kc_ref[0], qt) + _tile_heads(cbias)
    pc = jnp.exp2(sc - jnp.max(sc, axis=0, keepdims=True)).astype(BF16)
    stacked = jnp.concatenate([vct_ref[0], ov_ref[...], jnp.ones((8, nch), BF16)], axis=0)
    res = _dot(stacked, pc)
    inv = jnp.where(_tile_heads(t_row >= CMP_BLOCK - 1), 1.0 / jnp.maximum(res[LANES + nblk:LANES + nblk + 1], 1e-30),
                    0.0)
    oc = res[:LANES] * inv
    imp4 = res[LANES:LANES + nblk] * inv
    imp = (imp4[:, 0:NSA_Q] + imp4[:, NSA_Q:2 * NSA_Q] + imp4[:, 2 * NSA_Q:3 * NSA_Q]
           + imp4[:, 3 * NSA_Q:4 * NSA_Q])
    blk = lax.broadcasted_iota(jnp.int32, (nblk, NSA_Q), 0)
    cur = t_row // SLC_BLOCK
    imp = jnp.where(blk * SLC_BLOCK > t_row, NEG, imp)
    imp = jnp.where((blk == 0) | (blk == cur) | (blk == cur - 1), -jnp.inf, imp)
    rounds_left = n_top - 3

    def pick_rounds(v, rounds):
        for _ in range(rounds):
            mx = jnp.max(v, axis=0, keepdims=True)
            first = jnp.min(jnp.where(v == mx, blk, nblk), axis=0, keepdims=True)
            v = jnp.where(blk == first, -jnp.inf, v)
        return v

    quarter = rounds_left // 4
    picked = pick_rounds(imp, quarter)

    wlen = WINDOW + NSA_Q
    w0 = pl.multiple_of(jnp.maximum(s0 - WINDOW, 0), NSA_Q)
    wbias = wpat_ref[pl.ds(pl.multiple_of(w0 - s0 + WINDOW, NSA_Q), wlen), :]
    sw = _dot(kw_ref[0, pl.ds(w0, wlen), :], qt) + _tile_heads(wbias)
    picked = pick_rounds(picked, quarter)
    pw = jnp.exp2(sw - jnp.max(sw, axis=0, keepdims=True)).astype(BF16)
    picked = pick_rounds(picked, quarter)
    ow = _dot(vwt_ref[:, pl.ds(w0, wlen)], pw)
    ow = ow[:NSA_DH] / ow[NSA_DH:NSA_DH + 1]

    picked = pick_rounds(picked, rounds_left - 3 * quarter)
    chosen = picked == -jnp.inf

    selb = jnp.where(chosen & (blk * SLC_BLOCK < s0), 0.0, MASKED).astype(BF16)
    rhs = jnp.concatenate([qt, _tile_heads(selb)], axis=0)

    d0 = pl.multiple_of(s0, NSA_Q)
    selb_d = jnp.where(chosen & (blk <= cur), 0.0, MASKED).astype(BF16)
    sd = (_dot(jnp.concatenate([ks_ref[0, pl.ds(d0, NSA_Q), :], et_ref[pl.ds(d0, NSA_Q), :]], axis=1),
               jnp.concatenate([qt, _tile_heads(selb_d)], axis=0))
          + _tile_heads(dpat_ref[...]))
    m_diag = jnp.max(sd, axis=0, keepdims=True)
    acc_sc[0] = _dot(vst_ref[:, pl.ds(d0, NSA_Q)], jnp.exp2(sd - m_diag).astype(BF16))
    acc_sc[1] = jnp.zeros(acc_sc.shape[1:], F32)

    last_tile = ks_ref.shape[1] // kt - 1

    def scores(idx, slot):
        k0 = pl.multiple_of(jnp.minimum(idx, last_tile) * kt, kt)
        lhs = jnp.concatenate([ks_ref[0, pl.ds(k0, kt), :], et_ref[pl.ds(k0, kt), :]], axis=1)
        s_sc[slot] = _dot(lhs, rhs)

    def update(idx, slot, m_old, acc_ref):
        k0 = pl.multiple_of(idx * kt, kt)
        m_new = jnp.maximum(m_old, jnp.max(s_sc[slot], axis=0, keepdims=True))
        p = jnp.exp2(s_sc[slot] - m_new).astype(BF16)
        acc_ref[...] = jnp.exp2(m_old - m_new) * acc_ref[...] + _dot(vst_ref[:, pl.ds(k0, kt)], p)
        return m_new

    def tile_group(first, carry, count):
        ms = list(carry)
        for t in range(count):
            scores(first + t + 2, (t + 2) % 4)
            ms[t % 2] = update(first + t, t % 4, ms[t % 2], acc_sc.at[t % 2])
        return tuple(ms)

    n_tiles = (s0 + kt - 1) // kt
    scores(0, 0)
    scores(1, 1)
    carry = (m_diag, jnp.full((1, NSA_HPG * NSA_Q), NEG, F32))
    done = 0
    for size in FLASH_BODY_TILES:
        left = n_tiles - done
        groups = (left + size - 1) // size if size == FLASH_BODY_TILES[-1] else left // size
        carry = lax.fori_loop(0, groups, lambda j, c, done=done, size=size: tile_group(done + size * j, c, size),
                              carry)
        done = done + size * groups
    m0, m1 = carry
    m_fin = jnp.maximum(m0, m1)
    acc = acc_sc[0] * jnp.exp2(m0 - m_fin) + acc_sc[1] * jnp.exp2(m1 - m_fin)
    osl = acc[:NSA_DH] / acc[NSA_DH:NSA_DH + 1]

    oc = jnp.where(grp == 0, oc[:NSA_DH], oc[NSA_DH:])
    gts = jax.nn.sigmoid(gt_ref[...])
    for h in range(NSA_HPG):
        cols = slice(h * NSA_Q, (h + 1) * NSA_Q)
        o_ref[h * NSA_DH:(h + 1) * NSA_DH, :] = (
            gts[3 * h:3 * h + 1, :] * oc[:, cols] + gts[3 * h + 1:3 * h + 2, :] * osl[:, cols]
            + gts[3 * h + 2:3 * h + 3, :] * ow[:, cols])


def _mask_pattern(valid):
    return jnp.asarray(np.where(valid, 0.0, NEG), F32)


def _nsa_attention(qt, gt, kc, vct, okv, vt, expand_t, overlap, b, t, n_top, kt):
    nch = kc.shape[1]
    nq = t // NSA_Q
    n = b * t
    ql = np.arange(NSA_Q)[None, :]
    rc = np.arange(2 * nch)[:, None] - nch
    cpat = _mask_pattern(rc * CMP_STRIDE + CMP_BLOCK - 1 <= ql)
    rw = np.arange(2 * WINDOW + NSA_Q)[:, None] - WINDOW
    wpat = _mask_pattern((rw <= ql) & (rw > ql - WINDOW))
    dpat = _mask_pattern(np.arange(NSA_Q)[:, None] <= ql)
    full2 = lambda a: pl.BlockSpec(a.shape, lambda bi, g, i: (0, 0))
    return pl.pallas_call(
        functools.partial(_nsa_body, n_top=n_top, kt=kt),
        grid=(b, NSA_GROUPS, nq),
        in_specs=[
            pl.BlockSpec((NSA_HPG * NSA_DH, NSA_Q), lambda bi, g, i: (g, bi * nq + i)),
            pl.BlockSpec((GATE_ROWS, NSA_Q), lambda bi, g, i: (g, bi * nq + i)),
            pl.BlockSpec((1, nch, LANES), lambda bi, g, i: (bi, 0, 0)),
            pl.BlockSpec((1, LANES, nch), lambda bi, g, i: (bi, 0, 0)),
            pl.BlockSpec((1, t, LANES), lambda bi, g, i: (bi, 0, 0)),
            pl.BlockSpec((1, t, LANES), lambda bi, g, i: (bi, 0, 1)),
            pl.BlockSpec((LANES, t), lambda bi, g, i: (g, bi)),
            pl.BlockSpec((LANES, t), lambda bi, g, i: (NSA_GROUPS + g, bi)),
            full2(expand_t), full2(overlap), full2(cpat), full2(wpat), full2(dpat),
        ],
        out_specs=pl.BlockSpec((NSA_HPG * NSA_DH, NSA_Q), lambda bi, g, i: (g, bi * nq + i)),
        out_shape=jax.ShapeDtypeStruct((NSA_W, n), F32),
        scratch_shapes=[pltpu.VMEM((2, LANES, NSA_HPG * NSA_Q), F32),
                        pltpu.VMEM((4, kt, NSA_HPG * NSA_Q), F32)],
        compiler_params=_cparams(("parallel", "parallel", "arbitrary"), 56),
        name="nsa_attention",
    )(qt, gt, kc, vct, okv, okv, vt, vt, expand_t, overlap, cpat, wpat, dpat)


def _split_bf16(a):
    hi = a.astype(BF16)
    return hi, (a - hi.astype(F32)).astype(BF16)


def _unit_lower_inverses(lmats):
    c = lmats[0].shape[0]
    r = lax.broadcasted_iota(jnp.int32, (c, c), 0)
    col = lax.broadcasted_iota(jnp.int32, (c, c), 1)
    eye = jnp.where(r == col, 1.0, 0.0)
    xs = [eye - l for l in lmats]
    ps = []
    for l in lmats:
        l16 = l.astype(BF16)
        ps.append(_dot(l16, l16))
    steps = int(np.log2(c)) - 1
    for s in range(steps):
        last = s + 1 == steps
        for i in range(len(lmats)):
            rhs = ps[i].astype(BF16)
            if last:
                xs[i] = xs[i] + _dot(xs[i].astype(BF16), rhs)
            else:
                both = _dot(jnp.concatenate([xs[i], ps[i]], axis=0).astype(BF16), rhs)
                xs[i] = xs[i] + both[:c]
                ps[i] = both[c:]
    return xs


def _gdn_body(x_ref, z_ref, ab_ref, alog_ref, dtb_ref, og_ref, o_ref, s_sc, *, ct):
    nb = x_ref.shape[0]

    @pl.when(pl.program_id(0) == 0)
    def _():
        s_sc[...] = jnp.zeros(s_sc.shape, F32)

    ch = GDN_CHUNK
    r = lax.broadcasted_iota(jnp.int32, (ch, ch), 0)
    col = lax.broadcasted_iota(jnp.int32, (ch, ch), 1)
    incl = r >= col
    strict = r > col
    tril16 = jnp.concatenate([jnp.where(incl, 1.0, 0.0).astype(BF16)] * 3, axis=1)

    units = []
    for ci in range(ct // ch):
        rows = slice(ci * ch, (ci + 1) * ch)
        for bi in range(nb):
            ab = ab_ref[bi, rows, :]
            g_all = -jnp.exp(alog_ref[...]) * jax.nn.softplus(ab + dtb_ref[...])
            beta_all = jax.nn.sigmoid(ab)
            g_hi, g_lo = _split_bf16(g_all)
            g_lo2 = (g_all - g_hi.astype(F32) - g_lo.astype(F32)).astype(BF16)
            gc_all = _dot(tril16, jnp.concatenate([g_hi, g_lo, g_lo2], axis=0))
            gc_t = gc_all.T
            for h in range(GDN_HEADS):
                hs = slice(h * GDN_DH, (h + 1) * GDN_DH)
                q16 = x_ref[bi, rows, hs]
                k16 = x_ref[bi, rows, GDN_W + h * GDN_DH:GDN_W + (h + 1) * GDN_DH]
                qh, kh = q16.astype(F32), k16.astype(F32)
                vh = x_ref[bi, rows, 2 * GDN_W + h * GDN_DH:2 * GDN_W + (h + 1) * GDN_DH].astype(F32)
                gc = gc_all[:, h:h + 1]
                gr = gc_t[h:h + 1, :]
                g_last = gc_all[ch - 1:ch, h:h + 1]
                beta = beta_all[:, GDN_HEADS + h:GDN_HEADS + h + 1]
                eg = jnp.exp(gc)
                decay = jnp.where(incl, jnp.exp(jnp.minimum(gc - gr, 0.0)), 0.0)
                kb = kh * beta
                with_k = (_dot_nt(jnp.concatenate([kb.astype(BF16), q16], axis=0), k16)
                          * jnp.concatenate([decay, decay], axis=0))
                units.append(dict(
                    rows=rows, bi=bi, h=h,
                    lmat=jnp.where(strict, with_k[:ch], 0.0),
                    vb_kbg=jnp.concatenate([(vh * beta).astype(BF16), (kb * eg).astype(BF16)], axis=1),
                    qk=jnp.where(incl, with_k[ch:], 0.0).astype(BF16),
                    qg=(qh * eg).astype(BF16), kd_t=(kh * jnp.exp(g_last - gc)).T.astype(BF16),
                    gl=jnp.exp(g_last)))
    tinvs = _unit_lower_inverses([u["lmat"] for u in units])
    for u, tinv in zip(units, tinvs):
        u_w = _dot(tinv.astype(BF16), u["vb_kbg"])
        u["u"] = u_w[:, :GDN_DH]
        u["w_qg"] = jnp.concatenate([u_w[:, GDN_DH:].astype(BF16), u["qg"]], axis=0)
        u["kd_qk"] = jnp.concatenate([u["kd_t"], u["qk"]], axis=0)

    for u in units:
        bi, h, rows = u["bi"], u["h"], u["rows"]
        hs = slice(h * GDN_DH, (h + 1) * GDN_DH)
        s_old = s_sc[bi * GDN_HEADS + h]
        from_state = _dot(u["w_qg"], s_old.astype(BF16))
        v_new = (u["u"] - from_state[:ch]).astype(BF16)
        from_v = _dot(u["kd_qk"], v_new)
        s_sc[bi * GDN_HEADS + h] = s_old * u["gl"] + from_v[:GDN_DH]
        o = from_state[ch:] + from_v[GDN_DH:]
        on = o * lax.rsqrt(jnp.mean(o * o, axis=-1, keepdims=True) + EPS) * og_ref[...]
        zh = z_ref[bi, rows, hs].astype(F32)
        o_ref[bi, rows, hs] = (on * (zh * jax.nn.sigmoid(zh))).astype(BF16)


def _gdn(oqkv, oz, oab, alog, dtb, og, ct=128):
    b, t, w3 = oqkv.shape
    full = lambda a: pl.BlockSpec(a.shape, lambda c: (0,) * a.ndim)
    return pl.pallas_call(
        functools.partial(_gdn_body, ct=ct),
        grid=(t // ct,),
        in_specs=[
            pl.BlockSpec((b, ct, w3), lambda c: (0, c, 0)),
            pl.BlockSpec((b, ct, GDN_W), lambda c: (0, c, 0)),
            pl.BlockSpec((b, ct, LANES), lambda c: (0, c, 0)),
            full(alog), full(dtb), full(og),
        ],
        out_specs=pl.BlockSpec((b, ct, GDN_W), lambda c: (0, c, 0)),
        out_shape=jax.ShapeDtypeStruct((b, t, GDN_W), BF16),
        scratch_shapes=[pltpu.VMEM((b * GDN_HEADS, GDN_DH, GDN_DH), F32)],
        compiler_params=_cparams(("arbitrary",), 32),
        name="gdn",
    )(oqkv, oz, oab, alog, dtb, og)


def _outproj_body(ont_ref, og_ref, x_ref, ng_ref, wo_ref, fg_ref, wr_ref, br_ref, upper_ref,
                  x1_ref, h2_ref, gate_ref, route_ref, cnt_ref, cnt_sc):
    i = pl.program_id(0)
    tm = x_ref.shape[0]

    @pl.when(i == 0)
    def _():
        cnt_sc[...] = jnp.zeros(cnt_sc.shape, F32)

    a = ont_ref[...]
    a = (a * lax.rsqrt(jnp.mean(a * a, axis=0, keepdims=True) + EPS) * ng_ref[...]).astype(BF16)
    x1 = x_ref[...] + _dot_tn(a, wo_ref[0:NSA_W, :]) + _dot(og_ref[...], wo_ref[NSA_W:, :])
    x1_ref[...] = x1
    h2f = x1 * lax.rsqrt(jnp.mean(x1 * x1, axis=-1, keepdims=True) + EPS) * fg_ref[...]
    _store_pieces(h2_ref, h2f)
    h2 = h2f.astype(BF16)

    logits = (_dot(h2, wr_ref[...]) + br_ref[...]).T[:N_EXPERTS]
    erow = lax.broadcasted_iota(jnp.int32, (N_EXPERTS, tm), 0)
    onehot = jnp.zeros((N_EXPERTS, tm), F32)
    firsts, vals = [], []
    v = logits
    for k in range(TOP_K):
        mx = jnp.max(v, axis=0, keepdims=True)
        first = jnp.min(jnp.where(v == mx, erow, N_EXPERTS), axis=0, keepdims=True)
        hit = erow == first
        v = jnp.where(hit, -jnp.inf, v)
        onehot = jnp.where(hit, 1.0, onehot)
        firsts.append(first)
        vals.append(mx)
    vals = [jnp.exp(m - vals[0]) for m in vals]
    inv = 1.0 / (vals[0] + vals[1] + vals[2] + vals[3])
    gates_t = jnp.concatenate([m * inv for m in vals] + [jnp.zeros((LANES - TOP_K, tm), F32)], axis=0)
    gate_ref[...] = gates_t.T

    excl = cnt_sc[...] + _dot(onehot.astype(BF16), upper_ref[...])
    for k in range(TOP_K):
        route_ref[k:k + 1, :] = firsts[k]
        rank = jnp.sum(jnp.where(erow == firsts[k], excl, 0.0), axis=0, keepdims=True)
        route_ref[TOP_K + k:TOP_K + k + 1, :] = rank.astype(jnp.int32)
    cnt_sc[...] = cnt_sc[...] + jnp.sum(onehot, axis=1, keepdims=True)
    cnt_ref[...] = cnt_sc[...].astype(jnp.int32)


def _out_proj(o_nsa_t, o_gdn, x2, ng, wo, fg, wr, br, tm=512):
    upper = jnp.asarray(np.arange(tm)[:, None] < np.arange(tm)[None, :], BF16)
    n, d = x2.shape
    full = lambda a: pl.BlockSpec(a.shape, lambda i: (0,) * a.ndim)
    row = lambda w: pl.BlockSpec((tm, w), lambda i: (i, 0))
    return pl.pallas_call(
        _outproj_body,
        grid=(n // tm,),
        in_specs=[pl.BlockSpec((NSA_W, tm), lambda i: (0, i)), row(GDN_W), row(d), full(ng), full(wo), full(fg),
                  full(wr), full(br), full(upper)],
        out_specs=[row(d), pl.BlockSpec((d // 2 // SC_SUBROW, tm, SC_SUBROW), lambda i: (0, i, 0)),
                   row(LANES), pl.BlockSpec((2 * TOP_K, tm), lambda i: (0, i)),
                   pl.BlockSpec((N_EXPERTS, 1), lambda i: (0, 0))],
        out_shape=[jax.ShapeDtypeStruct((n, d), F32),
                   jax.ShapeDtypeStruct((d // 2 // SC_SUBROW, n, SC_SUBROW), jnp.int32),
                   jax.ShapeDtypeStruct((n, LANES), F32), jax.ShapeDtypeStruct((2 * TOP_K, n), jnp.int32),
                   jax.ShapeDtypeStruct((N_EXPERTS, 1), jnp.int32)],
        scratch_shapes=[pltpu.VMEM((N_EXPERTS, 1), F32)],
        compiler_params=_cparams(("arbitrary",), 48),
        name="out_proj_router",
    )(o_nsa_t, o_gdn, x2, ng, wo, fg, wr, br, upper)


def _dest_body(ps_ref, route_ref, o_ref, *, n_rows, pieces):
    expert = route_ref[0:TOP_K, :]
    start = jnp.zeros(expert.shape, jnp.int32)
    for e in range(N_EXPERTS):
        start = jnp.where(expert == e, ps_ref[e], start)
    dest = start + route_ref[TOP_K:2 * TOP_K, :]
    for k in range(TOP_K):
        for j in range(pieces):
            o_ref[k * pieces + j:k * pieces + j + 1, :] = dest[k:k + 1, :] + j * n_rows


def _dest_rows(pstarts, route, n_rows, pieces):
    n = route.shape[1]
    tn = min(2048, n)
    grid_spec = pltpu.PrefetchScalarGridSpec(
        num_scalar_prefetch=1,
        grid=(n // tn,),
        in_specs=[pl.BlockSpec((2 * TOP_K, tn), lambda i, ps: (0, i))],
        out_specs=pl.BlockSpec((TOP_K * pieces, tn), lambda i, ps: (0, i)),
    )
    return pl.pallas_call(
        functools.partial(_dest_body, n_rows=n_rows, pieces=pieces),
        grid_spec=grid_spec,
        out_shape=jax.ShapeDtypeStruct((TOP_K * pieces, n), jnp.int32),
        name="moe_dest_rows",
    )(pstarts, route)


def _expert_body(be_ref, end_ref, xs_ref, wg_hbm, bg_ref, wu_hbm, bu_ref, wd_hbm, bd_ref,
                 y_ref, wf32, w16, sems, slot_sc):
    i = pl.program_id(0)
    n_used = be_ref[pl.num_programs(0)]
    used = i < n_used
    expert = be_ref[i]
    fresh = used & ((i == 0) | (expert != be_ref[jnp.maximum(i - 1, 0)]))
    hbm = (wg_hbm, wu_hbm, wd_hbm)

    def weight_copy(e, slot, j):
        return pltpu.make_async_copy(hbm[j].at[e], wf32.at[slot, j], sems.at[slot, j])

    @pl.when(i == 0)
    def _():
        slot_sc[0] = 0

    @pl.when((i == 0) & used)
    def _():
        for j in range(3):
            weight_copy(expert, 0, j).start()

    @pl.when(fresh)
    def _():
        slot = slot_sc[0]
        for j in range(3):
            weight_copy(expert, slot, j).wait()
            w16[j] = wf32[slot, j].astype(BF16)
        following = end_ref[expert]

        @pl.when(following < n_used)
        def _():
            for j in range(3):
                weight_copy(be_ref[following], 1 - slot, j).start()

        slot_sc[0] = 1 - slot

    @pl.when(jnp.logical_not(used))
    def _():
        y_ref[...] = jnp.zeros(y_ref.shape, y_ref.dtype)

    @pl.when(used)
    def _():
        x = _join_pieces(xs_ref).astype(BF16)
        gate = jnp.minimum(_dot(x, w16[0]) + bg_ref[0], SWIGLU_LIMIT)
        up = jnp.clip(_dot(x, w16[1]) + bu_ref[0], -SWIGLU_LIMIT, SWIGLU_LIMIT)
        glu = gate * jax.nn.sigmoid(gate * SWIGLU_ALPHA)
        act = ((up + 1.0) * glu).astype(BF16)
        cols = 2 * y_ref.shape[2]
        for j in range(y_ref.shape[0]):
            sl = slice(j * cols, (j + 1) * cols)
            y_ref[j] = _pack_piece(_dot(act, w16[2, :, sl]) + bd_ref[0, :, sl])


def _experts(blk_e, end_blk, xs, wg, bg, wu, bu, wd, bd):
    pieces, n_rows, sub = xs.shape
    d, de = wg.shape[1], wg.shape[2]
    assert d == de
    r = MOE_ROW_BLOCK
    bspec = lambda w: pl.BlockSpec((1, 1, w), lambda i, be, *_: (be[i], 0, 0))
    hbm = pl.BlockSpec(memory_space=pl.ANY)
    grid_spec = pltpu.PrefetchScalarGridSpec(
        num_scalar_prefetch=2,
        grid=(n_rows // r,),
        in_specs=[pl.BlockSpec((pieces, r, sub), lambda i, *_: (0, i, 0)),
                  hbm, bspec(de), hbm, bspec(de), hbm, bspec(d)],
        out_specs=pl.BlockSpec((pieces, r, sub), lambda i, *_: (0, i, 0)),
        scratch_shapes=[pltpu.VMEM((2, 3, d, de), F32), pltpu.VMEM((3, d, de), BF16),
                        pltpu.SemaphoreType.DMA((2, 3)), pltpu.SMEM((1,), jnp.int32)],
    )
    return pl.pallas_call(
        _expert_body,
        grid_spec=grid_spec,
        out_shape=jax.ShapeDtypeStruct((pieces, n_rows, sub), jnp.int32),
        compiler_params=_cparams(("arbitrary",), 56),
        name="moe_experts",
    )(blk_e, end_blk, xs, wg, bg, wu, bu, wd, bd)


SC_WINDOW = 128
SC_SUBROW = 256


def _sc_mesh():
    return plsc.VectorSubcoreMesh(core_axis_name="c", subcore_axis_name="s")


def _sc_dispatch(h2, dest_rows, n_rows):
    n, d = h2.shape

    @functools.partial(pl.kernel, out_type=jax.ShapeDtypeStruct((n_rows, d), h2.dtype), mesh=_sc_mesh())
    def dispatch(x_hbm, *refs):
        idx_hbm, o_hbm = refs[:TOP_K], refs[TOP_K]

        def body(x_vmem, *idx_vmem):
            for iv in idx_vmem:
                pltpu.sync_copy(x_vmem, o_hbm.at[iv.at[0]])

        pltpu.emit_pipeline(
            body,
            grid=(n // SC_WINDOW,),
            in_specs=[pl.BlockSpec((SC_WINDOW, d), lambda i: (i, 0))]
                     + [pl.BlockSpec((1, SC_WINDOW), lambda i: (0, i))] * TOP_K,
            out_specs=[],
            core_axis_name=("c", "s"),
            dimension_semantics=(pltpu.PARALLEL,),
        )(x_hbm, *idx_hbm)

    return dispatch(h2, *dest_rows)


def _sc_gather(table, idx):
    _, d = table.shape
    m = idx.shape[1]

    @functools.partial(pl.kernel, out_type=jax.ShapeDtypeStruct((m, d), table.dtype), mesh=_sc_mesh())
    def gather(t_hbm, i_hbm, o_hbm):
        def body(i_vmem, o_vmem):
            pltpu.sync_copy(t_hbm.at[i_vmem.at[0]], o_vmem)

        pltpu.emit_pipeline(
            body,
            grid=(m // SC_WINDOW,),
            in_specs=[pl.BlockSpec((1, SC_WINDOW), lambda i: (0, i))],
            out_specs=[pl.BlockSpec((SC_WINDOW, d), lambda i: (i, 0))],
            core_axis_name=("c", "s"),
            dimension_semantics=(pltpu.PARALLEL,),
        )(i_hbm, o_hbm)

    return gather(table, idx)


def _combine_body(x1_ref, y_ref, gate_ref, o_ref):
    acc = x1_ref[...]
    for k in range(TOP_K):
        acc = acc + gate_ref[:, k:k + 1] * _join_pieces(y_ref.at[k])
    o_ref[...] = acc


def _combine(x1, y4, gates, tm=512):
    n, d = x1.shape
    pieces, sub = y4.shape[1], y4.shape[3]
    row = lambda w: pl.BlockSpec((tm, w), lambda i: (i, 0))
    return pl.pallas_call(
        _combine_body,
        grid=(n // tm,),
        in_specs=[row(d), pl.BlockSpec((TOP_K, pieces, tm, sub), lambda i: (0, 0, i, 0)), row(LANES)],
        out_specs=row(d),
        out_shape=jax.ShapeDtypeStruct((n, d), F32),
        compiler_params=_cparams(("parallel",), 48),
        name="moe_combine",
    )(x1, y4, gates)


def _pad_lanes(a, width=LANES):
    return jnp.pad(a, ((0, 0), (0, width - a.shape[1])))


def _layer(x, attn_norm_g, w_in, q_g, kc_g, ks_g, kw_g, ck_pos, ck_w1, ck_b1, ck_w2, ck_b2,
           cv_pos, cv_w1, cv_b1, cv_w2, cv_b2, nsa_out_g, conv_w, a_log, dt_bias, gdn_out_g, w_out,
           ffn_g, router_w, router_b, e_wg, e_bg, e_wu, e_bu, e_wd, e_bd):
    b, t, d = x.shape
    n = b * t
    x2 = x.reshape(n, d)

    o = np.cumsum([0, NSA_W] + [NSA_GROUPS * NSA_DH] * 6 + [3 * NSA_HEADS, 3 * GDN_W, GDN_W, GDN_HEADS, GDN_HEADS])
    wq_t = w_in[:, o[0]:o[1]].T.astype(BF16)
    qg_col = jnp.tile(q_g * (NSA_DH ** -0.5 * np.log2(np.e)), NSA_HEADS).reshape(NSA_W, 1)
    wkv = jnp.concatenate([w_in[:, o[1]:o[4]], w_in[:, o[5]:o[6]]], axis=1).astype(BF16)
    kg = jnp.concatenate([ks_g, ks_g, kw_g, kw_g]).reshape(1, 2 * LANES)
    wv_t = jnp.concatenate([w_in[:, o[4]:o[5]], w_in[:, o[6]:o[7]]], axis=1).T.reshape(2 * NSA_GROUPS, NSA_DH, d)
    wv_t = jnp.pad(wv_t, ((0, 0), (0, LANES - NSA_DH), (0, 0))).reshape(2 * NSA_GROUPS * LANES, d).astype(BF16)
    vone = jnp.asarray((np.arange(2 * NSA_GROUPS * LANES) % LANES == NSA_DH).astype(np.float32)[:, None])
    wg_t = w_in[:, o[7]:o[8]].T.reshape(NSA_GROUPS, NSA_HPG * 3, d)
    wg_t = jnp.pad(wg_t, ((0, 0), (0, GATE_ROWS - NSA_HPG * 3), (0, 0))).reshape(NSA_GROUPS * GATE_ROWS, d)
    wg_t = wg_t.astype(BF16)
    wab = _pad_lanes(w_in[:, o[10]:o[12]]).astype(BF16)
    wqkv = w_in[:, o[8]:o[9]].astype(BF16)
    wz = w_in[:, o[9]:o[10]].astype(BF16)

    tm = min(512, t)
    oqt, okn, xflat, ovt, ogt, oqkv, oz, oab = _in_proj(x2, attn_norm_g.reshape(1, d), wq_t, wkv, wv_t, wg_t, wqkv,
                                                        wz, wab, qg_col, kg, vone, conv_w, t // tm, tm)

    nch = t // CMP_STRIDE
    n_cmp = (t - CMP_BLOCK) // CMP_STRIDE + 1
    half = CMP_STRIDE * NSA_DH
    pos = jnp.stack([ck_pos, cv_pos]).reshape(2, 2, 1, half)
    w1 = jnp.stack([ck_w1, cv_w1]).reshape(2, 2, half, CMP_HIDDEN).astype(BF16)
    b1 = jnp.stack([ck_b1, cv_b1]).reshape(2, 1, CMP_HIDDEN)
    w2 = jnp.stack([ck_w2, cv_w2]).astype(BF16)
    b2 = jnp.stack([ck_b2, cv_b2]).reshape(2, 1, NSA_DH)
    w2t = jnp.stack([ck_w2.T, cv_w2.T]).astype(BF16)
    b2t = jnp.stack([ck_b2, cv_b2]).reshape(2, NSA_DH, 1)
    kc, vct = _compress(xflat, pos, w1, b1, w2, b2, w2t, b2t, kc_g.reshape(1, NSA_DH), n_cmp)

    n_slc = t // SLC_BLOCK
    n_top = min(SLC_TOPK, n_slc)
    assert n_top > 3
    nblk = max(n_slc, LANES)
    kt = min(256, t // 4)
    assert (t // kt) % 4 == 0
    ci = np.arange(nch)[None, :] * CMP_STRIDE
    sj = np.arange(nblk)[:, None] * SLC_BLOCK
    overlap = ((ci < sj + SLC_BLOCK) & (ci + CMP_BLOCK > sj) & (np.arange(nch)[None, :] < n_cmp)
               & (np.arange(nblk)[:, None] < n_slc))
    expand_t = (np.arange(t)[:, None] // SLC_BLOCK) == np.arange(nblk)[None, :]
    o_nsa_t = _nsa_attention(oqt, ogt, kc, vct, okn.reshape(b, t, -1), ovt, jnp.asarray(expand_t, BF16),
                             jnp.asarray(overlap, BF16), b, t, n_top, kt)

    alog_row = _pad_lanes(a_log.reshape(1, GDN_HEADS))
    dtb_row = _pad_lanes(dt_bias.reshape(1, GDN_HEADS))
    o_gdn = _gdn(oqkv.reshape(b, t, -1), oz.reshape(b, t, -1), oab.reshape(b, t, -1),
                 alog_row, dtb_row, gdn_out_g.reshape(1, GDN_DH))

    wr = _pad_lanes(router_w).astype(BF16)
    br = _pad_lanes(router_b.reshape(1, N_EXPERTS))
    x1, h2, gates, route, counts = _out_proj(
        o_nsa_t, o_gdn.reshape(n, GDN_W), x2, nsa_out_g.reshape(NSA_W, 1),
        w_out.astype(BF16), ffn_g.reshape(1, d), wr, br)

    r = MOE_ROW_BLOCK
    nk = n * TOP_K
    counts = counts[:, 0]
    pcounts = (counts + r - 1) // r * r
    pends = jnp.cumsum(pcounts)
    pstarts = pends - pcounts
    n_rows = (nk + r - 1) // r * r + N_EXPERTS * r
    n_blocks = n_rows // r
    blk_start = jnp.arange(n_blocks, dtype=jnp.int32)[:, None] * r
    blk_e = jnp.minimum(jnp.sum(pends[None, :] <= blk_start, axis=1), N_EXPERTS - 1).astype(jnp.int32)
    n_used = (pends[-1] // r).astype(jnp.int32)
    blk_e = jnp.concatenate([blk_e, n_used[None]])
    end_blk = (pends // r).astype(jnp.int32)
    pieces = d // 2 // SC_SUBROW
    dest_p = _dest_rows(pstarts.astype(jnp.int32), route, n_rows, pieces).reshape(TOP_K, pieces, n)
    xs = _sc_dispatch(h2.reshape(pieces * n, SC_SUBROW), [dest_p[k].reshape(1, pieces * n) for k in range(TOP_K)],
                      pieces * n_rows)
    ys = _experts(blk_e, end_blk, xs.reshape(pieces, n_rows, SC_SUBROW), e_wg, e_bg.reshape(N_EXPERTS, 1, -1),
                  e_wu, e_bu.reshape(N_EXPERTS, 1, -1), e_wd, e_bd.reshape(N_EXPERTS, 1, -1))
    y4 = _sc_gather(ys.reshape(pieces * n_rows, SC_SUBROW), dest_p.reshape(1, nk * pieces))
    return _combine(x1, y4.reshape(TOP_K, pieces, n, SC_SUBROW), gates).reshape(b, t, d)


def kernel(x, attn_norm_g, w_in, nsa_q_norm_g, nsa_kc_norm_g, nsa_ks_norm_g, nsa_kw_norm_g, cmp_k_pos, cmp_k_w1, cmp_k_b1, cmp_k_w2, cmp_k_b2, cmp_v_pos, cmp_v_w1, cmp_v_b1, cmp_v_w2, cmp_v_b2, nsa_out_norm_g, gdn_conv_w, gdn_a_log, gdn_dt_bias, gdn_out_norm_g, w_out, ffn_norm_g, router_w, router_b, exp_w_gate, exp_b_gate, exp_w_up, exp_b_up, exp_w_down, exp_b_down):
    params = (attn_norm_g, w_in, nsa_q_norm_g, nsa_kc_norm_g, nsa_ks_norm_g, nsa_kw_norm_g,
              cmp_k_pos, cmp_k_w1, cmp_k_b1, cmp_k_w2, cmp_k_b2, cmp_v_pos, cmp_v_w1, cmp_v_b1, cmp_v_w2, cmp_v_b2,
              nsa_out_norm_g, gdn_conv_w, gdn_a_log, gdn_dt_bias, gdn_out_norm_g, w_out, ffn_norm_g,
              router_w, router_b, exp_w_gate, exp_b_gate, exp_w_up, exp_b_up, exp_w_down, exp_b_down)
    for l in range(attn_norm_g.shape[0]):
        x = _layer(x, *(p[l] for p in params))
    return x
```

```python
import functools

import jax
import jax.numpy as jnp
import numpy as np
from jax import lax
from jax.experimental import pallas as pl
from jax.experimental.pallas import tpu as pltpu
from jax.experimental.pallas import tpu_sc as plsc

F32 = jnp.float32
BF16 = jnp.bfloat16

EPS = 1e-6
NEG = -1e30
MASKED = -2.0 ** 100

NSA_HEADS = 8
NSA_GROUPS = 2
NSA_HPG = 4
NSA_DH = 64
CMP_BLOCK = 32
CMP_STRIDE = 16
CMP_HIDDEN = 256
SLC_BLOCK = 64
SLC_TOPK = 16
WINDOW = 512
NSA_Q = 256
GDN_HEADS = 4
GDN_DH = 128
GDN_CHUNK = 64
N_EXPERTS = 32
TOP_K = 4
SWIGLU_LIMIT = 7.0
SWIGLU_ALPHA = 1.702
MOE_ROW_BLOCK = 256

LANES = 128
GATE_ROWS = 16
FLASH_BODY_TILES = (8, 4)
NSA_W = NSA_HEADS * NSA_DH
GDN_W = GDN_HEADS * GDN_DH

_NT = (((1,), (1,)), ((), ()))
_TN = (((0,), (0,)), ((), ()))


def _cparams(sem, vmem_mb):
    return pltpu.CompilerParams(dimension_semantics=sem, vmem_limit_bytes=vmem_mb * 1024 * 1024)


def _dot(a, b):
    return jnp.dot(a, b, preferred_element_type=F32)


def _dot_nt(a, b):
    return lax.dot_general(a, b, _NT, preferred_element_type=F32)


def _dot_tn(a, b):
    return lax.dot_general(a, b, _TN, preferred_element_type=F32)


def _pack_piece(block):
    words = block.shape[1] // 2
    hi = lax.bitcast_convert_type(block[:, :words].astype(BF16).astype(F32), jnp.uint32)
    lo = lax.bitcast_convert_type(block[:, words:].astype(BF16).astype(F32), jnp.uint32)
    return lax.bitcast_convert_type(hi | (lo >> 16), jnp.int32)


def _store_pieces(ref, val):
    cols = 2 * ref.shape[2]
    for j in range(ref.shape[0]):
        ref[j] = _pack_piece(val[:, j * cols:(j + 1) * cols])


def _join_pieces(ref):
    out = []
    for j in range(ref.shape[0]):
        words = lax.bitcast_convert_type(ref[j], jnp.uint32)
        out.append(lax.bitcast_convert_type(words & jnp.uint32(0xFFFF0000), F32))
        out.append(lax.bitcast_convert_type(words << 16, F32))
    return jnp.concatenate(out, axis=1)


def _inproj_body(x_ref, g_ref, wqt_ref, wkv_ref, wvt_ref, wgt_ref, wqkv_ref, wz_ref, wab_ref, qg_ref, kg_ref,
                 vone_ref, cw_ref, oqt_ref, okn_ref, ocf_ref, ovt_ref, ogt_ref, oqkv_ref, oz_ref, oab_ref, ybuf, cbuf,
                 *, tiles_per_seq):
    x = x_ref[...]
    h = (x * lax.rsqrt(jnp.mean(x * x, axis=-1, keepdims=True) + EPS) * g_ref[...]).astype(BF16)
    tm = x.shape[0]

    yq = _dot_nt(wqt_ref[...], h)
    for s in range(NSA_HEADS):
        sl = slice(s * NSA_DH, (s + 1) * NSA_DH)
        ys = yq[sl, :]
        ms = jnp.sum(ys * ys, axis=0, keepdims=True) * (1.0 / NSA_DH)
        oqt_ref[sl, :] = (ys * lax.rsqrt(ms + EPS) * qg_ref[sl, :]).astype(BF16)

    ykv = _dot(h, wkv_ref[...])
    lane = lax.broadcasted_iota(jnp.int32, (tm, LANES), 1)
    low = lane < NSA_DH
    for s in range(2):
        sl = slice(s * LANES, (s + 1) * LANES)
        ys = ykv[:, (2 + s) * LANES:(3 + s) * LANES]
        y2 = ys * ys
        s0 = jnp.sum(jnp.where(low, y2, 0.0), axis=-1, keepdims=True)
        s1 = jnp.sum(jnp.where(low, 0.0, y2), axis=-1, keepdims=True)
        ms = jnp.where(low, s0, s1) * (1.0 / NSA_DH)
        okn_ref[:, sl] = (ys * lax.rsqrt(ms + EPS) * kg_ref[:, sl]).astype(BF16)

    chunks = tm // CMP_STRIDE
    for br in range(2):
        cbuf[br] = ykv[:, br * LANES:(br + 1) * LANES]
        taken = [cbuf[br, pl.ds(l, chunks, stride=CMP_STRIDE), :] for l in range(CMP_STRIDE)]
        for grp in range(NSA_GROUPS):
            flat = jnp.concatenate([r[:, grp * NSA_DH:(grp + 1) * NSA_DH] for r in taken], axis=1)
            ocf_ref[0, br, grp] = flat.astype(BF16)

    ovt_ref[...] = (_dot_nt(wvt_ref[...], h) + vone_ref[...]).astype(BF16)
    ogt_ref[...] = _dot_nt(wgt_ref[...], h)
    oz_ref[...] = _dot(h, wz_ref[...]).astype(BF16)
    oab_ref[...] = _dot(h, wab_ref[...])

    halo = ybuf.shape[0] - tm
    first = pl.program_id(0) % tiles_per_seq == 0

    @pl.when(first)
    def _():
        ybuf[0:halo, :] = jnp.zeros((halo, ybuf.shape[1]), F32)

    @pl.when(jnp.logical_not(first))
    def _():
        ybuf[0:halo, :] = ybuf[tm:tm + halo, :]

    ybuf[halo:halo + tm, :] = _dot(h, wqkv_ref[...])
    taps = cw_ref.shape[0]
    y = cw_ref[0:1, :] * ybuf[pl.ds(halo - taps + 1, tm), :]
    for k in range(1, taps):
        y = y + cw_ref[k:k + 1, :] * ybuf[pl.ds(halo - taps + 1 + k, tm), :]
    hy = 0.5 * y
    y = hy + hy * jnp.tanh(hy)
    for s in range(3 * GDN_HEADS):
        sl = slice(s * GDN_DH, (s + 1) * GDN_DH)
        ys = y[:, sl]
        if s < 2 * GDN_HEADS:
            scale = GDN_DH ** -0.5 if s < GDN_HEADS else 1.0
            ys = ys * (lax.rsqrt(jnp.sum(ys * ys, axis=-1, keepdims=True) + EPS) * scale)
        oqkv_ref[:, sl] = ys.astype(BF16)


def _in_proj(x2, g, wqt, wkv, wvt, wgt, wqkv, wz, wab, qg, kg, vone, conv_w, tiles_per_seq, tm):
    n, d = x2.shape
    chunks, flat = tm // CMP_STRIDE, CMP_STRIDE * NSA_DH
    full = lambda a: pl.BlockSpec(a.shape, lambda i: (0,) * a.ndim)
    row = lambda w: pl.BlockSpec((tm, w), lambda i: (i, 0))
    colb = lambda r: pl.BlockSpec((r, tm), lambda i: (0, i))
    return pl.pallas_call(
        functools.partial(_inproj_body, tiles_per_seq=tiles_per_seq),
        grid=(n // tm,),
        in_specs=[row(d)] + [full(a) for a in (g, wqt, wkv, wvt, wgt, wqkv, wz, wab, qg, kg, vone, conv_w)],
        out_specs=[colb(wqt.shape[0]), row(2 * LANES),
                   pl.BlockSpec((1, 2, NSA_GROUPS, chunks, flat), lambda i: (i // tiles_per_seq, 0, 0, i % tiles_per_seq, 0)),
                   colb(wvt.shape[0]), colb(wgt.shape[0]), row(wqkv.shape[1]), row(wz.shape[1]), row(wab.shape[1])],
        out_shape=[jax.ShapeDtypeStruct((wqt.shape[0], n), BF16), jax.ShapeDtypeStruct((n, 2 * LANES), BF16),
                   jax.ShapeDtypeStruct((n // (tm * tiles_per_seq), 2, NSA_GROUPS, chunks * tiles_per_seq, flat), BF16),
                   jax.ShapeDtypeStruct((wvt.shape[0], n), BF16), jax.ShapeDtypeStruct((wgt.shape[0], n), F32),
                   jax.ShapeDtypeStruct((n, wqkv.shape[1]), BF16), jax.ShapeDtypeStruct((n, wz.shape[1]), BF16),
                   jax.ShapeDtypeStruct((n, wab.shape[1]), F32)],
        scratch_shapes=[pltpu.VMEM((tm + 8, wqkv.shape[1]), F32), pltpu.VMEM((2, tm, LANES), F32)],
        compiler_params=_cparams(("arbitrary",), 56),
        name="in_proj",
    )(x2, g, wqt, wkv, wvt, wgt, wqkv, wz, wab, qg, kg, vone, conv_w)


def _compress_body(x_ref, pos_ref, w1_ref, b1_ref, w2_ref, b2_ref, w2t_ref, b2t_ref, g_ref, ok_ref, ovt_ref,
                   *, n_cmp):
    is_key = pl.program_id(1) == 0
    nch = x_ref.shape[3]
    hids = []
    for grp in range(NSA_GROUPS):
        x = x_ref[0, 0, grp].astype(F32)
        xa = (x + pos_ref[0, 0]).astype(BF16)
        xb = (x + pos_ref[0, 1]).astype(BF16)
        a = _dot(xa, w1_ref[0, 0])
        b = _dot(xb, w1_ref[0, 1])
        b_next = pltpu.roll(b, nch - 1, 0)
        hids.append(jax.nn.gelu(a + b_next + b1_ref[0]).astype(BF16))

    @pl.when(is_key)
    def _():
        row = lax.broadcasted_iota(jnp.int32, (nch, NSA_DH), 0)
        outs = []
        for grp in range(NSA_GROUPS):
            out = _dot(hids[grp], w2_ref[0]) + b2_ref[0]
            out = out * lax.rsqrt(jnp.mean(out * out, axis=-1, keepdims=True) + EPS) * g_ref[...]
            outs.append(jnp.where(row < n_cmp, out, 0.0))
        ok_ref[0] = jnp.concatenate(outs, axis=-1).astype(BF16)

    @pl.when(jnp.logical_not(is_key))
    def _():
        col = lax.broadcasted_iota(jnp.int32, (NSA_DH, nch), 1)
        outs = []
        for grp in range(NSA_GROUPS):
            out = _dot_nt(w2t_ref[0], hids[grp]) + b2t_ref[0]
            outs.append(jnp.where(col < n_cmp, out, 0.0))
        ovt_ref[0] = jnp.concatenate(outs, axis=0).astype(BF16)


def _compress(xflat, pos, w1, b1, w2, b2, w2t, b2t, kc_g, n_cmp):
    b, _, _, nch, flat = xflat.shape
    return pl.pallas_call(
        functools.partial(_compress_body, n_cmp=n_cmp),
        grid=(b, 2),
        in_specs=[
            pl.BlockSpec((1, 1, NSA_GROUPS, nch, flat), lambda i, j: (i, j, 0, 0, 0)),
            pl.BlockSpec((1, 2, 1, flat), lambda i, j: (j, 0, 0, 0)),
            pl.BlockSpec((1, 2, flat, CMP_HIDDEN), lambda i, j: (j, 0, 0, 0)),
            pl.BlockSpec((1, 1, CMP_HIDDEN), lambda i, j: (j, 0, 0)),
            pl.BlockSpec((1, CMP_HIDDEN, NSA_DH), lambda i, j: (j, 0, 0)),
            pl.BlockSpec((1, 1, NSA_DH), lambda i, j: (j, 0, 0)),
            pl.BlockSpec((1, NSA_DH, CMP_HIDDEN), lambda i, j: (j, 0, 0)),
            pl.BlockSpec((1, NSA_DH, 1), lambda i, j: (j, 0, 0)),
            pl.BlockSpec((1, NSA_DH), lambda i, j: (0, 0)),
        ],
        out_specs=[pl.BlockSpec((1, nch, LANES), lambda i, j: (i, 0, 0)),
                   pl.BlockSpec((1, LANES, nch), lambda i, j: (i, 0, 0))],
        out_shape=[jax.ShapeDtypeStruct((b, nch, LANES), BF16), jax.ShapeDtypeStruct((b, LANES, nch), BF16)],
        compiler_params=_cparams(("parallel", "arbitrary"), 32),
        name="nsa_compress",
    )(xflat, pos, w1, b1, w2, b2, w2t, b2t, kc_g)


def _tile_heads(a):
    return jnp.concatenate([a] * NSA_HPG, axis=1)


def _nsa_body(qt_ref, gt_ref, kc_ref, vct_ref, ks_ref, kw_ref, vst_ref, vwt_ref, et_ref, ov_ref, cpat_ref, wpat_ref,
              dpat_ref, o_ref, acc_sc, s_sc, *, n_top, kt):
    grp = pl.program_id(1)
    s0 = pl.program_id(2) * NSA_Q
    nch = kc_ref.shape[1]
    nblk = ov_ref.shape[0]

    qh = jnp.concatenate([qt_ref[h * NSA_DH:(h + 1) * NSA_DH, :] for h in range(NSA_HPG)], axis=1)
    zq = jnp.zeros_like(qh)
    qt = jnp.where(grp == 0, jnp.concatenate([qh, zq], axis=0), jnp.concatenate([zq, qh], axis=0))
    t_row = s0 + lax.broadcasted_iota(jnp.int32, (1, NSA_Q), 1)

    cbias = cpat_ref[pl.ds(pl.multiple_of(nch - s0 // CMP_STRIDE, CMP_STRIDE), nch), :]
    sc = _dot(kc_ref[0], qt) + _tile_heads(cbias)
    wlen = WINDOW + NSA_Q
    w0 = pl.multiple_of(jnp.maximum(s0 - WINDOW, 0), NSA_Q)
    wbias = wpat_ref[pl.ds(pl.multiple_of(w0 - s0 + WINDOW, NSA_Q), wlen), :]
    sw = _dot(kw_ref[0, pl.ds(w0, wlen), :], qt) + _tile_heads(wbias)
    pc = jnp.exp2(sc - jnp.max(sc, axis=0, keepdims=True)).astype(BF16)
    stacked = jnp.concatenate([vct_ref[0], ov_ref[...], jnp.ones((8, nch), BF16)], axis=0)
    res = _dot(stacked, pc)
    inv = jnp.where(_tile_heads(t_row >= CMP_BLOCK - 1), 1.0 / jnp.maximum(res[LANES + nblk:LANES + nblk + 1], 1e-30),
                    0.0)
    oc = res[:LANES] * inv
    imp4 = res[LANES:LANES + nblk] * inv
    imp = (imp4[:, 0:NSA_Q] + imp4[:, NSA_Q:2 * NSA_Q] + imp4[:, 2 * NSA_Q:3 * NSA_Q]
           + imp4[:, 3 * NSA_Q:4 * NSA_Q])
    blk = lax.broadcasted_iota(jnp.int32, (nblk, NSA_Q), 0)
    cur = t_row // SLC_BLOCK
    imp = jnp.where(blk * SLC_BLOCK > t_row, NEG, imp)
    imp = jnp.where((blk == 0) | (blk == cur) | (blk == cur - 1), -jnp.inf, imp)
    rounds_left = n_top - 3

    def pick_rounds(v, rounds):
        for _ in range(rounds):
            mx = jnp.max(v, axis=0, keepdims=True)
            first = jnp.min(jnp.where(v == mx, blk, nblk), axis=0, keepdims=True)
            v = jnp.where(blk == first, -jnp.inf, v)
        return v

    quarter = rounds_left // 4
    picked = pick_rounds(imp, quarter)

    picked = pick_rounds(picked, quarter)
    pw = jnp.exp2(sw - jnp.max(sw, axis=0, keepdims=True)).astype(BF16)
    picked = pick_rounds(picked, quarter)
    ow = _dot(vwt_ref[:, pl.ds(w0, wlen)], pw)
    ow = ow[:NSA_DH] / ow[NSA_DH:NSA_DH + 1]

    picked = pick_rounds(picked, rounds_left - 3 * quarter)
    chosen = picked == -jnp.inf

    selb = jnp.where(chosen & (blk * SLC_BLOCK < s0), 0.0, MASKED).astype(BF16)
    rhs = jnp.concatenate([qt, _tile_heads(selb)], axis=0)
    last_tile = ks_ref.shape[1] // kt - 1

    def scores(idx, slot):
        k0 = pl.multiple_of(jnp.minimum(idx, last_tile) * kt, kt)
        lhs = jnp.concatenate([ks_ref[0, pl.ds(k0, kt), :], et_ref[pl.ds(k0, kt), :]], axis=1)
        s_sc[slot] = _dot(lhs, rhs)

    scores(0, 0)
    scores(1, 1)

    d0 = pl.multiple_of(s0, NSA_Q)
    selb_d = jnp.where(chosen & (blk <= cur), 0.0, MASKED).astype(BF16)
    sd = (_dot(jnp.concatenate([ks_ref[0, pl.ds(d0, NSA_Q), :], et_ref[pl.ds(d0, NSA_Q), :]], axis=1),
               jnp.concatenate([qt, _tile_heads(selb_d)], axis=0))
          + _tile_heads(dpat_ref[...]))
    m_diag = jnp.max(sd, axis=0, keepdims=True)
    acc_sc[0] = _dot(vst_ref[:, pl.ds(d0, NSA_Q)], jnp.exp2(sd - m_diag).astype(BF16))
    acc_sc[1] = jnp.zeros(acc_sc.shape[1:], F32)

    def update(idx, slot, m_old, acc_ref):
        k0 = pl.multiple_of(idx * kt, kt)
        m_new = jnp.maximum(m_old, jnp.max(s_sc[slot], axis=0, keepdims=True))
        p = jnp.exp2(s_sc[slot] - m_new).astype(BF16)
        acc_ref[...] = jnp.exp2(m_old - m_new) * acc_ref[...] + _dot(vst_ref[:, pl.ds(k0, kt)], p)
        return m_new

    def tile_group(first, carry, count):
        ms = list(carry)
        for t in range(count):
            scores(first + t + 2, (t + 2) % 4)
            ms[t % 2] = update(first + t, t % 4, ms[t % 2], acc_sc.at[t % 2])
        return tuple(ms)

    n_tiles = (s0 + kt - 1) // kt
    carry = (m_diag, jnp.full((1, NSA_HPG * NSA_Q), NEG, F32))
    done = 0
    for size in FLASH_BODY_TILES:
        left = n_tiles - done
        groups = (left + size - 1) // size if size == FLASH_BODY_TILES[-1] else left // size
        carry = lax.fori_loop(0, groups, lambda j, c, done=done, size=size: tile_group(done + size * j, c, size),
                              carry)
        done = done + size * groups
    m0, m1 = carry
    m_fin = jnp.maximum(m0, m1)
    acc = acc_sc[0] * jnp.exp2(m0 - m_fin) + acc_sc[1] * jnp.exp2(m1 - m_fin)
    osl = acc[:NSA_DH] / acc[NSA_DH:NSA_DH + 1]

    oc = jnp.where(grp == 0, oc[:NSA_DH], oc[NSA_DH:])
    gts = jax.nn.sigmoid(gt_ref[...])
    for h in range(NSA_HPG):
        cols = slice(h * NSA_Q, (h + 1) * NSA_Q)
        o_ref[h * NSA_DH:(h + 1) * NSA_DH, :] = (
            gts[3 * h:3 * h + 1, :] * oc[:, cols] + gts[3 * h + 1:3 * h + 2, :] * osl[:, cols]
            + gts[3 * h + 2:3 * h + 3, :] * ow[:, cols])


def _mask_pattern(valid):
    return jnp.asarray(np.where(valid, 0.0, NEG), F32)


def _nsa_attention(qt, gt, kc, vct, okv, vt, expand_t, overlap, b, t, n_top, kt):
    nch = kc.shape[1]
    nq = t // NSA_Q
    n = b * t
    ql = np.arange(NSA_Q)[None, :]
    rc = np.arange(2 * nch)[:, None] - nch
    cpat = _mask_pattern(rc * CMP_STRIDE + CMP_BLOCK - 1 <= ql)
    rw = np.arange(2 * WINDOW + NSA_Q)[:, None] - WINDOW
    wpat = _mask_pattern((rw <= ql) & (rw > ql - WINDOW))
    dpat = _mask_pattern(np.arange(NSA_Q)[:, None] <= ql)
    full2 = lambda a: pl.BlockSpec(a.shape, lambda bi, g, i: (0, 0))
    return pl.pallas_call(
        functools.partial(_nsa_body, n_top=n_top, kt=kt),
        grid=(b, NSA_GROUPS, nq),
        in_specs=[
            pl.BlockSpec((NSA_HPG * NSA_DH, NSA_Q), lambda bi, g, i: (g, bi * nq + i)),
            pl.BlockSpec((GATE_ROWS, NSA_Q), lambda bi, g, i: (g, bi * nq + i)),
            pl.BlockSpec((1, nch, LANES), lambda bi, g, i: (bi, 0, 0)),
            pl.BlockSpec((1, LANES, nch), lambda bi, g, i: (bi, 0, 0)),
            pl.BlockSpec((1, t, LANES), lambda bi, g, i: (bi, 0, 0)),
            pl.BlockSpec((1, t, LANES), lambda bi, g, i: (bi, 0, 1)),
            pl.BlockSpec((LANES, t), lambda bi, g, i: (g, bi)),
            pl.BlockSpec((LANES, t), lambda bi, g, i: (NSA_GROUPS + g, bi)),
            full2(expand_t), full2(overlap), full2(cpat), full2(wpat), full2(dpat),
        ],
        out_specs=pl.BlockSpec((NSA_HPG * NSA_DH, NSA_Q), lambda bi, g, i: (g, bi * nq + i)),
        out_shape=jax.ShapeDtypeStruct((NSA_W, n), F32),
        scratch_shapes=[pltpu.VMEM((2, LANES, NSA_HPG * NSA_Q), F32),
                        pltpu.VMEM((4, kt, NSA_HPG * NSA_Q), F32)],
        compiler_params=_cparams(("parallel", "parallel", "arbitrary"), 56),
        name="nsa_attention",
    )(qt, gt, kc, vct, okv, okv, vt, vt, expand_t, overlap, cpat, wpat, dpat)


def _split_bf16(a):
    hi = a.astype(BF16)
    return hi, (a - hi.astype(F32)).astype(BF16)


def _unit_lower_inverses(lmats):
    c = lmats[0].shape[0]
    r = lax.broadcasted_iota(jnp.int32, (c, c), 0)
    col = lax.broadcasted_iota(jnp.int32, (c, c), 1)
    eye = jnp.where(r == col, 1.0, 0.0)
    xs = [eye - l for l in lmats]
    ps = []
    for l in lmats:
        l16 = l.astype(BF16)
        ps.append(_dot(l16, l16))
    steps = int(np.log2(c)) - 1
    for s in range(steps):
        last = s + 1 == steps
        for i in range(len(lmats)):
            rhs = ps[i].astype(BF16)
            if last:
                xs[i] = xs[i] + _dot(xs[i].astype(BF16), rhs)
            else:
                both = _dot(jnp.concatenate([xs[i], ps[i]], axis=0).astype(BF16), rhs)
                xs[i] = xs[i] + both[:c]
                ps[i] = both[c:]
    return xs


def _gdn_body(x_ref, z_ref, ab_ref, alog_ref, dtb_ref, og_ref, o_ref, s_sc, *, ct):
    nb = x_ref.shape[0]

    @pl.when(pl.program_id(0) == 0)
    def _():
        s_sc[...] = jnp.zeros(s_sc.shape, F32)

    ch = GDN_CHUNK
    r = lax.broadcasted_iota(jnp.int32, (ch, ch), 0)
    col = lax.broadcasted_iota(jnp.int32, (ch, ch), 1)
    incl = r >= col
    strict = r > col
    tril16 = jnp.concatenate([jnp.where(incl, 1.0, 0.0).astype(BF16)] * 3, axis=1)

    units = []
    for ci in range(ct // ch):
        rows = slice(ci * ch, (ci + 1) * ch)
        for bi in range(nb):
            ab = ab_ref[bi, rows, :]
            g_all = -jnp.exp(alog_ref[...]) * jax.nn.softplus(ab + dtb_ref[...])
            beta_all = jax.nn.sigmoid(ab)
            g_hi, g_lo = _split_bf16(g_all)
            g_lo2 = (g_all - g_hi.astype(F32) - g_lo.astype(F32)).astype(BF16)
            gc_all = _dot(tril16, jnp.concatenate([g_hi, g_lo, g_lo2], axis=0))
            gc_t = gc_all.T
            for h in range(GDN_HEADS):
                hs = slice(h * GDN_DH, (h + 1) * GDN_DH)
                q16 = x_ref[bi, rows, hs]
                k16 = x_ref[bi, rows, GDN_W + h * GDN_DH:GDN_W + (h + 1) * GDN_DH]
                qh, kh = q16.astype(F32), k16.astype(F32)
                vh = x_ref[bi, rows, 2 * GDN_W + h * GDN_DH:2 * GDN_W + (h + 1) * GDN_DH].astype(F32)
                gc = gc_all[:, h:h + 1]
                gr = gc_t[h:h + 1, :]
                g_last = gc_all[ch - 1:ch, h:h + 1]
                beta = beta_all[:, GDN_HEADS + h:GDN_HEADS + h + 1]
                eg = jnp.exp(gc)
                decay = jnp.where(incl, jnp.exp(jnp.minimum(gc - gr, 0.0)), 0.0)
                kb = kh * beta
                with_k = (_dot_nt(jnp.concatenate([kb.astype(BF16), q16], axis=0), k16)
                          * jnp.concatenate([decay, decay], axis=0))
                units.append(dict(
                    rows=rows, bi=bi, h=h,
                    lmat=jnp.where(strict, with_k[:ch], 0.0),
                    vb_kbg=jnp.concatenate([(vh * beta).astype(BF16), (kb * eg).astype(BF16)], axis=1),
                    qk=jnp.where(incl, with_k[ch:], 0.0).astype(BF16),
                    qg=(qh * eg).astype(BF16), kd_t=(kh * jnp.exp(g_last - gc)).T.astype(BF16),
                    gl=jnp.exp(g_last)))
    tinvs = _unit_lower_inverses([u["lmat"] for u in units])
    for u, tinv in zip(units, tinvs):
        u_w = _dot(tinv.astype(BF16), u["vb_kbg"])
        u["u"] = u_w[:, :GDN_DH]
        u["w_qg"] = jnp.concatenate([u_w[:, GDN_DH:].astype(BF16), u["qg"]], axis=0)
        u["kd_qk"] = jnp.concatenate([u["kd_t"], u["qk"]], axis=0)

    for u in units:
        bi, h, rows = u["bi"], u["h"], u["rows"]
        hs = slice(h * GDN_DH, (h + 1) * GDN_DH)
        s_old = s_sc[bi * GDN_HEADS + h]
        from_state = _dot(u["w_qg"], s_old.astype(BF16))
        v_new = (u["u"] - from_state[:ch]).astype(BF16)
        from_v = _dot(u["kd_qk"], v_new)
        s_sc[bi * GDN_HEADS + h] = s_old * u["gl"] + from_v[:GDN_DH]
        o = from_state[ch:] + from_v[GDN_DH:]
        on = o * lax.rsqrt(jnp.mean(o * o, axis=-1, keepdims=True) + EPS) * og_ref[...]
        zh = z_ref[bi, rows, hs].astype(F32)
        o_ref[bi, rows, hs] = (on * (zh * jax.nn.sigmoid(zh))).astype(BF16)


def _gdn(oqkv, oz, oab, alog, dtb, og, ct=128):
    b, t, w3 = oqkv.shape
    full = lambda a: pl.BlockSpec(a.shape, lambda c: (0,) * a.ndim)
    return pl.pallas_call(
        functools.partial(_gdn_body, ct=ct),
        grid=(t // ct,),
        in_specs=[
            pl.BlockSpec((b, ct, w3), lambda c: (0, c, 0)),
            pl.BlockSpec((b, ct, GDN_W), lambda c: (0, c, 0)),
            pl.BlockSpec((b, ct, LANES), lambda c: (0, c, 0)),
            full(alog), full(dtb), full(og),
        ],
        out_specs=pl.BlockSpec((b, ct, GDN_W), lambda c: (0, c, 0)),
        out_shape=jax.ShapeDtypeStruct((b, t, GDN_W), BF16),
        scratch_shapes=[pltpu.VMEM((b * GDN_HEADS, GDN_DH, GDN_DH), F32)],
        compiler_params=_cparams(("arbitrary",), 32),
        name="gdn",
    )(oqkv, oz, oab, alog, dtb, og)


def _outproj_body(ont_ref, og_ref, x_ref, ng_ref, wo_ref, fg_ref, wr_ref, br_ref, upper_ref,
                  x1_ref, h2_ref, gate_ref, route_ref, cnt_ref, cnt_sc):
    i = pl.program_id(0)
    tm = x_ref.shape[0]

    @pl.when(i == 0)
    def _():
        cnt_sc[...] = jnp.zeros(cnt_sc.shape, F32)

    a = ont_ref[...]
    a = (a * lax.rsqrt(jnp.mean(a * a, axis=0, keepdims=True) + EPS) * ng_ref[...]).astype(BF16)
    x1 = x_ref[...] + _dot_tn(a, wo_ref[0:NSA_W, :]) + _dot(og_ref[...], wo_ref[NSA_W:, :])
    x1_ref[...] = x1
    h2f = x1 * lax.rsqrt(jnp.mean(x1 * x1, axis=-1, keepdims=True) + EPS) * fg_ref[...]
    _store_pieces(h2_ref, h2f)
    h2 = h2f.astype(BF16)

    logits = (_dot(h2, wr_ref[...]) + br_ref[...]).T[:N_EXPERTS]
    erow = lax.broadcasted_iota(jnp.int32, (N_EXPERTS, tm), 0)
    onehot = jnp.zeros((N_EXPERTS, tm), F32)
    firsts, vals = [], []
    v = logits
    for k in range(TOP_K):
        mx = jnp.max(v, axis=0, keepdims=True)
        first = jnp.min(jnp.where(v == mx, erow, N_EXPERTS), axis=0, keepdims=True)
        hit = erow == first
        v = jnp.where(hit, -jnp.inf, v)
        onehot = jnp.where(hit, 1.0, onehot)
        firsts.append(first)
        vals.append(mx)
    vals = [jnp.exp(m - vals[0]) for m in vals]
    inv = 1.0 / (vals[0] + vals[1] + vals[2] + vals[3])
    gates_t = jnp.concatenate([m * inv for m in vals] + [jnp.zeros((LANES - TOP_K, tm), F32)], axis=0)
    gate_ref[...] = gates_t.T

    excl = cnt_sc[...] + _dot(onehot.astype(BF16), upper_ref[...])
    for k in range(TOP_K):
        route_ref[k:k + 1, :] = firsts[k]
        rank = jnp.sum(jnp.where(erow == firsts[k], excl, 0.0), axis=0, keepdims=True)
        route_ref[TOP_K + k:TOP_K + k + 1, :] = rank.astype(jnp.int32)
    cnt_sc[...] = cnt_sc[...] + jnp.sum(onehot, axis=1, keepdims=True)
    cnt_ref[...] = cnt_sc[...].astype(jnp.int32)


def _out_proj(o_nsa_t, o_gdn, x2, ng, wo, fg, wr, br, tm=512):
    upper = jnp.asarray(np.arange(tm)[:, None] < np.arange(tm)[None, :], BF16)
    n, d = x2.shape
    full = lambda a: pl.BlockSpec(a.shape, lambda i: (0,) * a.ndim)
    row = lambda w: pl.BlockSpec((tm, w), lambda i: (i, 0))
    return pl.pallas_call(
        _outproj_body,
        grid=(n // tm,),
        in_specs=[pl.BlockSpec((NSA_W, tm), lambda i: (0, i)), row(GDN_W), row(d), full(ng), full(wo), full(fg),
                  full(wr), full(br), full(upper)],
        out_specs=[row(d), pl.BlockSpec((d // 2 // SC_SUBROW, tm, SC_SUBROW), lambda i: (0, i, 0)),
                   row(LANES), pl.BlockSpec((2 * TOP_K, tm), lambda i: (0, i)),
                   pl.BlockSpec((N_EXPERTS, 1), lambda i: (0, 0))],
        out_shape=[jax.ShapeDtypeStruct((n, d), F32),
                   jax.ShapeDtypeStruct((d // 2 // SC_SUBROW, n, SC_SUBROW), jnp.int32),
                   jax.ShapeDtypeStruct((n, LANES), F32), jax.ShapeDtypeStruct((2 * TOP_K, n), jnp.int32),
                   jax.ShapeDtypeStruct((N_EXPERTS, 1), jnp.int32)],
        scratch_shapes=[pltpu.VMEM((N_EXPERTS, 1), F32)],
        compiler_params=_cparams(("arbitrary",), 48),
        name="out_proj_router",
    )(o_nsa_t, o_gdn, x2, ng, wo, fg, wr, br, upper)


def _dest_body(ps_ref, route_ref, o_ref, *, n_rows, pieces):
    expert = route_ref[0:TOP_K, :]
    start = jnp.zeros(expert.shape, jnp.int32)
    for e in range(N_EXPERTS):
        start = jnp.where(expert == e, ps_ref[e], start)
    dest = start + route_ref[TOP_K:2 * TOP_K, :]
    for k in range(TOP_K):
        for j in range(pieces):
            o_ref[k * pieces + j:k * pieces + j + 1, :] = dest[k:k + 1, :] + j * n_rows


def _dest_rows(pstarts, route, n_rows, pieces):
    n = route.shape[1]
    tn = min(2048, n)
    grid_spec = pltpu.PrefetchScalarGridSpec(
        num_scalar_prefetch=1,
        grid=(n // tn,),
        in_specs=[pl.BlockSpec((2 * TOP_K, tn), lambda i, ps: (0, i))],
        out_specs=pl.BlockSpec((TOP_K * pieces, tn), lambda i, ps: (0, i)),
    )
    return pl.pallas_call(
        functools.partial(_dest_body, n_rows=n_rows, pieces=pieces),
        grid_spec=grid_spec,
        out_shape=jax.ShapeDtypeStruct((TOP_K * pieces, n), jnp.int32),
        name="moe_dest_rows",
    )(pstarts, route)


def _expert_body(be_ref, end_ref, xs_ref, wg_hbm, bg_ref, wu_hbm, bu_ref, wd_hbm, bd_ref,
                 y_ref, wf32, w16, sems, slot_sc):
    i = pl.program_id(0)
    n_used = be_ref[pl.num_programs(0)]
    used = i < n_used
    expert = be_ref[i]
    fresh = used & ((i == 0) | (expert != be_ref[jnp.maximum(i - 1, 0)]))
    hbm = (wg_hbm, wu_hbm, wd_hbm)

    def weight_copy(e, slot, j):
        return pltpu.make_async_copy(hbm[j].at[e], wf32.at[slot, j], sems.at[slot, j])

    @pl.when(i == 0)
    def _():
        slot_sc[0] = 0

    @pl.when((i == 0) & used)
    def _():
        for j in range(3):
            weight_copy(expert, 0, j).start()

    @pl.when(fresh)
    def _():
        slot = slot_sc[0]
        for j in range(3):
            weight_copy(expert, slot, j).wait()
            w16[j] = wf32[slot, j].astype(BF16)
        following = end_ref[expert]

        @pl.when(following < n_used)
        def _():
            for j in range(3):
                weight_copy(be_ref[following], 1 - slot, j).start()

        slot_sc[0] = 1 - slot

    @pl.when(jnp.logical_not(used))
    def _():
        y_ref[...] = jnp.zeros(y_ref.shape, y_ref.dtype)

    @pl.when(used)
    def _():
        x = _join_pieces(xs_ref).astype(BF16)
        gate = jnp.minimum(_dot(x, w16[0]) + bg_ref[0], SWIGLU_LIMIT)
        up = jnp.clip(_dot(x, w16[1]) + bu_ref[0], -SWIGLU_LIMIT, SWIGLU_LIMIT)
        glu = gate * jax.nn.sigmoid(gate * SWIGLU_ALPHA)
        act = ((up + 1.0) * glu).astype(BF16)
        cols = 2 * y_ref.shape[2]
        for j in range(y_ref.shape[0]):
            sl = slice(j * cols, (j + 1) * cols)
            y_ref[j] = _pack_piece(_dot(act, w16[2, :, sl]) + bd_ref[0, :, sl])


def _experts(blk_e, end_blk, xs, wg, bg, wu, bu, wd, bd):
    pieces, n_rows, sub = xs.shape
    d, de = wg.shape[1], wg.shape[2]
    assert d == de
    r = MOE_ROW_BLOCK
    bspec = lambda w: pl.BlockSpec((1, 1, w), lambda i, be, *_: (be[i], 0, 0))
    hbm = pl.BlockSpec(memory_space=pl.ANY)
    grid_spec = pltpu.PrefetchScalarGridSpec(
        num_scalar_prefetch=2,
        grid=(n_rows // r,),
        in_specs=[pl.BlockSpec((pieces, r, sub), lambda i, *_: (0, i, 0)),
                  hbm, bspec(de), hbm, bspec(de), hbm, bspec(d)],
        out_specs=pl.BlockSpec((pieces, r, sub), lambda i, *_: (0, i, 0)),
        scratch_shapes=[pltpu.VMEM((2, 3, d, de), F32), pltpu.VMEM((3, d, de), BF16),
                        pltpu.SemaphoreType.DMA((2, 3)), pltpu.SMEM((1,), jnp.int32)],
    )
    return pl.pallas_call(
        _expert_body,
        grid_spec=grid_spec,
        out_shape=jax.ShapeDtypeStruct((pieces, n_rows, sub), jnp.int32),
        compiler_params=_cparams(("arbitrary",), 56),
        name="moe_experts",
    )(blk_e, end_blk, xs, wg, bg, wu, bu, wd, bd)


SC_WINDOW = 128
SC_SUBROW = 256


def _sc_mesh():
    return plsc.VectorSubcoreMesh(core_axis_name="c", subcore_axis_name="s")


def _sc_dispatch(h2, dest_rows, n_rows):
    n, d = h2.shape

    @functools.partial(pl.kernel, out_type=jax.ShapeDtypeStruct((n_rows, d), h2.dtype), mesh=_sc_mesh())
    def dispatch(x_hbm, *refs):
        idx_hbm, o_hbm = refs[:TOP_K], refs[TOP_K]

        def body(x_vmem, *idx_vmem):
            for iv in idx_vmem:
                pltpu.sync_copy(x_vmem, o_hbm.at[iv.at[0]])

        pltpu.emit_pipeline(
            body,
            grid=(n // SC_WINDOW,),
            in_specs=[pl.BlockSpec((SC_WINDOW, d), lambda i: (i, 0))]
                     + [pl.BlockSpec((1, SC_WINDOW), lambda i: (0, i))] * TOP_K,
            out_specs=[],
            core_axis_name=("c", "s"),
            dimension_semantics=(pltpu.PARALLEL,),
        )(x_hbm, *idx_hbm)

    return dispatch(h2, *dest_rows)


def _sc_gather(table, idx):
    _, d = table.shape
    m = idx.shape[1]

    @functools.partial(pl.kernel, out_type=jax.ShapeDtypeStruct((m, d), table.dtype), mesh=_sc_mesh())
    def gather(t_hbm, i_hbm, o_hbm):
        def body(i_vmem, o_vmem):
            pltpu.sync_copy(t_hbm.at[i_vmem.at[0]], o_vmem)

        pltpu.emit_pipeline(
            body,
            grid=(m // SC_WINDOW,),
            in_specs=[pl.BlockSpec((1, SC_WINDOW), lambda i: (0, i))],
            out_specs=[pl.BlockSpec((SC_WINDOW, d), lambda i: (i, 0))],
            core_axis_name=("c", "s"),
            dimension_semantics=(pltpu.PARALLEL,),
        )(i_hbm, o_hbm)

    return gather(table, idx)


def _combine_body(x1_ref, y_ref, gate_ref, o_ref):
    acc = x1_ref[...]
    for k in range(TOP_K):
        acc = acc + gate_ref[:, k:k + 1] * _join_pieces(y_ref.at[k])
    o_ref[...] = acc


def _combine(x1, y4, gates, tm=512):
    n, d = x1.shape
    pieces, sub = y4.shape[1], y4.shape[3]
    row = lambda w: pl.BlockSpec((tm, w), lambda i: (i, 0))
    return pl.pallas_call(
        _combine_body,
        grid=(n // tm,),
        in_specs=[row(d), pl.BlockSpec((TOP_K, pieces, tm, sub), lambda i: (0, 0, i, 0)), row(LANES)],
        out_specs=row(d),
        out_shape=jax.ShapeDtypeStruct((n, d), F32),
        compiler_params=_cparams(("parallel",), 48),
        name="moe_combine",
    )(x1, y4, gates)


def _pad_lanes(a, width=LANES):
    return jnp.pad(a, ((0, 0), (0, width - a.shape[1])))


def _layer(x, attn_norm_g, w_in, q_g, kc_g, ks_g, kw_g, ck_pos, ck_w1, ck_b1, ck_w2, ck_b2,
           cv_pos, cv_w1, cv_b1, cv_w2, cv_b2, nsa_out_g, conv_w, a_log, dt_bias, gdn_out_g, w_out,
           ffn_g, router_w, router_b, e_wg, e_bg, e_wu, e_bu, e_wd, e_bd):
    b, t, d = x.shape
    n = b * t
    x2 = x.reshape(n, d)

    o = np.cumsum([0, NSA_W] + [NSA_GROUPS * NSA_DH] * 6 + [3 * NSA_HEADS, 3 * GDN_W, GDN_W, GDN_HEADS, GDN_HEADS])
    wq_t = w_in[:, o[0]:o[1]].T.astype(BF16)
    qg_col = jnp.tile(q_g * (NSA_DH ** -0.5 * np.log2(np.e)), NSA_HEADS).reshape(NSA_W, 1)
    wkv = jnp.concatenate([w_in[:, o[1]:o[4]], w_in[:, o[5]:o[6]]], axis=1).astype(BF16)
    kg = jnp.concatenate([ks_g, ks_g, kw_g, kw_g]).reshape(1, 2 * LANES)
    wv_t = jnp.concatenate([w_in[:, o[4]:o[5]], w_in[:, o[6]:o[7]]], axis=1).T.reshape(2 * NSA_GROUPS, NSA_DH, d)
    wv_t = jnp.pad(wv_t, ((0, 0), (0, LANES - NSA_DH), (0, 0))).reshape(2 * NSA_GROUPS * LANES, d).astype(BF16)
    vone = jnp.asarray((np.arange(2 * NSA_GROUPS * LANES) % LANES == NSA_DH).astype(np.float32)[:, None])
    wg_t = w_in[:, o[7]:o[8]].T.reshape(NSA_GROUPS, NSA_HPG * 3, d)
    wg_t = jnp.pad(wg_t, ((0, 0), (0, GATE_ROWS - NSA_HPG * 3), (0, 0))).reshape(NSA_GROUPS * GATE_ROWS, d)
    wg_t = wg_t.astype(BF16)
    wab = _pad_lanes(w_in[:, o[10]:o[12]]).astype(BF16)
    wqkv = w_in[:, o[8]:o[9]].astype(BF16)
    wz = w_in[:, o[9]:o[10]].astype(BF16)

    tm = min(512, t)
    oqt, okn, xflat, ovt, ogt, oqkv, oz, oab = _in_proj(x2, attn_norm_g.reshape(1, d), wq_t, wkv, wv_t, wg_t, wqkv,
                                                        wz, wab, qg_col, kg, vone, conv_w, t // tm, tm)

    nch = t // CMP_STRIDE
    n_cmp = (t - CMP_BLOCK) // CMP_STRIDE + 1
    half = CMP_STRIDE * NSA_DH
    pos = jnp.stack([ck_pos, cv_pos]).reshape(2, 2, 1, half)
    w1 = jnp.stack([ck_w1, cv_w1]).reshape(2, 2, half, CMP_HIDDEN).astype(BF16)
    b1 = jnp.stack([ck_b1, cv_b1]).reshape(2, 1, CMP_HIDDEN)
    w2 = jnp.stack([ck_w2, cv_w2]).astype(BF16)
    b2 = jnp.stack([ck_b2, cv_b2]).reshape(2, 1, NSA_DH)
    w2t = jnp.stack([ck_w2.T, cv_w2.T]).astype(BF16)
    b2t = jnp.stack([ck_b2, cv_b2]).reshape(2, NSA_DH, 1)
    kc, vct = _compress(xflat, pos, w1, b1, w2, b2, w2t, b2t, kc_g.reshape(1, NSA_DH), n_cmp)

    n_slc = t // SLC_BLOCK
    n_top = min(SLC_TOPK, n_slc)
    assert n_top > 3
    nblk = max(n_slc, LANES)
    kt = min(256, t // 4)
    assert (t // kt) % 4 == 0
    ci = np.arange(nch)[None, :] * CMP_STRIDE
    sj = np.arange(nblk)[:, None] * SLC_BLOCK
    overlap = ((ci < sj + SLC_BLOCK) & (ci + CMP_BLOCK > sj) & (np.arange(nch)[None, :] < n_cmp)
               & (np.arange(nblk)[:, None] < n_slc))
    expand_t = (np.arange(t)[:, None] // SLC_BLOCK) == np.arange(nblk)[None, :]
    o_nsa_t = _nsa_attention(oqt, ogt, kc, vct, okn.reshape(b, t, -1), ovt, jnp.asarray(expand_t, BF16),
                             jnp.asarray(overlap, BF16), b, t, n_top, kt)

    alog_row = _pad_lanes(a_log.reshape(1, GDN_HEADS))
    dtb_row = _pad_lanes(dt_bias.reshape(1, GDN_HEADS))
    o_gdn = _gdn(oqkv.reshape(b, t, -1), oz.reshape(b, t, -1), oab.reshape(b, t, -1),
                 alog_row, dtb_row, gdn_out_g.reshape(1, GDN_DH))

    wr = _pad_lanes(router_w).astype(BF16)
    br = _pad_lanes(router_b.reshape(1, N_EXPERTS))
    x1, h2, gates, route, counts = _out_proj(
        o_nsa_t, o_gdn.reshape(n, GDN_W), x2, nsa_out_g.reshape(NSA_W, 1),
        w_out.astype(BF16), ffn_g.reshape(1, d), wr, br)

    r = MOE_ROW_BLOCK
    nk = n * TOP_K
    counts = counts[:, 0]
    pcounts = (counts + r - 1) // r * r
    pends = jnp.cumsum(pcounts)
    pstarts = pends - pcounts
    n_rows = (nk + r - 1) // r * r + N_EXPERTS * r
    n_blocks = n_rows // r
    blk_start = jnp.arange(n_blocks, dtype=jnp.int32)[:, None] * r
    blk_e = jnp.minimum(jnp.sum(pends[None, :] <= blk_start, axis=1), N_EXPERTS - 1).astype(jnp.int32)
    n_used = (pends[-1] // r).astype(jnp.int32)
    blk_e = jnp.concatenate([blk_e, n_used[None]])
    end_blk = (pends // r).astype(jnp.int32)
    pieces = d // 2 // SC_SUBROW
    dest_p = _dest_rows(pstarts.astype(jnp.int32), route, n_rows, pieces).reshape(TOP_K, pieces, n)
    xs = _sc_dispatch(h2.reshape(pieces * n, SC_SUBROW), [dest_p[k].reshape(1, pieces * n) for k in range(TOP_K)],
                      pieces * n_rows)
    ys = _experts(blk_e, end_blk, xs.reshape(pieces, n_rows, SC_SUBROW), e_wg, e_bg.reshape(N_EXPERTS, 1, -1),
                  e_wu, e_bu.reshape(N_EXPERTS, 1, -1), e_wd, e_bd.reshape(N_EXPERTS, 1, -1))
    y4 = _sc_gather(ys.reshape(pieces * n_rows, SC_SUBROW), dest_p.reshape(1, nk * pieces))
    return _combine(x1, y4.reshape(TOP_K, pieces, n, SC_SUBROW), gates).reshape(b, t, d)


def kernel(x, attn_norm_g, w_in, nsa_q_norm_g, nsa_kc_norm_g, nsa_ks_norm_g, nsa_kw_norm_g, cmp_k_pos, cmp_k_w1, cmp_k_b1, cmp_k_w2, cmp_k_b2, cmp_v_pos, cmp_v_w1, cmp_v_b1, cmp_v_w2, cmp_v_b2, nsa_out_norm_g, gdn_conv_w, gdn_a_log, gdn_dt_bias, gdn_out_norm_g, w_out, ffn_norm_g, router_w, router_b, exp_w_gate, exp_b_gate, exp_w_up, exp_b_up, exp_w_down, exp_b_down):
    params = (attn_norm_g, w_in, nsa_q_norm_g, nsa_kc_norm_g, nsa_ks_norm_g, nsa_kw_norm_g,
              cmp_k_pos, cmp_k_w1, cmp_k_b1, cmp_k_w2, cmp_k_b2, cmp_v_pos, cmp_v_w1, cmp_v_b1, cmp_v_w2, cmp_v_b2,
              nsa_out_norm_g, gdn_conv_w, gdn_a_log, gdn_dt_bias, gdn_out_norm_g, w_out, ffn_norm_g,
              router_w, router_b, exp_w_gate, exp_b_gate, exp_w_up, exp_b_up, exp_w_down, exp_b_down)
    for l in range(attn_norm_g.shape[0]):
        x = _layer(x, *(p[l] for p in params))
    return x
```

```python
import functools

import jax
import jax.numpy as jnp
import numpy as np
from jax import lax
from jax.experimental import pallas as pl
from jax.experimental.pallas import tpu as pltpu
from jax.experimental.pallas import tpu_sc as plsc

F32 = jnp.float32
BF16 = jnp.bfloat16

EPS = 1e-6
NEG = -1e30
MASKED = -2.0 ** 100

NSA_HEADS = 8
NSA_GROUPS = 2
NSA_HPG = 4
NSA_DH = 64
CMP_BLOCK = 32
CMP_STRIDE = 16
CMP_HIDDEN = 256
SLC_BLOCK = 64
SLC_TOPK = 16
WINDOW = 512
NSA_Q = 256
GDN_HEADS = 4
GDN_DH = 128
GDN_CHUNK = 64
N_EXPERTS = 32
TOP_K = 4
SWIGLU_LIMIT = 7.0
SWIGLU_ALPHA = 1.702
MOE_ROW_BLOCK = 256

LANES = 128
GATE_ROWS = 16
FLASH_BODY_TILES = (8, 4)
NSA_W = NSA_HEADS * NSA_DH
GDN_W = GDN_HEADS * GDN_DH

_NT = (((1,), (1,)), ((), ()))
_TN = (((0,), (0,)), ((), ()))


def _cparams(sem, vmem_mb):
    return pltpu.CompilerParams(dimension_semantics=sem, vmem_limit_bytes=vmem_mb * 1024 * 1024)


def _dot(a, b):
    return jnp.dot(a, b, preferred_element_type=F32)


def _dot_nt(a, b):
    return lax.dot_general(a, b, _NT, preferred_element_type=F32)


def _dot_tn(a, b):
    return lax.dot_general(a, b, _TN, preferred_element_type=F32)


def _pack_piece(block):
    words = block.shape[1] // 2
    hi = lax.bitcast_convert_type(block[:, :words].astype(BF16).astype(F32), jnp.uint32)
    lo = lax.bitcast_convert_type(block[:, words:].astype(BF16).astype(F32), jnp.uint32)
    return lax.bitcast_convert_type(hi | (lo >> 16), jnp.int32)


def _store_pieces(ref, val):
    cols = 2 * ref.shape[2]
    for j in range(ref.shape[0]):
        ref[j] = _pack_piece(val[:, j * cols:(j + 1) * cols])


def _join_pieces(ref):
    out = []
    for j in range(ref.shape[0]):
        words = lax.bitcast_convert_type(ref[j], jnp.uint32)
        out.append(lax.bitcast_convert_type(words & jnp.uint32(0xFFFF0000), F32))
        out.append(lax.bitcast_convert_type(words << 16, F32))
    return jnp.concatenate(out, axis=1)


def _inproj_body(x_ref, g_ref, wqt_ref, wkv_ref, wvt_ref, wgt_ref, wqkv_ref, wz_ref, wab_ref, qg_ref, kg_ref,
                 vone_ref, cw_ref, oqt_ref, okn_ref, ocf_ref, ovt_ref, ogt_ref, oqkv_ref, oz_ref, oab_ref, ybuf, cbuf,
                 *, tiles_per_seq):
    tm = x_ref.shape[0]

    halo = ybuf.shape[0] - tm
    first = pl.program_id(0) % tiles_per_seq == 0

    @pl.when(first)
    def _():
        ybuf[0:halo, :] = jnp.zeros((halo, ybuf.shape[1]), F32)

    @pl.when(jnp.logical_not(first))
    def _():
        ybuf[0:halo, :] = ybuf[tm:tm + halo, :]

    x = x_ref[...]
    h = (x * lax.rsqrt(jnp.mean(x * x, axis=-1, keepdims=True) + EPS) * g_ref[...]).astype(BF16)
    ybuf[halo:halo + tm, :] = _dot(h, wqkv_ref[...])

    yq = _dot_nt(wqt_ref[...], h)
    for s in range(NSA_HEADS):
        sl = slice(s * NSA_DH, (s + 1) * NSA_DH)
        ys = yq[sl, :]
        ms = jnp.sum(ys * ys, axis=0, keepdims=True) * (1.0 / NSA_DH)
        oqt_ref[sl, :] = (ys * lax.rsqrt(ms + EPS) * qg_ref[sl, :]).astype(BF16)

    ykv = _dot(h, wkv_ref[...])
    lane = lax.broadcasted_iota(jnp.int32, (tm, LANES), 1)
    low = lane < NSA_DH
    for s in range(2):
        sl = slice(s * LANES, (s + 1) * LANES)
        ys = ykv[:, (2 + s) * LANES:(3 + s) * LANES]
        y2 = ys * ys
        s0 = jnp.sum(jnp.where(low, y2, 0.0), axis=-1, keepdims=True)
        s1 = jnp.sum(jnp.where(low, 0.0, y2), axis=-1, keepdims=True)
        ms = jnp.where(low, s0, s1) * (1.0 / NSA_DH)
        okn_ref[:, sl] = (ys * lax.rsqrt(ms + EPS) * kg_ref[:, sl]).astype(BF16)

    chunks = tm // CMP_STRIDE
    for br in range(2):
        cbuf[br] = ykv[:, br * LANES:(br + 1) * LANES]
        taken = [cbuf[br, pl.ds(l, chunks, stride=CMP_STRIDE), :] for l in range(CMP_STRIDE)]
        for grp in range(NSA_GROUPS):
            flat = jnp.concatenate([r[:, grp * NSA_DH:(grp + 1) * NSA_DH] for r in taken], axis=1)
            ocf_ref[0, br, grp] = flat.astype(BF16)

    ovt_ref[...] = (_dot_nt(wvt_ref[...], h) + vone_ref[...]).astype(BF16)
    ogt_ref[...] = _dot_nt(wgt_ref[...], h)
    oz_ref[...] = _dot(h, wz_ref[...]).astype(BF16)
    oab_ref[...] = _dot(h, wab_ref[...])

    taps = cw_ref.shape[0]
    y = cw_ref[0:1, :] * ybuf[pl.ds(halo - taps + 1, tm), :]
    for k in range(1, taps):
        y = y + cw_ref[k:k + 1, :] * ybuf[pl.ds(halo - taps + 1 + k, tm), :]
    hy = 0.5 * y
    y = hy + hy * jnp.tanh(hy)
    for s in range(3 * GDN_HEADS):
        sl = slice(s * GDN_DH, (s + 1) * GDN_DH)
        ys = y[:, sl]
        if s < 2 * GDN_HEADS:
            scale = GDN_DH ** -0.5 if s < GDN_HEADS else 1.0
            ys = ys * (lax.rsqrt(jnp.sum(ys * ys, axis=-1, keepdims=True) + EPS) * scale)
        oqkv_ref[:, sl] = ys.astype(BF16)


def _in_proj(x2, g, wqt, wkv, wvt, wgt, wqkv, wz, wab, qg, kg, vone, conv_w, tiles_per_seq, tm):
    n, d = x2.shape
    chunks, flat = tm // CMP_STRIDE, CMP_STRIDE * NSA_DH
    full = lambda a: pl.BlockSpec(a.shape, lambda i: (0,) * a.ndim)
    row = lambda w: pl.BlockSpec((tm, w), lambda i: (i, 0))
    colb = lambda r: pl.BlockSpec((r, tm), lambda i: (0, i))
    return pl.pallas_call(
        functools.partial(_inproj_body, tiles_per_seq=tiles_per_seq),
        grid=(n // tm,),
        in_specs=[row(d)] + [full(a) for a in (g, wqt, wkv, wvt, wgt, wqkv, wz, wab, qg, kg, vone, conv_w)],
        out_specs=[colb(wqt.shape[0]), row(2 * LANES),
                   pl.BlockSpec((1, 2, NSA_GROUPS, chunks, flat), lambda i: (i // tiles_per_seq, 0, 0, i % tiles_per_seq, 0)),
                   colb(wvt.shape[0]), colb(wgt.shape[0]), row(wqkv.shape[1]), row(wz.shape[1]), row(wab.shape[1])],
        out_shape=[jax.ShapeDtypeStruct((wqt.shape[0], n), BF16), jax.ShapeDtypeStruct((n, 2 * LANES), BF16),
                   jax.ShapeDtypeStruct((n // (tm * tiles_per_seq), 2, NSA_GROUPS, chunks * tiles_per_seq, flat), BF16),
                   jax.ShapeDtypeStruct((wvt.shape[0], n), BF16), jax.ShapeDtypeStruct((wgt.shape[0], n), F32),
                   jax.ShapeDtypeStruct((n, wqkv.shape[1]), BF16), jax.ShapeDtypeStruct((n, wz.shape[1]), BF16),
                   jax.ShapeDtypeStruct((n, wab.shape[1]), F32)],
        scratch_shapes=[pltpu.VMEM((tm + 8, wqkv.shape[1]), F32), pltpu.VMEM((2, tm, LANES), F32)],
        compiler_params=_cparams(("arbitrary",), 56),
        name="in_proj",
    )(x2, g, wqt, wkv, wvt, wgt, wqkv, wz, wab, qg, kg, vone, conv_w)


def _compress_body(x_ref, pos_ref, w1_ref, b1_ref, w2_ref, b2_ref, w2t_ref, b2t_ref, g_ref, ok_ref, ovt_ref,
                   *, n_cmp):
    is_key = pl.program_id(1) == 0
    nch = x_ref.shape[3]
    hids = []
    for grp in range(NSA_GROUPS):
        x = x_ref[0, 0, grp].astype(F32)
        xa = (x + pos_ref[0, 0]).astype(BF16)
        xb = (x + pos_ref[0, 1]).astype(BF16)
        a = _dot(xa, w1_ref[0, 0])
        b = _dot(xb, w1_ref[0, 1])
        b_next = pltpu.roll(b, nch - 1, 0)
        hids.append(jax.nn.gelu(a + b_next + b1_ref[0]).astype(BF16))

    @pl.when(is_key)
    def _():
        row = lax.broadcasted_iota(jnp.int32, (nch, NSA_DH), 0)
        outs = []
        for grp in range(NSA_GROUPS):
            out = _dot(hids[grp], w2_ref[0]) + b2_ref[0]
            out = out * lax.rsqrt(jnp.mean(out * out, axis=-1, keepdims=True) + EPS) * g_ref[...]
            outs.append(jnp.where(row < n_cmp, out, 0.0))
        ok_ref[0] = jnp.concatenate(outs, axis=-1).astype(BF16)

    @pl.when(jnp.logical_not(is_key))
    def _():
        col = lax.broadcasted_iota(jnp.int32, (NSA_DH, nch), 1)
        outs = []
        for grp in range(NSA_GROUPS):
            out = _dot_nt(w2t_ref[0], hids[grp]) + b2t_ref[0]
            outs.append(jnp.where(col < n_cmp, out, 0.0))
        ovt_ref[0] = jnp.concatenate(outs, axis=0).astype(BF16)


def _compress(xflat, pos, w1, b1, w2, b2, w2t, b2t, kc_g, n_cmp):
    b, _, _, nch, flat = xflat.shape
    return pl.pallas_call(
        functools.partial(_compress_body, n_cmp=n_cmp),
        grid=(b, 2),
        in_specs=[
            pl.BlockSpec((1, 1, NSA_GROUPS, nch, flat), lambda i, j: (i, j, 0, 0, 0)),
            pl.BlockSpec((1, 2, 1, flat), lambda i, j: (j, 0, 0, 0)),
            pl.BlockSpec((1, 2, flat, CMP_HIDDEN), lambda i, j: (j, 0, 0, 0)),
            pl.BlockSpec((1, 1, CMP_HIDDEN), lambda i, j: (j, 0, 0)),
            pl.BlockSpec((1, CMP_HIDDEN, NSA_DH), lambda i, j: (j, 0, 0)),
            pl.BlockSpec((1, 1, NSA_DH), lambda i, j: (j, 0, 0)),
            pl.BlockSpec((1, NSA_DH, CMP_HIDDEN), lambda i, j: (j, 0, 0)),
            pl.BlockSpec((1, NSA_DH, 1), lambda i, j: (j, 0, 0)),
            pl.BlockSpec((1, NSA_DH), lambda i, j: (0, 0)),
        ],
        out_specs=[pl.BlockSpec((1, nch, LANES), lambda i, j: (i, 0, 0)),
                   pl.BlockSpec((1, LANES, nch), lambda i, j: (i, 0, 0))],
        out_shape=[jax.ShapeDtypeStruct((b, nch, LANES), BF16), jax.ShapeDtypeStruct((b, LANES, nch), BF16)],
        compiler_params=_cparams(("parallel", "arbitrary"), 32),
        name="nsa_compress",
    )(xflat, pos, w1, b1, w2, b2, w2t, b2t, kc_g)


def _tile_heads(a):
    return jnp.concatenate([a] * NSA_HPG, axis=1)


def _nsa_body(qt_ref, gt_ref, kc_ref, vct_ref, ks_ref, kw_ref, vst_ref, vwt_ref, et_ref, ov_ref, cpat_ref, wpat_ref,
              dpat_ref, o_ref, acc_sc, s_sc, *, n_top, kt):
    grp = pl.program_id(1)
    s0 = pl.program_id(2) * NSA_Q
    nch = kc_ref.shape[1]
    nblk = ov_ref.shape[0]

    qh = jnp.concatenate([qt_ref[h * NSA_DH:(h + 1) * NSA_DH, :] for h in range(NSA_HPG)], axis=1)
    zq = jnp.zeros_like(qh)
    qt = jnp.where(grp == 0, jnp.concatenate([qh, zq], axis=0), jnp.concatenate([zq, qh], axis=0))
    t_row = s0 + lax.broadcasted_iota(jnp.int32, (1, NSA_Q), 1)

    cbias = cpat_ref[pl.ds(pl.multiple_of(nch - s0 // CMP_STRIDE, CMP_STRIDE), nch), :]
    sc = _dot(kc_ref[0], qt) + _tile_heads(cbias)
    wlen = WINDOW + NSA_Q
    w0 = pl.multiple_of(jnp.maximum(s0 - WINDOW, 0), NSA_Q)
    wbias = wpat_ref[pl.ds(pl.multiple_of(w0 - s0 + WINDOW, NSA_Q), wlen), :]
    sw = _dot(kw_ref[0, pl.ds(w0, wlen), :], qt) + _tile_heads(wbias)
    pc = jnp.exp2(sc - jnp.max(sc, axis=0, keepdims=True)).astype(BF16)
    stacked = jnp.concatenate([vct_ref[0], ov_ref[...], jnp.ones((8, nch), BF16)], axis=0)
    res = _dot(stacked, pc)
    inv = jnp.where(_tile_heads(t_row >= CMP_BLOCK - 1), 1.0 / jnp.maximum(res[LANES + nblk:LANES + nblk + 1], 1e-30),
                    0.0)
    oc = res[:LANES] * inv
    imp4 = res[LANES:LANES + nblk] * inv
    imp = (imp4[:, 0:NSA_Q] + imp4[:, NSA_Q:2 * NSA_Q] + imp4[:, 2 * NSA_Q:3 * NSA_Q]
           + imp4[:, 3 * NSA_Q:4 * NSA_Q])
    blk = lax.broadcasted_iota(jnp.int32, (nblk, NSA_Q), 0)
    cur = t_row // SLC_BLOCK
    imp = jnp.where(blk * SLC_BLOCK > t_row, NEG, imp)
    imp = jnp.where((blk == 0) | (blk == cur) | (blk == cur - 1), -jnp.inf, imp)
    rounds_left = n_top - 3

    def pick_rounds(v, rounds):
        for _ in range(rounds):
            mx = jnp.max(v, axis=0, keepdims=True)
            first = jnp.min(jnp.where(v == mx, blk, nblk), axis=0, keepdims=True)
            v = jnp.where(blk == first, -jnp.inf, v)
        return v

    quarter = rounds_left // 4
    picked = pick_rounds(imp, quarter)

    picked = pick_rounds(picked, quarter)
    pw = jnp.exp2(sw - jnp.max(sw, axis=0, keepdims=True)).astype(BF16)
    picked = pick_rounds(picked, quarter)
    ow = _dot(vwt_ref[:, pl.ds(w0, wlen)], pw)
    ow = ow[:NSA_DH] / ow[NSA_DH:NSA_DH + 1]

    picked = pick_rounds(picked, rounds_left - 3 * quarter)
    chosen = picked == -jnp.inf

    selb = jnp.where(chosen & (blk * SLC_BLOCK < s0), 0.0, MASKED).astype(BF16)
    rhs = jnp.concatenate([qt, _tile_heads(selb)], axis=0)
    last_tile = ks_ref.shape[1] // kt - 1

    def scores(idx, slot):
        k0 = pl.multiple_of(jnp.minimum(idx, last_tile) * kt, kt)
        lhs = jnp.concatenate([ks_ref[0, pl.ds(k0, kt), :], et_ref[pl.ds(k0, kt), :]], axis=1)
        s_sc[slot] = _dot(lhs, rhs)

    scores(0, 0)
    scores(1, 1)

    d0 = pl.multiple_of(s0, NSA_Q)
    selb_d = jnp.where(chosen & (blk <= cur), 0.0, MASKED).astype(BF16)
    sd = (_dot(jnp.concatenate([ks_ref[0, pl.ds(d0, NSA_Q), :], et_ref[pl.ds(d0, NSA_Q), :]], axis=1),
               jnp.concatenate([qt, _tile_heads(selb_d)], axis=0))
          + _tile_heads(dpat_ref[...]))
    m_diag = jnp.max(sd, axis=0, keepdims=True)
    acc_sc[0] = _dot(vst_ref[:, pl.ds(d0, NSA_Q)], jnp.exp2(sd - m_diag).astype(BF16))
    acc_sc[1] = jnp.zeros(acc_sc.shape[1:], F32)

    def update(idx, slot, m_old, acc_ref):
        k0 = pl.multiple_of(idx * kt, kt)
        m_new = jnp.maximum(m_old, jnp.max(s_sc[slot], axis=0, keepdims=True))
        p = jnp.exp2(s_sc[slot] - m_new).astype(BF16)
        acc_ref[...] = jnp.exp2(m_old - m_new) * acc_ref[...] + _dot(vst_ref[:, pl.ds(k0, kt)], p)
        return m_new

    def tile_group(first, carry, count):
        ms = list(carry)
        for t in range(count):
            scores(first + t + 2, (t + 2) % 4)
            ms[t % 2] = update(first + t, t % 4, ms[t % 2], acc_sc.at[t % 2])
        return tuple(ms)

    n_tiles = (s0 + kt - 1) // kt
    carry = (m_diag, jnp.full((1, NSA_HPG * NSA_Q), NEG, F32))
    done = 0
    for size in FLASH_BODY_TILES:
        left = n_tiles - done
        groups = (left + size - 1) // size if size == FLASH_BODY_TILES[-1] else left // size
        carry = lax.fori_loop(0, groups, lambda j, c, done=done, size=size: tile_group(done + size * j, c, size),
                              carry)
        done = done + size * groups
    m0, m1 = carry
    m_fin = jnp.maximum(m0, m1)
    acc = acc_sc[0] * jnp.exp2(m0 - m_fin) + acc_sc[1] * jnp.exp2(m1 - m_fin)
    osl = acc[:NSA_DH] / acc[NSA_DH:NSA_DH + 1]

    oc = jnp.where(grp == 0, oc[:NSA_DH], oc[NSA_DH:])
    gts = jax.nn.sigmoid(gt_ref[...])
    for h in range(NSA_HPG):
        cols = slice(h * NSA_Q, (h + 1) * NSA_Q)
        o_ref[h * NSA_DH:(h + 1) * NSA_DH, :] = (
            gts[3 * h:3 * h + 1, :] * oc[:, cols] + gts[3 * h + 1:3 * h + 2, :] * osl[:, cols]
            + gts[3 * h + 2:3 * h + 3, :] * ow[:, cols])


def _mask_pattern(valid):
    return jnp.asarray(np.where(valid, 0.0, NEG), F32)


def _nsa_attention(qt, gt, kc, vct, okv, vt, expand_t, overlap, b, t, n_top, kt):
    nch = kc.shape[1]
    nq = t // NSA_Q
    n = b * t
    ql = np.arange(NSA_Q)[None, :]
    rc = np.arange(2 * nch)[:, None] - nch
    cpat = _mask_pattern(rc * CMP_STRIDE + CMP_BLOCK - 1 <= ql)
    rw = np.arange(2 * WINDOW + NSA_Q)[:, None] - WINDOW
    wpat = _mask_pattern((rw <= ql) & (rw > ql - WINDOW))
    dpat = _mask_pattern(np.arange(NSA_Q)[:, None] <= ql)
    full2 = lambda a: pl.BlockSpec(a.shape, lambda bi, g, i: (0, 0))
    return pl.pallas_call(
        functools.partial(_nsa_body, n_top=n_top, kt=kt),
        grid=(b, NSA_GROUPS, nq),
        in_specs=[
            pl.BlockSpec((NSA_HPG * NSA_DH, NSA_Q), lambda bi, g, i: (g, bi * nq + i)),
            pl.BlockSpec((GATE_ROWS, NSA_Q), lambda bi, g, i: (g, bi * nq + i)),
            pl.BlockSpec((1, nch, LANES), lambda bi, g, i: (bi, 0, 0)),
            pl.BlockSpec((1, LANES, nch), lambda bi, g, i: (bi, 0, 0)),
            pl.BlockSpec((1, t, LANES), lambda bi, g, i: (bi, 0, 0)),
            pl.BlockSpec((1, t, LANES), lambda bi, g, i: (bi, 0, 1)),
            pl.BlockSpec((LANES, t), lambda bi, g, i: (g, bi)),
            pl.BlockSpec((LANES, t), lambda bi, g, i: (NSA_GROUPS + g, bi)),
            full2(expand_t), full2(overlap), full2(cpat), full2(wpat), full2(dpat),
        ],
        out_specs=pl.BlockSpec((NSA_HPG * NSA_DH, NSA_Q), lambda bi, g, i: (g, bi * nq + i)),
        out_shape=jax.ShapeDtypeStruct((NSA_W, n), F32),
        scratch_shapes=[pltpu.VMEM((2, LANES, NSA_HPG * NSA_Q), F32),
                        pltpu.VMEM((4, kt, NSA_HPG * NSA_Q), F32)],
        compiler_params=_cparams(("parallel", "parallel", "arbitrary"), 56),
        name="nsa_attention",
    )(qt, gt, kc, vct, okv, okv, vt, vt, expand_t, overlap, cpat, wpat, dpat)


def _split_bf16(a):
    hi = a.astype(BF16)
    return hi, (a - hi.astype(F32)).astype(BF16)


def _unit_lower_inverses(lmats):
    c = lmats[0].shape[0]
    r = lax.broadcasted_iota(jnp.int32, (c, c), 0)
    col = lax.broadcasted_iota(jnp.int32, (c, c), 1)
    eye = jnp.where(r == col, 1.0, 0.0)
    xs = [eye - l for l in lmats]
    ps = []
    for l in lmats:
        l16 = l.astype(BF16)
        ps.append(_dot(l16, l16))
    steps = int(np.log2(c)) - 1
    for s in range(steps):
        last = s + 1 == steps
        for i in range(len(lmats)):
            rhs = ps[i].astype(BF16)
            if last:
                xs[i] = xs[i] + _dot(xs[i].astype(BF16), rhs)
            else:
                both = _dot(jnp.concatenate([xs[i], ps[i]], axis=0).astype(BF16), rhs)
                xs[i] = xs[i] + both[:c]
                ps[i] = both[c:]
    return xs


def _gdn_body(x_ref, z_ref, ab_ref, alog_ref, dtb_ref, og_ref, o_ref, s_sc, *, ct):
    nb = x_ref.shape[0]

    @pl.when(pl.program_id(0) == 0)
    def _():
        s_sc[...] = jnp.zeros(s_sc.shape, F32)

    ch = GDN_CHUNK
    r = lax.broadcasted_iota(jnp.int32, (ch, ch), 0)
    col = lax.broadcasted_iota(jnp.int32, (ch, ch), 1)
    incl = r >= col
    strict = r > col
    tril16 = jnp.concatenate([jnp.where(incl, 1.0, 0.0).astype(BF16)] * 3, axis=1)

    units = []
    for ci in range(ct // ch):
        rows = slice(ci * ch, (ci + 1) * ch)
        for bi in range(nb):
            ab = ab_ref[bi, rows, :]
            g_all = -jnp.exp(alog_ref[...]) * jax.nn.softplus(ab + dtb_ref[...])
            beta_all = jax.nn.sigmoid(ab)
            g_hi, g_lo = _split_bf16(g_all)
            g_lo2 = (g_all - g_hi.astype(F32) - g_lo.astype(F32)).astype(BF16)
            gc_all = _dot(tril16, jnp.concatenate([g_hi, g_lo, g_lo2], axis=0))
            gc_t = gc_all.T
            for h in range(GDN_HEADS):
                hs = slice(h * GDN_DH, (h + 1) * GDN_DH)
                q16 = x_ref[bi, rows, hs]
                k16 = x_ref[bi, rows, GDN_W + h * GDN_DH:GDN_W + (h + 1) * GDN_DH]
                qh, kh = q16.astype(F32), k16.astype(F32)
                vh = x_ref[bi, rows, 2 * GDN_W + h * GDN_DH:2 * GDN_W + (h + 1) * GDN_DH].astype(F32)
                gc = gc_all[:, h:h + 1]
                gr = gc_t[h:h + 1, :]
                g_last = gc_all[ch - 1:ch, h:h + 1]
                beta = beta_all[:, GDN_HEADS + h:GDN_HEADS + h + 1]
                eg = jnp.exp(gc)
                decay = jnp.where(incl, jnp.exp(jnp.minimum(gc - gr, 0.0)), 0.0)
                kb = kh * beta
                with_k = (_dot_nt(jnp.concatenate([kb.astype(BF16), q16], axis=0), k16)
                          * jnp.concatenate([decay, decay], axis=0))
                units.append(dict(
                    rows=rows, bi=bi, h=h,
                    lmat=jnp.where(strict, with_k[:ch], 0.0),
                    vb_kbg=jnp.concatenate([(vh * beta).astype(BF16), (kb * eg).astype(BF16)], axis=1),
                    qk=jnp.where(incl, with_k[ch:], 0.0).astype(BF16),
                    qg=(qh * eg).astype(BF16), kd_t=(kh * jnp.exp(g_last - gc)).T.astype(BF16),
                    gl=jnp.exp(g_last)))
    tinvs = _unit_lower_inverses([u["lmat"] for u in units])
    for u, tinv in zip(units, tinvs):
        u_w = _dot(tinv.astype(BF16), u["vb_kbg"])
        u["u"] = u_w[:, :GDN_DH]
        u["w_qg"] = jnp.concatenate([u_w[:, GDN_DH:].astype(BF16), u["qg"]], axis=0)
        u["kd_qk"] = jnp.concatenate([u["kd_t"], u["qk"]], axis=0)

    for u in units:
        bi, h, rows = u["bi"], u["h"], u["rows"]
        hs = slice(h * GDN_DH, (h + 1) * GDN_DH)
        s_old = s_sc[bi * GDN_HEADS + h]
        from_state = _dot(u["w_qg"], s_old.astype(BF16))
        v_new = (u["u"] - from_state[:ch]).astype(BF16)
        from_v = _dot(u["kd_qk"], v_new)
        s_sc[bi * GDN_HEADS + h] = s_old * u["gl"] + from_v[:GDN_DH]
        o = from_state[ch:] + from_v[GDN_DH:]
        on = o * lax.rsqrt(jnp.mean(o * o, axis=-1, keepdims=True) + EPS) * og_ref[...]
        zh = z_ref[bi, rows, hs].astype(F32)
        o_ref[bi, rows, hs] = (on * (zh * jax.nn.sigmoid(zh))).astype(BF16)


def _gdn(oqkv, oz, oab, alog, dtb, og, ct=128):
    b, t, w3 = oqkv.shape
    full = lambda a: pl.BlockSpec(a.shape, lambda c: (0,) * a.ndim)
    return pl.pallas_call(
        functools.partial(_gdn_body, ct=ct),
        grid=(t // ct,),
        in_specs=[
            pl.BlockSpec((b, ct, w3), lambda c: (0, c, 0)),
            pl.BlockSpec((b, ct, GDN_W), lambda c: (0, c, 0)),
            pl.BlockSpec((b, ct, LANES), lambda c: (0, c, 0)),
            full(alog), full(dtb), full(og),
        ],
        out_specs=pl.BlockSpec((b, ct, GDN_W), lambda c: (0, c, 0)),
        out_shape=jax.ShapeDtypeStruct((b, t, GDN_W), BF16),
        scratch_shapes=[pltpu.VMEM((b * GDN_HEADS, GDN_DH, GDN_DH), F32)],
        compiler_params=_cparams(("arbitrary",), 32),
        name="gdn",
    )(oqkv, oz, oab, alog, dtb, og)


def _outproj_body(ont_ref, og_ref, x_ref, ng_ref, wo_ref, fg_ref, wr_ref, br_ref, upper_ref,
                  x1_ref, h2_ref, gate_ref, route_ref, cnt_ref, cnt_sc):
    i = pl.program_id(0)
    tm = x_ref.shape[0]

    @pl.when(i == 0)
    def _():
        cnt_sc[...] = jnp.zeros(cnt_sc.shape, F32)

    a = ont_ref[...]
    a = (a * lax.rsqrt(jnp.mean(a * a, axis=0, keepdims=True) + EPS) * ng_ref[...]).astype(BF16)
    x1 = x_ref[...] + _dot_tn(a, wo_ref[0:NSA_W, :]) + _dot(og_ref[...], wo_ref[NSA_W:, :])
    x1_ref[...] = x1
    h2f = x1 * lax.rsqrt(jnp.mean(x1 * x1, axis=-1, keepdims=True) + EPS) * fg_ref[...]
    _store_pieces(h2_ref, h2f)
    h2 = h2f.astype(BF16)

    logits = (_dot(h2, wr_ref[...]) + br_ref[...]).T[:N_EXPERTS]
    erow = lax.broadcasted_iota(jnp.int32, (N_EXPERTS, tm), 0)
    onehot = jnp.zeros((N_EXPERTS, tm), F32)
    firsts, vals = [], []
    v = logits
    for k in range(TOP_K):
        mx = jnp.max(v, axis=0, keepdims=True)
        first = jnp.min(jnp.where(v == mx, erow, N_EXPERTS), axis=0, keepdims=True)
        hit = erow == first
        v = jnp.where(hit, -jnp.inf, v)
        onehot = jnp.where(hit, 1.0, onehot)
        firsts.append(first)
        vals.append(mx)
    vals = [jnp.exp(m - vals[0]) for m in vals]
    inv = 1.0 / (vals[0] + vals[1] + vals[2] + vals[3])
    gates_t = jnp.concatenate([m * inv for m in vals] + [jnp.zeros((LANES - TOP_K, tm), F32)], axis=0)
    gate_ref[...] = gates_t.T

    excl = cnt_sc[...] + _dot(onehot.astype(BF16), upper_ref[...])
    for k in range(TOP_K):
        route_ref[k:k + 1, :] = firsts[k]
        rank = jnp.sum(jnp.where(erow == firsts[k], excl, 0.0), axis=0, keepdims=True)
        route_ref[TOP_K + k:TOP_K + k + 1, :] = rank.astype(jnp.int32)
    cnt_sc[...] = cnt_sc[...] + jnp.sum(onehot, axis=1, keepdims=True)
    cnt_ref[...] = cnt_sc[...].astype(jnp.int32)


def _out_proj(o_nsa_t, o_gdn, x2, ng, wo, fg, wr, br, tm=512):
    upper = jnp.asarray(np.arange(tm)[:, None] < np.arange(tm)[None, :], BF16)
    n, d = x2.shape
    full = lambda a: pl.BlockSpec(a.shape, lambda i: (0,) * a.ndim)
    row = lambda w: pl.BlockSpec((tm, w), lambda i: (i, 0))
    return pl.pallas_call(
        _outproj_body,
        grid=(n // tm,),
        in_specs=[pl.BlockSpec((NSA_W, tm), lambda i: (0, i)), row(GDN_W), row(d), full(ng), full(wo), full(fg),
                  full(wr), full(br), full(upper)],
        out_specs=[row(d), pl.BlockSpec((d // 2 // SC_SUBROW, tm, SC_SUBROW), lambda i: (0, i, 0)),
                   row(LANES), pl.BlockSpec((2 * TOP_K, tm), lambda i: (0, i)),
                   pl.BlockSpec((N_EXPERTS, 1), lambda i: (0, 0))],
        out_shape=[jax.ShapeDtypeStruct((n, d), F32),
                   jax.ShapeDtypeStruct((d // 2 // SC_SUBROW, n, SC_SUBROW), jnp.int32),
                   jax.ShapeDtypeStruct((n, LANES), F32), jax.ShapeDtypeStruct((2 * TOP_K, n), jnp.int32),
                   jax.ShapeDtypeStruct((N_EXPERTS, 1), jnp.int32)],
        scratch_shapes=[pltpu.VMEM((N_EXPERTS, 1), F32)],
        compiler_params=_cparams(("arbitrary",), 48),
        name="out_proj_router",
    )(o_nsa_t, o_gdn, x2, ng, wo, fg, wr, br, upper)


def _dest_body(ps_ref, route_ref, o_ref, *, n_rows, pieces):
    expert = route_ref[0:TOP_K, :]
    start = jnp.zeros(expert.shape, jnp.int32)
    for e in range(N_EXPERTS):
        start = jnp.where(expert == e, ps_ref[e], start)
    dest = start + route_ref[TOP_K:2 * TOP_K, :]
    for k in range(TOP_K):
        for j in range(pieces):
            o_ref[k * pieces + j:k * pieces + j + 1, :] = dest[k:k + 1, :] + j * n_rows


def _dest_rows(pstarts, route, n_rows, pieces):
    n = route.shape[1]
    tn = min(2048, n)
    grid_spec = pltpu.PrefetchScalarGridSpec(
        num_scalar_prefetch=1,
        grid=(n // tn,),
        in_specs=[pl.BlockSpec((2 * TOP_K, tn), lambda i, ps: (0, i))],
        out_specs=pl.BlockSpec((TOP_K * pieces, tn), lambda i, ps: (0, i)),
    )
    return pl.pallas_call(
        functools.partial(_dest_body, n_rows=n_rows, pieces=pieces),
        grid_spec=grid_spec,
        out_shape=jax.ShapeDtypeStruct((TOP_K * pieces, n), jnp.int32),
        name="moe_dest_rows",
    )(pstarts, route)


def _expert_body(be_ref, end_ref, xs_ref, wg_hbm, bg_ref, wu_hbm, bu_ref, wd_hbm, bd_ref,
                 y_ref, wf32, w16, sems, slot_sc):
    i = pl.program_id(0)
    n_used = be_ref[pl.num_programs(0)]
    used = i < n_used
    expert = be_ref[i]
    fresh = used & ((i == 0) | (expert != be_ref[jnp.maximum(i - 1, 0)]))
    hbm = (wg_hbm, wu_hbm, wd_hbm)

    def weight_copy(e, slot, j):
        return pltpu.make_async_copy(hbm[j].at[e], wf32.at[slot, j], sems.at[slot, j])

    @pl.when(i == 0)
    def _():
        slot_sc[0] = 0

    @pl.when((i == 0) & used)
    def _():
        for j in range(3):
            weight_copy(expert, 0, j).start()

    @pl.when(fresh)
    def _():
        slot = slot_sc[0]
        for j in range(3):
            weight_copy(expert, slot, j).wait()
            w16[j] = wf32[slot, j].astype(BF16)
        following = end_ref[expert]

        @pl.when(following < n_used)
        def _():
            for j in range(3):
                weight_copy(be_ref[following], 1 - slot, j).start()

        slot_sc[0] = 1 - slot

    @pl.when(jnp.logical_not(used))
    def _():
        y_ref[...] = jnp.zeros(y_ref.shape, y_ref.dtype)

    @pl.when(used)
    def _():
        x = _join_pieces(xs_ref).astype(BF16)
        gate = jnp.minimum(_dot(x, w16[0]) + bg_ref[0], SWIGLU_LIMIT)
        up = jnp.clip(_dot(x, w16[1]) + bu_ref[0], -SWIGLU_LIMIT, SWIGLU_LIMIT)
        glu = gate * jax.nn.sigmoid(gate * SWIGLU_ALPHA)
        act = ((up + 1.0) * glu).astype(BF16)
        cols = 2 * y_ref.shape[2]
        for j in range(y_ref.shape[0]):
            sl = slice(j * cols, (j + 1) * cols)
            y_ref[j] = _pack_piece(_dot(act, w16[2, :, sl]) + bd_ref[0, :, sl])


def _experts(blk_e, end_blk, xs, wg, bg, wu, bu, wd, bd):
    pieces, n_rows, sub = xs.shape
    d, de = wg.shape[1], wg.shape[2]
    assert d == de
    r = MOE_ROW_BLOCK
    bspec = lambda w: pl.BlockSpec((1, 1, w), lambda i, be, *_: (be[i], 0, 0))
    hbm = pl.BlockSpec(memory_space=pl.ANY)
    grid_spec = pltpu.PrefetchScalarGridSpec(
        num_scalar_prefetch=2,
        grid=(n_rows // r,),
        in_specs=[pl.BlockSpec((pieces, r, sub), lambda i, *_: (0, i, 0)),
                  hbm, bspec(de), hbm, bspec(de), hbm, bspec(d)],
        out_specs=pl.BlockSpec((pieces, r, sub), lambda i, *_: (0, i, 0)),
        scratch_shapes=[pltpu.VMEM((2, 3, d, de), F32), pltpu.VMEM((3, d, de), BF16),
                        pltpu.SemaphoreType.DMA((2, 3)), pltpu.SMEM((1,), jnp.int32)],
    )
    return pl.pallas_call(
        _expert_body,
        grid_spec=grid_spec,
        out_shape=jax.ShapeDtypeStruct((pieces, n_rows, sub), jnp.int32),
        compiler_params=_cparams(("arbitrary",), 56),
        name="moe_experts",
    )(blk_e, end_blk, xs, wg, bg, wu, bu, wd, bd)


SC_WINDOW = 128
SC_SUBROW = 256


def _sc_mesh():
    return plsc.VectorSubcoreMesh(core_axis_name="c", subcore_axis_name="s")


def _sc_dispatch(h2, dest_rows, n_rows):
    n, d = h2.shape

    @functools.partial(pl.kernel, out_type=jax.ShapeDtypeStruct((n_rows, d), h2.dtype), mesh=_sc_mesh())
    def dispatch(x_hbm, *refs):
        idx_hbm, o_hbm = refs[:TOP_K], refs[TOP_K]

        def body(x_vmem, *idx_vmem):
            for iv in idx_vmem:
                pltpu.sync_copy(x_vmem, o_hbm.at[iv.at[0]])

        pltpu.emit_pipeline(
            body,
            grid=(n // SC_WINDOW,),
            in_specs=[pl.BlockSpec((SC_WINDOW, d), lambda i: (i, 0))]
                     + [pl.BlockSpec((1, SC_WINDOW), lambda i: (0, i))] * TOP_K,
            out_specs=[],
            core_axis_name=("c", "s"),
            dimension_semantics=(pltpu.PARALLEL,),
        )(x_hbm, *idx_hbm)

    return dispatch(h2, *dest_rows)


def _sc_gather(table, idx):
    _, d = table.shape
    m = idx.shape[1]

    @functools.partial(pl.kernel, out_type=jax.ShapeDtypeStruct((m, d), table.dtype), mesh=_sc_mesh())
    def gather(t_hbm, i_hbm, o_hbm):
        def body(i_vmem, o_vmem):
            pltpu.sync_copy(t_hbm.at[i_vmem.at[0]], o_vmem)

        pltpu.emit_pipeline(
            body,
            grid=(m // SC_WINDOW,),
            in_specs=[pl.BlockSpec((1, SC_WINDOW), lambda i: (0, i))],
            out_specs=[pl.BlockSpec((SC_WINDOW, d), lambda i: (i, 0))],
            core_axis_name=("c", "s"),
            dimension_semantics=(pltpu.PARALLEL,),
        )(i_hbm, o_hbm)

    return gather(table, idx)


def _combine_body(x1_ref, y_ref, gate_ref, o_ref):
    acc = x1_ref[...]
    for k in range(TOP_K):
        acc = acc + gate_ref[:, k:k + 1] * _join_pieces(y_ref.at[k])
    o_ref[...] = acc


def _combine(x1, y4, gates, tm=512):
    n, d = x1.shape
    pieces, sub = y4.shape[1], y4.shape[3]
    row = lambda w: pl.BlockSpec((tm, w), lambda i: (i, 0))
    return pl.pallas_call(
        _combine_body,
        grid=(n // tm,),
        in_specs=[row(d), pl.BlockSpec((TOP_K, pieces, tm, sub), lambda i: (0, 0, i, 0)), row(LANES)],
        out_specs=row(d),
        out_shape=jax.ShapeDtypeStruct((n, d), F32),
        compiler_params=_cparams(("parallel",), 48),
        name="moe_combine",
    )(x1, y4, gates)


def _pad_lanes(a, width=LANES):
    return jnp.pad(a, ((0, 0), (0, width - a.shape[1])))


def _layer(x, attn_norm_g, w_in, q_g, kc_g, ks_g, kw_g, ck_pos, ck_w1, ck_b1, ck_w2, ck_b2,
           cv_pos, cv_w1, cv_b1, cv_w2, cv_b2, nsa_out_g, conv_w, a_log, dt_bias, gdn_out_g, w_out,
           ffn_g, router_w, router_b, e_wg, e_bg, e_wu, e_bu, e_wd, e_bd):
    b, t, d = x.shape
    n = b * t
    x2 = x.reshape(n, d)

    o = np.cumsum([0, NSA_W] + [NSA_GROUPS * NSA_DH] * 6 + [3 * NSA_HEADS, 3 * GDN_W, GDN_W, GDN_HEADS, GDN_HEADS])
    wq_t = w_in[:, o[0]:o[1]].T.astype(BF16)
    qg_col = jnp.tile(q_g * (NSA_DH ** -0.5 * np.log2(np.e)), NSA_HEADS).reshape(NSA_W, 1)
    wkv = jnp.concatenate([w_in[:, o[1]:o[4]], w_in[:, o[5]:o[6]]], axis=1).astype(BF16)
    kg = jnp.concatenate([ks_g, ks_g, kw_g, kw_g]).reshape(1, 2 * LANES)
    wv_t = jnp.concatenate([w_in[:, o[4]:o[5]], w_in[:, o[6]:o[7]]], axis=1).T.reshape(2 * NSA_GROUPS, NSA_DH, d)
    wv_t = jnp.pad(wv_t, ((0, 0), (0, LANES - NSA_DH), (0, 0))).reshape(2 * NSA_GROUPS * LANES, d).astype(BF16)
    vone = jnp.asarray((np.arange(2 * NSA_GROUPS * LANES) % LANES == NSA_DH).astype(np.float32)[:, None])
    wg_t = w_in[:, o[7]:o[8]].T.reshape(NSA_GROUPS, NSA_HPG * 3, d)
    wg_t = jnp.pad(wg_t, ((0, 0), (0, GATE_ROWS - NSA_HPG * 3), (0, 0))).reshape(NSA_GROUPS * GATE_ROWS, d)
    wg_t = wg_t.astype(BF16)
    wab = _pad_lanes(w_in[:, o[10]:o[12]]).astype(BF16)
    wqkv = w_in[:, o[8]:o[9]].astype(BF16)
    wz = w_in[:, o[9]:o[10]].astype(BF16)

    tm = min(512, t)
    oqt, okn, xflat, ovt, ogt, oqkv, oz, oab = _in_proj(x2, attn_norm_g.reshape(1, d), wq_t, wkv, wv_t, wg_t, wqkv,
                                                        wz, wab, qg_col, kg, vone, conv_w, t // tm, tm)

    nch = t // CMP_STRIDE
    n_cmp = (t - CMP_BLOCK) // CMP_STRIDE + 1
    half = CMP_STRIDE * NSA_DH
    pos = jnp.stack([ck_pos, cv_pos]).reshape(2, 2, 1, half)
    w1 = jnp.stack([ck_w1, cv_w1]).reshape(2, 2, half, CMP_HIDDEN).astype(BF16)
    b1 = jnp.stack([ck_b1, cv_b1]).reshape(2, 1, CMP_HIDDEN)
    w2 = jnp.stack([ck_w2, cv_w2]).astype(BF16)
    b2 = jnp.stack([ck_b2, cv_b2]).reshape(2, 1, NSA_DH)
    w2t = jnp.stack([ck_w2.T, cv_w2.T]).astype(BF16)
    b2t = jnp.stack([ck_b2, cv_b2]).reshape(2, NSA_DH, 1)
    kc, vct = _compress(xflat, pos, w1, b1, w2, b2, w2t, b2t, kc_g.reshape(1, NSA_DH), n_cmp)

    n_slc = t // SLC_BLOCK
    n_top = min(SLC_TOPK, n_slc)
    assert n_top > 3
    nblk = max(n_slc, LANES)
    kt = min(256, t // 4)
    assert (t // kt) % 4 == 0
    ci = np.arange(nch)[None, :] * CMP_STRIDE
    sj = np.arange(nblk)[:, None] * SLC_BLOCK
    overlap = ((ci < sj + SLC_BLOCK) & (ci + CMP_BLOCK > sj) & (np.arange(nch)[None, :] < n_cmp)
               & (np.arange(nblk)[:, None] < n_slc))
    expand_t = (np.arange(t)[:, None] // SLC_BLOCK) == np.arange(nblk)[None, :]
    o_nsa_t = _nsa_attention(oqt, ogt, kc, vct, okn.reshape(b, t, -1), ovt, jnp.asarray(expand_t, BF16),
                             jnp.asarray(overlap, BF16), b, t, n_top, kt)

    alog_row = _pad_lanes(a_log.reshape(1, GDN_HEADS))
    dtb_row = _pad_lanes(dt_bias.reshape(1, GDN_HEADS))
    o_gdn = _gdn(oqkv.reshape(b, t, -1), oz.reshape(b, t, -1), oab.reshape(b, t, -1),
                 alog_row, dtb_row, gdn_out_g.reshape(1, GDN_DH))

    wr = _pad_lanes(router_w).astype(BF16)
    br = _pad_lanes(router_b.reshape(1, N_EXPERTS))
    x1, h2, gates, route, counts = _out_proj(
        o_nsa_t, o_gdn.reshape(n, GDN_W), x2, nsa_out_g.reshape(NSA_W, 1),
        w_out.astype(BF16), ffn_g.reshape(1, d), wr, br)

    r = MOE_ROW_BLOCK
    nk = n * TOP_K
    counts = counts[:, 0]
    pcounts = (counts + r - 1) // r * r
    pends = jnp.cumsum(pcounts)
    pstarts = pends - pcounts
    n_rows = (nk + r - 1) // r * r + N_EXPERTS * r
    n_blocks = n_rows // r
    blk_start = jnp.arange(n_blocks, dtype=jnp.int32)[:, None] * r
    blk_e = jnp.minimum(jnp.sum(pends[None, :] <= blk_start, axis=1), N_EXPERTS - 1).astype(jnp.int32)
    n_used = (pends[-1] // r).astype(jnp.int32)
    blk_e = jnp.concatenate([blk_e, n_used[None]])
    end_blk = (pends // r).astype(jnp.int32)
    pieces = d // 2 // SC_SUBROW
    dest_p = _dest_rows(pstarts.astype(jnp.int32), route, n_rows, pieces).reshape(TOP_K, pieces, n)
    xs = _sc_dispatch(h2.reshape(pieces * n, SC_SUBROW), [dest_p[k].reshape(1, pieces * n) for k in range(TOP_K)],
                      pieces * n_rows)
    ys = _experts(blk_e, end_blk, xs.reshape(pieces, n_rows, SC_SUBROW), e_wg, e_bg.reshape(N_EXPERTS, 1, -1),
                  e_wu, e_bu.reshape(N_EXPERTS, 1, -1), e_wd, e_bd.reshape(N_EXPERTS, 1, -1))
    y4 = _sc_gather(ys.reshape(pieces * n_rows, SC_SUBROW), dest_p.reshape(1, nk * pieces))
    return _combine(x1, y4.reshape(TOP_K, pieces, n, SC_SUBROW), gates).reshape(b, t, d)


def kernel(x, attn_norm_g, w_in, nsa_q_norm_g, nsa_kc_norm_g, nsa_ks_norm_g, nsa_kw_norm_g, cmp_k_pos, cmp_k_w1, cmp_k_b1, cmp_k_w2, cmp_k_b2, cmp_v_pos, cmp_v_w1, cmp_v_b1, cmp_v_w2, cmp_v_b2, nsa_out_norm_g, gdn_conv_w, gdn_a_log, gdn_dt_bias, gdn_out_norm_g, w_out, ffn_norm_g, router_w, router_b, exp_w_gate, exp_b_gate, exp_w_up, exp_b_up, exp_w_down, exp_b_down):
    params = (attn_norm_g, w_in, nsa_q_norm_g, nsa_kc_norm_g, nsa_ks_norm_g, nsa_kw_norm_g,
              cmp_k_pos, cmp_k_w1, cmp_k_b1, cmp_k_w2, cmp_k_b2, cmp_v_pos, cmp_v_w1, cmp_v_b1, cmp_v_w2, cmp_v_b2,
              nsa_out_norm_g, gdn_conv_w, gdn_a_log, gdn_dt_bias, gdn_out_norm_g, w_out, ffn_norm_g,
              router_w, router_b, exp_w_gate, exp_b_gate, exp_w_up, exp_b_up, exp_w_down, exp_b_down)
    for l in range(attn_norm_g.shape[0]):
        x = _layer(x, *(p[l] for p in params))
    return x
```

```python
import functools

import jax
import jax.numpy as jnp
import numpy as np
from jax import lax
from jax.experimental import pallas as pl
from jax.experimental.pallas import tpu as pltpu
from jax.experimental.pallas import tpu_sc as plsc

F32 = jnp.float32
BF16 = jnp.bfloat16

EPS = 1e-6
NEG = -1e30
MASKED = -2.0 ** 100

NSA_HEADS = 8
NSA_GROUPS = 2
NSA_HPG = 4
NSA_DH = 64
CMP_BLOCK = 32
CMP_STRIDE = 16
CMP_HIDDEN = 256
SLC_BLOCK = 64
SLC_TOPK = 16
WINDOW = 512
NSA_Q = 256
GDN_HEADS = 4
GDN_DH = 128
GDN_CHUNK = 64
N_EXPERTS = 32
TOP_K = 4
SWIGLU_LIMIT = 7.0
SWIGLU_ALPHA = 1.702
MOE_ROW_BLOCK = 256

LANES = 128
GATE_ROWS = 16
FLASH_BODY_TILES = (8, 4)
NSA_W = NSA_HEADS * NSA_DH
GDN_W = GDN_HEADS * GDN_DH

_NT = (((1,), (1,)), ((), ()))
_TN = (((0,), (0,)), ((), ()))


def _cparams(sem, vmem_mb):
    return pltpu.CompilerParams(dimension_semantics=sem, vmem_limit_bytes=vmem_mb * 1024 * 1024)


def _dot(a, b):
    return jnp.dot(a, b, preferred_element_type=F32)


def _dot_nt(a, b):
    return lax.dot_general(a, b, _NT, preferred_element_type=F32)


def _dot_tn(a, b):
    return lax.dot_general(a, b, _TN, preferred_element_type=F32)


def _pack_piece(block):
    words = block.shape[1] // 2
    hi = lax.bitcast_convert_type(block[:, :words].astype(BF16).astype(F32), jnp.uint32)
    lo = lax.bitcast_convert_type(block[:, words:].astype(BF16).astype(F32), jnp.uint32)
    return lax.bitcast_convert_type(hi | (lo >> 16), jnp.int32)


def _store_pieces(ref, val):
    cols = 2 * ref.shape[2]
    for j in range(ref.shape[0]):
        ref[j] = _pack_piece(val[:, j * cols:(j + 1) * cols])


def _join_pieces(ref):
    out = []
    for j in range(ref.shape[0]):
        words = lax.bitcast_convert_type(ref[j], jnp.uint32)
        out.append(lax.bitcast_convert_type(words & jnp.uint32(0xFFFF0000), F32))
        out.append(lax.bitcast_convert_type(words << 16, F32))
    return jnp.concatenate(out, axis=1)


def _inproj_body(x_ref, g_ref, wqt_ref, wkv_ref, wvt_ref, wgt_ref, wqkv_ref, wz_ref, wab_ref, qg_ref, kg_ref,
                 vone_ref, cw_ref, oqt_ref, okn_ref, ocf_ref, ovt_ref, ogt_ref, oqkv_ref, oz_ref, oab_ref, ybuf, cbuf,
                 *, tiles_per_seq):
    tm = x_ref.shape[0]

    halo = ybuf.shape[0] - tm
    first = pl.program_id(0) % tiles_per_seq == 0

    @pl.when(first)
    def _():
        ybuf[0:halo, :] = jnp.zeros((halo, ybuf.shape[1]), F32)

    @pl.when(jnp.logical_not(first))
    def _():
        ybuf[0:halo, :] = ybuf[tm:tm + halo, :]

    x = x_ref[...]
    h = (x * lax.rsqrt(jnp.mean(x * x, axis=-1, keepdims=True) + EPS) * g_ref[...]).astype(BF16)
    ybuf[halo:halo + tm, :] = _dot(h, wqkv_ref[...])

    yq = _dot_nt(wqt_ref[...], h)
    for s in range(NSA_HEADS):
        sl = slice(s * NSA_DH, (s + 1) * NSA_DH)
        ys = yq[sl, :]
        ms = jnp.sum(ys * ys, axis=0, keepdims=True) * (1.0 / NSA_DH)
        oqt_ref[sl, :] = (ys * lax.rsqrt(ms + EPS) * qg_ref[sl, :]).astype(BF16)

    ykv = _dot(h, wkv_ref[...])
    lane = lax.broadcasted_iota(jnp.int32, (tm, LANES), 1)
    low = lane < NSA_DH
    for s in range(2):
        sl = slice(s * LANES, (s + 1) * LANES)
        ys = ykv[:, (2 + s) * LANES:(3 + s) * LANES]
        y2 = ys * ys
        s0 = jnp.sum(jnp.where(low, y2, 0.0), axis=-1, keepdims=True)
        s1 = jnp.sum(jnp.where(low, 0.0, y2), axis=-1, keepdims=True)
        ms = jnp.where(low, s0, s1) * (1.0 / NSA_DH)
        okn_ref[:, sl] = (ys * lax.rsqrt(ms + EPS) * kg_ref[:, sl]).astype(BF16)

    chunks = tm // CMP_STRIDE
    for br in range(2):
        cbuf[br] = ykv[:, br * LANES:(br + 1) * LANES]
        taken = [cbuf[br, pl.ds(l, chunks, stride=CMP_STRIDE), :] for l in range(CMP_STRIDE)]
        for grp in range(NSA_GROUPS):
            flat = jnp.concatenate([r[:, grp * NSA_DH:(grp + 1) * NSA_DH] for r in taken], axis=1)
            ocf_ref[0, br, grp] = flat.astype(BF16)

    ovt_ref[...] = (_dot_nt(wvt_ref[...], h) + vone_ref[...]).astype(BF16)
    ogt_ref[...] = _dot_nt(wgt_ref[...], h)
    oz_ref[...] = _dot(h, wz_ref[...]).astype(BF16)
    oab_ref[...] = _dot(h, wab_ref[...])

    taps = cw_ref.shape[0]
    y = cw_ref[0:1, :] * ybuf[pl.ds(halo - taps + 1, tm), :]
    for k in range(1, taps):
        y = y + cw_ref[k:k + 1, :] * ybuf[pl.ds(halo - taps + 1 + k, tm), :]
    hy = 0.5 * y
    y = hy + hy * jnp.tanh(hy)
    for s in range(3 * GDN_HEADS):
        sl = slice(s * GDN_DH, (s + 1) * GDN_DH)
        ys = y[:, sl]
        if s < 2 * GDN_HEADS:
            scale = GDN_DH ** -0.5 if s < GDN_HEADS else 1.0
            ys = ys * (lax.rsqrt(jnp.sum(ys * ys, axis=-1, keepdims=True) + EPS) * scale)
        oqkv_ref[:, sl] = ys.astype(BF16)


def _in_proj(x2, g, wqt, wkv, wvt, wgt, wqkv, wz, wab, qg, kg, vone, conv_w, tiles_per_seq, tm):
    n, d = x2.shape
    chunks, flat = tm // CMP_STRIDE, CMP_STRIDE * NSA_DH
    full = lambda a: pl.BlockSpec(a.shape, lambda i: (0,) * a.ndim)
    row = lambda w: pl.BlockSpec((tm, w), lambda i: (i, 0))
    colb = lambda r: pl.BlockSpec((r, tm), lambda i: (0, i))
    return pl.pallas_call(
        functools.partial(_inproj_body, tiles_per_seq=tiles_per_seq),
        grid=(n // tm,),
        in_specs=[row(d)] + [full(a) for a in (g, wqt, wkv, wvt, wgt, wqkv, wz, wab, qg, kg, vone, conv_w)],
        out_specs=[colb(wqt.shape[0]), row(2 * LANES),
                   pl.BlockSpec((1, 2, NSA_GROUPS, chunks, flat), lambda i: (i // tiles_per_seq, 0, 0, i % tiles_per_seq, 0)),
                   colb(wvt.shape[0]), colb(wgt.shape[0]), row(wqkv.shape[1]), row(wz.shape[1]), row(wab.shape[1])],
        out_shape=[jax.ShapeDtypeStruct((wqt.shape[0], n), BF16), jax.ShapeDtypeStruct((n, 2 * LANES), BF16),
                   jax.ShapeDtypeStruct((n // (tm * tiles_per_seq), 2, NSA_GROUPS, chunks * tiles_per_seq, flat), BF16),
                   jax.ShapeDtypeStruct((wvt.shape[0], n), BF16), jax.ShapeDtypeStruct((wgt.shape[0], n), F32),
                   jax.ShapeDtypeStruct((n, wqkv.shape[1]), BF16), jax.ShapeDtypeStruct((n, wz.shape[1]), BF16),
                   jax.ShapeDtypeStruct((n, wab.shape[1]), F32)],
        scratch_shapes=[pltpu.VMEM((tm + 8, wqkv.shape[1]), F32), pltpu.VMEM((2, tm, LANES), F32)],
        compiler_params=_cparams(("arbitrary",), 56),
        name="in_proj",
    )(x2, g, wqt, wkv, wvt, wgt, wqkv, wz, wab, qg, kg, vone, conv_w)


def _compress_body(x_ref, pos_ref, w1_ref, b1_ref, w2_ref, b2_ref, w2t_ref, b2t_ref, g_ref, ok_ref, ovt_ref,
                   *, n_cmp):
    is_key = pl.program_id(1) == 0
    nch = x_ref.shape[3]
    hids = []
    for grp in range(NSA_GROUPS):
        x = x_ref[0, 0, grp].astype(F32)
        xa = (x + pos_ref[0, 0]).astype(BF16)
        xb = (x + pos_ref[0, 1]).astype(BF16)
        a = _dot(xa, w1_ref[0, 0])
        b = _dot(xb, w1_ref[0, 1])
        b_next = pltpu.roll(b, nch - 1, 0)
        hids.append(jax.nn.gelu(a + b_next + b1_ref[0]).astype(BF16))

    @pl.when(is_key)
    def _():
        row = lax.broadcasted_iota(jnp.int32, (nch, NSA_DH), 0)
        outs = []
        for grp in range(NSA_GROUPS):
            out = _dot(hids[grp], w2_ref[0]) + b2_ref[0]
            out = out * lax.rsqrt(jnp.mean(out * out, axis=-1, keepdims=True) + EPS) * g_ref[...]
            outs.append(jnp.where(row < n_cmp, out, 0.0))
        ok_ref[0] = jnp.concatenate(outs, axis=-1).astype(BF16)

    @pl.when(jnp.logical_not(is_key))
    def _():
        col = lax.broadcasted_iota(jnp.int32, (NSA_DH, nch), 1)
        outs = []
        for grp in range(NSA_GROUPS):
            out = _dot_nt(w2t_ref[0], hids[grp]) + b2t_ref[0]
            outs.append(jnp.where(col < n_cmp, out, 0.0))
        ovt_ref[0] = jnp.concatenate(outs, axis=0).astype(BF16)


def _compress(xflat, pos, w1, b1, w2, b2, w2t, b2t, kc_g, n_cmp):
    b, _, _, nch, flat = xflat.shape
    return pl.pallas_call(
        functools.partial(_compress_body, n_cmp=n_cmp),
        grid=(b, 2),
        in_specs=[
            pl.BlockSpec((1, 1, NSA_GROUPS, nch, flat), lambda i, j: (i, j, 0, 0, 0)),
            pl.BlockSpec((1, 2, 1, flat), lambda i, j: (j, 0, 0, 0)),
            pl.BlockSpec((1, 2, flat, CMP_HIDDEN), lambda i, j: (j, 0, 0, 0)),
            pl.BlockSpec((1, 1, CMP_HIDDEN), lambda i, j: (j, 0, 0)),
            pl.BlockSpec((1, CMP_HIDDEN, NSA_DH), lambda i, j: (j, 0, 0)),
            pl.BlockSpec((1, 1, NSA_DH), lambda i, j: (j, 0, 0)),
            pl.BlockSpec((1, NSA_DH, CMP_HIDDEN), lambda i, j: (j, 0, 0)),
            pl.BlockSpec((1, NSA_DH, 1), lambda i, j: (j, 0, 0)),
            pl.BlockSpec((1, NSA_DH), lambda i, j: (0, 0)),
        ],
        out_specs=[pl.BlockSpec((1, nch, LANES), lambda i, j: (i, 0, 0)),
                   pl.BlockSpec((1, LANES, nch), lambda i, j: (i, 0, 0))],
        out_shape=[jax.ShapeDtypeStruct((b, nch, LANES), BF16), jax.ShapeDtypeStruct((b, LANES, nch), BF16)],
        compiler_params=_cparams(("parallel", "arbitrary"), 32),
        name="nsa_compress",
    )(xflat, pos, w1, b1, w2, b2, w2t, b2t, kc_g)


def _tile_heads(a):
    return jnp.concatenate([a] * NSA_HPG, axis=1)


def _nsa_body(qt_ref, gt_ref, kc_ref, vct_ref, ks_ref, kw_ref, vst_ref, vwt_ref, et_ref, ov_ref, cpat_ref, wpat_ref,
              dpat_ref, o_ref, acc_sc, s_sc, *, n_top, kt):
    grp = pl.program_id(1)
    s0 = pl.program_id(2) * NSA_Q
    nch = kc_ref.shape[1]
    nblk = ov_ref.shape[0]

    qh = jnp.concatenate([qt_ref[h * NSA_DH:(h + 1) * NSA_DH, :] for h in range(NSA_HPG)], axis=1)
    zq = jnp.zeros_like(qh)
    qt = jnp.where(grp == 0, jnp.concatenate([qh, zq], axis=0), jnp.concatenate([zq, qh], axis=0))
    t_row = s0 + lax.broadcasted_iota(jnp.int32, (1, NSA_Q), 1)

    cbias = cpat_ref[pl.ds(pl.multiple_of(nch - s0 // CMP_STRIDE, CMP_STRIDE), nch), :]
    sc = _dot(kc_ref[0], qt) + _tile_heads(cbias)
    wlen = WINDOW + NSA_Q
    w0 = pl.multiple_of(jnp.maximum(s0 - WINDOW, 0), NSA_Q)
    wbias = wpat_ref[pl.ds(pl.multiple_of(w0 - s0 + WINDOW, NSA_Q), wlen), :]
    sw = _dot(kw_ref[0, pl.ds(w0, wlen), :], qt) + _tile_heads(wbias)
    pc = jnp.exp2(sc - jnp.max(sc, axis=0, keepdims=True)).astype(BF16)
    stacked = jnp.concatenate([vct_ref[0], ov_ref[...], jnp.ones((8, nch), BF16)], axis=0)
    res = _dot(stacked, pc)
    inv = jnp.where(_tile_heads(t_row >= CMP_BLOCK - 1), 1.0 / jnp.maximum(res[LANES + nblk:LANES + nblk + 1], 1e-30),
                    0.0)
    oc = res[:LANES] * inv
    imp4 = res[LANES:LANES + nblk] * inv
    imp = (imp4[:, 0:NSA_Q] + imp4[:, NSA_Q:2 * NSA_Q] + imp4[:, 2 * NSA_Q:3 * NSA_Q]
           + imp4[:, 3 * NSA_Q:4 * NSA_Q])
    blk = lax.broadcasted_iota(jnp.int32, (nblk, NSA_Q), 0)
    cur = t_row // SLC_BLOCK
    imp = jnp.where(blk * SLC_BLOCK > t_row, NEG, imp)
    imp = jnp.where((blk == 0) | (blk == cur) | (blk == cur - 1), -jnp.inf, imp)
    rounds_left = n_top - 3

    def pick_rounds(v, rounds):
        for _ in range(rounds):
            mx = jnp.max(v, axis=0, keepdims=True)
            first = jnp.min(jnp.where(v == mx, blk, nblk), axis=0, keepdims=True)
            v = jnp.where(blk == first, -jnp.inf, v)
        return v

    quarter = rounds_left // 4
    picked = pick_rounds(imp, quarter)

    picked = pick_rounds(picked, quarter)
    pw = jnp.exp2(sw - jnp.max(sw, axis=0, keepdims=True)).astype(BF16)
    picked = pick_rounds(picked, quarter)
    ow = _dot(vwt_ref[:, pl.ds(w0, wlen)], pw)
    ow = ow[:NSA_DH] / ow[NSA_DH:NSA_DH + 1]

    picked = pick_rounds(picked, rounds_left - 3 * quarter)
    chosen = picked == -jnp.inf

    selb = jnp.where(chosen & (blk * SLC_BLOCK < s0), 0.0, MASKED).astype(BF16)
    rhs = jnp.concatenate([qt, _tile_heads(selb)], axis=0)
    last_tile = ks_ref.shape[1] // kt - 1

    def scores(idx, slot):
        k0 = pl.multiple_of(jnp.minimum(idx, last_tile) * kt, kt)
        lhs = jnp.concatenate([ks_ref[0, pl.ds(k0, kt), :], et_ref[pl.ds(k0, kt), :]], axis=1)
        s_sc[slot] = _dot(lhs, rhs)

    scores(0, 0)
    scores(1, 1)

    d0 = pl.multiple_of(s0, NSA_Q)
    selb_d = jnp.where(chosen & (blk <= cur), 0.0, MASKED).astype(BF16)
    sd = (_dot(jnp.concatenate([ks_ref[0, pl.ds(d0, NSA_Q), :], et_ref[pl.ds(d0, NSA_Q), :]], axis=1),
               jnp.concatenate([qt, _tile_heads(selb_d)], axis=0))
          + _tile_heads(dpat_ref[...]))
    m_diag = jnp.max(sd, axis=0, keepdims=True)
    acc_sc[0] = _dot(vst_ref[:, pl.ds(d0, NSA_Q)], jnp.exp2(sd - m_diag).astype(BF16))
    acc_sc[1] = jnp.zeros(acc_sc.shape[1:], F32)

    def update(idx, slot, m_old, acc_ref):
        k0 = pl.multiple_of(idx * kt, kt)
        m_new = jnp.maximum(m_old, jnp.max(s_sc[slot], axis=0, keepdims=True))
        p = jnp.exp2(s_sc[slot] - m_new).astype(BF16)
        acc_ref[...] = jnp.exp2(m_old - m_new) * acc_ref[...] + _dot(vst_ref[:, pl.ds(k0, kt)], p)
        return m_new

    def tile_group(first, carry, count):
        ms = list(carry)
        for t in range(count):
            scores(first + t + 2, (t + 2) % 4)
            ms[t % 2] = update(first + t, t % 4, ms[t % 2], acc_sc.at[t % 2])
        return tuple(ms)

    n_tiles = (s0 + kt - 1) // kt
    carry = (m_diag, jnp.full((1, NSA_HPG * NSA_Q), NEG, F32))
    done = 0
    for size in FLASH_BODY_TILES:
        left = n_tiles - done
        groups = (left + size - 1) // size if size == FLASH_BODY_TILES[-1] else left // size
        carry = lax.fori_loop(0, groups, lambda j, c, done=done, size=size: tile_group(done + size * j, c, size),
                              carry)
        done = done + size * groups
    m0, m1 = carry
    m_fin = jnp.maximum(m0, m1)
    acc = acc_sc[0] * jnp.exp2(m0 - m_fin) + acc_sc[1] * jnp.exp2(m1 - m_fin)
    osl = acc[:NSA_DH] / acc[NSA_DH:NSA_DH + 1]

    oc = jnp.where(grp == 0, oc[:NSA_DH], oc[NSA_DH:])
    gts = jax.nn.sigmoid(gt_ref[...])
    for h in range(NSA_HPG):
        cols = slice(h * NSA_Q, (h + 1) * NSA_Q)
        o_ref[h * NSA_DH:(h + 1) * NSA_DH, :] = (
            gts[3 * h:3 * h + 1, :] * oc[:, cols] + gts[3 * h + 1:3 * h + 2, :] * osl[:, cols]
            + gts[3 * h + 2:3 * h + 3, :] * ow[:, cols])


def _mask_pattern(valid):
    return jnp.asarray(np.where(valid, 0.0, NEG), F32)


def _nsa_attention(qt, gt, kc, vct, okv, vt, expand_t, overlap, b, t, n_top, kt):
    nch = kc.shape[1]
    nq = t // NSA_Q
    n = b * t
    ql = np.arange(NSA_Q)[None, :]
    rc = np.arange(2 * nch)[:, None] - nch
    cpat = _mask_pattern(rc * CMP_STRIDE + CMP_BLOCK - 1 <= ql)
    rw = np.arange(2 * WINDOW + NSA_Q)[:, None] - WINDOW
    wpat = _mask_pattern((rw <= ql) & (rw > ql - WINDOW))
    dpat = _mask_pattern(np.arange(NSA_Q)[:, None] <= ql)
    full2 = lambda a: pl.BlockSpec(a.shape, lambda bi, g, i: (0, 0))
    return pl.pallas_call(
        functools.partial(_nsa_body, n_top=n_top, kt=kt),
        grid=(b, NSA_GROUPS, nq),
        in_specs=[
            pl.BlockSpec((NSA_HPG * NSA_DH, NSA_Q), lambda bi, g, i: (g, bi * nq + i)),
            pl.BlockSpec((GATE_ROWS, NSA_Q), lambda bi, g, i: (g, bi * nq + i)),
            pl.BlockSpec((1, nch, LANES), lambda bi, g, i: (bi, 0, 0)),
            pl.BlockSpec((1, LANES, nch), lambda bi, g, i: (bi, 0, 0)),
            pl.BlockSpec((1, t, LANES), lambda bi, g, i: (bi, 0, 0)),
            pl.BlockSpec((1, t, LANES), lambda bi, g, i: (bi, 0, 1)),
            pl.BlockSpec((LANES, t), lambda bi, g, i: (g, bi)),
            pl.BlockSpec((LANES, t), lambda bi, g, i: (NSA_GROUPS + g, bi)),
            full2(expand_t), full2(overlap), full2(cpat), full2(wpat), full2(dpat),
        ],
        out_specs=pl.BlockSpec((NSA_HPG * NSA_DH, NSA_Q), lambda bi, g, i: (g, bi * nq + i)),
        out_shape=jax.ShapeDtypeStruct((NSA_W, n), F32),
        scratch_shapes=[pltpu.VMEM((2, LANES, NSA_HPG * NSA_Q), F32),
                        pltpu.VMEM((4, kt, NSA_HPG * NSA_Q), F32)],
        compiler_params=_cparams(("parallel", "parallel", "arbitrary"), 56),
        name="nsa_attention",
    )(qt, gt, kc, vct, okv, okv, vt, vt, expand_t, overlap, cpat, wpat, dpat)


def _split_bf16(a):
    hi = a.astype(BF16)
    return hi, (a - hi.astype(F32)).astype(BF16)


def _unit_lower_inverses(lmats):
    c = lmats[0].shape[0]
    r = lax.broadcasted_iota(jnp.int32, (c, c), 0)
    col = lax.broadcasted_iota(jnp.int32, (c, c), 1)
    eye = jnp.where(r == col, 1.0, 0.0)
    xs = [eye - l for l in lmats]
    ps = []
    for l in lmats:
        l16 = l.astype(BF16)
        ps.append(_dot(l16, l16))
    steps = int(np.log2(c)) - 1
    for s in range(steps):
        last = s + 1 == steps
        for i in range(len(lmats)):
            rhs = ps[i].astype(BF16)
            if last:
                xs[i] = xs[i] + _dot(xs[i].astype(BF16), rhs)
            else:
                both = _dot(jnp.concatenate([xs[i], ps[i]], axis=0).astype(BF16), rhs)
                xs[i] = xs[i] + both[:c]
                ps[i] = both[c:]
    return xs


def _gdn_body(x_ref, z_ref, ab_ref, alog_ref, dtb_ref, og_ref, o_ref, s_sc, u_sc, wq_sc, kq_sc, gl_sc, *, ct):
    nb = x_ref.shape[0]
    ch = GDN_CHUNK
    n_units = (ct // ch) * nb * GDN_HEADS

    @pl.when(pl.program_id(0) == 0)
    def _():
        s_sc[...] = jnp.zeros(s_sc.shape, F32)
        u_sc[...] = jnp.zeros(u_sc.shape, F32)
        wq_sc[...] = jnp.zeros(wq_sc.shape, BF16)
        kq_sc[...] = jnp.zeros(kq_sc.shape, BF16)
        gl_sc[...] = jnp.ones(gl_sc.shape, F32)

    def scan_unit(i):
        ci, bi, h = i // (nb * GDN_HEADS), (i // GDN_HEADS) % nb, i % GDN_HEADS
        rows = slice(ci * ch, (ci + 1) * ch)
        hs = slice(h * GDN_DH, (h + 1) * GDN_DH)
        s_old = s_sc[bi * GDN_HEADS + h]
        from_state = _dot(wq_sc[i], s_old.astype(BF16))
        v_new = (u_sc[i] - from_state[:ch]).astype(BF16)
        from_v = _dot(kq_sc[i], v_new)
        s_sc[bi * GDN_HEADS + h] = s_old * gl_sc[i, 0:1, 0:1] + from_v[:GDN_DH]
        o = from_state[ch:] + from_v[GDN_DH:]
        on = o * lax.rsqrt(jnp.mean(o * o, axis=-1, keepdims=True) + EPS) * og_ref[...]
        zh = z_ref[bi, rows, hs].astype(F32)
        o_ref[bi, rows, hs] = (on * (zh * jax.nn.sigmoid(zh))).astype(BF16)

    r = lax.broadcasted_iota(jnp.int32, (ch, ch), 0)
    col = lax.broadcasted_iota(jnp.int32, (ch, ch), 1)
    incl = r >= col
    strict = r > col
    tril16 = jnp.concatenate([jnp.where(incl, 1.0, 0.0).astype(BF16)] * 3, axis=1)

    units, xs, ps = [], [], []

    def prepare(ci, bi):
        def run():
            rows = slice(ci * ch, (ci + 1) * ch)
            ab = ab_ref[bi, rows, :]
            g_all = -jnp.exp(alog_ref[...]) * jax.nn.softplus(ab + dtb_ref[...])
            beta_all = jax.nn.sigmoid(ab)
            g_hi, g_lo = _split_bf16(g_all)
            g_lo2 = (g_all - g_hi.astype(F32) - g_lo.astype(F32)).astype(BF16)
            gc_all = _dot(tril16, jnp.concatenate([g_hi, g_lo, g_lo2], axis=0))
            gc_t = gc_all.T
            for h in range(GDN_HEADS):
                hs = slice(h * GDN_DH, (h + 1) * GDN_DH)
                q16 = x_ref[bi, rows, hs]
                k16 = x_ref[bi, rows, GDN_W + h * GDN_DH:GDN_W + (h + 1) * GDN_DH]
                qh, kh = q16.astype(F32), k16.astype(F32)
                vh = x_ref[bi, rows, 2 * GDN_W + h * GDN_DH:2 * GDN_W + (h + 1) * GDN_DH].astype(F32)
                gc = gc_all[:, h:h + 1]
                gr = gc_t[h:h + 1, :]
                g_last = gc_all[ch - 1:ch, h:h + 1]
                beta = beta_all[:, GDN_HEADS + h:GDN_HEADS + h + 1]
                eg = jnp.exp(gc)
                decay = jnp.where(incl, jnp.exp(jnp.minimum(gc - gr, 0.0)), 0.0)
                kb = kh * beta
                with_k = (_dot_nt(jnp.concatenate([kb.astype(BF16), q16], axis=0), k16)
                          * jnp.concatenate([decay, decay], axis=0))
                units.append(dict(
                    rows=rows, bi=bi, h=h,
                    lmat=jnp.where(strict, with_k[:ch], 0.0),
                    vb_kbg=jnp.concatenate([(vh * beta).astype(BF16), (kb * eg).astype(BF16)], axis=1),
                    qk=jnp.where(incl, with_k[ch:], 0.0).astype(BF16),
                    qg=(qh * eg).astype(BF16), kd_t=(kh * jnp.exp(g_last - gc)).T.astype(BF16),
                    gl=jnp.exp(g_last)))
        return run

    def inverse_start():
        eye = jnp.where(r == col, 1.0, 0.0)
        for u in units:
            l16 = u["lmat"].astype(BF16)
            xs.append(eye - u["lmat"])
            ps.append(_dot(l16, l16))

    inverse_stages = int(np.log2(ch)) - 1

    def inverse_stage(s):
        for i in range(n_units):
            rhs = ps[i].astype(BF16)
            if s + 1 == inverse_stages:
                xs[i] = xs[i] + _dot(xs[i].astype(BF16), rhs)
            else:
                both = _dot(jnp.concatenate([xs[i], ps[i]], axis=0).astype(BF16), rhs)
                xs[i] = xs[i] + both[:ch]
                ps[i] = both[ch:]

    pieces = [prepare(ci, bi) for ci in range(ct // ch) for bi in range(nb)] + [inverse_start]
    pieces += [functools.partial(inverse_stage, s) for s in range(inverse_stages)]
    for k in range(max(n_units, len(pieces))):
        if k < n_units:
            scan_unit(k)
        if k < len(pieces):
            pieces[k]()

    for i, u in enumerate(units):
        u_w = _dot(xs[i].astype(BF16), u["vb_kbg"])
        u_sc[i] = u_w[:, :GDN_DH]
        wq_sc[i] = jnp.concatenate([u_w[:, GDN_DH:].astype(BF16), u["qg"]], axis=0)
        kq_sc[i] = jnp.concatenate([u["kd_t"], u["qk"]], axis=0)
        gl_sc[i] = jnp.broadcast_to(u["gl"], gl_sc.shape[1:])


def _gdn(oqkv, oz, oab, alog, dtb, og, ct=128):
    b, t, w3 = oqkv.shape
    steps = t // ct
    n_units = (ct // GDN_CHUNK) * b * GDN_HEADS
    full = lambda a: pl.BlockSpec(a.shape, lambda s: (0,) * a.ndim)
    prep = lambda w: pl.BlockSpec((b, ct, w), lambda s: (0, jnp.minimum(s, steps - 1), 0))
    scan = pl.BlockSpec((b, ct, GDN_W), lambda s: (0, jnp.maximum(s - 1, 0), 0))
    return pl.pallas_call(
        functools.partial(_gdn_body, ct=ct),
        grid=(steps + 1,),
        in_specs=[prep(w3), scan, prep(LANES), full(alog), full(dtb), full(og)],
        out_specs=scan,
        out_shape=jax.ShapeDtypeStruct((b, t, GDN_W), BF16),
        scratch_shapes=[pltpu.VMEM((b * GDN_HEADS, GDN_DH, GDN_DH), F32),
                        pltpu.VMEM((n_units, GDN_CHUNK, GDN_DH), F32),
                        pltpu.VMEM((n_units, 2 * GDN_CHUNK, GDN_DH), BF16),
                        pltpu.VMEM((n_units, GDN_DH + GDN_CHUNK, GDN_CHUNK), BF16),
                        pltpu.VMEM((n_units, 8, LANES), F32)],
        compiler_params=_cparams(("arbitrary",), 32),
        name="gdn",
    )(oqkv, oz, oab, alog, dtb, og)


def _outproj_body(ont_ref, og_ref, x_ref, ng_ref, wo_ref, fg_ref, wr_ref, br_ref, upper_ref,
                  x1_ref, h2_ref, gate_ref, route_ref, cnt_ref, cnt_sc):
    i = pl.program_id(0)
    tm = x_ref.shape[0]

    @pl.when(i == 0)
    def _():
        cnt_sc[...] = jnp.zeros(cnt_sc.shape, F32)

    a = ont_ref[...]
    a = (a * lax.rsqrt(jnp.mean(a * a, axis=0, keepdims=True) + EPS) * ng_ref[...]).astype(BF16)
    x1 = x_ref[...] + _dot_tn(a, wo_ref[0:NSA_W, :]) + _dot(og_ref[...], wo_ref[NSA_W:, :])
    x1_ref[...] = x1
    h2f = x1 * lax.rsqrt(jnp.mean(x1 * x1, axis=-1, keepdims=True) + EPS) * fg_ref[...]
    _store_pieces(h2_ref, h2f)
    h2 = h2f.astype(BF16)

    logits = (_dot(h2, wr_ref[...]) + br_ref[...]).T[:N_EXPERTS]
    erow = lax.broadcasted_iota(jnp.int32, (N_EXPERTS, tm), 0)
    onehot = jnp.zeros((N_EXPERTS, tm), F32)
    firsts, vals = [], []
    v = logits
    for k in range(TOP_K):
        mx = jnp.max(v, axis=0, keepdims=True)
        first = jnp.min(jnp.where(v == mx, erow, N_EXPERTS), axis=0, keepdims=True)
        hit = erow == first
        v = jnp.where(hit, -jnp.inf, v)
        onehot = jnp.where(hit, 1.0, onehot)
        firsts.append(first)
        vals.append(mx)
    vals = [jnp.exp(m - vals[0]) for m in vals]
    inv = 1.0 / (vals[0] + vals[1] + vals[2] + vals[3])
    gates_t = jnp.concatenate([m * inv for m in vals] + [jnp.zeros((LANES - TOP_K, tm), F32)], axis=0)
    gate_ref[...] = gates_t.T

    excl = cnt_sc[...] + _dot(onehot.astype(BF16), upper_ref[...])
    for k in range(TOP_K):
        route_ref[k:k + 1, :] = firsts[k]
        rank = jnp.sum(jnp.where(erow == firsts[k], excl, 0.0), axis=0, keepdims=True)
        route_ref[TOP_K + k:TOP_K + k + 1, :] = rank.astype(jnp.int32)
    cnt_sc[...] = cnt_sc[...] + jnp.sum(onehot, axis=1, keepdims=True)
    cnt_ref[...] = cnt_sc[...].astype(jnp.int32)


def _out_proj(o_nsa_t, o_gdn, x2, ng, wo, fg, wr, br, tm=512):
    upper = jnp.asarray(np.arange(tm)[:, None] < np.arange(tm)[None, :], BF16)
    n, d = x2.shape
    full = lambda a: pl.BlockSpec(a.shape, lambda i: (0,) * a.ndim)
    row = lambda w: pl.BlockSpec((tm, w), lambda i: (i, 0))
    return pl.pallas_call(
        _outproj_body,
        grid=(n // tm,),
        in_specs=[pl.BlockSpec((NSA_W, tm), lambda i: (0, i)), row(GDN_W), row(d), full(ng), full(wo), full(fg),
                  full(wr), full(br), full(upper)],
        out_specs=[row(d), pl.BlockSpec((d // 2 // SC_SUBROW, tm, SC_SUBROW), lambda i: (0, i, 0)),
                   row(LANES), pl.BlockSpec((2 * TOP_K, tm), lambda i: (0, i)),
                   pl.BlockSpec((N_EXPERTS, 1), lambda i: (0, 0))],
        out_shape=[jax.ShapeDtypeStruct((n, d), F32),
                   jax.ShapeDtypeStruct((d // 2 // SC_SUBROW, n, SC_SUBROW), jnp.int32),
                   jax.ShapeDtypeStruct((n, LANES), F32), jax.ShapeDtypeStruct((2 * TOP_K, n), jnp.int32),
                   jax.ShapeDtypeStruct((N_EXPERTS, 1), jnp.int32)],
        scratch_shapes=[pltpu.VMEM((N_EXPERTS, 1), F32)],
        compiler_params=_cparams(("arbitrary",), 48),
        name="out_proj_router",
    )(o_nsa_t, o_gdn, x2, ng, wo, fg, wr, br, upper)


def _dest_body(ps_ref, route_ref, o_ref, *, n_rows, pieces):
    expert = route_ref[0:TOP_K, :]
    start = jnp.zeros(expert.shape, jnp.int32)
    for e in range(N_EXPERTS):
        start = jnp.where(expert == e, ps_ref[e], start)
    dest = start + route_ref[TOP_K:2 * TOP_K, :]
    for k in range(TOP_K):
        for j in range(pieces):
            o_ref[k * pieces + j:k * pieces + j + 1, :] = dest[k:k + 1, :] + j * n_rows


def _dest_rows(pstarts, route, n_rows, pieces):
    n = route.shape[1]
    tn = min(2048, n)
    grid_spec = pltpu.PrefetchScalarGridSpec(
        num_scalar_prefetch=1,
        grid=(n // tn,),
        in_specs=[pl.BlockSpec((2 * TOP_K, tn), lambda i, ps: (0, i))],
        out_specs=pl.BlockSpec((TOP_K * pieces, tn), lambda i, ps: (0, i)),
    )
    return pl.pallas_call(
        functools.partial(_dest_body, n_rows=n_rows, pieces=pieces),
        grid_spec=grid_spec,
        out_shape=jax.ShapeDtypeStruct((TOP_K * pieces, n), jnp.int32),
        name="moe_dest_rows",
    )(pstarts, route)


def _expert_body(be_ref, end_ref, xs_ref, wg_hbm, bg_ref, wu_hbm, bu_ref, wd_hbm, bd_ref,
                 y_ref, wf32, w16, sems, slot_sc):
    i = pl.program_id(0)
    n_used = be_ref[pl.num_programs(0)]
    used = i < n_used
    expert = be_ref[i]
    fresh = used & ((i == 0) | (expert != be_ref[jnp.maximum(i - 1, 0)]))
    hbm = (wg_hbm, wu_hbm, wd_hbm)

    def weight_copy(e, slot, j):
        return pltpu.make_async_copy(hbm[j].at[e], wf32.at[slot, j], sems.at[slot, j])

    @pl.when(i == 0)
    def _():
        slot_sc[0] = 0

    @pl.when((i == 0) & used)
    def _():
        for j in range(3):
            weight_copy(expert, 0, j).start()

    @pl.when(fresh)
    def _():
        slot = slot_sc[0]
        for j in range(3):
            weight_copy(expert, slot, j).wait()
            w16[j] = wf32[slot, j].astype(BF16)
        following = end_ref[expert]

        @pl.when(following < n_used)
        def _():
            for j in range(3):
                weight_copy(be_ref[following], 1 - slot, j).start()

        slot_sc[0] = 1 - slot

    @pl.when(jnp.logical_not(used))
    def _():
        y_ref[...] = jnp.zeros(y_ref.shape, y_ref.dtype)

    @pl.when(used)
    def _():
        x = _join_pieces(xs_ref).astype(BF16)
        gate = jnp.minimum(_dot(x, w16[0]) + bg_ref[0], SWIGLU_LIMIT)
        up = jnp.clip(_dot(x, w16[1]) + bu_ref[0], -SWIGLU_LIMIT, SWIGLU_LIMIT)
        glu = gate * jax.nn.sigmoid(gate * SWIGLU_ALPHA)
        act = ((up + 1.0) * glu).astype(BF16)
        cols = 2 * y_ref.shape[2]
        for j in range(y_ref.shape[0]):
            sl = slice(j * cols, (j + 1) * cols)
            y_ref[j] = _pack_piece(_dot(act, w16[2, :, sl]) + bd_ref[0, :, sl])


def _experts(blk_e, end_blk, xs, wg, bg, wu, bu, wd, bd):
    pieces, n_rows, sub = xs.shape
    d, de = wg.shape[1], wg.shape[2]
    assert d == de
    r = MOE_ROW_BLOCK
    bspec = lambda w: pl.BlockSpec((1, 1, w), lambda i, be, *_: (be[i], 0, 0))
    hbm = pl.BlockSpec(memory_space=pl.ANY)
    grid_spec = pltpu.PrefetchScalarGridSpec(
        num_scalar_prefetch=2,
        grid=(n_rows // r,),
        in_specs=[pl.BlockSpec((pieces, r, sub), lambda i, *_: (0, i, 0)),
                  hbm, bspec(de), hbm, bspec(de), hbm, bspec(d)],
        out_specs=pl.BlockSpec((pieces, r, sub), lambda i, *_: (0, i, 0)),
        scratch_shapes=[pltpu.VMEM((2, 3, d, de), F32), pltpu.VMEM((3, d, de), BF16),
                        pltpu.SemaphoreType.DMA((2, 3)), pltpu.SMEM((1,), jnp.int32)],
    )
    return pl.pallas_call(
        _expert_body,
        grid_spec=grid_spec,
        out_shape=jax.ShapeDtypeStruct((pieces, n_rows, sub), jnp.int32),
        compiler_params=_cparams(("arbitrary",), 56),
        name="moe_experts",
    )(blk_e, end_blk, xs, wg, bg, wu, bu, wd, bd)


SC_WINDOW = 128
SC_SUBROW = 256


def _sc_mesh():
    return plsc.VectorSubcoreMesh(core_axis_name="c", subcore_axis_name="s")


def _sc_dispatch(h2, dest_rows, n_rows):
    n, d = h2.shape

    @functools.partial(pl.kernel, out_type=jax.ShapeDtypeStruct((n_rows, d), h2.dtype), mesh=_sc_mesh())
    def dispatch(x_hbm, *refs):
        idx_hbm, o_hbm = refs[:TOP_K], refs[TOP_K]

        def body(x_vmem, *idx_vmem):
            for iv in idx_vmem:
                pltpu.sync_copy(x_vmem, o_hbm.at[iv.at[0]])

        pltpu.emit_pipeline(
            body,
            grid=(n // SC_WINDOW,),
            in_specs=[pl.BlockSpec((SC_WINDOW, d), lambda i: (i, 0))]
                     + [pl.BlockSpec((1, SC_WINDOW), lambda i: (0, i))] * TOP_K,
            out_specs=[],
            core_axis_name=("c", "s"),
            dimension_semantics=(pltpu.PARALLEL,),
        )(x_hbm, *idx_hbm)

    return dispatch(h2, *dest_rows)


def _sc_gather(table, idx):
    _, d = table.shape
    m = idx.shape[1]

    @functools.partial(pl.kernel, out_type=jax.ShapeDtypeStruct((m, d), table.dtype), mesh=_sc_mesh())
    def gather(t_hbm, i_hbm, o_hbm):
        def body(i_vmem, o_vmem):
            pltpu.sync_copy(t_hbm.at[i_vmem.at[0]], o_vmem)

        pltpu.emit_pipeline(
            body,
            grid=(m // SC_WINDOW,),
            in_specs=[pl.BlockSpec((1, SC_WINDOW), lambda i: (0, i))],
            out_specs=[pl.BlockSpec((SC_WINDOW, d), lambda i: (i, 0))],
            core_axis_name=("c", "s"),
            dimension_semantics=(pltpu.PARALLEL,),
        )(i_hbm, o_hbm)

    return gather(table, idx)


def _combine_body(x1_ref, y_ref, gate_ref, o_ref):
    acc = x1_ref[...]
    for k in range(TOP_K):
        acc = acc + gate_ref[:, k:k + 1] * _join_pieces(y_ref.at[k])
    o_ref[...] = acc


def _combine(x1, y4, gates, tm=512):
    n, d = x1.shape
    pieces, sub = y4.shape[1], y4.shape[3]
    row = lambda w: pl.BlockSpec((tm, w), lambda i: (i, 0))
    return pl.pallas_call(
        _combine_body,
        grid=(n // tm,),
        in_specs=[row(d), pl.BlockSpec((TOP_K, pieces, tm, sub), lambda i: (0, 0, i, 0)), row(LANES)],
        out_specs=row(d),
        out_shape=jax.ShapeDtypeStruct((n, d), F32),
        compiler_params=_cparams(("parallel",), 48),
        name="moe_combine",
    )(x1, y4, gates)


def _pad_lanes(a, width=LANES):
    return jnp.pad(a, ((0, 0), (0, width - a.shape[1])))


def _layer(x, attn_norm_g, w_in, q_g, kc_g, ks_g, kw_g, ck_pos, ck_w1, ck_b1, ck_w2, ck_b2,
           cv_pos, cv_w1, cv_b1, cv_w2, cv_b2, nsa_out_g, conv_w, a_log, dt_bias, gdn_out_g, w_out,
           ffn_g, router_w, router_b, e_wg, e_bg, e_wu, e_bu, e_wd, e_bd):
    b, t, d = x.shape
    n = b * t
    x2 = x.reshape(n, d)

    o = np.cumsum([0, NSA_W] + [NSA_GROUPS * NSA_DH] * 6 + [3 * NSA_HEADS, 3 * GDN_W, GDN_W, GDN_HEADS, GDN_HEADS])
    wq_t = w_in[:, o[0]:o[1]].T.astype(BF16)
    qg_col = jnp.tile(q_g * (NSA_DH ** -0.5 * np.log2(np.e)), NSA_HEADS).reshape(NSA_W, 1)
    wkv = jnp.concatenate([w_in[:, o[1]:o[4]], w_in[:, o[5]:o[6]]], axis=1).astype(BF16)
    kg = jnp.concatenate([ks_g, ks_g, kw_g, kw_g]).reshape(1, 2 * LANES)
    wv_t = jnp.concatenate([w_in[:, o[4]:o[5]], w_in[:, o[6]:o[7]]], axis=1).T.reshape(2 * NSA_GROUPS, NSA_DH, d)
    wv_t = jnp.pad(wv_t, ((0, 0), (0, LANES - NSA_DH), (0, 0))).reshape(2 * NSA_GROUPS * LANES, d).astype(BF16)
    vone = jnp.asarray((np.arange(2 * NSA_GROUPS * LANES) % LANES == NSA_DH).astype(np.float32)[:, None])
    wg_t = w_in[:, o[7]:o[8]].T.reshape(NSA_GROUPS, NSA_HPG * 3, d)
    wg_t = jnp.pad(wg_t, ((0, 0), (0, GATE_ROWS - NSA_HPG * 3), (0, 0))).reshape(NSA_GROUPS * GATE_ROWS, d)
    wg_t = wg_t.astype(BF16)
    wab = _pad_lanes(w_in[:, o[10]:o[12]]).astype(BF16)
    wqkv = w_in[:, o[8]:o[9]].astype(BF16)
    wz = w_in[:, o[9]:o[10]].astype(BF16)

    tm = min(512, t)
    oqt, okn, xflat, ovt, ogt, oqkv, oz, oab = _in_proj(x2, attn_norm_g.reshape(1, d), wq_t, wkv, wv_t, wg_t, wqkv,
                                                        wz, wab, qg_col, kg, vone, conv_w, t // tm, tm)

    nch = t // CMP_STRIDE
    n_cmp = (t - CMP_BLOCK) // CMP_STRIDE + 1
    half = CMP_STRIDE * NSA_DH
    pos = jnp.stack([ck_pos, cv_pos]).reshape(2, 2, 1, half)
    w1 = jnp.stack([ck_w1, cv_w1]).reshape(2, 2, half, CMP_HIDDEN).astype(BF16)
    b1 = jnp.stack([ck_b1, cv_b1]).reshape(2, 1, CMP_HIDDEN)
    w2 = jnp.stack([ck_w2, cv_w2]).astype(BF16)
    b2 = jnp.stack([ck_b2, cv_b2]).reshape(2, 1, NSA_DH)
    w2t = jnp.stack([ck_w2.T, cv_w2.T]).astype(BF16)
    b2t = jnp.stack([ck_b2, cv_b2]).reshape(2, NSA_DH, 1)
    kc, vct = _compress(xflat, pos, w1, b1, w2, b2, w2t, b2t, kc_g.reshape(1, NSA_DH), n_cmp)

    n_slc = t // SLC_BLOCK
    n_top = min(SLC_TOPK, n_slc)
    assert n_top > 3
    nblk = max(n_slc, LANES)
    kt = min(256, t // 4)
    assert (t // kt) % 4 == 0
    ci = np.arange(nch)[None, :] * CMP_STRIDE
    sj = np.arange(nblk)[:, None] * SLC_BLOCK
    overlap = ((ci < sj + SLC_BLOCK) & (ci + CMP_BLOCK > sj) & (np.arange(nch)[None, :] < n_cmp)
               & (np.arange(nblk)[:, None] < n_slc))
    expand_t = (np.arange(t)[:, None] // SLC_BLOCK) == np.arange(nblk)[None, :]
    o_nsa_t = _nsa_attention(oqt, ogt, kc, vct, okn.reshape(b, t, -1), ovt, jnp.asarray(expand_t, BF16),
                             jnp.asarray(overlap, BF16), b, t, n_top, kt)

    alog_row = _pad_lanes(a_log.reshape(1, GDN_HEADS))
    dtb_row = _pad_lanes(dt_bias.reshape(1, GDN_HEADS))
    o_gdn = _gdn(oqkv.reshape(b, t, -1), oz.reshape(b, t, -1), oab.reshape(b, t, -1),
                 alog_row, dtb_row, gdn_out_g.reshape(1, GDN_DH))

    wr = _pad_lanes(router_w).astype(BF16)
    br = _pad_lanes(router_b.reshape(1, N_EXPERTS))
    x1, h2, gates, route, counts = _out_proj(
        o_nsa_t, o_gdn.reshape(n, GDN_W), x2, nsa_out_g.reshape(NSA_W, 1),
        w_out.astype(BF16), ffn_g.reshape(1, d), wr, br)

    r = MOE_ROW_BLOCK
    nk = n * TOP_K
    counts = counts[:, 0]
    pcounts = (counts + r - 1) // r * r
    pends = jnp.cumsum(pcounts)
    pstarts = pends - pcounts
    n_rows = (nk + r - 1) // r * r + N_EXPERTS * r
    n_blocks = n_rows // r
    blk_start = jnp.arange(n_blocks, dtype=jnp.int32)[:, None] * r
    blk_e = jnp.minimum(jnp.sum(pends[None, :] <= blk_start, axis=1), N_EXPERTS - 1).astype(jnp.int32)
    n_used = (pends[-1] // r).astype(jnp.int32)
    blk_e = jnp.concatenate([blk_e, n_used[None]])
    end_blk = (pends // r).astype(jnp.int32)
    pieces = d // 2 // SC_SUBROW
    dest_p = _dest_rows(pstarts.astype(jnp.int32), route, n_rows, pieces).reshape(TOP_K, pieces, n)
    xs = _sc_dispatch(h2.reshape(pieces * n, SC_SUBROW), [dest_p[k].reshape(1, pieces * n) for k in range(TOP_K)],
                      pieces * n_rows)
    ys = _experts(blk_e, end_blk, xs.reshape(pieces, n_rows, SC_SUBROW), e_wg, e_bg.reshape(N_EXPERTS, 1, -1),
                  e_wu, e_bu.reshape(N_EXPERTS, 1, -1), e_wd, e_bd.reshape(N_EXPERTS, 1, -1))
    y4 = _sc_gather(ys.reshape(pieces * n_rows, SC_SUBROW), dest_p.reshape(1, nk * pieces))
    return _combine(x1, y4.reshape(TOP_K, pieces, n, SC_SUBROW), gates).reshape(b, t, d)


def kernel(x, attn_norm_g, w_in, nsa_q_norm_g, nsa_kc_norm_g, nsa_ks_norm_g, nsa_kw_norm_g, cmp_k_pos, cmp_k_w1, cmp_k_b1, cmp_k_w2, cmp_k_b2, cmp_v_pos, cmp_v_w1, cmp_v_b1, cmp_v_w2, cmp_v_b2, nsa_out_norm_g, gdn_conv_w, gdn_a_log, gdn_dt_bias, gdn_out_norm_g, w_out, ffn_norm_g, router_w, router_b, exp_w_gate, exp_b_gate, exp_w_up, exp_b_up, exp_w_down, exp_b_down):
    params = (attn_norm_g, w_in, nsa_q_norm_g, nsa_kc_norm_g, nsa_ks_norm_g, nsa_kw_norm_g,
              cmp_k_pos, cmp_k_w1, cmp_k_b1, cmp_k_w2, cmp_k_b2, cmp_v_pos, cmp_v_w1, cmp_v_b1, cmp_v_w2, cmp_v_b2,
              nsa_out_norm_g, gdn_conv_w, gdn_a_log, gdn_dt_bias, gdn_out_norm_g, w_out, ffn_norm_g,
              router_w, router_b, exp_w_gate, exp_b_gate, exp_w_up, exp_b_up, exp_w_down, exp_b_down)
    for l in range(attn_norm_g.shape[0]):
        x = _layer(x, *(p[l] for p in params))
    return x
```

```python
import functools

import jax
import jax.numpy as jnp
import numpy as np
from jax import lax
from jax.experimental import pallas as pl
from jax.experimental.pallas import tpu as pltpu
from jax.experimental.pallas import tpu_sc as plsc

F32 = jnp.float32
BF16 = jnp.bfloat16

EPS = 1e-6
NEG = -1e30
MASKED = -2.0 ** 100

NSA_HEADS = 8
NSA_GROUPS = 2
NSA_HPG = 4
NSA_DH = 64
CMP_BLOCK = 32
CMP_STRIDE = 16
CMP_HIDDEN = 256
SLC_BLOCK = 64
SLC_TOPK = 16
WINDOW = 512
NSA_Q = 256
GDN_HEADS = 4
GDN_DH = 128
GDN_CHUNK = 64
N_EXPERTS = 32
TOP_K = 4
SWIGLU_LIMIT = 7.0
SWIGLU_ALPHA = 1.702
MOE_ROW_BLOCK = 256

LANES = 128
GATE_ROWS = 16
FLASH_BODY_TILES = (8, 4)
NSA_W = NSA_HEADS * NSA_DH
GDN_W = GDN_HEADS * GDN_DH

_NT = (((1,), (1,)), ((), ()))
_TN = (((0,), (0,)), ((), ()))


def _cparams(sem, vmem_mb):
    return pltpu.CompilerParams(dimension_semantics=sem, vmem_limit_bytes=vmem_mb * 1024 * 1024)


def _dot(a, b):
    return jnp.dot(a, b, preferred_element_type=F32)


def _dot_nt(a, b):
    return lax.dot_general(a, b, _NT, preferred_element_type=F32)


def _dot_tn(a, b):
    return lax.dot_general(a, b, _TN, preferred_element_type=F32)


def _pack_piece(block):
    words = block.shape[1] // 2
    hi = lax.bitcast_convert_type(block[:, :words].astype(BF16).astype(F32), jnp.uint32)
    lo = lax.bitcast_convert_type(block[:, words:].astype(BF16).astype(F32), jnp.uint32)
    return lax.bitcast_convert_type(hi | (lo >> 16), jnp.int32)


def _store_pieces(ref, val):
    cols = 2 * ref.shape[2]
    for j in range(ref.shape[0]):
        ref[j] = _pack_piece(val[:, j * cols:(j + 1) * cols])


def _join_pieces(ref):
    out = []
    for j in range(ref.shape[0]):
        words = lax.bitcast_convert_type(ref[j], jnp.uint32)
        out.append(lax.bitcast_convert_type(words & jnp.uint32(0xFFFF0000), F32))
        out.append(lax.bitcast_convert_type(words << 16, F32))
    return jnp.concatenate(out, axis=1)


def _inproj_body(x_ref, g_ref, wqt_ref, wkv_ref, wvt_ref, wgt_ref, wqkv_ref, wz_ref, wab_ref, qg_ref, kg_ref,
                 vone_ref, cw_ref, oqt_ref, okn_ref, ocf_ref, ovt_ref, ogt_ref, oqkv_ref, oz_ref, oab_ref, ybuf, cbuf,
                 *, tiles_per_seq):
    tm = x_ref.shape[0]

    halo = ybuf.shape[0] - tm
    first = pl.program_id(0) % tiles_per_seq == 0

    @pl.when(first)
    def _():
        ybuf[0:halo, :] = jnp.zeros((halo, ybuf.shape[1]), F32)

    @pl.when(jnp.logical_not(first))
    def _():
        ybuf[0:halo, :] = ybuf[tm:tm + halo, :]

    x = x_ref[...]
    h = (x * lax.rsqrt(jnp.mean(x * x, axis=-1, keepdims=True) + EPS) * g_ref[...]).astype(BF16)
    ybuf[halo:halo + tm, :] = _dot(h, wqkv_ref[...])

    yq = _dot_nt(wqt_ref[...], h)
    for s in range(NSA_HEADS):
        sl = slice(s * NSA_DH, (s + 1) * NSA_DH)
        ys = yq[sl, :]
        ms = jnp.sum(ys * ys, axis=0, keepdims=True) * (1.0 / NSA_DH)
        oqt_ref[sl, :] = (ys * lax.rsqrt(ms + EPS) * qg_ref[sl, :]).astype(BF16)

    ykv = _dot(h, wkv_ref[...])
    lane = lax.broadcasted_iota(jnp.int32, (tm, LANES), 1)
    low = lane < NSA_DH
    for s in range(2):
        sl = slice(s * LANES, (s + 1) * LANES)
        ys = ykv[:, (2 + s) * LANES:(3 + s) * LANES]
        y2 = ys * ys
        s0 = jnp.sum(jnp.where(low, y2, 0.0), axis=-1, keepdims=True)
        s1 = jnp.sum(jnp.where(low, 0.0, y2), axis=-1, keepdims=True)
        ms = jnp.where(low, s0, s1) * (1.0 / NSA_DH)
        okn_ref[:, sl] = (ys * lax.rsqrt(ms + EPS) * kg_ref[:, sl]).astype(BF16)

    chunks = tm // CMP_STRIDE
    for br in range(2):
        cbuf[br] = ykv[:, br * LANES:(br + 1) * LANES]
        taken = [cbuf[br, pl.ds(l, chunks, stride=CMP_STRIDE), :] for l in range(CMP_STRIDE)]
        for grp in range(NSA_GROUPS):
            flat = jnp.concatenate([r[:, grp * NSA_DH:(grp + 1) * NSA_DH] for r in taken], axis=1)
            ocf_ref[0, br, grp] = flat.astype(BF16)

    ovt_ref[...] = (_dot_nt(wvt_ref[...], h) + vone_ref[...]).astype(BF16)
    ogt_ref[...] = _dot_nt(wgt_ref[...], h)
    oz_ref[...] = _dot(h, wz_ref[...]).astype(BF16)
    oab_ref[...] = _dot(h, wab_ref[...])

    taps = cw_ref.shape[0]
    y = cw_ref[0:1, :] * ybuf[pl.ds(halo - taps + 1, tm), :]
    for k in range(1, taps):
        y = y + cw_ref[k:k + 1, :] * ybuf[pl.ds(halo - taps + 1 + k, tm), :]
    hy = 0.5 * y
    y = hy + hy * jnp.tanh(hy)
    for s in range(3 * GDN_HEADS):
        sl = slice(s * GDN_DH, (s + 1) * GDN_DH)
        ys = y[:, sl]
        if s < 2 * GDN_HEADS:
            scale = GDN_DH ** -0.5 if s < GDN_HEADS else 1.0
            ys = ys * (lax.rsqrt(jnp.sum(ys * ys, axis=-1, keepdims=True) + EPS) * scale)
        oqkv_ref[:, sl] = ys.astype(BF16)


def _in_proj(x2, g, wqt, wkv, wvt, wgt, wqkv, wz, wab, qg, kg, vone, conv_w, tiles_per_seq, tm):
    n, d = x2.shape
    chunks, flat = tm // CMP_STRIDE, CMP_STRIDE * NSA_DH
    full = lambda a: pl.BlockSpec(a.shape, lambda i: (0,) * a.ndim)
    row = lambda w: pl.BlockSpec((tm, w), lambda i: (i, 0))
    colb = lambda r: pl.BlockSpec((r, tm), lambda i: (0, i))
    return pl.pallas_call(
        functools.partial(_inproj_body, tiles_per_seq=tiles_per_seq),
        grid=(n // tm,),
        in_specs=[row(d)] + [full(a) for a in (g, wqt, wkv, wvt, wgt, wqkv, wz, wab, qg, kg, vone, conv_w)],
        out_specs=[colb(wqt.shape[0]), row(2 * LANES),
                   pl.BlockSpec((1, 2, NSA_GROUPS, chunks, flat), lambda i: (i // tiles_per_seq, 0, 0, i % tiles_per_seq, 0)),
                   colb(wvt.shape[0]), colb(wgt.shape[0]), row(wqkv.shape[1]), row(wz.shape[1]), row(wab.shape[1])],
        out_shape=[jax.ShapeDtypeStruct((wqt.shape[0], n), BF16), jax.ShapeDtypeStruct((n, 2 * LANES), BF16),
                   jax.ShapeDtypeStruct((n // (tm * tiles_per_seq), 2, NSA_GROUPS, chunks * tiles_per_seq, flat), BF16),
                   jax.ShapeDtypeStruct((wvt.shape[0], n), BF16), jax.ShapeDtypeStruct((wgt.shape[0], n), F32),
                   jax.ShapeDtypeStruct((n, wqkv.shape[1]), BF16), jax.ShapeDtypeStruct((n, wz.shape[1]), BF16),
                   jax.ShapeDtypeStruct((n, wab.shape[1]), F32)],
        scratch_shapes=[pltpu.VMEM((tm + 8, wqkv.shape[1]), F32), pltpu.VMEM((2, tm, LANES), F32)],
        compiler_params=_cparams(("arbitrary",), 56),
        name="in_proj",
    )(x2, g, wqt, wkv, wvt, wgt, wqkv, wz, wab, qg, kg, vone, conv_w)


def _compress_body(x_ref, pos_ref, w1_ref, b1_ref, w2_ref, b2_ref, w2t_ref, b2t_ref, g_ref, ok_ref, ovt_ref,
                   *, n_cmp):
    is_key = pl.program_id(1) == 0
    nch = x_ref.shape[3]
    hids = []
    for grp in range(NSA_GROUPS):
        x = x_ref[0, 0, grp].astype(F32)
        xa = (x + pos_ref[0, 0]).astype(BF16)
        xb = (x + pos_ref[0, 1]).astype(BF16)
        a = _dot(xa, w1_ref[0, 0])
        b = _dot(xb, w1_ref[0, 1])
        b_next = pltpu.roll(b, nch - 1, 0)
        hids.append(jax.nn.gelu(a + b_next + b1_ref[0]).astype(BF16))

    @pl.when(is_key)
    def _():
        row = lax.broadcasted_iota(jnp.int32, (nch, NSA_DH), 0)
        outs = []
        for grp in range(NSA_GROUPS):
            out = _dot(hids[grp], w2_ref[0]) + b2_ref[0]
            out = out * lax.rsqrt(jnp.mean(out * out, axis=-1, keepdims=True) + EPS) * g_ref[...]
            outs.append(jnp.where(row < n_cmp, out, 0.0))
        ok_ref[0] = jnp.concatenate(outs, axis=-1).astype(BF16)

    @pl.when(jnp.logical_not(is_key))
    def _():
        col = lax.broadcasted_iota(jnp.int32, (NSA_DH, nch), 1)
        outs = []
        for grp in range(NSA_GROUPS):
            out = _dot_nt(w2t_ref[0], hids[grp]) + b2t_ref[0]
            outs.append(jnp.where(col < n_cmp, out, 0.0))
        ovt_ref[0] = jnp.concatenate(outs, axis=0).astype(BF16)


def _compress(xflat, pos, w1, b1, w2, b2, w2t, b2t, kc_g, n_cmp):
    b, _, _, nch, flat = xflat.shape
    return pl.pallas_call(
        functools.partial(_compress_body, n_cmp=n_cmp),
        grid=(b, 2),
        in_specs=[
            pl.BlockSpec((1, 1, NSA_GROUPS, nch, flat), lambda i, j: (i, j, 0, 0, 0)),
            pl.BlockSpec((1, 2, 1, flat), lambda i, j: (j, 0, 0, 0)),
            pl.BlockSpec((1, 2, flat, CMP_HIDDEN), lambda i, j: (j, 0, 0, 0)),
            pl.BlockSpec((1, 1, CMP_HIDDEN), lambda i, j: (j, 0, 0)),
            pl.BlockSpec((1, CMP_HIDDEN, NSA_DH), lambda i, j: (j, 0, 0)),
            pl.BlockSpec((1, 1, NSA_DH), lambda i, j: (j, 0, 0)),
            pl.BlockSpec((1, NSA_DH, CMP_HIDDEN), lambda i, j: (j, 0, 0)),
            pl.BlockSpec((1, NSA_DH, 1), lambda i, j: (j, 0, 0)),
            pl.BlockSpec((1, NSA_DH), lambda i, j: (0, 0)),
        ],
        out_specs=[pl.BlockSpec((1, nch, LANES), lambda i, j: (i, 0, 0)),
                   pl.BlockSpec((1, LANES, nch), lambda i, j: (i, 0, 0))],
        out_shape=[jax.ShapeDtypeStruct((b, nch, LANES), BF16), jax.ShapeDtypeStruct((b, LANES, nch), BF16)],
        compiler_params=_cparams(("parallel", "arbitrary"), 32),
        name="nsa_compress",
    )(xflat, pos, w1, b1, w2, b2, w2t, b2t, kc_g)


def _tile_heads(a):
    return jnp.concatenate([a] * NSA_HPG, axis=1)


def _nsa_body(qt_ref, gt_ref, kc_ref, vct_ref, ks_ref, kw_ref, vst_ref, vwt_ref, et_ref, ov_ref, cpat_ref, wpat_ref,
              dpat_ref, o_ref, acc_sc, s_sc, *, n_top, kt):
    grp = pl.program_id(1)
    s0 = pl.program_id(2) * NSA_Q
    nch = kc_ref.shape[1]
    nblk = ov_ref.shape[0]

    qh = jnp.concatenate([qt_ref[h * NSA_DH:(h + 1) * NSA_DH, :] for h in range(NSA_HPG)], axis=1)
    zq = jnp.zeros_like(qh)
    qt = jnp.where(grp == 0, jnp.concatenate([qh, zq], axis=0), jnp.concatenate([zq, qh], axis=0))
    t_row = s0 + lax.broadcasted_iota(jnp.int32, (1, NSA_Q), 1)

    cbias = cpat_ref[pl.ds(pl.multiple_of(nch - s0 // CMP_STRIDE, CMP_STRIDE), nch), :]
    sc = _dot(kc_ref[0], qt) + _tile_heads(cbias)
    wlen = WINDOW + NSA_Q
    w0 = pl.multiple_of(jnp.maximum(s0 - WINDOW, 0), NSA_Q)
    wbias = wpat_ref[pl.ds(pl.multiple_of(w0 - s0 + WINDOW, NSA_Q), wlen), :]
    sw = _dot(kw_ref[0, pl.ds(w0, wlen), :], qt) + _tile_heads(wbias)
    pc = jnp.exp2(sc - jnp.max(sc, axis=0, keepdims=True)).astype(BF16)
    stacked = jnp.concatenate([vct_ref[0], ov_ref[...], jnp.ones((8, nch), BF16)], axis=0)
    res = _dot(stacked, pc)
    inv = jnp.where(_tile_heads(t_row >= CMP_BLOCK - 1), 1.0 / jnp.maximum(res[LANES + nblk:LANES + nblk + 1], 1e-30),
                    0.0)
    oc = res[:LANES] * inv
    imp4 = res[LANES:LANES + nblk] * inv
    imp = (imp4[:, 0:NSA_Q] + imp4[:, NSA_Q:2 * NSA_Q] + imp4[:, 2 * NSA_Q:3 * NSA_Q]
           + imp4[:, 3 * NSA_Q:4 * NSA_Q])
    blk = lax.broadcasted_iota(jnp.int32, (nblk, NSA_Q), 0)
    cur = t_row // SLC_BLOCK
    imp = jnp.where(blk * SLC_BLOCK > t_row, NEG, imp)
    imp = jnp.where((blk == 0) | (blk == cur) | (blk == cur - 1), -jnp.inf, imp)
    rounds_left = n_top - 3

    def pick_rounds(v, rounds):
        for _ in range(rounds):
            mx = jnp.max(v, axis=0, keepdims=True)
            first = jnp.min(jnp.where(v == mx, blk, nblk), axis=0, keepdims=True)
            v = jnp.where(blk == first, -jnp.inf, v)
        return v

    quarter = rounds_left // 4
    picked = pick_rounds(imp, quarter)

    picked = pick_rounds(picked, quarter)
    pw = jnp.exp2(sw - jnp.max(sw, axis=0, keepdims=True)).astype(BF16)
    picked = pick_rounds(picked, quarter)
    ow = _dot(vwt_ref[:, pl.ds(w0, wlen)], pw)
    ow = ow[:NSA_DH] / ow[NSA_DH:NSA_DH + 1]

    picked = pick_rounds(picked, rounds_left - 3 * quarter)
    chosen = picked == -jnp.inf

    selb = jnp.where(chosen & (blk * SLC_BLOCK < s0), 0.0, MASKED).astype(BF16)
    rhs = jnp.concatenate([qt, _tile_heads(selb)], axis=0)
    last_tile = ks_ref.shape[1] // kt - 1

    def scores(idx, slot):
        k0 = pl.multiple_of(jnp.minimum(idx, last_tile) * kt, kt)
        lhs = jnp.concatenate([ks_ref[0, pl.ds(k0, kt), :], et_ref[pl.ds(k0, kt), :]], axis=1)
        s_sc[slot] = _dot(lhs, rhs)

    scores(0, 0)
    scores(1, 1)

    d0 = pl.multiple_of(s0, NSA_Q)
    selb_d = jnp.where(chosen & (blk <= cur), 0.0, MASKED).astype(BF16)
    sd = (_dot(jnp.concatenate([ks_ref[0, pl.ds(d0, NSA_Q), :], et_ref[pl.ds(d0, NSA_Q), :]], axis=1),
               jnp.concatenate([qt, _tile_heads(selb_d)], axis=0))
          + _tile_heads(dpat_ref[...]))
    m_diag = jnp.max(sd, axis=0, keepdims=True)
    acc_sc[0] = _dot(vst_ref[:, pl.ds(d0, NSA_Q)], jnp.exp2(sd - m_diag).astype(BF16))
    acc_sc[1] = jnp.zeros(acc_sc.shape[1:], F32)

    def update(idx, slot, m_old, acc_ref):
        k0 = pl.multiple_of(idx * kt, kt)
        m_new = jnp.maximum(m_old, jnp.max(s_sc[slot], axis=0, keepdims=True))
        p = jnp.exp2(s_sc[slot] - m_new).astype(BF16)
        acc_ref[...] = jnp.exp2(m_old - m_new) * acc_ref[...] + _dot(vst_ref[:, pl.ds(k0, kt)], p)
        return m_new

    def tile_group(first, carry, count):
        ms = list(carry)
        for t in range(count):
            scores(first + t + 2, (t + 2) % 4)
            ms[t % 2] = update(first + t, t % 4, ms[t % 2], acc_sc.at[t % 2])
        return tuple(ms)

    n_tiles = (s0 + kt - 1) // kt
    carry = (m_diag, jnp.full((1, NSA_HPG * NSA_Q), NEG, F32))
    done = 0
    for size in FLASH_BODY_TILES:
        left = n_tiles - done
        groups = (left + size - 1) // size if size == FLASH_BODY_TILES[-1] else left // size
        carry = lax.fori_loop(0, groups, lambda j, c, done=done, size=size: tile_group(done + size * j, c, size),
                              carry)
        done = done + size * groups
    m0, m1 = carry
    m_fin = jnp.maximum(m0, m1)
    acc = acc_sc[0] * jnp.exp2(m0 - m_fin) + acc_sc[1] * jnp.exp2(m1 - m_fin)
    osl = acc[:NSA_DH] / acc[NSA_DH:NSA_DH + 1]

    oc = jnp.where(grp == 0, oc[:NSA_DH], oc[NSA_DH:])
    gts = jax.nn.sigmoid(gt_ref[...])
    for h in range(NSA_HPG):
        cols = slice(h * NSA_Q, (h + 1) * NSA_Q)
        o_ref[h * NSA_DH:(h + 1) * NSA_DH, :] = (
            gts[3 * h:3 * h + 1, :] * oc[:, cols] + gts[3 * h + 1:3 * h + 2, :] * osl[:, cols]
            + gts[3 * h + 2:3 * h + 3, :] * ow[:, cols])


def _mask_pattern(valid):
    return jnp.asarray(np.where(valid, 0.0, NEG), F32)


def _nsa_attention(qt, gt, kc, vct, okv, vt, expand_t, overlap, b, t, n_top, kt):
    nch = kc.shape[1]
    nq = t // NSA_Q
    n = b * t
    ql = np.arange(NSA_Q)[None, :]
    rc = np.arange(2 * nch)[:, None] - nch
    cpat = _mask_pattern(rc * CMP_STRIDE + CMP_BLOCK - 1 <= ql)
    rw = np.arange(2 * WINDOW + NSA_Q)[:, None] - WINDOW
    wpat = _mask_pattern((rw <= ql) & (rw > ql - WINDOW))
    dpat = _mask_pattern(np.arange(NSA_Q)[:, None] <= ql)
    full2 = lambda a: pl.BlockSpec(a.shape, lambda bi, g, i: (0, 0))
    return pl.pallas_call(
        functools.partial(_nsa_body, n_top=n_top, kt=kt),
        grid=(b, NSA_GROUPS, nq),
        in_specs=[
            pl.BlockSpec((NSA_HPG * NSA_DH, NSA_Q), lambda bi, g, i: (g, bi * nq + i)),
            pl.BlockSpec((GATE_ROWS, NSA_Q), lambda bi, g, i: (g, bi * nq + i)),
            pl.BlockSpec((1, nch, LANES), lambda bi, g, i: (bi, 0, 0)),
            pl.BlockSpec((1, LANES, nch), lambda bi, g, i: (bi, 0, 0)),
            pl.BlockSpec((1, t, LANES), lambda bi, g, i: (bi, 0, 0)),
            pl.BlockSpec((1, t, LANES), lambda bi, g, i: (bi, 0, 1)),
            pl.BlockSpec((LANES, t), lambda bi, g, i: (g, bi)),
            pl.BlockSpec((LANES, t), lambda bi, g, i: (NSA_GROUPS + g, bi)),
            full2(expand_t), full2(overlap), full2(cpat), full2(wpat), full2(dpat),
        ],
        out_specs=pl.BlockSpec((NSA_HPG * NSA_DH, NSA_Q), lambda bi, g, i: (g, bi * nq + i)),
        out_shape=jax.ShapeDtypeStruct((NSA_W, n), F32),
        scratch_shapes=[pltpu.VMEM((2, LANES, NSA_HPG * NSA_Q), F32),
                        pltpu.VMEM((4, kt, NSA_HPG * NSA_Q), F32)],
        compiler_params=_cparams(("parallel", "parallel", "arbitrary"), 56),
        name="nsa_attention",
    )(qt, gt, kc, vct, okv, okv, vt, vt, expand_t, overlap, cpat, wpat, dpat)


def _split_bf16(a):
    hi = a.astype(BF16)
    return hi, (a - hi.astype(F32)).astype(BF16)


def _gdn_body(x_ref, z_ref, ab_ref, alog_ref, dtb_ref, og_ref, o_ref, s_sc, u_sc, wq_sc, kq_sc, gl_sc, *, ct):
    nb = x_ref.shape[0]
    ch = GDN_CHUNK
    n_units = (ct // ch) * nb * GDN_HEADS

    @pl.when(pl.program_id(0) == 0)
    def _():
        s_sc[...] = jnp.zeros(s_sc.shape, F32)
        u_sc[...] = jnp.zeros(u_sc.shape, F32)
        wq_sc[...] = jnp.zeros(wq_sc.shape, BF16)
        kq_sc[...] = jnp.zeros(kq_sc.shape, BF16)
        gl_sc[...] = jnp.ones(gl_sc.shape, F32)

    def scan_unit(i):
        ci, bi, h = i // (nb * GDN_HEADS), (i // GDN_HEADS) % nb, i % GDN_HEADS
        rows = slice(ci * ch, (ci + 1) * ch)
        hs = slice(h * GDN_DH, (h + 1) * GDN_DH)
        s_old = s_sc[bi * GDN_HEADS + h]
        from_state = _dot(wq_sc[i], s_old.astype(BF16))
        v_new = (u_sc[i] - from_state[:ch]).astype(BF16)
        from_v = _dot(kq_sc[i], v_new)
        s_sc[bi * GDN_HEADS + h] = s_old * gl_sc[i, 0:1, 0:1] + from_v[:GDN_DH]
        o = from_state[ch:] + from_v[GDN_DH:]
        on = o * lax.rsqrt(jnp.mean(o * o, axis=-1, keepdims=True) + EPS) * og_ref[...]
        zh = z_ref[bi, rows, hs].astype(F32)
        o_ref[bi, rows, hs] = (on * (zh * jax.nn.sigmoid(zh))).astype(BF16)

    r = lax.broadcasted_iota(jnp.int32, (ch, ch), 0)
    col = lax.broadcasted_iota(jnp.int32, (ch, ch), 1)
    incl = r >= col
    strict = r > col
    tril16 = jnp.concatenate([jnp.where(incl, 1.0, 0.0).astype(BF16)] * 3, axis=1)

    units, xs, ps = [], [], []

    def prepare(ci, bi):
        def run():
            rows = slice(ci * ch, (ci + 1) * ch)
            ab = ab_ref[bi, rows, :]
            g_all = -jnp.exp(alog_ref[...]) * jax.nn.softplus(ab + dtb_ref[...])
            beta_all = jax.nn.sigmoid(ab)
            g_hi, g_lo = _split_bf16(g_all)
            g_lo2 = (g_all - g_hi.astype(F32) - g_lo.astype(F32)).astype(BF16)
            gc_all = _dot(tril16, jnp.concatenate([g_hi, g_lo, g_lo2], axis=0))
            gc_t = gc_all.T
            for h in range(GDN_HEADS):
                hs = slice(h * GDN_DH, (h + 1) * GDN_DH)
                q16 = x_ref[bi, rows, hs]
                k16 = x_ref[bi, rows, GDN_W + h * GDN_DH:GDN_W + (h + 1) * GDN_DH]
                qh, kh = q16.astype(F32), k16.astype(F32)
                vh = x_ref[bi, rows, 2 * GDN_W + h * GDN_DH:2 * GDN_W + (h + 1) * GDN_DH].astype(F32)
                gc = gc_all[:, h:h + 1]
                gr = gc_t[h:h + 1, :]
                g_last = gc_all[ch - 1:ch, h:h + 1]
                beta = beta_all[:, GDN_HEADS + h:GDN_HEADS + h + 1]
                eg = jnp.exp(gc)
                decay = jnp.where(incl, jnp.exp(jnp.minimum(gc - gr, 0.0)), 0.0)
                kb = kh * beta
                with_k = (_dot_nt(jnp.concatenate([kb.astype(BF16), q16], axis=0), k16)
                          * jnp.concatenate([decay, decay], axis=0))
                units.append(dict(
                    rows=rows, bi=bi, h=h,
                    lmat=jnp.where(strict, with_k[:ch], 0.0),
                    vb_kbg=jnp.concatenate([(vh * beta).astype(BF16), (kb * eg).astype(BF16)], axis=1),
                    qk=jnp.where(incl, with_k[ch:], 0.0).astype(BF16),
                    qg=(qh * eg).astype(BF16), kd_t=(kh * jnp.exp(g_last - gc)).T.astype(BF16),
                    gl=jnp.exp(g_last)))
        return run

    def inverse_start():
        eye = jnp.where(r == col, 1.0, 0.0)
        for u in units:
            l16 = u["lmat"].astype(BF16)
            xs.append(eye - u["lmat"])
            ps.append(_dot(l16, l16))

    inverse_stages = int(np.log2(ch)) - 1

    def inverse_stage(s):
        for i in range(n_units):
            rhs = ps[i].astype(BF16)
            if s + 1 == inverse_stages:
                xs[i] = xs[i] + _dot(xs[i].astype(BF16), rhs)
            else:
                both = _dot(jnp.concatenate([xs[i], ps[i]], axis=0).astype(BF16), rhs)
                xs[i] = xs[i] + both[:ch]
                ps[i] = both[ch:]

    pieces = [prepare(ci, bi) for ci in range(ct // ch) for bi in range(nb)] + [inverse_start]
    pieces += [functools.partial(inverse_stage, s) for s in range(inverse_stages)]

    def store_unit(i):
        u = units[i]
        u_w = _dot(xs[i].astype(BF16), u["vb_kbg"])
        u_sc[i] = u_w[:, :GDN_DH]
        wq_sc[i] = jnp.concatenate([u_w[:, GDN_DH:].astype(BF16), u["qg"]], axis=0)
        kq_sc[i] = jnp.concatenate([u["kd_t"], u["qk"]], axis=0)
        gl_sc[i] = jnp.broadcast_to(u["gl"], gl_sc.shape[1:])

    for k in range(max(n_units, len(pieces))):
        if k < n_units:
            scan_unit(k)
        if k < len(pieces):
            pieces[k]()
    for i in range(n_units):
        store_unit(i)


def _gdn(oqkv, oz, oab, alog, dtb, og, ct=128):
    b, t, w3 = oqkv.shape
    steps = t // ct
    n_units = (ct // GDN_CHUNK) * b * GDN_HEADS
    full = lambda a: pl.BlockSpec(a.shape, lambda s: (0,) * a.ndim)
    prep = lambda w: pl.BlockSpec((b, ct, w), lambda s: (0, jnp.minimum(s, steps - 1), 0))
    scan = pl.BlockSpec((b, ct, GDN_W), lambda s: (0, jnp.maximum(s - 1, 0), 0))
    return pl.pallas_call(
        functools.partial(_gdn_body, ct=ct),
        grid=(steps + 1,),
        in_specs=[prep(w3), scan, prep(LANES), full(alog), full(dtb), full(og)],
        out_specs=scan,
        out_shape=jax.ShapeDtypeStruct((b, t, GDN_W), BF16),
        scratch_shapes=[pltpu.VMEM((b * GDN_HEADS, GDN_DH, GDN_DH), F32),
                        pltpu.VMEM((n_units, GDN_CHUNK, GDN_DH), F32),
                        pltpu.VMEM((n_units, 2 * GDN_CHUNK, GDN_DH), BF16),
                        pltpu.VMEM((n_units, GDN_DH + GDN_CHUNK, GDN_CHUNK), BF16),
                        pltpu.VMEM((n_units, 8, LANES), F32)],
        compiler_params=_cparams(("arbitrary",), 32),
        name="gdn",
    )(oqkv, oz, oab, alog, dtb, og)


def _outproj_body(ont_ref, og_ref, x_ref, ng_ref, wo_ref, fg_ref, wr_ref, br_ref, upper_ref,
                  x1_ref, h2_ref, gate_ref, route_ref, cnt_ref, cnt_sc):
    i = pl.program_id(0)
    tm = x_ref.shape[0]

    @pl.when(i == 0)
    def _():
        cnt_sc[...] = jnp.zeros(cnt_sc.shape, F32)

    a = ont_ref[...]
    a = (a * lax.rsqrt(jnp.mean(a * a, axis=0, keepdims=True) + EPS) * ng_ref[...]).astype(BF16)
    x1 = x_ref[...] + _dot_tn(a, wo_ref[0:NSA_W, :]) + _dot(og_ref[...], wo_ref[NSA_W:, :])
    x1_ref[...] = x1
    h2f = x1 * lax.rsqrt(jnp.mean(x1 * x1, axis=-1, keepdims=True) + EPS) * fg_ref[...]
    _store_pieces(h2_ref, h2f)
    h2 = h2f.astype(BF16)

    logits = (_dot(h2, wr_ref[...]) + br_ref[...]).T[:N_EXPERTS]
    erow = lax.broadcasted_iota(jnp.int32, (N_EXPERTS, tm), 0)
    onehot = jnp.zeros((N_EXPERTS, tm), F32)
    firsts, vals = [], []
    v = logits
    for k in range(TOP_K):
        mx = jnp.max(v, axis=0, keepdims=True)
        first = jnp.min(jnp.where(v == mx, erow, N_EXPERTS), axis=0, keepdims=True)
        hit = erow == first
        v = jnp.where(hit, -jnp.inf, v)
        onehot = jnp.where(hit, 1.0, onehot)
        firsts.append(first)
        vals.append(mx)
    vals = [jnp.exp(m - vals[0]) for m in vals]
    inv = 1.0 / (vals[0] + vals[1] + vals[2] + vals[3])
    gates_t = jnp.concatenate([m * inv for m in vals] + [jnp.zeros((LANES - TOP_K, tm), F32)], axis=0)
    gate_ref[...] = gates_t.T

    excl = cnt_sc[...] + _dot(onehot.astype(BF16), upper_ref[...])
    for k in range(TOP_K):
        route_ref[k:k + 1, :] = firsts[k]
        rank = jnp.sum(jnp.where(erow == firsts[k], excl, 0.0), axis=0, keepdims=True)
        route_ref[TOP_K + k:TOP_K + k + 1, :] = rank.astype(jnp.int32)
    cnt_sc[...] = cnt_sc[...] + jnp.sum(onehot, axis=1, keepdims=True)
    cnt_ref[...] = cnt_sc[...].astype(jnp.int32)


def _out_proj(o_nsa_t, o_gdn, x2, ng, wo, fg, wr, br, tm=512):
    upper = jnp.asarray(np.arange(tm)[:, None] < np.arange(tm)[None, :], BF16)
    n, d = x2.shape
    full = lambda a: pl.BlockSpec(a.shape, lambda i: (0,) * a.ndim)
    row = lambda w: pl.BlockSpec((tm, w), lambda i: (i, 0))
    return pl.pallas_call(
        _outproj_body,
        grid=(n // tm,),
        in_specs=[pl.BlockSpec((NSA_W, tm), lambda i: (0, i)), row(GDN_W), row(d), full(ng), full(wo), full(fg),
                  full(wr), full(br), full(upper)],
        out_specs=[row(d), pl.BlockSpec((d // 2 // SC_SUBROW, tm, SC_SUBROW), lambda i: (0, i, 0)),
                   row(LANES), pl.BlockSpec((2 * TOP_K, tm), lambda i: (0, i)),
                   pl.BlockSpec((N_EXPERTS, 1), lambda i: (0, 0))],
        out_shape=[jax.ShapeDtypeStruct((n, d), F32),
                   jax.ShapeDtypeStruct((d // 2 // SC_SUBROW, n, SC_SUBROW), jnp.int32),
                   jax.ShapeDtypeStruct((n, LANES), F32), jax.ShapeDtypeStruct((2 * TOP_K, n), jnp.int32),
                   jax.ShapeDtypeStruct((N_EXPERTS, 1), jnp.int32)],
        scratch_shapes=[pltpu.VMEM((N_EXPERTS, 1), F32)],
        compiler_params=_cparams(("arbitrary",), 48),
        name="out_proj_router",
    )(o_nsa_t, o_gdn, x2, ng, wo, fg, wr, br, upper)


def _dest_body(ps_ref, route_ref, o_ref, *, n_rows, pieces):
    expert = route_ref[0:TOP_K, :]
    start = jnp.zeros(expert.shape, jnp.int32)
    for e in range(N_EXPERTS):
        start = jnp.where(expert == e, ps_ref[e], start)
    dest = start + route_ref[TOP_K:2 * TOP_K, :]
    for k in range(TOP_K):
        for j in range(pieces):
            o_ref[k * pieces + j:k * pieces + j + 1, :] = dest[k:k + 1, :] + j * n_rows


def _dest_rows(pstarts, route, n_rows, pieces):
    n = route.shape[1]
    tn = min(2048, n)
    grid_spec = pltpu.PrefetchScalarGridSpec(
        num_scalar_prefetch=1,
        grid=(n // tn,),
        in_specs=[pl.BlockSpec((2 * TOP_K, tn), lambda i, ps: (0, i))],
        out_specs=pl.BlockSpec((TOP_K * pieces, tn), lambda i, ps: (0, i)),
    )
    return pl.pallas_call(
        functools.partial(_dest_body, n_rows=n_rows, pieces=pieces),
        grid_spec=grid_spec,
        out_shape=jax.ShapeDtypeStruct((TOP_K * pieces, n), jnp.int32),
        name="moe_dest_rows",
    )(pstarts, route)


def _expert_body(be_ref, end_ref, xs_ref, wg_hbm, bg_ref, wu_hbm, bu_ref, wd_hbm, bd_ref,
                 y_ref, wf32, w16, sems, slot_sc):
    i = pl.program_id(0)
    n_used = be_ref[pl.num_programs(0)]
    used = i < n_used
    expert = be_ref[i]
    fresh = used & ((i == 0) | (expert != be_ref[jnp.maximum(i - 1, 0)]))
    hbm = (wg_hbm, wu_hbm, wd_hbm)

    def weight_copy(e, slot, j):
        return pltpu.make_async_copy(hbm[j].at[e], wf32.at[slot, j], sems.at[slot, j])

    @pl.when(i == 0)
    def _():
        slot_sc[0] = 0

    @pl.when((i == 0) & used)
    def _():
        for j in range(3):
            weight_copy(expert, 0, j).start()

    @pl.when(fresh)
    def _():
        slot = slot_sc[0]
        for j in range(3):
            weight_copy(expert, slot, j).wait()
            w16[j] = wf32[slot, j].astype(BF16)
        following = end_ref[expert]

        @pl.when(following < n_used)
        def _():
            for j in range(3):
                weight_copy(be_ref[following], 1 - slot, j).start()

        slot_sc[0] = 1 - slot

    @pl.when(jnp.logical_not(used))
    def _():
        y_ref[...] = jnp.zeros(y_ref.shape, y_ref.dtype)

    @pl.when(used)
    def _():
        x = _join_pieces(xs_ref).astype(BF16)
        gate = jnp.minimum(_dot(x, w16[0]) + bg_ref[0], SWIGLU_LIMIT)
        up = jnp.clip(_dot(x, w16[1]) + bu_ref[0], -SWIGLU_LIMIT, SWIGLU_LIMIT)
        glu = gate * jax.nn.sigmoid(gate * SWIGLU_ALPHA)
        act = ((up + 1.0) * glu).astype(BF16)
        cols = 2 * y_ref.shape[2]
        for j in range(y_ref.shape[0]):
            sl = slice(j * cols, (j + 1) * cols)
            y_ref[j] = _pack_piece(_dot(act, w16[2, :, sl]) + bd_ref[0, :, sl])


def _experts(blk_e, end_blk, xs, wg, bg, wu, bu, wd, bd):
    pieces, n_rows, sub = xs.shape
    d, de = wg.shape[1], wg.shape[2]
    assert d == de
    r = MOE_ROW_BLOCK
    bspec = lambda w: pl.BlockSpec((1, 1, w), lambda i, be, *_: (be[i], 0, 0))
    hbm = pl.BlockSpec(memory_space=pl.ANY)
    grid_spec = pltpu.PrefetchScalarGridSpec(
        num_scalar_prefetch=2,
        grid=(n_rows // r,),
        in_specs=[pl.BlockSpec((pieces, r, sub), lambda i, *_: (0, i, 0)),
                  hbm, bspec(de), hbm, bspec(de), hbm, bspec(d)],
        out_specs=pl.BlockSpec((pieces, r, sub), lambda i, *_: (0, i, 0)),
        scratch_shapes=[pltpu.VMEM((2, 3, d, de), F32), pltpu.VMEM((3, d, de), BF16),
                        pltpu.SemaphoreType.DMA((2, 3)), pltpu.SMEM((1,), jnp.int32)],
    )
    return pl.pallas_call(
        _expert_body,
        grid_spec=grid_spec,
        out_shape=jax.ShapeDtypeStruct((pieces, n_rows, sub), jnp.int32),
        compiler_params=_cparams(("arbitrary",), 56),
        name="moe_experts",
    )(blk_e, end_blk, xs, wg, bg, wu, bu, wd, bd)


SC_WINDOW = 128
SC_SUBROW = 256


def _sc_mesh():
    return plsc.VectorSubcoreMesh(core_axis_name="c", subcore_axis_name="s")


def _sc_dispatch(h2, dest_rows, n_rows):
    n, d = h2.shape

    @functools.partial(pl.kernel, out_type=jax.ShapeDtypeStruct((n_rows, d), h2.dtype), mesh=_sc_mesh())
    def dispatch(x_hbm, *refs):
        idx_hbm, o_hbm = refs[:TOP_K], refs[TOP_K]

        def body(x_vmem, *idx_vmem):
            for iv in idx_vmem:
                pltpu.sync_copy(x_vmem, o_hbm.at[iv.at[0]])

        pltpu.emit_pipeline(
            body,
            grid=(n // SC_WINDOW,),
            in_specs=[pl.BlockSpec((SC_WINDOW, d), lambda i: (i, 0))]
                     + [pl.BlockSpec((1, SC_WINDOW), lambda i: (0, i))] * TOP_K,
            out_specs=[],
            core_axis_name=("c", "s"),
            dimension_semantics=(pltpu.PARALLEL,),
        )(x_hbm, *idx_hbm)

    return dispatch(h2, *dest_rows)


def _sc_gather(table, idx):
    _, d = table.shape
    m = idx.shape[1]

    @functools.partial(pl.kernel, out_type=jax.ShapeDtypeStruct((m, d), table.dtype), mesh=_sc_mesh())
    def gather(t_hbm, i_hbm, o_hbm):
        def body(i_vmem, o_vmem):
            pltpu.sync_copy(t_hbm.at[i_vmem.at[0]], o_vmem)

        pltpu.emit_pipeline(
            body,
            grid=(m // SC_WINDOW,),
            in_specs=[pl.BlockSpec((1, SC_WINDOW), lambda i: (0, i))],
            out_specs=[pl.BlockSpec((SC_WINDOW, d), lambda i: (i, 0))],
            core_axis_name=("c", "s"),
            dimension_semantics=(pltpu.PARALLEL,),
        )(i_hbm, o_hbm)

    return gather(table, idx)


def _combine_body(x1_ref, y_ref, gate_ref, o_ref):
    acc = x1_ref[...]
    for k in range(TOP_K):
        acc = acc + gate_ref[:, k:k + 1] * _join_pieces(y_ref.at[k])
    o_ref[...] = acc


def _combine(x1, y4, gates, tm=512):
    n, d = x1.shape
    pieces, sub = y4.shape[1], y4.shape[3]
    row = lambda w: pl.BlockSpec((tm, w), lambda i: (i, 0))
    return pl.pallas_call(
        _combine_body,
        grid=(n // tm,),
        in_specs=[row(d), pl.BlockSpec((TOP_K, pieces, tm, sub), lambda i: (0, 0, i, 0)), row(LANES)],
        out_specs=row(d),
        out_shape=jax.ShapeDtypeStruct((n, d), F32),
        compiler_params=_cparams(("parallel",), 48),
        name="moe_combine",
    )(x1, y4, gates)


def _pad_lanes(a, width=LANES):
    return jnp.pad(a, ((0, 0), (0, width - a.shape[1])))


def _layer(x, attn_norm_g, w_in, q_g, kc_g, ks_g, kw_g, ck_pos, ck_w1, ck_b1, ck_w2, ck_b2,
           cv_pos, cv_w1, cv_b1, cv_w2, cv_b2, nsa_out_g, conv_w, a_log, dt_bias, gdn_out_g, w_out,
           ffn_g, router_w, router_b, e_wg, e_bg, e_wu, e_bu, e_wd, e_bd):
    b, t, d = x.shape
    n = b * t
    x2 = x.reshape(n, d)

    o = np.cumsum([0, NSA_W] + [NSA_GROUPS * NSA_DH] * 6 + [3 * NSA_HEADS, 3 * GDN_W, GDN_W, GDN_HEADS, GDN_HEADS])
    wq_t = w_in[:, o[0]:o[1]].T.astype(BF16)
    qg_col = jnp.tile(q_g * (NSA_DH ** -0.5 * np.log2(np.e)), NSA_HEADS).reshape(NSA_W, 1)
    wkv = jnp.concatenate([w_in[:, o[1]:o[4]], w_in[:, o[5]:o[6]]], axis=1).astype(BF16)
    kg = jnp.concatenate([ks_g, ks_g, kw_g, kw_g]).reshape(1, 2 * LANES)
    wv_t = jnp.concatenate([w_in[:, o[4]:o[5]], w_in[:, o[6]:o[7]]], axis=1).T.reshape(2 * NSA_GROUPS, NSA_DH, d)
    wv_t = jnp.pad(wv_t, ((0, 0), (0, LANES - NSA_DH), (0, 0))).reshape(2 * NSA_GROUPS * LANES, d).astype(BF16)
    vone = jnp.asarray((np.arange(2 * NSA_GROUPS * LANES) % LANES == NSA_DH).astype(np.float32)[:, None])
    wg_t = w_in[:, o[7]:o[8]].T.reshape(NSA_GROUPS, NSA_HPG * 3, d)
    wg_t = jnp.pad(wg_t, ((0, 0), (0, GATE_ROWS - NSA_HPG * 3), (0, 0))).reshape(NSA_GROUPS * GATE_ROWS, d)
    wg_t = wg_t.astype(BF16)
    wab = _pad_lanes(w_in[:, o[10]:o[12]]).astype(BF16)
    wqkv = w_in[:, o[8]:o[9]].astype(BF16)
    wz = w_in[:, o[9]:o[10]].astype(BF16)

    tm = min(512, t)
    oqt, okn, xflat, ovt, ogt, oqkv, oz, oab = _in_proj(x2, attn_norm_g.reshape(1, d), wq_t, wkv, wv_t, wg_t, wqkv,
                                                        wz, wab, qg_col, kg, vone, conv_w, t // tm, tm)

    nch = t // CMP_STRIDE
    n_cmp = (t - CMP_BLOCK) // CMP_STRIDE + 1
    half = CMP_STRIDE * NSA_DH
    pos = jnp.stack([ck_pos, cv_pos]).reshape(2, 2, 1, half)
    w1 = jnp.stack([ck_w1, cv_w1]).reshape(2, 2, half, CMP_HIDDEN).astype(BF16)
    b1 = jnp.stack([ck_b1, cv_b1]).reshape(2, 1, CMP_HIDDEN)
    w2 = jnp.stack([ck_w2, cv_w2]).astype(BF16)
    b2 = jnp.stack([ck_b2, cv_b2]).reshape(2, 1, NSA_DH)
    w2t = jnp.stack([ck_w2.T, cv_w2.T]).astype(BF16)
    b2t = jnp.stack([ck_b2, cv_b2]).reshape(2, NSA_DH, 1)
    kc, vct = _compress(xflat, pos, w1, b1, w2, b2, w2t, b2t, kc_g.reshape(1, NSA_DH), n_cmp)

    n_slc = t // SLC_BLOCK
    n_top = min(SLC_TOPK, n_slc)
    assert n_top > 3
    nblk = max(n_slc, LANES)
    kt = min(256, t // 4)
    assert (t // kt) % 4 == 0
    ci = np.arange(nch)[None, :] * CMP_STRIDE
    sj = np.arange(nblk)[:, None] * SLC_BLOCK
    overlap = ((ci < sj + SLC_BLOCK) & (ci + CMP_BLOCK > sj) & (np.arange(nch)[None, :] < n_cmp)
               & (np.arange(nblk)[:, None] < n_slc))
    expand_t = (np.arange(t)[:, None] // SLC_BLOCK) == np.arange(nblk)[None, :]
    o_nsa_t = _nsa_attention(oqt, ogt, kc, vct, okn.reshape(b, t, -1), ovt, jnp.asarray(expand_t, BF16),
                             jnp.asarray(overlap, BF16), b, t, n_top, kt)

    alog_row = _pad_lanes(a_log.reshape(1, GDN_HEADS))
    dtb_row = _pad_lanes(dt_bias.reshape(1, GDN_HEADS))
    o_gdn = _gdn(oqkv.reshape(b, t, -1), oz.reshape(b, t, -1), oab.reshape(b, t, -1),
                 alog_row, dtb_row, gdn_out_g.reshape(1, GDN_DH))

    wr = _pad_lanes(router_w).astype(BF16)
    br = _pad_lanes(router_b.reshape(1, N_EXPERTS))
    x1, h2, gates, route, counts = _out_proj(
        o_nsa_t, o_gdn.reshape(n, GDN_W), x2, nsa_out_g.reshape(NSA_W, 1),
        w_out.astype(BF16), ffn_g.reshape(1, d), wr, br)

    r = MOE_ROW_BLOCK
    nk = n * TOP_K
    counts = counts[:, 0]
    pcounts = (counts + r - 1) // r * r
    pends = jnp.cumsum(pcounts)
    pstarts = pends - pcounts
    n_rows = (nk + r - 1) // r * r + N_EXPERTS * r
    n_blocks = n_rows // r
    blk_start = jnp.arange(n_blocks, dtype=jnp.int32)[:, None] * r
    blk_e = jnp.minimum(jnp.sum(pends[None, :] <= blk_start, axis=1), N_EXPERTS - 1).astype(jnp.int32)
    n_used = (pends[-1] // r).astype(jnp.int32)
    blk_e = jnp.concatenate([blk_e, n_used[None]])
    end_blk = (pends // r).astype(jnp.int32)
    pieces = d // 2 // SC_SUBROW
    dest_p = _dest_rows(pstarts.astype(jnp.int32), route, n_rows, pieces).reshape(TOP_K, pieces, n)
    xs = _sc_dispatch(h2.reshape(pieces * n, SC_SUBROW), [dest_p[k].reshape(1, pieces * n) for k in range(TOP_K)],
                      pieces * n_rows)
    ys = _experts(blk_e, end_blk, xs.reshape(pieces, n_rows, SC_SUBROW), e_wg, e_bg.reshape(N_EXPERTS, 1, -1),
                  e_wu, e_bu.reshape(N_EXPERTS, 1, -1), e_wd, e_bd.reshape(N_EXPERTS, 1, -1))
    y4 = _sc_gather(ys.reshape(pieces * n_rows, SC_SUBROW), dest_p.reshape(1, nk * pieces))
    return _combine(x1, y4.reshape(TOP_K, pieces, n, SC_SUBROW), gates).reshape(b, t, d)


def kernel(x, attn_norm_g, w_in, nsa_q_norm_g, nsa_kc_norm_g, nsa_ks_norm_g, nsa_kw_norm_g, cmp_k_pos, cmp_k_w1, cmp_k_b1, cmp_k_w2, cmp_k_b2, cmp_v_pos, cmp_v_w1, cmp_v_b1, cmp_v_w2, cmp_v_b2, nsa_out_norm_g, gdn_conv_w, gdn_a_log, gdn_dt_bias, gdn_out_norm_g, w_out, ffn_norm_g, router_w, router_b, exp_w_gate, exp_b_gate, exp_w_up, exp_b_up, exp_w_down, exp_b_down):
    params = (attn_norm_g, w_in, nsa_q_norm_g, nsa_kc_norm_g, nsa_ks_norm_g, nsa_kw_norm_g,
              cmp_k_pos, cmp_k_w1, cmp_k_b1, cmp_k_w2, cmp_k_b2, cmp_v_pos, cmp_v_w1, cmp_v_b1, cmp_v_w2, cmp_v_b2,
              nsa_out_norm_g, gdn_conv_w, gdn_a_log, gdn_dt_bias, gdn_out_norm_g, w_out, ffn_norm_g,
              router_w, router_b, exp_w_gate, exp_b_gate, exp_w_up, exp_b_up, exp_w_down, exp_b_down)
    for l in range(attn_norm_g.shape[0]):
        x = _layer(x, *(p[l] for p in params))
    return x
```

```python
import functools

import jax
import jax.numpy as jnp
import numpy as np
from jax import lax
from jax.experimental import pallas as pl
from jax.experimental.pallas import tpu as pltpu
from jax.experimental.pallas import tpu_sc as plsc

F32 = jnp.float32
BF16 = jnp.bfloat16

EPS = 1e-6
NEG = -1e30
MASKED = -2.0 ** 100

NSA_HEADS = 8
NSA_GROUPS = 2
NSA_HPG = 4
NSA_DH = 64
CMP_BLOCK = 32
CMP_STRIDE = 16
CMP_HIDDEN = 256
SLC_BLOCK = 64
SLC_TOPK = 16
WINDOW = 512
NSA_Q = 256
GDN_HEADS = 4
GDN_DH = 128
GDN_CHUNK = 64
N_EXPERTS = 32
TOP_K = 4
SWIGLU_LIMIT = 7.0
SWIGLU_ALPHA = 1.702
MOE_ROW_BLOCK = 256

LANES = 128
GATE_ROWS = 16
FLASH_BODY_TILES = (8, 4)
NSA_W = NSA_HEADS * NSA_DH
GDN_W = GDN_HEADS * GDN_DH

_NT = (((1,), (1,)), ((), ()))
_TN = (((0,), (0,)), ((), ()))


def _cparams(sem, vmem_mb):
    return pltpu.CompilerParams(dimension_semantics=sem, vmem_limit_bytes=vmem_mb * 1024 * 1024)


def _dot(a, b):
    return jnp.dot(a, b, preferred_element_type=F32)


def _dot_nt(a, b):
    return lax.dot_general(a, b, _NT, preferred_element_type=F32)


def _dot_tn(a, b):
    return lax.dot_general(a, b, _TN, preferred_element_type=F32)


def _pack_piece(block):
    words = block.shape[1] // 2
    hi = lax.bitcast_convert_type(block[:, :words].astype(BF16).astype(F32), jnp.uint32)
    lo = lax.bitcast_convert_type(block[:, words:].astype(BF16).astype(F32), jnp.uint32)
    return lax.bitcast_convert_type(hi | (lo >> 16), jnp.int32)


def _store_pieces(ref, val):
    cols = 2 * ref.shape[2]
    for j in range(ref.shape[0]):
        ref[j] = _pack_piece(val[:, j * cols:(j + 1) * cols])


def _join_pieces(ref):
    out = []
    for j in range(ref.shape[0]):
        words = lax.bitcast_convert_type(ref[j], jnp.uint32)
        out.append(lax.bitcast_convert_type(words & jnp.uint32(0xFFFF0000), F32))
        out.append(lax.bitcast_convert_type(words << 16, F32))
    return jnp.concatenate(out, axis=1)


def _inproj_body(x_ref, g_ref, wqt_ref, wkv_ref, wvt_ref, wgt_ref, wqkv_ref, wz_ref, wab_ref, qg_ref, kg_ref,
                 vone_ref, cw_ref, oqt_ref, okn_ref, ocf_ref, ovt_ref, ogt_ref, oqkv_ref, oz_ref, oab_ref, ybuf, cbuf,
                 *, tiles_per_seq):
    tm = x_ref.shape[0]

    halo = ybuf.shape[0] - tm
    first = pl.program_id(0) % tiles_per_seq == 0

    @pl.when(first)
    def _():
        ybuf[0:halo, :] = jnp.zeros((halo, ybuf.shape[1]), F32)

    @pl.when(jnp.logical_not(first))
    def _():
        ybuf[0:halo, :] = ybuf[tm:tm + halo, :]

    x = x_ref[...]
    h = (x * lax.rsqrt(jnp.mean(x * x, axis=-1, keepdims=True) + EPS) * g_ref[...]).astype(BF16)
    ybuf[halo:halo + tm, :] = _dot(h, wqkv_ref[...])

    yq = _dot_nt(wqt_ref[...], h)
    for s in range(NSA_HEADS):
        sl = slice(s * NSA_DH, (s + 1) * NSA_DH)
        ys = yq[sl, :]
        ms = jnp.sum(ys * ys, axis=0, keepdims=True) * (1.0 / NSA_DH)
        oqt_ref[sl, :] = (ys * lax.rsqrt(ms + EPS) * qg_ref[sl, :]).astype(BF16)

    ykv = _dot(h, wkv_ref[...])
    lane = lax.broadcasted_iota(jnp.int32, (tm, LANES), 1)
    low = lane < NSA_DH
    for s in range(2):
        sl = slice(s * LANES, (s + 1) * LANES)
        ys = ykv[:, (2 + s) * LANES:(3 + s) * LANES]
        y2 = ys * ys
        s0 = jnp.sum(jnp.where(low, y2, 0.0), axis=-1, keepdims=True)
        s1 = jnp.sum(jnp.where(low, 0.0, y2), axis=-1, keepdims=True)
        ms = jnp.where(low, s0, s1) * (1.0 / NSA_DH)
        okn_ref[:, sl] = (ys * lax.rsqrt(ms + EPS) * kg_ref[:, sl]).astype(BF16)

    chunks = tm // CMP_STRIDE
    for br in range(2):
        cbuf[br] = ykv[:, br * LANES:(br + 1) * LANES]
        taken = [cbuf[br, pl.ds(l, chunks, stride=CMP_STRIDE), :] for l in range(CMP_STRIDE)]
        for grp in range(NSA_GROUPS):
            flat = jnp.concatenate([r[:, grp * NSA_DH:(grp + 1) * NSA_DH] for r in taken], axis=1)
            ocf_ref[0, br, grp] = flat.astype(BF16)

    ovt_ref[...] = (_dot_nt(wvt_ref[...], h) + vone_ref[...]).astype(BF16)
    ogt_ref[...] = _dot_nt(wgt_ref[...], h)
    oz_ref[...] = _dot(h, wz_ref[...]).astype(BF16)
    oab_ref[...] = _dot(h, wab_ref[...])

    taps = cw_ref.shape[0]
    y = cw_ref[0:1, :] * ybuf[pl.ds(halo - taps + 1, tm), :]
    for k in range(1, taps):
        y = y + cw_ref[k:k + 1, :] * ybuf[pl.ds(halo - taps + 1 + k, tm), :]
    hy = 0.5 * y
    y = hy + hy * jnp.tanh(hy)
    for s in range(3 * GDN_HEADS):
        sl = slice(s * GDN_DH, (s + 1) * GDN_DH)
        ys = y[:, sl]
        if s < 2 * GDN_HEADS:
            scale = GDN_DH ** -0.5 if s < GDN_HEADS else 1.0
            ys = ys * (lax.rsqrt(jnp.sum(ys * ys, axis=-1, keepdims=True) + EPS) * scale)
        oqkv_ref[:, sl] = ys.astype(BF16)


def _in_proj(x2, g, wqt, wkv, wvt, wgt, wqkv, wz, wab, qg, kg, vone, conv_w, tiles_per_seq, tm):
    n, d = x2.shape
    chunks, flat = tm // CMP_STRIDE, CMP_STRIDE * NSA_DH
    full = lambda a: pl.BlockSpec(a.shape, lambda i: (0,) * a.ndim)
    row = lambda w: pl.BlockSpec((tm, w), lambda i: (i, 0))
    colb = lambda r: pl.BlockSpec((r, tm), lambda i: (0, i))
    return pl.pallas_call(
        functools.partial(_inproj_body, tiles_per_seq=tiles_per_seq),
        grid=(n // tm,),
        in_specs=[row(d)] + [full(a) for a in (g, wqt, wkv, wvt, wgt, wqkv, wz, wab, qg, kg, vone, conv_w)],
        out_specs=[colb(wqt.shape[0]), row(2 * LANES),
                   pl.BlockSpec((1, 2, NSA_GROUPS, chunks, flat), lambda i: (i // tiles_per_seq, 0, 0, i % tiles_per_seq, 0)),
                   colb(wvt.shape[0]), colb(wgt.shape[0]), row(wqkv.shape[1]), row(wz.shape[1]), row(wab.shape[1])],
        out_shape=[jax.ShapeDtypeStruct((wqt.shape[0], n), BF16), jax.ShapeDtypeStruct((n, 2 * LANES), BF16),
                   jax.ShapeDtypeStruct((n // (tm * tiles_per_seq), 2, NSA_GROUPS, chunks * tiles_per_seq, flat), BF16),
                   jax.ShapeDtypeStruct((wvt.shape[0], n), BF16), jax.ShapeDtypeStruct((wgt.shape[0], n), F32),
                   jax.ShapeDtypeStruct((n, wqkv.shape[1]), BF16), jax.ShapeDtypeStruct((n, wz.shape[1]), BF16),
                   jax.ShapeDtypeStruct((n, wab.shape[1]), F32)],
        scratch_shapes=[pltpu.VMEM((tm + 8, wqkv.shape[1]), F32), pltpu.VMEM((2, tm, LANES), F32)],
        compiler_params=_cparams(("arbitrary",), 56),
        name="in_proj",
    )(x2, g, wqt, wkv, wvt, wgt, wqkv, wz, wab, qg, kg, vone, conv_w)


def _compress_body(x_ref, pos_ref, w1_ref, b1_ref, w2_ref, b2_ref, w2t_ref, b2t_ref, g_ref, ok_ref, ovt_ref,
                   *, n_cmp):
    is_key = pl.program_id(1) == 0
    nch = x_ref.shape[3]
    hids = []
    for grp in range(NSA_GROUPS):
        x = x_ref[0, 0, grp].astype(F32)
        xa = (x + pos_ref[0, 0]).astype(BF16)
        xb = (x + pos_ref[0, 1]).astype(BF16)
        a = _dot(xa, w1_ref[0, 0])
        b = _dot(xb, w1_ref[0, 1])
        b_next = pltpu.roll(b, nch - 1, 0)
        hids.append(jax.nn.gelu(a + b_next + b1_ref[0]).astype(BF16))

    @pl.when(is_key)
    def _():
        row = lax.broadcasted_iota(jnp.int32, (nch, NSA_DH), 0)
        outs = []
        for grp in range(NSA_GROUPS):
            out = _dot(hids[grp], w2_ref[0]) + b2_ref[0]
            out = out * lax.rsqrt(jnp.mean(out * out, axis=-1, keepdims=True) + EPS) * g_ref[...]
            outs.append(jnp.where(row < n_cmp, out, 0.0))
        ok_ref[0] = jnp.concatenate(outs, axis=-1).astype(BF16)

    @pl.when(jnp.logical_not(is_key))
    def _():
        col = lax.broadcasted_iota(jnp.int32, (NSA_DH, nch), 1)
        outs = []
        for grp in range(NSA_GROUPS):
            out = _dot_nt(w2t_ref[0], hids[grp]) + b2t_ref[0]
            outs.append(jnp.where(col < n_cmp, out, 0.0))
        ovt_ref[0] = jnp.concatenate(outs, axis=0).astype(BF16)


def _compress(xflat, pos, w1, b1, w2, b2, w2t, b2t, kc_g, n_cmp):
    b, _, _, nch, flat = xflat.shape
    return pl.pallas_call(
        functools.partial(_compress_body, n_cmp=n_cmp),
        grid=(b, 2),
        in_specs=[
            pl.BlockSpec((1, 1, NSA_GROUPS, nch, flat), lambda i, j: (i, j, 0, 0, 0)),
            pl.BlockSpec((1, 2, 1, flat), lambda i, j: (j, 0, 0, 0)),
            pl.BlockSpec((1, 2, flat, CMP_HIDDEN), lambda i, j: (j, 0, 0, 0)),
            pl.BlockSpec((1, 1, CMP_HIDDEN), lambda i, j: (j, 0, 0)),
            pl.BlockSpec((1, CMP_HIDDEN, NSA_DH), lambda i, j: (j, 0, 0)),
            pl.BlockSpec((1, 1, NSA_DH), lambda i, j: (j, 0, 0)),
            pl.BlockSpec((1, NSA_DH, CMP_HIDDEN), lambda i, j: (j, 0, 0)),
            pl.BlockSpec((1, NSA_DH, 1), lambda i, j: (j, 0, 0)),
            pl.BlockSpec((1, NSA_DH), lambda i, j: (0, 0)),
        ],
        out_specs=[pl.BlockSpec((1, nch, LANES), lambda i, j: (i, 0, 0)),
                   pl.BlockSpec((1, LANES, nch), lambda i, j: (i, 0, 0))],
        out_shape=[jax.ShapeDtypeStruct((b, nch, LANES), BF16), jax.ShapeDtypeStruct((b, LANES, nch), BF16)],
        compiler_params=_cparams(("parallel", "arbitrary"), 32),
        name="nsa_compress",
    )(xflat, pos, w1, b1, w2, b2, w2t, b2t, kc_g)


def _tile_heads(a):
    return jnp.concatenate([a] * NSA_HPG, axis=1)


def _nsa_body(qt_ref, qn_ref, gt_ref, kc_ref, vct_ref, ks_ref, kw_ref, vst_ref, vwt_ref, et_ref, ov_ref, cpat_ref,
              wpat_ref, dpat_ref, o_ref, acc_sc, s_sc, oc_sc, ch_sc, *, n_top, kt, nq):
    grp = pl.program_id(1)
    step = pl.program_id(2)
    s0 = step * NSA_Q
    nch = kc_ref.shape[1]
    nblk = ov_ref.shape[0]
    blk = lax.broadcasted_iota(jnp.int32, (nblk, NSA_Q), 0)
    rounds_left = n_top - 3
    quarter = rounds_left // 4

    def padded_q(ref):
        qh = jnp.concatenate([ref[h * NSA_DH:(h + 1) * NSA_DH, :] for h in range(NSA_HPG)], axis=1)
        zq = jnp.zeros_like(qh)
        return jnp.where(grp == 0, jnp.concatenate([qh, zq], axis=0), jnp.concatenate([zq, qh], axis=0))

    def pick_rounds(v, rounds):
        for _ in range(rounds):
            mx = jnp.max(v, axis=0, keepdims=True)
            first = jnp.min(jnp.where(v == mx, blk, nblk), axis=0, keepdims=True)
            v = jnp.where(blk == first, -jnp.inf, v)
        return v

    def selection_pieces(q, start):
        t_r = start + lax.broadcasted_iota(jnp.int32, (1, NSA_Q), 1)
        cur_r = t_r // SLC_BLOCK
        st = {}

        def compressed_scores():
            cbias = cpat_ref[pl.ds(pl.multiple_of(nch - start // CMP_STRIDE, CMP_STRIDE), nch), :]
            st["sc"] = _dot(kc_ref[0], q) + _tile_heads(cbias)

        def compressed_softmax():
            sc = st["sc"]
            pc = jnp.exp2(sc - jnp.max(sc, axis=0, keepdims=True)).astype(BF16)
            stacked = jnp.concatenate([vct_ref[0], ov_ref[...], jnp.ones((8, nch), BF16)], axis=0)
            res = _dot(stacked, pc)
            inv = jnp.where(_tile_heads(t_r >= CMP_BLOCK - 1),
                            1.0 / jnp.maximum(res[LANES + nblk:LANES + nblk + 1], 1e-30), 0.0)
            oc_sc[1] = res[:LANES] * inv
            imp4 = res[LANES:LANES + nblk] * inv
            imp = (imp4[:, 0:NSA_Q] + imp4[:, NSA_Q:2 * NSA_Q] + imp4[:, 2 * NSA_Q:3 * NSA_Q]
                   + imp4[:, 3 * NSA_Q:4 * NSA_Q])
            imp = jnp.where(blk * SLC_BLOCK > t_r, NEG, imp)
            st["v"] = jnp.where((blk == 0) | (blk == cur_r) | (blk == cur_r - 1), -jnp.inf, imp)

        def rounds(count):
            def run():
                st["v"] = pick_rounds(st["v"], count)
            return run

        def finish():
            ch_sc[1] = jnp.where(st["v"] == -jnp.inf, 1.0, 0.0)

        return [compressed_scores, compressed_softmax, rounds(quarter), rounds(quarter), rounds(quarter),
                rounds(rounds_left - 3 * quarter), finish]

    @pl.when(step == 0)
    def _():
        for piece in selection_pieces(padded_q(qt_ref), s0):
            piece()

    oc_sc[0] = oc_sc[1]
    ch_sc[0] = ch_sc[1]
    nxt = selection_pieces(padded_q(qn_ref), jnp.minimum(step + 1, nq - 1) * NSA_Q)

    qt = padded_q(qt_ref)
    t_row = s0 + lax.broadcasted_iota(jnp.int32, (1, NSA_Q), 1)
    cur = t_row // SLC_BLOCK
    chosen = ch_sc[0] > 0.5

    selb = jnp.where(chosen & (blk * SLC_BLOCK < s0), 0.0, MASKED).astype(BF16)
    rhs = jnp.concatenate([qt, _tile_heads(selb)], axis=0)
    last_tile = ks_ref.shape[1] // kt - 1

    def scores(idx, slot):
        k0 = pl.multiple_of(jnp.minimum(idx, last_tile) * kt, kt)
        lhs = jnp.concatenate([ks_ref[0, pl.ds(k0, kt), :], et_ref[pl.ds(k0, kt), :]], axis=1)
        s_sc[slot] = _dot(lhs, rhs)

    scores(0, 0)
    scores(1, 1)
    nxt[0]()
    wlen = WINDOW + NSA_Q
    w0 = pl.multiple_of(jnp.maximum(s0 - WINDOW, 0), NSA_Q)
    wbias = wpat_ref[pl.ds(pl.multiple_of(w0 - s0 + WINDOW, NSA_Q), wlen), :]
    sw = _dot(kw_ref[0, pl.ds(w0, wlen), :], qt) + _tile_heads(wbias)
    nxt[1]()

    d0 = pl.multiple_of(s0, NSA_Q)
    selb_d = jnp.where(chosen & (blk <= cur), 0.0, MASKED).astype(BF16)
    sd = (_dot(jnp.concatenate([ks_ref[0, pl.ds(d0, NSA_Q), :], et_ref[pl.ds(d0, NSA_Q), :]], axis=1),
               jnp.concatenate([qt, _tile_heads(selb_d)], axis=0))
          + _tile_heads(dpat_ref[...]))
    m_diag = jnp.max(sd, axis=0, keepdims=True)
    acc_sc[0] = _dot(vst_ref[:, pl.ds(d0, NSA_Q)], jnp.exp2(sd - m_diag).astype(BF16))
    acc_sc[1] = jnp.zeros(acc_sc.shape[1:], F32)

    nxt[2]()
    pw = jnp.exp2(sw - jnp.max(sw, axis=0, keepdims=True)).astype(BF16)
    nxt[3]()
    ow = _dot(vwt_ref[:, pl.ds(w0, wlen)], pw)
    ow = ow[:NSA_DH] / ow[NSA_DH:NSA_DH + 1]
    for piece in nxt[4:]:
        piece()

    def update(idx, slot, m_old, acc_ref):
        k0 = pl.multiple_of(idx * kt, kt)
        m_new = jnp.maximum(m_old, jnp.max(s_sc[slot], axis=0, keepdims=True))
        p = jnp.exp2(s_sc[slot] - m_new).astype(BF16)
        acc_ref[...] = jnp.exp2(m_old - m_new) * acc_ref[...] + _dot(vst_ref[:, pl.ds(k0, kt)], p)
        return m_new

    def tile_group(first, carry, count):
        ms = list(carry)
        for t in range(count):
            scores(first + t + 2, (t + 2) % 4)
            ms[t % 2] = update(first + t, t % 4, ms[t % 2], acc_sc.at[t % 2])
        return tuple(ms)

    n_tiles = (s0 + kt - 1) // kt
    carry = (m_diag, jnp.full((1, NSA_HPG * NSA_Q), NEG, F32))
    done = 0
    for size in FLASH_BODY_TILES:
        left = n_tiles - done
        groups = (left + size - 1) // size if size == FLASH_BODY_TILES[-1] else left // size
        carry = lax.fori_loop(0, groups, lambda j, c, done=done, size=size: tile_group(done + size * j, c, size),
                              carry)
        done = done + size * groups
    m0, m1 = carry
    m_fin = jnp.maximum(m0, m1)
    acc = acc_sc[0] * jnp.exp2(m0 - m_fin) + acc_sc[1] * jnp.exp2(m1 - m_fin)
    osl = acc[:NSA_DH] / acc[NSA_DH:NSA_DH + 1]

    oc = jnp.where(grp == 0, oc_sc[0, :NSA_DH], oc_sc[0, NSA_DH:])
    gts = jax.nn.sigmoid(gt_ref[...])
    for h in range(NSA_HPG):
        cols = slice(h * NSA_Q, (h + 1) * NSA_Q)
        o_ref[h * NSA_DH:(h + 1) * NSA_DH, :] = (
            gts[3 * h:3 * h + 1, :] * oc[:, cols] + gts[3 * h + 1:3 * h + 2, :] * osl[:, cols]
            + gts[3 * h + 2:3 * h + 3, :] * ow[:, cols])


def _mask_pattern(valid):
    return jnp.asarray(np.where(valid, 0.0, NEG), F32)


def _nsa_attention(qt, gt, kc, vct, okv, vt, expand_t, overlap, b, t, n_top, kt):
    nch = kc.shape[1]
    nq = t // NSA_Q
    n = b * t
    ql = np.arange(NSA_Q)[None, :]
    rc = np.arange(2 * nch)[:, None] - nch
    cpat = _mask_pattern(rc * CMP_STRIDE + CMP_BLOCK - 1 <= ql)
    rw = np.arange(2 * WINDOW + NSA_Q)[:, None] - WINDOW
    wpat = _mask_pattern((rw <= ql) & (rw > ql - WINDOW))
    dpat = _mask_pattern(np.arange(NSA_Q)[:, None] <= ql)
    full2 = lambda a: pl.BlockSpec(a.shape, lambda bi, g, i: (0, 0))
    return pl.pallas_call(
        functools.partial(_nsa_body, n_top=n_top, kt=kt, nq=nq),
        grid=(b, NSA_GROUPS, nq),
        in_specs=[
            pl.BlockSpec((NSA_HPG * NSA_DH, NSA_Q), lambda bi, g, i: (g, bi * nq + i)),
            pl.BlockSpec((NSA_HPG * NSA_DH, NSA_Q), lambda bi, g, i: (g, bi * nq + jnp.minimum(i + 1, nq - 1))),
            pl.BlockSpec((GATE_ROWS, NSA_Q), lambda bi, g, i: (g, bi * nq + i)),
            pl.BlockSpec((1, nch, LANES), lambda bi, g, i: (bi, 0, 0)),
            pl.BlockSpec((1, LANES, nch), lambda bi, g, i: (bi, 0, 0)),
            pl.BlockSpec((1, t, LANES), lambda bi, g, i: (bi, 0, 0)),
            pl.BlockSpec((1, t, LANES), lambda bi, g, i: (bi, 0, 1)),
            pl.BlockSpec((LANES, t), lambda bi, g, i: (g, bi)),
            pl.BlockSpec((LANES, t), lambda bi, g, i: (NSA_GROUPS + g, bi)),
            full2(expand_t), full2(overlap), full2(cpat), full2(wpat), full2(dpat),
        ],
        out_specs=pl.BlockSpec((NSA_HPG * NSA_DH, NSA_Q), lambda bi, g, i: (g, bi * nq + i)),
        out_shape=jax.ShapeDtypeStruct((NSA_W, n), F32),
        scratch_shapes=[pltpu.VMEM((2, LANES, NSA_HPG * NSA_Q), F32),
                        pltpu.VMEM((4, kt, NSA_HPG * NSA_Q), F32),
                        pltpu.VMEM((2, LANES, NSA_HPG * NSA_Q), F32),
                        pltpu.VMEM((2, overlap.shape[0], NSA_Q), F32)],
        compiler_params=_cparams(("parallel", "parallel", "arbitrary"), 56),
        name="nsa_attention",
    )(qt, qt, gt, kc, vct, okv, okv, vt, vt, expand_t, overlap, cpat, wpat, dpat)


def _split_bf16(a):
    hi = a.astype(BF16)
    return hi, (a - hi.astype(F32)).astype(BF16)


def _gdn_body(x_ref, z_ref, ab_ref, alog_ref, dtb_ref, og_ref, o_ref, s_sc, u_sc, wq_sc, kq_sc, gl_sc, *, ct):
    nb = x_ref.shape[0]
    ch = GDN_CHUNK
    n_units = (ct // ch) * nb * GDN_HEADS

    @pl.when(pl.program_id(0) == 0)
    def _():
        s_sc[...] = jnp.zeros(s_sc.shape, F32)
        u_sc[...] = jnp.zeros(u_sc.shape, F32)
        wq_sc[...] = jnp.zeros(wq_sc.shape, BF16)
        kq_sc[...] = jnp.zeros(kq_sc.shape, BF16)
        gl_sc[...] = jnp.ones(gl_sc.shape, F32)

    def scan_unit(i):
        ci, bi, h = i // (nb * GDN_HEADS), (i // GDN_HEADS) % nb, i % GDN_HEADS
        rows = slice(ci * ch, (ci + 1) * ch)
        hs = slice(h * GDN_DH, (h + 1) * GDN_DH)
        s_old = s_sc[bi * GDN_HEADS + h]
        from_state = _dot(wq_sc[i], s_old.astype(BF16))
        v_new = (u_sc[i] - from_state[:ch]).astype(BF16)
        from_v = _dot(kq_sc[i], v_new)
        s_sc[bi * GDN_HEADS + h] = s_old * gl_sc[i, 0:1, 0:1] + from_v[:GDN_DH]
        o = from_state[ch:] + from_v[GDN_DH:]
        on = o * lax.rsqrt(jnp.mean(o * o, axis=-1, keepdims=True) + EPS) * og_ref[...]
        zh = z_ref[bi, rows, hs].astype(F32)
        o_ref[bi, rows, hs] = (on * (zh * jax.nn.sigmoid(zh))).astype(BF16)

    r = lax.broadcasted_iota(jnp.int32, (ch, ch), 0)
    col = lax.broadcasted_iota(jnp.int32, (ch, ch), 1)
    incl = r >= col
    strict = r > col
    tril16 = jnp.concatenate([jnp.where(incl, 1.0, 0.0).astype(BF16)] * 3, axis=1)

    units, xs, ps = [], [], []

    def prepare(ci, bi):
        def run():
            rows = slice(ci * ch, (ci + 1) * ch)
            ab = ab_ref[bi, rows, :]
            g_all = -jnp.exp(alog_ref[...]) * jax.nn.softplus(ab + dtb_ref[...])
            beta_all = jax.nn.sigmoid(ab)
            g_hi, g_lo = _split_bf16(g_all)
            g_lo2 = (g_all - g_hi.astype(F32) - g_lo.astype(F32)).astype(BF16)
            gc_all = _dot(tril16, jnp.concatenate([g_hi, g_lo, g_lo2], axis=0))
            gc_t = gc_all.T
            for h in range(GDN_HEADS):
                hs = slice(h * GDN_DH, (h + 1) * GDN_DH)
                q16 = x_ref[bi, rows, hs]
                k16 = x_ref[bi, rows, GDN_W + h * GDN_DH:GDN_W + (h + 1) * GDN_DH]
                qh, kh = q16.astype(F32), k16.astype(F32)
                vh = x_ref[bi, rows, 2 * GDN_W + h * GDN_DH:2 * GDN_W + (h + 1) * GDN_DH].astype(F32)
                gc = gc_all[:, h:h + 1]
                gr = gc_t[h:h + 1, :]
                g_last = gc_all[ch - 1:ch, h:h + 1]
                beta = beta_all[:, GDN_HEADS + h:GDN_HEADS + h + 1]
                eg = jnp.exp(gc)
                decay = jnp.where(incl, jnp.exp(jnp.minimum(gc - gr, 0.0)), 0.0)
                kb = kh * beta
                with_k = (_dot_nt(jnp.concatenate([kb.astype(BF16), q16], axis=0), k16)
                          * jnp.concatenate([decay, decay], axis=0))
                units.append(dict(
                    rows=rows, bi=bi, h=h,
                    lmat=jnp.where(strict, with_k[:ch], 0.0),
                    vb_kbg=jnp.concatenate([(vh * beta).astype(BF16), (kb * eg).astype(BF16)], axis=1),
                    qk=jnp.where(incl, with_k[ch:], 0.0).astype(BF16),
                    qg=(qh * eg).astype(BF16), kd_t=(kh * jnp.exp(g_last - gc)).T.astype(BF16),
                    gl=jnp.exp(g_last)))
        return run

    def inverse_start():
        eye = jnp.where(r == col, 1.0, 0.0)
        for u in units:
            l16 = u["lmat"].astype(BF16)
            xs.append(eye - u["lmat"])
            ps.append(_dot(l16, l16))

    inverse_stages = int(np.log2(ch)) - 1

    def inverse_stage(s):
        for i in range(n_units):
            rhs = ps[i].astype(BF16)
            if s + 1 == inverse_stages:
                xs[i] = xs[i] + _dot(xs[i].astype(BF16), rhs)
            else:
                both = _dot(jnp.concatenate([xs[i], ps[i]], axis=0).astype(BF16), rhs)
                xs[i] = xs[i] + both[:ch]
                ps[i] = both[ch:]

    pieces = [prepare(ci, bi) for ci in range(ct // ch) for bi in range(nb)] + [inverse_start]
    pieces += [functools.partial(inverse_stage, s) for s in range(inverse_stages)]

    def store_unit(i):
        u = units[i]
        u_w = _dot(xs[i].astype(BF16), u["vb_kbg"])
        u_sc[i] = u_w[:, :GDN_DH]
        wq_sc[i] = jnp.concatenate([u_w[:, GDN_DH:].astype(BF16), u["qg"]], axis=0)
        kq_sc[i] = jnp.concatenate([u["kd_t"], u["qk"]], axis=0)
        gl_sc[i] = jnp.broadcast_to(u["gl"], gl_sc.shape[1:])

    for k in range(max(n_units, len(pieces))):
        if k < n_units:
            scan_unit(k)
        if k < len(pieces):
            pieces[k]()
    for i in range(n_units):
        store_unit(i)


def _gdn(oqkv, oz, oab, alog, dtb, og, ct=128):
    b, t, w3 = oqkv.shape
    steps = t // ct
    n_units = (ct // GDN_CHUNK) * b * GDN_HEADS
    full = lambda a: pl.BlockSpec(a.shape, lambda s: (0,) * a.ndim)
    prep = lambda w: pl.BlockSpec((b, ct, w), lambda s: (0, jnp.minimum(s, steps - 1), 0))
    scan = pl.BlockSpec((b, ct, GDN_W), lambda s: (0, jnp.maximum(s - 1, 0), 0))
    return pl.pallas_call(
        functools.partial(_gdn_body, ct=ct),
        grid=(steps + 1,),
        in_specs=[prep(w3), scan, prep(LANES), full(alog), full(dtb), full(og)],
        out_specs=scan,
        out_shape=jax.ShapeDtypeStruct((b, t, GDN_W), BF16),
        scratch_shapes=[pltpu.VMEM((b * GDN_HEADS, GDN_DH, GDN_DH), F32),
                        pltpu.VMEM((n_units, GDN_CHUNK, GDN_DH), F32),
                        pltpu.VMEM((n_units, 2 * GDN_CHUNK, GDN_DH), BF16),
                        pltpu.VMEM((n_units, GDN_DH + GDN_CHUNK, GDN_CHUNK), BF16),
                        pltpu.VMEM((n_units, 8, LANES), F32)],
        compiler_params=_cparams(("arbitrary",), 32),
        name="gdn",
    )(oqkv, oz, oab, alog, dtb, og)


def _outproj_body(ont_ref, og_ref, x_ref, ng_ref, wo_ref, fg_ref, wr_ref, br_ref, upper_ref,
                  x1_ref, h2_ref, gate_ref, route_ref, cnt_ref, cnt_sc):
    i = pl.program_id(0)
    tm = x_ref.shape[0]

    @pl.when(i == 0)
    def _():
        cnt_sc[...] = jnp.zeros(cnt_sc.shape, F32)

    a = ont_ref[...]
    a = (a * lax.rsqrt(jnp.mean(a * a, axis=0, keepdims=True) + EPS) * ng_ref[...]).astype(BF16)
    x1 = x_ref[...] + _dot_tn(a, wo_ref[0:NSA_W, :]) + _dot(og_ref[...], wo_ref[NSA_W:, :])
    x1_ref[...] = x1
    h2f = x1 * lax.rsqrt(jnp.mean(x1 * x1, axis=-1, keepdims=True) + EPS) * fg_ref[...]
    _store_pieces(h2_ref, h2f)
    h2 = h2f.astype(BF16)

    logits = (_dot(h2, wr_ref[...]) + br_ref[...]).T[:N_EXPERTS]
    erow = lax.broadcasted_iota(jnp.int32, (N_EXPERTS, tm), 0)
    onehot = jnp.zeros((N_EXPERTS, tm), F32)
    firsts, vals = [], []
    v = logits
    for k in range(TOP_K):
        mx = jnp.max(v, axis=0, keepdims=True)
        first = jnp.min(jnp.where(v == mx, erow, N_EXPERTS), axis=0, keepdims=True)
        hit = erow == first
        v = jnp.where(hit, -jnp.inf, v)
        onehot = jnp.where(hit, 1.0, onehot)
        firsts.append(first)
        vals.append(mx)
    vals = [jnp.exp(m - vals[0]) for m in vals]
    inv = 1.0 / (vals[0] + vals[1] + vals[2] + vals[3])
    gates_t = jnp.concatenate([m * inv for m in vals] + [jnp.zeros((LANES - TOP_K, tm), F32)], axis=0)
    gate_ref[...] = gates_t.T

    excl = cnt_sc[...] + _dot(onehot.astype(BF16), upper_ref[...])
    for k in range(TOP_K):
        route_ref[k:k + 1, :] = firsts[k]
        rank = jnp.sum(jnp.where(erow == firsts[k], excl, 0.0), axis=0, keepdims=True)
        route_ref[TOP_K + k:TOP_K + k + 1, :] = rank.astype(jnp.int32)
    cnt_sc[...] = cnt_sc[...] + jnp.sum(onehot, axis=1, keepdims=True)
    cnt_ref[...] = cnt_sc[...].astype(jnp.int32)


def _out_proj(o_nsa_t, o_gdn, x2, ng, wo, fg, wr, br, tm=512):
    upper = jnp.asarray(np.arange(tm)[:, None] < np.arange(tm)[None, :], BF16)
    n, d = x2.shape
    full = lambda a: pl.BlockSpec(a.shape, lambda i: (0,) * a.ndim)
    row = lambda w: pl.BlockSpec((tm, w), lambda i: (i, 0))
    return pl.pallas_call(
        _outproj_body,
        grid=(n // tm,),
        in_specs=[pl.BlockSpec((NSA_W, tm), lambda i: (0, i)), row(GDN_W), row(d), full(ng), full(wo), full(fg),
                  full(wr), full(br), full(upper)],
        out_specs=[row(d), pl.BlockSpec((d // 2 // SC_SUBROW, tm, SC_SUBROW), lambda i: (0, i, 0)),
                   row(LANES), pl.BlockSpec((2 * TOP_K, tm), lambda i: (0, i)),
                   pl.BlockSpec((N_EXPERTS, 1), lambda i: (0, 0))],
        out_shape=[jax.ShapeDtypeStruct((n, d), F32),
                   jax.ShapeDtypeStruct((d // 2 // SC_SUBROW, n, SC_SUBROW), jnp.int32),
                   jax.ShapeDtypeStruct((n, LANES), F32), jax.ShapeDtypeStruct((2 * TOP_K, n), jnp.int32),
                   jax.ShapeDtypeStruct((N_EXPERTS, 1), jnp.int32)],
        scratch_shapes=[pltpu.VMEM((N_EXPERTS, 1), F32)],
        compiler_params=_cparams(("arbitrary",), 48),
        name="out_proj_router",
    )(o_nsa_t, o_gdn, x2, ng, wo, fg, wr, br, upper)


def _dest_body(ps_ref, route_ref, o_ref, *, n_rows, pieces):
    expert = route_ref[0:TOP_K, :]
    start = jnp.zeros(expert.shape, jnp.int32)
    for e in range(N_EXPERTS):
        start = jnp.where(expert == e, ps_ref[e], start)
    dest = start + route_ref[TOP_K:2 * TOP_K, :]
    for k in range(TOP_K):
        for j in range(pieces):
            o_ref[k * pieces + j:k * pieces + j + 1, :] = dest[k:k + 1, :] + j * n_rows


def _dest_rows(pstarts, route, n_rows, pieces):
    n = route.shape[1]
    tn = min(2048, n)
    grid_spec = pltpu.PrefetchScalarGridSpec(
        num_scalar_prefetch=1,
        grid=(n // tn,),
        in_specs=[pl.BlockSpec((2 * TOP_K, tn), lambda i, ps: (0, i))],
        out_specs=pl.BlockSpec((TOP_K * pieces, tn), lambda i, ps: (0, i)),
    )
    return pl.pallas_call(
        functools.partial(_dest_body, n_rows=n_rows, pieces=pieces),
        grid_spec=grid_spec,
        out_shape=jax.ShapeDtypeStruct((TOP_K * pieces, n), jnp.int32),
        name="moe_dest_rows",
    )(pstarts, route)


def _expert_body(be_ref, end_ref, xs_ref, wg_hbm, bg_ref, wu_hbm, bu_ref, wd_hbm, bd_ref,
                 y_ref, wf32, w16, sems, slot_sc):
    i = pl.program_id(0)
    n_used = be_ref[pl.num_programs(0)]
    used = i < n_used
    expert = be_ref[i]
    fresh = used & ((i == 0) | (expert != be_ref[jnp.maximum(i - 1, 0)]))
    hbm = (wg_hbm, wu_hbm, wd_hbm)

    def weight_copy(e, slot, j):
        return pltpu.make_async_copy(hbm[j].at[e], wf32.at[slot, j], sems.at[slot, j])

    @pl.when(i == 0)
    def _():
        slot_sc[0] = 0

    @pl.when((i == 0) & used)
    def _():
        for j in range(3):
            weight_copy(expert, 0, j).start()

    @pl.when(fresh)
    def _():
        slot = slot_sc[0]
        for j in range(3):
            weight_copy(expert, slot, j).wait()
            w16[j] = wf32[slot, j].astype(BF16)
        following = end_ref[expert]

        @pl.when(following < n_used)
        def _():
            for j in range(3):
                weight_copy(be_ref[following], 1 - slot, j).start()

        slot_sc[0] = 1 - slot

    @pl.when(jnp.logical_not(used))
    def _():
        y_ref[...] = jnp.zeros(y_ref.shape, y_ref.dtype)

    @pl.when(used)
    def _():
        x = _join_pieces(xs_ref).astype(BF16)
        gate = jnp.minimum(_dot(x, w16[0]) + bg_ref[0], SWIGLU_LIMIT)
        up = jnp.clip(_dot(x, w16[1]) + bu_ref[0], -SWIGLU_LIMIT, SWIGLU_LIMIT)
        glu = gate * jax.nn.sigmoid(gate * SWIGLU_ALPHA)
        act = ((up + 1.0) * glu).astype(BF16)
        cols = 2 * y_ref.shape[2]
        for j in range(y_ref.shape[0]):
            sl = slice(j * cols, (j + 1) * cols)
            y_ref[j] = _pack_piece(_dot(act, w16[2, :, sl]) + bd_ref[0, :, sl])


def _experts(blk_e, end_blk, xs, wg, bg, wu, bu, wd, bd):
    pieces, n_rows, sub = xs.shape
    d, de = wg.shape[1], wg.shape[2]
    assert d == de
    r = MOE_ROW_BLOCK
    bspec = lambda w: pl.BlockSpec((1, 1, w), lambda i, be, *_: (be[i], 0, 0))
    hbm = pl.BlockSpec(memory_space=pl.ANY)
    grid_spec = pltpu.PrefetchScalarGridSpec(
        num_scalar_prefetch=2,
        grid=(n_rows // r,),
        in_specs=[pl.BlockSpec((pieces, r, sub), lambda i, *_: (0, i, 0)),
                  hbm, bspec(de), hbm, bspec(de), hbm, bspec(d)],
        out_specs=pl.BlockSpec((pieces, r, sub), lambda i, *_: (0, i, 0)),
        scratch_shapes=[pltpu.VMEM((2, 3, d, de), F32), pltpu.VMEM((3, d, de), BF16),
                        pltpu.SemaphoreType.DMA((2, 3)), pltpu.SMEM((1,), jnp.int32)],
    )
    return pl.pallas_call(
        _expert_body,
        grid_spec=grid_spec,
        out_shape=jax.ShapeDtypeStruct((pieces, n_rows, sub), jnp.int32),
        compiler_params=_cparams(("arbitrary",), 56),
        name="moe_experts",
    )(blk_e, end_blk, xs, wg, bg, wu, bu, wd, bd)


SC_WINDOW = 128
SC_SUBROW = 256


def _sc_mesh():
    return plsc.VectorSubcoreMesh(core_axis_name="c", subcore_axis_name="s")


def _sc_dispatch(h2, dest_rows, n_rows):
    n, d = h2.shape

    @functools.partial(pl.kernel, out_type=jax.ShapeDtypeStruct((n_rows, d), h2.dtype), mesh=_sc_mesh())
    def dispatch(x_hbm, *refs):
        idx_hbm, o_hbm = refs[:TOP_K], refs[TOP_K]

        def body(x_vmem, *idx_vmem):
            for iv in idx_vmem:
                pltpu.sync_copy(x_vmem, o_hbm.at[iv.at[0]])

        pltpu.emit_pipeline(
            body,
            grid=(n // SC_WINDOW,),
            in_specs=[pl.BlockSpec((SC_WINDOW, d), lambda i: (i, 0))]
                     + [pl.BlockSpec((1, SC_WINDOW), lambda i: (0, i))] * TOP_K,
            out_specs=[],
            core_axis_name=("c", "s"),
            dimension_semantics=(pltpu.PARALLEL,),
        )(x_hbm, *idx_hbm)

    return dispatch(h2, *dest_rows)


def _sc_gather(table, idx):
    _, d = table.shape
    m = idx.shape[1]

    @functools.partial(pl.kernel, out_type=jax.ShapeDtypeStruct((m, d), table.dtype), mesh=_sc_mesh())
    def gather(t_hbm, i_hbm, o_hbm):
        def body(i_vmem, o_vmem):
            pltpu.sync_copy(t_hbm.at[i_vmem.at[0]], o_vmem)

        pltpu.emit_pipeline(
            body,
            grid=(m // SC_WINDOW,),
            in_specs=[pl.BlockSpec((1, SC_WINDOW), lambda i: (0, i))],
            out_specs=[pl.BlockSpec((SC_WINDOW, d), lambda i: (i, 0))],
            core_axis_name=("c", "s"),
            dimension_semantics=(pltpu.PARALLEL,),
        )(i_hbm, o_hbm)

    return gather(table, idx)


def _combine_body(x1_ref, y_ref, gate_ref, o_ref):
    acc = x1_ref[...]
    for k in range(TOP_K):
        acc = acc + gate_ref[:, k:k + 1] * _join_pieces(y_ref.at[k])
    o_ref[...] = acc


def _combine(x1, y4, gates, tm=512):
    n, d = x1.shape
    pieces, sub = y4.shape[1], y4.shape[3]
    row = lambda w: pl.BlockSpec((tm, w), lambda i: (i, 0))
    return pl.pallas_call(
        _combine_body,
        grid=(n // tm,),
        in_specs=[row(d), pl.BlockSpec((TOP_K, pieces, tm, sub), lambda i: (0, 0, i, 0)), row(LANES)],
        out_specs=row(d),
        out_shape=jax.ShapeDtypeStruct((n, d), F32),
        compiler_params=_cparams(("parallel",), 48),
        name="moe_combine",
    )(x1, y4, gates)


def _pad_lanes(a, width=LANES):
    return jnp.pad(a, ((0, 0), (0, width - a.shape[1])))


def _layer(x, attn_norm_g, w_in, q_g, kc_g, ks_g, kw_g, ck_pos, ck_w1, ck_b1, ck_w2, ck_b2,
           cv_pos, cv_w1, cv_b1, cv_w2, cv_b2, nsa_out_g, conv_w, a_log, dt_bias, gdn_out_g, w_out,
           ffn_g, router_w, router_b, e_wg, e_bg, e_wu, e_bu, e_wd, e_bd):
    b, t, d = x.shape
    n = b * t
    x2 = x.reshape(n, d)

    o = np.cumsum([0, NSA_W] + [NSA_GROUPS * NSA_DH] * 6 + [3 * NSA_HEADS, 3 * GDN_W, GDN_W, GDN_HEADS, GDN_HEADS])
    wq_t = w_in[:, o[0]:o[1]].T.astype(BF16)
    qg_col = jnp.tile(q_g * (NSA_DH ** -0.5 * np.log2(np.e)), NSA_HEADS).reshape(NSA_W, 1)
    wkv = jnp.concatenate([w_in[:, o[1]:o[4]], w_in[:, o[5]:o[6]]], axis=1).astype(BF16)
    kg = jnp.concatenate([ks_g, ks_g, kw_g, kw_g]).reshape(1, 2 * LANES)
    wv_t = jnp.concatenate([w_in[:, o[4]:o[5]], w_in[:, o[6]:o[7]]], axis=1).T.reshape(2 * NSA_GROUPS, NSA_DH, d)
    wv_t = jnp.pad(wv_t, ((0, 0), (0, LANES - NSA_DH), (0, 0))).reshape(2 * NSA_GROUPS * LANES, d).astype(BF16)
    vone = jnp.asarray((np.arange(2 * NSA_GROUPS * LANES) % LANES == NSA_DH).astype(np.float32)[:, None])
    wg_t = w_in[:, o[7]:o[8]].T.reshape(NSA_GROUPS, NSA_HPG * 3, d)
    wg_t = jnp.pad(wg_t, ((0, 0), (0, GATE_ROWS - NSA_HPG * 3), (0, 0))).reshape(NSA_GROUPS * GATE_ROWS, d)
    wg_t = wg_t.astype(BF16)
    wab = _pad_lanes(w_in[:, o[10]:o[12]]).astype(BF16)
    wqkv = w_in[:, o[8]:o[9]].astype(BF16)
    wz = w_in[:, o[9]:o[10]].astype(BF16)

    tm = min(512, t)
    oqt, okn, xflat, ovt, ogt, oqkv, oz, oab = _in_proj(x2, attn_norm_g.reshape(1, d), wq_t, wkv, wv_t, wg_t, wqkv,
                                                        wz, wab, qg_col, kg, vone, conv_w, t // tm, tm)

    nch = t // CMP_STRIDE
    n_cmp = (t - CMP_BLOCK) // CMP_STRIDE + 1
    half = CMP_STRIDE * NSA_DH
    pos = jnp.stack([ck_pos, cv_pos]).reshape(2, 2, 1, half)
    w1 = jnp.stack([ck_w1, cv_w1]).reshape(2, 2, half, CMP_HIDDEN).astype(BF16)
    b1 = jnp.stack([ck_b1, cv_b1]).reshape(2, 1, CMP_HIDDEN)
    w2 = jnp.stack([ck_w2, cv_w2]).astype(BF16)
    b2 = jnp.stack([ck_b2, cv_b2]).reshape(2, 1, NSA_DH)
    w2t = jnp.stack([ck_w2.T, cv_w2.T]).astype(BF16)
    b2t = jnp.stack([ck_b2, cv_b2]).reshape(2, NSA_DH, 1)
    kc, vct = _compress(xflat, pos, w1, b1, w2, b2, w2t, b2t, kc_g.reshape(1, NSA_DH), n_cmp)

    n_slc = t // SLC_BLOCK
    n_top = min(SLC_TOPK, n_slc)
    assert n_top > 3
    nblk = max(n_slc, LANES)
    kt = min(256, t // 4)
    assert (t // kt) % 4 == 0
    ci = np.arange(nch)[None, :] * CMP_STRIDE
    sj = np.arange(nblk)[:, None] * SLC_BLOCK
    overlap = ((ci < sj + SLC_BLOCK) & (ci + CMP_BLOCK > sj) & (np.arange(nch)[None, :] < n_cmp)
               & (np.arange(nblk)[:, None] < n_slc))
    expand_t = (np.arange(t)[:, None] // SLC_BLOCK) == np.arange(nblk)[None, :]
    o_nsa_t = _nsa_attention(oqt, ogt, kc, vct, okn.reshape(b, t, -1), ovt, jnp.asarray(expand_t, BF16),
                             jnp.asarray(overlap, BF16), b, t, n_top, kt)

    alog_row = _pad_lanes(a_log.reshape(1, GDN_HEADS))
    dtb_row = _pad_lanes(dt_bias.reshape(1, GDN_HEADS))
    o_gdn = _gdn(oqkv.reshape(b, t, -1), oz.reshape(b, t, -1), oab.reshape(b, t, -1),
                 alog_row, dtb_row, gdn_out_g.reshape(1, GDN_DH))

    wr = _pad_lanes(router_w).astype(BF16)
    br = _pad_lanes(router_b.reshape(1, N_EXPERTS))
    x1, h2, gates, route, counts = _out_proj(
        o_nsa_t, o_gdn.reshape(n, GDN_W), x2, nsa_out_g.reshape(NSA_W, 1),
        w_out.astype(BF16), ffn_g.reshape(1, d), wr, br)

    r = MOE_ROW_BLOCK
    nk = n * TOP_K
    counts = counts[:, 0]
    pcounts = (counts + r - 1) // r * r
    pends = jnp.cumsum(pcounts)
    pstarts = pends - pcounts
    n_rows = (nk + r - 1) // r * r + N_EXPERTS * r
    n_blocks = n_rows // r
    blk_start = jnp.arange(n_blocks, dtype=jnp.int32)[:, None] * r
    blk_e = jnp.minimum(jnp.sum(pends[None, :] <= blk_start, axis=1), N_EXPERTS - 1).astype(jnp.int32)
    n_used = (pends[-1] // r).astype(jnp.int32)
    blk_e = jnp.concatenate([blk_e, n_used[None]])
    end_blk = (pends // r).astype(jnp.int32)
    pieces = d // 2 // SC_SUBROW
    dest_p = _dest_rows(pstarts.astype(jnp.int32), route, n_rows, pieces).reshape(TOP_K, pieces, n)
    xs = _sc_dispatch(h2.reshape(pieces * n, SC_SUBROW), [dest_p[k].reshape(1, pieces * n) for k in range(TOP_K)],
                      pieces * n_rows)
    ys = _experts(blk_e, end_blk, xs.reshape(pieces, n_rows, SC_SUBROW), e_wg, e_bg.reshape(N_EXPERTS, 1, -1),
                  e_wu, e_bu.reshape(N_EXPERTS, 1, -1), e_wd, e_bd.reshape(N_EXPERTS, 1, -1))
    y4 = _sc_gather(ys.reshape(pieces * n_rows, SC_SUBROW), dest_p.reshape(1, nk * pieces))
    return _combine(x1, y4.reshape(TOP_K, pieces, n, SC_SUBROW), gates).reshape(b, t, d)


def kernel(x, attn_norm_g, w_in, nsa_q_norm_g, nsa_kc_norm_g, nsa_ks_norm_g, nsa_kw_norm_g, cmp_k_pos, cmp_k_w1, cmp_k_b1, cmp_k_w2, cmp_k_b2, cmp_v_pos, cmp_v_w1, cmp_v_b1, cmp_v_w2, cmp_v_b2, nsa_out_norm_g, gdn_conv_w, gdn_a_log, gdn_dt_bias, gdn_out_norm_g, w_out, ffn_norm_g, router_w, router_b, exp_w_gate, exp_b_gate, exp_w_up, exp_b_up, exp_w_down, exp_b_down):
    params = (attn_norm_g, w_in, nsa_q_norm_g, nsa_kc_norm_g, nsa_ks_norm_g, nsa_kw_norm_g,
              cmp_k_pos, cmp_k_w1, cmp_k_b1, cmp_k_w2, cmp_k_b2, cmp_v_pos, cmp_v_w1, cmp_v_b1, cmp_v_w2, cmp_v_b2,
              nsa_out_norm_g, gdn_conv_w, gdn_a_log, gdn_dt_bias, gdn_out_norm_g, w_out, ffn_norm_g,
              router_w, router_b, exp_w_gate, exp_b_gate, exp_w_up, exp_b_up, exp_w_down, exp_b_down)
    for l in range(attn_norm_g.shape[0]):
        x = _layer(x, *(p[l] for p in params))
    return x
```

```python
import functools

import jax
import jax.numpy as jnp
import numpy as np
from jax import lax
from jax.experimental import pallas as pl
from jax.experimental.pallas import tpu as pltpu
from jax.experimental.pallas import tpu_sc as plsc

F32 = jnp.float32
BF16 = jnp.bfloat16

EPS = 1e-6
NEG = -1e30
MASKED = -2.0 ** 100

NSA_HEADS = 8
NSA_GROUPS = 2
NSA_HPG = 4
NSA_DH = 64
CMP_BLOCK = 32
CMP_STRIDE = 16
CMP_HIDDEN = 256
SLC_BLOCK = 64
SLC_TOPK = 16
WINDOW = 512
NSA_Q = 256
GDN_HEADS = 4
GDN_DH = 128
GDN_CHUNK = 64
N_EXPERTS = 32
TOP_K = 4
SWIGLU_LIMIT = 7.0
SWIGLU_ALPHA = 1.702
MOE_ROW_BLOCK = 256

LANES = 128
GATE_ROWS = 16
FLASH_BODY_TILES = (8, 4)
NSA_W = NSA_HEADS * NSA_DH
GDN_W = GDN_HEADS * GDN_DH

_NT = (((1,), (1,)), ((), ()))
_TN = (((0,), (0,)), ((), ()))


def _cparams(sem, vmem_mb):
    return pltpu.CompilerParams(dimension_semantics=sem, vmem_limit_bytes=vmem_mb * 1024 * 1024)


def _dot(a, b):
    return jnp.dot(a, b, preferred_element_type=F32)


def _dot_nt(a, b):
    return lax.dot_general(a, b, _NT, preferred_element_type=F32)


def _dot_tn(a, b):
    return lax.dot_general(a, b, _TN, preferred_element_type=F32)


def _pack_piece(block):
    words = block.shape[1] // 2
    hi = lax.bitcast_convert_type(block[:, :words].astype(BF16).astype(F32), jnp.uint32)
    lo = lax.bitcast_convert_type(block[:, words:].astype(BF16).astype(F32), jnp.uint32)
    return lax.bitcast_convert_type(hi | (lo >> 16), jnp.int32)


def _store_pieces(ref, val):
    cols = 2 * ref.shape[2]
    for j in range(ref.shape[0]):
        ref[j] = _pack_piece(val[:, j * cols:(j + 1) * cols])


def _join_pieces(ref):
    out = []
    for j in range(ref.shape[0]):
        words = lax.bitcast_convert_type(ref[j], jnp.uint32)
        out.append(lax.bitcast_convert_type(words & jnp.uint32(0xFFFF0000), F32))
        out.append(lax.bitcast_convert_type(words << 16, F32))
    return jnp.concatenate(out, axis=1)


def _inproj_body(x_ref, g_ref, wqt_ref, wkv_ref, wvt_ref, wgt_ref, wqkv_ref, wz_ref, wab_ref, qg_ref, kg_ref,
                 vone_ref, cw_ref, oqt_ref, okn_ref, ocf_ref, ovt_ref, ogt_ref, oqkv_ref, oz_ref, oab_ref, ybuf, cbuf,
                 *, tiles_per_seq):
    tm = x_ref.shape[0]

    halo = ybuf.shape[0] - tm
    first = pl.program_id(0) % tiles_per_seq == 0

    @pl.when(first)
    def _():
        ybuf[0:halo, :] = jnp.zeros((halo, ybuf.shape[1]), F32)

    @pl.when(jnp.logical_not(first))
    def _():
        ybuf[0:halo, :] = ybuf[tm:tm + halo, :]

    x = x_ref[...]
    h = (x * lax.rsqrt(jnp.mean(x * x, axis=-1, keepdims=True) + EPS) * g_ref[...]).astype(BF16)
    ybuf[halo:halo + tm, :] = _dot(h, wqkv_ref[...])

    yq = _dot_nt(wqt_ref[...], h)
    for s in range(NSA_HEADS):
        sl = slice(s * NSA_DH, (s + 1) * NSA_DH)
        ys = yq[sl, :]
        ms = jnp.sum(ys * ys, axis=0, keepdims=True) * (1.0 / NSA_DH)
        oqt_ref[sl, :] = (ys * lax.rsqrt(ms + EPS) * qg_ref[sl, :]).astype(BF16)

    ykv = _dot(h, wkv_ref[...])
    lane = lax.broadcasted_iota(jnp.int32, (tm, LANES), 1)
    low = lane < NSA_DH
    for s in range(2):
        sl = slice(s * LANES, (s + 1) * LANES)
        ys = ykv[:, (2 + s) * LANES:(3 + s) * LANES]
        y2 = ys * ys
        s0 = jnp.sum(jnp.where(low, y2, 0.0), axis=-1, keepdims=True)
        s1 = jnp.sum(jnp.where(low, 0.0, y2), axis=-1, keepdims=True)
        ms = jnp.where(low, s0, s1) * (1.0 / NSA_DH)
        okn_ref[:, sl] = (ys * lax.rsqrt(ms + EPS) * kg_ref[:, sl]).astype(BF16)

    chunks = tm // CMP_STRIDE
    for br in range(2):
        cbuf[br] = ykv[:, br * LANES:(br + 1) * LANES]
        taken = [cbuf[br, pl.ds(l, chunks, stride=CMP_STRIDE), :] for l in range(CMP_STRIDE)]
        for grp in range(NSA_GROUPS):
            flat = jnp.concatenate([r[:, grp * NSA_DH:(grp + 1) * NSA_DH] for r in taken], axis=1)
            ocf_ref[0, br, grp] = flat.astype(BF16)

    ovt_ref[...] = (_dot_nt(wvt_ref[...], h) + vone_ref[...]).astype(BF16)
    ogt_ref[...] = _dot_nt(wgt_ref[...], h)
    oz_ref[...] = _dot(h, wz_ref[...]).astype(BF16)
    oab_ref[...] = _dot(h, wab_ref[...])

    taps = cw_ref.shape[0]
    y = cw_ref[0:1, :] * ybuf[pl.ds(halo - taps + 1, tm), :]
    for k in range(1, taps):
        y = y + cw_ref[k:k + 1, :] * ybuf[pl.ds(halo - taps + 1 + k, tm), :]
    hy = 0.5 * y
    y = hy + hy * jnp.tanh(hy)
    for s in range(3 * GDN_HEADS):
        sl = slice(s * GDN_DH, (s + 1) * GDN_DH)
        ys = y[:, sl]
        if s < 2 * GDN_HEADS:
            scale = GDN_DH ** -0.5 if s < GDN_HEADS else 1.0
            ys = ys * (lax.rsqrt(jnp.sum(ys * ys, axis=-1, keepdims=True) + EPS) * scale)
        oqkv_ref[:, sl] = ys.astype(BF16)


def _in_proj(x2, g, wqt, wkv, wvt, wgt, wqkv, wz, wab, qg, kg, vone, conv_w, tiles_per_seq, tm):
    n, d = x2.shape
    chunks, flat = tm // CMP_STRIDE, CMP_STRIDE * NSA_DH
    full = lambda a: pl.BlockSpec(a.shape, lambda i: (0,) * a.ndim)
    row = lambda w: pl.BlockSpec((tm, w), lambda i: (i, 0))
    colb = lambda r: pl.BlockSpec((r, tm), lambda i: (0, i))
    return pl.pallas_call(
        functools.partial(_inproj_body, tiles_per_seq=tiles_per_seq),
        grid=(n // tm,),
        in_specs=[row(d)] + [full(a) for a in (g, wqt, wkv, wvt, wgt, wqkv, wz, wab, qg, kg, vone, conv_w)],
        out_specs=[colb(wqt.shape[0]), row(2 * LANES),
                   pl.BlockSpec((1, 2, NSA_GROUPS, chunks, flat), lambda i: (i // tiles_per_seq, 0, 0, i % tiles_per_seq, 0)),
                   colb(wvt.shape[0]), colb(wgt.shape[0]), row(wqkv.shape[1]), row(wz.shape[1]), row(wab.shape[1])],
        out_shape=[jax.ShapeDtypeStruct((wqt.shape[0], n), BF16), jax.ShapeDtypeStruct((n, 2 * LANES), BF16),
                   jax.ShapeDtypeStruct((n // (tm * tiles_per_seq), 2, NSA_GROUPS, chunks * tiles_per_seq, flat), BF16),
                   jax.ShapeDtypeStruct((wvt.shape[0], n), BF16), jax.ShapeDtypeStruct((wgt.shape[0], n), F32),
                   jax.ShapeDtypeStruct((n, wqkv.shape[1]), BF16), jax.ShapeDtypeStruct((n, wz.shape[1]), BF16),
                   jax.ShapeDtypeStruct((n, wab.shape[1]), F32)],
        scratch_shapes=[pltpu.VMEM((tm + 8, wqkv.shape[1]), F32), pltpu.VMEM((2, tm, LANES), F32)],
        compiler_params=_cparams(("arbitrary",), 56),
        name="in_proj",
    )(x2, g, wqt, wkv, wvt, wgt, wqkv, wz, wab, qg, kg, vone, conv_w)


def _compress_body(x_ref, pos_ref, w1_ref, b1_ref, w2_ref, b2_ref, w2t_ref, b2t_ref, g_ref, ok_ref, ovt_ref,
                   *, n_cmp):
    is_key = pl.program_id(1) == 0
    nch = x_ref.shape[3]
    hids = []
    for grp in range(NSA_GROUPS):
        x = x_ref[0, 0, grp].astype(F32)
        xa = (x + pos_ref[0, 0]).astype(BF16)
        xb = (x + pos_ref[0, 1]).astype(BF16)
        a = _dot(xa, w1_ref[0, 0])
        b = _dot(xb, w1_ref[0, 1])
        b_next = pltpu.roll(b, nch - 1, 0)
        hids.append(jax.nn.gelu(a + b_next + b1_ref[0]).astype(BF16))

    @pl.when(is_key)
    def _():
        row = lax.broadcasted_iota(jnp.int32, (nch, NSA_DH), 0)
        outs = []
        for grp in range(NSA_GROUPS):
            out = _dot(hids[grp], w2_ref[0]) + b2_ref[0]
            out = out * lax.rsqrt(jnp.mean(out * out, axis=-1, keepdims=True) + EPS) * g_ref[...]
            outs.append(jnp.where(row < n_cmp, out, 0.0))
        ok_ref[0] = jnp.concatenate(outs, axis=-1).astype(BF16)

    @pl.when(jnp.logical_not(is_key))
    def _():
        col = lax.broadcasted_iota(jnp.int32, (NSA_DH, nch), 1)
        outs = []
        for grp in range(NSA_GROUPS):
            out = _dot_nt(w2t_ref[0], hids[grp]) + b2t_ref[0]
            outs.append(jnp.where(col < n_cmp, out, 0.0))
        ovt_ref[0] = jnp.concatenate(outs, axis=0).astype(BF16)


def _compress(xflat, pos, w1, b1, w2, b2, w2t, b2t, kc_g, n_cmp):
    b, _, _, nch, flat = xflat.shape
    return pl.pallas_call(
        functools.partial(_compress_body, n_cmp=n_cmp),
        grid=(b, 2),
        in_specs=[
            pl.BlockSpec((1, 1, NSA_GROUPS, nch, flat), lambda i, j: (i, j, 0, 0, 0)),
            pl.BlockSpec((1, 2, 1, flat), lambda i, j: (j, 0, 0, 0)),
            pl.BlockSpec((1, 2, flat, CMP_HIDDEN), lambda i, j: (j, 0, 0, 0)),
            pl.BlockSpec((1, 1, CMP_HIDDEN), lambda i, j: (j, 0, 0)),
            pl.BlockSpec((1, CMP_HIDDEN, NSA_DH), lambda i, j: (j, 0, 0)),
            pl.BlockSpec((1, 1, NSA_DH), lambda i, j: (j, 0, 0)),
            pl.BlockSpec((1, NSA_DH, CMP_HIDDEN), lambda i, j: (j, 0, 0)),
            pl.BlockSpec((1, NSA_DH, 1), lambda i, j: (j, 0, 0)),
            pl.BlockSpec((1, NSA_DH), lambda i, j: (0, 0)),
        ],
        out_specs=[pl.BlockSpec((1, nch, LANES), lambda i, j: (i, 0, 0)),
                   pl.BlockSpec((1, LANES, nch), lambda i, j: (i, 0, 0))],
        out_shape=[jax.ShapeDtypeStruct((b, nch, LANES), BF16), jax.ShapeDtypeStruct((b, LANES, nch), BF16)],
        compiler_params=_cparams(("parallel", "arbitrary"), 32),
        name="nsa_compress",
    )(xflat, pos, w1, b1, w2, b2, w2t, b2t, kc_g)


def _tile_heads(a):
    return jnp.concatenate([a] * NSA_HPG, axis=1)


def _nsa_body(qt_ref, qn_ref, gt_ref, kc_ref, vct_ref, ks_ref, kw_ref, vst_ref, vwt_ref, et_ref, ov_ref, cpat_ref,
              wpat_ref, dpat_ref, o_ref, acc_sc, s_sc, oc_sc, ch_sc, *, n_top, kt, nq):
    grp = pl.program_id(1)
    step = pl.program_id(2)
    s0 = step * NSA_Q
    nch = kc_ref.shape[1]
    nblk = ov_ref.shape[0]
    blk = lax.broadcasted_iota(jnp.int32, (nblk, NSA_Q), 0)
    rounds_left = n_top - 3
    quarter = rounds_left // 4

    def padded_q(ref):
        qh = jnp.concatenate([ref[h * NSA_DH:(h + 1) * NSA_DH, :] for h in range(NSA_HPG)], axis=1)
        zq = jnp.zeros_like(qh)
        return jnp.where(grp == 0, jnp.concatenate([qh, zq], axis=0), jnp.concatenate([zq, qh], axis=0))

    def pick_rounds(v, rounds):
        for _ in range(rounds):
            mx = jnp.max(v, axis=0, keepdims=True)
            first = jnp.min(jnp.where(v == mx, blk, nblk), axis=0, keepdims=True)
            v = jnp.where(blk == first, -jnp.inf, v)
        return v

    def selection_pieces(q, start):
        t_r = start + lax.broadcasted_iota(jnp.int32, (1, NSA_Q), 1)
        cur_r = t_r // SLC_BLOCK
        st = {}

        def compressed_scores():
            cbias = cpat_ref[pl.ds(pl.multiple_of(nch - start // CMP_STRIDE, CMP_STRIDE), nch), :]
            st["sc"] = _dot(kc_ref[0], q) + _tile_heads(cbias)

        def compressed_softmax():
            sc = st["sc"]
            pc = jnp.exp2(sc - jnp.max(sc, axis=0, keepdims=True)).astype(BF16)
            stacked = jnp.concatenate([vct_ref[0], ov_ref[...], jnp.ones((8, nch), BF16)], axis=0)
            res = _dot(stacked, pc)
            inv = jnp.where(_tile_heads(t_r >= CMP_BLOCK - 1),
                            1.0 / jnp.maximum(res[LANES + nblk:LANES + nblk + 1], 1e-30), 0.0)
            oc_sc[1] = res[:LANES] * inv
            imp4 = res[LANES:LANES + nblk] * inv
            imp = (imp4[:, 0:NSA_Q] + imp4[:, NSA_Q:2 * NSA_Q] + imp4[:, 2 * NSA_Q:3 * NSA_Q]
                   + imp4[:, 3 * NSA_Q:4 * NSA_Q])
            imp = jnp.where(blk * SLC_BLOCK > t_r, NEG, imp)
            st["v"] = jnp.where((blk == 0) | (blk == cur_r) | (blk == cur_r - 1), -jnp.inf, imp)

        def rounds(count):
            def run():
                st["v"] = pick_rounds(st["v"], count)
            return run

        def finish():
            ch_sc[1] = jnp.where(st["v"] == -jnp.inf, 1.0, 0.0)

        return [compressed_scores, compressed_softmax, rounds(quarter), rounds(quarter), rounds(quarter),
                rounds(rounds_left - 3 * quarter), finish]

    @pl.when(step == 0)
    def _():
        for piece in selection_pieces(padded_q(qt_ref), s0):
            piece()

    oc_sc[0] = oc_sc[1]
    ch_sc[0] = ch_sc[1]
    nxt = selection_pieces(padded_q(qn_ref), jnp.minimum(step + 1, nq - 1) * NSA_Q)

    qt = padded_q(qt_ref)
    t_row = s0 + lax.broadcasted_iota(jnp.int32, (1, NSA_Q), 1)
    cur = t_row // SLC_BLOCK
    chosen = ch_sc[0] > 0.5

    selb = jnp.where(chosen & (blk * SLC_BLOCK < s0), 0.0, MASKED).astype(BF16)
    rhs = jnp.concatenate([qt, _tile_heads(selb)], axis=0)
    last_tile = ks_ref.shape[1] // kt - 1

    def scores(idx, slot):
        k0 = pl.multiple_of(jnp.minimum(idx, last_tile) * kt, kt)
        lhs = jnp.concatenate([ks_ref[0, pl.ds(k0, kt), :], et_ref[pl.ds(k0, kt), :]], axis=1)
        s_sc[slot] = _dot(lhs, rhs)

    nxt[0]()
    wlen = WINDOW + NSA_Q
    w0 = pl.multiple_of(jnp.maximum(s0 - WINDOW, 0), NSA_Q)
    wbias = wpat_ref[pl.ds(pl.multiple_of(w0 - s0 + WINDOW, NSA_Q), wlen), :]
    sw = _dot(kw_ref[0, pl.ds(w0, wlen), :], qt) + _tile_heads(wbias)

    d0 = pl.multiple_of(s0, NSA_Q)
    selb_d = jnp.where(chosen & (blk <= cur), 0.0, MASKED).astype(BF16)
    sd = (_dot(jnp.concatenate([ks_ref[0, pl.ds(d0, NSA_Q), :], et_ref[pl.ds(d0, NSA_Q), :]], axis=1),
               jnp.concatenate([qt, _tile_heads(selb_d)], axis=0))
          + _tile_heads(dpat_ref[...]))
    m_diag = jnp.max(sd, axis=0, keepdims=True)
    acc_sc[0] = _dot(vst_ref[:, pl.ds(d0, NSA_Q)], jnp.exp2(sd - m_diag).astype(BF16))
    acc_sc[1] = jnp.zeros(acc_sc.shape[1:], F32)

    nxt[1]()
    nxt[2]()
    scores(0, 0)
    pw = jnp.exp2(sw - jnp.max(sw, axis=0, keepdims=True)).astype(BF16)
    nxt[3]()
    scores(1, 1)
    ow = _dot(vwt_ref[:, pl.ds(w0, wlen)], pw)
    ow = ow[:NSA_DH] / ow[NSA_DH:NSA_DH + 1]
    for piece in nxt[4:]:
        piece()

    def update(idx, slot, m_old, acc_ref):
        k0 = pl.multiple_of(idx * kt, kt)
        m_new = jnp.maximum(m_old, jnp.max(s_sc[slot], axis=0, keepdims=True))
        p = jnp.exp2(s_sc[slot] - m_new).astype(BF16)
        acc_ref[...] = jnp.exp2(m_old - m_new) * acc_ref[...] + _dot(vst_ref[:, pl.ds(k0, kt)], p)
        return m_new

    def tile_group(first, carry, count):
        ms = list(carry)
        for t in range(count):
            scores(first + t + 2, (t + 2) % 4)
            ms[t % 2] = update(first + t, t % 4, ms[t % 2], acc_sc.at[t % 2])
        return tuple(ms)

    n_tiles = (s0 + kt - 1) // kt
    carry = (m_diag, jnp.full((1, NSA_HPG * NSA_Q), NEG, F32))
    done = 0
    for size in FLASH_BODY_TILES:
        left = n_tiles - done
        groups = (left + size - 1) // size if size == FLASH_BODY_TILES[-1] else left // size
        carry = lax.fori_loop(0, groups, lambda j, c, done=done, size=size: tile_group(done + size * j, c, size),
                              carry)
        done = done + size * groups
    m0, m1 = carry
    m_fin = jnp.maximum(m0, m1)
    acc = acc_sc[0] * jnp.exp2(m0 - m_fin) + acc_sc[1] * jnp.exp2(m1 - m_fin)
    osl = acc[:NSA_DH] / acc[NSA_DH:NSA_DH + 1]

    oc = jnp.where(grp == 0, oc_sc[0, :NSA_DH], oc_sc[0, NSA_DH:])
    gts = jax.nn.sigmoid(gt_ref[...])
    for h in range(NSA_HPG):
        cols = slice(h * NSA_Q, (h + 1) * NSA_Q)
        o_ref[h * NSA_DH:(h + 1) * NSA_DH, :] = (
            gts[3 * h:3 * h + 1, :] * oc[:, cols] + gts[3 * h + 1:3 * h + 2, :] * osl[:, cols]
            + gts[3 * h + 2:3 * h + 3, :] * ow[:, cols])


def _mask_pattern(valid):
    return jnp.asarray(np.where(valid, 0.0, NEG), F32)


def _nsa_attention(qt, gt, kc, vct, okv, vt, expand_t, overlap, b, t, n_top, kt):
    nch = kc.shape[1]
    nq = t // NSA_Q
    n = b * t
    ql = np.arange(NSA_Q)[None, :]
    rc = np.arange(2 * nch)[:, None] - nch
    cpat = _mask_pattern(rc * CMP_STRIDE + CMP_BLOCK - 1 <= ql)
    rw = np.arange(2 * WINDOW + NSA_Q)[:, None] - WINDOW
    wpat = _mask_pattern((rw <= ql) & (rw > ql - WINDOW))
    dpat = _mask_pattern(np.arange(NSA_Q)[:, None] <= ql)
    full2 = lambda a: pl.BlockSpec(a.shape, lambda bi, g, i: (0, 0))
    return pl.pallas_call(
        functools.partial(_nsa_body, n_top=n_top, kt=kt, nq=nq),
        grid=(b, NSA_GROUPS, nq),
        in_specs=[
            pl.BlockSpec((NSA_HPG * NSA_DH, NSA_Q), lambda bi, g, i: (g, bi * nq + i)),
            pl.BlockSpec((NSA_HPG * NSA_DH, NSA_Q), lambda bi, g, i: (g, bi * nq + jnp.minimum(i + 1, nq - 1))),
            pl.BlockSpec((GATE_ROWS, NSA_Q), lambda bi, g, i: (g, bi * nq + i)),
            pl.BlockSpec((1, nch, LANES), lambda bi, g, i: (bi, 0, 0)),
            pl.BlockSpec((1, LANES, nch), lambda bi, g, i: (bi, 0, 0)),
            pl.BlockSpec((1, t, LANES), lambda bi, g, i: (bi, 0, 0)),
            pl.BlockSpec((1, t, LANES), lambda bi, g, i: (bi, 0, 1)),
            pl.BlockSpec((LANES, t), lambda bi, g, i: (g, bi)),
            pl.BlockSpec((LANES, t), lambda bi, g, i: (NSA_GROUPS + g, bi)),
            full2(expand_t), full2(overlap), full2(cpat), full2(wpat), full2(dpat),
        ],
        out_specs=pl.BlockSpec((NSA_HPG * NSA_DH, NSA_Q), lambda bi, g, i: (g, bi * nq + i)),
        out_shape=jax.ShapeDtypeStruct((NSA_W, n), F32),
        scratch_shapes=[pltpu.VMEM((2, LANES, NSA_HPG * NSA_Q), F32),
                        pltpu.VMEM((4, kt, NSA_HPG * NSA_Q), F32),
                        pltpu.VMEM((2, LANES, NSA_HPG * NSA_Q), F32),
                        pltpu.VMEM((2, overlap.shape[0], NSA_Q), F32)],
        compiler_params=_cparams(("parallel", "parallel", "arbitrary"), 56),
        name="nsa_attention",
    )(qt, qt, gt, kc, vct, okv, okv, vt, vt, expand_t, overlap, cpat, wpat, dpat)


def _split_bf16(a):
    hi = a.astype(BF16)
    return hi, (a - hi.astype(F32)).astype(BF16)


def _gdn_body(x_ref, z_ref, ab_ref, alog_ref, dtb_ref, og_ref, o_ref, s_sc, u_sc, wq_sc, kq_sc, gl_sc, *, ct):
    nb = x_ref.shape[0]
    ch = GDN_CHUNK
    n_units = (ct // ch) * nb * GDN_HEADS

    @pl.when(pl.program_id(0) == 0)
    def _():
        s_sc[...] = jnp.zeros(s_sc.shape, F32)
        u_sc[...] = jnp.zeros(u_sc.shape, F32)
        wq_sc[...] = jnp.zeros(wq_sc.shape, BF16)
        kq_sc[...] = jnp.zeros(kq_sc.shape, BF16)
        gl_sc[...] = jnp.ones(gl_sc.shape, F32)

    def scan_unit(i):
        ci, bi, h = i // (nb * GDN_HEADS), (i // GDN_HEADS) % nb, i % GDN_HEADS
        rows = slice(ci * ch, (ci + 1) * ch)
        hs = slice(h * GDN_DH, (h + 1) * GDN_DH)
        s_old = s_sc[bi * GDN_HEADS + h]
        from_state = _dot(wq_sc[i], s_old.astype(BF16))
        v_new = (u_sc[i] - from_state[:ch]).astype(BF16)
        from_v = _dot(kq_sc[i], v_new)
        s_sc[bi * GDN_HEADS + h] = s_old * gl_sc[i, 0:1, 0:1] + from_v[:GDN_DH]
        o = from_state[ch:] + from_v[GDN_DH:]
        on = o * lax.rsqrt(jnp.mean(o * o, axis=-1, keepdims=True) + EPS) * og_ref[...]
        zh = z_ref[bi, rows, hs].astype(F32)
        o_ref[bi, rows, hs] = (on * (zh * jax.nn.sigmoid(zh))).astype(BF16)

    r = lax.broadcasted_iota(jnp.int32, (ch, ch), 0)
    col = lax.broadcasted_iota(jnp.int32, (ch, ch), 1)
    incl = r >= col
    strict = r > col
    tril16 = jnp.concatenate([jnp.where(incl, 1.0, 0.0).astype(BF16)] * 3, axis=1)

    units, xs, ps = [], [], []

    def prepare(ci, bi):
        def run():
            rows = slice(ci * ch, (ci + 1) * ch)
            ab = ab_ref[bi, rows, :]
            g_all = -jnp.exp(alog_ref[...]) * jax.nn.softplus(ab + dtb_ref[...])
            beta_all = jax.nn.sigmoid(ab)
            g_hi, g_lo = _split_bf16(g_all)
            g_lo2 = (g_all - g_hi.astype(F32) - g_lo.astype(F32)).astype(BF16)
            gc_all = _dot(tril16, jnp.concatenate([g_hi, g_lo, g_lo2], axis=0))
            gc_t = gc_all.T
            for h in range(GDN_HEADS):
                hs = slice(h * GDN_DH, (h + 1) * GDN_DH)
                q16 = x_ref[bi, rows, hs]
                k16 = x_ref[bi, rows, GDN_W + h * GDN_DH:GDN_W + (h + 1) * GDN_DH]
                qh, kh = q16.astype(F32), k16.astype(F32)
                vh = x_ref[bi, rows, 2 * GDN_W + h * GDN_DH:2 * GDN_W + (h + 1) * GDN_DH].astype(F32)
                gc = gc_all[:, h:h + 1]
                gr = gc_t[h:h + 1, :]
                g_last = gc_all[ch - 1:ch, h:h + 1]
                beta = beta_all[:, GDN_HEADS + h:GDN_HEADS + h + 1]
                eg = jnp.exp(gc)
                decay = jnp.where(incl, jnp.exp(jnp.minimum(gc - gr, 0.0)), 0.0)
                kb = kh * beta
                with_k = (_dot_nt(jnp.concatenate([kb.astype(BF16), q16], axis=0), k16)
                          * jnp.concatenate([decay, decay], axis=0))
                units.append(dict(
                    rows=rows, bi=bi, h=h,
                    lmat=jnp.where(strict, with_k[:ch], 0.0),
                    vb_kbg=jnp.concatenate([(vh * beta).astype(BF16), (kb * eg).astype(BF16)], axis=1),
                    qk=jnp.where(incl, with_k[ch:], 0.0).astype(BF16),
                    qg=(qh * eg).astype(BF16), kd_t=(kh * jnp.exp(g_last - gc)).T.astype(BF16),
                    gl=jnp.exp(g_last)))
        return run

    def inverse_start():
        eye = jnp.where(r == col, 1.0, 0.0)
        for u in units:
            l16 = u["lmat"].astype(BF16)
            xs.append(eye - u["lmat"])
            ps.append(_dot(l16, l16))

    inverse_stages = int(np.log2(ch)) - 1

    def inverse_stage(s):
        for i in range(n_units):
            rhs = ps[i].astype(BF16)
            if s + 1 == inverse_stages:
                xs[i] = xs[i] + _dot(xs[i].astype(BF16), rhs)
            else:
                both = _dot(jnp.concatenate([xs[i], ps[i]], axis=0).astype(BF16), rhs)
                xs[i] = xs[i] + both[:ch]
                ps[i] = both[ch:]

    pieces = [prepare(ci, bi) for ci in range(ct // ch) for bi in range(nb)] + [inverse_start]
    pieces += [functools.partial(inverse_stage, s) for s in range(inverse_stages)]

    def store_unit(i):
        u = units[i]
        u_w = _dot(xs[i].astype(BF16), u["vb_kbg"])
        u_sc[i] = u_w[:, :GDN_DH]
        wq_sc[i] = jnp.concatenate([u_w[:, GDN_DH:].astype(BF16), u["qg"]], axis=0)
        kq_sc[i] = jnp.concatenate([u["kd_t"], u["qk"]], axis=0)
        gl_sc[i] = jnp.broadcast_to(u["gl"], gl_sc.shape[1:])

    for k in range(max(n_units, len(pieces))):
        if k < n_units:
            scan_unit(k)
        if k < len(pieces):
            pieces[k]()
    for i in range(n_units):
        store_unit(i)


def _gdn(oqkv, oz, oab, alog, dtb, og, ct=128):
    b, t, w3 = oqkv.shape
    steps = t // ct
    n_units = (ct // GDN_CHUNK) * b * GDN_HEADS
    full = lambda a: pl.BlockSpec(a.shape, lambda s: (0,) * a.ndim)
    prep = lambda w: pl.BlockSpec((b, ct, w), lambda s: (0, jnp.minimum(s, steps - 1), 0))
    scan = pl.BlockSpec((b, ct, GDN_W), lambda s: (0, jnp.maximum(s - 1, 0), 0))
    return pl.pallas_call(
        functools.partial(_gdn_body, ct=ct),
        grid=(steps + 1,),
        in_specs=[prep(w3), scan, prep(LANES), full(alog), full(dtb), full(og)],
        out_specs=scan,
        out_shape=jax.ShapeDtypeStruct((b, t, GDN_W), BF16),
        scratch_shapes=[pltpu.VMEM((b * GDN_HEADS, GDN_DH, GDN_DH), F32),
                        pltpu.VMEM((n_units, GDN_CHUNK, GDN_DH), F32),
                        pltpu.VMEM((n_units, 2 * GDN_CHUNK, GDN_DH), BF16),
                        pltpu.VMEM((n_units, GDN_DH + GDN_CHUNK, GDN_CHUNK), BF16),
                        pltpu.VMEM((n_units, 8, LANES), F32)],
        compiler_params=_cparams(("arbitrary",), 32),
        name="gdn",
    )(oqkv, oz, oab, alog, dtb, og)


def _outproj_body(ont_ref, og_ref, x_ref, ng_ref, wo_ref, fg_ref, wr_ref, br_ref, upper_ref,
                  x1_ref, h2_ref, gate_ref, route_ref, cnt_ref, cnt_sc):
    i = pl.program_id(0)
    tm = x_ref.shape[0]

    @pl.when(i == 0)
    def _():
        cnt_sc[...] = jnp.zeros(cnt_sc.shape, F32)

    a = ont_ref[...]
    a = (a * lax.rsqrt(jnp.mean(a * a, axis=0, keepdims=True) + EPS) * ng_ref[...]).astype(BF16)
    x1 = x_ref[...] + _dot_tn(a, wo_ref[0:NSA_W, :]) + _dot(og_ref[...], wo_ref[NSA_W:, :])
    x1_ref[...] = x1
    h2f = x1 * lax.rsqrt(jnp.mean(x1 * x1, axis=-1, keepdims=True) + EPS) * fg_ref[...]
    _store_pieces(h2_ref, h2f)
    h2 = h2f.astype(BF16)

    logits = (_dot(h2, wr_ref[...]) + br_ref[...]).T[:N_EXPERTS]
    erow = lax.broadcasted_iota(jnp.int32, (N_EXPERTS, tm), 0)
    onehot = jnp.zeros((N_EXPERTS, tm), F32)
    firsts, vals = [], []
    v = logits
    for k in range(TOP_K):
        mx = jnp.max(v, axis=0, keepdims=True)
        first = jnp.min(jnp.where(v == mx, erow, N_EXPERTS), axis=0, keepdims=True)
        hit = erow == first
        v = jnp.where(hit, -jnp.inf, v)
        onehot = jnp.where(hit, 1.0, onehot)
        firsts.append(first)
        vals.append(mx)
    vals = [jnp.exp(m - vals[0]) for m in vals]
    inv = 1.0 / (vals[0] + vals[1] + vals[2] + vals[3])
    gates_t = jnp.concatenate([m * inv for m in vals] + [jnp.zeros((LANES - TOP_K, tm), F32)], axis=0)
    gate_ref[...] = gates_t.T

    excl = cnt_sc[...] + _dot(onehot.astype(BF16), upper_ref[...])
    for k in range(TOP_K):
        route_ref[k:k + 1, :] = firsts[k]
        rank = jnp.sum(jnp.where(erow == firsts[k], excl, 0.0), axis=0, keepdims=True)
        route_ref[TOP_K + k:TOP_K + k + 1, :] = rank.astype(jnp.int32)
    cnt_sc[...] = cnt_sc[...] + jnp.sum(onehot, axis=1, keepdims=True)
    cnt_ref[...] = cnt_sc[...].astype(jnp.int32)


def _out_proj(o_nsa_t, o_gdn, x2, ng, wo, fg, wr, br, tm=512):
    upper = jnp.asarray(np.arange(tm)[:, None] < np.arange(tm)[None, :], BF16)
    n, d = x2.shape
    full = lambda a: pl.BlockSpec(a.shape, lambda i: (0,) * a.ndim)
    row = lambda w: pl.BlockSpec((tm, w), lambda i: (i, 0))
    return pl.pallas_call(
        _outproj_body,
        grid=(n // tm,),
        in_specs=[pl.BlockSpec((NSA_W, tm), lambda i: (0, i)), row(GDN_W), row(d), full(ng), full(wo), full(fg),
                  full(wr), full(br), full(upper)],
        out_specs=[row(d), pl.BlockSpec((d // 2 // SC_SUBROW, tm, SC_SUBROW), lambda i: (0, i, 0)),
                   row(LANES), pl.BlockSpec((2 * TOP_K, tm), lambda i: (0, i)),
                   pl.BlockSpec((N_EXPERTS, 1), lambda i: (0, 0))],
        out_shape=[jax.ShapeDtypeStruct((n, d), F32),
                   jax.ShapeDtypeStruct((d // 2 // SC_SUBROW, n, SC_SUBROW), jnp.int32),
                   jax.ShapeDtypeStruct((n, LANES), F32), jax.ShapeDtypeStruct((2 * TOP_K, n), jnp.int32),
                   jax.ShapeDtypeStruct((N_EXPERTS, 1), jnp.int32)],
        scratch_shapes=[pltpu.VMEM((N_EXPERTS, 1), F32)],
        compiler_params=_cparams(("arbitrary",), 48),
        name="out_proj_router",
    )(o_nsa_t, o_gdn, x2, ng, wo, fg, wr, br, upper)


def _dest_body(ps_ref, route_ref, o_ref, *, n_rows, pieces):
    expert = route_ref[0:TOP_K, :]
    start = jnp.zeros(expert.shape, jnp.int32)
    for e in range(N_EXPERTS):
        start = jnp.where(expert == e, ps_ref[e], start)
    dest = start + route_ref[TOP_K:2 * TOP_K, :]
    for k in range(TOP_K):
        for j in range(pieces):
            o_ref[k * pieces + j:k * pieces + j + 1, :] = dest[k:k + 1, :] + j * n_rows


def _dest_rows(pstarts, route, n_rows, pieces):
    n = route.shape[1]
    tn = min(2048, n)
    grid_spec = pltpu.PrefetchScalarGridSpec(
        num_scalar_prefetch=1,
        grid=(n // tn,),
        in_specs=[pl.BlockSpec((2 * TOP_K, tn), lambda i, ps: (0, i))],
        out_specs=pl.BlockSpec((TOP_K * pieces, tn), lambda i, ps: (0, i)),
    )
    return pl.pallas_call(
        functools.partial(_dest_body, n_rows=n_rows, pieces=pieces),
        grid_spec=grid_spec,
        out_shape=jax.ShapeDtypeStruct((TOP_K * pieces, n), jnp.int32),
        name="moe_dest_rows",
    )(pstarts, route)


def _expert_body(be_ref, end_ref, xs_ref, wg_hbm, bg_ref, wu_hbm, bu_ref, wd_hbm, bd_ref,
                 y_ref, wf32, w16, sems, slot_sc):
    i = pl.program_id(0)
    n_used = be_ref[pl.num_programs(0)]
    used = i < n_used
    expert = be_ref[i]
    fresh = used & ((i == 0) | (expert != be_ref[jnp.maximum(i - 1, 0)]))
    hbm = (wg_hbm, wu_hbm, wd_hbm)

    def weight_copy(e, slot, j):
        return pltpu.make_async_copy(hbm[j].at[e], wf32.at[slot, j], sems.at[slot, j])

    @pl.when(i == 0)
    def _():
        slot_sc[0] = 0

    @pl.when((i == 0) & used)
    def _():
        for j in range(3):
            weight_copy(expert, 0, j).start()

    @pl.when(fresh)
    def _():
        slot = slot_sc[0]
        for j in range(3):
            weight_copy(expert, slot, j).wait()
            w16[j] = wf32[slot, j].astype(BF16)
        following = end_ref[expert]

        @pl.when(following < n_used)
        def _():
            for j in range(3):
                weight_copy(be_ref[following], 1 - slot, j).start()

        slot_sc[0] = 1 - slot

    @pl.when(jnp.logical_not(used))
    def _():
        y_ref[...] = jnp.zeros(y_ref.shape, y_ref.dtype)

    @pl.when(used)
    def _():
        x = _join_pieces(xs_ref).astype(BF16)
        gate = jnp.minimum(_dot(x, w16[0]) + bg_ref[0], SWIGLU_LIMIT)
        up = jnp.clip(_dot(x, w16[1]) + bu_ref[0], -SWIGLU_LIMIT, SWIGLU_LIMIT)
        glu = gate * jax.nn.sigmoid(gate * SWIGLU_ALPHA)
        act = ((up + 1.0) * glu).astype(BF16)
        cols = 2 * y_ref.shape[2]
        for j in range(y_ref.shape[0]):
            sl = slice(j * cols, (j + 1) * cols)
            y_ref[j] = _pack_piece(_dot(act, w16[2, :, sl]) + bd_ref[0, :, sl])


def _experts(blk_e, end_blk, xs, wg, bg, wu, bu, wd, bd):
    pieces, n_rows, sub = xs.shape
    d, de = wg.shape[1], wg.shape[2]
    assert d == de
    r = MOE_ROW_BLOCK
    bspec = lambda w: pl.BlockSpec((1, 1, w), lambda i, be, *_: (be[i], 0, 0))
    hbm = pl.BlockSpec(memory_space=pl.ANY)
    grid_spec = pltpu.PrefetchScalarGridSpec(
        num_scalar_prefetch=2,
        grid=(n_rows // r,),
        in_specs=[pl.BlockSpec((pieces, r, sub), lambda i, *_: (0, i, 0)),
                  hbm, bspec(de), hbm, bspec(de), hbm, bspec(d)],
        out_specs=pl.BlockSpec((pieces, r, sub), lambda i, *_: (0, i, 0)),
        scratch_shapes=[pltpu.VMEM((2, 3, d, de), F32), pltpu.VMEM((3, d, de), BF16),
                        pltpu.SemaphoreType.DMA((2, 3)), pltpu.SMEM((1,), jnp.int32)],
    )
    return pl.pallas_call(
        _expert_body,
        grid_spec=grid_spec,
        out_shape=jax.ShapeDtypeStruct((pieces, n_rows, sub), jnp.int32),
        compiler_params=_cparams(("arbitrary",), 56),
        name="moe_experts",
    )(blk_e, end_blk, xs, wg, bg, wu, bu, wd, bd)


SC_WINDOW = 128
SC_SUBROW = 256


def _sc_mesh():
    return plsc.VectorSubcoreMesh(core_axis_name="c", subcore_axis_name="s")


def _sc_dispatch(h2, dest_rows, n_rows):
    n, d = h2.shape

    @functools.partial(pl.kernel, out_type=jax.ShapeDtypeStruct((n_rows, d), h2.dtype), mesh=_sc_mesh())
    def dispatch(x_hbm, *refs):
        idx_hbm, o_hbm = refs[:TOP_K], refs[TOP_K]

        def body(x_vmem, *idx_vmem):
            for iv in idx_vmem:
                pltpu.sync_copy(x_vmem, o_hbm.at[iv.at[0]])

        pltpu.emit_pipeline(
            body,
            grid=(n // SC_WINDOW,),
            in_specs=[pl.BlockSpec((SC_WINDOW, d), lambda i: (i, 0))]
                     + [pl.BlockSpec((1, SC_WINDOW), lambda i: (0, i))] * TOP_K,
            out_specs=[],
            core_axis_name=("c", "s"),
            dimension_semantics=(pltpu.PARALLEL,),
        )(x_hbm, *idx_hbm)

    return dispatch(h2, *dest_rows)


def _sc_gather(table, idx):
    _, d = table.shape
    m = idx.shape[1]

    @functools.partial(pl.kernel, out_type=jax.ShapeDtypeStruct((m, d), table.dtype), mesh=_sc_mesh())
    def gather(t_hbm, i_hbm, o_hbm):
        def body(i_vmem, o_vmem):
            pltpu.sync_copy(t_hbm.at[i_vmem.at[0]], o_vmem)

        pltpu.emit_pipeline(
            body,
            grid=(m // SC_WINDOW,),
            in_specs=[pl.BlockSpec((1, SC_WINDOW), lambda i: (0, i))],
            out_specs=[pl.BlockSpec((SC_WINDOW, d), lambda i: (i, 0))],
            core_axis_name=("c", "s"),
            dimension_semantics=(pltpu.PARALLEL,),
        )(i_hbm, o_hbm)

    return gather(table, idx)


def _combine_body(x1_ref, y_ref, gate_ref, o_ref):
    acc = x1_ref[...]
    for k in range(TOP_K):
        acc = acc + gate_ref[:, k:k + 1] * _join_pieces(y_ref.at[k])
    o_ref[...] = acc


def _combine(x1, y4, gates, tm=512):
    n, d = x1.shape
    pieces, sub = y4.shape[1], y4.shape[3]
    row = lambda w: pl.BlockSpec((tm, w), lambda i: (i, 0))
    return pl.pallas_call(
        _combine_body,
        grid=(n // tm,),
        in_specs=[row(d), pl.BlockSpec((TOP_K, pieces, tm, sub), lambda i: (0, 0, i, 0)), row(LANES)],
        out_specs=row(d),
        out_shape=jax.ShapeDtypeStruct((n, d), F32),
        compiler_params=_cparams(("parallel",), 48),
        name="moe_combine",
    )(x1, y4, gates)


def _pad_lanes(a, width=LANES):
    return jnp.pad(a, ((0, 0), (0, width - a.shape[1])))


def _layer(x, attn_norm_g, w_in, q_g, kc_g, ks_g, kw_g, ck_pos, ck_w1, ck_b1, ck_w2, ck_b2,
           cv_pos, cv_w1, cv_b1, cv_w2, cv_b2, nsa_out_g, conv_w, a_log, dt_bias, gdn_out_g, w_out,
           ffn_g, router_w, router_b, e_wg, e_bg, e_wu, e_bu, e_wd, e_bd):
    b, t, d = x.shape
    n = b * t
    x2 = x.reshape(n, d)

    o = np.cumsum([0, NSA_W] + [NSA_GROUPS * NSA_DH] * 6 + [3 * NSA_HEADS, 3 * GDN_W, GDN_W, GDN_HEADS, GDN_HEADS])
    wq_t = w_in[:, o[0]:o[1]].T.astype(BF16)
    qg_col = jnp.tile(q_g * (NSA_DH ** -0.5 * np.log2(np.e)), NSA_HEADS).reshape(NSA_W, 1)
    wkv = jnp.concatenate([w_in[:, o[1]:o[4]], w_in[:, o[5]:o[6]]], axis=1).astype(BF16)
    kg = jnp.concatenate([ks_g, ks_g, kw_g, kw_g]).reshape(1, 2 * LANES)
    wv_t = jnp.concatenate([w_in[:, o[4]:o[5]], w_in[:, o[6]:o[7]]], axis=1).T.reshape(2 * NSA_GROUPS, NSA_DH, d)
    wv_t = jnp.pad(wv_t, ((0, 0), (0, LANES - NSA_DH), (0, 0))).reshape(2 * NSA_GROUPS * LANES, d).astype(BF16)
    vone = jnp.asarray((np.arange(2 * NSA_GROUPS * LANES) % LANES == NSA_DH).astype(np.float32)[:, None])
    wg_t = w_in[:, o[7]:o[8]].T.reshape(NSA_GROUPS, NSA_HPG * 3, d)
    wg_t = jnp.pad(wg_t, ((0, 0), (0, GATE_ROWS - NSA_HPG * 3), (0, 0))).reshape(NSA_GROUPS * GATE_ROWS, d)
    wg_t = wg_t.astype(BF16)
    wab = _pad_lanes(w_in[:, o[10]:o[12]]).astype(BF16)
    wqkv = w_in[:, o[8]:o[9]].astype(BF16)
    wz = w_in[:, o[9]:o[10]].astype(BF16)

    tm = min(512, t)
    oqt, okn, xflat, ovt, ogt, oqkv, oz, oab = _in_proj(x2, attn_norm_g.reshape(1, d), wq_t, wkv, wv_t, wg_t, wqkv,
                                                        wz, wab, qg_col, kg, vone, conv_w, t // tm, tm)

    nch = t // CMP_STRIDE
    n_cmp = (t - CMP_BLOCK) // CMP_STRIDE + 1
    half = CMP_STRIDE * NSA_DH
    pos = jnp.stack([ck_pos, cv_pos]).reshape(2, 2, 1, half)
    w1 = jnp.stack([ck_w1, cv_w1]).reshape(2, 2, half, CMP_HIDDEN).astype(BF16)
    b1 = jnp.stack([ck_b1, cv_b1]).reshape(2, 1, CMP_HIDDEN)
    w2 = jnp.stack([ck_w2, cv_w2]).astype(BF16)
    b2 = jnp.stack([ck_b2, cv_b2]).reshape(2, 1, NSA_DH)
    w2t = jnp.stack([ck_w2.T, cv_w2.T]).astype(BF16)
    b2t = jnp.stack([ck_b2, cv_b2]).reshape(2, NSA_DH, 1)
    kc, vct = _compress(xflat, pos, w1, b1, w2, b2, w2t, b2t, kc_g.reshape(1, NSA_DH), n_cmp)

    n_slc = t // SLC_BLOCK
    n_top = min(SLC_TOPK, n_slc)
    assert n_top > 3
    nblk = max(n_slc, LANES)
    kt = min(256, t // 4)
    assert (t // kt) % 4 == 0
    ci = np.arange(nch)[None, :] * CMP_STRIDE
    sj = np.arange(nblk)[:, None] * SLC_BLOCK
    overlap = ((ci < sj + SLC_BLOCK) & (ci + CMP_BLOCK > sj) & (np.arange(nch)[None, :] < n_cmp)
               & (np.arange(nblk)[:, None] < n_slc))
    expand_t = (np.arange(t)[:, None] // SLC_BLOCK) == np.arange(nblk)[None, :]
    o_nsa_t = _nsa_attention(oqt, ogt, kc, vct, okn.reshape(b, t, -1), ovt, jnp.asarray(expand_t, BF16),
                             jnp.asarray(overlap, BF16), b, t, n_top, kt)

    alog_row = _pad_lanes(a_log.reshape(1, GDN_HEADS))
    dtb_row = _pad_lanes(dt_bias.reshape(1, GDN_HEADS))
    o_gdn = _gdn(oqkv.reshape(b, t, -1), oz.reshape(b, t, -1), oab.reshape(b, t, -1),
                 alog_row, dtb_row, gdn_out_g.reshape(1, GDN_DH))

    wr = _pad_lanes(router_w).astype(BF16)
    br = _pad_lanes(router_b.reshape(1, N_EXPERTS))
    x1, h2, gates, route, counts = _out_proj(
        o_nsa_t, o_gdn.reshape(n, GDN_W), x2, nsa_out_g.reshape(NSA_W, 1),
        w_out.astype(BF16), ffn_g.reshape(1, d), wr, br)

    r = MOE_ROW_BLOCK
    nk = n * TOP_K
    counts = counts[:, 0]
    pcounts = (counts + r - 1) // r * r
    pends = jnp.cumsum(pcounts)
    pstarts = pends - pcounts
    n_rows = (nk + r - 1) // r * r + N_EXPERTS * r
    n_blocks = n_rows // r
    blk_start = jnp.arange(n_blocks, dtype=jnp.int32)[:, None] * r
    blk_e = jnp.minimum(jnp.sum(pends[None, :] <= blk_start, axis=1), N_EXPERTS - 1).astype(jnp.int32)
    n_used = (pends[-1] // r).astype(jnp.int32)
    blk_e = jnp.concatenate([blk_e, n_used[None]])
    end_blk = (pends // r).astype(jnp.int32)
    pieces = d // 2 // SC_SUBROW
    dest_p = _dest_rows(pstarts.astype(jnp.int32), route, n_rows, pieces).reshape(TOP_K, pieces, n)
    xs = _sc_dispatch(h2.reshape(pieces * n, SC_SUBROW), [dest_p[k].reshape(1, pieces * n) for k in range(TOP_K)],
                      pieces * n_rows)
    ys = _experts(blk_e, end_blk, xs.reshape(pieces, n_rows, SC_SUBROW), e_wg, e_bg.reshape(N_EXPERTS, 1, -1),
                  e_wu, e_bu.reshape(N_EXPERTS, 1, -1), e_wd, e_bd.reshape(N_EXPERTS, 1, -1))
    y4 = _sc_gather(ys.reshape(pieces * n_rows, SC_SUBROW), dest_p.reshape(1, nk * pieces))
    return _combine(x1, y4.reshape(TOP_K, pieces, n, SC_SUBROW), gates).reshape(b, t, d)


def kernel(x, attn_norm_g, w_in, nsa_q_norm_g, nsa_kc_norm_g, nsa_ks_norm_g, nsa_kw_norm_g, cmp_k_pos, cmp_k_w1, cmp_k_b1, cmp_k_w2, cmp_k_b2, cmp_v_pos, cmp_v_w1, cmp_v_b1, cmp_v_w2, cmp_v_b2, nsa_out_norm_g, gdn_conv_w, gdn_a_log, gdn_dt_bias, gdn_out_norm_g, w_out, ffn_norm_g, router_w, router_b, exp_w_gate, exp_b_gate, exp_w_up, exp_b_up, exp_w_down, exp_b_down):
    params = (attn_norm_g, w_in, nsa_q_norm_g, nsa_kc_norm_g, nsa_ks_norm_g, nsa_kw_norm_g,
              cmp_k_pos, cmp_k_w1, cmp_k_b1, cmp_k_w2, cmp_k_b2, cmp_v_pos, cmp_v_w1, cmp_v_b1, cmp_v_w2, cmp_v_b2,
              nsa_out_norm_g, gdn_conv_w, gdn_a_log, gdn_dt_bias, gdn_out_norm_g, w_out, ffn_norm_g,
              router_w, router_b, exp_w_gate, exp_b_gate, exp_w_up, exp_b_up, exp_w_down, exp_b_down)
    for l in range(attn_norm_g.shape[0]):
        x = _layer(x, *(p[l] for p in params))
    return x
```

```python
import functools

import jax
import jax.numpy as jnp
import numpy as np
from jax import lax
from jax.experimental import pallas as pl
from jax.experimental.pallas import tpu as pltpu
from jax.experimental.pallas import tpu_sc as plsc

F32 = jnp.float32
BF16 = jnp.bfloat16

EPS = 1e-6
NEG = -1e30
MASKED = -2.0 ** 100

NSA_HEADS = 8
NSA_GROUPS = 2
NSA_HPG = 4
NSA_DH = 64
CMP_BLOCK = 32
CMP_STRIDE = 16
CMP_HIDDEN = 256
SLC_BLOCK = 64
SLC_TOPK = 16
WINDOW = 512
NSA_Q = 256
GDN_HEADS = 4
GDN_DH = 128
GDN_CHUNK = 64
N_EXPERTS = 32
TOP_K = 4
SWIGLU_LIMIT = 7.0
SWIGLU_ALPHA = 1.702
MOE_ROW_BLOCK = 256

LANES = 128
GATE_ROWS = 16
FLASH_BODY_TILES = (8, 4)
NSA_W = NSA_HEADS * NSA_DH
GDN_W = GDN_HEADS * GDN_DH

_NT = (((1,), (1,)), ((), ()))
_TN = (((0,), (0,)), ((), ()))


def _cparams(sem, vmem_mb):
    return pltpu.CompilerParams(dimension_semantics=sem, vmem_limit_bytes=vmem_mb * 1024 * 1024)


def _dot(a, b):
    return jnp.dot(a, b, preferred_element_type=F32)


def _dot_nt(a, b):
    return lax.dot_general(a, b, _NT, preferred_element_type=F32)


def _dot_tn(a, b):
    return lax.dot_general(a, b, _TN, preferred_element_type=F32)


def _pack_piece(block):
    words = block.shape[1] // 2
    hi = lax.bitcast_convert_type(block[:, :words].astype(BF16).astype(F32), jnp.uint32)
    lo = lax.bitcast_convert_type(block[:, words:].astype(BF16).astype(F32), jnp.uint32)
    return lax.bitcast_convert_type(hi | (lo >> 16), jnp.int32)


def _store_pieces(ref, val):
    cols = 2 * ref.shape[2]
    for j in range(ref.shape[0]):
        ref[j] = _pack_piece(val[:, j * cols:(j + 1) * cols])


def _join_pieces(ref):
    out = []
    for j in range(ref.shape[0]):
        words = lax.bitcast_convert_type(ref[j], jnp.uint32)
        out.append(lax.bitcast_convert_type(words & jnp.uint32(0xFFFF0000), F32))
        out.append(lax.bitcast_convert_type(words << 16, F32))
    return jnp.concatenate(out, axis=1)


def _inproj_body(x_ref, g_ref, wqt_ref, wkv_ref, wvt_ref, wgt_ref, wqkv_ref, wz_ref, wab_ref, qg_ref, kg_ref,
                 vone_ref, cw_ref, oqt_ref, okn_ref, ocf_ref, ovt_ref, ogt_ref, oqkv_ref, oz_ref, oab_ref, ybuf, cbuf,
                 *, tiles_per_seq):
    tm = x_ref.shape[0]

    halo = ybuf.shape[0] - tm
    first = pl.program_id(0) % tiles_per_seq == 0

    @pl.when(first)
    def _():
        ybuf[0:halo, :] = jnp.zeros((halo, ybuf.shape[1]), F32)

    @pl.when(jnp.logical_not(first))
    def _():
        ybuf[0:halo, :] = ybuf[tm:tm + halo, :]

    x = x_ref[...]
    h = (x * lax.rsqrt(jnp.mean(x * x, axis=-1, keepdims=True) + EPS) * g_ref[...]).astype(BF16)
    ybuf[halo:halo + tm, :] = _dot(h, wqkv_ref[...])

    yq = _dot_nt(wqt_ref[...], h)
    for s in range(NSA_HEADS):
        sl = slice(s * NSA_DH, (s + 1) * NSA_DH)
        ys = yq[sl, :]
        ms = jnp.sum(ys * ys, axis=0, keepdims=True) * (1.0 / NSA_DH)
        oqt_ref[sl, :] = (ys * lax.rsqrt(ms + EPS) * qg_ref[sl, :]).astype(BF16)

    ykv = _dot(h, wkv_ref[...])
    lane = lax.broadcasted_iota(jnp.int32, (tm, LANES), 1)
    low = lane < NSA_DH
    for s in range(2):
        sl = slice(s * LANES, (s + 1) * LANES)
        ys = ykv[:, (2 + s) * LANES:(3 + s) * LANES]
        y2 = ys * ys
        s0 = jnp.sum(jnp.where(low, y2, 0.0), axis=-1, keepdims=True)
        s1 = jnp.sum(jnp.where(low, 0.0, y2), axis=-1, keepdims=True)
        ms = jnp.where(low, s0, s1) * (1.0 / NSA_DH)
        okn_ref[:, sl] = (ys * lax.rsqrt(ms + EPS) * kg_ref[:, sl]).astype(BF16)

    chunks = tm // CMP_STRIDE
    for br in range(2):
        cbuf[br] = ykv[:, br * LANES:(br + 1) * LANES]
        taken = [cbuf[br, pl.ds(l, chunks, stride=CMP_STRIDE), :] for l in range(CMP_STRIDE)]
        for grp in range(NSA_GROUPS):
            flat = jnp.concatenate([r[:, grp * NSA_DH:(grp + 1) * NSA_DH] for r in taken], axis=1)
            ocf_ref[0, br, grp] = flat.astype(BF16)

    ovt_ref[...] = (_dot_nt(wvt_ref[...], h) + vone_ref[...]).astype(BF16)
    ogt_ref[...] = _dot_nt(wgt_ref[...], h)
    oz_ref[...] = _dot(h, wz_ref[...]).astype(BF16)
    oab_ref[...] = _dot(h, wab_ref[...])

    taps = cw_ref.shape[0]
    y = cw_ref[0:1, :] * ybuf[pl.ds(halo - taps + 1, tm), :]
    for k in range(1, taps):
        y = y + cw_ref[k:k + 1, :] * ybuf[pl.ds(halo - taps + 1 + k, tm), :]
    hy = 0.5 * y
    y = hy + hy * jnp.tanh(hy)
    for s in range(3 * GDN_HEADS):
        sl = slice(s * GDN_DH, (s + 1) * GDN_DH)
        ys = y[:, sl]
        if s < 2 * GDN_HEADS:
            scale = GDN_DH ** -0.5 if s < GDN_HEADS else 1.0
            ys = ys * (lax.rsqrt(jnp.sum(ys * ys, axis=-1, keepdims=True) + EPS) * scale)
        oqkv_ref[:, sl] = ys.astype(BF16)


def _in_proj(x2, g, wqt, wkv, wvt, wgt, wqkv, wz, wab, qg, kg, vone, conv_w, tiles_per_seq, tm):
    n, d = x2.shape
    chunks, flat = tm // CMP_STRIDE, CMP_STRIDE * NSA_DH
    full = lambda a: pl.BlockSpec(a.shape, lambda i: (0,) * a.ndim)
    row = lambda w: pl.BlockSpec((tm, w), lambda i: (i, 0))
    colb = lambda r: pl.BlockSpec((r, tm), lambda i: (0, i))
    return pl.pallas_call(
        functools.partial(_inproj_body, tiles_per_seq=tiles_per_seq),
        grid=(n // tm,),
        in_specs=[row(d)] + [full(a) for a in (g, wqt, wkv, wvt, wgt, wqkv, wz, wab, qg, kg, vone, conv_w)],
        out_specs=[colb(wqt.shape[0]), row(2 * LANES),
                   pl.BlockSpec((1, 2, NSA_GROUPS, chunks, flat), lambda i: (i // tiles_per_seq, 0, 0, i % tiles_per_seq, 0)),
                   colb(wvt.shape[0]), colb(wgt.shape[0]), row(wqkv.shape[1]), row(wz.shape[1]), row(wab.shape[1])],
        out_shape=[jax.ShapeDtypeStruct((wqt.shape[0], n), BF16), jax.ShapeDtypeStruct((n, 2 * LANES), BF16),
                   jax.ShapeDtypeStruct((n // (tm * tiles_per_seq), 2, NSA_GROUPS, chunks * tiles_per_seq, flat), BF16),
                   jax.ShapeDtypeStruct((wvt.shape[0], n), BF16), jax.ShapeDtypeStruct((wgt.shape[0], n), F32),
                   jax.ShapeDtypeStruct((n, wqkv.shape[1]), BF16), jax.ShapeDtypeStruct((n, wz.shape[1]), BF16),
                   jax.ShapeDtypeStruct((n, wab.shape[1]), F32)],
        scratch_shapes=[pltpu.VMEM((tm + 8, wqkv.shape[1]), F32), pltpu.VMEM((2, tm, LANES), F32)],
        compiler_params=_cparams(("arbitrary",), 56),
        name="in_proj",
    )(x2, g, wqt, wkv, wvt, wgt, wqkv, wz, wab, qg, kg, vone, conv_w)


def _compress_body(x_ref, pos_ref, w1_ref, b1_ref, w2_ref, b2_ref, w2t_ref, b2t_ref, g_ref, ok_ref, ovt_ref,
                   *, n_cmp):
    is_key = pl.program_id(1) == 0
    nch = x_ref.shape[3]
    hids = []
    for grp in range(NSA_GROUPS):
        x = x_ref[0, 0, grp].astype(F32)
        xa = (x + pos_ref[0, 0]).astype(BF16)
        xb = (x + pos_ref[0, 1]).astype(BF16)
        a = _dot(xa, w1_ref[0, 0])
        b = _dot(xb, w1_ref[0, 1])
        b_next = pltpu.roll(b, nch - 1, 0)
        hids.append(jax.nn.gelu(a + b_next + b1_ref[0]).astype(BF16))

    @pl.when(is_key)
    def _():
        row = lax.broadcasted_iota(jnp.int32, (nch, NSA_DH), 0)
        outs = []
        for grp in range(NSA_GROUPS):
            out = _dot(hids[grp], w2_ref[0]) + b2_ref[0]
            out = out * lax.rsqrt(jnp.mean(out * out, axis=-1, keepdims=True) + EPS) * g_ref[...]
            outs.append(jnp.where(row < n_cmp, out, 0.0))
        ok_ref[0] = jnp.concatenate(outs, axis=-1).astype(BF16)

    @pl.when(jnp.logical_not(is_key))
    def _():
        col = lax.broadcasted_iota(jnp.int32, (NSA_DH, nch), 1)
        outs = []
        for grp in range(NSA_GROUPS):
            out = _dot_nt(w2t_ref[0], hids[grp]) + b2t_ref[0]
            outs.append(jnp.where(col < n_cmp, out, 0.0))
        ovt_ref[0] = jnp.concatenate(outs, axis=0).astype(BF16)


def _compress(xflat, pos, w1, b1, w2, b2, w2t, b2t, kc_g, n_cmp):
    b, _, _, nch, flat = xflat.shape
    return pl.pallas_call(
        functools.partial(_compress_body, n_cmp=n_cmp),
        grid=(b, 2),
        in_specs=[
            pl.BlockSpec((1, 1, NSA_GROUPS, nch, flat), lambda i, j: (i, j, 0, 0, 0)),
            pl.BlockSpec((1, 2, 1, flat), lambda i, j: (j, 0, 0, 0)),
            pl.BlockSpec((1, 2, flat, CMP_HIDDEN), lambda i, j: (j, 0, 0, 0)),
            pl.BlockSpec((1, 1, CMP_HIDDEN), lambda i, j: (j, 0, 0)),
            pl.BlockSpec((1, CMP_HIDDEN, NSA_DH), lambda i, j: (j, 0, 0)),
            pl.BlockSpec((1, 1, NSA_DH), lambda i, j: (j, 0, 0)),
            pl.BlockSpec((1, NSA_DH, CMP_HIDDEN), lambda i, j: (j, 0, 0)),
            pl.BlockSpec((1, NSA_DH, 1), lambda i, j: (j, 0, 0)),
            pl.BlockSpec((1, NSA_DH), lambda i, j: (0, 0)),
        ],
        out_specs=[pl.BlockSpec((1, nch, LANES), lambda i, j: (i, 0, 0)),
                   pl.BlockSpec((1, LANES, nch), lambda i, j: (i, 0, 0))],
        out_shape=[jax.ShapeDtypeStruct((b, nch, LANES), BF16), jax.ShapeDtypeStruct((b, LANES, nch), BF16)],
        compiler_params=_cparams(("parallel", "arbitrary"), 32),
        name="nsa_compress",
    )(xflat, pos, w1, b1, w2, b2, w2t, b2t, kc_g)


def _tile_heads(a):
    return jnp.concatenate([a] * NSA_HPG, axis=1)


def _nsa_body(qt_ref, qn_ref, gt_ref, kc_ref, vct_ref, ks_ref, kw_ref, vst_ref, vwt_ref, et_ref, ov_ref, cpat_ref,
              wpat_ref, dpat_ref, o_ref, acc_sc, s_sc, oc_sc, ch_sc, *, n_top, kt, nq):
    grp = pl.program_id(1)
    step = pl.program_id(2)
    s0 = step * NSA_Q
    nch = kc_ref.shape[1]
    nblk = ov_ref.shape[0]
    blk = lax.broadcasted_iota(jnp.int32, (nblk, NSA_Q), 0)
    rounds_left = n_top - 3
    quarter = rounds_left // 4

    def padded_q(ref):
        qh = jnp.concatenate([ref[h * NSA_DH:(h + 1) * NSA_DH, :] for h in range(NSA_HPG)], axis=1)
        zq = jnp.zeros_like(qh)
        return jnp.where(grp == 0, jnp.concatenate([qh, zq], axis=0), jnp.concatenate([zq, qh], axis=0))

    def pick_rounds(v, rounds):
        for _ in range(rounds):
            mx = jnp.max(v, axis=0, keepdims=True)
            first = jnp.min(jnp.where(v == mx, blk, nblk), axis=0, keepdims=True)
            v = jnp.where(blk == first, -jnp.inf, v)
        return v

    def selection_pieces(q, start):
        t_r = start + lax.broadcasted_iota(jnp.int32, (1, NSA_Q), 1)
        cur_r = t_r // SLC_BLOCK
        st = {}

        def compressed_scores():
            cbias = cpat_ref[pl.ds(pl.multiple_of(nch - start // CMP_STRIDE, CMP_STRIDE), nch), :]
            st["sc"] = _dot(kc_ref[0], q) + _tile_heads(cbias)

        def compressed_softmax():
            sc = st["sc"]
            pc = jnp.exp2(sc - jnp.max(sc, axis=0, keepdims=True)).astype(BF16)
            stacked = jnp.concatenate([vct_ref[0], ov_ref[...], jnp.ones((8, nch), BF16)], axis=0)
            res = _dot(stacked, pc)
            inv = jnp.where(_tile_heads(t_r >= CMP_BLOCK - 1),
                            1.0 / jnp.maximum(res[LANES + nblk:LANES + nblk + 1], 1e-30), 0.0)
            oc_sc[1] = res[:LANES] * inv
            imp4 = res[LANES:LANES + nblk] * inv
            imp = (imp4[:, 0:NSA_Q] + imp4[:, NSA_Q:2 * NSA_Q] + imp4[:, 2 * NSA_Q:3 * NSA_Q]
                   + imp4[:, 3 * NSA_Q:4 * NSA_Q])
            imp = jnp.where(blk * SLC_BLOCK > t_r, NEG, imp)
            st["v"] = jnp.where((blk == 0) | (blk == cur_r) | (blk == cur_r - 1), -jnp.inf, imp)

        def rounds(count):
            def run():
                st["v"] = pick_rounds(st["v"], count)
            return run

        def finish():
            ch_sc[1] = jnp.where(st["v"] == -jnp.inf, 1.0, 0.0)

        return [compressed_scores, compressed_softmax, rounds(quarter), rounds(quarter), rounds(quarter),
                rounds(rounds_left - 3 * quarter), finish]

    @pl.when(step == 0)
    def _():
        for piece in selection_pieces(padded_q(qt_ref), s0):
            piece()

    oc_sc[0] = oc_sc[1]
    ch_sc[0] = ch_sc[1]
    nxt = selection_pieces(padded_q(qn_ref), jnp.minimum(step + 1, nq - 1) * NSA_Q)

    qt = padded_q(qt_ref)
    t_row = s0 + lax.broadcasted_iota(jnp.int32, (1, NSA_Q), 1)
    cur = t_row // SLC_BLOCK
    chosen = ch_sc[0] > 0.5

    selb = jnp.where(chosen & (blk * SLC_BLOCK < s0), 0.0, MASKED).astype(BF16)
    rhs = jnp.concatenate([qt, _tile_heads(selb)], axis=0)
    last_tile = ks_ref.shape[1] // kt - 1

    def scores(idx, slot):
        k0 = pl.multiple_of(jnp.minimum(idx, last_tile) * kt, kt)
        lhs = jnp.concatenate([ks_ref[0, pl.ds(k0, kt), :], et_ref[pl.ds(k0, kt), :]], axis=1)
        s_sc[slot] = _dot(lhs, rhs)

    nxt[0]()
    wlen = WINDOW + NSA_Q
    w0 = pl.multiple_of(jnp.maximum(s0 - WINDOW, 0), NSA_Q)
    wbias = wpat_ref[pl.ds(pl.multiple_of(w0 - s0 + WINDOW, NSA_Q), wlen), :]
    sw = _dot(kw_ref[0, pl.ds(w0, wlen), :], qt) + _tile_heads(wbias)

    d0 = pl.multiple_of(s0, NSA_Q)
    selb_d = jnp.where(chosen & (blk <= cur), 0.0, MASKED).astype(BF16)
    sd = (_dot(jnp.concatenate([ks_ref[0, pl.ds(d0, NSA_Q), :], et_ref[pl.ds(d0, NSA_Q), :]], axis=1),
               jnp.concatenate([qt, _tile_heads(selb_d)], axis=0))
          + _tile_heads(dpat_ref[...]))
    m_diag = jnp.max(sd, axis=0, keepdims=True)
    acc_sc[0] = _dot(vst_ref[:, pl.ds(d0, NSA_Q)], jnp.exp2(sd - m_diag).astype(BF16))
    acc_sc[1] = jnp.zeros(acc_sc.shape[1:], F32)

    nxt[1]()
    nxt[2]()
    scores(0, 0)
    pw = jnp.exp2(sw - jnp.max(sw, axis=0, keepdims=True)).astype(BF16)
    nxt[3]()
    scores(1, 1)
    ow = _dot(vwt_ref[:, pl.ds(w0, wlen)], pw)
    ow = ow[:NSA_DH] / ow[NSA_DH:NSA_DH + 1]
    for piece in nxt[4:]:
        piece()

    def update(idx, slot, m_old, acc_ref):
        k0 = pl.multiple_of(idx * kt, kt)
        m_new = jnp.maximum(m_old, jnp.max(s_sc[slot], axis=0, keepdims=True))
        p = jnp.exp2(s_sc[slot] - m_new).astype(BF16)
        acc_ref[...] = jnp.exp2(m_old - m_new) * acc_ref[...] + _dot(vst_ref[:, pl.ds(k0, kt)], p)
        return m_new

    def tile_group(first, carry, count):
        ms = list(carry)
        for t in range(count):
            scores(first + t + 2, (t + 2) % 4)
            ms[t % 2] = update(first + t, t % 4, ms[t % 2], acc_sc.at[t % 2])
        return tuple(ms)

    n_tiles = (s0 + kt - 1) // kt
    carry = (m_diag, jnp.full((1, NSA_HPG * NSA_Q), NEG, F32))
    done = 0
    for size in FLASH_BODY_TILES:
        left = n_tiles - done
        groups = (left + size - 1) // size if size == FLASH_BODY_TILES[-1] else left // size
        carry = lax.fori_loop(0, groups, lambda j, c, done=done, size=size: tile_group(done + size * j, c, size),
                              carry)
        done = done + size * groups
    m0, m1 = carry
    m_fin = jnp.maximum(m0, m1)
    acc = acc_sc[0] * jnp.exp2(m0 - m_fin) + acc_sc[1] * jnp.exp2(m1 - m_fin)
    osl = acc[:NSA_DH] / acc[NSA_DH:NSA_DH + 1]

    oc = jnp.where(grp == 0, oc_sc[0, :NSA_DH], oc_sc[0, NSA_DH:])
    gts = jax.nn.sigmoid(gt_ref[...])
    for h in range(NSA_HPG):
        cols = slice(h * NSA_Q, (h + 1) * NSA_Q)
        o_ref[h * NSA_DH:(h + 1) * NSA_DH, :] = (
            gts[3 * h:3 * h + 1, :] * oc[:, cols] + gts[3 * h + 1:3 * h + 2, :] * osl[:, cols]
            + gts[3 * h + 2:3 * h + 3, :] * ow[:, cols])


def _mask_pattern(valid):
    return jnp.asarray(np.where(valid, 0.0, NEG), F32)


def _nsa_attention(qt, gt, kc, vct, okv, vt, expand_t, overlap, b, t, n_top, kt):
    nch = kc.shape[1]
    nq = t // NSA_Q
    n = b * t
    ql = np.arange(NSA_Q)[None, :]
    rc = np.arange(2 * nch)[:, None] - nch
    cpat = _mask_pattern(rc * CMP_STRIDE + CMP_BLOCK - 1 <= ql)
    rw = np.arange(2 * WINDOW + NSA_Q)[:, None] - WINDOW
    wpat = _mask_pattern((rw <= ql) & (rw > ql - WINDOW))
    dpat = _mask_pattern(np.arange(NSA_Q)[:, None] <= ql)
    full2 = lambda a: pl.BlockSpec(a.shape, lambda bi, g, i: (0, 0))
    return pl.pallas_call(
        functools.partial(_nsa_body, n_top=n_top, kt=kt, nq=nq),
        grid=(b, NSA_GROUPS, nq),
        in_specs=[
            pl.BlockSpec((NSA_HPG * NSA_DH, NSA_Q), lambda bi, g, i: (g, bi * nq + i)),
            pl.BlockSpec((NSA_HPG * NSA_DH, NSA_Q), lambda bi, g, i: (g, bi * nq + jnp.minimum(i + 1, nq - 1))),
            pl.BlockSpec((GATE_ROWS, NSA_Q), lambda bi, g, i: (g, bi * nq + i)),
            pl.BlockSpec((1, nch, LANES), lambda bi, g, i: (bi, 0, 0)),
            pl.BlockSpec((1, LANES, nch), lambda bi, g, i: (bi, 0, 0)),
            pl.BlockSpec((1, t, LANES), lambda bi, g, i: (bi, 0, 0)),
            pl.BlockSpec((1, t, LANES), lambda bi, g, i: (bi, 0, 1)),
            pl.BlockSpec((LANES, t), lambda bi, g, i: (g, bi)),
            pl.BlockSpec((LANES, t), lambda bi, g, i: (NSA_GROUPS + g, bi)),
            full2(expand_t), full2(overlap), full2(cpat), full2(wpat), full2(dpat),
        ],
        out_specs=pl.BlockSpec((NSA_HPG * NSA_DH, NSA_Q), lambda bi, g, i: (g, bi * nq + i)),
        out_shape=jax.ShapeDtypeStruct((NSA_W, n), F32),
        scratch_shapes=[pltpu.VMEM((2, LANES, NSA_HPG * NSA_Q), F32),
                        pltpu.VMEM((4, kt, NSA_HPG * NSA_Q), F32),
                        pltpu.VMEM((2, LANES, NSA_HPG * NSA_Q), F32),
                        pltpu.VMEM((2, overlap.shape[0], NSA_Q), F32)],
        compiler_params=_cparams(("parallel", "parallel", "arbitrary"), 56),
        name="nsa_attention",
    )(qt, qt, gt, kc, vct, okv, okv, vt, vt, expand_t, overlap, cpat, wpat, dpat)


def _split_bf16(a):
    hi = a.astype(BF16)
    return hi, (a - hi.astype(F32)).astype(BF16)


def _gdn_body(x_ref, z_ref, ab_ref, alog_ref, dtb_ref, og_ref, o_ref, s_sc, u_sc, wq_sc, kq_sc, gl_sc, *, ct):
    nb = x_ref.shape[0]
    ch = GDN_CHUNK
    n_units = (ct // ch) * nb * GDN_HEADS

    @pl.when(pl.program_id(0) == 0)
    def _():
        s_sc[...] = jnp.zeros(s_sc.shape, F32)
        u_sc[...] = jnp.zeros(u_sc.shape, F32)
        wq_sc[...] = jnp.zeros(wq_sc.shape, BF16)
        kq_sc[...] = jnp.zeros(kq_sc.shape, BF16)
        gl_sc[...] = jnp.ones(gl_sc.shape, F32)

    def scan_unit(i):
        ci, bi, h = i // (nb * GDN_HEADS), (i // GDN_HEADS) % nb, i % GDN_HEADS
        rows = slice(ci * ch, (ci + 1) * ch)
        hs = slice(h * GDN_DH, (h + 1) * GDN_DH)
        s_old = s_sc[bi * GDN_HEADS + h]
        from_state = _dot(wq_sc[i], s_old.astype(BF16))
        v_new = (u_sc[i] - from_state[:ch]).astype(BF16)
        from_v = _dot(kq_sc[i], v_new)
        s_sc[bi * GDN_HEADS + h] = s_old * gl_sc[i, 0:1, 0:1] + from_v[:GDN_DH]
        o = from_state[ch:] + from_v[GDN_DH:]
        on = o * lax.rsqrt(jnp.mean(o * o, axis=-1, keepdims=True) + EPS) * og_ref[...]
        zh = z_ref[bi, rows, hs].astype(F32)
        o_ref[bi, rows, hs] = (on * (zh * jax.nn.sigmoid(zh))).astype(BF16)

    r = lax.broadcasted_iota(jnp.int32, (ch, ch), 0)
    col = lax.broadcasted_iota(jnp.int32, (ch, ch), 1)
    incl = r >= col
    strict = r > col
    tril16 = jnp.concatenate([jnp.where(incl, 1.0, 0.0).astype(BF16)] * 3, axis=1)

    units, xs, ps = [], [], []

    def prepare(ci, bi):
        def run():
            rows = slice(ci * ch, (ci + 1) * ch)
            ab = ab_ref[bi, rows, :]
            g_all = -jnp.exp(alog_ref[...]) * jax.nn.softplus(ab + dtb_ref[...])
            beta_all = jax.nn.sigmoid(ab)
            g_hi, g_lo = _split_bf16(g_all)
            g_lo2 = (g_all - g_hi.astype(F32) - g_lo.astype(F32)).astype(BF16)
            gc_all = _dot(tril16, jnp.concatenate([g_hi, g_lo, g_lo2], axis=0))
            gc_t = gc_all.T
            for h in range(GDN_HEADS):
                hs = slice(h * GDN_DH, (h + 1) * GDN_DH)
                q16 = x_ref[bi, rows, hs]
                k16 = x_ref[bi, rows, GDN_W + h * GDN_DH:GDN_W + (h + 1) * GDN_DH]
                qh, kh = q16.astype(F32), k16.astype(F32)
                vh = x_ref[bi, rows, 2 * GDN_W + h * GDN_DH:2 * GDN_W + (h + 1) * GDN_DH].astype(F32)
                gc = gc_all[:, h:h + 1]
                gr = gc_t[h:h + 1, :]
                g_last = gc_all[ch - 1:ch, h:h + 1]
                beta = beta_all[:, GDN_HEADS + h:GDN_HEADS + h + 1]
                eg = jnp.exp(gc)
                decay = jnp.where(incl, jnp.exp(jnp.minimum(gc - gr, 0.0)), 0.0)
                kb = kh * beta
                with_k = (_dot_nt(jnp.concatenate([kb.astype(BF16), q16], axis=0), k16)
                          * jnp.concatenate([decay, decay], axis=0))
                units.append(dict(
                    rows=rows, bi=bi, h=h,
                    lmat=jnp.where(strict, with_k[:ch], 0.0),
                    vb_kbg=jnp.concatenate([(vh * beta).astype(BF16), (kb * eg).astype(BF16)], axis=1),
                    qk=jnp.where(incl, with_k[ch:], 0.0).astype(BF16),
                    qg=(qh * eg).astype(BF16), kd_t=(kh * jnp.exp(g_last - gc)).T.astype(BF16),
                    gl=jnp.exp(g_last)))
        return run

    def inverse_start():
        eye = jnp.where(r == col, 1.0, 0.0)
        for u in units:
            l16 = u["lmat"].astype(BF16)
            xs.append(eye - u["lmat"])
            ps.append(_dot(l16, l16))

    inverse_stages = int(np.log2(ch)) - 1

    def inverse_stage(s):
        for i in range(n_units):
            rhs = ps[i].astype(BF16)
            if s + 1 == inverse_stages:
                xs[i] = xs[i] + _dot(xs[i].astype(BF16), rhs)
            else:
                both = _dot(jnp.concatenate([xs[i], ps[i]], axis=0).astype(BF16), rhs)
                xs[i] = xs[i] + both[:ch]
                ps[i] = both[ch:]

    pieces = [prepare(ci, bi) for ci in range(ct // ch) for bi in range(nb)] + [inverse_start]
    pieces += [functools.partial(inverse_stage, s) for s in range(inverse_stages)]

    def store_unit(i):
        u = units[i]
        u_w = _dot(xs[i].astype(BF16), u["vb_kbg"])
        u_sc[i] = u_w[:, :GDN_DH]
        wq_sc[i] = jnp.concatenate([u_w[:, GDN_DH:].astype(BF16), u["qg"]], axis=0)
        kq_sc[i] = jnp.concatenate([u["kd_t"], u["qk"]], axis=0)
        gl_sc[i] = jnp.broadcast_to(u["gl"], gl_sc.shape[1:])

    for k in range(max(n_units, len(pieces))):
        if k < n_units:
            scan_unit(k)
        if k < len(pieces):
            pieces[k]()
    for i in range(n_units):
        store_unit(i)


def _gdn(oqkv, oz, oab, alog, dtb, og, ct=128):
    b, t, w3 = oqkv.shape
    steps = t // ct
    n_units = (ct // GDN_CHUNK) * b * GDN_HEADS
    full = lambda a: pl.BlockSpec(a.shape, lambda s: (0,) * a.ndim)
    prep = lambda w: pl.BlockSpec((b, ct, w), lambda s: (0, jnp.minimum(s, steps - 1), 0))
    scan = pl.BlockSpec((b, ct, GDN_W), lambda s: (0, jnp.maximum(s - 1, 0), 0))
    return pl.pallas_call(
        functools.partial(_gdn_body, ct=ct),
        grid=(steps + 1,),
        in_specs=[prep(w3), scan, prep(LANES), full(alog), full(dtb), full(og)],
        out_specs=scan,
        out_shape=jax.ShapeDtypeStruct((b, t, GDN_W), BF16),
        scratch_shapes=[pltpu.VMEM((b * GDN_HEADS, GDN_DH, GDN_DH), F32),
                        pltpu.VMEM((n_units, GDN_CHUNK, GDN_DH), F32),
                        pltpu.VMEM((n_units, 2 * GDN_CHUNK, GDN_DH), BF16),
                        pltpu.VMEM((n_units, GDN_DH + GDN_CHUNK, GDN_CHUNK), BF16),
                        pltpu.VMEM((n_units, 8, LANES), F32)],
        compiler_params=_cparams(("arbitrary",), 32),
        name="gdn",
    )(oqkv, oz, oab, alog, dtb, og)


def _outproj_body(ont_ref, og_ref, x_ref, ng_ref, wo_ref, fg_ref, wr_ref, br_ref, upper_ref,
                  x1_ref, h2_ref, gate_ref, route_ref, cnt_ref, cnt_sc, hprev):
    i = pl.program_id(0)
    tm = x_ref.shape[0]

    @pl.when(i == 0)
    def _():
        cnt_sc[...] = jnp.zeros(cnt_sc.shape, F32)
        hprev[...] = jnp.zeros(hprev.shape, BF16)

    logits = (_dot(hprev[...], wr_ref[...]) + br_ref[...]).T[:N_EXPERTS]

    a = ont_ref[...]
    a = (a * lax.rsqrt(jnp.mean(a * a, axis=0, keepdims=True) + EPS) * ng_ref[...]).astype(BF16)
    x1 = x_ref[...] + _dot_tn(a, wo_ref[0:NSA_W, :]) + _dot(og_ref[...], wo_ref[NSA_W:, :])
    x1_ref[...] = x1
    h2f = x1 * lax.rsqrt(jnp.mean(x1 * x1, axis=-1, keepdims=True) + EPS) * fg_ref[...]
    _store_pieces(h2_ref, h2f)

    routed = jnp.where(i > 0, 1.0, 0.0)
    erow = lax.broadcasted_iota(jnp.int32, (N_EXPERTS, tm), 0)
    onehot = jnp.zeros((N_EXPERTS, tm), F32)
    firsts, vals = [], []
    v = logits
    for k in range(TOP_K):
        mx = jnp.max(v, axis=0, keepdims=True)
        first = jnp.min(jnp.where(v == mx, erow, N_EXPERTS), axis=0, keepdims=True)
        hit = erow == first
        v = jnp.where(hit, -jnp.inf, v)
        onehot = jnp.where(hit, 1.0, onehot)
        firsts.append(first)
        vals.append(mx)
    vals = [jnp.exp(m - vals[0]) for m in vals]
    inv = 1.0 / (vals[0] + vals[1] + vals[2] + vals[3])
    gates_t = jnp.concatenate([m * inv for m in vals] + [jnp.zeros((LANES - TOP_K, tm), F32)], axis=0)
    gate_ref[...] = gates_t.T

    excl = cnt_sc[...] + _dot(onehot.astype(BF16), upper_ref[...])
    for k in range(TOP_K):
        route_ref[k:k + 1, :] = firsts[k]
        rank = jnp.sum(jnp.where(erow == firsts[k], excl, 0.0), axis=0, keepdims=True)
        route_ref[TOP_K + k:TOP_K + k + 1, :] = rank.astype(jnp.int32)
    cnt_sc[...] = cnt_sc[...] + routed * jnp.sum(onehot, axis=1, keepdims=True)
    cnt_ref[...] = cnt_sc[...].astype(jnp.int32)
    hprev[...] = h2f.astype(BF16)


def _out_proj(o_nsa_t, o_gdn, x2, ng, wo, fg, wr, br, tm=512):
    upper = jnp.asarray(np.arange(tm)[:, None] < np.arange(tm)[None, :], BF16)
    n, d = x2.shape
    full = lambda a: pl.BlockSpec(a.shape, lambda i: (0,) * a.ndim)
    last = n // tm - 1
    proj = lambda i: jnp.minimum(i, last)
    rout = lambda i: jnp.maximum(i - 1, 0)
    row = lambda w: pl.BlockSpec((tm, w), lambda i: (proj(i), 0))
    return pl.pallas_call(
        _outproj_body,
        grid=(n // tm + 1,),
        in_specs=[pl.BlockSpec((NSA_W, tm), lambda i: (0, proj(i))), row(GDN_W), row(d), full(ng), full(wo),
                  full(fg), full(wr), full(br), full(upper)],
        out_specs=[row(d), pl.BlockSpec((d // 2 // SC_SUBROW, tm, SC_SUBROW), lambda i: (0, proj(i), 0)),
                   pl.BlockSpec((tm, LANES), lambda i: (rout(i), 0)),
                   pl.BlockSpec((2 * TOP_K, tm), lambda i: (0, rout(i))),
                   pl.BlockSpec((N_EXPERTS, 1), lambda i: (0, 0))],
        out_shape=[jax.ShapeDtypeStruct((n, d), F32),
                   jax.ShapeDtypeStruct((d // 2 // SC_SUBROW, n, SC_SUBROW), jnp.int32),
                   jax.ShapeDtypeStruct((n, LANES), F32), jax.ShapeDtypeStruct((2 * TOP_K, n), jnp.int32),
                   jax.ShapeDtypeStruct((N_EXPERTS, 1), jnp.int32)],
        scratch_shapes=[pltpu.VMEM((N_EXPERTS, 1), F32), pltpu.VMEM((tm, d), BF16)],
        compiler_params=_cparams(("arbitrary",), 48),
        name="out_proj_router",
    )(o_nsa_t, o_gdn, x2, ng, wo, fg, wr, br, upper)


def _dest_body(ps_ref, route_ref, o_ref, *, n_rows, pieces):
    expert = route_ref[0:TOP_K, :]
    start = jnp.zeros(expert.shape, jnp.int32)
    for e in range(N_EXPERTS):
        start = jnp.where(expert == e, ps_ref[e], start)
    dest = start + route_ref[TOP_K:2 * TOP_K, :]
    for k in range(TOP_K):
        for j in range(pieces):
            o_ref[k * pieces + j:k * pieces + j + 1, :] = dest[k:k + 1, :] + j * n_rows


def _dest_rows(pstarts, route, n_rows, pieces):
    n = route.shape[1]
    tn = min(2048, n)
    grid_spec = pltpu.PrefetchScalarGridSpec(
        num_scalar_prefetch=1,
        grid=(n // tn,),
        in_specs=[pl.BlockSpec((2 * TOP_K, tn), lambda i, ps: (0, i))],
        out_specs=pl.BlockSpec((TOP_K * pieces, tn), lambda i, ps: (0, i)),
    )
    return pl.pallas_call(
        functools.partial(_dest_body, n_rows=n_rows, pieces=pieces),
        grid_spec=grid_spec,
        out_shape=jax.ShapeDtypeStruct((TOP_K * pieces, n), jnp.int32),
        name="moe_dest_rows",
    )(pstarts, route)


def _expert_body(be_ref, end_ref, xs_ref, wg_hbm, bg_ref, wu_hbm, bu_ref, wd_hbm, bd_ref,
                 y_ref, wf32, w16, sems, slot_sc):
    i = pl.program_id(0)
    n_used = be_ref[pl.num_programs(0)]
    used = i < n_used
    expert = be_ref[i]
    fresh = used & ((i == 0) | (expert != be_ref[jnp.maximum(i - 1, 0)]))
    hbm = (wg_hbm, wu_hbm, wd_hbm)

    def weight_copy(e, slot, j):
        return pltpu.make_async_copy(hbm[j].at[e], wf32.at[slot, j], sems.at[slot, j])

    @pl.when(i == 0)
    def _():
        slot_sc[0] = 0

    @pl.when((i == 0) & used)
    def _():
        for j in range(3):
            weight_copy(expert, 0, j).start()

    @pl.when(fresh)
    def _():
        slot = slot_sc[0]
        for j in range(3):
            weight_copy(expert, slot, j).wait()
            w16[j] = wf32[slot, j].astype(BF16)
        following = end_ref[expert]

        @pl.when(following < n_used)
        def _():
            for j in range(3):
                weight_copy(be_ref[following], 1 - slot, j).start()

        slot_sc[0] = 1 - slot

    @pl.when(jnp.logical_not(used))
    def _():
        y_ref[...] = jnp.zeros(y_ref.shape, y_ref.dtype)

    @pl.when(used)
    def _():
        x = _join_pieces(xs_ref).astype(BF16)
        gate = jnp.minimum(_dot(x, w16[0]) + bg_ref[0], SWIGLU_LIMIT)
        up = jnp.clip(_dot(x, w16[1]) + bu_ref[0], -SWIGLU_LIMIT, SWIGLU_LIMIT)
        glu = gate * jax.nn.sigmoid(gate * SWIGLU_ALPHA)
        act = ((up + 1.0) * glu).astype(BF16)
        cols = 2 * y_ref.shape[2]
        for j in range(y_ref.shape[0]):
            sl = slice(j * cols, (j + 1) * cols)
            y_ref[j] = _pack_piece(_dot(act, w16[2, :, sl]) + bd_ref[0, :, sl])


def _experts(blk_e, end_blk, xs, wg, bg, wu, bu, wd, bd):
    pieces, n_rows, sub = xs.shape
    d, de = wg.shape[1], wg.shape[2]
    assert d == de
    r = MOE_ROW_BLOCK
    bspec = lambda w: pl.BlockSpec((1, 1, w), lambda i, be, *_: (be[i], 0, 0))
    hbm = pl.BlockSpec(memory_space=pl.ANY)
    grid_spec = pltpu.PrefetchScalarGridSpec(
        num_scalar_prefetch=2,
        grid=(n_rows // r,),
        in_specs=[pl.BlockSpec((pieces, r, sub), lambda i, *_: (0, i, 0)),
                  hbm, bspec(de), hbm, bspec(de), hbm, bspec(d)],
        out_specs=pl.BlockSpec((pieces, r, sub), lambda i, *_: (0, i, 0)),
        scratch_shapes=[pltpu.VMEM((2, 3, d, de), F32), pltpu.VMEM((3, d, de), BF16),
                        pltpu.SemaphoreType.DMA((2, 3)), pltpu.SMEM((1,), jnp.int32)],
    )
    return pl.pallas_call(
        _expert_body,
        grid_spec=grid_spec,
        out_shape=jax.ShapeDtypeStruct((pieces, n_rows, sub), jnp.int32),
        compiler_params=_cparams(("arbitrary",), 56),
        name="moe_experts",
    )(blk_e, end_blk, xs, wg, bg, wu, bu, wd, bd)


SC_WINDOW = 128
SC_SUBROW = 256


def _sc_mesh():
    return plsc.VectorSubcoreMesh(core_axis_name="c", subcore_axis_name="s")


def _sc_dispatch(h2, dest_rows, n_rows):
    n, d = h2.shape

    @functools.partial(pl.kernel, out_type=jax.ShapeDtypeStruct((n_rows, d), h2.dtype), mesh=_sc_mesh())
    def dispatch(x_hbm, *refs):
        idx_hbm, o_hbm = refs[:TOP_K], refs[TOP_K]

        def body(x_vmem, *idx_vmem):
            for iv in idx_vmem:
                pltpu.sync_copy(x_vmem, o_hbm.at[iv.at[0]])

        pltpu.emit_pipeline(
            body,
            grid=(n // SC_WINDOW,),
            in_specs=[pl.BlockSpec((SC_WINDOW, d), lambda i: (i, 0))]
                     + [pl.BlockSpec((1, SC_WINDOW), lambda i: (0, i))] * TOP_K,
            out_specs=[],
            core_axis_name=("c", "s"),
            dimension_semantics=(pltpu.PARALLEL,),
        )(x_hbm, *idx_hbm)

    return dispatch(h2, *dest_rows)


def _sc_gather(table, idx):
    _, d = table.shape
    m = idx.shape[1]

    @functools.partial(pl.kernel, out_type=jax.ShapeDtypeStruct((m, d), table.dtype), mesh=_sc_mesh())
    def gather(t_hbm, i_hbm, o_hbm):
        def body(i_vmem, o_vmem):
            pltpu.sync_copy(t_hbm.at[i_vmem.at[0]], o_vmem)

        pltpu.emit_pipeline(
            body,
            grid=(m // SC_WINDOW,),
            in_specs=[pl.BlockSpec((1, SC_WINDOW), lambda i: (0, i))],
            out_specs=[pl.BlockSpec((SC_WINDOW, d), lambda i: (i, 0))],
            core_axis_name=("c", "s"),
            dimension_semantics=(pltpu.PARALLEL,),
        )(i_hbm, o_hbm)

    return gather(table, idx)


def _combine_body(x1_ref, y_ref, gate_ref, o_ref):
    acc = x1_ref[...]
    for k in range(TOP_K):
        acc = acc + gate_ref[:, k:k + 1] * _join_pieces(y_ref.at[k])
    o_ref[...] = acc


def _combine(x1, y4, gates, tm=512):
    n, d = x1.shape
    pieces, sub = y4.shape[1], y4.shape[3]
    row = lambda w: pl.BlockSpec((tm, w), lambda i: (i, 0))
    return pl.pallas_call(
        _combine_body,
        grid=(n // tm,),
        in_specs=[row(d), pl.BlockSpec((TOP_K, pieces, tm, sub), lambda i: (0, 0, i, 0)), row(LANES)],
        out_specs=row(d),
        out_shape=jax.ShapeDtypeStruct((n, d), F32),
        compiler_params=_cparams(("parallel",), 48),
        name="moe_combine",
    )(x1, y4, gates)


def _pad_lanes(a, width=LANES):
    return jnp.pad(a, ((0, 0), (0, width - a.shape[1])))


def _layer(x, attn_norm_g, w_in, q_g, kc_g, ks_g, kw_g, ck_pos, ck_w1, ck_b1, ck_w2, ck_b2,
           cv_pos, cv_w1, cv_b1, cv_w2, cv_b2, nsa_out_g, conv_w, a_log, dt_bias, gdn_out_g, w_out,
           ffn_g, router_w, router_b, e_wg, e_bg, e_wu, e_bu, e_wd, e_bd):
    b, t, d = x.shape
    n = b * t
    x2 = x.reshape(n, d)

    o = np.cumsum([0, NSA_W] + [NSA_GROUPS * NSA_DH] * 6 + [3 * NSA_HEADS, 3 * GDN_W, GDN_W, GDN_HEADS, GDN_HEADS])
    wq_t = w_in[:, o[0]:o[1]].T.astype(BF16)
    qg_col = jnp.tile(q_g * (NSA_DH ** -0.5 * np.log2(np.e)), NSA_HEADS).reshape(NSA_W, 1)
    wkv = jnp.concatenate([w_in[:, o[1]:o[4]], w_in[:, o[5]:o[6]]], axis=1).astype(BF16)
    kg = jnp.concatenate([ks_g, ks_g, kw_g, kw_g]).reshape(1, 2 * LANES)
    wv_t = jnp.concatenate([w_in[:, o[4]:o[5]], w_in[:, o[6]:o[7]]], axis=1).T.reshape(2 * NSA_GROUPS, NSA_DH, d)
    wv_t = jnp.pad(wv_t, ((0, 0), (0, LANES - NSA_DH), (0, 0))).reshape(2 * NSA_GROUPS * LANES, d).astype(BF16)
    vone = jnp.asarray((np.arange(2 * NSA_GROUPS * LANES) % LANES == NSA_DH).astype(np.float32)[:, None])
    wg_t = w_in[:, o[7]:o[8]].T.reshape(NSA_GROUPS, NSA_HPG * 3, d)
    wg_t = jnp.pad(wg_t, ((0, 0), (0, GATE_ROWS - NSA_HPG * 3), (0, 0))).reshape(NSA_GROUPS * GATE_ROWS, d)
    wg_t = wg_t.astype(BF16)
    wab = _pad_lanes(w_in[:, o[10]:o[12]]).astype(BF16)
    wqkv = w_in[:, o[8]:o[9]].astype(BF16)
    wz = w_in[:, o[9]:o[10]].astype(BF16)

    tm = min(512, t)
    oqt, okn, xflat, ovt, ogt, oqkv, oz, oab = _in_proj(x2, attn_norm_g.reshape(1, d), wq_t, wkv, wv_t, wg_t, wqkv,
                                                        wz, wab, qg_col, kg, vone, conv_w, t // tm, tm)

    nch = t // CMP_STRIDE
    n_cmp = (t - CMP_BLOCK) // CMP_STRIDE + 1
    half = CMP_STRIDE * NSA_DH
    pos = jnp.stack([ck_pos, cv_pos]).reshape(2, 2, 1, half)
    w1 = jnp.stack([ck_w1, cv_w1]).reshape(2, 2, half, CMP_HIDDEN).astype(BF16)
    b1 = jnp.stack([ck_b1, cv_b1]).reshape(2, 1, CMP_HIDDEN)
    w2 = jnp.stack([ck_w2, cv_w2]).astype(BF16)
    b2 = jnp.stack([ck_b2, cv_b2]).reshape(2, 1, NSA_DH)
    w2t = jnp.stack([ck_w2.T, cv_w2.T]).astype(BF16)
    b2t = jnp.stack([ck_b2, cv_b2]).reshape(2, NSA_DH, 1)
    kc, vct = _compress(xflat, pos, w1, b1, w2, b2, w2t, b2t, kc_g.reshape(1, NSA_DH), n_cmp)

    n_slc = t // SLC_BLOCK
    n_top = min(SLC_TOPK, n_slc)
    assert n_top > 3
    nblk = max(n_slc, LANES)
    kt = min(256, t // 4)
    assert (t // kt) % 4 == 0
    ci = np.arange(nch)[None, :] * CMP_STRIDE
    sj = np.arange(nblk)[:, None] * SLC_BLOCK
    overlap = ((ci < sj + SLC_BLOCK) & (ci + CMP_BLOCK > sj) & (np.arange(nch)[None, :] < n_cmp)
               & (np.arange(nblk)[:, None] < n_slc))
    expand_t = (np.arange(t)[:, None] // SLC_BLOCK) == np.arange(nblk)[None, :]
    o_nsa_t = _nsa_attention(oqt, ogt, kc, vct, okn.reshape(b, t, -1), ovt, jnp.asarray(expand_t, BF16),
                             jnp.asarray(overlap, BF16), b, t, n_top, kt)

    alog_row = _pad_lanes(a_log.reshape(1, GDN_HEADS))
    dtb_row = _pad_lanes(dt_bias.reshape(1, GDN_HEADS))
    o_gdn = _gdn(oqkv.reshape(b, t, -1), oz.reshape(b, t, -1), oab.reshape(b, t, -1),
                 alog_row, dtb_row, gdn_out_g.reshape(1, GDN_DH))

    wr = _pad_lanes(router_w).astype(BF16)
    br = _pad_lanes(router_b.reshape(1, N_EXPERTS))
    x1, h2, gates, route, counts = _out_proj(
        o_nsa_t, o_gdn.reshape(n, GDN_W), x2, nsa_out_g.reshape(NSA_W, 1),
        w_out.astype(BF16), ffn_g.reshape(1, d), wr, br)

    r = MOE_ROW_BLOCK
    nk = n * TOP_K
    counts = counts[:, 0]
    pcounts = (counts + r - 1) // r * r
    pends = jnp.cumsum(pcounts)
    pstarts = pends - pcounts
    n_rows = (nk + r - 1) // r * r + N_EXPERTS * r
    n_blocks = n_rows // r
    blk_start = jnp.arange(n_blocks, dtype=jnp.int32)[:, None] * r
    blk_e = jnp.minimum(jnp.sum(pends[None, :] <= blk_start, axis=1), N_EXPERTS - 1).astype(jnp.int32)
    n_used = (pends[-1] // r).astype(jnp.int32)
    blk_e = jnp.concatenate([blk_e, n_used[None]])
    end_blk = (pends // r).astype(jnp.int32)
    pieces = d // 2 // SC_SUBROW
    dest_p = _dest_rows(pstarts.astype(jnp.int32), route, n_rows, pieces).reshape(TOP_K, pieces, n)
    xs = _sc_dispatch(h2.reshape(pieces * n, SC_SUBROW), [dest_p[k].reshape(1, pieces * n) for k in range(TOP_K)],
                      pieces * n_rows)
    ys = _experts(blk_e, end_blk, xs.reshape(pieces, n_rows, SC_SUBROW), e_wg, e_bg.reshape(N_EXPERTS, 1, -1),
                  e_wu, e_bu.reshape(N_EXPERTS, 1, -1), e_wd, e_bd.reshape(N_EXPERTS, 1, -1))
    y4 = _sc_gather(ys.reshape(pieces * n_rows, SC_SUBROW), dest_p.reshape(1, nk * pieces))
    return _combine(x1, y4.reshape(TOP_K, pieces, n, SC_SUBROW), gates).reshape(b, t, d)


def kernel(x, attn_norm_g, w_in, nsa_q_norm_g, nsa_kc_norm_g, nsa_ks_norm_g, nsa_kw_norm_g, cmp_k_pos, cmp_k_w1, cmp_k_b1, cmp_k_w2, cmp_k_b2, cmp_v_pos, cmp_v_w1, cmp_v_b1, cmp_v_w2, cmp_v_b2, nsa_out_norm_g, gdn_conv_w, gdn_a_log, gdn_dt_bias, gdn_out_norm_g, w_out, ffn_norm_g, router_w, router_b, exp_w_gate, exp_b_gate, exp_w_up, exp_b_up, exp_w_down, exp_b_down):
    params = (attn_norm_g, w_in, nsa_q_norm_g, nsa_kc_norm_g, nsa_ks_norm_g, nsa_kw_norm_g,
              cmp_k_pos, cmp_k_w1, cmp_k_b1, cmp_k_w2, cmp_k_b2, cmp_v_pos, cmp_v_w1, cmp_v_b1, cmp_v_w2, cmp_v_b2,
              nsa_out_norm_g, gdn_conv_w, gdn_a_log, gdn_dt_bias, gdn_out_norm_g, w_out, ffn_norm_g,
              router_w, router_b, exp_w_gate, exp_b_gate, exp_w_up, exp_b_up, exp_w_down, exp_b_down)
    for l in range(attn_norm_g.shape[0]):
        x = _layer(x, *(p[l] for p in params))
    return x
```
